```python
import jax, jax.numpy as jnp
from jax import lax
import numpy as np

D_MODEL = 1024
BATCH = 8
SEQ = 4096
DEPTH = 4

MEM_LEN = 256
HEAD_DIM = 64
EPS = 1e-6
GM_WIDTH = D_MODEL // 4
GM_GROUPS = GM_WIDTH // HEAD_DIM
CHUNK = 128
POOL_WIDTH = D_MODEL // 4
POOL_WINDOWS = (2, 4, 8, 16)
POOL_GROUPS = len(POOL_WINDOWS)
POOL_GROUP_DIM = POOL_WIDTH // POOL_GROUPS
ATT_WIDTH = D_MODEL // 2
ATT_Q_HEADS = ATT_WIDTH // HEAD_DIM
ATT_KV_HEADS = ATT_Q_HEADS // 4
ATT_GROUP = ATT_Q_HEADS // ATT_KV_HEADS
WINDOW = 128
ROPE_THETA = 10000.0
D_MIX = GM_WIDTH + POOL_WIDTH + ATT_WIDTH
IN_SIZES = (2 * GM_WIDTH, POOL_WIDTH, ATT_Q_HEADS * HEAD_DIM, ATT_KV_HEADS * HEAD_DIM, ATT_KV_HEADS * HEAD_DIM)
IN_SPLITS = tuple(int(s) for s in np.cumsum(IN_SIZES)[:-1])
D_IN = sum(IN_SIZES)
X_HEADS = 4
X_HEAD_DIM = D_MODEL // X_HEADS
D_FF = -(-8 * D_MODEL // (3 * 256)) * 256

kernel_name = "hybrid_gmlp_pool_swa_sink_trunk"


def rms_norm(x, g):
    xf = x.astype(jnp.float32)
    y = xf * lax.rsqrt(jnp.mean(xf * xf, axis=-1, keepdims=True) + EPS)
    return (y * g.astype(jnp.float32)).astype(x.dtype)


def spatial_gating(uv, v_gain, w_s, b_s):
    B, S, _ = uv.shape
    u, v = jnp.split(uv, 2, axis=-1)
    v = v.reshape(B, S // CHUNK, CHUNK, GM_GROUPS, HEAD_DIM)
    v = rms_norm(v, v_gain.reshape(GM_GROUPS, HEAD_DIM))
    causal = jnp.tril(jnp.ones((CHUNK, CHUNK), dtype=bool))
    w = jnp.where(causal[None], w_s, jnp.zeros_like(w_s))
    mixed = jnp.einsum('gts,bcsgd->bctgd', w, v) + b_s.T[None, None, :, :, None]
    return u * mixed.reshape(B, S, GM_WIDTH)


def multiscale_pool(p, pool_w, pool_scale):
    B, S, _ = p.shape
    pf = p.astype(jnp.float32)
    cs = jnp.pad(jnp.cumsum(pf, axis=1), ((0, 0), (1, 0), (0, 0)))
    t = jnp.arange(S)
    outs = []
    for g, w in enumerate(POOL_WINDOWS):
        sl = slice(g * POOL_GROUP_DIM, (g + 1) * POOL_GROUP_DIM)
        hi = cs[:, 1:, sl]
        lo = jnp.pad(cs[:, :S - w + 1, sl], ((0, 0), (w - 1, 0), (0, 0)))
        count = jnp.minimum(t + 1, w).astype(jnp.float32)[None, :, None]
        outs.append((hi - lo) / count - pf[:, :, sl])
    pooled = jnp.stack(outs, axis=2).astype(p.dtype)
    mapped = jnp.einsum('bsgc,gcd->bsgd', pooled, pool_w).reshape(B, S, POOL_WIDTH)
    return mapped * pool_scale


def rope(x, positions):
    half = HEAD_DIM // 2
    inv = ROPE_THETA ** (-jnp.arange(half, dtype=jnp.float32) / half)
    ang = positions.astype(jnp.float32)[..., None] * inv
    cos = jnp.cos(ang)[:, :, None, :]
    sin = jnp.sin(ang)[:, :, None, :]
    xf = x.astype(jnp.float32)
    x1, x2 = xf[..., :half], xf[..., half:]
    return jnp.concatenate([x1 * cos - x2 * sin, x2 * cos + x1 * sin], axis=-1).astype(x.dtype)


def sliding_window_attention(q, k, v, sinks):
    B, S, _, _ = q.shape
    NB = S // WINDOW
    qb = q.reshape(B, NB, WINDOW, ATT_KV_HEADS, ATT_GROUP, HEAD_DIM)

    def band(t_):
        tb = t_.reshape(B, NB, WINDOW, ATT_KV_HEADS, HEAD_DIM)
        prev = jnp.pad(tb[:, :-1], ((0, 0), (1, 0), (0, 0), (0, 0), (0, 0)))
        return jnp.concatenate([prev, tb], axis=2)

    kb, vb = band(k), band(v)
    scores = jnp.einsum('bnqhgd,bnkhd->bnhgqk', qb, kb,
                        preferred_element_type=jnp.float32) * (HEAD_DIM ** -0.5)
    qi = jnp.arange(WINDOW)[:, None] + WINDOW
    ki = jnp.arange(2 * WINDOW)[None, :]
    rel = qi - ki
    valid = (rel >= 0) & (rel < WINDOW)
    valid = valid[None] & ((jnp.arange(NB) > 0)[:, None, None] | (ki >= WINDOW)[None])
    scores = jnp.where(valid[None, :, None, None], scores, -jnp.inf)
    sink = jnp.broadcast_to(
        sinks.astype(jnp.float32).reshape(ATT_KV_HEADS, ATT_GROUP)[None, None, :, :, None, None],
        scores.shape[:-1] + (1,))
    probs = jax.nn.softmax(jnp.concatenate([scores, sink], axis=-1), axis=-1)[..., :-1]
    out = jnp.einsum('bnhgqk,bnkhd->bnqhgd', probs.astype(v.dtype), vb)
    return out.reshape(B, S, ATT_WIDTH)


def cross_attention(h, mem_n, w_xq, w_xkv, w_xo):
    B, S, _ = h.shape
    M = mem_n.shape[1]
    q = (h @ w_xq).reshape(B, S, X_HEADS, X_HEAD_DIM)
    k, v = jnp.split(mem_n @ w_xkv, 2, axis=-1)
    k = k.reshape(B, M, X_HEADS, X_HEAD_DIM)
    v = v.reshape(B, M, X_HEADS, X_HEAD_DIM)
    s = jnp.einsum('bshd,bmhd->bhsm', q, k, preferred_element_type=jnp.float32) * (X_HEAD_DIM ** -0.5)
    p = jax.nn.softmax(s, axis=-1).astype(v.dtype)
    o = jnp.einsum('bhsm,bmhd->bshd', p, v).reshape(B, S, D_MODEL)
    return o @ w_xo


def _fwd_setup_inputs(seed: int = 0) -> dict:
    key = jax.random.key(seed)
    ks = jax.random.split(key, 32)
    f32 = jnp.float32

    def nrm(k, shape, scale):
        return jax.random.normal(k, shape, f32) * scale

    def gain(k, shape):
        return 1.0 + 0.05 * jax.random.normal(k, shape, f32)

    offsets = jax.random.randint(ks[2], (BATCH, 1), 0, 1024, dtype=jnp.int32)
    positions = offsets + jnp.arange(SEQ, dtype=jnp.int32)[None, :]
    return {
        "x": nrm(ks[0], (BATCH, SEQ, D_MODEL), 1.0),
        "mem": nrm(ks[1], (BATCH, MEM_LEN, D_MODEL), 1.0),
        "positions": positions,
        "mem_norm_g": gain(ks[3], (D_MODEL,)),
        "mix_pre_g": gain(ks[4], (DEPTH, D_MODEL)),
        "mix_post_g": gain(ks[5], (DEPTH, D_MODEL)),
        "w_in": nrm(ks[6], (DEPTH, D_MODEL, D_IN), D_MODEL ** -0.5),
        "gm_v_g": gain(ks[7], (DEPTH, GM_WIDTH)),
        "gm_w_s": nrm(ks[8], (DEPTH, GM_GROUPS, CHUNK, CHUNK), 0.5 * CHUNK ** -0.5),
        "gm_b_s": 1.0 + 0.1 * jax.random.normal(ks[9], (DEPTH, GM_GROUPS, CHUNK), f32),
        "pool_w": nrm(ks[10], (DEPTH, POOL_GROUPS, POOL_GROUP_DIM, POOL_GROUP_DIM), POOL_GROUP_DIM ** -0.5),
        "pool_scale": 1.0 + 0.1 * jax.random.normal(ks[11], (DEPTH, POOL_WIDTH), f32),
        "attn_sinks": nrm(ks[12], (DEPTH, ATT_Q_HEADS), 1.0),
        "w_o": nrm(ks[13], (DEPTH, D_MIX, D_MODEL), D_MIX ** -0.5),
        "x_pre_g": gain(ks[14], (DEPTH, D_MODEL)),
        "x_post_g": gain(ks[15], (DEPTH, D_MODEL)),
        "w_xq": nrm(ks[16], (DEPTH, D_MODEL, D_MODEL), D_MODEL ** -0.5),
        "w_xkv": nrm(ks[17], (DEPTH, D_MODEL, 2 * D_MODEL), D_MODEL ** -0.5),
        "w_xo": nrm(ks[18], (DEPTH, D_MODEL, D_MODEL), D_MODEL ** -0.5),
        "ffn_pre_g": gain(ks[19], (DEPTH, D_MODEL)),
        "ffn_post_g": gain(ks[20], (DEPTH, D_MODEL)),
        "w_gate_up": nrm(ks[21], (DEPTH, D_MODEL, 2 * D_FF), D_MODEL ** -0.5),
        "w_down": nrm(ks[22], (DEPTH, D_FF, D_MODEL), D_FF ** -0.5),
    }


def _fwd_reference(x, mem, positions, mem_norm_g, mix_pre_g, mix_post_g, w_in, gm_v_g, gm_w_s, gm_b_s,
              pool_w, pool_scale, attn_sinks, w_o, x_pre_g, x_post_g, w_xq, w_xkv, w_xo,
              ffn_pre_g, ffn_post_g, w_gate_up, w_down):
    B, S, _ = x.shape
    mem_n = rms_norm(mem, mem_norm_g)
    for l in range(DEPTH):
        h = rms_norm(x, mix_pre_g[l])
        z = h @ w_in[l]
        z_gm, z_pool, z_q, z_k, z_v = jnp.split(z, IN_SPLITS, axis=-1)
        a = spatial_gating(jax.nn.gelu(z_gm), gm_v_g[l], gm_w_s[l], gm_b_s[l])
        b = multiscale_pool(z_pool, pool_w[l], pool_scale[l])
        q = rope(z_q.reshape(B, S, ATT_Q_HEADS, HEAD_DIM), positions)
        k = rope(z_k.reshape(B, S, ATT_KV_HEADS, HEAD_DIM), positions)
        v = z_v.reshape(B, S, ATT_KV_HEADS, HEAD_DIM)
        c = sliding_window_attention(q, k, v, attn_sinks[l])
        mix = jnp.concatenate([a, b, c], axis=-1) @ w_o[l]
        x = x + rms_norm(mix, mix_post_g[l])
        h = rms_norm(x, x_pre_g[l])
        x = x + rms_norm(cross_attention(h, mem_n, w_xq[l], w_xkv[l], w_xo[l]), x_post_g[l])
        h = rms_norm(x, ffn_pre_g[l])
        gate, up = jnp.split(h @ w_gate_up[l], 2, axis=-1)
        f = (jax.nn.silu(gate) * up) @ w_down[l]
        x = x + rms_norm(f, ffn_post_g[l])
    return x


import jax as _jax
import jax.numpy as _jnp

TWIN_FORMAT = 'train_step'
FWD_PARAMS = ['x', 'mem', 'positions', 'mem_norm_g', 'mix_pre_g', 'mix_post_g', 'w_in', 'gm_v_g', 'gm_w_s', 'gm_b_s', 'pool_w', 'pool_scale', 'attn_sinks', 'w_o', 'x_pre_g', 'x_post_g', 'w_xq', 'w_xkv', 'w_xo', 'ffn_pre_g', 'ffn_post_g', 'w_gate_up', 'w_down']
TWIN_WEIGHTS = ['mem_norm_g', 'mix_pre_g', 'mix_post_g', 'w_in', 'gm_v_g', 'gm_w_s', 'gm_b_s', 'pool_w', 'pool_scale', 'attn_sinks', 'w_o', 'x_pre_g', 'x_post_g', 'w_xq', 'w_xkv', 'w_xo', 'ffn_pre_g', 'ffn_post_g', 'w_gate_up', 'w_down']
TWIN_DIFF_INPUT = 'x'
TWIN_INPUTS = ['x', 'mem', 'positions', 'mem_norm_g', 'mix_pre_g', 'mix_post_g', 'w_in', 'gm_v_g', 'gm_w_s', 'gm_b_s', 'pool_w', 'pool_scale', 'attn_sinks', 'w_o', 'x_pre_g', 'x_post_g', 'w_xq', 'w_xkv', 'w_xo', 'ffn_pre_g', 'ffn_post_g', 'w_gate_up', 'w_down', 'loss_target', 'm_mem_norm_g', 'm_mix_pre_g', 'm_mix_post_g', 'm_w_in', 'm_gm_v_g', 'm_gm_w_s', 'm_gm_b_s', 'm_pool_w', 'm_pool_scale', 'm_attn_sinks', 'm_w_o', 'm_x_pre_g', 'm_x_post_g', 'm_w_xq', 'm_w_xkv', 'm_w_xo', 'm_ffn_pre_g', 'm_ffn_post_g', 'm_w_gate_up', 'm_w_down', 'v_mem_norm_g', 'v_mix_pre_g', 'v_mix_post_g', 'v_w_in', 'v_gm_v_g', 'v_gm_w_s', 'v_gm_b_s', 'v_pool_w', 'v_pool_scale', 'v_attn_sinks', 'v_w_o', 'v_x_pre_g', 'v_x_post_g', 'v_w_xq', 'v_w_xkv', 'v_w_xo', 'v_ffn_pre_g', 'v_ffn_post_g', 'v_w_gate_up', 'v_w_down']
TWIN_OUTPUTS = ['loss', 'grad_x', 'grad_mem_norm_g', 'grad_mix_pre_g', 'grad_mix_post_g', 'grad_w_in', 'grad_gm_v_g', 'grad_gm_w_s', 'grad_gm_b_s', 'grad_pool_w', 'grad_pool_scale', 'grad_attn_sinks', 'grad_w_o', 'grad_x_pre_g', 'grad_x_post_g', 'grad_w_xq', 'grad_w_xkv', 'grad_w_xo', 'grad_ffn_pre_g', 'grad_ffn_post_g', 'grad_w_gate_up', 'grad_w_down', 'delta_mem_norm_g', 'delta_mix_pre_g', 'delta_mix_post_g', 'delta_w_in', 'delta_gm_v_g', 'delta_gm_w_s', 'delta_gm_b_s', 'delta_pool_w', 'delta_pool_scale', 'delta_attn_sinks', 'delta_w_o', 'delta_x_pre_g', 'delta_x_post_g', 'delta_w_xq', 'delta_w_xkv', 'delta_w_xo', 'delta_ffn_pre_g', 'delta_ffn_post_g', 'delta_w_gate_up', 'delta_w_down', 'new_m_mem_norm_g', 'new_m_mix_pre_g', 'new_m_mix_post_g', 'new_m_w_in', 'new_m_gm_v_g', 'new_m_gm_w_s', 'new_m_gm_b_s', 'new_m_pool_w', 'new_m_pool_scale', 'new_m_attn_sinks', 'new_m_w_o', 'new_m_x_pre_g', 'new_m_x_post_g', 'new_m_w_xq', 'new_m_w_xkv', 'new_m_w_xo', 'new_m_ffn_pre_g', 'new_m_ffn_post_g', 'new_m_w_gate_up', 'new_m_w_down', 'new_v_mem_norm_g', 'new_v_mix_pre_g', 'new_v_mix_post_g', 'new_v_w_in', 'new_v_gm_v_g', 'new_v_gm_w_s', 'new_v_gm_b_s', 'new_v_pool_w', 'new_v_pool_scale', 'new_v_attn_sinks', 'new_v_w_o', 'new_v_x_pre_g', 'new_v_x_post_g', 'new_v_w_xq', 'new_v_w_xkv', 'new_v_w_xo', 'new_v_ffn_pre_g', 'new_v_ffn_post_g', 'new_v_w_gate_up', 'new_v_w_down']
TWIN_LEAF_KINDS = {'loss': 'loss', 'grad_x': 'grad_x', 'grad_mem_norm_g': 'grad_w', 'grad_mix_pre_g': 'grad_w', 'grad_mix_post_g': 'grad_w', 'grad_w_in': 'grad_w', 'grad_gm_v_g': 'grad_w', 'grad_gm_w_s': 'grad_w', 'grad_gm_b_s': 'grad_w', 'grad_pool_w': 'grad_w', 'grad_pool_scale': 'grad_w', 'grad_attn_sinks': 'grad_w', 'grad_w_o': 'grad_w', 'grad_x_pre_g': 'grad_w', 'grad_x_post_g': 'grad_w', 'grad_w_xq': 'grad_w', 'grad_w_xkv': 'grad_w', 'grad_w_xo': 'grad_w', 'grad_ffn_pre_g': 'grad_w', 'grad_ffn_post_g': 'grad_w', 'grad_w_gate_up': 'grad_w', 'grad_w_down': 'grad_w', 'delta_mem_norm_g': 'delta_w', 'delta_mix_pre_g': 'delta_w', 'delta_mix_post_g': 'delta_w', 'delta_w_in': 'delta_w', 'delta_gm_v_g': 'delta_w', 'delta_gm_w_s': 'delta_w', 'delta_gm_b_s': 'delta_w', 'delta_pool_w': 'delta_w', 'delta_pool_scale': 'delta_w', 'delta_attn_sinks': 'delta_w', 'delta_w_o': 'delta_w', 'delta_x_pre_g': 'delta_w', 'delta_x_post_g': 'delta_w', 'delta_w_xq': 'delta_w', 'delta_w_xkv': 'delta_w', 'delta_w_xo': 'delta_w', 'delta_ffn_pre_g': 'delta_w', 'delta_ffn_post_g': 'delta_w', 'delta_w_gate_up': 'delta_w', 'delta_w_down': 'delta_w', 'new_m_mem_norm_g': 'new_m', 'new_m_mix_pre_g': 'new_m', 'new_m_mix_post_g': 'new_m', 'new_m_w_in': 'new_m', 'new_m_gm_v_g': 'new_m', 'new_m_gm_w_s': 'new_m', 'new_m_gm_b_s': 'new_m', 'new_m_pool_w': 'new_m', 'new_m_pool_scale': 'new_m', 'new_m_attn_sinks': 'new_m', 'new_m_w_o': 'new_m', 'new_m_x_pre_g': 'new_m', 'new_m_x_post_g': 'new_m', 'new_m_w_xq': 'new_m', 'new_m_w_xkv': 'new_m', 'new_m_w_xo': 'new_m', 'new_m_ffn_pre_g': 'new_m', 'new_m_ffn_post_g': 'new_m', 'new_m_w_gate_up': 'new_m', 'new_m_w_down': 'new_m', 'new_v_mem_norm_g': 'new_v', 'new_v_mix_pre_g': 'new_v', 'new_v_mix_post_g': 'new_v', 'new_v_w_in': 'new_v', 'new_v_gm_v_g': 'new_v', 'new_v_gm_w_s': 'new_v', 'new_v_gm_b_s': 'new_v', 'new_v_pool_w': 'new_v', 'new_v_pool_scale': 'new_v', 'new_v_attn_sinks': 'new_v', 'new_v_w_o': 'new_v', 'new_v_x_pre_g': 'new_v', 'new_v_x_post_g': 'new_v', 'new_v_w_xq': 'new_v', 'new_v_w_xkv': 'new_v', 'new_v_w_xo': 'new_v', 'new_v_ffn_pre_g': 'new_v', 'new_v_ffn_post_g': 'new_v', 'new_v_w_gate_up': 'new_v', 'new_v_w_down': 'new_v'}


def _forward(args):
    return _fwd_reference(*[args[k] for k in FWD_PARAMS])


def _output_shape():
    out = _jax.eval_shape(lambda: _forward(_fwd_setup_inputs(0)))
    return out.shape, out.dtype

N_MICROBATCH = 1
ADAM_LR = 0.001
ADAM_B1 = 0.9
ADAM_B2 = 0.999
ADAM_EPS = 1e-08
ADAM_WD = 0.01
ADAM_STEP = 10
PER_EXAMPLE_BATCH_AXIS = {'x': 0, 'mem': 0, 'positions': 0, 'loss_target': 0}
SHARED_INPUTS = []
_WEIGHT_DTYPES = {'mem_norm_g': _jnp.float32, 'mix_pre_g': _jnp.float32, 'mix_post_g': _jnp.float32, 'w_in': _jnp.float32, 'gm_v_g': _jnp.float32, 'gm_w_s': _jnp.float32, 'gm_b_s': _jnp.float32, 'pool_w': _jnp.float32, 'pool_scale': _jnp.float32, 'attn_sinks': _jnp.float32, 'w_o': _jnp.float32, 'x_pre_g': _jnp.float32, 'x_post_g': _jnp.float32, 'w_xq': _jnp.float32, 'w_xkv': _jnp.float32, 'w_xo': _jnp.float32, 'ffn_pre_g': _jnp.float32, 'ffn_post_g': _jnp.float32, 'w_gate_up': _jnp.float32, 'w_down': _jnp.float32}
MOMENT_SCALE = {'mem_norm_g': 3.940520e+01, 'mix_pre_g': 9.424588e+00, 'mix_post_g': 3.483714e+01, 'w_in': 8.131345e+00, 'gm_v_g': 5.988946e-01, 'gm_w_s': 7.983212e-01, 'gm_b_s': 1.157927e+00, 'pool_w': 3.925209e+00, 'pool_scale': 4.741486e+00, 'attn_sinks': 8.681859e-01, 'w_o': 1.359225e+01, 'x_pre_g': 5.954501e+00, 'x_post_g': 3.769192e+01, 'w_xq': 5.845645e+00, 'w_xkv': 1.381565e+01, 'w_xo': 1.900777e+01, 'ffn_pre_g': 7.335269e+00, 'ffn_post_g': 3.224893e+01, 'w_gate_up': 2.967658e+00, 'w_down': 5.734798e+00}


def _to_microbatches(a, axis):
    t = _jnp.moveaxis(a, axis, 0)
    t = t.reshape((N_MICROBATCH, t.shape[0] // N_MICROBATCH) + t.shape[1:])
    return _jnp.moveaxis(t, 1, axis + 1)


def setup_inputs(seed: int = 0) -> dict:
    inp = _fwd_setup_inputs(seed)
    key = _jax.random.fold_in(_jax.random.key(seed), 7919)
    shape, _ = _output_shape()
    out = dict(inp)
    out["loss_target"] = _jax.random.normal(_jax.random.fold_in(key, 0), shape, _jnp.float32)
    for i, name in enumerate(TWIN_WEIGHTS):
        w = inp[name].astype(_jnp.float32)
        if MOMENT_SCALE is None:
            s = _jnp.sqrt(_jnp.mean(_jnp.square(w)) + 1e-30)
        else:
            s = MOMENT_SCALE[name]
        km, kv = _jax.random.split(_jax.random.fold_in(key, i + 1))
        out[name] = w
        out["m_" + name] = s * _jax.random.normal(km, w.shape, _jnp.float32)
        out["v_" + name] = (s * s) * _jax.random.uniform(kv, w.shape, _jnp.float32, 0.5, 1.5)
    if N_MICROBATCH > 1:
        for name, axis in PER_EXAMPLE_BATCH_AXIS.items():
            out[name] = _to_microbatches(out[name], axis)
    return {'x': out['x'], 'mem': out['mem'], 'positions': out['positions'], 'mem_norm_g': out['mem_norm_g'], 'mix_pre_g': out['mix_pre_g'], 'mix_post_g': out['mix_post_g'], 'w_in': out['w_in'], 'gm_v_g': out['gm_v_g'], 'gm_w_s': out['gm_w_s'], 'gm_b_s': out['gm_b_s'], 'pool_w': out['pool_w'], 'pool_scale': out['pool_scale'], 'attn_sinks': out['attn_sinks'], 'w_o': out['w_o'], 'x_pre_g': out['x_pre_g'], 'x_post_g': out['x_post_g'], 'w_xq': out['w_xq'], 'w_xkv': out['w_xkv'], 'w_xo': out['w_xo'], 'ffn_pre_g': out['ffn_pre_g'], 'ffn_post_g': out['ffn_post_g'], 'w_gate_up': out['w_gate_up'], 'w_down': out['w_down'], 'loss_target': out['loss_target'], 'm_mem_norm_g': out['m_mem_norm_g'], 'm_mix_pre_g': out['m_mix_pre_g'], 'm_mix_post_g': out['m_mix_post_g'], 'm_w_in': out['m_w_in'], 'm_gm_v_g': out['m_gm_v_g'], 'm_gm_w_s': out['m_gm_w_s'], 'm_gm_b_s': out['m_gm_b_s'], 'm_pool_w': out['m_pool_w'], 'm_pool_scale': out['m_pool_scale'], 'm_attn_sinks': out['m_attn_sinks'], 'm_w_o': out['m_w_o'], 'm_x_pre_g': out['m_x_pre_g'], 'm_x_post_g': out['m_x_post_g'], 'm_w_xq': out['m_w_xq'], 'm_w_xkv': out['m_w_xkv'], 'm_w_xo': out['m_w_xo'], 'm_ffn_pre_g': out['m_ffn_pre_g'], 'm_ffn_post_g': out['m_ffn_post_g'], 'm_w_gate_up': out['m_w_gate_up'], 'm_w_down': out['m_w_down'], 'v_mem_norm_g': out['v_mem_norm_g'], 'v_mix_pre_g': out['v_mix_pre_g'], 'v_mix_post_g': out['v_mix_post_g'], 'v_w_in': out['v_w_in'], 'v_gm_v_g': out['v_gm_v_g'], 'v_gm_w_s': out['v_gm_w_s'], 'v_gm_b_s': out['v_gm_b_s'], 'v_pool_w': out['v_pool_w'], 'v_pool_scale': out['v_pool_scale'], 'v_attn_sinks': out['v_attn_sinks'], 'v_w_o': out['v_w_o'], 'v_x_pre_g': out['v_x_pre_g'], 'v_x_post_g': out['v_x_post_g'], 'v_w_xq': out['v_w_xq'], 'v_w_xkv': out['v_w_xkv'], 'v_w_xo': out['v_w_xo'], 'v_ffn_pre_g': out['v_ffn_pre_g'], 'v_ffn_post_g': out['v_ffn_post_g'], 'v_w_gate_up': out['v_w_gate_up'], 'v_w_down': out['v_w_down']}


def _loss(weights, diff, rest, loss_target):
    with _jax.named_scope("forward"):
        args = {**rest, TWIN_DIFF_INPUT: diff, **{k: w.astype(_WEIGHT_DTYPES[k]) for k, w in weights.items()}}
        y = _forward(args)
    with _jax.named_scope("loss_head"):
        err = _jnp.square(y.astype(_jnp.float32) - loss_target)
        return 0.5 * _jnp.sum(_jnp.mean(err, axis=-1)) if err.ndim else 0.5 * err


def _adamw(w, g, m, v):
    m = ADAM_B1 * m + (1.0 - ADAM_B1) * g
    v = ADAM_B2 * v + (1.0 - ADAM_B2) * _jnp.square(g)
    m_hat = m / (1.0 - ADAM_B1 ** ADAM_STEP)
    v_hat = v / (1.0 - ADAM_B2 ** ADAM_STEP)
    delta = -ADAM_LR * (m_hat / (_jnp.sqrt(v_hat) + ADAM_EPS) + ADAM_WD * w)
    return delta, m, v


def reference(x, mem, positions, mem_norm_g, mix_pre_g, mix_post_g, w_in, gm_v_g, gm_w_s, gm_b_s, pool_w, pool_scale, attn_sinks, w_o, x_pre_g, x_post_g, w_xq, w_xkv, w_xo, ffn_pre_g, ffn_post_g, w_gate_up, w_down, loss_target, m_mem_norm_g, m_mix_pre_g, m_mix_post_g, m_w_in, m_gm_v_g, m_gm_w_s, m_gm_b_s, m_pool_w, m_pool_scale, m_attn_sinks, m_w_o, m_x_pre_g, m_x_post_g, m_w_xq, m_w_xkv, m_w_xo, m_ffn_pre_g, m_ffn_post_g, m_w_gate_up, m_w_down, v_mem_norm_g, v_mix_pre_g, v_mix_post_g, v_w_in, v_gm_v_g, v_gm_w_s, v_gm_b_s, v_pool_w, v_pool_scale, v_attn_sinks, v_w_o, v_x_pre_g, v_x_post_g, v_w_xq, v_w_xkv, v_w_xo, v_ffn_pre_g, v_ffn_post_g, v_w_gate_up, v_w_down):
    given = dict(x=x, mem=mem, positions=positions, mem_norm_g=mem_norm_g, mix_pre_g=mix_pre_g, mix_post_g=mix_post_g, w_in=w_in, gm_v_g=gm_v_g, gm_w_s=gm_w_s, gm_b_s=gm_b_s, pool_w=pool_w, pool_scale=pool_scale, attn_sinks=attn_sinks, w_o=w_o, x_pre_g=x_pre_g, x_post_g=x_post_g, w_xq=w_xq, w_xkv=w_xkv, w_xo=w_xo, ffn_pre_g=ffn_pre_g, ffn_post_g=ffn_post_g, w_gate_up=w_gate_up, w_down=w_down, loss_target=loss_target, m_mem_norm_g=m_mem_norm_g, m_mix_pre_g=m_mix_pre_g, m_mix_post_g=m_mix_post_g, m_w_in=m_w_in, m_gm_v_g=m_gm_v_g, m_gm_w_s=m_gm_w_s, m_gm_b_s=m_gm_b_s, m_pool_w=m_pool_w, m_pool_scale=m_pool_scale, m_attn_sinks=m_attn_sinks, m_w_o=m_w_o, m_x_pre_g=m_x_pre_g, m_x_post_g=m_x_post_g, m_w_xq=m_w_xq, m_w_xkv=m_w_xkv, m_w_xo=m_w_xo, m_ffn_pre_g=m_ffn_pre_g, m_ffn_post_g=m_ffn_post_g, m_w_gate_up=m_w_gate_up, m_w_down=m_w_down, v_mem_norm_g=v_mem_norm_g, v_mix_pre_g=v_mix_pre_g, v_mix_post_g=v_mix_post_g, v_w_in=v_w_in, v_gm_v_g=v_gm_v_g, v_gm_w_s=v_gm_w_s, v_gm_b_s=v_gm_b_s, v_pool_w=v_pool_w, v_pool_scale=v_pool_scale, v_attn_sinks=v_attn_sinks, v_w_o=v_w_o, v_x_pre_g=v_x_pre_g, v_x_post_g=v_x_post_g, v_w_xq=v_w_xq, v_w_xkv=v_w_xkv, v_w_xo=v_w_xo, v_ffn_pre_g=v_ffn_pre_g, v_ffn_post_g=v_ffn_post_g, v_w_gate_up=v_w_gate_up, v_w_down=v_w_down)
    weights = {n: given[n] for n in TWIN_WEIGHTS}
    shared = {n: given[n] for n in SHARED_INPUTS}
    per_example = {n: given[n] for n in ['x', 'mem', 'positions']}
    grad_fn = _jax.value_and_grad(_loss, argnums=(0, 1))

    def one_microbatch(ex, loss_target):
        ex = dict(ex)
        diff = ex.pop(TWIN_DIFF_INPUT)
        return grad_fn(weights, diff, {**shared, **ex}, loss_target)

    if N_MICROBATCH == 1:
        loss, (grad_w, grad_x) = one_microbatch(per_example, given["loss_target"])
    else:
        def body(carry, xs):
            loss_sum, grad_sum = carry
            l_k, (gw_k, gx_k) = one_microbatch(xs[0], xs[1])
            with _jax.named_scope("update"):
                return (loss_sum + l_k, _jax.tree.map(_jnp.add, grad_sum, gw_k)), gx_k

        init = (_jnp.zeros((), _jnp.float32), _jax.tree.map(_jnp.zeros_like, weights))
        (loss, grad_w), grad_x = _jax.lax.scan(body, init, (per_example, given["loss_target"]))
    with _jax.named_scope("update"):
        delta_w, new_m, new_v = {}, {}, {}
        for n in TWIN_WEIGHTS:
            delta_w[n], new_m[n], new_v[n] = _adamw(weights[n], grad_w[n], given["m_" + n], given["v_" + n])
    return (loss, grad_x, *[grad_w[n] for n in TWIN_WEIGHTS], *[delta_w[n] for n in TWIN_WEIGHTS],
            *[new_m[n] for n in TWIN_WEIGHTS], *[new_v[n] for n in TWIN_WEIGHTS])
```

```python
import functools
import math

import jax
import jax.numpy as jnp
from jax import lax
from jax.experimental import pallas as pl
from jax.experimental.pallas import tpu as pltpu

F32, BF16 = jnp.float32, jnp.bfloat16
EPS = 1e-6
HEAD_DIM = 64
BLOCK = 128
POOL_WINDOWS = (2, 4, 8, 16)
ATT_Q_HEADS = 8
X_HEADS = 4
ROPE_THETA = 10000.0
ADAM_LR, ADAM_B1, ADAM_B2, ADAM_EPS, ADAM_WD, ADAM_STEP = 0.001, 0.9, 0.999, 1e-08, 0.01, 10
N_DEV = 8
TOKEN_TILE = 512
VMEM_LIMIT_BYTES = 50 * 2**20
NEG = -1e30
MESH = pl.DeviceIdType.MESH

NN = ((1,), (0,))
NT = ((1,), (1,))
TN = ((0,), (0,))


def _dot(a, b, dims):
    return lax.dot_general(a.astype(BF16), b.astype(BF16), (dims, ((), ())), preferred_element_type=F32)


def _params(*sem):
    return pltpu.CompilerParams(dimension_semantics=sem, vmem_limit_bytes=VMEM_LIMIT_BYTES)


def _full(shape):
    return pl.BlockSpec(shape, lambda *_: (0,) * len(shape))


def _lane(shape):
    return lax.broadcasted_iota(jnp.int32, shape, len(shape) - 1)


def _rms_fwd(x, g):
    r = lax.rsqrt(jnp.mean(x * x, axis=-1, keepdims=True) + EPS)
    return x * r * g


def _rms_bwd(x, g, dy):
    r = lax.rsqrt(jnp.mean(x * x, axis=-1, keepdims=True) + EPS)
    xh = x * r
    dg = jnp.sum(dy * xh, axis=0, keepdims=True)
    dxh = dy * g
    dx = r * (dxh - xh * jnp.mean(dxh * xh, axis=-1, keepdims=True))
    return dx, dg


def _silu_parts(gate):
    sg = 1.0 / (1.0 + jnp.exp(-gate))
    return gate * sg, sg


def _swiglu_cols(gu_ref, out_ref, width):
    step = 128 * max(1, (width // 128) // 4)
    for c0 in range(0, width, step):
        c1 = min(width, c0 + step)
        gate = gu_ref[:, c0:c1].astype(F32)
        up = gu_ref[:, width + c0:width + c1].astype(F32)
        out_ref[:, c0:c1] = (_silu_parts(gate)[0] * up).astype(out_ref.dtype)


def _col_tile(n, cap=1408):
    best = n
    for t in range(128, cap + 1, 128):
        if n % t == 0:
            best = t
    return best if n > cap else n


def _mm_fwd(a, w, *, nt, name, pre=None, g_pre=None, post=None, g_post=None, xres=None, out_dtype=BF16, tile=TOKEN_TILE):
    S = a.shape[0]
    n_out, k = (w.shape[0], w.shape[1]) if nt else (w.shape[1], w.shape[0])
    T = min(tile, S)
    tn = n_out if post else _col_tile(n_out)
    grid = (S // T, n_out // tn)

    def body(*refs):
        it = iter(refs)
        a_ref, w_ref = next(it), next(it)
        gpre_ref = next(it) if pre == "norm" else None
        gpost_ref, x_ref = (next(it), next(it)) if post else (None, None)
        outs = [next(it) for _ in range((2 if post else 1) + (1 if pre == "norm" else 0))]
        as_ref = next(it) if pre else None
        if pre:
            @pl.when(pl.program_id(1) == 0)
            def _():
                if pre == "norm":
                    h = _rms_fwd(a_ref[...], gpre_ref[...]).astype(BF16)
                    outs[-1][...] = h
                    as_ref[...] = h
                else:
                    _swiglu_cols(a_ref, as_ref, k)
            av = as_ref[...]
        else:
            av = a_ref[...]
        acc = _dot(av, w_ref[...], NT if nt else NN)
        if post:
            outs[0][...] = acc
            outs[1][...] = x_ref[...] + _rms_fwd(acc, gpost_ref[...])
        else:
            outs[0][...] = acc.astype(out_dtype)

    in_specs = [pl.BlockSpec((T, a.shape[1]), lambda i, j: (i, 0)),
                pl.BlockSpec((tn, k), lambda i, j: (j, 0)) if nt else pl.BlockSpec((k, tn), lambda i, j: (0, j))]
    args = [a, w]
    if pre == "norm":
        in_specs.append(_full((1, k)))
        args.append(g_pre)
    if post:
        in_specs += [_full((1, n_out)), pl.BlockSpec((T, n_out), lambda i, j: (i, 0))]
        args += [g_post, xres]
    out_block = pl.BlockSpec((T, tn), lambda i, j: (i, j))
    if post:
        out_shape = [jax.ShapeDtypeStruct((S, n_out), F32)] * 2
        out_specs = [out_block, out_block]
    else:
        out_shape = [jax.ShapeDtypeStruct((S, n_out), out_dtype)]
        out_specs = [out_block]
    if pre == "norm":
        out_shape.append(jax.ShapeDtypeStruct((S, k), BF16))
        out_specs.append(pl.BlockSpec((T, k), lambda i, j: (i, 0)))
    scratch = [pltpu.VMEM((T, k), BF16)] if pre else []
    return pl.pallas_call(body, name=name, grid=grid, in_specs=in_specs, out_specs=out_specs, out_shape=out_shape,
                          scratch_shapes=scratch, compiler_params=_params("parallel", "arbitrary"))(*args)


def _mm_bwd_pre(m, dx, g, w, *, name, tile=TOKEN_TILE):
    S, n = m.shape
    ku = w.shape[0]
    T = min(tile, S)
    tn = _col_tile(ku)
    grid = (S // T, ku // tn)

    def body(m_ref, dx_ref, g_ref, w_ref, du_ref, dm_ref, dg_ref, as_ref):
        i, j = pl.program_id(0), pl.program_id(1)

        @pl.when((i == 0) & (j == 0))
        def _():
            dg_ref[...] = jnp.zeros_like(dg_ref)

        @pl.when(j == 0)
        def _():
            dmv, dg = _rms_bwd(m_ref[...], g_ref[...], dx_ref[...])
            dg_ref[...] += dg
            dmb = dmv.astype(BF16)
            dm_ref[...] = dmb
            as_ref[...] = dmb

        du_ref[...] = _dot(as_ref[...], w_ref[...], NT).astype(BF16)

    row = pl.BlockSpec((T, n), lambda i, j: (i, 0))
    return pl.pallas_call(
        body, name=name, grid=grid,
        in_specs=[row, row, _full((1, n)), pl.BlockSpec((tn, n), lambda i, j: (j, 0))],
        out_specs=[pl.BlockSpec((T, tn), lambda i, j: (i, j)), row, _full((1, n))],
        out_shape=[jax.ShapeDtypeStruct((S, ku), BF16), jax.ShapeDtypeStruct((S, n), BF16), jax.ShapeDtypeStruct((1, n), F32)],
        scratch_shapes=[pltpu.VMEM((T, n), BF16)], compiler_params=_params("arbitrary", "arbitrary"))(m, dx, g, w)


def _mm_bwd_post(dy, w, x, g, dx_in, *, nt, name, tile=TOKEN_TILE):
    S, k = dy.shape
    n = x.shape[1]
    T = min(tile, S)

    def body(dy_ref, w_ref, x_ref, g_ref, dxi_ref, dxo_ref, dg_ref):
        @pl.when(pl.program_id(0) == 0)
        def _():
            dg_ref[...] = jnp.zeros_like(dg_ref)

        dh = _dot(dy_ref[...], w_ref[...], NT if nt else NN)
        dxn, dg = _rms_bwd(x_ref[...], g_ref[...], dh)
        dg_ref[...] += dg
        dxo_ref[...] = dxi_ref[...] + dxn

    row = pl.BlockSpec((T, n), lambda i: (i, 0))
    return pl.pallas_call(
        body, name=name, grid=(S // T,),
        in_specs=[pl.BlockSpec((T, k), lambda i: (i, 0)), _full(w.shape), row, _full((1, n)), row],
        out_specs=[row, _full((1, n))],
        out_shape=[jax.ShapeDtypeStruct((S, n), F32), jax.ShapeDtypeStruct((1, n), F32)],
        compiler_params=_params("arbitrary"))(dy, w, x, g, dx_in)


def _wgrad(a, b, *, name, swiglu=False, tile=TOKEN_TILE):
    S = a.shape[0]
    k = a.shape[1] // 2 if swiglu else a.shape[1]
    n = b.shape[1]
    T = min(tile, S)
    tk = _col_tile(k, cap=1024)
    nk = k // tk
    grid = (nk, S // T)

    def body(*refs):
        if swiglu:
            gate_ref, up_ref, b_ref, o_ref = refs
            av = (_silu_parts(gate_ref[...].astype(F32))[0] * up_ref[...].astype(F32)).astype(BF16)
        else:
            a_ref, b_ref, o_ref = refs
            av = a_ref[...]
        part = _dot(av, b_ref[...], TN)

        @pl.when(pl.program_id(1) == 0)
        def _():
            o_ref[...] = part

        @pl.when(pl.program_id(1) > 0)
        def _():
            o_ref[...] += part

    in_specs = [pl.BlockSpec((T, tk), lambda kk, t: (t, kk))]
    args = [a]
    if swiglu:
        in_specs.append(pl.BlockSpec((T, tk), lambda kk, t: (t, kk + nk)))
        args.append(a)
    in_specs.append(pl.BlockSpec((T, n), lambda kk, t: (t, 0)))
    args.append(b)
    return pl.pallas_call(body, name=name, grid=grid, in_specs=in_specs,
                          out_specs=pl.BlockSpec((tk, n), lambda kk, t: (kk, 0)),
                          out_shape=jax.ShapeDtypeStruct((k, n), F32),
                          compiler_params=_params("parallel", "arbitrary"))(*args)


def _swiglu_bwd(dact, gu, *, name, tile=256):
    S, f = dact.shape
    T = min(tile, S)

    def body(da_ref, gu_ref, o_ref):
        step = 128 * max(1, (f // 128) // 4)
        for c0 in range(0, f, step):
            c1 = min(f, c0 + step)
            gate = gu_ref[:, c0:c1].astype(F32)
            up = gu_ref[:, f + c0:f + c1].astype(F32)
            da = da_ref[:, c0:c1].astype(F32)
            sl, sg = _silu_parts(gate)
            o_ref[:, c0:c1] = (da * up * (sg + sl * (1.0 - sg))).astype(BF16)
            o_ref[:, f + c0:f + c1] = (da * sl).astype(BF16)

    return pl.pallas_call(
        body, name=name, grid=(S // T,),
        in_specs=[pl.BlockSpec((T, f), lambda i: (i, 0)), pl.BlockSpec((T, 2 * f), lambda i: (i, 0))],
        out_specs=pl.BlockSpec((T, 2 * f), lambda i: (i, 0)),
        out_shape=jax.ShapeDtypeStruct((S, 2 * f), BF16), compiler_params=_params("parallel"))(dact, gu)


def _loss_grad(y, target, *, name, tile=TOKEN_TILE):
    S, d = y.shape
    T = min(tile, S)
    steps = S // T

    def body(y_ref, t_ref, loss_ref, dy_ref, acc_ref):
        i = pl.program_id(0)

        @pl.when(i == 0)
        def _():
            acc_ref[...] = jnp.zeros_like(acc_ref)

        diff = y_ref[...] - t_ref[...]
        dy_ref[...] = diff * (1.0 / d)
        acc_ref[...] += jnp.sum(diff * diff, axis=0, keepdims=True)

        @pl.when(i == steps - 1)
        def _():
            total = jnp.sum(acc_ref[...], axis=1, keepdims=True) * (0.5 / d)
            loss_ref[...] = jnp.broadcast_to(total, loss_ref.shape)

    row = pl.BlockSpec((T, d), lambda i: (i, 0))
    return pl.pallas_call(
        body, name=name, grid=(steps,), in_specs=[row, row], out_specs=[_full((1, 128)), row],
        out_shape=[jax.ShapeDtypeStruct((1, 128), F32), jax.ShapeDtypeStruct((S, d), F32)],
        scratch_shapes=[pltpu.VMEM((1, d), F32)], compiler_params=_params("arbitrary"))(y, target)


def _norm_dgain(x, dys, *, name):
    def body(x_ref, dy_ref, o_ref):
        xv = x_ref[...]
        dy = dy_ref[0]
        for l in range(1, dys.shape[0]):
            dy = dy + dy_ref[l]
        r = lax.rsqrt(jnp.mean(xv * xv, axis=-1, keepdims=True) + EPS)
        o_ref[...] = jnp.sum(dy * xv * r, axis=0, keepdims=True)

    return pl.pallas_call(body, name=name, out_shape=jax.ShapeDtypeStruct((1, x.shape[1]), F32),
                          compiler_params=pltpu.CompilerParams(vmem_limit_bytes=VMEM_LIMIT_BYTES))(x, dys)


def _xattn_probs(q_ref, kv_ref, h, d, hd):
    q = q_ref[:, h * hd:(h + 1) * hd]
    kh = kv_ref[:, h * hd:(h + 1) * hd]
    vh = kv_ref[:, d + h * hd:d + (h + 1) * hd]
    s = _dot(q, kh, NT) * (hd ** -0.5)
    e = jnp.exp(s - jnp.max(s, axis=-1, keepdims=True))
    return q, kh, vh, e / jnp.sum(e, axis=-1, keepdims=True)


def _xattn_fwd(q, kv, *, name, tile=TOKEN_TILE):
    S, d = q.shape
    mlen = kv.shape[0]
    hd = d // X_HEADS
    T = min(tile, S)

    def body(q_ref, kv_ref, o_ref):
        for h in range(X_HEADS):
            _, _, vh, p = _xattn_probs(q_ref, kv_ref, h, d, hd)
            o_ref[:, h * hd:(h + 1) * hd] = _dot(p, vh, NN).astype(BF16)

    row = pl.BlockSpec((T, d), lambda i: (i, 0))
    return pl.pallas_call(body, name=name, grid=(S // T,), in_specs=[row, _full((mlen, 2 * d))], out_specs=row,
                          out_shape=jax.ShapeDtypeStruct((S, d), BF16), compiler_params=_params("parallel"))(q, kv)


def _xattn_bwd(q, kv, do, *, name, tile=TOKEN_TILE):
    S, d = q.shape
    mlen = kv.shape[0]
    hd = d // X_HEADS
    T = min(tile, S)

    def body(q_ref, kv_ref, do_ref, dq_ref, dkv_ref):
        @pl.when(pl.program_id(0) == 0)
        def _():
            dkv_ref[...] = jnp.zeros_like(dkv_ref)

        for h in range(X_HEADS):
            qh, kh, vh, p = _xattn_probs(q_ref, kv_ref, h, d, hd)
            doh = do_ref[:, h * hd:(h + 1) * hd]
            dp = _dot(doh, vh, NT)
            ds = p * (dp - jnp.sum(p * dp, axis=-1, keepdims=True)) * (hd ** -0.5)
            dq_ref[:, h * hd:(h + 1) * hd] = _dot(ds, kh, NN).astype(BF16)
            dkv_ref[:, h * hd:(h + 1) * hd] += _dot(ds, qh, TN)
            dkv_ref[:, d + h * hd:d + (h + 1) * hd] += _dot(p, doh, TN)

    row = pl.BlockSpec((T, d), lambda i: (i, 0))
    return pl.pallas_call(
        body, name=name, grid=(S // T,), in_specs=[row, _full((mlen, 2 * d)), row],
        out_specs=[row, _full((mlen, 2 * d))],
        out_shape=[jax.ShapeDtypeStruct((S, d), BF16), jax.ShapeDtypeStruct((mlen, 2 * d), F32)],
        compiler_params=_params("arbitrary"))(q, kv, do)


GM0, POOL0, Q0, KV0, Z_END = 0, 512, 768, 1280, 1536


def _rope_tables(pos_col, inv_row, *, name):
    S = pos_col.shape[0]

    def body(p_ref, inv_ref, cos_ref, sin_ref):
        ang = p_ref[...].astype(F32) * inv_ref[...]
        sin = jnp.sin(ang)
        cos_ref[...] = jnp.cos(ang)
        sin_ref[...] = jnp.where(_lane(ang.shape) % HEAD_DIM < HEAD_DIM // 2, -sin, sin)

    T = min(TOKEN_TILE, S)
    return pl.pallas_call(
        body, name=name, grid=(S // T,),
        in_specs=[pl.BlockSpec((T, 1), lambda i: (i, 0)), _full((1, 128))],
        out_specs=[pl.BlockSpec((T, 128), lambda i: (i, 0))] * 2,
        out_shape=[jax.ShapeDtypeStruct((S, 128), F32)] * 2, compiler_params=_params("parallel"))(pos_col, inv_row)


def _swap_halves(x):
    n = x.shape[-1]
    return jnp.where(_lane(x.shape) % HEAD_DIM < HEAD_DIM // 2, pltpu.roll(x, n - HEAD_DIM // 2, 1), pltpu.roll(x, HEAD_DIM // 2, 1))


def _rope(x, cos, sin_s):
    reps = x.shape[-1] // 128
    if reps > 1:
        cos, sin_s = jnp.tile(cos, (1, reps)), jnp.tile(sin_s, (1, reps))
    return x * cos + _swap_halves(x) * sin_s


def _rope_bwd(dy, cos, sin_s):
    reps = dy.shape[-1] // 128
    if reps > 1:
        cos, sin_s = jnp.tile(cos, (1, reps)), jnp.tile(sin_s, (1, reps))
    return dy * cos + _swap_halves(dy * sin_s)


def _gelu_parts(x):
    c = math.sqrt(2.0 / math.pi)
    t = jnp.tanh(c * (x + 0.044715 * x * x * x))
    return 0.5 * x * (1.0 + t), t


def _gelu_grad(x, t):
    c = math.sqrt(2.0 / math.pi)
    return 0.5 * (1.0 + t) + 0.5 * x * (1.0 - t * t) * c * (1.0 + 3.0 * 0.044715 * x * x)


def _group_sum(x):
    gid = _lane(x.shape) // HEAD_DIM
    out = jnp.zeros_like(x)
    for g in range(x.shape[-1] // HEAD_DIM):
        sel = gid == g
        out = jnp.where(sel, jnp.sum(jnp.where(sel, x, 0.0), axis=-1, keepdims=True), out)
    return out


def _low_half(shape):
    return _lane(shape) % 128 < HEAD_DIM


def _tril(w):
    r = lax.broadcasted_iota(jnp.int32, w.shape, 0)
    c = lax.broadcasted_iota(jnp.int32, w.shape, 1)
    return jnp.where(r >= c, w, 0.0)


def _gating_fwd(zg, vg, wc_ref, bfull):
    ge, th = _gelu_parts(zg)
    u, v = ge[:, :256], ge[:, 256:]
    r = lax.rsqrt(_group_sum(v * v) * (1.0 / HEAD_DIM) + EPS)
    vhat = v * r
    vn = vhat * vg
    low = _low_half((BLOCK, 128))
    blocks = []
    for blk in range(2):
        vb = vn[:, 128 * blk:128 * (blk + 1)]
        m0 = _dot(_tril(wc_ref[2 * blk]), vb, NN)
        m1 = _dot(_tril(wc_ref[2 * blk + 1]), vb, NN)
        blocks.append(jnp.where(low, m0, m1))
    mixed = jnp.concatenate(blocks, axis=1) + bfull
    return u * mixed, (th, u, r, vhat, vn, mixed)


def _band(w, transpose):
    if transpose:
        s2 = lax.broadcasted_iota(jnp.int32, (2 * BLOCK, BLOCK), 0)
        t = lax.broadcasted_iota(jnp.int32, (2 * BLOCK, BLOCK), 1)
    else:
        t = lax.broadcasted_iota(jnp.int32, (BLOCK, 2 * BLOCK), 0)
        s2 = lax.broadcasted_iota(jnp.int32, (BLOCK, 2 * BLOCK), 1)
    d = t - s2 + BLOCK
    return jnp.where((d >= 0) & (d < w), 1.0, 0.0).astype(BF16)


def _exact_dot(b01, x):
    hi = x.astype(BF16)
    lo = (x - hi.astype(F32)).astype(BF16)
    return _dot(b01, hi, NN) + _dot(b01, lo, NN)


def _inv_count(n):
    t = lax.broadcasted_iota(jnp.int32, (BLOCK, 256), 0) + n * BLOCK
    gid = _lane((BLOCK, 256)) // HEAD_DIM
    win = jnp.where(gid == 0, POOL_WINDOWS[0], jnp.where(gid == 1, POOL_WINDOWS[1], jnp.where(gid == 2, POOL_WINDOWS[2], POOL_WINDOWS[3])))
    return 1.0 / jnp.minimum(t + 1, win).astype(F32)


def _pool_fwd(p, pp, n, bd, scale):
    low = _low_half((BLOCK, 128))
    blocks = []
    for blk in range(2):
        both = jnp.concatenate([pp[:, 128 * blk:128 * (blk + 1)], p[:, 128 * blk:128 * (blk + 1)]], axis=0)
        s0 = _exact_dot(_band(POOL_WINDOWS[2 * blk], False), both)
        s1 = _exact_dot(_band(POOL_WINDOWS[2 * blk + 1], False), both)
        blocks.append(jnp.where(low, s0, s1))
    inv = _inv_count(n)
    pooled = jnp.concatenate(blocks, axis=1) * inv - p
    mapped = _dot(pooled, bd, NN)
    return mapped * scale, (inv, pooled, mapped)


def _attn_mask(first_key):
    qi = lax.broadcasted_iota(jnp.int32, (BLOCK, 2 * BLOCK), 0)
    ki = lax.broadcasted_iota(jnp.int32, (BLOCK, 2 * BLOCK), 1)
    return (ki > qi) & (ki <= qi + BLOCK) & (ki >= first_key)


def _attn_head(j, qrot, kk, kks, vv, vvs, sinks_ref, valid):
    half, kv_head = j % 2, j // 4
    mine = (_lane((BLOCK, 128)) >= HEAD_DIM) if half else (_lane((BLOCK, 128)) < HEAD_DIM)
    swapped = kv_head != half
    kx, vx = (kks, vvs) if swapped else (kk, vv)
    qm = jnp.where(mine, qrot[:, 128 * (j // 2):128 * (j // 2 + 1)], 0.0).astype(BF16)
    sc = jnp.where(valid, _dot(qm, kx, NT) * (HEAD_DIM ** -0.5), NEG)
    sink = sinks_ref[j]
    mx = jnp.maximum(jnp.max(sc, axis=-1, keepdims=True), sink)
    e = jnp.exp(sc - mx)
    es = jnp.exp(sink - mx)
    den = jnp.sum(e, axis=-1, keepdims=True) + es
    return e / den, es / den, qm, kx, vx, mine, swapped


def _mixer_fwd(z, cos_t, sin_t, vg, wc, bfull, bd, scale, sinks, *, name):
    S = z.shape[0]
    nb = S // BLOCK

    def body(zc_ref, zpp_ref, zpk_ref, cc_ref, cp_ref, sc_ref, sp_ref, vg_ref, wc_ref, bf_ref, bd_ref, scale_ref, sinks_ref, out_ref):
        n = pl.program_id(0)
        keep = (n > 0).astype(F32)
        a, _ = _gating_fwd(zc_ref[:, GM0:POOL0], vg_ref[...], wc_ref, bf_ref[...])
        out_ref[:, 0:256] = a.astype(BF16)
        b, _ = _pool_fwd(zc_ref[:, POOL0:Q0], zpp_ref[...] * keep, n, bd_ref[...], scale_ref[...])
        out_ref[:, 256:512] = b.astype(BF16)
        cos_c, sin_c = cc_ref[...], sc_ref[...]
        qrot = _rope(zc_ref[:, Q0:KV0], cos_c, sin_c)
        kk = jnp.concatenate([_rope(zpk_ref[:, 0:128], cp_ref[...], sp_ref[...]), _rope(zc_ref[:, KV0:KV0 + 128], cos_c, sin_c)], axis=0)
        vv = jnp.concatenate([zpk_ref[:, 128:256], zc_ref[:, KV0 + 128:Z_END]], axis=0)
        kks, vvs = pltpu.roll(kk, HEAD_DIM, 1).astype(BF16), pltpu.roll(vv, HEAD_DIM, 1).astype(BF16)
        kk, vv = kk.astype(BF16), vv.astype(BF16)
        valid = _attn_mask(jnp.where(n > 0, 0, BLOCK))
        low = _low_half((BLOCK, 128))
        for s in range(ATT_Q_HEADS // 2):
            o = []
            for j in (2 * s, 2 * s + 1):
                p, _, _, _, vx, _, _ = _attn_head(j, qrot, kk, kks, vv, vvs, sinks_ref, valid)
                o.append(_dot(p, vx, NN))
            out_ref[:, 512 + 128 * s:512 + 128 * (s + 1)] = jnp.where(low, o[0], o[1]).astype(BF16)

    cur = lambda n: (n, 0)
    prev = lambda n: (jnp.maximum(n - 1, 0), 0)
    in_specs = [
        pl.BlockSpec((BLOCK, Z_END), cur),
        pl.BlockSpec((BLOCK, 256), lambda n: (jnp.maximum(n - 1, 0), POOL0 // 256)),
        pl.BlockSpec((BLOCK, 256), lambda n: (jnp.maximum(n - 1, 0), KV0 // 256)),
        pl.BlockSpec((BLOCK, 128), cur), pl.BlockSpec((BLOCK, 128), prev),
        pl.BlockSpec((BLOCK, 128), cur), pl.BlockSpec((BLOCK, 128), prev),
        _full((1, 256)), _full((4, BLOCK, BLOCK)), _full((BLOCK, 256)), _full((256, 256)), _full((1, 256)),
        pl.BlockSpec(memory_space=pltpu.SMEM),
    ]
    return pl.pallas_call(
        body, name=name, grid=(nb,), in_specs=in_specs, out_specs=pl.BlockSpec((BLOCK, 1024), cur),
        out_shape=jax.ShapeDtypeStruct((S, 1024), BF16),
        compiler_params=_params("parallel"))(z, z, z, cos_t, cos_t, sin_t, sin_t, vg, wc, bfull, bd, scale, sinks)


def _mixer_bwd(z, dcat, cos_t, sin_t, vg, wc, bfull, bd, scale, sinks, *, name):
    S = z.shape[0]
    nb = S // BLOCK

    def body(zc_ref, zpp_ref, zpk_ref, dc_ref, cc_ref, cp_ref, sc_ref, sp_ref, vg_ref, wc_ref, bf_ref, bd_ref, scale_ref, sinks_ref,
             dz_ref, dvg_ref, dwc_ref, dbf_ref, dbd_ref, dscale_ref, dsink_ref, carry_ref):
        n = pl.program_id(0)

        @pl.when(n == 0)
        def _():
            for ref in (dvg_ref, dwc_ref, dbf_ref, dbd_ref, dscale_ref, dsink_ref):
                ref[...] = jnp.zeros_like(ref)

        @pl.when(n < nb)
        def _():
            keep = (n > 0).astype(F32)
            low = _low_half((BLOCK, 128))
            zg = zc_ref[:, GM0:POOL0]
            vg_v = vg_ref[...]
            _, (th, u, r, vhat, vn, mixed) = _gating_fwd(zg, vg_v, wc_ref, bf_ref[...])
            da = dc_ref[:, 0:256].astype(F32)
            dmixed = da * u
            dbf_ref[...] += dmixed
            dvn = []
            for blk in range(2):
                dmb = dmixed[:, 128 * blk:128 * (blk + 1)]
                vb = vn[:, 128 * blk:128 * (blk + 1)]
                acc = None
                for half in range(2):
                    g = 2 * blk + half
                    dmg = jnp.where(low if half == 0 else ~low, dmb, 0.0)
                    dwc_ref[g] += _tril(_dot(dmg, vb, NT))
                    part = _dot(_tril(wc_ref[g]), dmg, TN)
                    acc = part if acc is None else acc + part
                dvn.append(acc)
            dvn = jnp.concatenate(dvn, axis=1)
            dvg_ref[...] += jnp.sum(dvn * vhat, axis=0, keepdims=True)
            dvhat = dvn * vg_v
            dv = r * (dvhat - vhat * (_group_sum(dvhat * vhat) * (1.0 / HEAD_DIM)))
            dge = jnp.concatenate([da * mixed, dv], axis=1)
            own_g = dge * _gelu_grad(zg, th)
            p = zc_ref[:, POOL0:Q0]
            bd_v, scale_v = bd_ref[...], scale_ref[...]
            _, (inv, pooled, mapped) = _pool_fwd(p, zpp_ref[...] * keep, n, bd_v, scale_v)
            db = dc_ref[:, 256:512].astype(F32)
            dscale_ref[...] += jnp.sum(db * mapped, axis=0, keepdims=True)
            dmapped = db * scale_v
            dbd_ref[...] += _dot(pooled, dmapped, TN)
            dpooled = _dot(dmapped, bd_v, NT)
            dps = dpooled * inv
            back = []
            for blk in range(2):
                dpb = dps[:, 128 * blk:128 * (blk + 1)]
                t0 = _exact_dot(_band(POOL_WINDOWS[2 * blk], True), dpb)
                t1 = _exact_dot(_band(POOL_WINDOWS[2 * blk + 1], True), dpb)
                back.append(jnp.where(jnp.concatenate([low, low], axis=0), t0, t1))
            back = jnp.concatenate(back, axis=1)
            halo_p = back[:BLOCK] * keep
            own_p = back[BLOCK:] - dpooled
            cos_c, sin_c, cos_p, sin_p = cc_ref[...], sc_ref[...], cp_ref[...], sp_ref[...]
            qrot = _rope(zc_ref[:, Q0:KV0], cos_c, sin_c)
            kk = jnp.concatenate([_rope(zpk_ref[:, 0:128], cos_p, sin_p), _rope(zc_ref[:, KV0:KV0 + 128], cos_c, sin_c)], axis=0)
            vv = jnp.concatenate([zpk_ref[:, 128:256], zc_ref[:, KV0 + 128:Z_END]], axis=0)
            kks, vvs = pltpu.roll(kk, HEAD_DIM, 1).astype(BF16), pltpu.roll(vv, HEAD_DIM, 1).astype(BF16)
            kk, vv = kk.astype(BF16), vv.astype(BF16)
            valid = _attn_mask(jnp.where(n > 0, 0, BLOCK))
            dk = [jnp.zeros((2 * BLOCK, 128), F32), jnp.zeros((2 * BLOCK, 128), F32)]
            dv_acc = [jnp.zeros((2 * BLOCK, 128), F32), jnp.zeros((2 * BLOCK, 128), F32)]
            dq = []
            for s in range(ATT_Q_HEADS // 2):
                dqs = []
                dcs = dc_ref[:, 512 + 128 * s:512 + 128 * (s + 1)]
                for j in (2 * s, 2 * s + 1):
                    pr, ps, qm, kx, vx, mine, swapped = _attn_head(j, qrot, kk, kks, vv, vvs, sinks_ref, valid)
                    dom = jnp.where(mine, dcs, jnp.zeros_like(dcs))
                    dp = _dot(dom, vx, NT)
                    dv_acc[swapped] = dv_acc[swapped] + _dot(pr, dom, TN)
                    drow = jnp.sum(pr * dp, axis=-1, keepdims=True)
                    ds = (pr * (dp - drow) * (HEAD_DIM ** -0.5)).astype(BF16)
                    dsink = jnp.sum(ps * drow, axis=0, keepdims=True)
                    dsink_ref[pl.ds(j, 1), :] = dsink_ref[pl.ds(j, 1), :] - jnp.broadcast_to(dsink, (1, 128))
                    dqs.append(_dot(ds, kx, NN))
                    dk[swapped] = dk[swapped] + _dot(ds, qm, TN)
                dq.append(jnp.where(low, dqs[0], dqs[1]))
            dqrot = jnp.concatenate(dq, axis=1)
            own_q = _rope_bwd(dqrot, cos_c, sin_c)
            dkk = dk[0] + pltpu.roll(dk[1], HEAD_DIM, 1)
            dvv = dv_acc[0] + pltpu.roll(dv_acc[1], HEAD_DIM, 1)
            own_k = _rope_bwd(dkk[BLOCK:], cos_c, sin_c)
            halo_k = _rope_bwd(dkk[:BLOCK], cos_p, sin_p) * keep
            own_v, halo_v = dvv[BLOCK:], dvv[:BLOCK] * keep

            @pl.when(n > 0)
            def _():
                dz_ref[:, GM0:POOL0] = carry_ref[:, GM0:POOL0].astype(BF16)
                dz_ref[:, POOL0:Q0] = (carry_ref[:, POOL0:Q0] + halo_p).astype(BF16)
                dz_ref[:, Q0:KV0] = carry_ref[:, Q0:KV0].astype(BF16)
                dz_ref[:, KV0:KV0 + 128] = (carry_ref[:, KV0:KV0 + 128] + halo_k).astype(BF16)
                dz_ref[:, KV0 + 128:Z_END] = (carry_ref[:, KV0 + 128:Z_END] + halo_v).astype(BF16)

            carry_ref[:, GM0:POOL0] = own_g
            carry_ref[:, POOL0:Q0] = own_p
            carry_ref[:, Q0:KV0] = own_q
            carry_ref[:, KV0:KV0 + 128] = own_k
            carry_ref[:, KV0 + 128:Z_END] = own_v

        @pl.when(n == nb)
        def _():
            dz_ref[...] = carry_ref[...].astype(BF16)

    cur = lambda n: (jnp.minimum(n, nb - 1), 0)
    prev = lambda n: (jnp.maximum(jnp.minimum(n, nb - 1) - 1, 0), 0)
    in_specs = [
        pl.BlockSpec((BLOCK, Z_END), cur),
        pl.BlockSpec((BLOCK, 256), lambda n: (jnp.maximum(jnp.minimum(n, nb - 1) - 1, 0), POOL0 // 256)),
        pl.BlockSpec((BLOCK, 256), lambda n: (jnp.maximum(jnp.minimum(n, nb - 1) - 1, 0), KV0 // 256)),
        pl.BlockSpec((BLOCK, 1024), cur),
        pl.BlockSpec((BLOCK, 128), cur), pl.BlockSpec((BLOCK, 128), prev),
        pl.BlockSpec((BLOCK, 128), cur), pl.BlockSpec((BLOCK, 128), prev),
        _full((1, 256)), _full((4, BLOCK, BLOCK)), _full((BLOCK, 256)), _full((256, 256)), _full((1, 256)),
        pl.BlockSpec(memory_space=pltpu.SMEM),
    ]
    out_specs = [pl.BlockSpec((BLOCK, Z_END), lambda n: (jnp.maximum(n - 1, 0), 0)),
                 _full((1, 256)), _full((4, BLOCK, BLOCK)), _full((BLOCK, 256)), _full((256, 256)), _full((1, 256)), _full((8, 128))]
    out_shape = [jax.ShapeDtypeStruct((S, Z_END), BF16), jax.ShapeDtypeStruct((1, 256), F32),
                 jax.ShapeDtypeStruct((4, BLOCK, BLOCK), F32), jax.ShapeDtypeStruct((BLOCK, 256), F32),
                 jax.ShapeDtypeStruct((256, 256), F32), jax.ShapeDtypeStruct((1, 256), F32), jax.ShapeDtypeStruct((8, 128), F32)]
    return pl.pallas_call(
        body, name=name, grid=(nb + 1,), in_specs=in_specs, out_specs=out_specs, out_shape=out_shape,
        scratch_shapes=[pltpu.VMEM((BLOCK, Z_END), F32)],
        compiler_params=_params("arbitrary"))(z, z, z, dcat, cos_t, cos_t, sin_t, sin_t, vg, wc, bfull, bd, scale, sinks)


HBM = pl.BlockSpec(memory_space=pl.ANY)


def _place():
    return lax.axis_index("x"), lax.axis_index("y"), lax.axis_index("c")


def _chip(k, x, y):
    return (1 - x if k & 1 else x, 1 - y if k & 2 else y)


def _block_rows(ref, rows, dev):
    start = pl.multiple_of((4 * dev[0] + 2 * dev[1] + dev[2]) * rows, 8)
    return ref.at[pl.ds(start, rows), :]


def _group_size(arrays, group):
    return sum(arrays[w].shape[0] for w in group)


def _all_gather(shards, groups, carrier, *, name):
    n = len(shards)
    assert all(_group_size(shards, g) == carrier.shape[0] for g in groups) and carrier.dtype == shards[0].dtype

    def body(*refs):
        ins, car, outs = refs[:n], refs[n], refs[n + 1:2 * n + 1]
        send_sems, recv_sems, local_sems = refs[2 * n + 1:]
        x, y, c = _place()
        me, sibling = (x, y, c), (x, y, 1 - c)
        chips = [_chip(k, x, y) for k in (1, 2, 3)]

        def copy(gi, w, k, block, to, src=None):
            dst = _block_rows(outs[w], ins[w].shape[0], block)
            return pltpu.make_async_remote_copy(src_ref=dst if src is None else src, dst_ref=dst, send_sem=send_sems.at[gi, k],
                                                recv_sem=recv_sems.at[gi, k], device_id=to, device_id_type=MESH)

        def drained(gi, k):
            return pltpu.make_async_remote_copy(src_ref=car, dst_ref=car, send_sem=send_sems.at[gi, k],
                                                recv_sem=recv_sems.at[gi, k], device_id=me, device_id_type=MESH)

        for gi, group in enumerate(groups):
            for w in group:
                pltpu.make_async_copy(ins[w], _block_rows(outs[w], ins[w].shape[0], me), local_sems.at[gi]).start()
                copy(gi, w, 0, me, sibling, src=ins[w]).start()
                for j, chip in enumerate(chips):
                    copy(gi, w, 1 + j, me, (*chip, c), src=ins[w]).start()
        for gi, group in enumerate(groups):
            for j, chip in enumerate(chips):
                drained(gi, 1 + j).wait_recv()
                for w in group:
                    copy(gi, w, 4 + j, (*chip, c), sibling).start()
        for gi in range(len(groups)):
            drained(gi, 0).wait_recv()
            for j in range(3):
                drained(gi, 4 + j).wait_recv()
        for gi in range(len(groups)):
            for k in range(7):
                drained(gi, k).wait_send()
            pltpu.make_async_copy(car, car, local_sems.at[gi]).wait()

    g = len(groups)
    return pl.pallas_call(
        body, name=name, in_specs=[HBM] * (n + 1), out_specs=[HBM] * n,
        out_shape=[jax.ShapeDtypeStruct((N_DEV * s.shape[0], s.shape[1]), s.dtype) for s in shards],
        scratch_shapes=[pltpu.SemaphoreType.DMA((g, 7)), pltpu.SemaphoreType.DMA((g, 7)), pltpu.SemaphoreType.DMA((g,))],
    )(*shards, carrier)


def _scatter_in_chip(grads, groups, carrier, *, name):
    n = len(grads)
    assert all(_group_size(grads, g) == 8 * carrier.shape[0] for g in groups)

    def body(*refs):
        ins, car, own, got = refs[:n], refs[n], refs[n + 1:2 * n + 1], refs[2 * n + 1:3 * n + 1]
        send_sems, recv_sems, local_sems = refs[3 * n + 1:]
        x, y, c = _place()
        sibling = (x, y, 1 - c)
        for gi, group in enumerate(groups):
            for k in range(4):
                chip = _chip(k, x, y)
                for w in group:
                    r = ins[w].shape[0] // N_DEV
                    pltpu.make_async_copy(_block_rows(ins[w], r, (*chip, c)), own[w].at[k], local_sems.at[gi, k]).start()
                    pltpu.make_async_remote_copy(src_ref=_block_rows(ins[w], r, (*chip, 1 - c)), dst_ref=got[w].at[k],
                                                 send_sem=send_sems.at[gi, k], recv_sem=recv_sems.at[gi, k],
                                                 device_id=sibling, device_id_type=MESH).start()
        for gi in range(len(groups)):
            for k in range(4):
                pltpu.make_async_copy(car, car, local_sems.at[gi, k]).wait()
                pltpu.make_async_remote_copy(src_ref=car, dst_ref=car, send_sem=send_sems.at[gi, k], recv_sem=recv_sems.at[gi, k],
                                             device_id=sibling, device_id_type=MESH).wait()

    g = len(groups)
    slots = [jax.ShapeDtypeStruct((4, a.shape[0] // N_DEV, a.shape[1]), a.dtype) for a in grads]
    return pl.pallas_call(
        body, name=name, in_specs=[HBM] * (n + 1), out_specs=[HBM] * (2 * n), out_shape=slots + slots,
        scratch_shapes=[pltpu.SemaphoreType.DMA((g, 4))] * 3,
    )(*grads, carrier)


def _chip_sum(own, got, *, name):
    _, r, cdim = own.shape
    tr = r if r <= 256 else r // (2 if r % 32 == 0 and r // 2 <= 256 else 4)

    def body(a_ref, b_ref, mine_ref, away_ref):
        mine_ref[...] = a_ref[0] + b_ref[0]
        for k in range(1, 4):
            away_ref[k - 1] = (a_ref[k] + b_ref[k]).astype(BF16)

    blk = pl.BlockSpec((4, tr, cdim), lambda i: (0, i, 0))
    return pl.pallas_call(
        body, name=name, grid=(r // tr,), in_specs=[blk, blk],
        out_specs=[pl.BlockSpec((tr, cdim), lambda i: (i, 0)), pl.BlockSpec((3, tr, cdim), lambda i: (0, i, 0))],
        out_shape=[jax.ShapeDtypeStruct((r, cdim), F32), jax.ShapeDtypeStruct((3, r, cdim), BF16)],
        compiler_params=_params("parallel"))(own, got)


def _exchange_chips(parts, groups, carrier, *, name):
    n = len(parts)
    assert all(sum(parts[w].shape[1] for w in g) == carrier.shape[0] for g in groups)

    def body(*refs):
        ins, car, outs = refs[:n], refs[n], refs[n + 1:2 * n + 1]
        send_sems, recv_sems = refs[2 * n + 1:]
        x, y, c = _place()
        for gi, group in enumerate(groups):
            for k in range(1, 4):
                for w in group:
                    pltpu.make_async_remote_copy(src_ref=ins[w].at[k - 1], dst_ref=outs[w].at[k - 1], send_sem=send_sems.at[gi, k - 1],
                                                 recv_sem=recv_sems.at[gi, k - 1], device_id=(*_chip(k, x, y), c), device_id_type=MESH).start()
        for gi in range(len(groups)):
            for k in range(3):
                pltpu.make_async_remote_copy(src_ref=car, dst_ref=car, send_sem=send_sems.at[gi, k], recv_sem=recv_sems.at[gi, k],
                                             device_id=(x, y, c), device_id_type=MESH).wait()

    g = len(groups)
    return pl.pallas_call(
        body, name=name, in_specs=[HBM] * (n + 1), out_specs=[HBM] * n,
        out_shape=[jax.ShapeDtypeStruct(p.shape, p.dtype) for p in parts],
        scratch_shapes=[pltpu.SemaphoreType.DMA((g, 3))] * 2,
    )(*parts, carrier)


def _final_sum(mine, got, *, name):
    r, cdim = mine.shape
    tr = r if r <= 256 else r // (2 if r % 32 == 0 and r // 2 <= 256 else 4)

    def body(m_ref, g_ref, o_ref):
        o_ref[...] = ((m_ref[...] + g_ref[0].astype(F32)) + g_ref[1].astype(F32)) + g_ref[2].astype(F32)

    return pl.pallas_call(
        body, name=name, grid=(r // tr,),
        in_specs=[pl.BlockSpec((tr, cdim), lambda i: (i, 0)), pl.BlockSpec((3, tr, cdim), lambda i: (0, i, 0))],
        out_specs=pl.BlockSpec((tr, cdim), lambda i: (i, 0)), out_shape=jax.ShapeDtypeStruct((r, cdim), F32),
        compiler_params=_params("parallel"))(mine, got)


def _adamw_math(w, g, m, v):
    m = ADAM_B1 * m + (1.0 - ADAM_B1) * g
    v = ADAM_B2 * v + (1.0 - ADAM_B2) * (g * g)
    m_hat = m / (1.0 - ADAM_B1 ** ADAM_STEP)
    v_hat = v / (1.0 - ADAM_B2 ** ADAM_STEP)
    return -ADAM_LR * (m_hat / (jnp.sqrt(v_hat) + ADAM_EPS) + ADAM_WD * w), m, v


def _adamw(w, g, m, v, *, name):
    L, r, cdim = w.shape
    tr = r if r <= 512 else 512

    def body(w_ref, g_ref, m_ref, v_ref, d_ref, nm_ref, nv_ref):
        d_ref[...], nm_ref[...], nv_ref[...] = _adamw_math(w_ref[...], g_ref[...], m_ref[...], v_ref[...])

    blk = pl.BlockSpec((None, tr, cdim), lambda l, i: (l, i, 0))
    return pl.pallas_call(body, name=name, grid=(L, r // tr), in_specs=[blk] * 4, out_specs=[blk] * 3,
                          out_shape=[jax.ShapeDtypeStruct(w.shape, F32)] * 3, compiler_params=_params("parallel", "parallel"))(w, g, m, v)


def _small_update(parts, w, m, v, *, name):
    _, p, lanes = parts.shape
    tp = p // 2 if p % 16 == 0 else p

    def body(p_ref, w_ref, m_ref, v_ref, g_ref, d_ref, nm_ref, nv_ref):
        g = p_ref[0]
        for i in range(1, N_DEV):
            g = g + p_ref[i]
        g_ref[...] = g
        d_ref[...], nm_ref[...], nv_ref[...] = _adamw_math(w_ref[...], g, m_ref[...], v_ref[...])

    blk = pl.BlockSpec((tp, lanes), lambda i: (i, 0))
    return pl.pallas_call(body, name=name, grid=(p // tp,), in_specs=[pl.BlockSpec((N_DEV, tp, lanes), lambda i: (0, i, 0)), blk, blk, blk],
                          out_specs=[blk] * 4, out_shape=[jax.ShapeDtypeStruct((p, lanes), F32)] * 4,
                          compiler_params=_params("parallel"))(parts, w, m, v)


BIG = ("w_in", "w_o", "w_xq", "w_xkv", "w_xo", "w_gate_up", "w_down")
TRANSPOSED = ("w_in", "w_xkv", "w_gate_up")
SMALL = ("mem_norm_g", "mix_pre_g", "mix_post_g", "gm_v_g", "gm_w_s", "gm_b_s", "pool_w", "pool_scale", "attn_sinks",
         "x_pre_g", "x_post_g", "ffn_pre_g", "ffn_post_g")
WEIGHTS = ("mem_norm_g", "mix_pre_g", "mix_post_g", "w_in", "gm_v_g", "gm_w_s", "gm_b_s", "pool_w", "pool_scale", "attn_sinks", "w_o",
           "x_pre_g", "x_post_g", "w_xq", "w_xkv", "w_xo", "ffn_pre_g", "ffn_post_g", "w_gate_up", "w_down")
PACK_QUANTUM = 8 * 128


def _pack(arrays):
    flat = []
    for a in arrays:
        a = a.reshape(-1).astype(F32)
        flat.append(jnp.pad(a, (0, -a.size % PACK_QUANTUM)))
    return jnp.concatenate(flat).reshape(-1, 128)


def _unpack(pack, shapes):
    out, at = [], 0
    flat = pack.reshape(-1)
    for s in shapes:
        size = math.prod(s)
        out.append(flat[at:at + size].reshape(s))
        at += size + (-size % PACK_QUANTUM)
    return out


def _row(v):
    return v.reshape(1, -1)


def _local_step(x, mem, positions, target, small, big):
    depth = small["mix_pre_g"].shape[0]
    half = HEAD_DIM // 2
    inv = ROPE_THETA ** (-jnp.arange(half, dtype=F32) / half)
    cos_t, sin_t = _rope_tables(positions.reshape(-1, 1), jnp.tile(inv, 128 // half).reshape(1, 128), name="rope_tables")
    mem_g = _row(small["mem_norm_g"])
    saved = []
    for l in range(depth):
        w = {n: big[n][l] for n in BIG}
        bfull = jnp.repeat(small["gm_b_s"][l].T, HEAD_DIM, axis=1)
        bd = jax.scipy.linalg.block_diag(*[small["pool_w"][l, g] for g in range(len(POOL_WINDOWS))])
        mix_par = (_row(small["gm_v_g"][l]), small["gm_w_s"][l], bfull, bd, _row(small["pool_scale"][l]), small["attn_sinks"][l])
        z, h1 = _mm_fwd(x, w["w_in"], nt=True, pre="norm", g_pre=_row(small["mix_pre_g"][l]), out_dtype=F32, name="in_proj")
        cat = _mixer_fwd(z, cos_t, sin_t, *mix_par, name="mixer_fwd")
        mix, x1 = _mm_fwd(cat, w["w_o"], nt=False, post="norm_res", g_post=_row(small["mix_post_g"][l]), xres=x, name="mix_out")
        qx, h2 = _mm_fwd(x1, w["w_xq"], nt=False, pre="norm", g_pre=_row(small["x_pre_g"][l]), name="xq_proj")
        kv, memn = _mm_fwd(mem, w["w_xkv"], nt=True, pre="norm", g_pre=mem_g, name="xkv_proj")
        ox = _xattn_fwd(qx, kv, name="xattn_fwd")
        xo, x2 = _mm_fwd(ox, w["w_xo"], nt=False, post="norm_res", g_post=_row(small["x_post_g"][l]), xres=x1, name="xattn_out")
        gu, h3 = _mm_fwd(x2, w["w_gate_up"], nt=True, pre="norm", g_pre=_row(small["ffn_pre_g"][l]), name="ffn_in")
        f, x3 = _mm_fwd(gu, w["w_down"], nt=False, pre="swiglu", post="norm_res", g_post=_row(small["ffn_post_g"][l]), xres=x2,
                        tile=256, name="ffn_out")
        saved.append((w, mix_par, x, z, h1, cat, mix, x1, qx, h2, kv, memn, ox, xo, x2, gu, h3, f))
        x = x3

    loss, dx = _loss_grad(x, target, name="loss")

    gs = {n: [None] * depth for n in SMALL if n != "mem_norm_g"}
    gb = {n: [None] * depth for n in BIG}
    dmemn = [None] * depth
    for l in reversed(range(depth)):
        w, mix_par, x0, z, h1, cat, mix, x1, qx, h2, kv, memn, ox, xo, x2, gu, h3, f = saved[l]
        dact, df, gs["ffn_post_g"][l] = _mm_bwd_pre(f, dx, _row(small["ffn_post_g"][l]), w["w_down"], name="ffn_out_bwd")
        gb["w_down"][l] = _wgrad(gu, df, swiglu=True, name="w_down_grad")
        dgu = _swiglu_bwd(dact, gu, name="swiglu_bwd")
        gb["w_gate_up"][l] = _wgrad(dgu, h3, name="w_gate_up_grad")
        dx2, gs["ffn_pre_g"][l] = _mm_bwd_post(dgu, w["w_gate_up"], x2, _row(small["ffn_pre_g"][l]), dx, nt=False, tile=256, name="ffn_in_bwd")
        do, dxo, gs["x_post_g"][l] = _mm_bwd_pre(xo, dx2, _row(small["x_post_g"][l]), w["w_xo"], name="xattn_out_bwd")
        gb["w_xo"][l] = _wgrad(ox, dxo, name="w_xo_grad")
        dqx, dkv = _xattn_bwd(qx, kv, do, name="xattn_bwd")
        gb["w_xq"][l] = _wgrad(h2, dqx, name="w_xq_grad")
        dx1, gs["x_pre_g"][l] = _mm_bwd_post(dqx, w["w_xq"], x1, _row(small["x_pre_g"][l]), dx2, nt=True, name="xq_proj_bwd")
        gb["w_xkv"][l] = _wgrad(dkv, memn, name="w_xkv_grad")
        dmemn[l] = _mm_fwd(dkv, w["w_xkv"], nt=False, out_dtype=F32, name="xkv_proj_bwd")[0]
        dcat, dmix, gs["mix_post_g"][l] = _mm_bwd_pre(mix, dx1, _row(small["mix_post_g"][l]), w["w_o"], name="mix_out_bwd")
        gb["w_o"][l] = _wgrad(cat, dmix, name="w_o_grad")
        dz, dvg, dwc, dbf, dbd, dscale, dsink = _mixer_bwd(z, dcat, cos_t, sin_t, *mix_par, name="mixer_bwd")
        gs["gm_v_g"][l], gs["gm_w_s"][l], gs["pool_scale"][l], gs["attn_sinks"][l] = dvg, dwc, dscale, dsink[:, 0]
        gs["gm_b_s"][l] = dbf.reshape(BLOCK, -1, HEAD_DIM).sum(-1).T
        gs["pool_w"][l] = jnp.stack([dbd[HEAD_DIM * g:HEAD_DIM * (g + 1), HEAD_DIM * g:HEAD_DIM * (g + 1)] for g in range(len(POOL_WINDOWS))])
        gb["w_in"][l] = _wgrad(dz, h1, name="w_in_grad")
        dx, gs["mix_pre_g"][l] = _mm_bwd_post(dz, w["w_in"], x0, _row(small["mix_pre_g"][l]), dx1, nt=False, name="in_proj_bwd")

    small_grads = {n: jnp.stack(v).reshape(small[n].shape) for n, v in gs.items()}
    small_grads["mem_norm_g"] = _norm_dgain(mem, jnp.stack(dmemn), name="mem_norm_grad").reshape(-1)
    return loss, dx, small_grads, gb


def _step(a):
    depth = a["mix_pre_g"].shape[0]
    x, mem, target = a["x"][0], a["mem"][0], a["loss_target"][0]

    shards = [(a[n][l].T if n in TRANSPOSED else a[n][l]).astype(BF16) for l in range(depth) for n in BIG]
    groups = [list(range(l * len(BIG), (l + 1) * len(BIG))) for l in range(depth)]
    group_rows = _group_size(shards, groups[0])
    d_model = shards[0].shape[1]
    whole = _all_gather(shards, groups, jnp.zeros((group_rows, d_model), BF16), name="gather_weights")
    big = {n: [whole[l * len(BIG) + i] for l in range(depth)] for i, n in enumerate(BIG)}

    loss, dx, small_grads, gb = _local_step(x, mem, a["positions"][0], target, {n: a[n] for n in SMALL}, big)

    grads = [gb[n][l] for l in range(depth) for n in BIG]
    own_got = _scatter_in_chip(grads, groups, jnp.zeros((group_rows, d_model), F32), name="scatter_in_chip")
    sums = [_chip_sum(own_got[i], own_got[len(grads) + i], name="chip_sum") for i in range(len(grads))]
    away = _exchange_chips([s[1] for s in sums], groups, jnp.zeros((group_rows, d_model), BF16), name="exchange_chips")
    mine = [_final_sum(sums[i][0], away[i], name="final_sum") for i in range(len(grads))]

    out = {}
    for i, n in enumerate(BIG):
        g = jnp.stack([mine[l * len(BIG) + i].T if n in TRANSPOSED else mine[l * len(BIG) + i] for l in range(depth)])
        out["grad_" + n] = g
        out["delta_" + n], out["new_m_" + n], out["new_v_" + n] = _adamw(a[n], g, a["m_" + n], a["v_" + n], name="adamw_" + n)

    part = _pack([small_grads[n] for n in SMALL])
    parts = _all_gather([part], [[0]], part, name="gather_small_grads")[0].reshape(N_DEV, *part.shape)
    packs = _small_update(parts, *[_pack([a[p + n] for n in SMALL]) for p in ("", "m_", "v_")], name="small_update")
    for p, pack in zip(("grad_", "delta_", "new_m_", "new_v_"), packs):
        for n, v in zip(SMALL, _unpack(pack, [a[n].shape for n in SMALL])):
            out[p + n] = v

    total = lax.psum(loss[0, 0], ("x", "y", "c"))
    return (total, dx[None], *[out[p + n] for p in ("grad_", "delta_", "new_m_", "new_v_") for n in WEIGHTS])


def kernel(x, mem, positions, mem_norm_g, mix_pre_g, mix_post_g, w_in, gm_v_g, gm_w_s, gm_b_s, pool_w, pool_scale, attn_sinks,
           w_o, x_pre_g, x_post_g, w_xq, w_xkv, w_xo, ffn_pre_g, ffn_post_g, w_gate_up, w_down, loss_target, m_mem_norm_g,
           m_mix_pre_g, m_mix_post_g, m_w_in, m_gm_v_g, m_gm_w_s, m_gm_b_s, m_pool_w, m_pool_scale, m_attn_sinks, m_w_o,
           m_x_pre_g, m_x_post_g, m_w_xq, m_w_xkv, m_w_xo, m_ffn_pre_g, m_ffn_post_g, m_w_gate_up, m_w_down, v_mem_norm_g,
           v_mix_pre_g, v_mix_post_g, v_w_in, v_gm_v_g, v_gm_w_s, v_gm_b_s, v_pool_w, v_pool_scale, v_attn_sinks, v_w_o,
           v_x_pre_g, v_x_post_g, v_w_xq, v_w_xkv, v_w_xo, v_ffn_pre_g, v_ffn_post_g, v_w_gate_up, v_w_down):
    return _step(dict(locals()))
```

```python
import functools
import math

import jax
import jax.numpy as jnp
from jax import lax
from jax.experimental import pallas as pl
from jax.experimental.pallas import tpu as pltpu

F32, BF16 = jnp.float32, jnp.bfloat16
EPS = 1e-6
HEAD_DIM = 64
BLOCK = 128
POOL_WINDOWS = (2, 4, 8, 16)
ATT_Q_HEADS = 8
X_HEADS = 4
ROPE_THETA = 10000.0
ADAM_LR, ADAM_B1, ADAM_B2, ADAM_EPS, ADAM_WD, ADAM_STEP = 0.001, 0.9, 0.999, 1e-08, 0.01, 10
N_DEV = 8
TOKEN_TILE = 512
VMEM_LIMIT_BYTES = 50 * 2**20
NEG = -1e30
MESH = pl.DeviceIdType.MESH

NN = ((1,), (0,))
NT = ((1,), (1,))
TN = ((0,), (0,))


def _dot(a, b, dims):
    return lax.dot_general(a.astype(BF16), b.astype(BF16), (dims, ((), ())), preferred_element_type=F32)


def _params(*sem):
    return pltpu.CompilerParams(dimension_semantics=sem, vmem_limit_bytes=VMEM_LIMIT_BYTES)


def _full(shape):
    return pl.BlockSpec(shape, lambda *_: (0,) * len(shape))


def _lane(shape):
    return lax.broadcasted_iota(jnp.int32, shape, len(shape) - 1)


def _rms_fwd(x, g):
    r = lax.rsqrt(jnp.mean(x * x, axis=-1, keepdims=True) + EPS)
    return x * r * g


def _rms_bwd(x, g, dy):
    r = lax.rsqrt(jnp.mean(x * x, axis=-1, keepdims=True) + EPS)
    xh = x * r
    dg = jnp.sum(dy * xh, axis=0, keepdims=True)
    dxh = dy * g
    dx = r * (dxh - xh * jnp.mean(dxh * xh, axis=-1, keepdims=True))
    return dx, dg


def _silu_parts(gate):
    sg = 1.0 / (1.0 + jnp.exp(-gate))
    return gate * sg, sg


def _swiglu_cols(gu_ref, out_ref, width):
    step = 128 * max(1, (width // 128) // 4)
    for c0 in range(0, width, step):
        c1 = min(width, c0 + step)
        gate = gu_ref[:, c0:c1].astype(F32)
        up = gu_ref[:, width + c0:width + c1].astype(F32)
        out_ref[:, c0:c1] = (_silu_parts(gate)[0] * up).astype(out_ref.dtype)


def _col_tile(n, cap=1408):
    best = n
    for t in range(128, cap + 1, 128):
        if n % t == 0:
            best = t
    return best if n > cap else n


def _mm_fwd(a, w, *, nt, name, pre=None, g_pre=None, post=None, g_post=None, xres=None, out_dtype=BF16, tile=TOKEN_TILE):
    S = a.shape[0]
    n_out, k = (w.shape[0], w.shape[1]) if nt else (w.shape[1], w.shape[0])
    T = min(tile, S)
    tn = n_out if post else _col_tile(n_out)
    grid = (S // T, n_out // tn)

    def body(*refs):
        it = iter(refs)
        a_ref, w_ref = next(it), next(it)
        gpre_ref = next(it) if pre == "norm" else None
        gpost_ref, x_ref = (next(it), next(it)) if post else (None, None)
        outs = [next(it) for _ in range((2 if post else 1) + (1 if pre == "norm" else 0))]
        as_ref = next(it) if pre else None
        if pre:
            @pl.when(pl.program_id(1) == 0)
            def _():
                if pre == "norm":
                    h = _rms_fwd(a_ref[...], gpre_ref[...]).astype(BF16)
                    outs[-1][...] = h
                    as_ref[...] = h
                else:
                    _swiglu_cols(a_ref, as_ref, k)
            av = as_ref[...]
        else:
            av = a_ref[...]
        acc = _dot(av, w_ref[...], NT if nt else NN)
        if post:
            outs[0][...] = acc
            outs[1][...] = x_ref[...] + _rms_fwd(acc, gpost_ref[...])
        else:
            outs[0][...] = acc.astype(out_dtype)

    in_specs = [pl.BlockSpec((T, a.shape[1]), lambda i, j: (i, 0)),
                pl.BlockSpec((tn, k), lambda i, j: (j, 0)) if nt else pl.BlockSpec((k, tn), lambda i, j: (0, j))]
    args = [a, w]
    if pre == "norm":
        in_specs.append(_full((1, k)))
        args.append(g_pre)
    if post:
        in_specs += [_full((1, n_out)), pl.BlockSpec((T, n_out), lambda i, j: (i, 0))]
        args += [g_post, xres]
    out_block = pl.BlockSpec((T, tn), lambda i, j: (i, j))
    if post:
        out_shape = [jax.ShapeDtypeStruct((S, n_out), F32)] * 2
        out_specs = [out_block, out_block]
    else:
        out_shape = [jax.ShapeDtypeStruct((S, n_out), out_dtype)]
        out_specs = [out_block]
    if pre == "norm":
        out_shape.append(jax.ShapeDtypeStruct((S, k), BF16))
        out_specs.append(pl.BlockSpec((T, k), lambda i, j: (i, 0)))
    scratch = [pltpu.VMEM((T, k), BF16)] if pre else []
    return pl.pallas_call(body, name=name, grid=grid, in_specs=in_specs, out_specs=out_specs, out_shape=out_shape,
                          scratch_shapes=scratch, compiler_params=_params("parallel", "arbitrary"))(*args)


def _mm_bwd_pre(m, dx, g, w, *, name, tile=TOKEN_TILE):
    S, n = m.shape
    ku = w.shape[0]
    T = min(tile, S)
    tn = _col_tile(ku)
    grid = (S // T, ku // tn)

    def body(m_ref, dx_ref, g_ref, w_ref, du_ref, dm_ref, dg_ref, as_ref):
        i, j = pl.program_id(0), pl.program_id(1)

        @pl.when((i == 0) & (j == 0))
        def _():
            dg_ref[...] = jnp.zeros_like(dg_ref)

        @pl.when(j == 0)
        def _():
            dmv, dg = _rms_bwd(m_ref[...], g_ref[...], dx_ref[...])
            dg_ref[...] += dg
            dmb = dmv.astype(BF16)
            dm_ref[...] = dmb
            as_ref[...] = dmb

        du_ref[...] = _dot(as_ref[...], w_ref[...], NT).astype(BF16)

    row = pl.BlockSpec((T, n), lambda i, j: (i, 0))
    return pl.pallas_call(
        body, name=name, grid=grid,
        in_specs=[row, row, _full((1, n)), pl.BlockSpec((tn, n), lambda i, j: (j, 0))],
        out_specs=[pl.BlockSpec((T, tn), lambda i, j: (i, j)), row, _full((1, n))],
        out_shape=[jax.ShapeDtypeStruct((S, ku), BF16), jax.ShapeDtypeStruct((S, n), BF16), jax.ShapeDtypeStruct((1, n), F32)],
        scratch_shapes=[pltpu.VMEM((T, n), BF16)], compiler_params=_params("arbitrary", "arbitrary"))(m, dx, g, w)


def _mm_bwd_post(dy, w, x, g, dx_in, *, nt, name, tile=TOKEN_TILE):
    S, k = dy.shape
    n = x.shape[1]
    T = min(tile, S)

    def body(dy_ref, w_ref, x_ref, g_ref, dxi_ref, dxo_ref, dg_ref):
        @pl.when(pl.program_id(0) == 0)
        def _():
            dg_ref[...] = jnp.zeros_like(dg_ref)

        dh = _dot(dy_ref[...], w_ref[...], NT if nt else NN)
        dxn, dg = _rms_bwd(x_ref[...], g_ref[...], dh)
        dg_ref[...] += dg
        dxo_ref[...] = dxi_ref[...] + dxn

    row = pl.BlockSpec((T, n), lambda i: (i, 0))
    return pl.pallas_call(
        body, name=name, grid=(S // T,),
        in_specs=[pl.BlockSpec((T, k), lambda i: (i, 0)), _full(w.shape), row, _full((1, n)), row],
        out_specs=[row, _full((1, n))],
        out_shape=[jax.ShapeDtypeStruct((S, n), F32), jax.ShapeDtypeStruct((1, n), F32)],
        compiler_params=_params("arbitrary"))(dy, w, x, g, dx_in)


def _wgrad(a, b, *, name, swiglu=False, tile=TOKEN_TILE):
    S = a.shape[0]
    k = a.shape[1] // 2 if swiglu else a.shape[1]
    n = b.shape[1]
    T = min(tile, S)
    tk = 512 if k % 512 == 0 else 256
    nk = k // tk

    def body(*refs):
        if swiglu:
            gate_ref, up_ref, b_ref, o_ref, act_ref = refs
            for r0 in range(0, S, T):
                gate, up = gate_ref[r0:r0 + T, :].astype(F32), up_ref[r0:r0 + T, :].astype(F32)
                act_ref[r0:r0 + T, :] = (_silu_parts(gate)[0] * up).astype(BF16)
            av = act_ref[...]
        else:
            a_ref, b_ref, o_ref = refs
            av = a_ref[...]
        o_ref[...] = _dot(av, b_ref[...], TN)

    in_specs = [pl.BlockSpec((S, tk), lambda kk: (0, kk))]
    args = [a]
    if swiglu:
        in_specs.append(pl.BlockSpec((S, tk), lambda kk: (0, kk + nk)))
        args.append(a)
    in_specs.append(_full((S, n)))
    args.append(b)
    return pl.pallas_call(body, name=name, grid=(nk,), in_specs=in_specs,
                          out_specs=pl.BlockSpec((tk, n), lambda kk: (kk, 0)),
                          out_shape=jax.ShapeDtypeStruct((k, n), F32),
                          scratch_shapes=[pltpu.VMEM((S, tk), BF16)] if swiglu else [],
                          compiler_params=_params("parallel"))(*args)


def _swiglu_bwd(dact, gu, *, name, tile=256):
    S, f = dact.shape
    T = min(tile, S)

    def body(da_ref, gu_ref, o_ref):
        step = 128 * max(1, (f // 128) // 4)
        for c0 in range(0, f, step):
            c1 = min(f, c0 + step)
            gate = gu_ref[:, c0:c1].astype(F32)
            up = gu_ref[:, f + c0:f + c1].astype(F32)
            da = da_ref[:, c0:c1].astype(F32)
            sl, sg = _silu_parts(gate)
            o_ref[:, c0:c1] = (da * up * (sg + sl * (1.0 - sg))).astype(BF16)
            o_ref[:, f + c0:f + c1] = (da * sl).astype(BF16)

    return pl.pallas_call(
        body, name=name, grid=(S // T,),
        in_specs=[pl.BlockSpec((T, f), lambda i: (i, 0)), pl.BlockSpec((T, 2 * f), lambda i: (i, 0))],
        out_specs=pl.BlockSpec((T, 2 * f), lambda i: (i, 0)),
        out_shape=jax.ShapeDtypeStruct((S, 2 * f), BF16), compiler_params=_params("parallel"))(dact, gu)


def _loss_grad(y, target, *, name, tile=TOKEN_TILE):
    S, d = y.shape
    T = min(tile, S)
    steps = S // T

    def body(y_ref, t_ref, loss_ref, dy_ref, acc_ref):
        i = pl.program_id(0)

        @pl.when(i == 0)
        def _():
            acc_ref[...] = jnp.zeros_like(acc_ref)

        diff = y_ref[...] - t_ref[...]
        dy_ref[...] = diff * (1.0 / d)
        acc_ref[...] += jnp.sum(diff * diff, axis=0, keepdims=True)

        @pl.when(i == steps - 1)
        def _():
            total = jnp.sum(acc_ref[...], axis=1, keepdims=True) * (0.5 / d)
            loss_ref[...] = jnp.broadcast_to(total, loss_ref.shape)

    row = pl.BlockSpec((T, d), lambda i: (i, 0))
    return pl.pallas_call(
        body, name=name, grid=(steps,), in_specs=[row, row], out_specs=[_full((1, 128)), row],
        out_shape=[jax.ShapeDtypeStruct((1, 128), F32), jax.ShapeDtypeStruct((S, d), F32)],
        scratch_shapes=[pltpu.VMEM((1, d), F32)], compiler_params=_params("arbitrary"))(y, target)


def _norm_dgain(x, dys, *, name):
    def body(x_ref, dy_ref, o_ref):
        xv = x_ref[...]
        dy = dy_ref[0]
        for l in range(1, dys.shape[0]):
            dy = dy + dy_ref[l]
        r = lax.rsqrt(jnp.mean(xv * xv, axis=-1, keepdims=True) + EPS)
        o_ref[...] = jnp.sum(dy * xv * r, axis=0, keepdims=True)

    return pl.pallas_call(body, name=name, out_shape=jax.ShapeDtypeStruct((1, x.shape[1]), F32),
                          compiler_params=pltpu.CompilerParams(vmem_limit_bytes=VMEM_LIMIT_BYTES))(x, dys)


def _xattn_probs(q_ref, kv_ref, h, d, hd):
    q = q_ref[:, h * hd:(h + 1) * hd]
    kh = kv_ref[:, h * hd:(h + 1) * hd]
    vh = kv_ref[:, d + h * hd:d + (h + 1) * hd]
    s = _dot(q, kh, NT) * (hd ** -0.5)
    e = jnp.exp(s - jnp.max(s, axis=-1, keepdims=True))
    return q, kh, vh, e / jnp.sum(e, axis=-1, keepdims=True)


def _xattn_fwd(q, kv, *, name, tile=TOKEN_TILE):
    S, d = q.shape
    mlen = kv.shape[0]
    hd = d // X_HEADS
    T = min(tile, S)

    def body(q_ref, kv_ref, o_ref):
        for h in range(X_HEADS):
            _, _, vh, p = _xattn_probs(q_ref, kv_ref, h, d, hd)
            o_ref[:, h * hd:(h + 1) * hd] = _dot(p, vh, NN).astype(BF16)

    row = pl.BlockSpec((T, d), lambda i: (i, 0))
    return pl.pallas_call(body, name=name, grid=(S // T,), in_specs=[row, _full((mlen, 2 * d))], out_specs=row,
                          out_shape=jax.ShapeDtypeStruct((S, d), BF16), compiler_params=_params("parallel"))(q, kv)


def _xattn_bwd(q, kv, do, *, name, tile=TOKEN_TILE):
    S, d = q.shape
    mlen = kv.shape[0]
    hd = d // X_HEADS
    T = min(tile, S)

    def body(q_ref, kv_ref, do_ref, dq_ref, dkv_ref):
        @pl.when(pl.program_id(0) == 0)
        def _():
            dkv_ref[...] = jnp.zeros_like(dkv_ref)

        for h in range(X_HEADS):
            qh, kh, vh, p = _xattn_probs(q_ref, kv_ref, h, d, hd)
            doh = do_ref[:, h * hd:(h + 1) * hd]
            dp = _dot(doh, vh, NT)
            ds = p * (dp - jnp.sum(p * dp, axis=-1, keepdims=True)) * (hd ** -0.5)
            dq_ref[:, h * hd:(h + 1) * hd] = _dot(ds, kh, NN).astype(BF16)
            dkv_ref[:, h * hd:(h + 1) * hd] += _dot(ds, qh, TN)
            dkv_ref[:, d + h * hd:d + (h + 1) * hd] += _dot(p, doh, TN)

    row = pl.BlockSpec((T, d), lambda i: (i, 0))
    return pl.pallas_call(
        body, name=name, grid=(S // T,), in_specs=[row, _full((mlen, 2 * d)), row],
        out_specs=[row, _full((mlen, 2 * d))],
        out_shape=[jax.ShapeDtypeStruct((S, d), BF16), jax.ShapeDtypeStruct((mlen, 2 * d), F32)],
        compiler_params=_params("arbitrary"))(q, kv, do)


GM0, POOL0, Q0, KV0, Z_END = 0, 512, 768, 1280, 1536


def _rope_tables(pos_col, inv_row, *, name):
    S = pos_col.shape[0]

    def body(p_ref, inv_ref, cos_ref, sin_ref):
        ang = p_ref[...].astype(F32) * inv_ref[...]
        sin = jnp.sin(ang)
        cos_ref[...] = jnp.cos(ang)
        sin_ref[...] = jnp.where(_lane(ang.shape) % HEAD_DIM < HEAD_DIM // 2, -sin, sin)

    T = min(TOKEN_TILE, S)
    return pl.pallas_call(
        body, name=name, grid=(S // T,),
        in_specs=[pl.BlockSpec((T, 1), lambda i: (i, 0)), _full((1, 128))],
        out_specs=[pl.BlockSpec((T, 128), lambda i: (i, 0))] * 2,
        out_shape=[jax.ShapeDtypeStruct((S, 128), F32)] * 2, compiler_params=_params("parallel"))(pos_col, inv_row)


def _swap_halves(x):
    n = x.shape[-1]
    return jnp.where(_lane(x.shape) % HEAD_DIM < HEAD_DIM // 2, pltpu.roll(x, n - HEAD_DIM // 2, 1), pltpu.roll(x, HEAD_DIM // 2, 1))


def _rope(x, cos, sin_s):
    reps = x.shape[-1] // 128
    if reps > 1:
        cos, sin_s = jnp.tile(cos, (1, reps)), jnp.tile(sin_s, (1, reps))
    return x * cos + _swap_halves(x) * sin_s


def _rope_bwd(dy, cos, sin_s):
    reps = dy.shape[-1] // 128
    if reps > 1:
        cos, sin_s = jnp.tile(cos, (1, reps)), jnp.tile(sin_s, (1, reps))
    return dy * cos + _swap_halves(dy * sin_s)


def _gelu_parts(x):
    c = math.sqrt(2.0 / math.pi)
    t = jnp.tanh(c * (x + 0.044715 * x * x * x))
    return 0.5 * x * (1.0 + t), t


def _gelu_grad(x, t):
    c = math.sqrt(2.0 / math.pi)
    return 0.5 * (1.0 + t) + 0.5 * x * (1.0 - t * t) * c * (1.0 + 3.0 * 0.044715 * x * x)


def _group_sum(x):
    gid = _lane(x.shape) // HEAD_DIM
    out = jnp.zeros_like(x)
    for g in range(x.shape[-1] // HEAD_DIM):
        sel = gid == g
        out = jnp.where(sel, jnp.sum(jnp.where(sel, x, 0.0), axis=-1, keepdims=True), out)
    return out


def _low_half(shape):
    return _lane(shape) % 128 < HEAD_DIM


def _tril(w):
    r = lax.broadcasted_iota(jnp.int32, w.shape, 0)
    c = lax.broadcasted_iota(jnp.int32, w.shape, 1)
    return jnp.where(r >= c, w, 0.0)


def _gating_fwd(zg, vg, wc_ref, bfull):
    ge, th = _gelu_parts(zg)
    u, v = ge[:, :256], ge[:, 256:]
    r = lax.rsqrt(_group_sum(v * v) * (1.0 / HEAD_DIM) + EPS)
    vhat = v * r
    vn = vhat * vg
    low = _low_half((BLOCK, 128))
    blocks = []
    for blk in range(2):
        vb = vn[:, 128 * blk:128 * (blk + 1)]
        m0 = _dot(_tril(wc_ref[2 * blk]), vb, NN)
        m1 = _dot(_tril(wc_ref[2 * blk + 1]), vb, NN)
        blocks.append(jnp.where(low, m0, m1))
    mixed = jnp.concatenate(blocks, axis=1) + bfull
    return u * mixed, (th, u, r, vhat, vn, mixed)


def _band(w, transpose):
    if transpose:
        s2 = lax.broadcasted_iota(jnp.int32, (2 * BLOCK, BLOCK), 0)
        t = lax.broadcasted_iota(jnp.int32, (2 * BLOCK, BLOCK), 1)
    else:
        t = lax.broadcasted_iota(jnp.int32, (BLOCK, 2 * BLOCK), 0)
        s2 = lax.broadcasted_iota(jnp.int32, (BLOCK, 2 * BLOCK), 1)
    d = t - s2 + BLOCK
    return jnp.where((d >= 0) & (d < w), 1.0, 0.0).astype(BF16)


def _exact_dot(b01, x):
    hi = x.astype(BF16)
    lo = (x - hi.astype(F32)).astype(BF16)
    return _dot(b01, hi, NN) + _dot(b01, lo, NN)


def _inv_count(n):
    t = lax.broadcasted_iota(jnp.int32, (BLOCK, 256), 0) + n * BLOCK
    gid = _lane((BLOCK, 256)) // HEAD_DIM
    win = jnp.where(gid == 0, POOL_WINDOWS[0], jnp.where(gid == 1, POOL_WINDOWS[1], jnp.where(gid == 2, POOL_WINDOWS[2], POOL_WINDOWS[3])))
    return 1.0 / jnp.minimum(t + 1, win).astype(F32)


def _pool_fwd(p, pp, n, bd, scale):
    low = _low_half((BLOCK, 128))
    blocks = []
    for blk in range(2):
        both = jnp.concatenate([pp[:, 128 * blk:128 * (blk + 1)], p[:, 128 * blk:128 * (blk + 1)]], axis=0)
        s0 = _exact_dot(_band(POOL_WINDOWS[2 * blk], False), both)
        s1 = _exact_dot(_band(POOL_WINDOWS[2 * blk + 1], False), both)
        blocks.append(jnp.where(low, s0, s1))
    inv = _inv_count(n)
    pooled = jnp.concatenate(blocks, axis=1) * inv - p
    mapped = _dot(pooled, bd, NN)
    return mapped * scale, (inv, pooled, mapped)


def _attn_mask(first_key):
    qi = lax.broadcasted_iota(jnp.int32, (BLOCK, 2 * BLOCK), 0)
    ki = lax.broadcasted_iota(jnp.int32, (BLOCK, 2 * BLOCK), 1)
    return (ki > qi) & (ki <= qi + BLOCK) & (ki >= first_key)


def _attn_head(j, qrot, kk, kks, vv, vvs, sinks_ref, valid):
    half, kv_head = j % 2, j // 4
    mine = (_lane((BLOCK, 128)) >= HEAD_DIM) if half else (_lane((BLOCK, 128)) < HEAD_DIM)
    swapped = kv_head != half
    kx, vx = (kks, vvs) if swapped else (kk, vv)
    qm = jnp.where(mine, qrot[:, 128 * (j // 2):128 * (j // 2 + 1)], 0.0).astype(BF16)
    sc = jnp.where(valid, _dot(qm, kx, NT) * (HEAD_DIM ** -0.5), NEG)
    sink = sinks_ref[j]
    mx = jnp.maximum(jnp.max(sc, axis=-1, keepdims=True), sink)
    e = jnp.exp(sc - mx)
    es = jnp.exp(sink - mx)
    den = jnp.sum(e, axis=-1, keepdims=True) + es
    return e / den, es / den, qm, kx, vx, mine, swapped


def _mixer_fwd(z, cos_t, sin_t, vg, wc, bfull, bd, scale, sinks, *, name):
    S = z.shape[0]
    nb = S // BLOCK

    def body(zc_ref, zpp_ref, zpk_ref, cc_ref, cp_ref, sc_ref, sp_ref, vg_ref, wc_ref, bf_ref, bd_ref, scale_ref, sinks_ref, out_ref):
        n = pl.program_id(0)
        keep = (n > 0).astype(F32)
        a, _ = _gating_fwd(zc_ref[:, GM0:POOL0], vg_ref[...], wc_ref, bf_ref[...])
        out_ref[:, 0:256] = a.astype(BF16)
        b, _ = _pool_fwd(zc_ref[:, POOL0:Q0], zpp_ref[...] * keep, n, bd_ref[...], scale_ref[...])
        out_ref[:, 256:512] = b.astype(BF16)
        cos_c, sin_c = cc_ref[...], sc_ref[...]
        qrot = _rope(zc_ref[:, Q0:KV0], cos_c, sin_c)
        kk = jnp.concatenate([_rope(zpk_ref[:, 0:128], cp_ref[...], sp_ref[...]), _rope(zc_ref[:, KV0:KV0 + 128], cos_c, sin_c)], axis=0)
        vv = jnp.concatenate([zpk_ref[:, 128:256], zc_ref[:, KV0 + 128:Z_END]], axis=0)
        kks, vvs = pltpu.roll(kk, HEAD_DIM, 1).astype(BF16), pltpu.roll(vv, HEAD_DIM, 1).astype(BF16)
        kk, vv = kk.astype(BF16), vv.astype(BF16)
        valid = _attn_mask(jnp.where(n > 0, 0, BLOCK))
        low = _low_half((BLOCK, 128))
        for s in range(ATT_Q_HEADS // 2):
            o = []
            for j in (2 * s, 2 * s + 1):
                p, _, _, _, vx, _, _ = _attn_head(j, qrot, kk, kks, vv, vvs, sinks_ref, valid)
                o.append(_dot(p, vx, NN))
            out_ref[:, 512 + 128 * s:512 + 128 * (s + 1)] = jnp.where(low, o[0], o[1]).astype(BF16)

    cur = lambda n: (n, 0)
    prev = lambda n: (jnp.maximum(n - 1, 0), 0)
    in_specs = [
        pl.BlockSpec((BLOCK, Z_END), cur),
        pl.BlockSpec((BLOCK, 256), lambda n: (jnp.maximum(n - 1, 0), POOL0 // 256)),
        pl.BlockSpec((BLOCK, 256), lambda n: (jnp.maximum(n - 1, 0), KV0 // 256)),
        pl.BlockSpec((BLOCK, 128), cur), pl.BlockSpec((BLOCK, 128), prev),
        pl.BlockSpec((BLOCK, 128), cur), pl.BlockSpec((BLOCK, 128), prev),
        _full((1, 256)), _full((4, BLOCK, BLOCK)), _full((BLOCK, 256)), _full((256, 256)), _full((1, 256)),
        pl.BlockSpec(memory_space=pltpu.SMEM),
    ]
    return pl.pallas_call(
        body, name=name, grid=(nb,), in_specs=in_specs, out_specs=pl.BlockSpec((BLOCK, 1024), cur),
        out_shape=jax.ShapeDtypeStruct((S, 1024), BF16),
        compiler_params=_params("parallel"))(z, z, z, cos_t, cos_t, sin_t, sin_t, vg, wc, bfull, bd, scale, sinks)


def _mixer_bwd(z, dcat, cos_t, sin_t, vg, wc, bfull, bd, scale, sinks, *, name):
    S = z.shape[0]
    nb = S // BLOCK

    def body(zc_ref, zpp_ref, zpk_ref, dc_ref, cc_ref, cp_ref, sc_ref, sp_ref, vg_ref, wc_ref, bf_ref, bd_ref, scale_ref, sinks_ref,
             dz_ref, dvg_ref, dwc_ref, dbf_ref, dbd_ref, dscale_ref, dsink_ref, carry_ref):
        n = pl.program_id(0)

        @pl.when(n == 0)
        def _():
            for ref in (dvg_ref, dwc_ref, dbf_ref, dbd_ref, dscale_ref, dsink_ref):
                ref[...] = jnp.zeros_like(ref)

        @pl.when(n < nb)
        def _():
            keep = (n > 0).astype(F32)
            low = _low_half((BLOCK, 128))
            zg = zc_ref[:, GM0:POOL0]
            vg_v = vg_ref[...]
            _, (th, u, r, vhat, vn, mixed) = _gating_fwd(zg, vg_v, wc_ref, bf_ref[...])
            da = dc_ref[:, 0:256].astype(F32)
            dmixed = da * u
            dbf_ref[...] += dmixed
            dvn = []
            for blk in range(2):
                dmb = dmixed[:, 128 * blk:128 * (blk + 1)]
                vb = vn[:, 128 * blk:128 * (blk + 1)]
                acc = None
                for half in range(2):
                    g = 2 * blk + half
                    dmg = jnp.where(low if half == 0 else ~low, dmb, 0.0)
                    dwc_ref[g] += _tril(_dot(dmg, vb, NT))
                    part = _dot(_tril(wc_ref[g]), dmg, TN)
                    acc = part if acc is None else acc + part
                dvn.append(acc)
            dvn = jnp.concatenate(dvn, axis=1)
            dvg_ref[...] += jnp.sum(dvn * vhat, axis=0, keepdims=True)
            dvhat = dvn * vg_v
            dv = r * (dvhat - vhat * (_group_sum(dvhat * vhat) * (1.0 / HEAD_DIM)))
            dge = jnp.concatenate([da * mixed, dv], axis=1)
            own_g = dge * _gelu_grad(zg, th)
            p = zc_ref[:, POOL0:Q0]
            bd_v, scale_v = bd_ref[...], scale_ref[...]
            _, (inv, pooled, mapped) = _pool_fwd(p, zpp_ref[...] * keep, n, bd_v, scale_v)
            db = dc_ref[:, 256:512].astype(F32)
            dscale_ref[...] += jnp.sum(db * mapped, axis=0, keepdims=True)
            dmapped = db * scale_v
            dbd_ref[...] += _dot(pooled, dmapped, TN)
            dpooled = _dot(dmapped, bd_v, NT)
            dps = dpooled * inv
            back = []
            for blk in range(2):
                dpb = dps[:, 128 * blk:128 * (blk + 1)]
                t0 = _exact_dot(_band(POOL_WINDOWS[2 * blk], True), dpb)
                t1 = _exact_dot(_band(POOL_WINDOWS[2 * blk + 1], True), dpb)
                back.append(jnp.where(jnp.concatenate([low, low], axis=0), t0, t1))
            back = jnp.concatenate(back, axis=1)
            halo_p = back[:BLOCK] * keep
            own_p = back[BLOCK:] - dpooled
            cos_c, sin_c, cos_p, sin_p = cc_ref[...], sc_ref[...], cp_ref[...], sp_ref[...]
            qrot = _rope(zc_ref[:, Q0:KV0], cos_c, sin_c)
            kk = jnp.concatenate([_rope(zpk_ref[:, 0:128], cos_p, sin_p), _rope(zc_ref[:, KV0:KV0 + 128], cos_c, sin_c)], axis=0)
            vv = jnp.concatenate([zpk_ref[:, 128:256], zc_ref[:, KV0 + 128:Z_END]], axis=0)
            kks, vvs = pltpu.roll(kk, HEAD_DIM, 1).astype(BF16), pltpu.roll(vv, HEAD_DIM, 1).astype(BF16)
            kk, vv = kk.astype(BF16), vv.astype(BF16)
            valid = _attn_mask(jnp.where(n > 0, 0, BLOCK))
            dk = [jnp.zeros((2 * BLOCK, 128), F32), jnp.zeros((2 * BLOCK, 128), F32)]
            dv_acc = [jnp.zeros((2 * BLOCK, 128), F32), jnp.zeros((2 * BLOCK, 128), F32)]
            dq = []
            for s in range(ATT_Q_HEADS // 2):
                dqs = []
                dcs = dc_ref[:, 512 + 128 * s:512 + 128 * (s + 1)]
                for j in (2 * s, 2 * s + 1):
                    pr, ps, qm, kx, vx, mine, swapped = _attn_head(j, qrot, kk, kks, vv, vvs, sinks_ref, valid)
                    dom = jnp.where(mine, dcs, jnp.zeros_like(dcs))
                    dp = _dot(dom, vx, NT)
                    dv_acc[swapped] = dv_acc[swapped] + _dot(pr, dom, TN)
                    drow = jnp.sum(pr * dp, axis=-1, keepdims=True)
                    ds = (pr * (dp - drow) * (HEAD_DIM ** -0.5)).astype(BF16)
                    dsink = jnp.sum(ps * drow, axis=0, keepdims=True)
                    dsink_ref[pl.ds(j, 1), :] = dsink_ref[pl.ds(j, 1), :] - jnp.broadcast_to(dsink, (1, 128))
                    dqs.append(_dot(ds, kx, NN))
                    dk[swapped] = dk[swapped] + _dot(ds, qm, TN)
                dq.append(jnp.where(low, dqs[0], dqs[1]))
            dqrot = jnp.concatenate(dq, axis=1)
            own_q = _rope_bwd(dqrot, cos_c, sin_c)
            dkk = dk[0] + pltpu.roll(dk[1], HEAD_DIM, 1)
            dvv = dv_acc[0] + pltpu.roll(dv_acc[1], HEAD_DIM, 1)
            own_k = _rope_bwd(dkk[BLOCK:], cos_c, sin_c)
            halo_k = _rope_bwd(dkk[:BLOCK], cos_p, sin_p) * keep
            own_v, halo_v = dvv[BLOCK:], dvv[:BLOCK] * keep

            @pl.when(n > 0)
            def _():
                dz_ref[:, GM0:POOL0] = carry_ref[:, GM0:POOL0].astype(BF16)
                dz_ref[:, POOL0:Q0] = (carry_ref[:, POOL0:Q0] + halo_p).astype(BF16)
                dz_ref[:, Q0:KV0] = carry_ref[:, Q0:KV0].astype(BF16)
                dz_ref[:, KV0:KV0 + 128] = (carry_ref[:, KV0:KV0 + 128] + halo_k).astype(BF16)
                dz_ref[:, KV0 + 128:Z_END] = (carry_ref[:, KV0 + 128:Z_END] + halo_v).astype(BF16)

            carry_ref[:, GM0:POOL0] = own_g
            carry_ref[:, POOL0:Q0] = own_p
            carry_ref[:, Q0:KV0] = own_q
            carry_ref[:, KV0:KV0 + 128] = own_k
            carry_ref[:, KV0 + 128:Z_END] = own_v

        @pl.when(n == nb)
        def _():
            dz_ref[...] = carry_ref[...].astype(BF16)

    cur = lambda n: (jnp.minimum(n, nb - 1), 0)
    prev = lambda n: (jnp.maximum(jnp.minimum(n, nb - 1) - 1, 0), 0)
    in_specs = [
        pl.BlockSpec((BLOCK, Z_END), cur),
        pl.BlockSpec((BLOCK, 256), lambda n: (jnp.maximum(jnp.minimum(n, nb - 1) - 1, 0), POOL0 // 256)),
        pl.BlockSpec((BLOCK, 256), lambda n: (jnp.maximum(jnp.minimum(n, nb - 1) - 1, 0), KV0 // 256)),
        pl.BlockSpec((BLOCK, 1024), cur),
        pl.BlockSpec((BLOCK, 128), cur), pl.BlockSpec((BLOCK, 128), prev),
        pl.BlockSpec((BLOCK, 128), cur), pl.BlockSpec((BLOCK, 128), prev),
        _full((1, 256)), _full((4, BLOCK, BLOCK)), _full((BLOCK, 256)), _full((256, 256)), _full((1, 256)),
        pl.BlockSpec(memory_space=pltpu.SMEM),
    ]
    out_specs = [pl.BlockSpec((BLOCK, Z_END), lambda n: (jnp.maximum(n - 1, 0), 0)),
                 _full((1, 256)), _full((4, BLOCK, BLOCK)), _full((BLOCK, 256)), _full((256, 256)), _full((1, 256)), _full((8, 128))]
    out_shape = [jax.ShapeDtypeStruct((S, Z_END), BF16), jax.ShapeDtypeStruct((1, 256), F32),
                 jax.ShapeDtypeStruct((4, BLOCK, BLOCK), F32), jax.ShapeDtypeStruct((BLOCK, 256), F32),
                 jax.ShapeDtypeStruct((256, 256), F32), jax.ShapeDtypeStruct((1, 256), F32), jax.ShapeDtypeStruct((8, 128), F32)]
    return pl.pallas_call(
        body, name=name, grid=(nb + 1,), in_specs=in_specs, out_specs=out_specs, out_shape=out_shape,
        scratch_shapes=[pltpu.VMEM((BLOCK, Z_END), F32)],
        compiler_params=_params("arbitrary"))(z, z, z, dcat, cos_t, cos_t, sin_t, sin_t, vg, wc, bfull, bd, scale, sinks)


HBM = pl.BlockSpec(memory_space=pl.ANY)


def _place():
    return lax.axis_index("x"), lax.axis_index("y"), lax.axis_index("c")


def _chip(k, x, y):
    return (1 - x if k & 1 else x, 1 - y if k & 2 else y)


def _block_rows(ref, rows, dev):
    start = pl.multiple_of((4 * dev[0] + 2 * dev[1] + dev[2]) * rows, 8)
    return ref.at[pl.ds(start, rows), :]


def _group_size(arrays, group):
    return sum(arrays[w].shape[0] for w in group)


def _all_gather(shards, groups, carrier, *, name):
    n = len(shards)
    assert all(_group_size(shards, g) == carrier.shape[0] for g in groups) and carrier.dtype == shards[0].dtype

    def body(*refs):
        ins, car, outs = refs[:n], refs[n], refs[n + 1:2 * n + 1]
        send_sems, recv_sems, local_sems = refs[2 * n + 1:]
        x, y, c = _place()
        me, sibling = (x, y, c), (x, y, 1 - c)
        chips = [_chip(k, x, y) for k in (1, 2, 3)]

        def copy(gi, w, k, block, to, src=None):
            dst = _block_rows(outs[w], ins[w].shape[0], block)
            return pltpu.make_async_remote_copy(src_ref=dst if src is None else src, dst_ref=dst, send_sem=send_sems.at[gi, k],
                                                recv_sem=recv_sems.at[gi, k], device_id=to, device_id_type=MESH)

        def drained(gi, k):
            return pltpu.make_async_remote_copy(src_ref=car, dst_ref=car, send_sem=send_sems.at[gi, k],
                                                recv_sem=recv_sems.at[gi, k], device_id=me, device_id_type=MESH)

        for gi, group in enumerate(groups):
            for w in group:
                pltpu.make_async_copy(ins[w], _block_rows(outs[w], ins[w].shape[0], me), local_sems.at[gi]).start()
                copy(gi, w, 0, me, sibling, src=ins[w]).start()
                for j, chip in enumerate(chips):
                    copy(gi, w, 1 + j, me, (*chip, c), src=ins[w]).start()
        for gi, group in enumerate(groups):
            for j, chip in enumerate(chips):
                drained(gi, 1 + j).wait_recv()
                for w in group:
                    copy(gi, w, 4 + j, (*chip, c), sibling).start()
        for gi in range(len(groups)):
            drained(gi, 0).wait_recv()
            for j in range(3):
                drained(gi, 4 + j).wait_recv()
        for gi in range(len(groups)):
            for k in range(7):
                drained(gi, k).wait_send()
            pltpu.make_async_copy(car, car, local_sems.at[gi]).wait()

    g = len(groups)
    return pl.pallas_call(
        body, name=name, in_specs=[HBM] * (n + 1), out_specs=[HBM] * n,
        out_shape=[jax.ShapeDtypeStruct((N_DEV * s.shape[0], s.shape[1]), s.dtype) for s in shards],
        scratch_shapes=[pltpu.SemaphoreType.DMA((g, 7)), pltpu.SemaphoreType.DMA((g, 7)), pltpu.SemaphoreType.DMA((g,))],
    )(*shards, carrier)


def _scatter_in_chip(grads, groups, carrier, *, name):
    n = len(grads)
    assert all(_group_size(grads, g) == 8 * carrier.shape[0] for g in groups)

    def body(*refs):
        ins, car, got = refs[:n], refs[n], refs[n + 1:2 * n + 1]
        send_sems, recv_sems = refs[2 * n + 1:]
        x, y, c = _place()
        sibling = (x, y, 1 - c)
        for gi, group in enumerate(groups):
            for k in range(4):
                chip = _chip(k, x, y)
                for w in group:
                    r = ins[w].shape[0] // N_DEV
                    pltpu.make_async_remote_copy(src_ref=_block_rows(ins[w], r, (*chip, 1 - c)), dst_ref=got[w].at[k],
                                                 send_sem=send_sems.at[gi, k], recv_sem=recv_sems.at[gi, k],
                                                 device_id=sibling, device_id_type=MESH).start()
        for gi in range(len(groups)):
            for k in range(4):
                pltpu.make_async_remote_copy(src_ref=car, dst_ref=car, send_sem=send_sems.at[gi, k], recv_sem=recv_sems.at[gi, k],
                                             device_id=sibling, device_id_type=MESH).wait()

    g = len(groups)
    return pl.pallas_call(
        body, name=name, in_specs=[HBM] * (n + 1), out_specs=[HBM] * n,
        out_shape=[jax.ShapeDtypeStruct((4, a.shape[0] // N_DEV, a.shape[1]), a.dtype) for a in grads],
        scratch_shapes=[pltpu.SemaphoreType.DMA((g, 4))] * 2,
    )(*grads, carrier)


def _chip_sum(grad, got, blocks, *, name):
    _, r, cdim = got.shape
    tr = r if r <= 256 else r // (2 if r % 32 == 0 and r // 2 <= 256 else 4)
    steps = r // tr

    def body(blocks_ref, a0_ref, a1_ref, a2_ref, a3_ref, b_ref, mine_ref, away_ref):
        mine_ref[...] = a0_ref[...] + b_ref[0]
        for k, a_ref in ((1, a1_ref), (2, a2_ref), (3, a3_ref)):
            away_ref[k - 1] = (a_ref[...] + b_ref[k]).astype(BF16)

    own = [pl.BlockSpec((tr, cdim), lambda i, blocks, k=k: (blocks[k] * steps + i, 0)) for k in range(4)]
    return pl.pallas_call(
        body, name=name,
        grid_spec=pltpu.PrefetchScalarGridSpec(
            num_scalar_prefetch=1, grid=(steps,),
            in_specs=own + [pl.BlockSpec((4, tr, cdim), lambda i, blocks: (0, i, 0))],
            out_specs=[pl.BlockSpec((tr, cdim), lambda i, blocks: (i, 0)), pl.BlockSpec((3, tr, cdim), lambda i, blocks: (0, i, 0))]),
        out_shape=[jax.ShapeDtypeStruct((r, cdim), F32), jax.ShapeDtypeStruct((3, r, cdim), BF16)],
        compiler_params=_params("arbitrary"))(blocks, grad, grad, grad, grad, got)


def _exchange_chips(parts, groups, carrier, *, name):
    n = len(parts)
    assert all(sum(parts[w].shape[1] for w in g) == carrier.shape[0] for g in groups)

    def body(*refs):
        ins, car, outs = refs[:n], refs[n], refs[n + 1:2 * n + 1]
        send_sems, recv_sems = refs[2 * n + 1:]
        x, y, c = _place()
        for gi, group in enumerate(groups):
            for k in range(1, 4):
                for w in group:
                    pltpu.make_async_remote_copy(src_ref=ins[w].at[k - 1], dst_ref=outs[w].at[k - 1], send_sem=send_sems.at[gi, k - 1],
                                                 recv_sem=recv_sems.at[gi, k - 1], device_id=(*_chip(k, x, y), c), device_id_type=MESH).start()
        for gi in range(len(groups)):
            for k in range(3):
                pltpu.make_async_remote_copy(src_ref=car, dst_ref=car, send_sem=send_sems.at[gi, k], recv_sem=recv_sems.at[gi, k],
                                             device_id=(x, y, c), device_id_type=MESH).wait()

    g = len(groups)
    return pl.pallas_call(
        body, name=name, in_specs=[HBM] * (n + 1), out_specs=[HBM] * n,
        out_shape=[jax.ShapeDtypeStruct(p.shape, p.dtype) for p in parts],
        scratch_shapes=[pltpu.SemaphoreType.DMA((g, 3))] * 2,
    )(*parts, carrier)


def _final_sum(mine, got, *, name):
    r, cdim = mine.shape
    tr = r if r <= 256 else r // (2 if r % 32 == 0 and r // 2 <= 256 else 4)

    def body(m_ref, g_ref, o_ref):
        o_ref[...] = ((m_ref[...] + g_ref[0].astype(F32)) + g_ref[1].astype(F32)) + g_ref[2].astype(F32)

    return pl.pallas_call(
        body, name=name, grid=(r // tr,),
        in_specs=[pl.BlockSpec((tr, cdim), lambda i: (i, 0)), pl.BlockSpec((3, tr, cdim), lambda i: (0, i, 0))],
        out_specs=pl.BlockSpec((tr, cdim), lambda i: (i, 0)), out_shape=jax.ShapeDtypeStruct((r, cdim), F32),
        compiler_params=_params("parallel"))(mine, got)


def _adamw_math(w, g, m, v):
    m = ADAM_B1 * m + (1.0 - ADAM_B1) * g
    v = ADAM_B2 * v + (1.0 - ADAM_B2) * (g * g)
    m_hat = m / (1.0 - ADAM_B1 ** ADAM_STEP)
    v_hat = v / (1.0 - ADAM_B2 ** ADAM_STEP)
    return -ADAM_LR * (m_hat / (jnp.sqrt(v_hat) + ADAM_EPS) + ADAM_WD * w), m, v


def _adamw(w, g, m, v, *, name):
    L, r, cdim = w.shape
    tr = r if r <= 512 else 512

    def body(w_ref, g_ref, m_ref, v_ref, d_ref, nm_ref, nv_ref):
        d_ref[...], nm_ref[...], nv_ref[...] = _adamw_math(w_ref[...], g_ref[...], m_ref[...], v_ref[...])

    blk = pl.BlockSpec((None, tr, cdim), lambda l, i: (l, i, 0))
    return pl.pallas_call(body, name=name, grid=(L, r // tr), in_specs=[blk] * 4, out_specs=[blk] * 3,
                          out_shape=[jax.ShapeDtypeStruct(w.shape, F32)] * 3, compiler_params=_params("parallel", "parallel"))(w, g, m, v)


def _small_update(parts, w, m, v, *, name):
    _, p, lanes = parts.shape
    tp = p // 2 if p % 16 == 0 else p

    def body(p_ref, w_ref, m_ref, v_ref, g_ref, d_ref, nm_ref, nv_ref):
        g = p_ref[0]
        for i in range(1, N_DEV):
            g = g + p_ref[i]
        g_ref[...] = g
        d_ref[...], nm_ref[...], nv_ref[...] = _adamw_math(w_ref[...], g, m_ref[...], v_ref[...])

    blk = pl.BlockSpec((tp, lanes), lambda i: (i, 0))
    return pl.pallas_call(body, name=name, grid=(p // tp,), in_specs=[pl.BlockSpec((N_DEV, tp, lanes), lambda i: (0, i, 0)), blk, blk, blk],
                          out_specs=[blk] * 4, out_shape=[jax.ShapeDtypeStruct((p, lanes), F32)] * 4,
                          compiler_params=_params("parallel"))(parts, w, m, v)


BIG = ("w_in", "w_o", "w_xq", "w_xkv", "w_xo", "w_gate_up", "w_down")
TRANSPOSED = ("w_in", "w_xkv", "w_gate_up")
SMALL = ("mem_norm_g", "mix_pre_g", "mix_post_g", "gm_v_g", "gm_w_s", "gm_b_s", "pool_w", "pool_scale", "attn_sinks",
         "x_pre_g", "x_post_g", "ffn_pre_g", "ffn_post_g")
WEIGHTS = ("mem_norm_g", "mix_pre_g", "mix_post_g", "w_in", "gm_v_g", "gm_w_s", "gm_b_s", "pool_w", "pool_scale", "attn_sinks", "w_o",
           "x_pre_g", "x_post_g", "w_xq", "w_xkv", "w_xo", "ffn_pre_g", "ffn_post_g", "w_gate_up", "w_down")
PACK_QUANTUM = 8 * 128


def _pack(arrays):
    flat = []
    for a in arrays:
        a = a.reshape(-1).astype(F32)
        flat.append(jnp.pad(a, (0, -a.size % PACK_QUANTUM)))
    return jnp.concatenate(flat).reshape(-1, 128)


def _unpack(pack, shapes):
    out, at = [], 0
    flat = pack.reshape(-1)
    for s in shapes:
        size = math.prod(s)
        out.append(flat[at:at + size].reshape(s))
        at += size + (-size % PACK_QUANTUM)
    return out


def _row(v):
    return v.reshape(1, -1)


def _local_step(x, mem, positions, target, small, big):
    depth = small["mix_pre_g"].shape[0]
    half = HEAD_DIM // 2
    inv = ROPE_THETA ** (-jnp.arange(half, dtype=F32) / half)
    cos_t, sin_t = _rope_tables(positions.reshape(-1, 1), jnp.tile(inv, 128 // half).reshape(1, 128), name="rope_tables")
    mem_g = _row(small["mem_norm_g"])
    saved = []
    for l in range(depth):
        w = {n: big[n][l] for n in BIG}
        bfull = jnp.repeat(small["gm_b_s"][l].T, HEAD_DIM, axis=1)
        bd = jax.scipy.linalg.block_diag(*[small["pool_w"][l, g] for g in range(len(POOL_WINDOWS))])
        mix_par = (_row(small["gm_v_g"][l]), small["gm_w_s"][l], bfull, bd, _row(small["pool_scale"][l]), small["attn_sinks"][l])
        z, h1 = _mm_fwd(x, w["w_in"], nt=True, pre="norm", g_pre=_row(small["mix_pre_g"][l]), out_dtype=F32, name="in_proj")
        cat = _mixer_fwd(z, cos_t, sin_t, *mix_par, name="mixer_fwd")
        mix, x1 = _mm_fwd(cat, w["w_o"], nt=False, post="norm_res", g_post=_row(small["mix_post_g"][l]), xres=x, name="mix_out")
        qx, h2 = _mm_fwd(x1, w["w_xq"], nt=False, pre="norm", g_pre=_row(small["x_pre_g"][l]), name="xq_proj")
        kv, memn = _mm_fwd(mem, w["w_xkv"], nt=True, pre="norm", g_pre=mem_g, name="xkv_proj")
        ox = _xattn_fwd(qx, kv, name="xattn_fwd")
        xo, x2 = _mm_fwd(ox, w["w_xo"], nt=False, post="norm_res", g_post=_row(small["x_post_g"][l]), xres=x1, name="xattn_out")
        gu, h3 = _mm_fwd(x2, w["w_gate_up"], nt=True, pre="norm", g_pre=_row(small["ffn_pre_g"][l]), name="ffn_in")
        f, x3 = _mm_fwd(gu, w["w_down"], nt=False, pre="swiglu", post="norm_res", g_post=_row(small["ffn_post_g"][l]), xres=x2,
                        tile=256, name="ffn_out")
        saved.append((w, mix_par, x, z, h1, cat, mix, x1, qx, h2, kv, memn, ox, xo, x2, gu, h3, f))
        x = x3

    loss, dx = _loss_grad(x, target, name="loss")

    gs = {n: [None] * depth for n in SMALL if n != "mem_norm_g"}
    gb = {n: [None] * depth for n in BIG}
    dmemn = [None] * depth
    for l in reversed(range(depth)):
        w, mix_par, x0, z, h1, cat, mix, x1, qx, h2, kv, memn, ox, xo, x2, gu, h3, f = saved[l]
        dact, df, gs["ffn_post_g"][l] = _mm_bwd_pre(f, dx, _row(small["ffn_post_g"][l]), w["w_down"], name="ffn_out_bwd")
        gb["w_down"][l] = _wgrad(gu, df, swiglu=True, name="w_down_grad")
        dgu = _swiglu_bwd(dact, gu, name="swiglu_bwd")
        gb["w_gate_up"][l] = _wgrad(dgu, h3, name="w_gate_up_grad")
        dx2, gs["ffn_pre_g"][l] = _mm_bwd_post(dgu, w["w_gate_up"], x2, _row(small["ffn_pre_g"][l]), dx, nt=False, tile=256, name="ffn_in_bwd")
        do, dxo, gs["x_post_g"][l] = _mm_bwd_pre(xo, dx2, _row(small["x_post_g"][l]), w["w_xo"], name="xattn_out_bwd")
        gb["w_xo"][l] = _wgrad(ox, dxo, name="w_xo_grad")
        dqx, dkv = _xattn_bwd(qx, kv, do, name="xattn_bwd")
        gb["w_xq"][l] = _wgrad(h2, dqx, name="w_xq_grad")
        dx1, gs["x_pre_g"][l] = _mm_bwd_post(dqx, w["w_xq"], x1, _row(small["x_pre_g"][l]), dx2, nt=True, name="xq_proj_bwd")
        gb["w_xkv"][l] = _wgrad(dkv, memn, name="w_xkv_grad")
        dmemn[l] = _mm_fwd(dkv, w["w_xkv"], nt=False, out_dtype=F32, name="xkv_proj_bwd")[0]
        dcat, dmix, gs["mix_post_g"][l] = _mm_bwd_pre(mix, dx1, _row(small["mix_post_g"][l]), w["w_o"], name="mix_out_bwd")
        gb["w_o"][l] = _wgrad(cat, dmix, name="w_o_grad")
        dz, dvg, dwc, dbf, dbd, dscale, dsink = _mixer_bwd(z, dcat, cos_t, sin_t, *mix_par, name="mixer_bwd")
        gs["gm_v_g"][l], gs["gm_w_s"][l], gs["pool_scale"][l], gs["attn_sinks"][l] = dvg, dwc, dscale, dsink[:, 0]
        gs["gm_b_s"][l] = dbf.reshape(BLOCK, -1, HEAD_DIM).sum(-1).T
        gs["pool_w"][l] = jnp.stack([dbd[HEAD_DIM * g:HEAD_DIM * (g + 1), HEAD_DIM * g:HEAD_DIM * (g + 1)] for g in range(len(POOL_WINDOWS))])
        gb["w_in"][l] = _wgrad(dz, h1, name="w_in_grad")
        dx, gs["mix_pre_g"][l] = _mm_bwd_post(dz, w["w_in"], x0, _row(small["mix_pre_g"][l]), dx1, nt=False, name="in_proj_bwd")

    small_grads = {n: jnp.stack(v).reshape(small[n].shape) for n, v in gs.items()}
    small_grads["mem_norm_g"] = _norm_dgain(mem, jnp.stack(dmemn), name="mem_norm_grad").reshape(-1)
    return loss, dx, small_grads, gb


def _step(a):
    depth = a["mix_pre_g"].shape[0]
    x, mem, target = a["x"][0], a["mem"][0], a["loss_target"][0]

    shards = [(a[n][l].T if n in TRANSPOSED else a[n][l]).astype(BF16) for l in range(depth) for n in BIG]
    groups = [list(range(l * len(BIG), (l + 1) * len(BIG))) for l in range(depth)]
    group_rows = _group_size(shards, groups[0])
    d_model = shards[0].shape[1]
    whole = _all_gather(shards, groups, jnp.zeros((group_rows, d_model), BF16), name="gather_weights")
    big = {n: [whole[l * len(BIG) + i] for l in range(depth)] for i, n in enumerate(BIG)}

    loss, dx, small_grads, gb = _local_step(x, mem, a["positions"][0], target, {n: a[n] for n in SMALL}, big)

    grads = [gb[n][l] for l in range(depth) for n in BIG]
    got = _scatter_in_chip(grads, groups, jnp.zeros((group_rows, d_model), F32), name="scatter_in_chip")
    px, py, pc = _place()
    blocks = jnp.stack([4 * cx + 2 * cy + pc for cx, cy in (_chip(k, px, py) for k in range(4))]).astype(jnp.int32)
    sums = [_chip_sum(grads[i], got[i], blocks, name="chip_sum") for i in range(len(grads))]
    away = _exchange_chips([s[1] for s in sums], groups, jnp.zeros((group_rows, d_model), BF16), name="exchange_chips")
    mine = [_final_sum(sums[i][0], away[i], name="final_sum") for i in range(len(grads))]

    out = {}
    for i, n in enumerate(BIG):
        g = jnp.stack([mine[l * len(BIG) + i].T if n in TRANSPOSED else mine[l * len(BIG) + i] for l in range(depth)])
        out["grad_" + n] = g
        out["delta_" + n], out["new_m_" + n], out["new_v_" + n] = _adamw(a[n], g, a["m_" + n], a["v_" + n], name="adamw_" + n)

    part = _pack([small_grads[n] for n in SMALL])
    parts = _all_gather([part], [[0]], part, name="gather_small_grads")[0].reshape(N_DEV, *part.shape)
    packs = _small_update(parts, *[_pack([a[p + n] for n in SMALL]) for p in ("", "m_", "v_")], name="small_update")
    for p, pack in zip(("grad_", "delta_", "new_m_", "new_v_"), packs):
        for n, v in zip(SMALL, _unpack(pack, [a[n].shape for n in SMALL])):
            out[p + n] = v

    total = lax.psum(loss[0, 0], ("x", "y", "c"))
    return (total, dx[None], *[out[p + n] for p in ("grad_", "delta_", "new_m_", "new_v_") for n in WEIGHTS])


def kernel(x, mem, positions, mem_norm_g, mix_pre_g, mix_post_g, w_in, gm_v_g, gm_w_s, gm_b_s, pool_w, pool_scale, attn_sinks,
           w_o, x_pre_g, x_post_g, w_xq, w_xkv, w_xo, ffn_pre_g, ffn_post_g, w_gate_up, w_down, loss_target, m_mem_norm_g,
           m_mix_pre_g, m_mix_post_g, m_w_in, m_gm_v_g, m_gm_w_s, m_gm_b_s, m_pool_w, m_pool_scale, m_attn_sinks, m_w_o,
           m_x_pre_g, m_x_post_g, m_w_xq, m_w_xkv, m_w_xo, m_ffn_pre_g, m_ffn_post_g, m_w_gate_up, m_w_down, v_mem_norm_g,
           v_mix_pre_g, v_mix_post_g, v_w_in, v_gm_v_g, v_gm_w_s, v_gm_b_s, v_pool_w, v_pool_scale, v_attn_sinks, v_w_o,
           v_x_pre_g, v_x_post_g, v_w_xq, v_w_xkv, v_w_xo, v_ffn_pre_g, v_ffn_post_g, v_w_gate_up, v_w_down):
    return _step(dict(locals()))
```

```python
import functools
import math

import jax
import jax.numpy as jnp
from jax import lax
from jax.experimental import pallas as pl
from jax.experimental.pallas import tpu as pltpu

F32, BF16 = jnp.float32, jnp.bfloat16
EPS = 1e-6
HEAD_DIM = 64
BLOCK = 128
POOL_WINDOWS = (2, 4, 8, 16)
ATT_Q_HEADS = 8
X_HEADS = 4
ROPE_THETA = 10000.0
ADAM_LR, ADAM_B1, ADAM_B2, ADAM_EPS, ADAM_WD, ADAM_STEP = 0.001, 0.9, 0.999, 1e-08, 0.01, 10
N_DEV = 8
TOKEN_TILE = 512
VMEM_LIMIT_BYTES = 50 * 2**20
NEG = -1e30
MESH = pl.DeviceIdType.MESH

NN = ((1,), (0,))
NT = ((1,), (1,))
TN = ((0,), (0,))


def _dot(a, b, dims):
    return lax.dot_general(a.astype(BF16), b.astype(BF16), (dims, ((), ())), preferred_element_type=F32)


def _params(*sem):
    return pltpu.CompilerParams(dimension_semantics=sem, vmem_limit_bytes=VMEM_LIMIT_BYTES)


def _full(shape):
    return pl.BlockSpec(shape, lambda *_: (0,) * len(shape))


def _lane(shape):
    return lax.broadcasted_iota(jnp.int32, shape, len(shape) - 1)


def _rms_fwd(x, g):
    r = lax.rsqrt(jnp.mean(x * x, axis=-1, keepdims=True) + EPS)
    return x * r * g


def _rms_bwd(x, g, dy):
    r = lax.rsqrt(jnp.mean(x * x, axis=-1, keepdims=True) + EPS)
    xh = x * r
    dg = jnp.sum(dy * xh, axis=0, keepdims=True)
    dxh = dy * g
    dx = r * (dxh - xh * jnp.mean(dxh * xh, axis=-1, keepdims=True))
    return dx, dg


def _silu_parts(gate):
    sg = 1.0 / (1.0 + jnp.exp(-gate))
    return gate * sg, sg


def _swiglu_cols(gu_ref, out_ref, width):
    step = 128 * max(1, (width // 128) // 4)
    for c0 in range(0, width, step):
        c1 = min(width, c0 + step)
        gate = gu_ref[:, c0:c1].astype(F32)
        up = gu_ref[:, width + c0:width + c1].astype(F32)
        out_ref[:, c0:c1] = (_silu_parts(gate)[0] * up).astype(out_ref.dtype)


def _col_tile(n, cap=1408):
    best = n
    for t in range(128, cap + 1, 128):
        if n % t == 0:
            best = t
    return best if n > cap else n


def _mm_fwd(a, w, *, nt, name, pre=None, g_pre=None, post=None, g_post=None, xres=None, out_dtype=BF16, tile=TOKEN_TILE):
    S = a.shape[0]
    n_out, k = (w.shape[0], w.shape[1]) if nt else (w.shape[1], w.shape[0])
    T = min(tile, S)
    tn = n_out if post else _col_tile(n_out)
    grid = (S // T, n_out // tn)

    def body(*refs):
        it = iter(refs)
        a_ref, w_ref = next(it), next(it)
        gpre_ref = next(it) if pre == "norm" else None
        gpost_ref, x_ref = (next(it), next(it)) if post else (None, None)
        outs = [next(it) for _ in range((2 if post else 1) + (1 if pre == "norm" else 0))]
        as_ref = next(it) if pre else None
        if pre:
            @pl.when(pl.program_id(1) == 0)
            def _():
                if pre == "norm":
                    h = _rms_fwd(a_ref[...], gpre_ref[...]).astype(BF16)
                    outs[-1][...] = h
                    as_ref[...] = h
                else:
                    _swiglu_cols(a_ref, as_ref, k)
            av = as_ref[...]
        else:
            av = a_ref[...]
        acc = _dot(av, w_ref[...], NT if nt else NN)
        if post:
            outs[0][...] = acc
            outs[1][...] = x_ref[...] + _rms_fwd(acc, gpost_ref[...])
        else:
            outs[0][...] = acc.astype(out_dtype)

    in_specs = [pl.BlockSpec((T, a.shape[1]), lambda i, j: (i, 0)),
                pl.BlockSpec((tn, k), lambda i, j: (j, 0)) if nt else pl.BlockSpec((k, tn), lambda i, j: (0, j))]
    args = [a, w]
    if pre == "norm":
        in_specs.append(_full((1, k)))
        args.append(g_pre)
    if post:
        in_specs += [_full((1, n_out)), pl.BlockSpec((T, n_out), lambda i, j: (i, 0))]
        args += [g_post, xres]
    out_block = pl.BlockSpec((T, tn), lambda i, j: (i, j))
    if post:
        out_shape = [jax.ShapeDtypeStruct((S, n_out), F32)] * 2
        out_specs = [out_block, out_block]
    else:
        out_shape = [jax.ShapeDtypeStruct((S, n_out), out_dtype)]
        out_specs = [out_block]
    if pre == "norm":
        out_shape.append(jax.ShapeDtypeStruct((S, k), BF16))
        out_specs.append(pl.BlockSpec((T, k), lambda i, j: (i, 0)))
    scratch = [pltpu.VMEM((T, k), BF16)] if pre else []
    return pl.pallas_call(body, name=name, grid=grid, in_specs=in_specs, out_specs=out_specs, out_shape=out_shape,
                          scratch_shapes=scratch, compiler_params=_params("parallel", "arbitrary"))(*args)


def _mm_bwd_pre(m, dx, g, w, *, name, tile=TOKEN_TILE):
    S, n = m.shape
    ku = w.shape[0]
    T = min(tile, S)
    tn = _col_tile(ku)
    grid = (S // T, ku // tn)

    def body(m_ref, dx_ref, g_ref, w_ref, du_ref, dm_ref, dg_ref, as_ref):
        i, j = pl.program_id(0), pl.program_id(1)

        @pl.when((i == 0) & (j == 0))
        def _():
            dg_ref[...] = jnp.zeros_like(dg_ref)

        @pl.when(j == 0)
        def _():
            dmv, dg = _rms_bwd(m_ref[...], g_ref[...], dx_ref[...])
            dg_ref[...] += dg
            dmb = dmv.astype(BF16)
            dm_ref[...] = dmb
            as_ref[...] = dmb

        du_ref[...] = _dot(as_ref[...], w_ref[...], NT).astype(BF16)

    row = pl.BlockSpec((T, n), lambda i, j: (i, 0))
    return pl.pallas_call(
        body, name=name, grid=grid,
        in_specs=[row, row, _full((1, n)), pl.BlockSpec((tn, n), lambda i, j: (j, 0))],
        out_specs=[pl.BlockSpec((T, tn), lambda i, j: (i, j)), row, _full((1, n))],
        out_shape=[jax.ShapeDtypeStruct((S, ku), BF16), jax.ShapeDtypeStruct((S, n), BF16), jax.ShapeDtypeStruct((1, n), F32)],
        scratch_shapes=[pltpu.VMEM((T, n), BF16)], compiler_params=_params("arbitrary", "arbitrary"))(m, dx, g, w)


def _mm_bwd_post(dy, w, x, g, dx_in, *, nt, name, tile=TOKEN_TILE):
    S, k = dy.shape
    n = x.shape[1]
    T = min(tile, S)

    def body(dy_ref, w_ref, x_ref, g_ref, dxi_ref, dxo_ref, dg_ref):
        @pl.when(pl.program_id(0) == 0)
        def _():
            dg_ref[...] = jnp.zeros_like(dg_ref)

        dh = _dot(dy_ref[...], w_ref[...], NT if nt else NN)
        dxn, dg = _rms_bwd(x_ref[...], g_ref[...], dh)
        dg_ref[...] += dg
        dxo_ref[...] = dxi_ref[...] + dxn

    row = pl.BlockSpec((T, n), lambda i: (i, 0))
    return pl.pallas_call(
        body, name=name, grid=(S // T,),
        in_specs=[pl.BlockSpec((T, k), lambda i: (i, 0)), _full(w.shape), row, _full((1, n)), row],
        out_specs=[row, _full((1, n))],
        out_shape=[jax.ShapeDtypeStruct((S, n), F32), jax.ShapeDtypeStruct((1, n), F32)],
        compiler_params=_params("arbitrary"))(dy, w, x, g, dx_in)


def _wgrad(a, b, *, name, swiglu=False, tile=TOKEN_TILE):
    S = a.shape[0]
    k = a.shape[1] // 2 if swiglu else a.shape[1]
    n = b.shape[1]
    T = min(tile, S)
    tk = 512 if k % 512 == 0 else 256
    nk = k // tk

    def body(*refs):
        if swiglu:
            gate_ref, up_ref, b_ref, o_ref, act_ref = refs
            for r0 in range(0, S, T):
                gate, up = gate_ref[r0:r0 + T, :].astype(F32), up_ref[r0:r0 + T, :].astype(F32)
                act_ref[r0:r0 + T, :] = (_silu_parts(gate)[0] * up).astype(BF16)
            av = act_ref[...]
        else:
            a_ref, b_ref, o_ref = refs
            av = a_ref[...]
        o_ref[...] = _dot(av, b_ref[...], TN)

    in_specs = [pl.BlockSpec((S, tk), lambda kk: (0, kk))]
    args = [a]
    if swiglu:
        in_specs.append(pl.BlockSpec((S, tk), lambda kk: (0, kk + nk)))
        args.append(a)
    in_specs.append(_full((S, n)))
    args.append(b)
    return pl.pallas_call(body, name=name, grid=(nk,), in_specs=in_specs,
                          out_specs=pl.BlockSpec((tk, n), lambda kk: (kk, 0)),
                          out_shape=jax.ShapeDtypeStruct((k, n), F32),
                          scratch_shapes=[pltpu.VMEM((S, tk), BF16)] if swiglu else [],
                          compiler_params=_params("parallel"))(*args)


def _swiglu_bwd(dact, gu, *, name, tile=256):
    S, f = dact.shape
    T = min(tile, S)

    def body(da_ref, gu_ref, o_ref):
        step = 128 * max(1, (f // 128) // 4)
        for c0 in range(0, f, step):
            c1 = min(f, c0 + step)
            gate = gu_ref[:, c0:c1].astype(F32)
            up = gu_ref[:, f + c0:f + c1].astype(F32)
            da = da_ref[:, c0:c1].astype(F32)
            sl, sg = _silu_parts(gate)
            o_ref[:, c0:c1] = (da * up * (sg + sl * (1.0 - sg))).astype(BF16)
            o_ref[:, f + c0:f + c1] = (da * sl).astype(BF16)

    return pl.pallas_call(
        body, name=name, grid=(S // T,),
        in_specs=[pl.BlockSpec((T, f), lambda i: (i, 0)), pl.BlockSpec((T, 2 * f), lambda i: (i, 0))],
        out_specs=pl.BlockSpec((T, 2 * f), lambda i: (i, 0)),
        out_shape=jax.ShapeDtypeStruct((S, 2 * f), BF16), compiler_params=_params("parallel"))(dact, gu)


def _loss_grad(y, target, *, name, tile=TOKEN_TILE):
    S, d = y.shape
    T = min(tile, S)
    steps = S // T

    def body(y_ref, t_ref, loss_ref, dy_ref, acc_ref):
        i = pl.program_id(0)

        @pl.when(i == 0)
        def _():
            acc_ref[...] = jnp.zeros_like(acc_ref)

        diff = y_ref[...] - t_ref[...]
        dy_ref[...] = diff * (1.0 / d)
        acc_ref[...] += jnp.sum(diff * diff, axis=0, keepdims=True)

        @pl.when(i == steps - 1)
        def _():
            total = jnp.sum(acc_ref[...], axis=1, keepdims=True) * (0.5 / d)
            loss_ref[...] = jnp.broadcast_to(total, loss_ref.shape)

    row = pl.BlockSpec((T, d), lambda i: (i, 0))
    return pl.pallas_call(
        body, name=name, grid=(steps,), in_specs=[row, row], out_specs=[_full((1, 128)), row],
        out_shape=[jax.ShapeDtypeStruct((1, 128), F32), jax.ShapeDtypeStruct((S, d), F32)],
        scratch_shapes=[pltpu.VMEM((1, d), F32)], compiler_params=_params("arbitrary"))(y, target)


def _norm_dgain(x, dys, *, name):
    def body(x_ref, dy_ref, o_ref):
        xv = x_ref[...]
        dy = dy_ref[0]
        for l in range(1, dys.shape[0]):
            dy = dy + dy_ref[l]
        r = lax.rsqrt(jnp.mean(xv * xv, axis=-1, keepdims=True) + EPS)
        o_ref[...] = jnp.sum(dy * xv * r, axis=0, keepdims=True)

    return pl.pallas_call(body, name=name, out_shape=jax.ShapeDtypeStruct((1, x.shape[1]), F32),
                          compiler_params=pltpu.CompilerParams(vmem_limit_bytes=VMEM_LIMIT_BYTES))(x, dys)


def _xattn_probs(q_ref, kv_ref, h, d, hd):
    q = q_ref[:, h * hd:(h + 1) * hd]
    kh = kv_ref[:, h * hd:(h + 1) * hd]
    vh = kv_ref[:, d + h * hd:d + (h + 1) * hd]
    s = _dot(q, kh, NT) * (hd ** -0.5)
    e = jnp.exp(s - jnp.max(s, axis=-1, keepdims=True))
    return q, kh, vh, e / jnp.sum(e, axis=-1, keepdims=True)


def _xattn_fwd(q, kv, *, name, tile=TOKEN_TILE):
    S, d = q.shape
    mlen = kv.shape[0]
    hd = d // X_HEADS
    T = min(tile, S)

    def body(q_ref, kv_ref, o_ref):
        for h in range(X_HEADS):
            _, _, vh, p = _xattn_probs(q_ref, kv_ref, h, d, hd)
            o_ref[:, h * hd:(h + 1) * hd] = _dot(p, vh, NN).astype(BF16)

    row = pl.BlockSpec((T, d), lambda i: (i, 0))
    return pl.pallas_call(body, name=name, grid=(S // T,), in_specs=[row, _full((mlen, 2 * d))], out_specs=row,
                          out_shape=jax.ShapeDtypeStruct((S, d), BF16), compiler_params=_params("parallel"))(q, kv)


def _xattn_bwd(q, kv, do, *, name, tile=TOKEN_TILE):
    S, d = q.shape
    mlen = kv.shape[0]
    hd = d // X_HEADS
    T = min(tile, S)

    def body(q_ref, kv_ref, do_ref, dq_ref, dkv_ref):
        @pl.when(pl.program_id(0) == 0)
        def _():
            dkv_ref[...] = jnp.zeros_like(dkv_ref)

        for h in range(X_HEADS):
            qh, kh, vh, p = _xattn_probs(q_ref, kv_ref, h, d, hd)
            doh = do_ref[:, h * hd:(h + 1) * hd]
            dp = _dot(doh, vh, NT)
            ds = p * (dp - jnp.sum(p * dp, axis=-1, keepdims=True)) * (hd ** -0.5)
            dq_ref[:, h * hd:(h + 1) * hd] = _dot(ds, kh, NN).astype(BF16)
            dkv_ref[:, h * hd:(h + 1) * hd] += _dot(ds, qh, TN)
            dkv_ref[:, d + h * hd:d + (h + 1) * hd] += _dot(p, doh, TN)

    row = pl.BlockSpec((T, d), lambda i: (i, 0))
    return pl.pallas_call(
        body, name=name, grid=(S // T,), in_specs=[row, _full((mlen, 2 * d)), row],
        out_specs=[row, _full((mlen, 2 * d))],
        out_shape=[jax.ShapeDtypeStruct((S, d), BF16), jax.ShapeDtypeStruct((mlen, 2 * d), F32)],
        compiler_params=_params("arbitrary"))(q, kv, do)


GM0, POOL0, Q0, KV0, Z_END = 0, 512, 768, 1280, 1536


def _rope_tables(pos_col, inv_row, *, name):
    S = pos_col.shape[0]

    def body(p_ref, inv_ref, cos_ref, sin_ref):
        ang = p_ref[...].astype(F32) * inv_ref[...]
        sin = jnp.sin(ang)
        cos_ref[...] = jnp.cos(ang)
        sin_ref[...] = jnp.where(_lane(ang.shape) % HEAD_DIM < HEAD_DIM // 2, -sin, sin)

    T = min(TOKEN_TILE, S)
    return pl.pallas_call(
        body, name=name, grid=(S // T,),
        in_specs=[pl.BlockSpec((T, 1), lambda i: (i, 0)), _full((1, 128))],
        out_specs=[pl.BlockSpec((T, 128), lambda i: (i, 0))] * 2,
        out_shape=[jax.ShapeDtypeStruct((S, 128), F32)] * 2, compiler_params=_params("parallel"))(pos_col, inv_row)


def _swap_halves(x):
    n = x.shape[-1]
    return jnp.where(_lane(x.shape) % HEAD_DIM < HEAD_DIM // 2, pltpu.roll(x, n - HEAD_DIM // 2, 1), pltpu.roll(x, HEAD_DIM // 2, 1))


def _rope(x, cos, sin_s):
    reps = x.shape[-1] // 128
    if reps > 1:
        cos, sin_s = jnp.tile(cos, (1, reps)), jnp.tile(sin_s, (1, reps))
    return x * cos + _swap_halves(x) * sin_s


def _rope_bwd(dy, cos, sin_s):
    reps = dy.shape[-1] // 128
    if reps > 1:
        cos, sin_s = jnp.tile(cos, (1, reps)), jnp.tile(sin_s, (1, reps))
    return dy * cos + _swap_halves(dy * sin_s)


def _gelu_parts(x):
    c = math.sqrt(2.0 / math.pi)
    t = jnp.tanh(c * (x + 0.044715 * x * x * x))
    return 0.5 * x * (1.0 + t), t


def _gelu_grad(x, t):
    c = math.sqrt(2.0 / math.pi)
    return 0.5 * (1.0 + t) + 0.5 * x * (1.0 - t * t) * c * (1.0 + 3.0 * 0.044715 * x * x)


def _group_sum(x):
    gid = _lane(x.shape) // HEAD_DIM
    out = jnp.zeros_like(x)
    for g in range(x.shape[-1] // HEAD_DIM):
        sel = gid == g
        out = jnp.where(sel, jnp.sum(jnp.where(sel, x, 0.0), axis=-1, keepdims=True), out)
    return out


def _low_half(shape):
    return _lane(shape) % 128 < HEAD_DIM


def _tril(w):
    r = lax.broadcasted_iota(jnp.int32, w.shape, 0)
    c = lax.broadcasted_iota(jnp.int32, w.shape, 1)
    return jnp.where(r >= c, w, 0.0)


def _gating_fwd(zg, vg, wc_ref, bfull):
    ge, th = _gelu_parts(zg)
    u, v = ge[:, :256], ge[:, 256:]
    r = lax.rsqrt(_group_sum(v * v) * (1.0 / HEAD_DIM) + EPS)
    vhat = v * r
    vn = vhat * vg
    low = _low_half((BLOCK, 128))
    blocks = []
    for blk in range(2):
        vb = vn[:, 128 * blk:128 * (blk + 1)]
        m0 = _dot(_tril(wc_ref[2 * blk]), vb, NN)
        m1 = _dot(_tril(wc_ref[2 * blk + 1]), vb, NN)
        blocks.append(jnp.where(low, m0, m1))
    mixed = jnp.concatenate(blocks, axis=1) + bfull
    return u * mixed, (th, u, r, vhat, vn, mixed)


def _band(w, transpose):
    if transpose:
        s2 = lax.broadcasted_iota(jnp.int32, (2 * BLOCK, BLOCK), 0)
        t = lax.broadcasted_iota(jnp.int32, (2 * BLOCK, BLOCK), 1)
    else:
        t = lax.broadcasted_iota(jnp.int32, (BLOCK, 2 * BLOCK), 0)
        s2 = lax.broadcasted_iota(jnp.int32, (BLOCK, 2 * BLOCK), 1)
    d = t - s2 + BLOCK
    return jnp.where((d >= 0) & (d < w), 1.0, 0.0).astype(BF16)


def _exact_dot(b01, x):
    hi = x.astype(BF16)
    lo = (x - hi.astype(F32)).astype(BF16)
    return _dot(b01, hi, NN) + _dot(b01, lo, NN)


def _inv_count(n):
    t = lax.broadcasted_iota(jnp.int32, (BLOCK, 256), 0) + n * BLOCK
    gid = _lane((BLOCK, 256)) // HEAD_DIM
    win = jnp.where(gid == 0, POOL_WINDOWS[0], jnp.where(gid == 1, POOL_WINDOWS[1], jnp.where(gid == 2, POOL_WINDOWS[2], POOL_WINDOWS[3])))
    return 1.0 / jnp.minimum(t + 1, win).astype(F32)


def _pool_fwd(p, pp, n, bd, scale):
    low = _low_half((BLOCK, 128))
    blocks = []
    for blk in range(2):
        both = jnp.concatenate([pp[:, 128 * blk:128 * (blk + 1)], p[:, 128 * blk:128 * (blk + 1)]], axis=0)
        s0 = _exact_dot(_band(POOL_WINDOWS[2 * blk], False), both)
        s1 = _exact_dot(_band(POOL_WINDOWS[2 * blk + 1], False), both)
        blocks.append(jnp.where(low, s0, s1))
    inv = _inv_count(n)
    pooled = jnp.concatenate(blocks, axis=1) * inv - p
    mapped = _dot(pooled, bd, NN)
    return mapped * scale, (inv, pooled, mapped)


def _attn_mask(first_key):
    qi = lax.broadcasted_iota(jnp.int32, (BLOCK, 2 * BLOCK), 0)
    ki = lax.broadcasted_iota(jnp.int32, (BLOCK, 2 * BLOCK), 1)
    return (ki > qi) & (ki <= qi + BLOCK) & (ki >= first_key)


def _attn_head(j, qrot, kk, kks, vv, vvs, sinks_ref, valid):
    half, kv_head = j % 2, j // 4
    mine = (_lane((BLOCK, 128)) >= HEAD_DIM) if half else (_lane((BLOCK, 128)) < HEAD_DIM)
    swapped = kv_head != half
    kx, vx = (kks, vvs) if swapped else (kk, vv)
    qm = jnp.where(mine, qrot[:, 128 * (j // 2):128 * (j // 2 + 1)], 0.0).astype(BF16)
    sc = jnp.where(valid, _dot(qm, kx, NT) * (HEAD_DIM ** -0.5), NEG)
    sink = sinks_ref[j]
    mx = jnp.maximum(jnp.max(sc, axis=-1, keepdims=True), sink)
    e = jnp.exp(sc - mx)
    es = jnp.exp(sink - mx)
    den = jnp.sum(e, axis=-1, keepdims=True) + es
    return e / den, es / den, qm, kx, vx, mine, swapped


def _mixer_fwd(z, cos_t, sin_t, vg, wc, bfull, bd, scale, sinks, *, name):
    S = z.shape[0]
    nb = S // BLOCK

    def body(zc_ref, zpp_ref, zpk_ref, cc_ref, cp_ref, sc_ref, sp_ref, vg_ref, wc_ref, bf_ref, bd_ref, scale_ref, sinks_ref, out_ref):
        n = pl.program_id(0)
        keep = (n > 0).astype(F32)
        a, _ = _gating_fwd(zc_ref[:, GM0:POOL0], vg_ref[...], wc_ref, bf_ref[...])
        out_ref[:, 0:256] = a.astype(BF16)
        b, _ = _pool_fwd(zc_ref[:, POOL0:Q0], zpp_ref[...] * keep, n, bd_ref[...], scale_ref[...])
        out_ref[:, 256:512] = b.astype(BF16)
        cos_c, sin_c = cc_ref[...], sc_ref[...]
        qrot = _rope(zc_ref[:, Q0:KV0], cos_c, sin_c)
        kk = jnp.concatenate([_rope(zpk_ref[:, 0:128], cp_ref[...], sp_ref[...]), _rope(zc_ref[:, KV0:KV0 + 128], cos_c, sin_c)], axis=0)
        vv = jnp.concatenate([zpk_ref[:, 128:256], zc_ref[:, KV0 + 128:Z_END]], axis=0)
        kks, vvs = pltpu.roll(kk, HEAD_DIM, 1).astype(BF16), pltpu.roll(vv, HEAD_DIM, 1).astype(BF16)
        kk, vv = kk.astype(BF16), vv.astype(BF16)
        valid = _attn_mask(jnp.where(n > 0, 0, BLOCK))
        low = _low_half((BLOCK, 128))
        for s in range(ATT_Q_HEADS // 2):
            o = []
            for j in (2 * s, 2 * s + 1):
                p, _, _, _, vx, _, _ = _attn_head(j, qrot, kk, kks, vv, vvs, sinks_ref, valid)
                o.append(_dot(p, vx, NN))
            out_ref[:, 512 + 128 * s:512 + 128 * (s + 1)] = jnp.where(low, o[0], o[1]).astype(BF16)

    cur = lambda n: (n, 0)
    prev = lambda n: (jnp.maximum(n - 1, 0), 0)
    in_specs = [
        pl.BlockSpec((BLOCK, Z_END), cur),
        pl.BlockSpec((BLOCK, 256), lambda n: (jnp.maximum(n - 1, 0), POOL0 // 256)),
        pl.BlockSpec((BLOCK, 256), lambda n: (jnp.maximum(n - 1, 0), KV0 // 256)),
        pl.BlockSpec((BLOCK, 128), cur), pl.BlockSpec((BLOCK, 128), prev),
        pl.BlockSpec((BLOCK, 128), cur), pl.BlockSpec((BLOCK, 128), prev),
        _full((1, 256)), _full((4, BLOCK, BLOCK)), _full((BLOCK, 256)), _full((256, 256)), _full((1, 256)),
        pl.BlockSpec(memory_space=pltpu.SMEM),
    ]
    return pl.pallas_call(
        body, name=name, grid=(nb,), in_specs=in_specs, out_specs=pl.BlockSpec((BLOCK, 1024), cur),
        out_shape=jax.ShapeDtypeStruct((S, 1024), BF16),
        compiler_params=_params("parallel"))(z, z, z, cos_t, cos_t, sin_t, sin_t, vg, wc, bfull, bd, scale, sinks)


def _mixer_bwd(z, dcat, cos_t, sin_t, vg, wc, bfull, bd, scale, sinks, *, name):
    S = z.shape[0]
    nb = S // BLOCK

    def body(zc_ref, zpp_ref, zpk_ref, dc_ref, cc_ref, cp_ref, sc_ref, sp_ref, vg_ref, wc_ref, bf_ref, bd_ref, scale_ref, sinks_ref,
             dz_ref, dvg_ref, dwc_ref, dbf_ref, dbd_ref, dscale_ref, dsink_ref, carry_ref):
        n = pl.program_id(0)

        @pl.when(n == 0)
        def _():
            for ref in (dvg_ref, dwc_ref, dbf_ref, dbd_ref, dscale_ref, dsink_ref):
                ref[...] = jnp.zeros_like(ref)

        @pl.when(n < nb)
        def _():
            keep = (n > 0).astype(F32)
            low = _low_half((BLOCK, 128))
            zg = zc_ref[:, GM0:POOL0]
            vg_v = vg_ref[...]
            _, (th, u, r, vhat, vn, mixed) = _gating_fwd(zg, vg_v, wc_ref, bf_ref[...])
            da = dc_ref[:, 0:256].astype(F32)
            dmixed = da * u
            dbf_ref[...] += dmixed
            dvn = []
            for blk in range(2):
                dmb = dmixed[:, 128 * blk:128 * (blk + 1)]
                vb = vn[:, 128 * blk:128 * (blk + 1)]
                acc = None
                for half in range(2):
                    g = 2 * blk + half
                    dmg = jnp.where(low if half == 0 else ~low, dmb, 0.0)
                    dwc_ref[g] += _tril(_dot(dmg, vb, NT))
                    part = _dot(_tril(wc_ref[g]), dmg, TN)
                    acc = part if acc is None else acc + part
                dvn.append(acc)
            dvn = jnp.concatenate(dvn, axis=1)
            dvg_ref[...] += jnp.sum(dvn * vhat, axis=0, keepdims=True)
            dvhat = dvn * vg_v
            dv = r * (dvhat - vhat * (_group_sum(dvhat * vhat) * (1.0 / HEAD_DIM)))
            dge = jnp.concatenate([da * mixed, dv], axis=1)
            own_g = dge * _gelu_grad(zg, th)
            p = zc_ref[:, POOL0:Q0]
            bd_v, scale_v = bd_ref[...], scale_ref[...]
            _, (inv, pooled, mapped) = _pool_fwd(p, zpp_ref[...] * keep, n, bd_v, scale_v)
            db = dc_ref[:, 256:512].astype(F32)
            dscale_ref[...] += jnp.sum(db * mapped, axis=0, keepdims=True)
            dmapped = db * scale_v
            dbd_ref[...] += _dot(pooled, dmapped, TN)
            dpooled = _dot(dmapped, bd_v, NT)
            dps = dpooled * inv
            back = []
            for blk in range(2):
                dpb = dps[:, 128 * blk:128 * (blk + 1)]
                t0 = _exact_dot(_band(POOL_WINDOWS[2 * blk], True), dpb)
                t1 = _exact_dot(_band(POOL_WINDOWS[2 * blk + 1], True), dpb)
                back.append(jnp.where(jnp.concatenate([low, low], axis=0), t0, t1))
            back = jnp.concatenate(back, axis=1)
            halo_p = back[:BLOCK] * keep
            own_p = back[BLOCK:] - dpooled
            cos_c, sin_c, cos_p, sin_p = cc_ref[...], sc_ref[...], cp_ref[...], sp_ref[...]
            qrot = _rope(zc_ref[:, Q0:KV0], cos_c, sin_c)
            kk = jnp.concatenate([_rope(zpk_ref[:, 0:128], cos_p, sin_p), _rope(zc_ref[:, KV0:KV0 + 128], cos_c, sin_c)], axis=0)
            vv = jnp.concatenate([zpk_ref[:, 128:256], zc_ref[:, KV0 + 128:Z_END]], axis=0)
            kks, vvs = pltpu.roll(kk, HEAD_DIM, 1).astype(BF16), pltpu.roll(vv, HEAD_DIM, 1).astype(BF16)
            kk, vv = kk.astype(BF16), vv.astype(BF16)
            valid = _attn_mask(jnp.where(n > 0, 0, BLOCK))
            dk = [jnp.zeros((2 * BLOCK, 128), F32), jnp.zeros((2 * BLOCK, 128), F32)]
            dv_acc = [jnp.zeros((2 * BLOCK, 128), F32), jnp.zeros((2 * BLOCK, 128), F32)]
            dq = []
            for s in range(ATT_Q_HEADS // 2):
                dqs = []
                dcs = dc_ref[:, 512 + 128 * s:512 + 128 * (s + 1)]
                for j in (2 * s, 2 * s + 1):
                    pr, ps, qm, kx, vx, mine, swapped = _attn_head(j, qrot, kk, kks, vv, vvs, sinks_ref, valid)
                    dom = jnp.where(mine, dcs, jnp.zeros_like(dcs))
                    dp = _dot(dom, vx, NT)
                    dv_acc[swapped] = dv_acc[swapped] + _dot(pr, dom, TN)
                    drow = jnp.sum(pr * dp, axis=-1, keepdims=True)
                    ds = (pr * (dp - drow) * (HEAD_DIM ** -0.5)).astype(BF16)
                    dsink = jnp.sum(ps * drow, axis=0, keepdims=True)
                    dsink_ref[pl.ds(j, 1), :] = dsink_ref[pl.ds(j, 1), :] - jnp.broadcast_to(dsink, (1, 128))
                    dqs.append(_dot(ds, kx, NN))
                    dk[swapped] = dk[swapped] + _dot(ds, qm, TN)
                dq.append(jnp.where(low, dqs[0], dqs[1]))
            dqrot = jnp.concatenate(dq, axis=1)
            own_q = _rope_bwd(dqrot, cos_c, sin_c)
            dkk = dk[0] + pltpu.roll(dk[1], HEAD_DIM, 1)
            dvv = dv_acc[0] + pltpu.roll(dv_acc[1], HEAD_DIM, 1)
            own_k = _rope_bwd(dkk[BLOCK:], cos_c, sin_c)
            halo_k = _rope_bwd(dkk[:BLOCK], cos_p, sin_p) * keep
            own_v, halo_v = dvv[BLOCK:], dvv[:BLOCK] * keep

            @pl.when(n > 0)
            def _():
                dz_ref[:, GM0:POOL0] = carry_ref[:, GM0:POOL0].astype(BF16)
                dz_ref[:, POOL0:Q0] = (carry_ref[:, POOL0:Q0] + halo_p).astype(BF16)
                dz_ref[:, Q0:KV0] = carry_ref[:, Q0:KV0].astype(BF16)
                dz_ref[:, KV0:KV0 + 128] = (carry_ref[:, KV0:KV0 + 128] + halo_k).astype(BF16)
                dz_ref[:, KV0 + 128:Z_END] = (carry_ref[:, KV0 + 128:Z_END] + halo_v).astype(BF16)

            carry_ref[:, GM0:POOL0] = own_g
            carry_ref[:, POOL0:Q0] = own_p
            carry_ref[:, Q0:KV0] = own_q
            carry_ref[:, KV0:KV0 + 128] = own_k
            carry_ref[:, KV0 + 128:Z_END] = own_v

        @pl.when(n == nb)
        def _():
            dz_ref[...] = carry_ref[...].astype(BF16)

    cur = lambda n: (jnp.minimum(n, nb - 1), 0)
    prev = lambda n: (jnp.maximum(jnp.minimum(n, nb - 1) - 1, 0), 0)
    in_specs = [
        pl.BlockSpec((BLOCK, Z_END), cur),
        pl.BlockSpec((BLOCK, 256), lambda n: (jnp.maximum(jnp.minimum(n, nb - 1) - 1, 0), POOL0 // 256)),
        pl.BlockSpec((BLOCK, 256), lambda n: (jnp.maximum(jnp.minimum(n, nb - 1) - 1, 0), KV0 // 256)),
        pl.BlockSpec((BLOCK, 1024), cur),
        pl.BlockSpec((BLOCK, 128), cur), pl.BlockSpec((BLOCK, 128), prev),
        pl.BlockSpec((BLOCK, 128), cur), pl.BlockSpec((BLOCK, 128), prev),
        _full((1, 256)), _full((4, BLOCK, BLOCK)), _full((BLOCK, 256)), _full((256, 256)), _full((1, 256)),
        pl.BlockSpec(memory_space=pltpu.SMEM),
    ]
    out_specs = [pl.BlockSpec((BLOCK, Z_END), lambda n: (jnp.maximum(n - 1, 0), 0)),
                 _full((1, 256)), _full((4, BLOCK, BLOCK)), _full((BLOCK, 256)), _full((256, 256)), _full((1, 256)), _full((8, 128))]
    out_shape = [jax.ShapeDtypeStruct((S, Z_END), BF16), jax.ShapeDtypeStruct((1, 256), F32),
                 jax.ShapeDtypeStruct((4, BLOCK, BLOCK), F32), jax.ShapeDtypeStruct((BLOCK, 256), F32),
                 jax.ShapeDtypeStruct((256, 256), F32), jax.ShapeDtypeStruct((1, 256), F32), jax.ShapeDtypeStruct((8, 128), F32)]
    return pl.pallas_call(
        body, name=name, grid=(nb + 1,), in_specs=in_specs, out_specs=out_specs, out_shape=out_shape,
        scratch_shapes=[pltpu.VMEM((BLOCK, Z_END), F32)],
        compiler_params=_params("arbitrary"))(z, z, z, dcat, cos_t, cos_t, sin_t, sin_t, vg, wc, bfull, bd, scale, sinks)


HBM = pl.BlockSpec(memory_space=pl.ANY)


def _place():
    return lax.axis_index("x"), lax.axis_index("y"), lax.axis_index("c")


def _chip(k, x, y):
    return (1 - x if k & 1 else x, 1 - y if k & 2 else y)


def _block_rows(ref, rows, dev):
    start = pl.multiple_of((4 * dev[0] + 2 * dev[1] + dev[2]) * rows, 8)
    return ref.at[pl.ds(start, rows), :]


def _group_size(arrays, group):
    return sum(arrays[w].shape[0] for w in group)


def _all_gather(shards, groups, carrier, *, name):
    n = len(shards)
    assert all(_group_size(shards, g) == carrier.shape[0] for g in groups) and carrier.dtype == shards[0].dtype

    def body(*refs):
        ins, car, outs = refs[:n], refs[n], refs[n + 1:2 * n + 1]
        send_sems, recv_sems, local_sems = refs[2 * n + 1:]
        x, y, c = _place()
        me, sibling = (x, y, c), (x, y, 1 - c)
        chips = [_chip(k, x, y) for k in (1, 2, 3)]

        def copy(gi, w, k, block, to, src=None):
            dst = _block_rows(outs[w], ins[w].shape[0], block)
            return pltpu.make_async_remote_copy(src_ref=dst if src is None else src, dst_ref=dst, send_sem=send_sems.at[gi, k],
                                                recv_sem=recv_sems.at[gi, k], device_id=to, device_id_type=MESH)

        def drained(gi, k):
            return pltpu.make_async_remote_copy(src_ref=car, dst_ref=car, send_sem=send_sems.at[gi, k],
                                                recv_sem=recv_sems.at[gi, k], device_id=me, device_id_type=MESH)

        for gi, group in enumerate(groups):
            for w in group:
                pltpu.make_async_copy(ins[w], _block_rows(outs[w], ins[w].shape[0], me), local_sems.at[gi]).start()
                copy(gi, w, 0, me, sibling, src=ins[w]).start()
                for j, chip in enumerate(chips):
                    copy(gi, w, 1 + j, me, (*chip, c), src=ins[w]).start()
        for gi, group in enumerate(groups):
            for j, chip in enumerate(chips):
                drained(gi, 1 + j).wait_recv()
                for w in group:
                    copy(gi, w, 4 + j, (*chip, c), sibling).start()
        for gi in range(len(groups)):
            drained(gi, 0).wait_recv()
            for j in range(3):
                drained(gi, 4 + j).wait_recv()
        for gi in range(len(groups)):
            for k in range(7):
                drained(gi, k).wait_send()
            pltpu.make_async_copy(car, car, local_sems.at[gi]).wait()

    g = len(groups)
    return pl.pallas_call(
        body, name=name, in_specs=[HBM] * (n + 1), out_specs=[HBM] * n,
        out_shape=[jax.ShapeDtypeStruct((N_DEV * s.shape[0], s.shape[1]), s.dtype) for s in shards],
        scratch_shapes=[pltpu.SemaphoreType.DMA((g, 7)), pltpu.SemaphoreType.DMA((g, 7)), pltpu.SemaphoreType.DMA((g,))],
    )(*shards, carrier)


_HBM_SPEC = pl.BlockSpec(memory_space=pltpu.HBM)
_SEM_SPEC = pl.BlockSpec(memory_space=pltpu.SEMAPHORE)
_SPLIT_PARAMS = dict(compiler_params=pltpu.CompilerParams(has_side_effects=pltpu.SideEffectType.DATAFLOW_SIDE_EFFECTING))


def _descriptors(copies, refs, send_sems, recv_sems):
    return [pltpu.make_async_remote_copy(src_ref=s, dst_ref=d, send_sem=send_sems.at[i], recv_sem=recv_sems.at[i], device_id=to, device_id_type=MESH)
            for i, (s, d, to) in enumerate(copies(refs))]


def _start(copies, arrays, fresh, n_copies, *, name):
    operands = [pltpu.with_memory_space_constraint(v, pltpu.HBM) for v in (*arrays, *[lax.empty(f.shape, f.dtype) for f in fresh])]
    n = len(operands)

    def body(*refs):
        for dma in _descriptors(copies, refs[:n], refs[n], refs[n + 1]):
            dma.start()
        refs[-1][...] = jnp.zeros_like(refs[-1])

    res = pl.pallas_call(
        body, name=name, in_specs=[_HBM_SPEC] * n,
        out_shape=(pltpu.SemaphoreType.DMA((n_copies,)), pltpu.SemaphoreType.DMA((n_copies,)),
                   *[pltpu.HBM(v.shape, v.dtype) for v in operands], jax.ShapeDtypeStruct((8, 128), F32)),
        out_specs=(_SEM_SPEC, _SEM_SPEC, *[_HBM_SPEC] * n, pl.BlockSpec(memory_space=pltpu.VMEM)),
        input_output_aliases={i: 2 + i for i in range(n)}, **_SPLIT_PARAMS)(*operands)
    return res[0], res[1], list(res[2:2 + n]), res[-1]


def _finish(copies, send_sems, recv_sems, arrays, after, *, name):
    n = len(arrays)

    def body(*refs):
        for dma in _descriptors(copies, refs[:n], refs[n], refs[n + 1]):
            dma.wait_send()
            dma.wait_recv()

    return list(pl.pallas_call(
        body, name=name, in_specs=[_HBM_SPEC] * n + [_SEM_SPEC, _SEM_SPEC] + [HBM] * len(after),
        out_shape=tuple(pltpu.HBM(v.shape, v.dtype) for v in arrays), out_specs=tuple([_HBM_SPEC] * n),
        input_output_aliases={i: i for i in range(n)}, **_SPLIT_PARAMS)(*arrays, send_sems, recv_sems, *after))


def _tie(value, tokens):
    return lax.optimization_barrier((value, *tokens))[0] if tokens else value


def _gather_copies(stage):
    def copies(refs):
        x, y, c = _place()
        me, sibling = (x, y, c), (x, y, 1 - c)
        out = []
        for ref in refs:
            r = ref.shape[0] // N_DEV
            if stage == "a":
                hops = [(me, sibling)] + [(me, (*_chip(k, x, y), c)) for k in (1, 2, 3)]
            else:
                hops = [((*_chip(k, x, y), c), sibling) for k in (1, 2, 3)]
            for block, to in hops:
                rows = _block_rows(ref, r, block)
                out.append((rows, rows, to))
        return out
    return copies


def _scatter_copies(stage, n):
    def copies(refs):
        x, y, c = _place()
        out = []
        for w in range(n):
            src, dst = refs[w], refs[n + w]
            if stage == "a":
                r = src.shape[0] // N_DEV
                out += [(_block_rows(src, r, (*_chip(k, x, y), 1 - c)), dst.at[k], (x, y, 1 - c)) for k in range(4)]
            else:
                out += [(src.at[k - 1], dst.at[k - 1], (*_chip(k, x, y), c)) for k in (1, 2, 3)]
        return out
    return copies


def _place_own(shard, blocks, *, name):
    r, cdim = shard.shape

    def body(blocks_ref, s_ref, o_ref):
        o_ref[...] = s_ref[...].astype(BF16)

    return pl.pallas_call(
        body, name=name,
        grid_spec=pltpu.PrefetchScalarGridSpec(
            num_scalar_prefetch=1, grid=(1,), in_specs=[pl.BlockSpec((r, cdim), lambda i, blocks: (0, 0))],
            out_specs=pl.BlockSpec((r, cdim), lambda i, blocks: (blocks[0], 0))),
        out_shape=jax.ShapeDtypeStruct((N_DEV * r, cdim), BF16), compiler_params=_params("arbitrary"))(blocks, shard)


def _chip_sum(grad, got, blocks, *, name):
    _, r, cdim = got.shape
    tr = r if r <= 256 else r // (2 if r % 32 == 0 and r // 2 <= 256 else 4)
    steps = r // tr

    def body(blocks_ref, a0_ref, a1_ref, a2_ref, a3_ref, b_ref, mine_ref, away_ref):
        mine_ref[...] = a0_ref[...] + b_ref[0]
        for k, a_ref in ((1, a1_ref), (2, a2_ref), (3, a3_ref)):
            away_ref[k - 1] = (a_ref[...] + b_ref[k]).astype(BF16)

    own = [pl.BlockSpec((tr, cdim), lambda i, blocks, k=k: (blocks[k] * steps + i, 0)) for k in range(4)]
    return pl.pallas_call(
        body, name=name,
        grid_spec=pltpu.PrefetchScalarGridSpec(
            num_scalar_prefetch=1, grid=(steps,),
            in_specs=own + [pl.BlockSpec((4, tr, cdim), lambda i, blocks: (0, i, 0))],
            out_specs=[pl.BlockSpec((tr, cdim), lambda i, blocks: (i, 0)), pl.BlockSpec((3, tr, cdim), lambda i, blocks: (0, i, 0))]),
        out_shape=[jax.ShapeDtypeStruct((r, cdim), F32), jax.ShapeDtypeStruct((3, r, cdim), BF16)],
        compiler_params=_params("arbitrary"))(blocks, grad, grad, grad, grad, got)


def _final_sum(mine, got, *, name):
    r, cdim = mine.shape
    tr = r if r <= 256 else r // (2 if r % 32 == 0 and r // 2 <= 256 else 4)

    def body(m_ref, g_ref, o_ref):
        o_ref[...] = ((m_ref[...] + g_ref[0].astype(F32)) + g_ref[1].astype(F32)) + g_ref[2].astype(F32)

    return pl.pallas_call(
        body, name=name, grid=(r // tr,),
        in_specs=[pl.BlockSpec((tr, cdim), lambda i: (i, 0)), pl.BlockSpec((3, tr, cdim), lambda i: (0, i, 0))],
        out_specs=pl.BlockSpec((tr, cdim), lambda i: (i, 0)), out_shape=jax.ShapeDtypeStruct((r, cdim), F32),
        compiler_params=_params("parallel"))(mine, got)


def _adamw_math(w, g, m, v):
    m = ADAM_B1 * m + (1.0 - ADAM_B1) * g
    v = ADAM_B2 * v + (1.0 - ADAM_B2) * (g * g)
    m_hat = m / (1.0 - ADAM_B1 ** ADAM_STEP)
    v_hat = v / (1.0 - ADAM_B2 ** ADAM_STEP)
    return -ADAM_LR * (m_hat / (jnp.sqrt(v_hat) + ADAM_EPS) + ADAM_WD * w), m, v


def _adamw(w, g, m, v, *, name):
    L, r, cdim = w.shape
    tr = r if r <= 512 else 512

    def body(w_ref, g_ref, m_ref, v_ref, d_ref, nm_ref, nv_ref):
        d_ref[...], nm_ref[...], nv_ref[...] = _adamw_math(w_ref[...], g_ref[...], m_ref[...], v_ref[...])

    blk = pl.BlockSpec((None, tr, cdim), lambda l, i: (l, i, 0))
    return pl.pallas_call(body, name=name, grid=(L, r // tr), in_specs=[blk] * 4, out_specs=[blk] * 3,
                          out_shape=[jax.ShapeDtypeStruct(w.shape, F32)] * 3, compiler_params=_params("parallel", "parallel"))(w, g, m, v)


def _small_update(parts, w, m, v, *, name):
    _, p, lanes = parts.shape
    tp = p // 2 if p % 16 == 0 else p

    def body(p_ref, w_ref, m_ref, v_ref, g_ref, d_ref, nm_ref, nv_ref):
        g = p_ref[0]
        for i in range(1, N_DEV):
            g = g + p_ref[i]
        g_ref[...] = g
        d_ref[...], nm_ref[...], nv_ref[...] = _adamw_math(w_ref[...], g, m_ref[...], v_ref[...])

    blk = pl.BlockSpec((tp, lanes), lambda i: (i, 0))
    return pl.pallas_call(body, name=name, grid=(p // tp,), in_specs=[pl.BlockSpec((N_DEV, tp, lanes), lambda i: (0, i, 0)), blk, blk, blk],
                          out_specs=[blk] * 4, out_shape=[jax.ShapeDtypeStruct((p, lanes), F32)] * 4,
                          compiler_params=_params("parallel"))(parts, w, m, v)


BIG = ("w_in", "w_o", "w_xq", "w_xkv", "w_xo", "w_gate_up", "w_down")
TRANSPOSED = ("w_in", "w_xkv", "w_gate_up")
SMALL = ("mem_norm_g", "mix_pre_g", "mix_post_g", "gm_v_g", "gm_w_s", "gm_b_s", "pool_w", "pool_scale", "attn_sinks",
         "x_pre_g", "x_post_g", "ffn_pre_g", "ffn_post_g")
WEIGHTS = ("mem_norm_g", "mix_pre_g", "mix_post_g", "w_in", "gm_v_g", "gm_w_s", "gm_b_s", "pool_w", "pool_scale", "attn_sinks", "w_o",
           "x_pre_g", "x_post_g", "w_xq", "w_xkv", "w_xo", "ffn_pre_g", "ffn_post_g", "w_gate_up", "w_down")
PACK_QUANTUM = 8 * 128


def _pack(arrays):
    flat = []
    for a in arrays:
        a = a.reshape(-1).astype(F32)
        flat.append(jnp.pad(a, (0, -a.size % PACK_QUANTUM)))
    return jnp.concatenate(flat).reshape(-1, 128)


def _unpack(pack, shapes):
    out, at = [], 0
    flat = pack.reshape(-1)
    for s in shapes:
        size = math.prod(s)
        out.append(flat[at:at + size].reshape(s))
        at += size + (-size % PACK_QUANTUM)
    return out


def _row(v):
    return v.reshape(1, -1)


def _local_step(x, mem, positions, target, small, weight, fwd_hook=None, bwd_hook=None):
    fwd_hook = fwd_hook or (lambda l, point, v: v)
    bwd_hook = bwd_hook or (lambda l, point, v, gb: v)
    depth = small["mix_pre_g"].shape[0]
    half = HEAD_DIM // 2
    inv = ROPE_THETA ** (-jnp.arange(half, dtype=F32) / half)
    cos_t, sin_t = _rope_tables(positions.reshape(-1, 1), jnp.tile(inv, 128 // half).reshape(1, 128), name="rope_tables")
    mem_g = _row(small["mem_norm_g"])
    saved = []
    for l in range(depth):
        w = functools.partial(weight, l)
        bfull = jnp.repeat(small["gm_b_s"][l].T, HEAD_DIM, axis=1)
        bd = jax.scipy.linalg.block_diag(*[small["pool_w"][l, g] for g in range(len(POOL_WINDOWS))])
        mix_par = (_row(small["gm_v_g"][l]), small["gm_w_s"][l], bfull, bd, _row(small["pool_scale"][l]), small["attn_sinks"][l])
        x = fwd_hook(l, 0, x)
        z, h1 = _mm_fwd(x, w("w_in"), nt=True, pre="norm", g_pre=_row(small["mix_pre_g"][l]), out_dtype=F32, name="in_proj")
        cat = fwd_hook(l, 1, _mixer_fwd(z, cos_t, sin_t, *mix_par, name="mixer_fwd"))
        mix, x1 = _mm_fwd(cat, w("w_o"), nt=False, post="norm_res", g_post=_row(small["mix_post_g"][l]), xres=x, name="mix_out")
        qx, h2 = _mm_fwd(x1, w("w_xq"), nt=False, pre="norm", g_pre=_row(small["x_pre_g"][l]), name="xq_proj")
        kv, memn = _mm_fwd(mem, w("w_xkv"), nt=True, pre="norm", g_pre=mem_g, name="xkv_proj")
        ox = _xattn_fwd(qx, kv, name="xattn_fwd")
        xo, x2 = _mm_fwd(ox, w("w_xo"), nt=False, post="norm_res", g_post=_row(small["x_post_g"][l]), xres=x1, name="xattn_out")
        x2 = fwd_hook(l, 2, x2)
        gu, h3 = _mm_fwd(x2, w("w_gate_up"), nt=True, pre="norm", g_pre=_row(small["ffn_pre_g"][l]), name="ffn_in")
        f, x3 = _mm_fwd(gu, w("w_down"), nt=False, pre="swiglu", post="norm_res", g_post=_row(small["ffn_post_g"][l]), xres=x2,
                        tile=256, name="ffn_out")
        saved.append((mix_par, x, z, h1, cat, mix, x1, qx, h2, kv, memn, ox, xo, x2, gu, h3, f))
        x = fwd_hook(l, 3, x3)

    loss, dx = _loss_grad(x, target, name="loss")

    gs = {n: [None] * depth for n in SMALL if n != "mem_norm_g"}
    gb = {n: [None] * depth for n in BIG}
    dmemn = [None] * depth
    for l in reversed(range(depth)):
        w = functools.partial(weight, l)
        mix_par, x0, z, h1, cat, mix, x1, qx, h2, kv, memn, ox, xo, x2, gu, h3, f = saved[l]
        dact, df, gs["ffn_post_g"][l] = _mm_bwd_pre(f, dx, _row(small["ffn_post_g"][l]), w("w_down"), name="ffn_out_bwd")
        gb["w_down"][l] = _wgrad(gu, df, swiglu=True, name="w_down_grad")
        dgu = _swiglu_bwd(dact, gu, name="swiglu_bwd")
        gb["w_gate_up"][l] = _wgrad(dgu, h3, name="w_gate_up_grad")
        dx2, gs["ffn_pre_g"][l] = _mm_bwd_post(dgu, w("w_gate_up"), x2, _row(small["ffn_pre_g"][l]), dx, nt=False, tile=256, name="ffn_in_bwd")
        dx2 = bwd_hook(l, 1, dx2, gb)
        do, dxo, gs["x_post_g"][l] = _mm_bwd_pre(xo, dx2, _row(small["x_post_g"][l]), w("w_xo"), name="xattn_out_bwd")
        gb["w_xo"][l] = _wgrad(ox, dxo, name="w_xo_grad")
        dqx, dkv = _xattn_bwd(qx, kv, do, name="xattn_bwd")
        gb["w_xq"][l] = _wgrad(h2, dqx, name="w_xq_grad")
        dx1, gs["x_pre_g"][l] = _mm_bwd_post(dqx, w("w_xq"), x1, _row(small["x_pre_g"][l]), dx2, nt=True, name="xq_proj_bwd")
        dx1 = bwd_hook(l, 2, dx1, gb)
        gb["w_xkv"][l] = _wgrad(dkv, memn, name="w_xkv_grad")
        dmemn[l] = _mm_fwd(dkv, w("w_xkv"), nt=False, out_dtype=F32, name="xkv_proj_bwd")[0]
        dcat, dmix, gs["mix_post_g"][l] = _mm_bwd_pre(mix, dx1, _row(small["mix_post_g"][l]), w("w_o"), name="mix_out_bwd")
        gb["w_o"][l] = _wgrad(cat, dmix, name="w_o_grad")
        dz, dvg, dwc, dbf, dbd, dscale, dsink = _mixer_bwd(z, dcat, cos_t, sin_t, *mix_par, name="mixer_bwd")
        gs["gm_v_g"][l], gs["gm_w_s"][l], gs["pool_scale"][l], gs["attn_sinks"][l] = dvg, dwc, dscale, dsink[:, 0]
        gs["gm_b_s"][l] = dbf.reshape(BLOCK, -1, HEAD_DIM).sum(-1).T
        gs["pool_w"][l] = jnp.stack([dbd[HEAD_DIM * g:HEAD_DIM * (g + 1), HEAD_DIM * g:HEAD_DIM * (g + 1)] for g in range(len(POOL_WINDOWS))])
        gb["w_in"][l] = _wgrad(dz, h1, name="w_in_grad")
        dx, gs["mix_pre_g"][l] = _mm_bwd_post(dz, w("w_in"), x0, _row(small["mix_pre_g"][l]), dx1, nt=False, name="in_proj_bwd")
        dx = bwd_hook(l, 3, dx, gb)

    small_grads = {n: jnp.stack(v).reshape(small[n].shape) for n, v in gs.items()}
    small_grads["mem_norm_g"] = _norm_dgain(mem, jnp.stack(dmemn), name="mem_norm_grad").reshape(-1)
    return loss, dx, small_grads, gb


EARLY = ("w_in", "w_o", "w_xq", "w_xkv", "w_xo")
LATE = ("w_gate_up", "w_down")


def _step(a):
    depth = a["mix_pre_g"].shape[0]
    x, mem, target = a["x"][0], a["mem"][0], a["loss_target"][0]
    px, py, pc = _place()
    blocks = jnp.stack([4 * cx + 2 * cy + pc for cx, cy in (_chip(k, px, py) for k in range(4))]).astype(jnp.int32)
    flight = {}

    whole = {(l, n): _place_own(a[n][l].T if n in TRANSPOSED else a[n][l], blocks, name="place_" + n) for l in range(depth) for n in BIG}
    gather = {"first_early": [(0, n) for n in EARLY], "first_late": [(0, n) for n in LATE]}
    gather.update({f"layer{l}": [(l, n) for n in BIG] for l in range(1, depth)})
    forward_plan = {(0, 0): [("a", "first_early"), ("mid", "first_early"), ("end", "first_early"), ("a", "first_late"), ("a", "layer1")],
                    (0, 1): [("mid", "first_late")], (0, 2): [("end", "first_late"), ("mid", "layer1")], (0, 3): [("end", "layer1")]}
    for l in range(1, depth):
        forward_plan.update({(l, 0): [("a", f"layer{l + 1}")], (l, 2): [("mid", f"layer{l + 1}")], (l, 3): [("end", f"layer{l + 1}")]})

    def gather_step(kind, tag, v):
        keys = gather[tag]
        if kind == "a":
            send, recv, arrays, token = _start(_gather_copies("a"), [whole[k] for k in keys], [], 4 * len(keys), name=f"gather_a_{tag}")
            flight[tag] = (send, recv, arrays)
            return [token]
        send, recv, arrays = flight.pop(tag)
        if kind == "mid":
            arrays = _finish(_gather_copies("a"), send, recv, arrays, [v], name=f"gather_a_done_{tag}")
            send, recv, arrays, token = _start(_gather_copies("b"), arrays, [], 3 * len(keys), name=f"gather_b_{tag}")
            flight[tag] = (send, recv, arrays)
            return [token]
        for key, arr in zip(keys, _finish(_gather_copies("b"), send, recv, arrays, [v], name=f"gather_b_done_{tag}")):
            whole[key] = arr
        return []

    def fwd_hook(l, point, v):
        for kind, tag in forward_plan.get((l, point), []):
            if tag in gather:
                v = _tie(v, gather_step(kind, tag, v))
        return v

    mine = {}
    scatter = {f"{part}{l}": [(l, n) for n in names] for l in range(depth) for part, names in (("early", EARLY), ("late", LATE))}
    backward_plan = {}
    for l in range(depth):
        backward_plan[(l, 1)] = [("a", f"late{l}")] + ([("mid", f"early{l + 1}")] if l + 1 < depth else [])
        backward_plan[(l, 2)] = [("mid", f"late{l}")] + ([("end", f"early{l + 1}")] if l + 1 < depth else [])
        backward_plan[(l, 3)] = [("end", f"late{l}"), ("a", f"early{l}")]
    backward_plan[(0, 3)] += [("mid", "early0"), ("end", "early0")]

    def scatter_step(kind, tag, v, gb):
        keys = scatter[tag]
        n = len(keys)
        if kind == "a":
            grads = [gb[name][l] for l, name in keys]
            zones = [jax.ShapeDtypeStruct((4, g.shape[0] // N_DEV, g.shape[1]), F32) for g in grads]
            send, recv, arrays, token = _start(_scatter_copies("a", n), grads, zones, 4 * n, name=f"scatter_a_{tag}")
            flight[tag] = (send, recv, arrays, None)
            return [token]
        send, recv, arrays, kept = flight.pop(tag)
        if kind == "mid":
            arrays = _finish(_scatter_copies("a", n), send, recv, arrays, [v], name=f"scatter_a_done_{tag}")
            sums = [_chip_sum(arrays[i], arrays[n + i], blocks, name="chip_sum") for i in range(n)]
            zones = [jax.ShapeDtypeStruct(s[1].shape, BF16) for s in sums]
            send, recv, arrays, token = _start(_scatter_copies("b", n), [s[1] for s in sums], zones, 3 * n, name=f"scatter_b_{tag}")
            flight[tag] = (send, recv, arrays, [s[0] for s in sums])
            return [token]
        arrays = _finish(_scatter_copies("b", n), send, recv, arrays, [v], name=f"scatter_b_done_{tag}")
        for i, key in enumerate(keys):
            mine[key] = _final_sum(kept[i], arrays[n + i], name="final_sum")
        return []

    def bwd_hook(l, point, v, gb):
        for kind, tag in backward_plan.get((l, point), []):
            v = _tie(v, scatter_step(kind, tag, v, gb))
        return v

    loss, dx, small_grads, _ = _local_step(x, mem, a["positions"][0], target, {n: a[n] for n in SMALL}, lambda l, n: whole[(l, n)],
                                           fwd_hook, bwd_hook)

    out = {}
    for n in BIG:
        g = jnp.stack([mine[(l, n)].T if n in TRANSPOSED else mine[(l, n)] for l in range(depth)])
        out["grad_" + n] = g
        out["delta_" + n], out["new_m_" + n], out["new_v_" + n] = _adamw(a[n], g, a["m_" + n], a["v_" + n], name="adamw_" + n)

    part = _pack([small_grads[n] for n in SMALL])
    parts = _all_gather([part], [[0]], part, name="gather_small_grads")[0].reshape(N_DEV, *part.shape)
    packs = _small_update(parts, *[_pack([a[p + n] for n in SMALL]) for p in ("", "m_", "v_")], name="small_update")
    for p, pack in zip(("grad_", "delta_", "new_m_", "new_v_"), packs):
        for n, v in zip(SMALL, _unpack(pack, [a[n].shape for n in SMALL])):
            out[p + n] = v

    total = lax.psum(loss[0, 0], ("x", "y", "c"))
    return (total, dx[None], *[out[p + n] for p in ("grad_", "delta_", "new_m_", "new_v_") for n in WEIGHTS])


def kernel(x, mem, positions, mem_norm_g, mix_pre_g, mix_post_g, w_in, gm_v_g, gm_w_s, gm_b_s, pool_w, pool_scale, attn_sinks,
           w_o, x_pre_g, x_post_g, w_xq, w_xkv, w_xo, ffn_pre_g, ffn_post_g, w_gate_up, w_down, loss_target, m_mem_norm_g,
           m_mix_pre_g, m_mix_post_g, m_w_in, m_gm_v_g, m_gm_w_s, m_gm_b_s, m_pool_w, m_pool_scale, m_attn_sinks, m_w_o,
           m_x_pre_g, m_x_post_g, m_w_xq, m_w_xkv, m_w_xo, m_ffn_pre_g, m_ffn_post_g, m_w_gate_up, m_w_down, v_mem_norm_g,
           v_mix_pre_g, v_mix_post_g, v_w_in, v_gm_v_g, v_gm_w_s, v_gm_b_s, v_pool_w, v_pool_scale, v_attn_sinks, v_w_o,
           v_x_pre_g, v_x_post_g, v_w_xq, v_w_xkv, v_w_xo, v_ffn_pre_g, v_ffn_post_g, v_w_gate_up, v_w_down):
    return _step(dict(locals()))
```

```python
import functools
import math

import jax
import jax.numpy as jnp
from jax import lax
from jax.experimental import pallas as pl
from jax.experimental.pallas import tpu as pltpu

F32, BF16 = jnp.float32, jnp.bfloat16
EPS = 1e-6
HEAD_DIM = 64
BLOCK = 128
POOL_WINDOWS = (2, 4, 8, 16)
ATT_Q_HEADS = 8
X_HEADS = 4
ROPE_THETA = 10000.0
ADAM_LR, ADAM_B1, ADAM_B2, ADAM_EPS, ADAM_WD, ADAM_STEP = 0.001, 0.9, 0.999, 1e-08, 0.01, 10
N_DEV = 8
TOKEN_TILE = 512
VMEM_LIMIT_BYTES = 50 * 2**20
NEG = -1e30
MESH = pl.DeviceIdType.MESH

NN = ((1,), (0,))
NT = ((1,), (1,))
TN = ((0,), (0,))


def _dot(a, b, dims):
    return lax.dot_general(a.astype(BF16), b.astype(BF16), (dims, ((), ())), preferred_element_type=F32)


def _params(*sem):
    return pltpu.CompilerParams(dimension_semantics=sem, vmem_limit_bytes=VMEM_LIMIT_BYTES)


def _full(shape):
    return pl.BlockSpec(shape, lambda *_: (0,) * len(shape))


def _lane(shape):
    return lax.broadcasted_iota(jnp.int32, shape, len(shape) - 1)


def _rms_fwd(x, g):
    r = lax.rsqrt(jnp.mean(x * x, axis=-1, keepdims=True) + EPS)
    return x * r * g


def _rms_bwd(x, g, dy):
    r = lax.rsqrt(jnp.mean(x * x, axis=-1, keepdims=True) + EPS)
    xh = x * r
    dg = jnp.sum(dy * xh, axis=0, keepdims=True)
    dxh = dy * g
    dx = r * (dxh - xh * jnp.mean(dxh * xh, axis=-1, keepdims=True))
    return dx, dg


def _silu_parts(gate):
    sg = 1.0 / (1.0 + jnp.exp(-gate))
    return gate * sg, sg


def _swiglu_cols(gu_ref, out_ref, width):
    step = 128 * max(1, (width // 128) // 4)
    for c0 in range(0, width, step):
        c1 = min(width, c0 + step)
        gate = gu_ref[:, c0:c1].astype(F32)
        up = gu_ref[:, width + c0:width + c1].astype(F32)
        out_ref[:, c0:c1] = (_silu_parts(gate)[0] * up).astype(out_ref.dtype)


def _col_tile(n, cap=1408):
    best = n
    for t in range(128, cap + 1, 128):
        if n % t == 0:
            best = t
    return best if n > cap else n


def _mm_fwd(a, w, *, nt, name, pre=None, g_pre=None, post=None, g_post=None, xres=None, out_dtype=BF16, tile=TOKEN_TILE):
    S = a.shape[0]
    n_out, k = (w.shape[0], w.shape[1]) if nt else (w.shape[1], w.shape[0])
    T = min(tile, S)
    tn = n_out if post else _col_tile(n_out)
    grid = (S // T, n_out // tn)

    def body(*refs):
        it = iter(refs)
        a_ref, w_ref = next(it), next(it)
        gpre_ref = next(it) if pre == "norm" else None
        gpost_ref, x_ref = (next(it), next(it)) if post else (None, None)
        outs = [next(it) for _ in range((2 if post else 1) + (1 if pre == "norm" else 0))]
        as_ref = next(it) if pre else None
        if pre:
            @pl.when(pl.program_id(1) == 0)
            def _():
                if pre == "norm":
                    h = _rms_fwd(a_ref[...], gpre_ref[...]).astype(BF16)
                    outs[-1][...] = h
                    as_ref[...] = h
                else:
                    _swiglu_cols(a_ref, as_ref, k)
            av = as_ref[...]
        else:
            av = a_ref[...]
        acc = _dot(av, w_ref[...], NT if nt else NN)
        if post:
            outs[0][...] = acc
            outs[1][...] = x_ref[...] + _rms_fwd(acc, gpost_ref[...])
        else:
            outs[0][...] = acc.astype(out_dtype)

    in_specs = [pl.BlockSpec((T, a.shape[1]), lambda i, j: (i, 0)),
                pl.BlockSpec((tn, k), lambda i, j: (j, 0)) if nt else pl.BlockSpec((k, tn), lambda i, j: (0, j))]
    args = [a, w]
    if pre == "norm":
        in_specs.append(_full((1, k)))
        args.append(g_pre)
    if post:
        in_specs += [_full((1, n_out)), pl.BlockSpec((T, n_out), lambda i, j: (i, 0))]
        args += [g_post, xres]
    out_block = pl.BlockSpec((T, tn), lambda i, j: (i, j))
    if post:
        out_shape = [jax.ShapeDtypeStruct((S, n_out), F32)] * 2
        out_specs = [out_block, out_block]
    else:
        out_shape = [jax.ShapeDtypeStruct((S, n_out), out_dtype)]
        out_specs = [out_block]
    if pre == "norm":
        out_shape.append(jax.ShapeDtypeStruct((S, k), BF16))
        out_specs.append(pl.BlockSpec((T, k), lambda i, j: (i, 0)))
    scratch = [pltpu.VMEM((T, k), BF16)] if pre else []
    return pl.pallas_call(body, name=name, grid=grid, in_specs=in_specs, out_specs=out_specs, out_shape=out_shape,
                          scratch_shapes=scratch, compiler_params=_params("parallel", "arbitrary"))(*args)


def _mm_bwd_pre(m, dx, g, w, *, name, tile=TOKEN_TILE):
    S, n = m.shape
    ku = w.shape[0]
    T = min(tile, S)
    tn = _col_tile(ku)
    grid = (S // T, ku // tn)

    def body(m_ref, dx_ref, g_ref, w_ref, du_ref, dm_ref, dg_ref, as_ref):
        i, j = pl.program_id(0), pl.program_id(1)

        @pl.when((i == 0) & (j == 0))
        def _():
            dg_ref[...] = jnp.zeros_like(dg_ref)

        @pl.when(j == 0)
        def _():
            dmv, dg = _rms_bwd(m_ref[...], g_ref[...], dx_ref[...])
            dg_ref[...] += dg
            dmb = dmv.astype(BF16)
            dm_ref[...] = dmb
            as_ref[...] = dmb

        du_ref[...] = _dot(as_ref[...], w_ref[...], NT).astype(BF16)

    row = pl.BlockSpec((T, n), lambda i, j: (i, 0))
    return pl.pallas_call(
        body, name=name, grid=grid,
        in_specs=[row, row, _full((1, n)), pl.BlockSpec((tn, n), lambda i, j: (j, 0))],
        out_specs=[pl.BlockSpec((T, tn), lambda i, j: (i, j)), row, _full((1, n))],
        out_shape=[jax.ShapeDtypeStruct((S, ku), BF16), jax.ShapeDtypeStruct((S, n), BF16), jax.ShapeDtypeStruct((1, n), F32)],
        scratch_shapes=[pltpu.VMEM((T, n), BF16)], compiler_params=_params("arbitrary", "arbitrary"))(m, dx, g, w)


def _mm_bwd_post(dy, w, x, g, dx_in, *, nt, name, tile=TOKEN_TILE):
    S, k = dy.shape
    n = x.shape[1]
    T = min(tile, S)

    def body(dy_ref, w_ref, x_ref, g_ref, dxi_ref, dxo_ref, dg_ref):
        @pl.when(pl.program_id(0) == 0)
        def _():
            dg_ref[...] = jnp.zeros_like(dg_ref)

        dh = _dot(dy_ref[...], w_ref[...], NT if nt else NN)
        dxn, dg = _rms_bwd(x_ref[...], g_ref[...], dh)
        dg_ref[...] += dg
        dxo_ref[...] = dxi_ref[...] + dxn

    row = pl.BlockSpec((T, n), lambda i: (i, 0))
    return pl.pallas_call(
        body, name=name, grid=(S // T,),
        in_specs=[pl.BlockSpec((T, k), lambda i: (i, 0)), _full(w.shape), row, _full((1, n)), row],
        out_specs=[row, _full((1, n))],
        out_shape=[jax.ShapeDtypeStruct((S, n), F32), jax.ShapeDtypeStruct((1, n), F32)],
        compiler_params=_params("arbitrary"))(dy, w, x, g, dx_in)


def _wgrad(a, b, *, name, swiglu=False, tile=TOKEN_TILE):
    S = a.shape[0]
    k = a.shape[1] // 2 if swiglu else a.shape[1]
    n = b.shape[1]
    T = min(tile, S)
    tk = 512 if k % 512 == 0 else 256
    nk = k // tk

    def body(*refs):
        if swiglu:
            gate_ref, up_ref, b_ref, o_ref, act_ref = refs
            for r0 in range(0, S, T):
                gate, up = gate_ref[r0:r0 + T, :].astype(F32), up_ref[r0:r0 + T, :].astype(F32)
                act_ref[r0:r0 + T, :] = (_silu_parts(gate)[0] * up).astype(BF16)
            av = act_ref[...]
        else:
            a_ref, b_ref, o_ref = refs
            av = a_ref[...]
        o_ref[...] = _dot(av, b_ref[...], TN)

    in_specs = [pl.BlockSpec((S, tk), lambda kk: (0, kk))]
    args = [a]
    if swiglu:
        in_specs.append(pl.BlockSpec((S, tk), lambda kk: (0, kk + nk)))
        args.append(a)
    in_specs.append(_full((S, n)))
    args.append(b)
    return pl.pallas_call(body, name=name, grid=(nk,), in_specs=in_specs,
                          out_specs=pl.BlockSpec((tk, n), lambda kk: (kk, 0)),
                          out_shape=jax.ShapeDtypeStruct((k, n), F32),
                          scratch_shapes=[pltpu.VMEM((S, tk), BF16)] if swiglu else [],
                          compiler_params=_params("parallel"))(*args)


def _swiglu_bwd(dact, gu, *, name, tile=256):
    S, f = dact.shape
    T = min(tile, S)

    def body(da_ref, gu_ref, o_ref):
        step = 128 * max(1, (f // 128) // 4)
        for c0 in range(0, f, step):
            c1 = min(f, c0 + step)
            gate = gu_ref[:, c0:c1].astype(F32)
            up = gu_ref[:, f + c0:f + c1].astype(F32)
            da = da_ref[:, c0:c1].astype(F32)
            sl, sg = _silu_parts(gate)
            o_ref[:, c0:c1] = (da * up * (sg + sl * (1.0 - sg))).astype(BF16)
            o_ref[:, f + c0:f + c1] = (da * sl).astype(BF16)

    return pl.pallas_call(
        body, name=name, grid=(S // T,),
        in_specs=[pl.BlockSpec((T, f), lambda i: (i, 0)), pl.BlockSpec((T, 2 * f), lambda i: (i, 0))],
        out_specs=pl.BlockSpec((T, 2 * f), lambda i: (i, 0)),
        out_shape=jax.ShapeDtypeStruct((S, 2 * f), BF16), compiler_params=_params("parallel"))(dact, gu)


def _loss_grad(y, target, *, name, tile=TOKEN_TILE):
    S, d = y.shape
    T = min(tile, S)
    steps = S // T

    def body(y_ref, t_ref, loss_ref, dy_ref, acc_ref):
        i = pl.program_id(0)

        @pl.when(i == 0)
        def _():
            acc_ref[...] = jnp.zeros_like(acc_ref)

        diff = y_ref[...] - t_ref[...]
        dy_ref[...] = diff * (1.0 / d)
        acc_ref[...] += jnp.sum(diff * diff, axis=0, keepdims=True)

        @pl.when(i == steps - 1)
        def _():
            total = jnp.sum(acc_ref[...], axis=1, keepdims=True) * (0.5 / d)
            loss_ref[...] = jnp.broadcast_to(total, loss_ref.shape)

    row = pl.BlockSpec((T, d), lambda i: (i, 0))
    return pl.pallas_call(
        body, name=name, grid=(steps,), in_specs=[row, row], out_specs=[_full((1, 128)), row],
        out_shape=[jax.ShapeDtypeStruct((1, 128), F32), jax.ShapeDtypeStruct((S, d), F32)],
        scratch_shapes=[pltpu.VMEM((1, d), F32)], compiler_params=_params("arbitrary"))(y, target)


def _norm_dgain(x, dys, *, name):
    def body(x_ref, dy_ref, o_ref):
        xv = x_ref[...]
        dy = dy_ref[0]
        for l in range(1, dys.shape[0]):
            dy = dy + dy_ref[l]
        r = lax.rsqrt(jnp.mean(xv * xv, axis=-1, keepdims=True) + EPS)
        o_ref[...] = jnp.sum(dy * xv * r, axis=0, keepdims=True)

    return pl.pallas_call(body, name=name, out_shape=jax.ShapeDtypeStruct((1, x.shape[1]), F32),
                          compiler_params=pltpu.CompilerParams(vmem_limit_bytes=VMEM_LIMIT_BYTES))(x, dys)


def _xattn_probs(q_ref, kv_ref, h, d, hd):
    q = q_ref[:, h * hd:(h + 1) * hd]
    kh = kv_ref[:, h * hd:(h + 1) * hd]
    vh = kv_ref[:, d + h * hd:d + (h + 1) * hd]
    s = _dot(q, kh, NT) * (hd ** -0.5)
    e = jnp.exp(s - jnp.max(s, axis=-1, keepdims=True))
    return q, kh, vh, e / jnp.sum(e, axis=-1, keepdims=True)


def _xattn_fwd(q, kv, *, name, tile=TOKEN_TILE):
    S, d = q.shape
    mlen = kv.shape[0]
    hd = d // X_HEADS
    T = min(tile, S)

    def body(q_ref, kv_ref, o_ref):
        for h in range(X_HEADS):
            _, _, vh, p = _xattn_probs(q_ref, kv_ref, h, d, hd)
            o_ref[:, h * hd:(h + 1) * hd] = _dot(p, vh, NN).astype(BF16)

    row = pl.BlockSpec((T, d), lambda i: (i, 0))
    return pl.pallas_call(body, name=name, grid=(S // T,), in_specs=[row, _full((mlen, 2 * d))], out_specs=row,
                          out_shape=jax.ShapeDtypeStruct((S, d), BF16), compiler_params=_params("parallel"))(q, kv)


def _xattn_bwd(q, kv, do, *, name, tile=TOKEN_TILE):
    S, d = q.shape
    mlen = kv.shape[0]
    hd = d // X_HEADS
    T = min(tile, S)

    def body(q_ref, kv_ref, do_ref, dq_ref, dkv_ref):
        @pl.when(pl.program_id(0) == 0)
        def _():
            dkv_ref[...] = jnp.zeros_like(dkv_ref)

        for h in range(X_HEADS):
            qh, kh, vh, p = _xattn_probs(q_ref, kv_ref, h, d, hd)
            doh = do_ref[:, h * hd:(h + 1) * hd]
            dp = _dot(doh, vh, NT)
            ds = p * (dp - jnp.sum(p * dp, axis=-1, keepdims=True)) * (hd ** -0.5)
            dq_ref[:, h * hd:(h + 1) * hd] = _dot(ds, kh, NN).astype(BF16)
            dkv_ref[:, h * hd:(h + 1) * hd] += _dot(ds, qh, TN)
            dkv_ref[:, d + h * hd:d + (h + 1) * hd] += _dot(p, doh, TN)

    row = pl.BlockSpec((T, d), lambda i: (i, 0))
    return pl.pallas_call(
        body, name=name, grid=(S // T,), in_specs=[row, _full((mlen, 2 * d)), row],
        out_specs=[row, _full((mlen, 2 * d))],
        out_shape=[jax.ShapeDtypeStruct((S, d), BF16), jax.ShapeDtypeStruct((mlen, 2 * d), F32)],
        compiler_params=_params("arbitrary"))(q, kv, do)


GM0, POOL0, Q0, KV0, Z_END = 0, 512, 768, 1280, 1536


def _rope_tables(pos_col, inv_row, *, name):
    S = pos_col.shape[0]

    def body(p_ref, inv_ref, cos_ref, sin_ref):
        ang = p_ref[...].astype(F32) * inv_ref[...]
        sin = jnp.sin(ang)
        cos_ref[...] = jnp.cos(ang)
        sin_ref[...] = jnp.where(_lane(ang.shape) % HEAD_DIM < HEAD_DIM // 2, -sin, sin)

    T = min(TOKEN_TILE, S)
    return pl.pallas_call(
        body, name=name, grid=(S // T,),
        in_specs=[pl.BlockSpec((T, 1), lambda i: (i, 0)), _full((1, 128))],
        out_specs=[pl.BlockSpec((T, 128), lambda i: (i, 0))] * 2,
        out_shape=[jax.ShapeDtypeStruct((S, 128), F32)] * 2, compiler_params=_params("parallel"))(pos_col, inv_row)


def _swap_halves(x):
    n = x.shape[-1]
    return jnp.where(_lane(x.shape) % HEAD_DIM < HEAD_DIM // 2, pltpu.roll(x, n - HEAD_DIM // 2, 1), pltpu.roll(x, HEAD_DIM // 2, 1))


def _rope(x, cos, sin_s):
    reps = x.shape[-1] // 128
    if reps > 1:
        cos, sin_s = jnp.tile(cos, (1, reps)), jnp.tile(sin_s, (1, reps))
    return x * cos + _swap_halves(x) * sin_s


def _rope_bwd(dy, cos, sin_s):
    reps = dy.shape[-1] // 128
    if reps > 1:
        cos, sin_s = jnp.tile(cos, (1, reps)), jnp.tile(sin_s, (1, reps))
    return dy * cos + _swap_halves(dy * sin_s)


def _gelu_parts(x):
    c = math.sqrt(2.0 / math.pi)
    t = jnp.tanh(c * (x + 0.044715 * x * x * x))
    return 0.5 * x * (1.0 + t), t


def _gelu_grad(x, t):
    c = math.sqrt(2.0 / math.pi)
    return 0.5 * (1.0 + t) + 0.5 * x * (1.0 - t * t) * c * (1.0 + 3.0 * 0.044715 * x * x)


def _group_sum(x):
    gid = _lane(x.shape) // HEAD_DIM
    out = jnp.zeros_like(x)
    for g in range(x.shape[-1] // HEAD_DIM):
        sel = gid == g
        out = jnp.where(sel, jnp.sum(jnp.where(sel, x, 0.0), axis=-1, keepdims=True), out)
    return out


def _low_half(shape):
    return _lane(shape) % 128 < HEAD_DIM


def _tril(w):
    r = lax.broadcasted_iota(jnp.int32, w.shape, 0)
    c = lax.broadcasted_iota(jnp.int32, w.shape, 1)
    return jnp.where(r >= c, w, 0.0)


def _gating_fwd(zg, vg, wc_ref, bfull):
    ge, th = _gelu_parts(zg)
    u, v = ge[:, :256], ge[:, 256:]
    r = lax.rsqrt(_group_sum(v * v) * (1.0 / HEAD_DIM) + EPS)
    vhat = v * r
    vn = vhat * vg
    low = _low_half((BLOCK, 128))
    blocks = []
    for blk in range(2):
        vb = vn[:, 128 * blk:128 * (blk + 1)]
        m0 = _dot(_tril(wc_ref[2 * blk]), vb, NN)
        m1 = _dot(_tril(wc_ref[2 * blk + 1]), vb, NN)
        blocks.append(jnp.where(low, m0, m1))
    mixed = jnp.concatenate(blocks, axis=1) + bfull
    return u * mixed, (th, u, r, vhat, vn, mixed)


def _band(w, transpose):
    if transpose:
        s2 = lax.broadcasted_iota(jnp.int32, (2 * BLOCK, BLOCK), 0)
        t = lax.broadcasted_iota(jnp.int32, (2 * BLOCK, BLOCK), 1)
    else:
        t = lax.broadcasted_iota(jnp.int32, (BLOCK, 2 * BLOCK), 0)
        s2 = lax.broadcasted_iota(jnp.int32, (BLOCK, 2 * BLOCK), 1)
    d = t - s2 + BLOCK
    return jnp.where((d >= 0) & (d < w), 1.0, 0.0).astype(BF16)


def _exact_dot(b01, x):
    hi = x.astype(BF16)
    lo = (x - hi.astype(F32)).astype(BF16)
    return _dot(b01, hi, NN) + _dot(b01, lo, NN)


def _inv_count(n):
    t = lax.broadcasted_iota(jnp.int32, (BLOCK, 256), 0) + n * BLOCK
    gid = _lane((BLOCK, 256)) // HEAD_DIM
    win = jnp.where(gid == 0, POOL_WINDOWS[0], jnp.where(gid == 1, POOL_WINDOWS[1], jnp.where(gid == 2, POOL_WINDOWS[2], POOL_WINDOWS[3])))
    return 1.0 / jnp.minimum(t + 1, win).astype(F32)


def _pool_fwd(p, pp, n, bd, scale):
    low = _low_half((BLOCK, 128))
    blocks = []
    for blk in range(2):
        both = jnp.concatenate([pp[:, 128 * blk:128 * (blk + 1)], p[:, 128 * blk:128 * (blk + 1)]], axis=0)
        s0 = _exact_dot(_band(POOL_WINDOWS[2 * blk], False), both)
        s1 = _exact_dot(_band(POOL_WINDOWS[2 * blk + 1], False), both)
        blocks.append(jnp.where(low, s0, s1))
    inv = _inv_count(n)
    pooled = jnp.concatenate(blocks, axis=1) * inv - p
    mapped = _dot(pooled, bd, NN)
    return mapped * scale, (inv, pooled, mapped)


def _attn_mask(first_key):
    qi = lax.broadcasted_iota(jnp.int32, (BLOCK, 2 * BLOCK), 0)
    ki = lax.broadcasted_iota(jnp.int32, (BLOCK, 2 * BLOCK), 1)
    return (ki > qi) & (ki <= qi + BLOCK) & (ki >= first_key)


def _attn_head(j, qrot, kk, kks, vv, vvs, sinks_ref, valid):
    half, kv_head = j % 2, j // 4
    mine = (_lane((BLOCK, 128)) >= HEAD_DIM) if half else (_lane((BLOCK, 128)) < HEAD_DIM)
    swapped = kv_head != half
    kx, vx = (kks, vvs) if swapped else (kk, vv)
    qm = jnp.where(mine, qrot[:, 128 * (j // 2):128 * (j // 2 + 1)], 0.0).astype(BF16)
    sc = jnp.where(valid, _dot(qm, kx, NT) * (HEAD_DIM ** -0.5), NEG)
    sink = sinks_ref[j]
    mx = jnp.maximum(jnp.max(sc, axis=-1, keepdims=True), sink)
    e = jnp.exp(sc - mx)
    es = jnp.exp(sink - mx)
    den = jnp.sum(e, axis=-1, keepdims=True) + es
    return e / den, es / den, qm, kx, vx, mine, swapped


def _mixer_fwd(z, cos_t, sin_t, vg, wc, bfull, bd, scale, sinks, *, name):
    S = z.shape[0]
    nb = S // BLOCK

    def body(zc_ref, zpp_ref, zpk_ref, cc_ref, cp_ref, sc_ref, sp_ref, vg_ref, wc_ref, bf_ref, bd_ref, scale_ref, sinks_ref, out_ref):
        n = pl.program_id(0)
        keep = (n > 0).astype(F32)
        a, _ = _gating_fwd(zc_ref[:, GM0:POOL0], vg_ref[...], wc_ref, bf_ref[...])
        out_ref[:, 0:256] = a.astype(BF16)
        b, _ = _pool_fwd(zc_ref[:, POOL0:Q0], zpp_ref[...] * keep, n, bd_ref[...], scale_ref[...])
        out_ref[:, 256:512] = b.astype(BF16)
        cos_c, sin_c = cc_ref[...], sc_ref[...]
        qrot = _rope(zc_ref[:, Q0:KV0], cos_c, sin_c)
        kk = jnp.concatenate([_rope(zpk_ref[:, 0:128], cp_ref[...], sp_ref[...]), _rope(zc_ref[:, KV0:KV0 + 128], cos_c, sin_c)], axis=0)
        vv = jnp.concatenate([zpk_ref[:, 128:256], zc_ref[:, KV0 + 128:Z_END]], axis=0)
        kks, vvs = pltpu.roll(kk, HEAD_DIM, 1).astype(BF16), pltpu.roll(vv, HEAD_DIM, 1).astype(BF16)
        kk, vv = kk.astype(BF16), vv.astype(BF16)
        valid = _attn_mask(jnp.where(n > 0, 0, BLOCK))
        low = _low_half((BLOCK, 128))
        for s in range(ATT_Q_HEADS // 2):
            o = []
            for j in (2 * s, 2 * s + 1):
                p, _, _, _, vx, _, _ = _attn_head(j, qrot, kk, kks, vv, vvs, sinks_ref, valid)
                o.append(_dot(p, vx, NN))
            out_ref[:, 512 + 128 * s:512 + 128 * (s + 1)] = jnp.where(low, o[0], o[1]).astype(BF16)

    cur = lambda n: (n, 0)
    prev = lambda n: (jnp.maximum(n - 1, 0), 0)
    in_specs = [
        pl.BlockSpec((BLOCK, Z_END), cur),
        pl.BlockSpec((BLOCK, 256), lambda n: (jnp.maximum(n - 1, 0), POOL0 // 256)),
        pl.BlockSpec((BLOCK, 256), lambda n: (jnp.maximum(n - 1, 0), KV0 // 256)),
        pl.BlockSpec((BLOCK, 128), cur), pl.BlockSpec((BLOCK, 128), prev),
        pl.BlockSpec((BLOCK, 128), cur), pl.BlockSpec((BLOCK, 128), prev),
        _full((1, 256)), _full((4, BLOCK, BLOCK)), _full((BLOCK, 256)), _full((256, 256)), _full((1, 256)),
        pl.BlockSpec(memory_space=pltpu.SMEM),
    ]
    return pl.pallas_call(
        body, name=name, grid=(nb,), in_specs=in_specs, out_specs=pl.BlockSpec((BLOCK, 1024), cur),
        out_shape=jax.ShapeDtypeStruct((S, 1024), BF16),
        compiler_params=_params("parallel"))(z, z, z, cos_t, cos_t, sin_t, sin_t, vg, wc, bfull, bd, scale, sinks)


def _mixer_bwd(z, dcat, cos_t, sin_t, vg, wc, bfull, bd, scale, sinks, *, name):
    S = z.shape[0]
    nb = S // BLOCK

    def body(zc_ref, zpp_ref, zpk_ref, dc_ref, cc_ref, cp_ref, sc_ref, sp_ref, vg_ref, wc_ref, bf_ref, bd_ref, scale_ref, sinks_ref,
             dz_ref, dvg_ref, dwc_ref, dbf_ref, dbd_ref, dscale_ref, dsink_ref, carry_ref):
        n = pl.program_id(0)

        @pl.when(n == 0)
        def _():
            for ref in (dvg_ref, dwc_ref, dbf_ref, dbd_ref, dscale_ref, dsink_ref):
                ref[...] = jnp.zeros_like(ref)

        @pl.when(n < nb)
        def _():
            keep = (n > 0).astype(F32)
            low = _low_half((BLOCK, 128))
            zg = zc_ref[:, GM0:POOL0]
            vg_v = vg_ref[...]
            _, (th, u, r, vhat, vn, mixed) = _gating_fwd(zg, vg_v, wc_ref, bf_ref[...])
            da = dc_ref[:, 0:256].astype(F32)
            dmixed = da * u
            dbf_ref[...] += dmixed
            dvn = []
            for blk in range(2):
                dmb = dmixed[:, 128 * blk:128 * (blk + 1)]
                vb = vn[:, 128 * blk:128 * (blk + 1)]
                acc = None
                for half in range(2):
                    g = 2 * blk + half
                    dmg = jnp.where(low if half == 0 else ~low, dmb, 0.0)
                    dwc_ref[g] += _tril(_dot(dmg, vb, NT))
                    part = _dot(_tril(wc_ref[g]), dmg, TN)
                    acc = part if acc is None else acc + part
                dvn.append(acc)
            dvn = jnp.concatenate(dvn, axis=1)
            dvg_ref[...] += jnp.sum(dvn * vhat, axis=0, keepdims=True)
            dvhat = dvn * vg_v
            dv = r * (dvhat - vhat * (_group_sum(dvhat * vhat) * (1.0 / HEAD_DIM)))
            dge = jnp.concatenate([da * mixed, dv], axis=1)
            own_g = dge * _gelu_grad(zg, th)
            p = zc_ref[:, POOL0:Q0]
            bd_v, scale_v = bd_ref[...], scale_ref[...]
            _, (inv, pooled, mapped) = _pool_fwd(p, zpp_ref[...] * keep, n, bd_v, scale_v)
            db = dc_ref[:, 256:512].astype(F32)
            dscale_ref[...] += jnp.sum(db * mapped, axis=0, keepdims=True)
            dmapped = db * scale_v
            dbd_ref[...] += _dot(pooled, dmapped, TN)
            dpooled = _dot(dmapped, bd_v, NT)
            dps = dpooled * inv
            back = []
            for blk in range(2):
                dpb = dps[:, 128 * blk:128 * (blk + 1)]
                t0 = _exact_dot(_band(POOL_WINDOWS[2 * blk], True), dpb)
                t1 = _exact_dot(_band(POOL_WINDOWS[2 * blk + 1], True), dpb)
                back.append(jnp.where(jnp.concatenate([low, low], axis=0), t0, t1))
            back = jnp.concatenate(back, axis=1)
            halo_p = back[:BLOCK] * keep
            own_p = back[BLOCK:] - dpooled
            cos_c, sin_c, cos_p, sin_p = cc_ref[...], sc_ref[...], cp_ref[...], sp_ref[...]
            qrot = _rope(zc_ref[:, Q0:KV0], cos_c, sin_c)
            kk = jnp.concatenate([_rope(zpk_ref[:, 0:128], cos_p, sin_p), _rope(zc_ref[:, KV0:KV0 + 128], cos_c, sin_c)], axis=0)
            vv = jnp.concatenate([zpk_ref[:, 128:256], zc_ref[:, KV0 + 128:Z_END]], axis=0)
            kks, vvs = pltpu.roll(kk, HEAD_DIM, 1).astype(BF16), pltpu.roll(vv, HEAD_DIM, 1).astype(BF16)
            kk, vv = kk.astype(BF16), vv.astype(BF16)
            valid = _attn_mask(jnp.where(n > 0, 0, BLOCK))
            dk = [jnp.zeros((2 * BLOCK, 128), F32), jnp.zeros((2 * BLOCK, 128), F32)]
            dv_acc = [jnp.zeros((2 * BLOCK, 128), F32), jnp.zeros((2 * BLOCK, 128), F32)]
            dq = []
            for s in range(ATT_Q_HEADS // 2):
                dqs = []
                dcs = dc_ref[:, 512 + 128 * s:512 + 128 * (s + 1)]
                for j in (2 * s, 2 * s + 1):
                    pr, ps, qm, kx, vx, mine, swapped = _attn_head(j, qrot, kk, kks, vv, vvs, sinks_ref, valid)
                    dom = jnp.where(mine, dcs, jnp.zeros_like(dcs))
                    dp = _dot(dom, vx, NT)
                    dv_acc[swapped] = dv_acc[swapped] + _dot(pr, dom, TN)
                    drow = jnp.sum(pr * dp, axis=-1, keepdims=True)
                    ds = (pr * (dp - drow) * (HEAD_DIM ** -0.5)).astype(BF16)
                    dsink = jnp.sum(ps * drow, axis=0, keepdims=True)
                    dsink_ref[pl.ds(j, 1), :] = dsink_ref[pl.ds(j, 1), :] - jnp.broadcast_to(dsink, (1, 128))
                    dqs.append(_dot(ds, kx, NN))
                    dk[swapped] = dk[swapped] + _dot(ds, qm, TN)
                dq.append(jnp.where(low, dqs[0], dqs[1]))
            dqrot = jnp.concatenate(dq, axis=1)
            own_q = _rope_bwd(dqrot, cos_c, sin_c)
            dkk = dk[0] + pltpu.roll(dk[1], HEAD_DIM, 1)
            dvv = dv_acc[0] + pltpu.roll(dv_acc[1], HEAD_DIM, 1)
            own_k = _rope_bwd(dkk[BLOCK:], cos_c, sin_c)
            halo_k = _rope_bwd(dkk[:BLOCK], cos_p, sin_p) * keep
            own_v, halo_v = dvv[BLOCK:], dvv[:BLOCK] * keep

            @pl.when(n > 0)
            def _():
                dz_ref[:, GM0:POOL0] = carry_ref[:, GM0:POOL0].astype(BF16)
                dz_ref[:, POOL0:Q0] = (carry_ref[:, POOL0:Q0] + halo_p).astype(BF16)
                dz_ref[:, Q0:KV0] = carry_ref[:, Q0:KV0].astype(BF16)
                dz_ref[:, KV0:KV0 + 128] = (carry_ref[:, KV0:KV0 + 128] + halo_k).astype(BF16)
                dz_ref[:, KV0 + 128:Z_END] = (carry_ref[:, KV0 + 128:Z_END] + halo_v).astype(BF16)

            carry_ref[:, GM0:POOL0] = own_g
            carry_ref[:, POOL0:Q0] = own_p
            carry_ref[:, Q0:KV0] = own_q
            carry_ref[:, KV0:KV0 + 128] = own_k
            carry_ref[:, KV0 + 128:Z_END] = own_v

        @pl.when(n == nb)
        def _():
            dz_ref[...] = carry_ref[...].astype(BF16)

    cur = lambda n: (jnp.minimum(n, nb - 1), 0)
    prev = lambda n: (jnp.maximum(jnp.minimum(n, nb - 1) - 1, 0), 0)
    in_specs = [
        pl.BlockSpec((BLOCK, Z_END), cur),
        pl.BlockSpec((BLOCK, 256), lambda n: (jnp.maximum(jnp.minimum(n, nb - 1) - 1, 0), POOL0 // 256)),
        pl.BlockSpec((BLOCK, 256), lambda n: (jnp.maximum(jnp.minimum(n, nb - 1) - 1, 0), KV0 // 256)),
        pl.BlockSpec((BLOCK, 1024), cur),
        pl.BlockSpec((BLOCK, 128), cur), pl.BlockSpec((BLOCK, 128), prev),
        pl.BlockSpec((BLOCK, 128), cur), pl.BlockSpec((BLOCK, 128), prev),
        _full((1, 256)), _full((4, BLOCK, BLOCK)), _full((BLOCK, 256)), _full((256, 256)), _full((1, 256)),
        pl.BlockSpec(memory_space=pltpu.SMEM),
    ]
    out_specs = [pl.BlockSpec((BLOCK, Z_END), lambda n: (jnp.maximum(n - 1, 0), 0)),
                 _full((1, 256)), _full((4, BLOCK, BLOCK)), _full((BLOCK, 256)), _full((256, 256)), _full((1, 256)), _full((8, 128))]
    out_shape = [jax.ShapeDtypeStruct((S, Z_END), BF16), jax.ShapeDtypeStruct((1, 256), F32),
                 jax.ShapeDtypeStruct((4, BLOCK, BLOCK), F32), jax.ShapeDtypeStruct((BLOCK, 256), F32),
                 jax.ShapeDtypeStruct((256, 256), F32), jax.ShapeDtypeStruct((1, 256), F32), jax.ShapeDtypeStruct((8, 128), F32)]
    return pl.pallas_call(
        body, name=name, grid=(nb + 1,), in_specs=in_specs, out_specs=out_specs, out_shape=out_shape,
        scratch_shapes=[pltpu.VMEM((BLOCK, Z_END), F32)],
        compiler_params=_params("arbitrary"))(z, z, z, dcat, cos_t, cos_t, sin_t, sin_t, vg, wc, bfull, bd, scale, sinks)


HBM = pl.BlockSpec(memory_space=pl.ANY)


def _place():
    return lax.axis_index("x"), lax.axis_index("y"), lax.axis_index("c")


def _chip(k, x, y):
    return (1 - x if k & 1 else x, 1 - y if k & 2 else y)


def _block_rows(ref, rows, dev):
    start = pl.multiple_of((4 * dev[0] + 2 * dev[1] + dev[2]) * rows, 8)
    return ref.at[pl.ds(start, rows), :]


def _group_size(arrays, group):
    return sum(arrays[w].shape[0] for w in group)


def _all_gather(shards, groups, carrier, *, name):
    n = len(shards)
    assert all(_group_size(shards, g) == carrier.shape[0] for g in groups) and carrier.dtype == shards[0].dtype

    def body(*refs):
        ins, car, outs = refs[:n], refs[n], refs[n + 1:2 * n + 1]
        send_sems, recv_sems, local_sems = refs[2 * n + 1:]
        x, y, c = _place()
        me, sibling = (x, y, c), (x, y, 1 - c)
        chips = [_chip(k, x, y) for k in (1, 2, 3)]

        def copy(gi, w, k, block, to, src=None):
            dst = _block_rows(outs[w], ins[w].shape[0], block)
            return pltpu.make_async_remote_copy(src_ref=dst if src is None else src, dst_ref=dst, send_sem=send_sems.at[gi, k],
                                                recv_sem=recv_sems.at[gi, k], device_id=to, device_id_type=MESH)

        def drained(gi, k):
            return pltpu.make_async_remote_copy(src_ref=car, dst_ref=car, send_sem=send_sems.at[gi, k],
                                                recv_sem=recv_sems.at[gi, k], device_id=me, device_id_type=MESH)

        for gi, group in enumerate(groups):
            for w in group:
                pltpu.make_async_copy(ins[w], _block_rows(outs[w], ins[w].shape[0], me), local_sems.at[gi]).start()
                copy(gi, w, 0, me, sibling, src=ins[w]).start()
                for j, chip in enumerate(chips):
                    copy(gi, w, 1 + j, me, (*chip, c), src=ins[w]).start()
        for gi, group in enumerate(groups):
            for j, chip in enumerate(chips):
                drained(gi, 1 + j).wait_recv()
                for w in group:
                    copy(gi, w, 4 + j, (*chip, c), sibling).start()
        for gi in range(len(groups)):
            drained(gi, 0).wait_recv()
            for j in range(3):
                drained(gi, 4 + j).wait_recv()
        for gi in range(len(groups)):
            for k in range(7):
                drained(gi, k).wait_send()
            pltpu.make_async_copy(car, car, local_sems.at[gi]).wait()

    g = len(groups)
    return pl.pallas_call(
        body, name=name, in_specs=[HBM] * (n + 1), out_specs=[HBM] * n,
        out_shape=[jax.ShapeDtypeStruct((N_DEV * s.shape[0], s.shape[1]), s.dtype) for s in shards],
        scratch_shapes=[pltpu.SemaphoreType.DMA((g, 7)), pltpu.SemaphoreType.DMA((g, 7)), pltpu.SemaphoreType.DMA((g,))],
    )(*shards, carrier)


_HBM_SPEC = pl.BlockSpec(memory_space=pltpu.HBM)
_SEM_SPEC = pl.BlockSpec(memory_space=pltpu.SEMAPHORE)
_SPLIT_PARAMS = dict(compiler_params=pltpu.CompilerParams(has_side_effects=pltpu.SideEffectType.DATAFLOW_SIDE_EFFECTING))


def _descriptors(copies, refs, send_sems, recv_sems):
    return [pltpu.make_async_remote_copy(src_ref=s, dst_ref=d, send_sem=send_sems.at[i], recv_sem=recv_sems.at[i], device_id=to, device_id_type=MESH)
            for i, (s, d, to) in enumerate(copies(refs))]


def _start(copies, arrays, fresh, n_copies, *, name):
    operands = [pltpu.with_memory_space_constraint(v, pltpu.HBM) for v in (*arrays, *[lax.empty(f.shape, f.dtype) for f in fresh])]
    n = len(operands)

    def body(*refs):
        for dma in _descriptors(copies, refs[:n], refs[n], refs[n + 1]):
            dma.start()
        refs[-1][...] = jnp.zeros_like(refs[-1])

    res = pl.pallas_call(
        body, name=name, in_specs=[_HBM_SPEC] * n,
        out_shape=(pltpu.SemaphoreType.DMA((n_copies,)), pltpu.SemaphoreType.DMA((n_copies,)),
                   *[pltpu.HBM(v.shape, v.dtype) for v in operands], jax.ShapeDtypeStruct((8, 128), F32)),
        out_specs=(_SEM_SPEC, _SEM_SPEC, *[_HBM_SPEC] * n, pl.BlockSpec(memory_space=pltpu.VMEM)),
        input_output_aliases={i: 2 + i for i in range(n)}, **_SPLIT_PARAMS)(*operands)
    return res[0], res[1], list(res[2:2 + n]), res[-1]


def _finish(copies, send_sems, recv_sems, arrays, after, *, name):
    n = len(arrays)

    def body(*refs):
        for dma in _descriptors(copies, refs[:n], refs[n], refs[n + 1]):
            dma.wait_send()
            dma.wait_recv()

    return list(pl.pallas_call(
        body, name=name, in_specs=[_HBM_SPEC] * n + [_SEM_SPEC, _SEM_SPEC] + [HBM] * len(after),
        out_shape=tuple(pltpu.HBM(v.shape, v.dtype) for v in arrays), out_specs=tuple([_HBM_SPEC] * n),
        input_output_aliases={i: i for i in range(n)}, **_SPLIT_PARAMS)(*arrays, send_sems, recv_sems, *after))


def _tie(value, tokens):
    if not tokens:
        return value

    def body(*refs):
        pass

    return pl.pallas_call(body, name="tie", in_specs=[HBM] * (1 + len(tokens)), out_specs=HBM,
                          out_shape=jax.ShapeDtypeStruct(value.shape, value.dtype), input_output_aliases={0: 0})(value, *tokens)


def _gather_copies(stage):
    def copies(refs):
        x, y, c = _place()
        me, sibling = (x, y, c), (x, y, 1 - c)
        out = []
        for ref in refs:
            r = ref.shape[0] // N_DEV
            if stage == "a":
                hops = [(me, sibling)] + [(me, (*_chip(k, x, y), c)) for k in (1, 2, 3)]
            else:
                hops = [((*_chip(k, x, y), c), sibling) for k in (1, 2, 3)]
            for block, to in hops:
                rows = _block_rows(ref, r, block)
                out.append((rows, rows, to))
        return out
    return copies


def _scatter_copies(stage, n):
    def copies(refs):
        x, y, c = _place()
        out = []
        for w in range(n):
            src, dst = refs[w], refs[n + w]
            if stage == "a":
                r = src.shape[0] // N_DEV
                out += [(_block_rows(src, r, (*_chip(k, x, y), 1 - c)), dst.at[k], (x, y, 1 - c)) for k in range(4)]
            else:
                out += [(src.at[k - 1], dst.at[k - 1], (*_chip(k, x, y), c)) for k in (1, 2, 3)]
        return out
    return copies


def _place_own(shard, blocks, *, name):
    r, cdim = shard.shape

    def body(blocks_ref, s_ref, o_ref):
        o_ref[...] = s_ref[...].astype(BF16)

    return pl.pallas_call(
        body, name=name,
        grid_spec=pltpu.PrefetchScalarGridSpec(
            num_scalar_prefetch=1, grid=(1,), in_specs=[pl.BlockSpec((r, cdim), lambda i, blocks: (0, 0))],
            out_specs=pl.BlockSpec((r, cdim), lambda i, blocks: (blocks[0], 0))),
        out_shape=jax.ShapeDtypeStruct((N_DEV * r, cdim), BF16), compiler_params=_params("arbitrary"))(blocks, shard)


def _chip_sum(grad, got, blocks, *, name):
    _, r, cdim = got.shape
    tr = r if r <= 256 else r // (2 if r % 32 == 0 and r // 2 <= 256 else 4)
    steps = r // tr

    def body(blocks_ref, a0_ref, a1_ref, a2_ref, a3_ref, b_ref, mine_ref, away_ref):
        mine_ref[...] = a0_ref[...] + b_ref[0]
        for k, a_ref in ((1, a1_ref), (2, a2_ref), (3, a3_ref)):
            away_ref[k - 1] = (a_ref[...] + b_ref[k]).astype(BF16)

    own = [pl.BlockSpec((tr, cdim), lambda i, blocks, k=k: (blocks[k] * steps + i, 0)) for k in range(4)]
    return pl.pallas_call(
        body, name=name,
        grid_spec=pltpu.PrefetchScalarGridSpec(
            num_scalar_prefetch=1, grid=(steps,),
            in_specs=own + [pl.BlockSpec((4, tr, cdim), lambda i, blocks: (0, i, 0))],
            out_specs=[pl.BlockSpec((tr, cdim), lambda i, blocks: (i, 0)), pl.BlockSpec((3, tr, cdim), lambda i, blocks: (0, i, 0))]),
        out_shape=[jax.ShapeDtypeStruct((r, cdim), F32), jax.ShapeDtypeStruct((3, r, cdim), BF16)],
        compiler_params=_params("arbitrary"))(blocks, grad, grad, grad, grad, got)


def _final_sum(mine, got, *, name):
    r, cdim = mine.shape
    tr = r if r <= 256 else r // (2 if r % 32 == 0 and r // 2 <= 256 else 4)

    def body(m_ref, g_ref, o_ref):
        o_ref[...] = ((m_ref[...] + g_ref[0].astype(F32)) + g_ref[1].astype(F32)) + g_ref[2].astype(F32)

    return pl.pallas_call(
        body, name=name, grid=(r // tr,),
        in_specs=[pl.BlockSpec((tr, cdim), lambda i: (i, 0)), pl.BlockSpec((3, tr, cdim), lambda i: (0, i, 0))],
        out_specs=pl.BlockSpec((tr, cdim), lambda i: (i, 0)), out_shape=jax.ShapeDtypeStruct((r, cdim), F32),
        compiler_params=_params("parallel"))(mine, got)


def _adamw_math(w, g, m, v):
    m = ADAM_B1 * m + (1.0 - ADAM_B1) * g
    v = ADAM_B2 * v + (1.0 - ADAM_B2) * (g * g)
    m_hat = m / (1.0 - ADAM_B1 ** ADAM_STEP)
    v_hat = v / (1.0 - ADAM_B2 ** ADAM_STEP)
    return -ADAM_LR * (m_hat / (jnp.sqrt(v_hat) + ADAM_EPS) + ADAM_WD * w), m, v


def _adamw(w, g, m, v, *, name):
    L, r, cdim = w.shape
    tr = r if r <= 512 else 512

    def body(w_ref, g_ref, m_ref, v_ref, d_ref, nm_ref, nv_ref):
        d_ref[...], nm_ref[...], nv_ref[...] = _adamw_math(w_ref[...], g_ref[...], m_ref[...], v_ref[...])

    blk = pl.BlockSpec((None, tr, cdim), lambda l, i: (l, i, 0))
    return pl.pallas_call(body, name=name, grid=(L, r // tr), in_specs=[blk] * 4, out_specs=[blk] * 3,
                          out_shape=[jax.ShapeDtypeStruct(w.shape, F32)] * 3, compiler_params=_params("parallel", "parallel"))(w, g, m, v)


def _small_update(parts, w, m, v, *, name):
    _, p, lanes = parts.shape
    tp = p // 2 if p % 16 == 0 else p

    def body(p_ref, w_ref, m_ref, v_ref, g_ref, d_ref, nm_ref, nv_ref):
        g = p_ref[0]
        for i in range(1, N_DEV):
            g = g + p_ref[i]
        g_ref[...] = g
        d_ref[...], nm_ref[...], nv_ref[...] = _adamw_math(w_ref[...], g, m_ref[...], v_ref[...])

    blk = pl.BlockSpec((tp, lanes), lambda i: (i, 0))
    return pl.pallas_call(body, name=name, grid=(p // tp,), in_specs=[pl.BlockSpec((N_DEV, tp, lanes), lambda i: (0, i, 0)), blk, blk, blk],
                          out_specs=[blk] * 4, out_shape=[jax.ShapeDtypeStruct((p, lanes), F32)] * 4,
                          compiler_params=_params("parallel"))(parts, w, m, v)


BIG = ("w_in", "w_o", "w_xq", "w_xkv", "w_xo", "w_gate_up", "w_down")
TRANSPOSED = ("w_in", "w_xkv", "w_gate_up")
SMALL = ("mem_norm_g", "mix_pre_g", "mix_post_g", "gm_v_g", "gm_w_s", "gm_b_s", "pool_w", "pool_scale", "attn_sinks",
         "x_pre_g", "x_post_g", "ffn_pre_g", "ffn_post_g")
WEIGHTS = ("mem_norm_g", "mix_pre_g", "mix_post_g", "w_in", "gm_v_g", "gm_w_s", "gm_b_s", "pool_w", "pool_scale", "attn_sinks", "w_o",
           "x_pre_g", "x_post_g", "w_xq", "w_xkv", "w_xo", "ffn_pre_g", "ffn_post_g", "w_gate_up", "w_down")
PACK_QUANTUM = 8 * 128


def _pack(arrays):
    flat = []
    for a in arrays:
        a = a.reshape(-1).astype(F32)
        flat.append(jnp.pad(a, (0, -a.size % PACK_QUANTUM)))
    return jnp.concatenate(flat).reshape(-1, 128)


def _unpack(pack, shapes):
    out, at = [], 0
    flat = pack.reshape(-1)
    for s in shapes:
        size = math.prod(s)
        out.append(flat[at:at + size].reshape(s))
        at += size + (-size % PACK_QUANTUM)
    return out


def _row(v):
    return v.reshape(1, -1)


def _local_step(x, mem, positions, target, small, weight, fwd_hook=None, bwd_hook=None):
    fwd_hook = fwd_hook or (lambda l, point, v: v)
    bwd_hook = bwd_hook or (lambda l, point, v, gb: v)
    depth = small["mix_pre_g"].shape[0]
    half = HEAD_DIM // 2
    inv = ROPE_THETA ** (-jnp.arange(half, dtype=F32) / half)
    cos_t, sin_t = _rope_tables(positions.reshape(-1, 1), jnp.tile(inv, 128 // half).reshape(1, 128), name="rope_tables")
    mem_g = _row(small["mem_norm_g"])
    saved = []
    for l in range(depth):
        w = functools.partial(weight, l)
        bfull = jnp.repeat(small["gm_b_s"][l].T, HEAD_DIM, axis=1)
        bd = jax.scipy.linalg.block_diag(*[small["pool_w"][l, g] for g in range(len(POOL_WINDOWS))])
        mix_par = (_row(small["gm_v_g"][l]), small["gm_w_s"][l], bfull, bd, _row(small["pool_scale"][l]), small["attn_sinks"][l])
        x = fwd_hook(l, 0, x)
        z, h1 = _mm_fwd(x, w("w_in"), nt=True, pre="norm", g_pre=_row(small["mix_pre_g"][l]), out_dtype=F32, name="in_proj")
        cat = fwd_hook(l, 1, _mixer_fwd(z, cos_t, sin_t, *mix_par, name="mixer_fwd"))
        mix, x1 = _mm_fwd(cat, w("w_o"), nt=False, post="norm_res", g_post=_row(small["mix_post_g"][l]), xres=x, name="mix_out")
        qx, h2 = _mm_fwd(x1, w("w_xq"), nt=False, pre="norm", g_pre=_row(small["x_pre_g"][l]), name="xq_proj")
        kv, memn = _mm_fwd(mem, w("w_xkv"), nt=True, pre="norm", g_pre=mem_g, name="xkv_proj")
        ox = _xattn_fwd(qx, kv, name="xattn_fwd")
        xo, x2 = _mm_fwd(ox, w("w_xo"), nt=False, post="norm_res", g_post=_row(small["x_post_g"][l]), xres=x1, name="xattn_out")
        x2 = fwd_hook(l, 2, x2)
        gu, h3 = _mm_fwd(x2, w("w_gate_up"), nt=True, pre="norm", g_pre=_row(small["ffn_pre_g"][l]), name="ffn_in")
        f, x3 = _mm_fwd(gu, w("w_down"), nt=False, pre="swiglu", post="norm_res", g_post=_row(small["ffn_post_g"][l]), xres=x2,
                        tile=256, name="ffn_out")
        saved.append((mix_par, x, z, h1, cat, mix, x1, qx, h2, kv, memn, ox, xo, x2, gu, h3, f))
        x = fwd_hook(l, 3, x3)

    loss, dx = _loss_grad(x, target, name="loss")

    gs = {n: [None] * depth for n in SMALL if n != "mem_norm_g"}
    gb = {n: [None] * depth for n in BIG}
    dmemn = [None] * depth
    for l in reversed(range(depth)):
        w = functools.partial(weight, l)
        mix_par, x0, z, h1, cat, mix, x1, qx, h2, kv, memn, ox, xo, x2, gu, h3, f = saved[l]
        dact, df, gs["ffn_post_g"][l] = _mm_bwd_pre(f, dx, _row(small["ffn_post_g"][l]), w("w_down"), name="ffn_out_bwd")
        gb["w_down"][l] = _wgrad(gu, df, swiglu=True, name="w_down_grad")
        dgu = _swiglu_bwd(dact, gu, name="swiglu_bwd")
        gb["w_gate_up"][l] = _wgrad(dgu, h3, name="w_gate_up_grad")
        dx2, gs["ffn_pre_g"][l] = _mm_bwd_post(dgu, w("w_gate_up"), x2, _row(small["ffn_pre_g"][l]), dx, nt=False, tile=256, name="ffn_in_bwd")
        dx2 = bwd_hook(l, 1, dx2, gb)
        do, dxo, gs["x_post_g"][l] = _mm_bwd_pre(xo, dx2, _row(small["x_post_g"][l]), w("w_xo"), name="xattn_out_bwd")
        gb["w_xo"][l] = _wgrad(ox, dxo, name="w_xo_grad")
        dqx, dkv = _xattn_bwd(qx, kv, do, name="xattn_bwd")
        gb["w_xq"][l] = _wgrad(h2, dqx, name="w_xq_grad")
        dx1, gs["x_pre_g"][l] = _mm_bwd_post(dqx, w("w_xq"), x1, _row(small["x_pre_g"][l]), dx2, nt=True, name="xq_proj_bwd")
        dx1 = bwd_hook(l, 2, dx1, gb)
        gb["w_xkv"][l] = _wgrad(dkv, memn, name="w_xkv_grad")
        dmemn[l] = _mm_fwd(dkv, w("w_xkv"), nt=False, out_dtype=F32, name="xkv_proj_bwd")[0]
        dcat, dmix, gs["mix_post_g"][l] = _mm_bwd_pre(mix, dx1, _row(small["mix_post_g"][l]), w("w_o"), name="mix_out_bwd")
        gb["w_o"][l] = _wgrad(cat, dmix, name="w_o_grad")
        dz, dvg, dwc, dbf, dbd, dscale, dsink = _mixer_bwd(z, dcat, cos_t, sin_t, *mix_par, name="mixer_bwd")
        gs["gm_v_g"][l], gs["gm_w_s"][l], gs["pool_scale"][l], gs["attn_sinks"][l] = dvg, dwc, dscale, dsink[:, 0]
        gs["gm_b_s"][l] = dbf.reshape(BLOCK, -1, HEAD_DIM).sum(-1).T
        gs["pool_w"][l] = jnp.stack([dbd[HEAD_DIM * g:HEAD_DIM * (g + 1), HEAD_DIM * g:HEAD_DIM * (g + 1)] for g in range(len(POOL_WINDOWS))])
        gb["w_in"][l] = _wgrad(dz, h1, name="w_in_grad")
        dx, gs["mix_pre_g"][l] = _mm_bwd_post(dz, w("w_in"), x0, _row(small["mix_pre_g"][l]), dx1, nt=False, name="in_proj_bwd")
        dx = bwd_hook(l, 3, dx, gb)

    small_grads = {n: jnp.stack(v).reshape(small[n].shape) for n, v in gs.items()}
    small_grads["mem_norm_g"] = _norm_dgain(mem, jnp.stack(dmemn), name="mem_norm_grad").reshape(-1)
    return loss, dx, small_grads, gb


EARLY = ("w_in", "w_o", "w_xq", "w_xkv", "w_xo")
LATE = ("w_gate_up", "w_down")


def _step(a):
    depth = a["mix_pre_g"].shape[0]
    x, mem, target = a["x"][0], a["mem"][0], a["loss_target"][0]
    px, py, pc = _place()
    blocks = jnp.stack([4 * cx + 2 * cy + pc for cx, cy in (_chip(k, px, py) for k in range(4))]).astype(jnp.int32)
    flight = {}

    whole = {(l, n): _place_own(a[n][l].T if n in TRANSPOSED else a[n][l], blocks, name="place_" + n) for l in range(depth) for n in BIG}
    gather = {"first_early": [(0, n) for n in EARLY], "first_late": [(0, n) for n in LATE]}
    gather.update({f"layer{l}": [(l, n) for n in BIG] for l in range(1, depth)})
    forward_plan = {(0, 0): [("a", "first_early"), ("mid", "first_early"), ("end", "first_early"), ("a", "first_late"), ("a", "layer1")],
                    (0, 1): [("mid", "first_late")], (0, 2): [("end", "first_late"), ("mid", "layer1")], (0, 3): [("end", "layer1")]}
    for l in range(1, depth):
        forward_plan.update({(l, 0): [("a", f"layer{l + 1}")], (l, 2): [("mid", f"layer{l + 1}")], (l, 3): [("end", f"layer{l + 1}")]})

    def gather_step(kind, tag, v):
        keys = gather[tag]
        if kind == "a":
            send, recv, arrays, token = _start(_gather_copies("a"), [whole[k] for k in keys], [], 4 * len(keys), name=f"gather_a_{tag}")
            flight[tag] = (send, recv, arrays)
            return [token]
        send, recv, arrays = flight.pop(tag)
        if kind == "mid":
            arrays = _finish(_gather_copies("a"), send, recv, arrays, [v], name=f"gather_a_done_{tag}")
            send, recv, arrays, token = _start(_gather_copies("b"), arrays, [], 3 * len(keys), name=f"gather_b_{tag}")
            flight[tag] = (send, recv, arrays)
            return [token]
        for key, arr in zip(keys, _finish(_gather_copies("b"), send, recv, arrays, [v], name=f"gather_b_done_{tag}")):
            whole[key] = arr
        return []

    def fwd_hook(l, point, v):
        tokens = []
        for kind, tag in forward_plan.get((l, point), []):
            if tag in gather:
                tokens += gather_step(kind, tag, v)
        return _tie(v, tokens)

    mine = {}
    scatter = {f"{part}{l}": [(l, n) for n in names] for l in range(depth) for part, names in (("early", EARLY), ("late", LATE))}
    backward_plan = {}
    for l in range(depth):
        backward_plan[(l, 1)] = [("a", f"late{l}")] + ([("mid", f"early{l + 1}")] if l + 1 < depth else [])
        backward_plan[(l, 2)] = [("mid", f"late{l}")] + ([("end", f"early{l + 1}")] if l + 1 < depth else [])
        backward_plan[(l, 3)] = [("end", f"late{l}"), ("a", f"early{l}")]
    backward_plan[(0, 3)] += [("mid", "early0"), ("end", "early0")]

    def scatter_step(kind, tag, v, gb):
        keys = scatter[tag]
        n = len(keys)
        if kind == "a":
            grads = [gb[name][l] for l, name in keys]
            zones = [jax.ShapeDtypeStruct((4, g.shape[0] // N_DEV, g.shape[1]), F32) for g in grads]
            send, recv, arrays, token = _start(_scatter_copies("a", n), grads, zones, 4 * n, name=f"scatter_a_{tag}")
            flight[tag] = (send, recv, arrays, None)
            return [token]
        send, recv, arrays, kept = flight.pop(tag)
        if kind == "mid":
            arrays = _finish(_scatter_copies("a", n), send, recv, arrays, [v], name=f"scatter_a_done_{tag}")
            sums = [_chip_sum(arrays[i], arrays[n + i], blocks, name="chip_sum") for i in range(n)]
            zones = [jax.ShapeDtypeStruct(s[1].shape, BF16) for s in sums]
            send, recv, arrays, token = _start(_scatter_copies("b", n), [s[1] for s in sums], zones, 3 * n, name=f"scatter_b_{tag}")
            flight[tag] = (send, recv, arrays, [s[0] for s in sums])
            return [token]
        arrays = _finish(_scatter_copies("b", n), send, recv, arrays, [v], name=f"scatter_b_done_{tag}")
        for i, key in enumerate(keys):
            mine[key] = _final_sum(kept[i], arrays[n + i], name="final_sum")
        return []

    def bwd_hook(l, point, v, gb):
        tokens = []
        for kind, tag in backward_plan.get((l, point), []):
            tokens += scatter_step(kind, tag, v, gb)
        return _tie(v, tokens)

    loss, dx, small_grads, _ = _local_step(x, mem, a["positions"][0], target, {n: a[n] for n in SMALL}, lambda l, n: whole[(l, n)],
                                           fwd_hook, bwd_hook)

    out = {}
    for n in BIG:
        g = jnp.stack([mine[(l, n)].T if n in TRANSPOSED else mine[(l, n)] for l in range(depth)])
        out["grad_" + n] = g
        out["delta_" + n], out["new_m_" + n], out["new_v_" + n] = _adamw(a[n], g, a["m_" + n], a["v_" + n], name="adamw_" + n)

    part = _pack([small_grads[n] for n in SMALL])
    parts = _all_gather([part], [[0]], part, name="gather_small_grads")[0].reshape(N_DEV, *part.shape)
    packs = _small_update(parts, *[_pack([a[p + n] for n in SMALL]) for p in ("", "m_", "v_")], name="small_update")
    for p, pack in zip(("grad_", "delta_", "new_m_", "new_v_"), packs):
        for n, v in zip(SMALL, _unpack(pack, [a[n].shape for n in SMALL])):
            out[p + n] = v

    total = lax.psum(loss[0, 0], ("x", "y", "c"))
    return (total, dx[None], *[out[p + n] for p in ("grad_", "delta_", "new_m_", "new_v_") for n in WEIGHTS])


def kernel(x, mem, positions, mem_norm_g, mix_pre_g, mix_post_g, w_in, gm_v_g, gm_w_s, gm_b_s, pool_w, pool_scale, attn_sinks,
           w_o, x_pre_g, x_post_g, w_xq, w_xkv, w_xo, ffn_pre_g, ffn_post_g, w_gate_up, w_down, loss_target, m_mem_norm_g,
           m_mix_pre_g, m_mix_post_g, m_w_in, m_gm_v_g, m_gm_w_s, m_gm_b_s, m_pool_w, m_pool_scale, m_attn_sinks, m_w_o,
           m_x_pre_g, m_x_post_g, m_w_xq, m_w_xkv, m_w_xo, m_ffn_pre_g, m_ffn_post_g, m_w_gate_up, m_w_down, v_mem_norm_g,
           v_mix_pre_g, v_mix_post_g, v_w_in, v_gm_v_g, v_gm_w_s, v_gm_b_s, v_pool_w, v_pool_scale, v_attn_sinks, v_w_o,
           v_x_pre_g, v_x_post_g, v_w_xq, v_w_xkv, v_w_xo, v_ffn_pre_g, v_ffn_post_g, v_w_gate_up, v_w_down):
    return _step(dict(locals()))
```

```python
import functools
import math

import jax
import jax.numpy as jnp
from jax import lax
from jax.experimental import pallas as pl
from jax.experimental.pallas import tpu as pltpu

F32, BF16 = jnp.float32, jnp.bfloat16
EPS = 1e-6
HEAD_DIM = 64
BLOCK = 128
POOL_WINDOWS = (2, 4, 8, 16)
ATT_Q_HEADS = 8
X_HEADS = 4
ROPE_THETA = 10000.0
ADAM_LR, ADAM_B1, ADAM_B2, ADAM_EPS, ADAM_WD, ADAM_STEP = 0.001, 0.9, 0.999, 1e-08, 0.01, 10
N_DEV = 8
TOKEN_TILE = 512
VMEM_LIMIT_BYTES = 50 * 2**20
NEG = -1e30
MESH = pl.DeviceIdType.MESH

NN = ((1,), (0,))
NT = ((1,), (1,))
TN = ((0,), (0,))


def _dot(a, b, dims):
    return lax.dot_general(a.astype(BF16), b.astype(BF16), (dims, ((), ())), preferred_element_type=F32)


def _params(*sem):
    return pltpu.CompilerParams(dimension_semantics=sem, vmem_limit_bytes=VMEM_LIMIT_BYTES)


def _full(shape):
    return pl.BlockSpec(shape, lambda *_: (0,) * len(shape))


def _lane(shape):
    return lax.broadcasted_iota(jnp.int32, shape, len(shape) - 1)


def _rms_fwd(x, g):
    r = lax.rsqrt(jnp.mean(x * x, axis=-1, keepdims=True) + EPS)
    return x * r * g


def _rms_bwd(x, g, dy):
    r = lax.rsqrt(jnp.mean(x * x, axis=-1, keepdims=True) + EPS)
    xh = x * r
    dg = jnp.sum(dy * xh, axis=0, keepdims=True)
    dxh = dy * g
    dx = r * (dxh - xh * jnp.mean(dxh * xh, axis=-1, keepdims=True))
    return dx, dg


def _silu_parts(gate):
    sg = 1.0 / (1.0 + jnp.exp(-gate))
    return gate * sg, sg


def _swiglu_cols(gu_ref, out_ref, width):
    step = 128 * max(1, (width // 128) // 4)
    for c0 in range(0, width, step):
        c1 = min(width, c0 + step)
        gate = gu_ref[:, c0:c1].astype(F32)
        up = gu_ref[:, width + c0:width + c1].astype(F32)
        out_ref[:, c0:c1] = (_silu_parts(gate)[0] * up).astype(out_ref.dtype)


def _col_tile(n, cap=1408):
    best = n
    for t in range(128, cap + 1, 128):
        if n % t == 0:
            best = t
    return best if n > cap else n


def _mm_fwd(a, w, *, nt, name, pre=None, g_pre=None, post=None, g_post=None, xres=None, out_dtype=BF16, tile=TOKEN_TILE):
    S = a.shape[0]
    n_out, k = (w.shape[0], w.shape[1]) if nt else (w.shape[1], w.shape[0])
    T = min(tile, S)
    tn = n_out if post else _col_tile(n_out)
    grid = (S // T, n_out // tn)

    def body(*refs):
        it = iter(refs)
        a_ref, w_ref = next(it), next(it)
        gpre_ref = next(it) if pre == "norm" else None
        gpost_ref, x_ref = (next(it), next(it)) if post else (None, None)
        outs = [next(it) for _ in range((2 if post else 1) + (1 if pre == "norm" else 0))]
        as_ref = next(it) if pre else None
        if pre:
            @pl.when(pl.program_id(1) == 0)
            def _():
                if pre == "norm":
                    h = _rms_fwd(a_ref[...], gpre_ref[...]).astype(BF16)
                    outs[-1][...] = h
                    as_ref[...] = h
                else:
                    _swiglu_cols(a_ref, as_ref, k)
            av = as_ref[...]
        else:
            av = a_ref[...]
        acc = _dot(av, w_ref[...], NT if nt else NN)
        if post:
            outs[0][...] = acc
            outs[1][...] = x_ref[...] + _rms_fwd(acc, gpost_ref[...])
        else:
            outs[0][...] = acc.astype(out_dtype)

    in_specs = [pl.BlockSpec((T, a.shape[1]), lambda i, j: (i, 0)),
                pl.BlockSpec((tn, k), lambda i, j: (j, 0)) if nt else pl.BlockSpec((k, tn), lambda i, j: (0, j))]
    args = [a, w]
    if pre == "norm":
        in_specs.append(_full((1, k)))
        args.append(g_pre)
    if post:
        in_specs += [_full((1, n_out)), pl.BlockSpec((T, n_out), lambda i, j: (i, 0))]
        args += [g_post, xres]
    out_block = pl.BlockSpec((T, tn), lambda i, j: (i, j))
    if post:
        out_shape = [jax.ShapeDtypeStruct((S, n_out), F32)] * 2
        out_specs = [out_block, out_block]
    else:
        out_shape = [jax.ShapeDtypeStruct((S, n_out), out_dtype)]
        out_specs = [out_block]
    if pre == "norm":
        out_shape.append(jax.ShapeDtypeStruct((S, k), BF16))
        out_specs.append(pl.BlockSpec((T, k), lambda i, j: (i, 0)))
    scratch = [pltpu.VMEM((T, k), BF16)] if pre else []
    return pl.pallas_call(body, name=name, grid=grid, in_specs=in_specs, out_specs=out_specs, out_shape=out_shape,
                          scratch_shapes=scratch, compiler_params=_params("parallel", "arbitrary"))(*args)


def _mm_bwd_pre(m, dx, g, w, *, name, tile=TOKEN_TILE):
    S, n = m.shape
    ku = w.shape[0]
    T = min(tile, S)
    tn = _col_tile(ku)
    grid = (S // T, ku // tn)

    def body(m_ref, dx_ref, g_ref, w_ref, du_ref, dm_ref, dg_ref, as_ref):
        i, j = pl.program_id(0), pl.program_id(1)

        @pl.when((i == 0) & (j == 0))
        def _():
            dg_ref[...] = jnp.zeros_like(dg_ref)

        @pl.when(j == 0)
        def _():
            dmv, dg = _rms_bwd(m_ref[...], g_ref[...], dx_ref[...])
            dg_ref[...] += dg
            dmb = dmv.astype(BF16)
            dm_ref[...] = dmb
            as_ref[...] = dmb

        du_ref[...] = _dot(as_ref[...], w_ref[...], NT).astype(BF16)

    row = pl.BlockSpec((T, n), lambda i, j: (i, 0))
    return pl.pallas_call(
        body, name=name, grid=grid,
        in_specs=[row, row, _full((1, n)), pl.BlockSpec((tn, n), lambda i, j: (j, 0))],
        out_specs=[pl.BlockSpec((T, tn), lambda i, j: (i, j)), row, _full((1, n))],
        out_shape=[jax.ShapeDtypeStruct((S, ku), BF16), jax.ShapeDtypeStruct((S, n), BF16), jax.ShapeDtypeStruct((1, n), F32)],
        scratch_shapes=[pltpu.VMEM((T, n), BF16)], compiler_params=_params("arbitrary", "arbitrary"))(m, dx, g, w)


def _mm_bwd_post(dy, w, x, g, dx_in, *, nt, name, tile=TOKEN_TILE):
    S, k = dy.shape
    n = x.shape[1]
    T = min(tile, S)

    def body(dy_ref, w_ref, x_ref, g_ref, dxi_ref, dxo_ref, dg_ref):
        @pl.when(pl.program_id(0) == 0)
        def _():
            dg_ref[...] = jnp.zeros_like(dg_ref)

        dh = _dot(dy_ref[...], w_ref[...], NT if nt else NN)
        dxn, dg = _rms_bwd(x_ref[...], g_ref[...], dh)
        dg_ref[...] += dg
        dxo_ref[...] = dxi_ref[...] + dxn

    row = pl.BlockSpec((T, n), lambda i: (i, 0))
    return pl.pallas_call(
        body, name=name, grid=(S // T,),
        in_specs=[pl.BlockSpec((T, k), lambda i: (i, 0)), _full(w.shape), row, _full((1, n)), row],
        out_specs=[row, _full((1, n))],
        out_shape=[jax.ShapeDtypeStruct((S, n), F32), jax.ShapeDtypeStruct((1, n), F32)],
        compiler_params=_params("arbitrary"))(dy, w, x, g, dx_in)


def _wgrad(a, b, *, name, swiglu=False, tile=TOKEN_TILE):
    S = a.shape[0]
    k = a.shape[1] // 2 if swiglu else a.shape[1]
    n = b.shape[1]
    T = min(tile, S)
    tk = 512 if k % 512 == 0 else 256
    nk = k // tk

    def body(*refs):
        if swiglu:
            gate_ref, up_ref, b_ref, o_ref, act_ref = refs
            for r0 in range(0, S, T):
                gate, up = gate_ref[r0:r0 + T, :].astype(F32), up_ref[r0:r0 + T, :].astype(F32)
                act_ref[r0:r0 + T, :] = (_silu_parts(gate)[0] * up).astype(BF16)
            av = act_ref[...]
        else:
            a_ref, b_ref, o_ref = refs
            av = a_ref[...]
        o_ref[...] = _dot(av, b_ref[...], TN)

    in_specs = [pl.BlockSpec((S, tk), lambda kk: (0, kk))]
    args = [a]
    if swiglu:
        in_specs.append(pl.BlockSpec((S, tk), lambda kk: (0, kk + nk)))
        args.append(a)
    in_specs.append(_full((S, n)))
    args.append(b)
    return pl.pallas_call(body, name=name, grid=(nk,), in_specs=in_specs,
                          out_specs=pl.BlockSpec((tk, n), lambda kk: (kk, 0)),
                          out_shape=jax.ShapeDtypeStruct((k, n), F32),
                          scratch_shapes=[pltpu.VMEM((S, tk), BF16)] if swiglu else [],
                          compiler_params=_params("parallel"))(*args)


def _ffn_out_bwd(m, dx, g, w, gu, *, name, tile=256):
    S, n = m.shape
    f = w.shape[0]
    T = min(tile, S)

    def body(m_ref, dx_ref, g_ref, w_ref, gu_ref, dgu_ref, dm_ref, dg_ref, da_ref):
        @pl.when(pl.program_id(0) == 0)
        def _():
            dg_ref[...] = jnp.zeros_like(dg_ref)

        dmv, dg = _rms_bwd(m_ref[...], g_ref[...], dx_ref[...])
        dg_ref[...] += dg
        dmb = dmv.astype(BF16)
        dm_ref[...] = dmb
        da_ref[...] = _dot(dmb, w_ref[...], NT)
        step = 128 * max(1, (f // 128) // 4)
        for c0 in range(0, f, step):
            c1 = min(f, c0 + step)
            gate = gu_ref[:, c0:c1].astype(F32)
            up = gu_ref[:, f + c0:f + c1].astype(F32)
            da = da_ref[:, c0:c1]
            sl, sg = _silu_parts(gate)
            dgu_ref[:, c0:c1] = (da * up * (sg + sl * (1.0 - sg))).astype(BF16)
            dgu_ref[:, f + c0:f + c1] = (da * sl).astype(BF16)

    row = pl.BlockSpec((T, n), lambda i: (i, 0))
    wide = pl.BlockSpec((T, 2 * f), lambda i: (i, 0))
    return pl.pallas_call(
        body, name=name, grid=(S // T,), in_specs=[row, row, _full((1, n)), _full((f, n)), wide],
        out_specs=[wide, row, _full((1, n))],
        out_shape=[jax.ShapeDtypeStruct((S, 2 * f), BF16), jax.ShapeDtypeStruct((S, n), BF16), jax.ShapeDtypeStruct((1, n), F32)],
        scratch_shapes=[pltpu.VMEM((T, f), F32)], compiler_params=_params("arbitrary"))(m, dx, g, w, gu)


def _loss_grad(y, target, *, name, tile=TOKEN_TILE):
    S, d = y.shape
    T = min(tile, S)
    steps = S // T

    def body(y_ref, t_ref, loss_ref, dy_ref, acc_ref):
        i = pl.program_id(0)

        @pl.when(i == 0)
        def _():
            acc_ref[...] = jnp.zeros_like(acc_ref)

        diff = y_ref[...] - t_ref[...]
        dy_ref[...] = diff * (1.0 / d)
        acc_ref[...] += jnp.sum(diff * diff, axis=0, keepdims=True)

        @pl.when(i == steps - 1)
        def _():
            total = jnp.sum(acc_ref[...], axis=1, keepdims=True) * (0.5 / d)
            loss_ref[...] = jnp.broadcast_to(total, loss_ref.shape)

    row = pl.BlockSpec((T, d), lambda i: (i, 0))
    return pl.pallas_call(
        body, name=name, grid=(steps,), in_specs=[row, row], out_specs=[_full((1, 128)), row],
        out_shape=[jax.ShapeDtypeStruct((1, 128), F32), jax.ShapeDtypeStruct((S, d), F32)],
        scratch_shapes=[pltpu.VMEM((1, d), F32)], compiler_params=_params("arbitrary"))(y, target)


def _norm_dgain(x, dys, *, name):
    def body(x_ref, dy_ref, o_ref):
        xv = x_ref[...]
        dy = dy_ref[0]
        for l in range(1, dys.shape[0]):
            dy = dy + dy_ref[l]
        r = lax.rsqrt(jnp.mean(xv * xv, axis=-1, keepdims=True) + EPS)
        o_ref[...] = jnp.sum(dy * xv * r, axis=0, keepdims=True)

    return pl.pallas_call(body, name=name, out_shape=jax.ShapeDtypeStruct((1, x.shape[1]), F32),
                          compiler_params=pltpu.CompilerParams(vmem_limit_bytes=VMEM_LIMIT_BYTES))(x, dys)


def _xattn_probs(q_ref, kv_ref, h, d, hd):
    q = q_ref[:, h * hd:(h + 1) * hd]
    kh = kv_ref[:, h * hd:(h + 1) * hd]
    vh = kv_ref[:, d + h * hd:d + (h + 1) * hd]
    s = _dot(q, kh, NT) * (hd ** -0.5)
    e = jnp.exp(s - jnp.max(s, axis=-1, keepdims=True))
    return q, kh, vh, e / jnp.sum(e, axis=-1, keepdims=True)


def _xattn_fwd(q, kv, *, name, tile=TOKEN_TILE):
    S, d = q.shape
    mlen = kv.shape[0]
    hd = d // X_HEADS
    T = min(tile, S)

    def body(q_ref, kv_ref, o_ref):
        for h in range(X_HEADS):
            _, _, vh, p = _xattn_probs(q_ref, kv_ref, h, d, hd)
            o_ref[:, h * hd:(h + 1) * hd] = _dot(p, vh, NN).astype(BF16)

    row = pl.BlockSpec((T, d), lambda i: (i, 0))
    return pl.pallas_call(body, name=name, grid=(S // T,), in_specs=[row, _full((mlen, 2 * d))], out_specs=row,
                          out_shape=jax.ShapeDtypeStruct((S, d), BF16), compiler_params=_params("parallel"))(q, kv)


def _xattn_bwd(q, kv, do, *, name, tile=TOKEN_TILE):
    S, d = q.shape
    mlen = kv.shape[0]
    hd = d // X_HEADS
    T = min(tile, S)

    def body(q_ref, kv_ref, do_ref, dq_ref, dkv_ref):
        @pl.when(pl.program_id(0) == 0)
        def _():
            dkv_ref[...] = jnp.zeros_like(dkv_ref)

        for h in range(X_HEADS):
            qh, kh, vh, p = _xattn_probs(q_ref, kv_ref, h, d, hd)
            doh = do_ref[:, h * hd:(h + 1) * hd]
            dp = _dot(doh, vh, NT)
            ds = p * (dp - jnp.sum(p * dp, axis=-1, keepdims=True)) * (hd ** -0.5)
            dq_ref[:, h * hd:(h + 1) * hd] = _dot(ds, kh, NN).astype(BF16)
            dkv_ref[:, h * hd:(h + 1) * hd] += _dot(ds, qh, TN)
            dkv_ref[:, d + h * hd:d + (h + 1) * hd] += _dot(p, doh, TN)

    row = pl.BlockSpec((T, d), lambda i: (i, 0))
    return pl.pallas_call(
        body, name=name, grid=(S // T,), in_specs=[row, _full((mlen, 2 * d)), row],
        out_specs=[row, _full((mlen, 2 * d))],
        out_shape=[jax.ShapeDtypeStruct((S, d), BF16), jax.ShapeDtypeStruct((mlen, 2 * d), F32)],
        compiler_params=_params("arbitrary"))(q, kv, do)


GM0, POOL0, Q0, KV0, Z_END = 0, 512, 768, 1280, 1536


def _rope_tables(pos_col, inv_row, *, name):
    S = pos_col.shape[0]

    def body(p_ref, inv_ref, cos_ref, sin_ref):
        ang = p_ref[...].astype(F32) * inv_ref[...]
        sin = jnp.sin(ang)
        cos_ref[...] = jnp.cos(ang)
        sin_ref[...] = jnp.where(_lane(ang.shape) % HEAD_DIM < HEAD_DIM // 2, -sin, sin)

    T = min(TOKEN_TILE, S)
    return pl.pallas_call(
        body, name=name, grid=(S // T,),
        in_specs=[pl.BlockSpec((T, 1), lambda i: (i, 0)), _full((1, 128))],
        out_specs=[pl.BlockSpec((T, 128), lambda i: (i, 0))] * 2,
        out_shape=[jax.ShapeDtypeStruct((S, 128), F32)] * 2, compiler_params=_params("parallel"))(pos_col, inv_row)


def _swap_halves(x):
    n = x.shape[-1]
    return jnp.where(_lane(x.shape) % HEAD_DIM < HEAD_DIM // 2, pltpu.roll(x, n - HEAD_DIM // 2, 1), pltpu.roll(x, HEAD_DIM // 2, 1))


def _rope(x, cos, sin_s):
    reps = x.shape[-1] // 128
    if reps > 1:
        cos, sin_s = jnp.tile(cos, (1, reps)), jnp.tile(sin_s, (1, reps))
    return x * cos + _swap_halves(x) * sin_s


def _rope_bwd(dy, cos, sin_s):
    reps = dy.shape[-1] // 128
    if reps > 1:
        cos, sin_s = jnp.tile(cos, (1, reps)), jnp.tile(sin_s, (1, reps))
    return dy * cos + _swap_halves(dy * sin_s)


def _gelu_parts(x):
    c = math.sqrt(2.0 / math.pi)
    t = jnp.tanh(c * (x + 0.044715 * x * x * x))
    return 0.5 * x * (1.0 + t), t


def _gelu_grad(x, t):
    c = math.sqrt(2.0 / math.pi)
    return 0.5 * (1.0 + t) + 0.5 * x * (1.0 - t * t) * c * (1.0 + 3.0 * 0.044715 * x * x)


def _group_sum(x):
    gid = _lane(x.shape) // HEAD_DIM
    out = jnp.zeros_like(x)
    for g in range(x.shape[-1] // HEAD_DIM):
        sel = gid == g
        out = jnp.where(sel, jnp.sum(jnp.where(sel, x, 0.0), axis=-1, keepdims=True), out)
    return out


def _low_half(shape):
    return _lane(shape) % 128 < HEAD_DIM


def _tril(w):
    r = lax.broadcasted_iota(jnp.int32, w.shape, 0)
    c = lax.broadcasted_iota(jnp.int32, w.shape, 1)
    return jnp.where(r >= c, w, 0.0)


def _gating_fwd(zg, vg, wc_ref, bfull):
    ge, th = _gelu_parts(zg)
    u, v = ge[:, :256], ge[:, 256:]
    r = lax.rsqrt(_group_sum(v * v) * (1.0 / HEAD_DIM) + EPS)
    vhat = v * r
    vn = vhat * vg
    low = _low_half((BLOCK, 128))
    blocks = []
    for blk in range(2):
        vb = vn[:, 128 * blk:128 * (blk + 1)]
        m0 = _dot(_tril(wc_ref[2 * blk]), vb, NN)
        m1 = _dot(_tril(wc_ref[2 * blk + 1]), vb, NN)
        blocks.append(jnp.where(low, m0, m1))
    mixed = jnp.concatenate(blocks, axis=1) + bfull
    return u * mixed, (th, u, r, vhat, vn, mixed)


def _band(w, transpose):
    if transpose:
        s2 = lax.broadcasted_iota(jnp.int32, (2 * BLOCK, BLOCK), 0)
        t = lax.broadcasted_iota(jnp.int32, (2 * BLOCK, BLOCK), 1)
    else:
        t = lax.broadcasted_iota(jnp.int32, (BLOCK, 2 * BLOCK), 0)
        s2 = lax.broadcasted_iota(jnp.int32, (BLOCK, 2 * BLOCK), 1)
    d = t - s2 + BLOCK
    return jnp.where((d >= 0) & (d < w), 1.0, 0.0).astype(BF16)


def _exact_dot(b01, x):
    hi = x.astype(BF16)
    lo = (x - hi.astype(F32)).astype(BF16)
    return _dot(b01, hi, NN) + _dot(b01, lo, NN)


def _inv_count(n):
    t = lax.broadcasted_iota(jnp.int32, (BLOCK, 256), 0) + n * BLOCK
    gid = _lane((BLOCK, 256)) // HEAD_DIM
    win = jnp.where(gid == 0, POOL_WINDOWS[0], jnp.where(gid == 1, POOL_WINDOWS[1], jnp.where(gid == 2, POOL_WINDOWS[2], POOL_WINDOWS[3])))
    return 1.0 / jnp.minimum(t + 1, win).astype(F32)


def _pool_fwd(p, pp, n, bd, scale):
    low = _low_half((BLOCK, 128))
    blocks = []
    for blk in range(2):
        both = jnp.concatenate([pp[:, 128 * blk:128 * (blk + 1)], p[:, 128 * blk:128 * (blk + 1)]], axis=0)
        s0 = _exact_dot(_band(POOL_WINDOWS[2 * blk], False), both)
        s1 = _exact_dot(_band(POOL_WINDOWS[2 * blk + 1], False), both)
        blocks.append(jnp.where(low, s0, s1))
    inv = _inv_count(n)
    pooled = jnp.concatenate(blocks, axis=1) * inv - p
    mapped = _dot(pooled, bd, NN)
    return mapped * scale, (inv, pooled, mapped)


def _attn_mask(first_key):
    qi = lax.broadcasted_iota(jnp.int32, (BLOCK, 2 * BLOCK), 0)
    ki = lax.broadcasted_iota(jnp.int32, (BLOCK, 2 * BLOCK), 1)
    return (ki > qi) & (ki <= qi + BLOCK) & (ki >= first_key)


def _attn_head(j, qrot, kk, kks, vv, vvs, sinks_ref, valid):
    half, kv_head = j % 2, j // 4
    mine = (_lane((BLOCK, 128)) >= HEAD_DIM) if half else (_lane((BLOCK, 128)) < HEAD_DIM)
    swapped = kv_head != half
    kx, vx = (kks, vvs) if swapped else (kk, vv)
    qm = jnp.where(mine, qrot[:, 128 * (j // 2):128 * (j // 2 + 1)], 0.0).astype(BF16)
    sc = jnp.where(valid, _dot(qm, kx, NT) * (HEAD_DIM ** -0.5), NEG)
    sink = sinks_ref[j]
    mx = jnp.maximum(jnp.max(sc, axis=-1, keepdims=True), sink)
    e = jnp.exp(sc - mx)
    es = jnp.exp(sink - mx)
    den = jnp.sum(e, axis=-1, keepdims=True) + es
    return e / den, es / den, qm, kx, vx, mine, swapped


def _mixer_fwd(z, cos_t, sin_t, vg, wc, bfull, bd, scale, sinks, *, name):
    S = z.shape[0]
    nb = S // BLOCK

    def body(zc_ref, zpp_ref, zpk_ref, cc_ref, cp_ref, sc_ref, sp_ref, vg_ref, wc_ref, bf_ref, bd_ref, scale_ref, sinks_ref, out_ref):
        n = pl.program_id(0)
        keep = (n > 0).astype(F32)
        a, _ = _gating_fwd(zc_ref[:, GM0:POOL0], vg_ref[...], wc_ref, bf_ref[...])
        out_ref[:, 0:256] = a.astype(BF16)
        b, _ = _pool_fwd(zc_ref[:, POOL0:Q0], zpp_ref[...] * keep, n, bd_ref[...], scale_ref[...])
        out_ref[:, 256:512] = b.astype(BF16)
        cos_c, sin_c = cc_ref[...], sc_ref[...]
        qrot = _rope(zc_ref[:, Q0:KV0], cos_c, sin_c)
        kk = jnp.concatenate([_rope(zpk_ref[:, 0:128], cp_ref[...], sp_ref[...]), _rope(zc_ref[:, KV0:KV0 + 128], cos_c, sin_c)], axis=0)
        vv = jnp.concatenate([zpk_ref[:, 128:256], zc_ref[:, KV0 + 128:Z_END]], axis=0)
        kks, vvs = pltpu.roll(kk, HEAD_DIM, 1).astype(BF16), pltpu.roll(vv, HEAD_DIM, 1).astype(BF16)
        kk, vv = kk.astype(BF16), vv.astype(BF16)
        valid = _attn_mask(jnp.where(n > 0, 0, BLOCK))
        low = _low_half((BLOCK, 128))
        for s in range(ATT_Q_HEADS // 2):
            o = []
            for j in (2 * s, 2 * s + 1):
                p, _, _, _, vx, _, _ = _attn_head(j, qrot, kk, kks, vv, vvs, sinks_ref, valid)
                o.append(_dot(p, vx, NN))
            out_ref[:, 512 + 128 * s:512 + 128 * (s + 1)] = jnp.where(low, o[0], o[1]).astype(BF16)

    cur = lambda n: (n, 0)
    prev = lambda n: (jnp.maximum(n - 1, 0), 0)
    in_specs = [
        pl.BlockSpec((BLOCK, Z_END), cur),
        pl.BlockSpec((BLOCK, 256), lambda n: (jnp.maximum(n - 1, 0), POOL0 // 256)),
        pl.BlockSpec((BLOCK, 256), lambda n: (jnp.maximum(n - 1, 0), KV0 // 256)),
        pl.BlockSpec((BLOCK, 128), cur), pl.BlockSpec((BLOCK, 128), prev),
        pl.BlockSpec((BLOCK, 128), cur), pl.BlockSpec((BLOCK, 128), prev),
        _full((1, 256)), _full((4, BLOCK, BLOCK)), _full((BLOCK, 256)), _full((256, 256)), _full((1, 256)),
        pl.BlockSpec(memory_space=pltpu.SMEM),
    ]
    return pl.pallas_call(
        body, name=name, grid=(nb,), in_specs=in_specs, out_specs=pl.BlockSpec((BLOCK, 1024), cur),
        out_shape=jax.ShapeDtypeStruct((S, 1024), BF16),
        compiler_params=_params("parallel"))(z, z, z, cos_t, cos_t, sin_t, sin_t, vg, wc, bfull, bd, scale, sinks)


def _mixer_bwd(z, dcat, cos_t, sin_t, vg, wc, bfull, bd, scale, sinks, *, name):
    S = z.shape[0]
    nb = S // BLOCK

    def body(zc_ref, zpp_ref, zpk_ref, dc_ref, cc_ref, cp_ref, sc_ref, sp_ref, vg_ref, wc_ref, bf_ref, bd_ref, scale_ref, sinks_ref,
             dz_ref, dvg_ref, dwc_ref, dbf_ref, dbd_ref, dscale_ref, dsink_ref, carry_ref):
        n = pl.program_id(0)

        @pl.when(n == 0)
        def _():
            for ref in (dvg_ref, dwc_ref, dbf_ref, dbd_ref, dscale_ref, dsink_ref):
                ref[...] = jnp.zeros_like(ref)

        @pl.when(n < nb)
        def _():
            keep = (n > 0).astype(F32)
            low = _low_half((BLOCK, 128))
            zg = zc_ref[:, GM0:POOL0]
            vg_v = vg_ref[...]
            _, (th, u, r, vhat, vn, mixed) = _gating_fwd(zg, vg_v, wc_ref, bf_ref[...])
            da = dc_ref[:, 0:256].astype(F32)
            dmixed = da * u
            dbf_ref[...] += dmixed
            dvn = []
            for blk in range(2):
                dmb = dmixed[:, 128 * blk:128 * (blk + 1)]
                vb = vn[:, 128 * blk:128 * (blk + 1)]
                acc = None
                for half in range(2):
                    g = 2 * blk + half
                    dmg = jnp.where(low if half == 0 else ~low, dmb, 0.0)
                    dwc_ref[g] += _tril(_dot(dmg, vb, NT))
                    part = _dot(_tril(wc_ref[g]), dmg, TN)
                    acc = part if acc is None else acc + part
                dvn.append(acc)
            dvn = jnp.concatenate(dvn, axis=1)
            dvg_ref[...] += jnp.sum(dvn * vhat, axis=0, keepdims=True)
            dvhat = dvn * vg_v
            dv = r * (dvhat - vhat * (_group_sum(dvhat * vhat) * (1.0 / HEAD_DIM)))
            dge = jnp.concatenate([da * mixed, dv], axis=1)
            own_g = dge * _gelu_grad(zg, th)
            p = zc_ref[:, POOL0:Q0]
            bd_v, scale_v = bd_ref[...], scale_ref[...]
            _, (inv, pooled, mapped) = _pool_fwd(p, zpp_ref[...] * keep, n, bd_v, scale_v)
            db = dc_ref[:, 256:512].astype(F32)
            dscale_ref[...] += jnp.sum(db * mapped, axis=0, keepdims=True)
            dmapped = db * scale_v
            dbd_ref[...] += _dot(pooled, dmapped, TN)
            dpooled = _dot(dmapped, bd_v, NT)
            dps = dpooled * inv
            back = []
            for blk in range(2):
                dpb = dps[:, 128 * blk:128 * (blk + 1)]
                t0 = _exact_dot(_band(POOL_WINDOWS[2 * blk], True), dpb)
                t1 = _exact_dot(_band(POOL_WINDOWS[2 * blk + 1], True), dpb)
                back.append(jnp.where(jnp.concatenate([low, low], axis=0), t0, t1))
            back = jnp.concatenate(back, axis=1)
            halo_p = back[:BLOCK] * keep
            own_p = back[BLOCK:] - dpooled
            cos_c, sin_c, cos_p, sin_p = cc_ref[...], sc_ref[...], cp_ref[...], sp_ref[...]
            qrot = _rope(zc_ref[:, Q0:KV0], cos_c, sin_c)
            kk = jnp.concatenate([_rope(zpk_ref[:, 0:128], cos_p, sin_p), _rope(zc_ref[:, KV0:KV0 + 128], cos_c, sin_c)], axis=0)
            vv = jnp.concatenate([zpk_ref[:, 128:256], zc_ref[:, KV0 + 128:Z_END]], axis=0)
            kks, vvs = pltpu.roll(kk, HEAD_DIM, 1).astype(BF16), pltpu.roll(vv, HEAD_DIM, 1).astype(BF16)
            kk, vv = kk.astype(BF16), vv.astype(BF16)
            valid = _attn_mask(jnp.where(n > 0, 0, BLOCK))
            dk = [jnp.zeros((2 * BLOCK, 128), F32), jnp.zeros((2 * BLOCK, 128), F32)]
            dv_acc = [jnp.zeros((2 * BLOCK, 128), F32), jnp.zeros((2 * BLOCK, 128), F32)]
            dq = []
            for s in range(ATT_Q_HEADS // 2):
                dqs = []
                dcs = dc_ref[:, 512 + 128 * s:512 + 128 * (s + 1)]
                for j in (2 * s, 2 * s + 1):
                    pr, ps, qm, kx, vx, mine, swapped = _attn_head(j, qrot, kk, kks, vv, vvs, sinks_ref, valid)
                    dom = jnp.where(mine, dcs, jnp.zeros_like(dcs))
                    dp = _dot(dom, vx, NT)
                    dv_acc[swapped] = dv_acc[swapped] + _dot(pr, dom, TN)
                    drow = jnp.sum(pr * dp, axis=-1, keepdims=True)
                    ds = (pr * (dp - drow) * (HEAD_DIM ** -0.5)).astype(BF16)
                    dsink = jnp.sum(ps * drow, axis=0, keepdims=True)
                    dsink_ref[pl.ds(j, 1), :] = dsink_ref[pl.ds(j, 1), :] - jnp.broadcast_to(dsink, (1, 128))
                    dqs.append(_dot(ds, kx, NN))
                    dk[swapped] = dk[swapped] + _dot(ds, qm, TN)
                dq.append(jnp.where(low, dqs[0], dqs[1]))
            dqrot = jnp.concatenate(dq, axis=1)
            own_q = _rope_bwd(dqrot, cos_c, sin_c)
            dkk = dk[0] + pltpu.roll(dk[1], HEAD_DIM, 1)
            dvv = dv_acc[0] + pltpu.roll(dv_acc[1], HEAD_DIM, 1)
            own_k = _rope_bwd(dkk[BLOCK:], cos_c, sin_c)
            halo_k = _rope_bwd(dkk[:BLOCK], cos_p, sin_p) * keep
            own_v, halo_v = dvv[BLOCK:], dvv[:BLOCK] * keep

            @pl.when(n > 0)
            def _():
                dz_ref[:, GM0:POOL0] = carry_ref[:, GM0:POOL0].astype(BF16)
                dz_ref[:, POOL0:Q0] = (carry_ref[:, POOL0:Q0] + halo_p).astype(BF16)
                dz_ref[:, Q0:KV0] = carry_ref[:, Q0:KV0].astype(BF16)
                dz_ref[:, KV0:KV0 + 128] = (carry_ref[:, KV0:KV0 + 128] + halo_k).astype(BF16)
                dz_ref[:, KV0 + 128:Z_END] = (carry_ref[:, KV0 + 128:Z_END] + halo_v).astype(BF16)

            carry_ref[:, GM0:POOL0] = own_g
            carry_ref[:, POOL0:Q0] = own_p
            carry_ref[:, Q0:KV0] = own_q
            carry_ref[:, KV0:KV0 + 128] = own_k
            carry_ref[:, KV0 + 128:Z_END] = own_v

        @pl.when(n == nb)
        def _():
            dz_ref[...] = carry_ref[...].astype(BF16)

    cur = lambda n: (jnp.minimum(n, nb - 1), 0)
    prev = lambda n: (jnp.maximum(jnp.minimum(n, nb - 1) - 1, 0), 0)
    in_specs = [
        pl.BlockSpec((BLOCK, Z_END), cur),
        pl.BlockSpec((BLOCK, 256), lambda n: (jnp.maximum(jnp.minimum(n, nb - 1) - 1, 0), POOL0 // 256)),
        pl.BlockSpec((BLOCK, 256), lambda n: (jnp.maximum(jnp.minimum(n, nb - 1) - 1, 0), KV0 // 256)),
        pl.BlockSpec((BLOCK, 1024), cur),
        pl.BlockSpec((BLOCK, 128), cur), pl.BlockSpec((BLOCK, 128), prev),
        pl.BlockSpec((BLOCK, 128), cur), pl.BlockSpec((BLOCK, 128), prev),
        _full((1, 256)), _full((4, BLOCK, BLOCK)), _full((BLOCK, 256)), _full((256, 256)), _full((1, 256)),
        pl.BlockSpec(memory_space=pltpu.SMEM),
    ]
    out_specs = [pl.BlockSpec((BLOCK, Z_END), lambda n: (jnp.maximum(n - 1, 0), 0)),
                 _full((1, 256)), _full((4, BLOCK, BLOCK)), _full((BLOCK, 256)), _full((256, 256)), _full((1, 256)), _full((8, 128))]
    out_shape = [jax.ShapeDtypeStruct((S, Z_END), BF16), jax.ShapeDtypeStruct((1, 256), F32),
                 jax.ShapeDtypeStruct((4, BLOCK, BLOCK), F32), jax.ShapeDtypeStruct((BLOCK, 256), F32),
                 jax.ShapeDtypeStruct((256, 256), F32), jax.ShapeDtypeStruct((1, 256), F32), jax.ShapeDtypeStruct((8, 128), F32)]
    return pl.pallas_call(
        body, name=name, grid=(nb + 1,), in_specs=in_specs, out_specs=out_specs, out_shape=out_shape,
        scratch_shapes=[pltpu.VMEM((BLOCK, Z_END), F32)],
        compiler_params=_params("arbitrary"))(z, z, z, dcat, cos_t, cos_t, sin_t, sin_t, vg, wc, bfull, bd, scale, sinks)


HBM = pl.BlockSpec(memory_space=pl.ANY)


def _place():
    return lax.axis_index("x"), lax.axis_index("y"), lax.axis_index("c")


def _chip(k, x, y):
    return (1 - x if k & 1 else x, 1 - y if k & 2 else y)


def _block_rows(ref, rows, dev):
    start = pl.multiple_of((4 * dev[0] + 2 * dev[1] + dev[2]) * rows, 8)
    return ref.at[pl.ds(start, rows), :]


def _group_size(arrays, group):
    return sum(arrays[w].shape[0] for w in group)


def _all_gather(shards, groups, carrier, *, name):
    n = len(shards)
    assert all(_group_size(shards, g) == carrier.shape[0] for g in groups) and carrier.dtype == shards[0].dtype

    def body(*refs):
        ins, car, outs = refs[:n], refs[n], refs[n + 1:2 * n + 1]
        send_sems, recv_sems, local_sems = refs[2 * n + 1:]
        x, y, c = _place()
        me, sibling = (x, y, c), (x, y, 1 - c)
        chips = [_chip(k, x, y) for k in (1, 2, 3)]

        def copy(gi, w, k, block, to, src=None):
            dst = _block_rows(outs[w], ins[w].shape[0], block)
            return pltpu.make_async_remote_copy(src_ref=dst if src is None else src, dst_ref=dst, send_sem=send_sems.at[gi, k],
                                                recv_sem=recv_sems.at[gi, k], device_id=to, device_id_type=MESH)

        def drained(gi, k):
            return pltpu.make_async_remote_copy(src_ref=car, dst_ref=car, send_sem=send_sems.at[gi, k],
                                                recv_sem=recv_sems.at[gi, k], device_id=me, device_id_type=MESH)

        for gi, group in enumerate(groups):
            for w in group:
                pltpu.make_async_copy(ins[w], _block_rows(outs[w], ins[w].shape[0], me), local_sems.at[gi]).start()
                copy(gi, w, 0, me, sibling, src=ins[w]).start()
                for j, chip in enumerate(chips):
                    copy(gi, w, 1 + j, me, (*chip, c), src=ins[w]).start()
        for gi, group in enumerate(groups):
            for j, chip in enumerate(chips):
                drained(gi, 1 + j).wait_recv()
                for w in group:
                    copy(gi, w, 4 + j, (*chip, c), sibling).start()
        for gi in range(len(groups)):
            drained(gi, 0).wait_recv()
            for j in range(3):
                drained(gi, 4 + j).wait_recv()
        for gi in range(len(groups)):
            for k in range(7):
                drained(gi, k).wait_send()
            pltpu.make_async_copy(car, car, local_sems.at[gi]).wait()

    g = len(groups)
    return pl.pallas_call(
        body, name=name, in_specs=[HBM] * (n + 1), out_specs=[HBM] * n,
        out_shape=[jax.ShapeDtypeStruct((N_DEV * s.shape[0], s.shape[1]), s.dtype) for s in shards],
        scratch_shapes=[pltpu.SemaphoreType.DMA((g, 7)), pltpu.SemaphoreType.DMA((g, 7)), pltpu.SemaphoreType.DMA((g,))],
    )(*shards, carrier)


_HBM_SPEC = pl.BlockSpec(memory_space=pltpu.HBM)
_SEM_SPEC = pl.BlockSpec(memory_space=pltpu.SEMAPHORE)
_SPLIT_PARAMS = dict(compiler_params=pltpu.CompilerParams(has_side_effects=pltpu.SideEffectType.DATAFLOW_SIDE_EFFECTING))


def _descriptors(copies, refs, send_sems, recv_sems):
    return [pltpu.make_async_remote_copy(src_ref=s, dst_ref=d, send_sem=send_sems.at[i], recv_sem=recv_sems.at[i], device_id=to, device_id_type=MESH)
            for i, (s, d, to) in enumerate(copies(refs))]


def _start(copies, arrays, fresh, n_copies, *, name):
    operands = [pltpu.with_memory_space_constraint(v, pltpu.HBM) for v in (*arrays, *[lax.empty(f.shape, f.dtype) for f in fresh])]
    n = len(operands)

    def body(*refs):
        for dma in _descriptors(copies, refs[:n], refs[n], refs[n + 1]):
            dma.start()
        refs[-1][...] = jnp.zeros_like(refs[-1])

    res = pl.pallas_call(
        body, name=name, in_specs=[_HBM_SPEC] * n,
        out_shape=(pltpu.SemaphoreType.DMA((n_copies,)), pltpu.SemaphoreType.DMA((n_copies,)),
                   *[pltpu.HBM(v.shape, v.dtype) for v in operands], jax.ShapeDtypeStruct((8, 128), F32)),
        out_specs=(_SEM_SPEC, _SEM_SPEC, *[_HBM_SPEC] * n, pl.BlockSpec(memory_space=pltpu.VMEM)),
        input_output_aliases={i: 2 + i for i in range(n)}, **_SPLIT_PARAMS)(*operands)
    return res[0], res[1], list(res[2:2 + n]), res[-1]


def _finish(copies, send_sems, recv_sems, arrays, after, *, name):
    n = len(arrays)

    def body(*refs):
        for dma in _descriptors(copies, refs[:n], refs[n], refs[n + 1]):
            dma.wait_send()
            dma.wait_recv()

    return list(pl.pallas_call(
        body, name=name, in_specs=[_HBM_SPEC] * n + [_SEM_SPEC, _SEM_SPEC] + [HBM] * len(after),
        out_shape=tuple(pltpu.HBM(v.shape, v.dtype) for v in arrays), out_specs=tuple([_HBM_SPEC] * n),
        input_output_aliases={i: i for i in range(n)}, **_SPLIT_PARAMS)(*arrays, send_sems, recv_sems, *after))


def _tie(value, tokens):
    if not tokens:
        return value

    def body(*refs):
        pass

    return pl.pallas_call(body, name="tie", in_specs=[HBM] * (1 + len(tokens)), out_specs=HBM,
                          out_shape=jax.ShapeDtypeStruct(value.shape, value.dtype), input_output_aliases={0: 0})(value, *tokens)


def _gather_copies(stage):
    def copies(refs):
        x, y, c = _place()
        me, sibling = (x, y, c), (x, y, 1 - c)
        out = []
        for ref in refs:
            r = ref.shape[0] // N_DEV
            if stage == "a":
                hops = [(me, sibling)] + [(me, (*_chip(k, x, y), c)) for k in (1, 2, 3)]
            else:
                hops = [((*_chip(k, x, y), c), sibling) for k in (1, 2, 3)]
            for block, to in hops:
                rows = _block_rows(ref, r, block)
                out.append((rows, rows, to))
        return out
    return copies


def _scatter_copies(stage, n):
    def copies(refs):
        x, y, c = _place()
        out = []
        for w in range(n):
            src, dst = refs[w], refs[n + w]
            if stage == "a":
                r = src.shape[0] // N_DEV
                out += [(_block_rows(src, r, (*_chip(k, x, y), 1 - c)), dst.at[k], (x, y, 1 - c)) for k in range(4)]
            else:
                out += [(src.at[k - 1], dst.at[k - 1], (*_chip(k, x, y), c)) for k in (1, 2, 3)]
        return out
    return copies


def _place_own(shards, layer, blocks, *, name):
    _, r, cdim = shards.shape

    def body(blocks_ref, s_ref, o_ref):
        o_ref[...] = s_ref[...].astype(BF16)

    return pl.pallas_call(
        body, name=name,
        grid_spec=pltpu.PrefetchScalarGridSpec(
            num_scalar_prefetch=1, grid=(1,), in_specs=[pl.BlockSpec((None, r, cdim), lambda i, blocks: (layer, 0, 0))],
            out_specs=pl.BlockSpec((r, cdim), lambda i, blocks: (blocks[0], 0))),
        out_shape=jax.ShapeDtypeStruct((N_DEV * r, cdim), BF16), compiler_params=_params("arbitrary"))(blocks, shards)


def _chip_sum(grad, got, blocks, *, name):
    _, r, cdim = got.shape
    tr = r if r <= 256 else r // (2 if r % 32 == 0 and r // 2 <= 256 else 4)
    steps = r // tr

    def body(blocks_ref, a0_ref, a1_ref, a2_ref, a3_ref, b_ref, mine_ref, away_ref):
        mine_ref[...] = a0_ref[...] + b_ref[0]
        for k, a_ref in ((1, a1_ref), (2, a2_ref), (3, a3_ref)):
            away_ref[k - 1] = (a_ref[...] + b_ref[k]).astype(BF16)

    own = [pl.BlockSpec((tr, cdim), lambda i, blocks, k=k: (blocks[k] * steps + i, 0)) for k in range(4)]
    return pl.pallas_call(
        body, name=name,
        grid_spec=pltpu.PrefetchScalarGridSpec(
            num_scalar_prefetch=1, grid=(steps,),
            in_specs=own + [pl.BlockSpec((4, tr, cdim), lambda i, blocks: (0, i, 0))],
            out_specs=[pl.BlockSpec((tr, cdim), lambda i, blocks: (i, 0)), pl.BlockSpec((3, tr, cdim), lambda i, blocks: (0, i, 0))]),
        out_shape=[jax.ShapeDtypeStruct((r, cdim), F32), jax.ShapeDtypeStruct((3, r, cdim), BF16)],
        compiler_params=_params("arbitrary"))(blocks, grad, grad, grad, grad, got)


def _final_sum(mine, got, *, name):
    r, cdim = mine.shape
    tr = r if r <= 256 else r // (2 if r % 32 == 0 and r // 2 <= 256 else 4)

    def body(m_ref, g_ref, o_ref):
        o_ref[...] = ((m_ref[...] + g_ref[0].astype(F32)) + g_ref[1].astype(F32)) + g_ref[2].astype(F32)

    return pl.pallas_call(
        body, name=name, grid=(r // tr,),
        in_specs=[pl.BlockSpec((tr, cdim), lambda i: (i, 0)), pl.BlockSpec((3, tr, cdim), lambda i: (0, i, 0))],
        out_specs=pl.BlockSpec((tr, cdim), lambda i: (i, 0)), out_shape=jax.ShapeDtypeStruct((r, cdim), F32),
        compiler_params=_params("parallel"))(mine, got)


def _adamw_math(w, g, m, v):
    m = ADAM_B1 * m + (1.0 - ADAM_B1) * g
    v = ADAM_B2 * v + (1.0 - ADAM_B2) * (g * g)
    m_hat = m / (1.0 - ADAM_B1 ** ADAM_STEP)
    v_hat = v / (1.0 - ADAM_B2 ** ADAM_STEP)
    return -ADAM_LR * (m_hat / (jnp.sqrt(v_hat) + ADAM_EPS) + ADAM_WD * w), m, v


def _adamw(w, g, m, v, *, name):
    L, r, cdim = w.shape
    tr = r if r <= 512 else r // 2
    assert r % tr == 0 and tr % 8 == 0

    def body(w_ref, g_ref, m_ref, v_ref, d_ref, nm_ref, nv_ref):
        d_ref[...], nm_ref[...], nv_ref[...] = _adamw_math(w_ref[...], g_ref[...], m_ref[...], v_ref[...])

    blk = pl.BlockSpec((None, tr, cdim), lambda l, i: (l, i, 0))
    return pl.pallas_call(body, name=name, grid=(L, r // tr), in_specs=[blk] * 4, out_specs=[blk] * 3,
                          out_shape=[jax.ShapeDtypeStruct(w.shape, F32)] * 3, compiler_params=_params("parallel", "parallel"))(w, g, m, v)


def _small_update(parts, w, m, v, *, name):
    _, p, lanes = parts.shape
    tp = p // 2 if p % 16 == 0 else p

    def body(p_ref, w_ref, m_ref, v_ref, g_ref, d_ref, nm_ref, nv_ref):
        g = p_ref[0]
        for i in range(1, N_DEV):
            g = g + p_ref[i]
        g_ref[...] = g
        d_ref[...], nm_ref[...], nv_ref[...] = _adamw_math(w_ref[...], g, m_ref[...], v_ref[...])

    blk = pl.BlockSpec((tp, lanes), lambda i: (i, 0))
    return pl.pallas_call(body, name=name, grid=(p // tp,), in_specs=[pl.BlockSpec((N_DEV, tp, lanes), lambda i: (0, i, 0)), blk, blk, blk],
                          out_specs=[blk] * 4, out_shape=[jax.ShapeDtypeStruct((p, lanes), F32)] * 4,
                          compiler_params=_params("parallel"))(parts, w, m, v)


BIG = ("w_in", "w_o", "w_xq", "w_xkv", "w_xo", "w_gate_up", "w_down")
TRANSPOSED = ("w_in", "w_xkv", "w_gate_up")
SMALL = ("mem_norm_g", "mix_pre_g", "mix_post_g", "gm_v_g", "gm_w_s", "gm_b_s", "pool_w", "pool_scale", "attn_sinks",
         "x_pre_g", "x_post_g", "ffn_pre_g", "ffn_post_g")
WEIGHTS = ("mem_norm_g", "mix_pre_g", "mix_post_g", "w_in", "gm_v_g", "gm_w_s", "gm_b_s", "pool_w", "pool_scale", "attn_sinks", "w_o",
           "x_pre_g", "x_post_g", "w_xq", "w_xkv", "w_xo", "ffn_pre_g", "ffn_post_g", "w_gate_up", "w_down")
PACK_QUANTUM = 8 * 128


def _pack(arrays):
    flat = []
    for a in arrays:
        a = a.reshape(-1).astype(F32)
        flat.append(jnp.pad(a, (0, -a.size % PACK_QUANTUM)))
    return jnp.concatenate(flat).reshape(-1, 128)


def _unpack(pack, shapes):
    out, at = [], 0
    flat = pack.reshape(-1)
    for s in shapes:
        size = math.prod(s)
        out.append(flat[at:at + size].reshape(s))
        at += size + (-size % PACK_QUANTUM)
    return out


def _row(v):
    return v.reshape(1, -1)


def _local_step(x, mem, positions, target, small, weight, fwd_hook=None, bwd_hook=None):
    fwd_hook = fwd_hook or (lambda l, point, v: v)
    bwd_hook = bwd_hook or (lambda l, point, v, gb: v)
    depth = small["mix_pre_g"].shape[0]
    half = HEAD_DIM // 2
    inv = ROPE_THETA ** (-jnp.arange(half, dtype=F32) / half)
    cos_t, sin_t = _rope_tables(positions.reshape(-1, 1), jnp.tile(inv, 128 // half).reshape(1, 128), name="rope_tables")
    mem_g = _row(small["mem_norm_g"])
    saved = []
    for l in range(depth):
        w = functools.partial(weight, l)
        bfull = jnp.repeat(small["gm_b_s"][l].T, HEAD_DIM, axis=1)
        bd = jax.scipy.linalg.block_diag(*[small["pool_w"][l, g] for g in range(len(POOL_WINDOWS))])
        mix_par = (_row(small["gm_v_g"][l]), small["gm_w_s"][l], bfull, bd, _row(small["pool_scale"][l]), small["attn_sinks"][l])
        x = fwd_hook(l, 0, x)
        z, h1 = _mm_fwd(x, w("w_in"), nt=True, pre="norm", g_pre=_row(small["mix_pre_g"][l]), out_dtype=F32, name="in_proj")
        cat = fwd_hook(l, 1, _mixer_fwd(z, cos_t, sin_t, *mix_par, name="mixer_fwd"))
        mix, x1 = _mm_fwd(cat, w("w_o"), nt=False, post="norm_res", g_post=_row(small["mix_post_g"][l]), xres=x, name="mix_out")
        qx, h2 = _mm_fwd(x1, w("w_xq"), nt=False, pre="norm", g_pre=_row(small["x_pre_g"][l]), name="xq_proj")
        kv, memn = _mm_fwd(mem, w("w_xkv"), nt=True, pre="norm", g_pre=mem_g, name="xkv_proj")
        ox = _xattn_fwd(qx, kv, name="xattn_fwd")
        xo, x2 = _mm_fwd(ox, w("w_xo"), nt=False, post="norm_res", g_post=_row(small["x_post_g"][l]), xres=x1, name="xattn_out")
        x2 = fwd_hook(l, 2, x2)
        gu, h3 = _mm_fwd(x2, w("w_gate_up"), nt=True, pre="norm", g_pre=_row(small["ffn_pre_g"][l]), name="ffn_in")
        f, x3 = _mm_fwd(gu, w("w_down"), nt=False, pre="swiglu", post="norm_res", g_post=_row(small["ffn_post_g"][l]), xres=x2,
                        tile=256, name="ffn_out")
        saved.append((mix_par, x, z, h1, cat, mix, x1, qx, h2, kv, memn, ox, xo, x2, gu, h3, f))
        x = fwd_hook(l, 3, x3)

    loss, dx = _loss_grad(x, target, name="loss")

    gs = {n: [None] * depth for n in SMALL if n != "mem_norm_g"}
    gb = {n: [None] * depth for n in BIG}
    dmemn = [None] * depth
    for l in reversed(range(depth)):
        w = functools.partial(weight, l)
        mix_par, x0, z, h1, cat, mix, x1, qx, h2, kv, memn, ox, xo, x2, gu, h3, f = saved[l]
        dgu, df, gs["ffn_post_g"][l] = _ffn_out_bwd(f, dx, _row(small["ffn_post_g"][l]), w("w_down"), gu, name="ffn_out_bwd")
        gb["w_down"][l] = _wgrad(gu, df, swiglu=True, name="w_down_grad")
        gb["w_gate_up"][l] = _wgrad(dgu, h3, name="w_gate_up_grad")
        dx2, gs["ffn_pre_g"][l] = _mm_bwd_post(dgu, w("w_gate_up"), x2, _row(small["ffn_pre_g"][l]), dx, nt=False, tile=256, name="ffn_in_bwd")
        dx2 = bwd_hook(l, 1, dx2, gb)
        do, dxo, gs["x_post_g"][l] = _mm_bwd_pre(xo, dx2, _row(small["x_post_g"][l]), w("w_xo"), name="xattn_out_bwd")
        gb["w_xo"][l] = _wgrad(ox, dxo, name="w_xo_grad")
        dqx, dkv = _xattn_bwd(qx, kv, do, name="xattn_bwd")
        gb["w_xq"][l] = _wgrad(h2, dqx, name="w_xq_grad")
        dx1, gs["x_pre_g"][l] = _mm_bwd_post(dqx, w("w_xq"), x1, _row(small["x_pre_g"][l]), dx2, nt=True, name="xq_proj_bwd")
        dx1 = bwd_hook(l, 2, dx1, gb)
        gb["w_xkv"][l] = _wgrad(dkv, memn, name="w_xkv_grad")
        dmemn[l] = _mm_fwd(dkv, w("w_xkv"), nt=False, out_dtype=F32, name="xkv_proj_bwd")[0]
        dcat, dmix, gs["mix_post_g"][l] = _mm_bwd_pre(mix, dx1, _row(small["mix_post_g"][l]), w("w_o"), name="mix_out_bwd")
        gb["w_o"][l] = _wgrad(cat, dmix, name="w_o_grad")
        dz, dvg, dwc, dbf, dbd, dscale, dsink = _mixer_bwd(z, dcat, cos_t, sin_t, *mix_par, name="mixer_bwd")
        gs["gm_v_g"][l], gs["gm_w_s"][l], gs["pool_scale"][l], gs["attn_sinks"][l] = dvg, dwc, dscale, dsink[:, 0]
        gs["gm_b_s"][l] = dbf.reshape(BLOCK, -1, HEAD_DIM).sum(-1).T
        gs["pool_w"][l] = jnp.stack([dbd[HEAD_DIM * g:HEAD_DIM * (g + 1), HEAD_DIM * g:HEAD_DIM * (g + 1)] for g in range(len(POOL_WINDOWS))])
        gb["w_in"][l] = _wgrad(dz, h1, name="w_in_grad")
        dx, gs["mix_pre_g"][l] = _mm_bwd_post(dz, w("w_in"), x0, _row(small["mix_pre_g"][l]), dx1, nt=False, name="in_proj_bwd")
        dx = bwd_hook(l, 3, dx, gb)

    small_grads = {n: jnp.stack(v).reshape(small[n].shape) for n, v in gs.items()}
    small_grads["mem_norm_g"] = _norm_dgain(mem, jnp.stack(dmemn), name="mem_norm_grad").reshape(-1)
    return loss, dx, small_grads, gb


EARLY = ("w_in", "w_o", "w_xq", "w_xkv", "w_xo")
LATE = ("w_gate_up", "w_down")


def _step(a):
    depth = a["mix_pre_g"].shape[0]
    x, mem, target = a["x"][0], a["mem"][0], a["loss_target"][0]
    px, py, pc = _place()
    blocks = jnp.stack([4 * cx + 2 * cy + pc for cx, cy in (_chip(k, px, py) for k in range(4))]).astype(jnp.int32)
    flight = {}

    def laid_out(v, n):
        return v.transpose(0, 2, 1) if n in TRANSPOSED else v

    whole = {(l, n): _place_own(laid_out(a[n], n), l, blocks, name="place_" + n) for l in range(depth) for n in BIG}
    gather = {"first_early": [(0, n) for n in EARLY], "first_late": [(0, n) for n in LATE]}
    gather.update({f"layer{l}": [(l, n) for n in BIG] for l in range(1, depth)})
    forward_plan = {(0, 0): [("a", "first_early"), ("a", "first_late"), ("a", "layer1"), ("mid", "first_early"), ("end", "first_early")],
                    (0, 1): [("mid", "first_late")], (0, 2): [("end", "first_late"), ("mid", "layer1")], (0, 3): [("end", "layer1")]}
    for l in range(1, depth):
        forward_plan.update({(l, 0): [("a", f"layer{l + 1}")], (l, 2): [("mid", f"layer{l + 1}")], (l, 3): [("end", f"layer{l + 1}")]})

    issued = []

    def gather_step(kind, tag, v):
        keys = gather[tag]
        if kind == "a":
            arrays = [whole[k] for k in keys]
            arrays[0] = _tie(arrays[0], issued[-1:])
            send, recv, arrays, token = _start(_gather_copies("a"), arrays, [], 4 * len(keys), name=f"gather_a_{tag}")
            flight[tag] = (send, recv, arrays)
            issued.append(token)
            return [token]
        send, recv, arrays = flight.pop(tag)
        if kind == "mid":
            arrays = _finish(_gather_copies("a"), send, recv, arrays, [v], name=f"gather_a_done_{tag}")
            send, recv, arrays, token = _start(_gather_copies("b"), arrays, [], 3 * len(keys), name=f"gather_b_{tag}")
            flight[tag] = (send, recv, arrays)
            return [token]
        for key, arr in zip(keys, _finish(_gather_copies("b"), send, recv, arrays, [v], name=f"gather_b_done_{tag}")):
            whole[key] = arr
        return []

    def fwd_hook(l, point, v):
        tokens = []
        for kind, tag in forward_plan.get((l, point), []):
            if tag in gather:
                tokens += gather_step(kind, tag, v)
        return _tie(v, tokens)

    mine = {}
    scatter = {f"{part}{l}": [(l, n) for n in names] for l in range(depth) for part, names in (("early", EARLY), ("late", LATE))}
    backward_plan = {}
    for l in range(depth):
        backward_plan[(l, 1)] = [("a", f"late{l}")] + ([("mid", f"early{l + 1}")] if l + 1 < depth else [])
        backward_plan[(l, 2)] = [("mid", f"late{l}")] + ([("end", f"early{l + 1}")] if l + 1 < depth else [])
        backward_plan[(l, 3)] = [("end", f"late{l}"), ("a", f"early{l}")]
    backward_plan[(0, 3)] += [("mid", "early0"), ("end", "early0")]

    def scatter_step(kind, tag, v, gb):
        keys = scatter[tag]
        n = len(keys)
        if kind == "a":
            grads = [gb[name][l] for l, name in keys]
            zones = [jax.ShapeDtypeStruct((4, g.shape[0] // N_DEV, g.shape[1]), F32) for g in grads]
            send, recv, arrays, token = _start(_scatter_copies("a", n), grads, zones, 4 * n, name=f"scatter_a_{tag}")
            flight[tag] = (send, recv, arrays, None)
            return [token]
        send, recv, arrays, kept = flight.pop(tag)
        if kind == "mid":
            arrays = _finish(_scatter_copies("a", n), send, recv, arrays, [v], name=f"scatter_a_done_{tag}")
            sums = [_chip_sum(arrays[i], arrays[n + i], blocks, name="chip_sum") for i in range(n)]
            zones = [jax.ShapeDtypeStruct(s[1].shape, BF16) for s in sums]
            send, recv, arrays, token = _start(_scatter_copies("b", n), [s[1] for s in sums], zones, 3 * n, name=f"scatter_b_{tag}")
            flight[tag] = (send, recv, arrays, [s[0] for s in sums])
            return [token]
        arrays = _finish(_scatter_copies("b", n), send, recv, arrays, [v], name=f"scatter_b_done_{tag}")
        for i, key in enumerate(keys):
            mine[key] = _final_sum(kept[i], arrays[n + i], name="final_sum")
        return []

    def bwd_hook(l, point, v, gb):
        tokens = []
        for kind, tag in backward_plan.get((l, point), []):
            tokens += scatter_step(kind, tag, v, gb)
        return _tie(v, tokens)

    loss, dx, small_grads, _ = _local_step(x, mem, a["positions"][0], target, {n: a[n] for n in SMALL}, lambda l, n: whole[(l, n)],
                                           fwd_hook, bwd_hook)

    out = {}
    for n in BIG:
        g = jnp.stack([mine[(l, n)] for l in range(depth)])
        moved = _adamw(laid_out(a[n], n), g, laid_out(a["m_" + n], n), laid_out(a["v_" + n], n), name="adamw_" + n)
        for p, v in zip(("grad_", "delta_", "new_m_", "new_v_"), (g, *moved)):
            out[p + n] = laid_out(v, n)

    part = _pack([small_grads[n] for n in SMALL])
    parts = _all_gather([part], [[0]], part, name="gather_small_grads")[0].reshape(N_DEV, *part.shape)
    packs = _small_update(parts, *[_pack([a[p + n] for n in SMALL]) for p in ("", "m_", "v_")], name="small_update")
    for p, pack in zip(("grad_", "delta_", "new_m_", "new_v_"), packs):
        for n, v in zip(SMALL, _unpack(pack, [a[n].shape for n in SMALL])):
            out[p + n] = v

    total = lax.psum(loss[0, 0], ("x", "y", "c"))
    return (total, dx[None], *[out[p + n] for p in ("grad_", "delta_", "new_m_", "new_v_") for n in WEIGHTS])


def kernel(x, mem, positions, mem_norm_g, mix_pre_g, mix_post_g, w_in, gm_v_g, gm_w_s, gm_b_s, pool_w, pool_scale, attn_sinks,
           w_o, x_pre_g, x_post_g, w_xq, w_xkv, w_xo, ffn_pre_g, ffn_post_g, w_gate_up, w_down, loss_target, m_mem_norm_g,
           m_mix_pre_g, m_mix_post_g, m_w_in, m_gm_v_g, m_gm_w_s, m_gm_b_s, m_pool_w, m_pool_scale, m_attn_sinks, m_w_o,
           m_x_pre_g, m_x_post_g, m_w_xq, m_w_xkv, m_w_xo, m_ffn_pre_g, m_ffn_post_g, m_w_gate_up, m_w_down, v_mem_norm_g,
           v_mix_pre_g, v_mix_post_g, v_w_in, v_gm_v_g, v_gm_w_s, v_gm_b_s, v_pool_w, v_pool_scale, v_attn_sinks, v_w_o,
           v_x_pre_g, v_x_post_g, v_w_xq, v_w_xkv, v_w_xo, v_ffn_pre_g, v_ffn_post_g, v_w_gate_up, v_w_down):
    return _step(dict(locals()))
```

```python
import functools
import math

import jax
import jax.numpy as jnp
from jax import lax
from jax.experimental import pallas as pl
from jax.experimental.pallas import tpu as pltpu

F32, BF16 = jnp.float32, jnp.bfloat16
EPS = 1e-6
HEAD_DIM = 64
BLOCK = 128
POOL_WINDOWS = (2, 4, 8, 16)
ATT_Q_HEADS = 8
X_HEADS = 4
ROPE_THETA = 10000.0
ADAM_LR, ADAM_B1, ADAM_B2, ADAM_EPS, ADAM_WD, ADAM_STEP = 0.001, 0.9, 0.999, 1e-08, 0.01, 10
N_DEV = 8
TOKEN_TILE = 512
VMEM_LIMIT_BYTES = 50 * 2**20
NEG = -1e30
MESH = pl.DeviceIdType.MESH

NN = ((1,), (0,))
NT = ((1,), (1,))
TN = ((0,), (0,))


def _dot(a, b, dims):
    return lax.dot_general(a.astype(BF16), b.astype(BF16), (dims, ((), ())), preferred_element_type=F32)


def _params(*sem):
    return pltpu.CompilerParams(dimension_semantics=sem, vmem_limit_bytes=VMEM_LIMIT_BYTES)


def _full(shape):
    return pl.BlockSpec(shape, lambda *_: (0,) * len(shape))


def _lane(shape):
    return lax.broadcasted_iota(jnp.int32, shape, len(shape) - 1)


def _rms_fwd(x, g):
    r = lax.rsqrt(jnp.mean(x * x, axis=-1, keepdims=True) + EPS)
    return x * r * g


def _rms_bwd(x, g, dy):
    r = lax.rsqrt(jnp.mean(x * x, axis=-1, keepdims=True) + EPS)
    xh = x * r
    dg = jnp.sum(dy * xh, axis=0, keepdims=True)
    dxh = dy * g
    dx = r * (dxh - xh * jnp.mean(dxh * xh, axis=-1, keepdims=True))
    return dx, dg


def _silu_parts(gate):
    sg = 1.0 / (1.0 + jnp.exp(-gate))
    return gate * sg, sg


def _swiglu_cols(gu_ref, out_ref, width):
    step = 128 * max(1, (width // 128) // 4)
    for c0 in range(0, width, step):
        c1 = min(width, c0 + step)
        gate = gu_ref[:, c0:c1].astype(F32)
        up = gu_ref[:, width + c0:width + c1].astype(F32)
        out_ref[:, c0:c1] = (_silu_parts(gate)[0] * up).astype(out_ref.dtype)


def _col_tile(n, cap=1408):
    best = n
    for t in range(128, cap + 1, 128):
        if n % t == 0:
            best = t
    return best if n > cap else n


def _mm_fwd(a, w, *, nt, name, pre=None, g_pre=None, post=None, g_post=None, xres=None, out_dtype=BF16, tile=TOKEN_TILE):
    S = a.shape[0]
    n_out, k = (w.shape[0], w.shape[1]) if nt else (w.shape[1], w.shape[0])
    T = min(tile, S)
    tn = n_out if post else _col_tile(n_out)
    grid = (S // T, n_out // tn)

    def body(*refs):
        it = iter(refs)
        a_ref, w_ref = next(it), next(it)
        gpre_ref = next(it) if pre == "norm" else None
        gpost_ref, x_ref = (next(it), next(it)) if post else (None, None)
        outs = [next(it) for _ in range((2 if post else 1) + (1 if pre == "norm" else 0))]
        as_ref = next(it) if pre else None
        if pre:
            @pl.when(pl.program_id(1) == 0)
            def _():
                if pre == "norm":
                    h = _rms_fwd(a_ref[...], gpre_ref[...]).astype(BF16)
                    outs[-1][...] = h
                    as_ref[...] = h
                else:
                    _swiglu_cols(a_ref, as_ref, k)
            av = as_ref[...]
        else:
            av = a_ref[...]
        acc = _dot(av, w_ref[...], NT if nt else NN)
        if post:
            outs[0][...] = acc.astype(BF16)
            outs[1][...] = x_ref[...] + _rms_fwd(acc, gpost_ref[...])
        else:
            outs[0][...] = acc.astype(out_dtype)

    in_specs = [pl.BlockSpec((T, a.shape[1]), lambda i, j: (i, 0)),
                pl.BlockSpec((tn, k), lambda i, j: (j, 0)) if nt else pl.BlockSpec((k, tn), lambda i, j: (0, j))]
    args = [a, w]
    if pre == "norm":
        in_specs.append(_full((1, k)))
        args.append(g_pre)
    if post:
        in_specs += [_full((1, n_out)), pl.BlockSpec((T, n_out), lambda i, j: (i, 0))]
        args += [g_post, xres]
    out_block = pl.BlockSpec((T, tn), lambda i, j: (i, j))
    if post:
        out_shape = [jax.ShapeDtypeStruct((S, n_out), BF16), jax.ShapeDtypeStruct((S, n_out), F32)]
        out_specs = [out_block, out_block]
    else:
        out_shape = [jax.ShapeDtypeStruct((S, n_out), out_dtype)]
        out_specs = [out_block]
    if pre == "norm":
        out_shape.append(jax.ShapeDtypeStruct((S, k), BF16))
        out_specs.append(pl.BlockSpec((T, k), lambda i, j: (i, 0)))
    scratch = [pltpu.VMEM((T, k), BF16)] if pre else []
    return pl.pallas_call(body, name=name, grid=grid, in_specs=in_specs, out_specs=out_specs, out_shape=out_shape,
                          scratch_shapes=scratch, compiler_params=_params("parallel", "arbitrary"))(*args)


def _mm_bwd_pre(m, dx, g, w, *, name, tile=TOKEN_TILE):
    S, n = m.shape
    ku = w.shape[0]
    T = min(tile, S)
    tn = _col_tile(ku)
    grid = (S // T, ku // tn)

    def body(m_ref, dx_ref, g_ref, w_ref, du_ref, dm_ref, dg_ref, as_ref):
        i, j = pl.program_id(0), pl.program_id(1)

        @pl.when((i == 0) & (j == 0))
        def _():
            dg_ref[...] = jnp.zeros_like(dg_ref)

        @pl.when(j == 0)
        def _():
            dmv, dg = _rms_bwd(m_ref[...].astype(F32), g_ref[...], dx_ref[...])
            dg_ref[...] += dg
            dmb = dmv.astype(BF16)
            dm_ref[...] = dmb
            as_ref[...] = dmb

        du_ref[...] = _dot(as_ref[...], w_ref[...], NT).astype(BF16)

    row = pl.BlockSpec((T, n), lambda i, j: (i, 0))
    return pl.pallas_call(
        body, name=name, grid=grid,
        in_specs=[row, row, _full((1, n)), pl.BlockSpec((tn, n), lambda i, j: (j, 0))],
        out_specs=[pl.BlockSpec((T, tn), lambda i, j: (i, j)), row, _full((1, n))],
        out_shape=[jax.ShapeDtypeStruct((S, ku), BF16), jax.ShapeDtypeStruct((S, n), BF16), jax.ShapeDtypeStruct((1, n), F32)],
        scratch_shapes=[pltpu.VMEM((T, n), BF16)], compiler_params=_params("arbitrary", "arbitrary"))(m, dx, g, w)


def _mm_bwd_post(dy, w, x, g, dx_in, *, nt, name, tile=TOKEN_TILE):
    S, k = dy.shape
    n = x.shape[1]
    T = min(tile, S)

    def body(dy_ref, w_ref, x_ref, g_ref, dxi_ref, dxo_ref, dg_ref):
        @pl.when(pl.program_id(0) == 0)
        def _():
            dg_ref[...] = jnp.zeros_like(dg_ref)

        dh = _dot(dy_ref[...], w_ref[...], NT if nt else NN)
        dxn, dg = _rms_bwd(x_ref[...], g_ref[...], dh)
        dg_ref[...] += dg
        dxo_ref[...] = dxi_ref[...] + dxn

    row = pl.BlockSpec((T, n), lambda i: (i, 0))
    return pl.pallas_call(
        body, name=name, grid=(S // T,),
        in_specs=[pl.BlockSpec((T, k), lambda i: (i, 0)), _full(w.shape), row, _full((1, n)), row],
        out_specs=[row, _full((1, n))],
        out_shape=[jax.ShapeDtypeStruct((S, n), F32), jax.ShapeDtypeStruct((1, n), F32)],
        compiler_params=_params("arbitrary"))(dy, w, x, g, dx_in)


def _wgrad(a, b, *, name, swiglu=False, tile=TOKEN_TILE):
    S = a.shape[0]
    k = a.shape[1] // 2 if swiglu else a.shape[1]
    n = b.shape[1]
    T = min(tile, S)
    tk = 512 if k % 512 == 0 else 256
    nk = k // tk

    def body(*refs):
        if swiglu:
            gate_ref, up_ref, b_ref, o_ref, act_ref = refs
            for r0 in range(0, S, T):
                gate, up = gate_ref[r0:r0 + T, :].astype(F32), up_ref[r0:r0 + T, :].astype(F32)
                act_ref[r0:r0 + T, :] = (_silu_parts(gate)[0] * up).astype(BF16)
            av = act_ref[...]
        else:
            a_ref, b_ref, o_ref = refs
            av = a_ref[...]
        o_ref[...] = _dot(av, b_ref[...], TN)

    in_specs = [pl.BlockSpec((S, tk), lambda kk: (0, kk))]
    args = [a]
    if swiglu:
        in_specs.append(pl.BlockSpec((S, tk), lambda kk: (0, kk + nk)))
        args.append(a)
    in_specs.append(_full((S, n)))
    args.append(b)
    return pl.pallas_call(body, name=name, grid=(nk,), in_specs=in_specs,
                          out_specs=pl.BlockSpec((tk, n), lambda kk: (kk, 0)),
                          out_shape=jax.ShapeDtypeStruct((k, n), F32),
                          scratch_shapes=[pltpu.VMEM((S, tk), BF16)] if swiglu else [],
                          compiler_params=_params("parallel"))(*args)


def _ffn_out_bwd(m, dx, g, w, gu, *, name, tile=256):
    S, n = m.shape
    f = w.shape[0]
    T = min(tile, S)

    def body(m_ref, dx_ref, g_ref, w_ref, gu_ref, dgu_ref, dm_ref, dg_ref, da_ref):
        @pl.when(pl.program_id(0) == 0)
        def _():
            dg_ref[...] = jnp.zeros_like(dg_ref)

        dmv, dg = _rms_bwd(m_ref[...].astype(F32), g_ref[...], dx_ref[...])
        dg_ref[...] += dg
        dmb = dmv.astype(BF16)
        dm_ref[...] = dmb
        da_ref[...] = _dot(dmb, w_ref[...], NT)
        step = 128 * max(1, (f // 128) // 4)
        for c0 in range(0, f, step):
            c1 = min(f, c0 + step)
            gate = gu_ref[:, c0:c1].astype(F32)
            up = gu_ref[:, f + c0:f + c1].astype(F32)
            da = da_ref[:, c0:c1]
            sl, sg = _silu_parts(gate)
            dgu_ref[:, c0:c1] = (da * up * (sg + sl * (1.0 - sg))).astype(BF16)
            dgu_ref[:, f + c0:f + c1] = (da * sl).astype(BF16)

    row = pl.BlockSpec((T, n), lambda i: (i, 0))
    wide = pl.BlockSpec((T, 2 * f), lambda i: (i, 0))
    return pl.pallas_call(
        body, name=name, grid=(S // T,), in_specs=[row, row, _full((1, n)), _full((f, n)), wide],
        out_specs=[wide, row, _full((1, n))],
        out_shape=[jax.ShapeDtypeStruct((S, 2 * f), BF16), jax.ShapeDtypeStruct((S, n), BF16), jax.ShapeDtypeStruct((1, n), F32)],
        scratch_shapes=[pltpu.VMEM((T, f), F32)], compiler_params=_params("arbitrary"))(m, dx, g, w, gu)


def _loss_grad(y, target, *, name, tile=TOKEN_TILE):
    S, d = y.shape
    T = min(tile, S)
    steps = S // T

    def body(y_ref, t_ref, loss_ref, dy_ref, acc_ref):
        i = pl.program_id(0)

        @pl.when(i == 0)
        def _():
            acc_ref[...] = jnp.zeros_like(acc_ref)

        diff = y_ref[...] - t_ref[...]
        dy_ref[...] = diff * (1.0 / d)
        acc_ref[...] += jnp.sum(diff * diff, axis=0, keepdims=True)

        @pl.when(i == steps - 1)
        def _():
            total = jnp.sum(acc_ref[...], axis=1, keepdims=True) * (0.5 / d)
            loss_ref[...] = jnp.broadcast_to(total, loss_ref.shape)

    row = pl.BlockSpec((T, d), lambda i: (i, 0))
    return pl.pallas_call(
        body, name=name, grid=(steps,), in_specs=[row, row], out_specs=[_full((1, 128)), row],
        out_shape=[jax.ShapeDtypeStruct((1, 128), F32), jax.ShapeDtypeStruct((S, d), F32)],
        scratch_shapes=[pltpu.VMEM((1, d), F32)], compiler_params=_params("arbitrary"))(y, target)


def _norm_dgain(x, dys, *, name):
    def body(x_ref, dy_ref, o_ref):
        xv = x_ref[...]
        dy = dy_ref[0]
        for l in range(1, dys.shape[0]):
            dy = dy + dy_ref[l]
        r = lax.rsqrt(jnp.mean(xv * xv, axis=-1, keepdims=True) + EPS)
        o_ref[...] = jnp.sum(dy * xv * r, axis=0, keepdims=True)

    return pl.pallas_call(body, name=name, out_shape=jax.ShapeDtypeStruct((1, x.shape[1]), F32),
                          compiler_params=pltpu.CompilerParams(vmem_limit_bytes=VMEM_LIMIT_BYTES))(x, dys)


def _xattn_probs(q_ref, kv_ref, h, d, hd):
    q = q_ref[:, h * hd:(h + 1) * hd]
    kh = kv_ref[:, h * hd:(h + 1) * hd]
    vh = kv_ref[:, d + h * hd:d + (h + 1) * hd]
    s = _dot(q, kh, NT) * (hd ** -0.5)
    e = jnp.exp(s - jnp.max(s, axis=-1, keepdims=True))
    return q, kh, vh, e / jnp.sum(e, axis=-1, keepdims=True)


def _xattn_fwd(q, kv, *, name, tile=TOKEN_TILE):
    S, d = q.shape
    mlen = kv.shape[0]
    hd = d // X_HEADS
    T = min(tile, S)

    def body(q_ref, kv_ref, o_ref):
        for h in range(X_HEADS):
            _, _, vh, p = _xattn_probs(q_ref, kv_ref, h, d, hd)
            o_ref[:, h * hd:(h + 1) * hd] = _dot(p, vh, NN).astype(BF16)

    row = pl.BlockSpec((T, d), lambda i: (i, 0))
    return pl.pallas_call(body, name=name, grid=(S // T,), in_specs=[row, _full((mlen, 2 * d))], out_specs=row,
                          out_shape=jax.ShapeDtypeStruct((S, d), BF16), compiler_params=_params("parallel"))(q, kv)


def _xattn_bwd(q, kv, do, *, name, tile=TOKEN_TILE):
    S, d = q.shape
    mlen = kv.shape[0]
    hd = d // X_HEADS
    T = min(tile, S)

    def body(q_ref, kv_ref, do_ref, dq_ref, dkv_ref):
        @pl.when(pl.program_id(0) == 0)
        def _():
            dkv_ref[...] = jnp.zeros_like(dkv_ref)

        for h in range(X_HEADS):
            qh, kh, vh, p = _xattn_probs(q_ref, kv_ref, h, d, hd)
            doh = do_ref[:, h * hd:(h + 1) * hd]
            dp = _dot(doh, vh, NT)
            ds = p * (dp - jnp.sum(p * dp, axis=-1, keepdims=True)) * (hd ** -0.5)
            dq_ref[:, h * hd:(h + 1) * hd] = _dot(ds, kh, NN).astype(BF16)
            dkv_ref[:, h * hd:(h + 1) * hd] += _dot(ds, qh, TN)
            dkv_ref[:, d + h * hd:d + (h + 1) * hd] += _dot(p, doh, TN)

    row = pl.BlockSpec((T, d), lambda i: (i, 0))
    return pl.pallas_call(
        body, name=name, grid=(S // T,), in_specs=[row, _full((mlen, 2 * d)), row],
        out_specs=[row, _full((mlen, 2 * d))],
        out_shape=[jax.ShapeDtypeStruct((S, d), BF16), jax.ShapeDtypeStruct((mlen, 2 * d), F32)],
        compiler_params=_params("arbitrary"))(q, kv, do)


GM0, POOL0, Q0, KV0, Z_END = 0, 512, 768, 1280, 1536


def _rope_tables(pos_col, inv_row, *, name):
    S = pos_col.shape[0]

    def body(p_ref, inv_ref, cos_ref, sin_ref):
        ang = p_ref[...].astype(F32) * inv_ref[...]
        sin = jnp.sin(ang)
        cos_ref[...] = jnp.cos(ang)
        sin_ref[...] = jnp.where(_lane(ang.shape) % HEAD_DIM < HEAD_DIM // 2, -sin, sin)

    T = min(TOKEN_TILE, S)
    return pl.pallas_call(
        body, name=name, grid=(S // T,),
        in_specs=[pl.BlockSpec((T, 1), lambda i: (i, 0)), _full((1, 128))],
        out_specs=[pl.BlockSpec((T, 128), lambda i: (i, 0))] * 2,
        out_shape=[jax.ShapeDtypeStruct((S, 128), F32)] * 2, compiler_params=_params("parallel"))(pos_col, inv_row)


def _swap_halves(x):
    n = x.shape[-1]
    return jnp.where(_lane(x.shape) % HEAD_DIM < HEAD_DIM // 2, pltpu.roll(x, n - HEAD_DIM // 2, 1), pltpu.roll(x, HEAD_DIM // 2, 1))


def _rope(x, cos, sin_s):
    reps = x.shape[-1] // 128
    if reps > 1:
        cos, sin_s = jnp.tile(cos, (1, reps)), jnp.tile(sin_s, (1, reps))
    return x * cos + _swap_halves(x) * sin_s


def _rope_bwd(dy, cos, sin_s):
    reps = dy.shape[-1] // 128
    if reps > 1:
        cos, sin_s = jnp.tile(cos, (1, reps)), jnp.tile(sin_s, (1, reps))
    return dy * cos + _swap_halves(dy * sin_s)


def _gelu_parts(x):
    c = math.sqrt(2.0 / math.pi)
    t = jnp.tanh(c * (x + 0.044715 * x * x * x))
    return 0.5 * x * (1.0 + t), t


def _gelu_grad(x, t):
    c = math.sqrt(2.0 / math.pi)
    return 0.5 * (1.0 + t) + 0.5 * x * (1.0 - t * t) * c * (1.0 + 3.0 * 0.044715 * x * x)


def _group_sum(x):
    gid = _lane(x.shape) // HEAD_DIM
    out = jnp.zeros_like(x)
    for g in range(x.shape[-1] // HEAD_DIM):
        sel = gid == g
        out = jnp.where(sel, jnp.sum(jnp.where(sel, x, 0.0), axis=-1, keepdims=True), out)
    return out


def _low_half(shape):
    return _lane(shape) % 128 < HEAD_DIM


def _tril(w):
    r = lax.broadcasted_iota(jnp.int32, w.shape, 0)
    c = lax.broadcasted_iota(jnp.int32, w.shape, 1)
    return jnp.where(r >= c, w, 0.0)


def _gating_fwd(zg, vg, wc_ref, bfull):
    ge, th = _gelu_parts(zg)
    u, v = ge[:, :256], ge[:, 256:]
    r = lax.rsqrt(_group_sum(v * v) * (1.0 / HEAD_DIM) + EPS)
    vhat = v * r
    vn = vhat * vg
    low = _low_half((BLOCK, 128))
    blocks = []
    for blk in range(2):
        vb = vn[:, 128 * blk:128 * (blk + 1)]
        m0 = _dot(_tril(wc_ref[2 * blk]), vb, NN)
        m1 = _dot(_tril(wc_ref[2 * blk + 1]), vb, NN)
        blocks.append(jnp.where(low, m0, m1))
    mixed = jnp.concatenate(blocks, axis=1) + bfull
    return u * mixed, (th, u, r, vhat, vn, mixed)


def _band(w, transpose):
    if transpose:
        s2 = lax.broadcasted_iota(jnp.int32, (2 * BLOCK, BLOCK), 0)
        t = lax.broadcasted_iota(jnp.int32, (2 * BLOCK, BLOCK), 1)
    else:
        t = lax.broadcasted_iota(jnp.int32, (BLOCK, 2 * BLOCK), 0)
        s2 = lax.broadcasted_iota(jnp.int32, (BLOCK, 2 * BLOCK), 1)
    d = t - s2 + BLOCK
    return jnp.where((d >= 0) & (d < w), 1.0, 0.0).astype(BF16)


def _exact_dot(b01, x):
    hi = x.astype(BF16)
    lo = (x - hi.astype(F32)).astype(BF16)
    return _dot(b01, hi, NN) + _dot(b01, lo, NN)


def _inv_count(n):
    t = lax.broadcasted_iota(jnp.int32, (BLOCK, 256), 0) + n * BLOCK
    gid = _lane((BLOCK, 256)) // HEAD_DIM
    win = jnp.where(gid == 0, POOL_WINDOWS[0], jnp.where(gid == 1, POOL_WINDOWS[1], jnp.where(gid == 2, POOL_WINDOWS[2], POOL_WINDOWS[3])))
    return 1.0 / jnp.minimum(t + 1, win).astype(F32)


def _pool_fwd(p, pp, n, bd, scale):
    low = _low_half((BLOCK, 128))
    blocks = []
    for blk in range(2):
        both = jnp.concatenate([pp[:, 128 * blk:128 * (blk + 1)], p[:, 128 * blk:128 * (blk + 1)]], axis=0)
        s0 = _exact_dot(_band(POOL_WINDOWS[2 * blk], False), both)
        s1 = _exact_dot(_band(POOL_WINDOWS[2 * blk + 1], False), both)
        blocks.append(jnp.where(low, s0, s1))
    inv = _inv_count(n)
    pooled = jnp.concatenate(blocks, axis=1) * inv - p
    mapped = _dot(pooled, bd, NN)
    return mapped * scale, (inv, pooled, mapped)


def _attn_mask(first_key):
    qi = lax.broadcasted_iota(jnp.int32, (BLOCK, 2 * BLOCK), 0)
    ki = lax.broadcasted_iota(jnp.int32, (BLOCK, 2 * BLOCK), 1)
    return (ki > qi) & (ki <= qi + BLOCK) & (ki >= first_key)


def _attn_head(j, qrot, kk, kks, vv, vvs, sinks_ref, valid):
    half, kv_head = j % 2, j // 4
    mine = (_lane((BLOCK, 128)) >= HEAD_DIM) if half else (_lane((BLOCK, 128)) < HEAD_DIM)
    swapped = kv_head != half
    kx, vx = (kks, vvs) if swapped else (kk, vv)
    qm = jnp.where(mine, qrot[:, 128 * (j // 2):128 * (j // 2 + 1)], 0.0).astype(BF16)
    sc = jnp.where(valid, _dot(qm, kx, NT) * (HEAD_DIM ** -0.5), NEG)
    sink = sinks_ref[j]
    mx = jnp.maximum(jnp.max(sc, axis=-1, keepdims=True), sink)
    e = jnp.exp(sc - mx)
    es = jnp.exp(sink - mx)
    den = jnp.sum(e, axis=-1, keepdims=True) + es
    return e / den, es / den, qm, kx, vx, mine, swapped


def _mixer_fwd(z, cos_t, sin_t, vg, wc, bfull, bd, scale, sinks, *, name):
    S = z.shape[0]
    nb = S // BLOCK

    def body(zc_ref, zpp_ref, zpk_ref, cc_ref, cp_ref, sc_ref, sp_ref, vg_ref, wc_ref, bf_ref, bd_ref, scale_ref, sinks_ref, out_ref):
        n = pl.program_id(0)
        keep = (n > 0).astype(F32)
        a, _ = _gating_fwd(zc_ref[:, GM0:POOL0].astype(F32), vg_ref[...], wc_ref, bf_ref[...])
        out_ref[:, 0:256] = a.astype(BF16)
        b, _ = _pool_fwd(zc_ref[:, POOL0:Q0].astype(F32), zpp_ref[...].astype(F32) * keep, n, bd_ref[...], scale_ref[...])
        out_ref[:, 256:512] = b.astype(BF16)
        cos_c, sin_c = cc_ref[...], sc_ref[...]
        qrot = _rope(zc_ref[:, Q0:KV0].astype(F32), cos_c, sin_c)
        kk = jnp.concatenate([_rope(zpk_ref[:, 0:128].astype(F32), cp_ref[...], sp_ref[...]), _rope(zc_ref[:, KV0:KV0 + 128].astype(F32), cos_c, sin_c)], axis=0)
        vv = jnp.concatenate([zpk_ref[:, 128:256].astype(F32), zc_ref[:, KV0 + 128:Z_END].astype(F32)], axis=0)
        kks, vvs = pltpu.roll(kk, HEAD_DIM, 1).astype(BF16), pltpu.roll(vv, HEAD_DIM, 1).astype(BF16)
        kk, vv = kk.astype(BF16), vv.astype(BF16)
        valid = _attn_mask(jnp.where(n > 0, 0, BLOCK))
        low = _low_half((BLOCK, 128))
        for s in range(ATT_Q_HEADS // 2):
            o = []
            for j in (2 * s, 2 * s + 1):
                p, _, _, _, vx, _, _ = _attn_head(j, qrot, kk, kks, vv, vvs, sinks_ref, valid)
                o.append(_dot(p, vx, NN))
            out_ref[:, 512 + 128 * s:512 + 128 * (s + 1)] = jnp.where(low, o[0], o[1]).astype(BF16)

    cur = lambda n: (n, 0)
    prev = lambda n: (jnp.maximum(n - 1, 0), 0)
    in_specs = [
        pl.BlockSpec((BLOCK, Z_END), cur),
        pl.BlockSpec((BLOCK, 256), lambda n: (jnp.maximum(n - 1, 0), POOL0 // 256)),
        pl.BlockSpec((BLOCK, 256), lambda n: (jnp.maximum(n - 1, 0), KV0 // 256)),
        pl.BlockSpec((BLOCK, 128), cur), pl.BlockSpec((BLOCK, 128), prev),
        pl.BlockSpec((BLOCK, 128), cur), pl.BlockSpec((BLOCK, 128), prev),
        _full((1, 256)), _full((4, BLOCK, BLOCK)), _full((BLOCK, 256)), _full((256, 256)), _full((1, 256)),
        pl.BlockSpec(memory_space=pltpu.SMEM),
    ]
    return pl.pallas_call(
        body, name=name, grid=(nb,), in_specs=in_specs, out_specs=pl.BlockSpec((BLOCK, 1024), cur),
        out_shape=jax.ShapeDtypeStruct((S, 1024), BF16),
        compiler_params=_params("parallel"))(z, z, z, cos_t, cos_t, sin_t, sin_t, vg, wc, bfull, bd, scale, sinks)


def _mixer_bwd(z, dcat, cos_t, sin_t, vg, wc, bfull, bd, scale, sinks, *, name):
    S = z.shape[0]
    nb = S // BLOCK

    def body(zc_ref, zpp_ref, zpk_ref, dc_ref, cc_ref, cp_ref, sc_ref, sp_ref, vg_ref, wc_ref, bf_ref, bd_ref, scale_ref, sinks_ref,
             dz_ref, dvg_ref, dwc_ref, dbf_ref, dbd_ref, dscale_ref, dsink_ref, carry_ref):
        n = pl.program_id(0)

        @pl.when(n == 0)
        def _():
            for ref in (dvg_ref, dwc_ref, dbf_ref, dbd_ref, dscale_ref, dsink_ref):
                ref[...] = jnp.zeros_like(ref)

        @pl.when(n < nb)
        def _():
            keep = (n > 0).astype(F32)
            low = _low_half((BLOCK, 128))
            zg = zc_ref[:, GM0:POOL0].astype(F32)
            vg_v = vg_ref[...]
            _, (th, u, r, vhat, vn, mixed) = _gating_fwd(zg, vg_v, wc_ref, bf_ref[...])
            da = dc_ref[:, 0:256].astype(F32)
            dmixed = da * u
            dbf_ref[...] += dmixed
            dvn = []
            for blk in range(2):
                dmb = dmixed[:, 128 * blk:128 * (blk + 1)]
                vb = vn[:, 128 * blk:128 * (blk + 1)]
                acc = None
                for half in range(2):
                    g = 2 * blk + half
                    dmg = jnp.where(low if half == 0 else ~low, dmb, 0.0)
                    dwc_ref[g] += _tril(_dot(dmg, vb, NT))
                    part = _dot(_tril(wc_ref[g]), dmg, TN)
                    acc = part if acc is None else acc + part
                dvn.append(acc)
            dvn = jnp.concatenate(dvn, axis=1)
            dvg_ref[...] += jnp.sum(dvn * vhat, axis=0, keepdims=True)
            dvhat = dvn * vg_v
            dv = r * (dvhat - vhat * (_group_sum(dvhat * vhat) * (1.0 / HEAD_DIM)))
            dge = jnp.concatenate([da * mixed, dv], axis=1)
            own_g = dge * _gelu_grad(zg, th)
            p = zc_ref[:, POOL0:Q0].astype(F32)
            bd_v, scale_v = bd_ref[...], scale_ref[...]
            _, (inv, pooled, mapped) = _pool_fwd(p, zpp_ref[...].astype(F32) * keep, n, bd_v, scale_v)
            db = dc_ref[:, 256:512].astype(F32)
            dscale_ref[...] += jnp.sum(db * mapped, axis=0, keepdims=True)
            dmapped = db * scale_v
            dbd_ref[...] += _dot(pooled, dmapped, TN)
            dpooled = _dot(dmapped, bd_v, NT)
            dps = dpooled * inv
            back = []
            for blk in range(2):
                dpb = dps[:, 128 * blk:128 * (blk + 1)]
                t0 = _exact_dot(_band(POOL_WINDOWS[2 * blk], True), dpb)
                t1 = _exact_dot(_band(POOL_WINDOWS[2 * blk + 1], True), dpb)
                back.append(jnp.where(jnp.concatenate([low, low], axis=0), t0, t1))
            back = jnp.concatenate(back, axis=1)
            halo_p = back[:BLOCK] * keep
            own_p = back[BLOCK:] - dpooled
            cos_c, sin_c, cos_p, sin_p = cc_ref[...], sc_ref[...], cp_ref[...], sp_ref[...]
            qrot = _rope(zc_ref[:, Q0:KV0].astype(F32), cos_c, sin_c)
            kk = jnp.concatenate([_rope(zpk_ref[:, 0:128].astype(F32), cos_p, sin_p), _rope(zc_ref[:, KV0:KV0 + 128].astype(F32), cos_c, sin_c)], axis=0)
            vv = jnp.concatenate([zpk_ref[:, 128:256].astype(F32), zc_ref[:, KV0 + 128:Z_END].astype(F32)], axis=0)
            kks, vvs = pltpu.roll(kk, HEAD_DIM, 1).astype(BF16), pltpu.roll(vv, HEAD_DIM, 1).astype(BF16)
            kk, vv = kk.astype(BF16), vv.astype(BF16)
            valid = _attn_mask(jnp.where(n > 0, 0, BLOCK))
            dk = [jnp.zeros((2 * BLOCK, 128), F32), jnp.zeros((2 * BLOCK, 128), F32)]
            dv_acc = [jnp.zeros((2 * BLOCK, 128), F32), jnp.zeros((2 * BLOCK, 128), F32)]
            dq = []
            for s in range(ATT_Q_HEADS // 2):
                dqs = []
                dcs = dc_ref[:, 512 + 128 * s:512 + 128 * (s + 1)]
                for j in (2 * s, 2 * s + 1):
                    pr, ps, qm, kx, vx, mine, swapped = _attn_head(j, qrot, kk, kks, vv, vvs, sinks_ref, valid)
                    dom = jnp.where(mine, dcs, jnp.zeros_like(dcs))
                    dp = _dot(dom, vx, NT)
                    dv_acc[swapped] = dv_acc[swapped] + _dot(pr, dom, TN)
                    drow = jnp.sum(pr * dp, axis=-1, keepdims=True)
                    ds = (pr * (dp - drow) * (HEAD_DIM ** -0.5)).astype(BF16)
                    dsink = jnp.sum(ps * drow, axis=0, keepdims=True)
                    dsink_ref[pl.ds(j, 1), :] = dsink_ref[pl.ds(j, 1), :] - jnp.broadcast_to(dsink, (1, 128))
                    dqs.append(_dot(ds, kx, NN))
                    dk[swapped] = dk[swapped] + _dot(ds, qm, TN)
                dq.append(jnp.where(low, dqs[0], dqs[1]))
            dqrot = jnp.concatenate(dq, axis=1)
            own_q = _rope_bwd(dqrot, cos_c, sin_c)
            dkk = dk[0] + pltpu.roll(dk[1], HEAD_DIM, 1)
            dvv = dv_acc[0] + pltpu.roll(dv_acc[1], HEAD_DIM, 1)
            own_k = _rope_bwd(dkk[BLOCK:], cos_c, sin_c)
            halo_k = _rope_bwd(dkk[:BLOCK], cos_p, sin_p) * keep
            own_v, halo_v = dvv[BLOCK:], dvv[:BLOCK] * keep

            @pl.when(n > 0)
            def _():
                dz_ref[:, GM0:POOL0] = carry_ref[:, GM0:POOL0].astype(BF16)
                dz_ref[:, POOL0:Q0] = (carry_ref[:, POOL0:Q0] + halo_p).astype(BF16)
                dz_ref[:, Q0:KV0] = carry_ref[:, Q0:KV0].astype(BF16)
                dz_ref[:, KV0:KV0 + 128] = (carry_ref[:, KV0:KV0 + 128] + halo_k).astype(BF16)
                dz_ref[:, KV0 + 128:Z_END] = (carry_ref[:, KV0 + 128:Z_END] + halo_v).astype(BF16)

            carry_ref[:, GM0:POOL0] = own_g
            carry_ref[:, POOL0:Q0] = own_p
            carry_ref[:, Q0:KV0] = own_q
            carry_ref[:, KV0:KV0 + 128] = own_k
            carry_ref[:, KV0 + 128:Z_END] = own_v

        @pl.when(n == nb)
        def _():
            dz_ref[...] = carry_ref[...].astype(BF16)

    cur = lambda n: (jnp.minimum(n, nb - 1), 0)
    prev = lambda n: (jnp.maximum(jnp.minimum(n, nb - 1) - 1, 0), 0)
    in_specs = [
        pl.BlockSpec((BLOCK, Z_END), cur),
        pl.BlockSpec((BLOCK, 256), lambda n: (jnp.maximum(jnp.minimum(n, nb - 1) - 1, 0), POOL0 // 256)),
        pl.BlockSpec((BLOCK, 256), lambda n: (jnp.maximum(jnp.minimum(n, nb - 1) - 1, 0), KV0 // 256)),
        pl.BlockSpec((BLOCK, 1024), cur),
        pl.BlockSpec((BLOCK, 128), cur), pl.BlockSpec((BLOCK, 128), prev),
        pl.BlockSpec((BLOCK, 128), cur), pl.BlockSpec((BLOCK, 128), prev),
        _full((1, 256)), _full((4, BLOCK, BLOCK)), _full((BLOCK, 256)), _full((256, 256)), _full((1, 256)),
        pl.BlockSpec(memory_space=pltpu.SMEM),
    ]
    out_specs = [pl.BlockSpec((BLOCK, Z_END), lambda n: (jnp.maximum(n - 1, 0), 0)),
                 _full((1, 256)), _full((4, BLOCK, BLOCK)), _full((BLOCK, 256)), _full((256, 256)), _full((1, 256)), _full((8, 128))]
    out_shape = [jax.ShapeDtypeStruct((S, Z_END), BF16), jax.ShapeDtypeStruct((1, 256), F32),
                 jax.ShapeDtypeStruct((4, BLOCK, BLOCK), F32), jax.ShapeDtypeStruct((BLOCK, 256), F32),
                 jax.ShapeDtypeStruct((256, 256), F32), jax.ShapeDtypeStruct((1, 256), F32), jax.ShapeDtypeStruct((8, 128), F32)]
    return pl.pallas_call(
        body, name=name, grid=(nb + 1,), in_specs=in_specs, out_specs=out_specs, out_shape=out_shape,
        scratch_shapes=[pltpu.VMEM((BLOCK, Z_END), F32)],
        compiler_params=_params("arbitrary"))(z, z, z, dcat, cos_t, cos_t, sin_t, sin_t, vg, wc, bfull, bd, scale, sinks)


HBM = pl.BlockSpec(memory_space=pl.ANY)


def _place():
    return lax.axis_index("x"), lax.axis_index("y"), lax.axis_index("c")


def _chip(k, x, y):
    return (1 - x if k & 1 else x, 1 - y if k & 2 else y)


def _block_rows(ref, rows, dev):
    start = pl.multiple_of((4 * dev[0] + 2 * dev[1] + dev[2]) * rows, 8)
    return ref.at[pl.ds(start, rows), :]


def _group_size(arrays, group):
    return sum(arrays[w].shape[0] for w in group)


def _all_gather(shards, groups, carrier, *, name):
    n = len(shards)
    assert all(_group_size(shards, g) == carrier.shape[0] for g in groups) and carrier.dtype == shards[0].dtype

    def body(*refs):
        ins, car, outs = refs[:n], refs[n], refs[n + 1:2 * n + 1]
        send_sems, recv_sems, local_sems = refs[2 * n + 1:]
        x, y, c = _place()
        me, sibling = (x, y, c), (x, y, 1 - c)
        chips = [_chip(k, x, y) for k in (1, 2, 3)]

        def copy(gi, w, k, block, to, src=None):
            dst = _block_rows(outs[w], ins[w].shape[0], block)
            return pltpu.make_async_remote_copy(src_ref=dst if src is None else src, dst_ref=dst, send_sem=send_sems.at[gi, k],
                                                recv_sem=recv_sems.at[gi, k], device_id=to, device_id_type=MESH)

        def drained(gi, k):
            return pltpu.make_async_remote_copy(src_ref=car, dst_ref=car, send_sem=send_sems.at[gi, k],
                                                recv_sem=recv_sems.at[gi, k], device_id=me, device_id_type=MESH)

        for gi, group in enumerate(groups):
            for w in group:
                pltpu.make_async_copy(ins[w], _block_rows(outs[w], ins[w].shape[0], me), local_sems.at[gi]).start()
                copy(gi, w, 0, me, sibling, src=ins[w]).start()
                for j, chip in enumerate(chips):
                    copy(gi, w, 1 + j, me, (*chip, c), src=ins[w]).start()
        for gi, group in enumerate(groups):
            for j, chip in enumerate(chips):
                drained(gi, 1 + j).wait_recv()
                for w in group:
                    copy(gi, w, 4 + j, (*chip, c), sibling).start()
        for gi in range(len(groups)):
            drained(gi, 0).wait_recv()
            for j in range(3):
                drained(gi, 4 + j).wait_recv()
        for gi in range(len(groups)):
            for k in range(7):
                drained(gi, k).wait_send()
            pltpu.make_async_copy(car, car, local_sems.at[gi]).wait()

    g = len(groups)
    return pl.pallas_call(
        body, name=name, in_specs=[HBM] * (n + 1), out_specs=[HBM] * n,
        out_shape=[jax.ShapeDtypeStruct((N_DEV * s.shape[0], s.shape[1]), s.dtype) for s in shards],
        scratch_shapes=[pltpu.SemaphoreType.DMA((g, 7)), pltpu.SemaphoreType.DMA((g, 7)), pltpu.SemaphoreType.DMA((g,))],
    )(*shards, carrier)


_HBM_SPEC = pl.BlockSpec(memory_space=pltpu.HBM)
_SEM_SPEC = pl.BlockSpec(memory_space=pltpu.SEMAPHORE)
_SPLIT_PARAMS = dict(compiler_params=pltpu.CompilerParams(has_side_effects=pltpu.SideEffectType.DATAFLOW_SIDE_EFFECTING))


def _descriptors(copies, refs, send_sems, recv_sems):
    return [pltpu.make_async_remote_copy(src_ref=s, dst_ref=d, send_sem=send_sems.at[i], recv_sem=recv_sems.at[i], device_id=to, device_id_type=MESH)
            for i, (s, d, to) in enumerate(copies(refs))]


def _start(copies, arrays, fresh, n_copies, *, name):
    operands = [pltpu.with_memory_space_constraint(v, pltpu.HBM) for v in (*arrays, *[lax.empty(f.shape, f.dtype) for f in fresh])]
    n = len(operands)

    def body(*refs):
        for dma in _descriptors(copies, refs[:n], refs[n], refs[n + 1]):
            dma.start()
        refs[-1][...] = jnp.zeros_like(refs[-1])

    res = pl.pallas_call(
        body, name=name, in_specs=[_HBM_SPEC] * n,
        out_shape=(pltpu.SemaphoreType.DMA((n_copies,)), pltpu.SemaphoreType.DMA((n_copies,)),
                   *[pltpu.HBM(v.shape, v.dtype) for v in operands], jax.ShapeDtypeStruct((8, 128), F32)),
        out_specs=(_SEM_SPEC, _SEM_SPEC, *[_HBM_SPEC] * n, pl.BlockSpec(memory_space=pltpu.VMEM)),
        input_output_aliases={i: 2 + i for i in range(n)}, **_SPLIT_PARAMS)(*operands)
    return res[0], res[1], list(res[2:2 + n]), res[-1]


def _finish(copies, send_sems, recv_sems, arrays, after, *, name):
    n = len(arrays)

    def body(*refs):
        for dma in _descriptors(copies, refs[:n], refs[n], refs[n + 1]):
            dma.wait_send()
            dma.wait_recv()

    return list(pl.pallas_call(
        body, name=name, in_specs=[_HBM_SPEC] * n + [_SEM_SPEC, _SEM_SPEC] + [HBM] * len(after),
        out_shape=tuple(pltpu.HBM(v.shape, v.dtype) for v in arrays), out_specs=tuple([_HBM_SPEC] * n),
        input_output_aliases={i: i for i in range(n)}, **_SPLIT_PARAMS)(*arrays, send_sems, recv_sems, *after))


def _tie(value, tokens):
    if not tokens:
        return value

    def body(*refs):
        pass

    return pl.pallas_call(body, name="tie", in_specs=[HBM] * (1 + len(tokens)), out_specs=HBM,
                          out_shape=jax.ShapeDtypeStruct(value.shape, value.dtype), input_output_aliases={0: 0})(value, *tokens)


def _gather_copies(stage):
    def copies(refs):
        x, y, c = _place()
        me, sibling = (x, y, c), (x, y, 1 - c)
        out = []
        for ref in refs:
            r = ref.shape[0] // N_DEV
            if stage == "a":
                hops = [(me, sibling)] + [(me, (*_chip(k, x, y), c)) for k in (1, 2, 3)]
            else:
                hops = [((*_chip(k, x, y), c), sibling) for k in (1, 2, 3)]
            for block, to in hops:
                rows = _block_rows(ref, r, block)
                out.append((rows, rows, to))
        return out
    return copies


def _scatter_copies(stage, n):
    def copies(refs):
        x, y, c = _place()
        out = []
        for w in range(n):
            src, dst = refs[w], refs[n + w]
            if stage == "a":
                r = src.shape[0] // N_DEV
                out += [(_block_rows(src, r, (*_chip(k, x, y), 1 - c)), dst.at[k], (x, y, 1 - c)) for k in range(4)]
            else:
                out += [(src.at[k - 1], dst.at[k - 1], (*_chip(k, x, y), c)) for k in (1, 2, 3)]
        return out
    return copies


def _place_own(shards, layer, blocks, *, name):
    _, r, cdim = shards.shape

    def body(blocks_ref, s_ref, o_ref):
        o_ref[...] = s_ref[...].astype(BF16)

    return pl.pallas_call(
        body, name=name,
        grid_spec=pltpu.PrefetchScalarGridSpec(
            num_scalar_prefetch=1, grid=(1,), in_specs=[pl.BlockSpec((None, r, cdim), lambda i, blocks: (layer, 0, 0))],
            out_specs=pl.BlockSpec((r, cdim), lambda i, blocks: (blocks[0], 0))),
        out_shape=jax.ShapeDtypeStruct((N_DEV * r, cdim), BF16), compiler_params=_params("arbitrary"))(blocks, shards)


def _chip_sum(grad, got, blocks, *, name):
    _, r, cdim = got.shape
    tr = r if r <= 256 else r // (2 if r % 32 == 0 and r // 2 <= 256 else 4)
    steps = r // tr

    def body(blocks_ref, a0_ref, a1_ref, a2_ref, a3_ref, b_ref, mine_ref, away_ref):
        mine_ref[...] = a0_ref[...] + b_ref[0]
        for k, a_ref in ((1, a1_ref), (2, a2_ref), (3, a3_ref)):
            away_ref[k - 1] = (a_ref[...] + b_ref[k]).astype(BF16)

    own = [pl.BlockSpec((tr, cdim), lambda i, blocks, k=k: (blocks[k] * steps + i, 0)) for k in range(4)]
    return pl.pallas_call(
        body, name=name,
        grid_spec=pltpu.PrefetchScalarGridSpec(
            num_scalar_prefetch=1, grid=(steps,),
            in_specs=own + [pl.BlockSpec((4, tr, cdim), lambda i, blocks: (0, i, 0))],
            out_specs=[pl.BlockSpec((tr, cdim), lambda i, blocks: (i, 0)), pl.BlockSpec((3, tr, cdim), lambda i, blocks: (0, i, 0))]),
        out_shape=[jax.ShapeDtypeStruct((r, cdim), F32), jax.ShapeDtypeStruct((3, r, cdim), BF16)],
        compiler_params=_params("arbitrary"))(blocks, grad, grad, grad, grad, got)


def _final_sum(mine, got, *, name):
    r, cdim = mine.shape
    tr = r if r <= 256 else r // (2 if r % 32 == 0 and r // 2 <= 256 else 4)

    def body(m_ref, g_ref, o_ref):
        o_ref[...] = ((m_ref[...] + g_ref[0].astype(F32)) + g_ref[1].astype(F32)) + g_ref[2].astype(F32)

    return pl.pallas_call(
        body, name=name, grid=(r // tr,),
        in_specs=[pl.BlockSpec((tr, cdim), lambda i: (i, 0)), pl.BlockSpec((3, tr, cdim), lambda i: (0, i, 0))],
        out_specs=pl.BlockSpec((tr, cdim), lambda i: (i, 0)), out_shape=jax.ShapeDtypeStruct((r, cdim), F32),
        compiler_params=_params("parallel"))(mine, got)


def _adamw_math(w, g, m, v):
    m = ADAM_B1 * m + (1.0 - ADAM_B1) * g
    v = ADAM_B2 * v + (1.0 - ADAM_B2) * (g * g)
    m_hat = m / (1.0 - ADAM_B1 ** ADAM_STEP)
    v_hat = v / (1.0 - ADAM_B2 ** ADAM_STEP)
    return -ADAM_LR * (m_hat / (jnp.sqrt(v_hat) + ADAM_EPS) + ADAM_WD * w), m, v


def _adamw(w, g, m, v, *, name):
    L, r, cdim = w.shape
    tr = r if r <= 512 else r // 2
    assert r % tr == 0 and tr % 8 == 0

    def body(w_ref, g_ref, m_ref, v_ref, d_ref, nm_ref, nv_ref):
        d_ref[...], nm_ref[...], nv_ref[...] = _adamw_math(w_ref[...], g_ref[...], m_ref[...], v_ref[...])

    blk = pl.BlockSpec((None, tr, cdim), lambda l, i: (l, i, 0))
    return pl.pallas_call(body, name=name, grid=(L, r // tr), in_specs=[blk] * 4, out_specs=[blk] * 3,
                          out_shape=[jax.ShapeDtypeStruct(w.shape, F32)] * 3, compiler_params=_params("parallel", "parallel"))(w, g, m, v)


def _small_update(parts, w, m, v, *, name):
    _, p, lanes = parts.shape
    tp = p // 2 if p % 16 == 0 else p

    def body(p_ref, w_ref, m_ref, v_ref, g_ref, d_ref, nm_ref, nv_ref):
        g = p_ref[0]
        for i in range(1, N_DEV):
            g = g + p_ref[i]
        g_ref[...] = g
        d_ref[...], nm_ref[...], nv_ref[...] = _adamw_math(w_ref[...], g, m_ref[...], v_ref[...])

    blk = pl.BlockSpec((tp, lanes), lambda i: (i, 0))
    return pl.pallas_call(body, name=name, grid=(p // tp,), in_specs=[pl.BlockSpec((N_DEV, tp, lanes), lambda i: (0, i, 0)), blk, blk, blk],
                          out_specs=[blk] * 4, out_shape=[jax.ShapeDtypeStruct((p, lanes), F32)] * 4,
                          compiler_params=_params("parallel"))(parts, w, m, v)


BIG = ("w_in", "w_o", "w_xq", "w_xkv", "w_xo", "w_gate_up", "w_down")
TRANSPOSED = ("w_in", "w_xkv", "w_gate_up")
SMALL = ("mem_norm_g", "mix_pre_g", "mix_post_g", "gm_v_g", "gm_w_s", "gm_b_s", "pool_w", "pool_scale", "attn_sinks",
         "x_pre_g", "x_post_g", "ffn_pre_g", "ffn_post_g")
WEIGHTS = ("mem_norm_g", "mix_pre_g", "mix_post_g", "w_in", "gm_v_g", "gm_w_s", "gm_b_s", "pool_w", "pool_scale", "attn_sinks", "w_o",
           "x_pre_g", "x_post_g", "w_xq", "w_xkv", "w_xo", "ffn_pre_g", "ffn_post_g", "w_gate_up", "w_down")
PACK_QUANTUM = 8 * 128


def _pack(arrays):
    flat = []
    for a in arrays:
        a = a.reshape(-1).astype(F32)
        flat.append(jnp.pad(a, (0, -a.size % PACK_QUANTUM)))
    return jnp.concatenate(flat).reshape(-1, 128)


def _unpack(pack, shapes):
    out, row = [], 0
    for s in shapes:
        size = math.prod(s)
        rows = (size + (-size % PACK_QUANTUM)) // 128
        out.append(pack[row:row + rows].reshape(-1)[:size].reshape(s))
        row += rows
    return out


def _row(v):
    return v.reshape(1, -1)


def _local_step(x, mem, positions, target, small, weight, fwd_hook=None, bwd_hook=None):
    fwd_hook = fwd_hook or (lambda l, point, v: v)
    bwd_hook = bwd_hook or (lambda l, point, v, gb: v)
    depth = small["mix_pre_g"].shape[0]
    half = HEAD_DIM // 2
    inv = ROPE_THETA ** (-jnp.arange(half, dtype=F32) / half)
    cos_t, sin_t = _rope_tables(positions.reshape(-1, 1), jnp.tile(inv, 128 // half).reshape(1, 128), name="rope_tables")
    mem_g = _row(small["mem_norm_g"])
    saved = []
    for l in range(depth):
        w = functools.partial(weight, l)
        bfull = jnp.repeat(small["gm_b_s"][l].T, HEAD_DIM, axis=1)
        bd = jax.scipy.linalg.block_diag(*[small["pool_w"][l, g] for g in range(len(POOL_WINDOWS))])
        mix_par = (_row(small["gm_v_g"][l]), small["gm_w_s"][l], bfull, bd, _row(small["pool_scale"][l]), small["attn_sinks"][l])
        x = fwd_hook(l, 0, x)
        z, h1 = _mm_fwd(x, w("w_in"), nt=True, pre="norm", g_pre=_row(small["mix_pre_g"][l]), name="in_proj")
        cat = fwd_hook(l, 1, _mixer_fwd(z, cos_t, sin_t, *mix_par, name="mixer_fwd"))
        mix, x1 = _mm_fwd(cat, w("w_o"), nt=False, post="norm_res", g_post=_row(small["mix_post_g"][l]), xres=x, name="mix_out")
        qx, h2 = _mm_fwd(x1, w("w_xq"), nt=False, pre="norm", g_pre=_row(small["x_pre_g"][l]), name="xq_proj")
        kv, memn = _mm_fwd(mem, w("w_xkv"), nt=True, pre="norm", g_pre=mem_g, name="xkv_proj")
        ox = _xattn_fwd(qx, kv, name="xattn_fwd")
        xo, x2 = _mm_fwd(ox, w("w_xo"), nt=False, post="norm_res", g_post=_row(small["x_post_g"][l]), xres=x1, name="xattn_out")
        x2 = fwd_hook(l, 2, x2)
        gu, h3 = _mm_fwd(x2, w("w_gate_up"), nt=True, pre="norm", g_pre=_row(small["ffn_pre_g"][l]), name="ffn_in")
        f, x3 = _mm_fwd(gu, w("w_down"), nt=False, pre="swiglu", post="norm_res", g_post=_row(small["ffn_post_g"][l]), xres=x2,
                        tile=256, name="ffn_out")
        saved.append((mix_par, x, z, h1, cat, mix, x1, qx, h2, kv, memn, ox, xo, x2, gu, h3, f))
        x = fwd_hook(l, 3, x3)

    loss, dx = _loss_grad(x, target, name="loss")

    gs = {n: [None] * depth for n in SMALL if n != "mem_norm_g"}
    gb = {n: [None] * depth for n in BIG}
    dmemn = [None] * depth
    for l in reversed(range(depth)):
        w = functools.partial(weight, l)
        mix_par, x0, z, h1, cat, mix, x1, qx, h2, kv, memn, ox, xo, x2, gu, h3, f = saved[l]
        dgu, df, gs["ffn_post_g"][l] = _ffn_out_bwd(f, dx, _row(small["ffn_post_g"][l]), w("w_down"), gu, name="ffn_out_bwd")
        gb["w_down"][l] = _wgrad(gu, df, swiglu=True, name="w_down_grad")
        gb["w_gate_up"][l] = _wgrad(dgu, h3, name="w_gate_up_grad")
        dx2, gs["ffn_pre_g"][l] = _mm_bwd_post(dgu, w("w_gate_up"), x2, _row(small["ffn_pre_g"][l]), dx, nt=False, tile=256, name="ffn_in_bwd")
        dx2 = bwd_hook(l, 1, dx2, gb)
        do, dxo, gs["x_post_g"][l] = _mm_bwd_pre(xo, dx2, _row(small["x_post_g"][l]), w("w_xo"), name="xattn_out_bwd")
        gb["w_xo"][l] = _wgrad(ox, dxo, name="w_xo_grad")
        dqx, dkv = _xattn_bwd(qx, kv, do, name="xattn_bwd")
        gb["w_xq"][l] = _wgrad(h2, dqx, name="w_xq_grad")
        dx1, gs["x_pre_g"][l] = _mm_bwd_post(dqx, w("w_xq"), x1, _row(small["x_pre_g"][l]), dx2, nt=True, name="xq_proj_bwd")
        dx1 = bwd_hook(l, 2, dx1, gb)
        gb["w_xkv"][l] = _wgrad(dkv, memn, name="w_xkv_grad")
        dmemn[l] = _mm_fwd(dkv, w("w_xkv"), nt=False, out_dtype=F32, name="xkv_proj_bwd")[0]
        dcat, dmix, gs["mix_post_g"][l] = _mm_bwd_pre(mix, dx1, _row(small["mix_post_g"][l]), w("w_o"), name="mix_out_bwd")
        gb["w_o"][l] = _wgrad(cat, dmix, name="w_o_grad")
        dz, dvg, dwc, dbf, dbd, dscale, dsink = _mixer_bwd(z, dcat, cos_t, sin_t, *mix_par, name="mixer_bwd")
        gs["gm_v_g"][l], gs["gm_w_s"][l], gs["pool_scale"][l], gs["attn_sinks"][l] = dvg, dwc, dscale, dsink[:, 0]
        gs["gm_b_s"][l] = dbf.reshape(BLOCK, -1, HEAD_DIM).sum(-1).T
        gs["pool_w"][l] = jnp.stack([dbd[HEAD_DIM * g:HEAD_DIM * (g + 1), HEAD_DIM * g:HEAD_DIM * (g + 1)] for g in range(len(POOL_WINDOWS))])
        gb["w_in"][l] = _wgrad(dz, h1, name="w_in_grad")
        dx, gs["mix_pre_g"][l] = _mm_bwd_post(dz, w("w_in"), x0, _row(small["mix_pre_g"][l]), dx1, nt=False, name="in_proj_bwd")
        dx = bwd_hook(l, 3, dx, gb)

    small_grads = {n: jnp.stack(v).reshape(small[n].shape) for n, v in gs.items()}
    small_grads["mem_norm_g"] = _norm_dgain(mem, jnp.stack(dmemn), name="mem_norm_grad").reshape(-1)
    return loss, dx, small_grads, gb


EARLY = ("w_in", "w_o", "w_xq", "w_xkv", "w_xo")
LATE = ("w_gate_up", "w_down")


def _step(a):
    depth = a["mix_pre_g"].shape[0]
    x, mem, target = a["x"][0], a["mem"][0], a["loss_target"][0]
    px, py, pc = _place()
    blocks = jnp.stack([4 * cx + 2 * cy + pc for cx, cy in (_chip(k, px, py) for k in range(4))]).astype(jnp.int32)
    flight = {}

    def laid_out(v, n):
        return v.transpose(0, 2, 1) if n in TRANSPOSED else v

    whole = {(l, n): _place_own(laid_out(a[n], n), l, blocks, name="place_" + n) for l in range(depth) for n in BIG}
    gather = {"first_in": [(0, "w_in")], "first_mid": [(0, n) for n in EARLY[1:]], "first_late": [(0, n) for n in LATE],
              "second_early": [(1, n) for n in EARLY], "second_late": [(1, n) for n in LATE]}
    gather.update({f"layer{l}": [(l, n) for n in BIG] for l in range(2, depth)})
    forward_plan = {(0, 0): [("a", "first_in"), ("a", "first_mid"), ("a", "first_late"), ("a", "second_early"), ("a", "second_late"),
                             ("mid", "first_in"), ("end", "first_in")],
                    (0, 1): [("mid", "first_mid"), ("end", "first_mid")], (0, 2): [("mid", "first_late"), ("end", "first_late")],
                    (0, 3): [("mid", "second_early")], (1, 0): [("end", "second_early"), ("a", "layer2")],
                    (1, 1): [("mid", "second_late")], (1, 2): [("end", "second_late"), ("mid", "layer2")], (1, 3): [("end", "layer2")]}
    for l in range(2, depth):
        forward_plan.update({(l, 0): [("a", f"layer{l + 1}")], (l, 2): [("mid", f"layer{l + 1}")], (l, 3): [("end", f"layer{l + 1}")]})
    if depth == 1:
        gather = {tag: keys for tag, keys in gather.items() if tag.startswith("first")}

    issued = []

    def gather_step(kind, tag, v):
        keys = gather[tag]
        if kind == "a":
            arrays = [whole[k] for k in keys]
            arrays[0] = _tie(arrays[0], issued[-1:])
            send, recv, arrays, token = _start(_gather_copies("a"), arrays, [], 4 * len(keys), name=f"gather_a_{tag}")
            flight[tag] = (send, recv, arrays)
            issued.append(token)
            return [token]
        send, recv, arrays = flight.pop(tag)
        if kind == "mid":
            arrays = _finish(_gather_copies("a"), send, recv, arrays, [v], name=f"gather_a_done_{tag}")
            send, recv, arrays, token = _start(_gather_copies("b"), arrays, [], 3 * len(keys), name=f"gather_b_{tag}")
            flight[tag] = (send, recv, arrays)
            return [token]
        for key, arr in zip(keys, _finish(_gather_copies("b"), send, recv, arrays, [v], name=f"gather_b_done_{tag}")):
            whole[key] = arr
        return []

    def fwd_hook(l, point, v):
        tokens = []
        for kind, tag in forward_plan.get((l, point), []):
            if tag in gather:
                tokens += gather_step(kind, tag, v)
        return _tie(v, tokens)

    mine = {}
    scatter = {f"{part}{l}": [(l, n) for n in names] for l in range(1, depth) for part, names in (("early", EARLY), ("late", LATE))}
    scatter.update({"late0": [(0, n) for n in LATE], "attn0": [(0, "w_xq"), (0, "w_xo")], "rest0": [(0, "w_in"), (0, "w_o"), (0, "w_xkv")]})
    backward_plan = {}
    for l in range(depth):
        backward_plan[(l, 1)] = [("a", f"late{l}")] + ([("mid", f"early{l + 1}")] if l + 1 < depth else [])
        backward_plan[(l, 2)] = [("mid", f"late{l}")] + ([("end", f"early{l + 1}")] if l + 1 < depth else [])
        backward_plan[(l, 3)] = [("end", f"late{l}"), ("a", f"early{l}")]
    backward_plan[(0, 2)] = [("a", "attn0")] + backward_plan[(0, 2)]
    backward_plan[(0, 3)] = [("end", "late0"), ("mid", "attn0"), ("a", "rest0")]

    def scatter_step(kind, tag, v, gb):
        keys = scatter[tag]
        n = len(keys)
        if kind == "a":
            grads = [gb[name][l] for l, name in keys]
            zones = [jax.ShapeDtypeStruct((4, g.shape[0] // N_DEV, g.shape[1]), F32) for g in grads]
            send, recv, arrays, token = _start(_scatter_copies("a", n), grads, zones, 4 * n, name=f"scatter_a_{tag}")
            flight[tag] = (send, recv, arrays, None)
            return [token]
        send, recv, arrays, kept = flight.pop(tag)
        if kind == "mid":
            arrays = _finish(_scatter_copies("a", n), send, recv, arrays, [v], name=f"scatter_a_done_{tag}")
            sums = [_chip_sum(arrays[i], arrays[n + i], blocks, name="chip_sum") for i in range(n)]
            zones = [jax.ShapeDtypeStruct(s[1].shape, BF16) for s in sums]
            send, recv, arrays, token = _start(_scatter_copies("b", n), [s[1] for s in sums], zones, 3 * n, name=f"scatter_b_{tag}")
            flight[tag] = (send, recv, arrays, [s[0] for s in sums])
            return [token]
        arrays = _finish(_scatter_copies("b", n), send, recv, arrays, [v], name=f"scatter_b_done_{tag}")
        for i, key in enumerate(keys):
            mine[key] = _final_sum(kept[i], arrays[n + i], name="final_sum")
        return []

    def bwd_hook(l, point, v, gb):
        tokens = []
        for kind, tag in backward_plan.get((l, point), []):
            tokens += scatter_step(kind, tag, v, gb)
        return _tie(v, tokens)

    loss, dx, small_grads, _ = _local_step(x, mem, a["positions"][0], target, {n: a[n] for n in SMALL}, lambda l, n: whole[(l, n)],
                                           fwd_hook, bwd_hook)

    out = {}

    def update(n, tokens):
        g = _tie(jnp.stack([mine[(l, n)] for l in range(depth)]), tokens)
        moved = _adamw(laid_out(a[n], n), g, laid_out(a["m_" + n], n), laid_out(a["v_" + n], n), name="adamw_" + n)
        for p, v in zip(("grad_", "delta_", "new_m_", "new_v_"), (g, *moved)):
            out[p + n] = laid_out(v, n)
        return moved[0]

    part = _tie(_pack([small_grads[n] for n in SMALL]), [dx])
    parts = _all_gather([part], [[0]], part, name="gather_small_grads")[0].reshape(N_DEV, *part.shape)
    packs = _small_update(parts, *[_pack([a[p + n] for n in SMALL]) for p in ("", "m_", "v_")], name="small_update")
    for p, pack in zip(("grad_", "delta_", "new_m_", "new_v_"), packs):
        for n, v in zip(SMALL, _unpack(pack, [a[n].shape for n in SMALL])):
            out[p + n] = v
    tokens = scatter_step("mid", "rest0", packs[0], None)
    behind = [update(n, tokens) for n in LATE]
    scatter_step("end", "attn0", behind[-1], None)
    scatter_step("end", "rest0", behind[-1], None)
    for n in EARLY:
        update(n, [])

    total = lax.psum(loss[0, 0], ("x", "y", "c"))
    return (total, dx[None], *[out[p + n] for p in ("grad_", "delta_", "new_m_", "new_v_") for n in WEIGHTS])


def kernel(x, mem, positions, mem_norm_g, mix_pre_g, mix_post_g, w_in, gm_v_g, gm_w_s, gm_b_s, pool_w, pool_scale, attn_sinks,
           w_o, x_pre_g, x_post_g, w_xq, w_xkv, w_xo, ffn_pre_g, ffn_post_g, w_gate_up, w_down, loss_target, m_mem_norm_g,
           m_mix_pre_g, m_mix_post_g, m_w_in, m_gm_v_g, m_gm_w_s, m_gm_b_s, m_pool_w, m_pool_scale, m_attn_sinks, m_w_o,
           m_x_pre_g, m_x_post_g, m_w_xq, m_w_xkv, m_w_xo, m_ffn_pre_g, m_ffn_post_g, m_w_gate_up, m_w_down, v_mem_norm_g,
           v_mix_pre_g, v_mix_post_g, v_w_in, v_gm_v_g, v_gm_w_s, v_gm_b_s, v_pool_w, v_pool_scale, v_attn_sinks, v_w_o,
           v_x_pre_g, v_x_post_g, v_w_xq, v_w_xkv, v_w_xo, v_ffn_pre_g, v_ffn_post_g, v_w_gate_up, v_w_down):
    return _step(dict(locals()))
```

```python
import functools
import math

import jax
import jax.numpy as jnp
from jax import lax
from jax.experimental import pallas as pl
from jax.experimental.pallas import tpu as pltpu

F32, BF16 = jnp.float32, jnp.bfloat16
EPS = 1e-6
HEAD_DIM = 64
BLOCK = 128
POOL_WINDOWS = (2, 4, 8, 16)
ATT_Q_HEADS = 8
X_HEADS = 4
ROPE_THETA = 10000.0
ADAM_LR, ADAM_B1, ADAM_B2, ADAM_EPS, ADAM_WD, ADAM_STEP = 0.001, 0.9, 0.999, 1e-08, 0.01, 10
N_DEV = 8
TOKEN_TILE = 512
VMEM_LIMIT_BYTES = 50 * 2**20
NEG = -1e30
MESH = pl.DeviceIdType.MESH

NN = ((1,), (0,))
NT = ((1,), (1,))
TN = ((0,), (0,))


def _dot(a, b, dims):
    return lax.dot_general(a.astype(BF16), b.astype(BF16), (dims, ((), ())), preferred_element_type=F32)


def _params(*sem):
    return pltpu.CompilerParams(dimension_semantics=sem, vmem_limit_bytes=VMEM_LIMIT_BYTES)


def _full(shape):
    return pl.BlockSpec(shape, lambda *_: (0,) * len(shape))


def _lane(shape):
    return lax.broadcasted_iota(jnp.int32, shape, len(shape) - 1)


def _rms_fwd(x, g):
    r = lax.rsqrt(jnp.mean(x * x, axis=-1, keepdims=True) + EPS)
    return x * r * g


def _rms_bwd(x, g, dy):
    r = lax.rsqrt(jnp.mean(x * x, axis=-1, keepdims=True) + EPS)
    xh = x * r
    dg = jnp.sum(dy * xh, axis=0, keepdims=True)
    dxh = dy * g
    dx = r * (dxh - xh * jnp.mean(dxh * xh, axis=-1, keepdims=True))
    return dx, dg


def _silu_parts(gate):
    sg = 1.0 / (1.0 + jnp.exp(-gate))
    return gate * sg, sg


def _swiglu_cols(gu_ref, out_ref, width):
    step = 128 * max(1, (width // 128) // 4)
    for c0 in range(0, width, step):
        c1 = min(width, c0 + step)
        gate = gu_ref[:, c0:c1].astype(F32)
        up = gu_ref[:, width + c0:width + c1].astype(F32)
        out_ref[:, c0:c1] = (_silu_parts(gate)[0] * up).astype(out_ref.dtype)


def _col_tile(n, cap=1408):
    best = n
    for t in range(128, cap + 1, 128):
        if n % t == 0:
            best = t
    return best if n > cap else n


def _mm_fwd(a, w, *, nt, name, pre=None, g_pre=None, post=None, g_post=None, xres=None, out_dtype=BF16, tile=TOKEN_TILE):
    S = a.shape[0]
    n_out, k = (w.shape[0], w.shape[1]) if nt else (w.shape[1], w.shape[0])
    T = min(tile, S)
    tn = n_out if post else _col_tile(n_out)
    grid = (S // T, n_out // tn)

    def body(*refs):
        it = iter(refs)
        a_ref, w_ref = next(it), next(it)
        gpre_ref = next(it) if pre == "norm" else None
        gpost_ref, x_ref = (next(it), next(it)) if post else (None, None)
        outs = [next(it) for _ in range((2 if post else 1) + (1 if pre == "norm" else 0))]
        as_ref = next(it) if pre else None
        if pre:
            @pl.when(pl.program_id(1) == 0)
            def _():
                if pre == "norm":
                    h = _rms_fwd(a_ref[...], gpre_ref[...]).astype(BF16)
                    outs[-1][...] = h
                    as_ref[...] = h
                else:
                    _swiglu_cols(a_ref, as_ref, k)
            av = as_ref[...]
        else:
            av = a_ref[...]
        acc = _dot(av, w_ref[...], NT if nt else NN)
        if post:
            outs[0][...] = acc.astype(BF16)
            outs[1][...] = x_ref[...] + _rms_fwd(acc, gpost_ref[...])
        else:
            outs[0][...] = acc.astype(out_dtype)

    in_specs = [pl.BlockSpec((T, a.shape[1]), lambda i, j: (i, 0)),
                pl.BlockSpec((tn, k), lambda i, j: (j, 0)) if nt else pl.BlockSpec((k, tn), lambda i, j: (0, j))]
    args = [a, w]
    if pre == "norm":
        in_specs.append(_full((1, k)))
        args.append(g_pre)
    if post:
        in_specs += [_full((1, n_out)), pl.BlockSpec((T, n_out), lambda i, j: (i, 0))]
        args += [g_post, xres]
    out_block = pl.BlockSpec((T, tn), lambda i, j: (i, j))
    if post:
        out_shape = [jax.ShapeDtypeStruct((S, n_out), BF16), jax.ShapeDtypeStruct((S, n_out), F32)]
        out_specs = [out_block, out_block]
    else:
        out_shape = [jax.ShapeDtypeStruct((S, n_out), out_dtype)]
        out_specs = [out_block]
    if pre == "norm":
        out_shape.append(jax.ShapeDtypeStruct((S, k), BF16))
        out_specs.append(pl.BlockSpec((T, k), lambda i, j: (i, 0)))
    scratch = [pltpu.VMEM((T, k), BF16)] if pre else []
    return pl.pallas_call(body, name=name, grid=grid, in_specs=in_specs, out_specs=out_specs, out_shape=out_shape,
                          scratch_shapes=scratch, compiler_params=_params("parallel", "arbitrary"))(*args)


def _mm_bwd_pre(m, dx, g, w, *, name, tile=TOKEN_TILE):
    S, n = m.shape
    ku = w.shape[0]
    T = min(tile, S)
    tn = _col_tile(ku)
    grid = (S // T, ku // tn)

    def body(m_ref, dx_ref, g_ref, w_ref, du_ref, dm_ref, dg_ref, as_ref):
        i, j = pl.program_id(0), pl.program_id(1)

        @pl.when((i == 0) & (j == 0))
        def _():
            dg_ref[...] = jnp.zeros_like(dg_ref)

        @pl.when(j == 0)
        def _():
            dmv, dg = _rms_bwd(m_ref[...].astype(F32), g_ref[...], dx_ref[...])
            dg_ref[...] += dg
            dmb = dmv.astype(BF16)
            dm_ref[...] = dmb
            as_ref[...] = dmb

        du_ref[...] = _dot(as_ref[...], w_ref[...], NT).astype(BF16)

    row = pl.BlockSpec((T, n), lambda i, j: (i, 0))
    return pl.pallas_call(
        body, name=name, grid=grid,
        in_specs=[row, row, _full((1, n)), pl.BlockSpec((tn, n), lambda i, j: (j, 0))],
        out_specs=[pl.BlockSpec((T, tn), lambda i, j: (i, j)), row, _full((1, n))],
        out_shape=[jax.ShapeDtypeStruct((S, ku), BF16), jax.ShapeDtypeStruct((S, n), BF16), jax.ShapeDtypeStruct((1, n), F32)],
        scratch_shapes=[pltpu.VMEM((T, n), BF16)], compiler_params=_params("arbitrary", "arbitrary"))(m, dx, g, w)


def _mm_bwd_post(dy, w, x, g, dx_in, *, nt, name, tile=TOKEN_TILE):
    S, k = dy.shape
    n = x.shape[1]
    T = min(tile, S)

    def body(dy_ref, w_ref, x_ref, g_ref, dxi_ref, dxo_ref, dg_ref):
        @pl.when(pl.program_id(0) == 0)
        def _():
            dg_ref[...] = jnp.zeros_like(dg_ref)

        dh = _dot(dy_ref[...], w_ref[...], NT if nt else NN)
        dxn, dg = _rms_bwd(x_ref[...], g_ref[...], dh)
        dg_ref[...] += dg
        dxo_ref[...] = dxi_ref[...] + dxn

    row = pl.BlockSpec((T, n), lambda i: (i, 0))
    return pl.pallas_call(
        body, name=name, grid=(S // T,),
        in_specs=[pl.BlockSpec((T, k), lambda i: (i, 0)), _full(w.shape), row, _full((1, n)), row],
        out_specs=[row, _full((1, n))],
        out_shape=[jax.ShapeDtypeStruct((S, n), F32), jax.ShapeDtypeStruct((1, n), F32)],
        compiler_params=_params("arbitrary"))(dy, w, x, g, dx_in)


def _wgrad(a, b, *, name, swiglu=False, tile=TOKEN_TILE):
    S = a.shape[0]
    k = a.shape[1] // 2 if swiglu else a.shape[1]
    n = b.shape[1]
    T = min(tile, S)
    tk = 512 if k % 512 == 0 else 256
    nk = k // tk

    def body(*refs):
        if swiglu:
            gate_ref, up_ref, b_ref, o_ref, act_ref = refs
            for r0 in range(0, S, T):
                gate, up = gate_ref[r0:r0 + T, :].astype(F32), up_ref[r0:r0 + T, :].astype(F32)
                act_ref[r0:r0 + T, :] = (_silu_parts(gate)[0] * up).astype(BF16)
            av = act_ref[...]
        else:
            a_ref, b_ref, o_ref = refs
            av = a_ref[...]
        o_ref[...] = _dot(av, b_ref[...], TN)

    in_specs = [pl.BlockSpec((S, tk), lambda kk: (0, kk))]
    args = [a]
    if swiglu:
        in_specs.append(pl.BlockSpec((S, tk), lambda kk: (0, kk + nk)))
        args.append(a)
    in_specs.append(_full((S, n)))
    args.append(b)
    return pl.pallas_call(body, name=name, grid=(nk,), in_specs=in_specs,
                          out_specs=pl.BlockSpec((tk, n), lambda kk: (kk, 0)),
                          out_shape=jax.ShapeDtypeStruct((k, n), F32),
                          scratch_shapes=[pltpu.VMEM((S, tk), BF16)] if swiglu else [],
                          compiler_params=_params("parallel"))(*args)


def _ffn_out_bwd(m, dx, g, w, gu, *, name, tile=256):
    S, n = m.shape
    f = w.shape[0]
    T = min(tile, S)

    def body(m_ref, dx_ref, g_ref, w_ref, gu_ref, dgu_ref, dm_ref, dg_ref, da_ref):
        @pl.when(pl.program_id(0) == 0)
        def _():
            dg_ref[...] = jnp.zeros_like(dg_ref)

        dmv, dg = _rms_bwd(m_ref[...].astype(F32), g_ref[...], dx_ref[...])
        dg_ref[...] += dg
        dmb = dmv.astype(BF16)
        dm_ref[...] = dmb
        da_ref[...] = _dot(dmb, w_ref[...], NT)
        step = 128 * max(1, (f // 128) // 4)
        for c0 in range(0, f, step):
            c1 = min(f, c0 + step)
            gate = gu_ref[:, c0:c1].astype(F32)
            up = gu_ref[:, f + c0:f + c1].astype(F32)
            da = da_ref[:, c0:c1]
            sl, sg = _silu_parts(gate)
            dgu_ref[:, c0:c1] = (da * up * (sg + sl * (1.0 - sg))).astype(BF16)
            dgu_ref[:, f + c0:f + c1] = (da * sl).astype(BF16)

    row = pl.BlockSpec((T, n), lambda i: (i, 0))
    wide = pl.BlockSpec((T, 2 * f), lambda i: (i, 0))
    return pl.pallas_call(
        body, name=name, grid=(S // T,), in_specs=[row, row, _full((1, n)), _full((f, n)), wide],
        out_specs=[wide, row, _full((1, n))],
        out_shape=[jax.ShapeDtypeStruct((S, 2 * f), BF16), jax.ShapeDtypeStruct((S, n), BF16), jax.ShapeDtypeStruct((1, n), F32)],
        scratch_shapes=[pltpu.VMEM((T, f), F32)], compiler_params=_params("arbitrary"))(m, dx, g, w, gu)


def _loss_grad(y, target, *, name, tile=TOKEN_TILE):
    S, d = y.shape
    T = min(tile, S)
    steps = S // T

    def body(y_ref, t_ref, loss_ref, dy_ref, acc_ref):
        i = pl.program_id(0)

        @pl.when(i == 0)
        def _():
            acc_ref[...] = jnp.zeros_like(acc_ref)

        diff = y_ref[...] - t_ref[...]
        dy_ref[...] = diff * (1.0 / d)
        acc_ref[...] += jnp.sum(diff * diff, axis=0, keepdims=True)

        @pl.when(i == steps - 1)
        def _():
            total = jnp.sum(acc_ref[...], axis=1, keepdims=True) * (0.5 / d)
            loss_ref[...] = jnp.broadcast_to(total, loss_ref.shape)

    row = pl.BlockSpec((T, d), lambda i: (i, 0))
    return pl.pallas_call(
        body, name=name, grid=(steps,), in_specs=[row, row], out_specs=[_full((1, 128)), row],
        out_shape=[jax.ShapeDtypeStruct((1, 128), F32), jax.ShapeDtypeStruct((S, d), F32)],
        scratch_shapes=[pltpu.VMEM((1, d), F32)], compiler_params=_params("arbitrary"))(y, target)


def _norm_dgain(x, dys, *, name):
    def body(x_ref, dy_ref, o_ref):
        xv = x_ref[...]
        dy = dy_ref[0]
        for l in range(1, dys.shape[0]):
            dy = dy + dy_ref[l]
        r = lax.rsqrt(jnp.mean(xv * xv, axis=-1, keepdims=True) + EPS)
        o_ref[...] = jnp.sum(dy * xv * r, axis=0, keepdims=True)

    return pl.pallas_call(body, name=name, out_shape=jax.ShapeDtypeStruct((1, x.shape[1]), F32),
                          compiler_params=pltpu.CompilerParams(vmem_limit_bytes=VMEM_LIMIT_BYTES))(x, dys)


def _xattn_probs(q_ref, kv_ref, h, d, hd):
    q = q_ref[:, h * hd:(h + 1) * hd]
    kh = kv_ref[:, h * hd:(h + 1) * hd]
    vh = kv_ref[:, d + h * hd:d + (h + 1) * hd]
    s = _dot(q, kh, NT) * (hd ** -0.5)
    e = jnp.exp(s - jnp.max(s, axis=-1, keepdims=True))
    return q, kh, vh, e / jnp.sum(e, axis=-1, keepdims=True)


def _xattn_fwd(q, kv, *, name, tile=TOKEN_TILE):
    S, d = q.shape
    mlen = kv.shape[0]
    hd = d // X_HEADS
    T = min(tile, S)

    def body(q_ref, kv_ref, o_ref):
        for h in range(X_HEADS):
            _, _, vh, p = _xattn_probs(q_ref, kv_ref, h, d, hd)
            o_ref[:, h * hd:(h + 1) * hd] = _dot(p, vh, NN).astype(BF16)

    row = pl.BlockSpec((T, d), lambda i: (i, 0))
    return pl.pallas_call(body, name=name, grid=(S // T,), in_specs=[row, _full((mlen, 2 * d))], out_specs=row,
                          out_shape=jax.ShapeDtypeStruct((S, d), BF16), compiler_params=_params("parallel"))(q, kv)


def _xattn_bwd(q, kv, do, *, name, tile=TOKEN_TILE):
    S, d = q.shape
    mlen = kv.shape[0]
    hd = d // X_HEADS
    T = min(tile, S)

    def body(q_ref, kv_ref, do_ref, dq_ref, dkv_ref):
        @pl.when(pl.program_id(0) == 0)
        def _():
            dkv_ref[...] = jnp.zeros_like(dkv_ref)

        for h in range(X_HEADS):
            qh, kh, vh, p = _xattn_probs(q_ref, kv_ref, h, d, hd)
            doh = do_ref[:, h * hd:(h + 1) * hd]
            dp = _dot(doh, vh, NT)
            ds = p * (dp - jnp.sum(p * dp, axis=-1, keepdims=True)) * (hd ** -0.5)
            dq_ref[:, h * hd:(h + 1) * hd] = _dot(ds, kh, NN).astype(BF16)
            dkv_ref[:, h * hd:(h + 1) * hd] += _dot(ds, qh, TN)
            dkv_ref[:, d + h * hd:d + (h + 1) * hd] += _dot(p, doh, TN)

    row = pl.BlockSpec((T, d), lambda i: (i, 0))
    return pl.pallas_call(
        body, name=name, grid=(S // T,), in_specs=[row, _full((mlen, 2 * d)), row],
        out_specs=[row, _full((mlen, 2 * d))],
        out_shape=[jax.ShapeDtypeStruct((S, d), BF16), jax.ShapeDtypeStruct((mlen, 2 * d), F32)],
        compiler_params=_params("arbitrary"))(q, kv, do)


GM0, POOL0, Q0, KV0, Z_END = 0, 512, 768, 1280, 1536


def _rope_tables(pos_col, inv_row, *, name):
    S = pos_col.shape[0]

    def body(p_ref, inv_ref, cos_ref, sin_ref):
        ang = p_ref[...].astype(F32) * inv_ref[...]
        sin = jnp.sin(ang)
        cos_ref[...] = jnp.cos(ang)
        sin_ref[...] = jnp.where(_lane(ang.shape) % HEAD_DIM < HEAD_DIM // 2, -sin, sin)

    T = min(TOKEN_TILE, S)
    return pl.pallas_call(
        body, name=name, grid=(S // T,),
        in_specs=[pl.BlockSpec((T, 1), lambda i: (i, 0)), _full((1, 128))],
        out_specs=[pl.BlockSpec((T, 128), lambda i: (i, 0))] * 2,
        out_shape=[jax.ShapeDtypeStruct((S, 128), F32)] * 2, compiler_params=_params("parallel"))(pos_col, inv_row)


def _swap_halves(x):
    n = x.shape[-1]
    return jnp.where(_lane(x.shape) % HEAD_DIM < HEAD_DIM // 2, pltpu.roll(x, n - HEAD_DIM // 2, 1), pltpu.roll(x, HEAD_DIM // 2, 1))


def _rope(x, cos, sin_s):
    reps = x.shape[-1] // 128
    if reps > 1:
        cos, sin_s = jnp.tile(cos, (1, reps)), jnp.tile(sin_s, (1, reps))
    return x * cos + _swap_halves(x) * sin_s


def _rope_bwd(dy, cos, sin_s):
    reps = dy.shape[-1] // 128
    if reps > 1:
        cos, sin_s = jnp.tile(cos, (1, reps)), jnp.tile(sin_s, (1, reps))
    return dy * cos + _swap_halves(dy * sin_s)


def _gelu_parts(x):
    c = math.sqrt(2.0 / math.pi)
    t = jnp.tanh(c * (x + 0.044715 * x * x * x))
    return 0.5 * x * (1.0 + t), t


def _gelu_grad(x, t):
    c = math.sqrt(2.0 / math.pi)
    return 0.5 * (1.0 + t) + 0.5 * x * (1.0 - t * t) * c * (1.0 + 3.0 * 0.044715 * x * x)


def _group_sum(x):
    gid = _lane(x.shape) // HEAD_DIM
    out = jnp.zeros_like(x)
    for g in range(x.shape[-1] // HEAD_DIM):
        sel = gid == g
        out = jnp.where(sel, jnp.sum(jnp.where(sel, x, 0.0), axis=-1, keepdims=True), out)
    return out


def _low_half(shape):
    return _lane(shape) % 128 < HEAD_DIM


def _tril(w):
    r = lax.broadcasted_iota(jnp.int32, w.shape, 0)
    c = lax.broadcasted_iota(jnp.int32, w.shape, 1)
    return jnp.where(r >= c, w, 0.0)


def _triu(w):
    r = lax.broadcasted_iota(jnp.int32, w.shape, 0)
    c = lax.broadcasted_iota(jnp.int32, w.shape, 1)
    return jnp.where(r <= c, w, 0.0)


def _gating_fwd(zg, vg, wc_ref, bfull):
    ge, th = _gelu_parts(zg)
    u, v = ge[:, :256], ge[:, 256:]
    r = lax.rsqrt(_group_sum(v * v) * (1.0 / HEAD_DIM) + EPS)
    vhat = v * r
    vn = vhat * vg
    low = _low_half((BLOCK, 128))
    blocks = []
    for blk in range(2):
        vb = vn[:, 128 * blk:128 * (blk + 1)]
        m0 = _dot(_tril(wc_ref[2 * blk]), vb, NN)
        m1 = _dot(_tril(wc_ref[2 * blk + 1]), vb, NN)
        blocks.append(jnp.where(low, m0, m1))
    mixed = jnp.concatenate(blocks, axis=1) + bfull
    return u * mixed, (th, u, r, vhat, vn, mixed)


def _band(w, transpose):
    if transpose:
        s2 = lax.broadcasted_iota(jnp.int32, (2 * BLOCK, BLOCK), 0)
        t = lax.broadcasted_iota(jnp.int32, (2 * BLOCK, BLOCK), 1)
    else:
        t = lax.broadcasted_iota(jnp.int32, (BLOCK, 2 * BLOCK), 0)
        s2 = lax.broadcasted_iota(jnp.int32, (BLOCK, 2 * BLOCK), 1)
    d = t - s2 + BLOCK
    return jnp.where((d >= 0) & (d < w), 1.0, 0.0).astype(BF16)


def _exact_dot(b01, x):
    hi = x.astype(BF16)
    lo = (x - hi.astype(F32)).astype(BF16)
    return _dot(b01, hi, NN) + _dot(b01, lo, NN)


def _inv_count(n):
    t = lax.broadcasted_iota(jnp.int32, (BLOCK, 256), 0) + n * BLOCK
    gid = _lane((BLOCK, 256)) // HEAD_DIM
    win = jnp.where(gid == 0, POOL_WINDOWS[0], jnp.where(gid == 1, POOL_WINDOWS[1], jnp.where(gid == 2, POOL_WINDOWS[2], POOL_WINDOWS[3])))
    return 1.0 / jnp.minimum(t + 1, win).astype(F32)


def _pool_fwd(p, pp, n, bd, scale):
    low = _low_half((BLOCK, 128))
    blocks = []
    for blk in range(2):
        both = jnp.concatenate([pp[:, 128 * blk:128 * (blk + 1)], p[:, 128 * blk:128 * (blk + 1)]], axis=0)
        s0 = _exact_dot(_band(POOL_WINDOWS[2 * blk], False), both)
        s1 = _exact_dot(_band(POOL_WINDOWS[2 * blk + 1], False), both)
        blocks.append(jnp.where(low, s0, s1))
    inv = _inv_count(n)
    pooled = jnp.concatenate(blocks, axis=1) * inv - p
    mapped = _dot(pooled, bd, NN)
    return mapped * scale, (inv, pooled, mapped)


def _mixer_fwd(z, cos_t, sin_t, vg, wc, bfull, bd, scale, sinks, *, name):
    S = z.shape[0]
    nb = S // BLOCK

    def body(zc_ref, zpp_ref, zpk_ref, cc_ref, cp_ref, sc_ref, sp_ref, vg_ref, wc_ref, bf_ref, bd_ref, scale_ref, sinks_ref, out_ref):
        n = pl.program_id(0)
        keep = (n > 0).astype(F32)
        a, _ = _gating_fwd(zc_ref[:, GM0:POOL0].astype(F32), vg_ref[...], wc_ref, bf_ref[...])
        out_ref[:, 0:256] = a.astype(BF16)
        b, _ = _pool_fwd(zc_ref[:, POOL0:Q0].astype(F32), zpp_ref[...].astype(F32) * keep, n, bd_ref[...], scale_ref[...])
        out_ref[:, 256:512] = b.astype(BF16)
        cos_c, sin_c = cc_ref[...], sc_ref[...]
        qrot = _rope(zc_ref[:, Q0:KV0].astype(F32), cos_c, sin_c)
        kk = jnp.concatenate([_rope(zpk_ref[:, 0:128].astype(F32), cp_ref[...], sp_ref[...]), _rope(zc_ref[:, KV0:KV0 + 128].astype(F32), cos_c, sin_c)], axis=0)
        vv = jnp.concatenate([zpk_ref[:, 128:256].astype(F32), zc_ref[:, KV0 + 128:Z_END].astype(F32)], axis=0)
        kk, vv = kk.astype(BF16), vv.astype(BF16)
        tall = (ATT_Q_HEADS // 2 * BLOCK, 2 * BLOCK)
        qi = lax.broadcasted_iota(jnp.int32, tall, 0) % BLOCK
        ki = lax.broadcasted_iota(jnp.int32, tall, 1)
        valid = (ki > qi) & (ki <= qi + BLOCK) & (ki >= jnp.where(n > 0, 0, BLOCK))
        low = _low_half((BLOCK, 128))
        q_slabs = [qrot[:, 128 * s:128 * (s + 1)] for s in range(ATT_Q_HEADS // 2)]
        q_rolled = [pltpu.roll(v, HEAD_DIM, 1) for v in q_slabs]
        o_heads = [None] * ATT_Q_HEADS
        for h in range(2):
            lanes = ~low if h else low
            heads = range(4 * h, 4 * h + 4)
            qm = jnp.concatenate([jnp.where(lanes, (q_slabs if j % 2 == h else q_rolled)[j // 2], 0.0) for j in heads], axis=0)
            sc = jnp.where(valid, _dot(qm, kk, NT) * (HEAD_DIM ** -0.5), NEG)
            p = []
            for i, j in enumerate(heads):
                sch, sink = sc[BLOCK * i:BLOCK * (i + 1)], sinks_ref[j]
                mx = jnp.maximum(jnp.max(sch, axis=-1, keepdims=True), sink)
                e = jnp.exp(sch - mx)
                p.append((e / (jnp.sum(e, axis=-1, keepdims=True) + jnp.exp(sink - mx))).astype(BF16))
            o = _dot(jnp.concatenate(p, axis=0), vv, NN)
            for i, j in enumerate(heads):
                oh = o[BLOCK * i:BLOCK * (i + 1)]
                o_heads[j] = oh if j % 2 == h else pltpu.roll(oh, HEAD_DIM, 1)
        for s in range(ATT_Q_HEADS // 2):
            out_ref[:, 512 + 128 * s:512 + 128 * (s + 1)] = jnp.where(low, o_heads[2 * s], o_heads[2 * s + 1]).astype(BF16)

    cur = lambda n: (n, 0)
    prev = lambda n: (jnp.maximum(n - 1, 0), 0)
    in_specs = [
        pl.BlockSpec((BLOCK, Z_END), cur),
        pl.BlockSpec((BLOCK, 256), lambda n: (jnp.maximum(n - 1, 0), POOL0 // 256)),
        pl.BlockSpec((BLOCK, 256), lambda n: (jnp.maximum(n - 1, 0), KV0 // 256)),
        pl.BlockSpec((BLOCK, 128), cur), pl.BlockSpec((BLOCK, 128), prev),
        pl.BlockSpec((BLOCK, 128), cur), pl.BlockSpec((BLOCK, 128), prev),
        _full((1, 256)), _full((4, BLOCK, BLOCK)), _full((BLOCK, 256)), _full((256, 256)), _full((1, 256)),
        pl.BlockSpec(memory_space=pltpu.SMEM),
    ]
    return pl.pallas_call(
        body, name=name, grid=(nb,), in_specs=in_specs, out_specs=pl.BlockSpec((BLOCK, 1024), cur),
        out_shape=jax.ShapeDtypeStruct((S, 1024), BF16),
        compiler_params=_params("parallel"))(z, z, z, cos_t, cos_t, sin_t, sin_t, vg, wc, bfull, bd, scale, sinks)


def _mixer_bwd(z, dcat, cos_t, sin_t, vg, wc, bfull, bd, scale, sinks, *, name):
    S = z.shape[0]
    nb = S // BLOCK

    def body(zc_ref, zpp_ref, zpk_ref, dc_ref, cc_ref, cp_ref, sc_ref, sp_ref, vg_ref, wc_ref, bf_ref, bd_ref, scale_ref, sinks_ref, wct_ref,
             dz_ref, dvg_ref, dwc_ref, dbf_ref, dbd_ref, dscale_ref, dsink_ref, carry_ref):
        n = pl.program_id(0)

        @pl.when(n == 0)
        def _():
            for ref in (dvg_ref, dwc_ref, dbf_ref, dbd_ref, dscale_ref, dsink_ref):
                ref[...] = jnp.zeros_like(ref)

        @pl.when(n < nb)
        def _():
            keep = (n > 0).astype(F32)
            low = _low_half((BLOCK, 128))
            zg = zc_ref[:, GM0:POOL0].astype(F32)
            vg_v = vg_ref[...]
            _, (th, u, r, vhat, vn, mixed) = _gating_fwd(zg, vg_v, wc_ref, bf_ref[...])
            da = dc_ref[:, 0:256].astype(F32)
            dmixed = da * u
            dbf_ref[...] += dmixed
            dvn = []
            for blk in range(2):
                dmb = dmixed[:, 128 * blk:128 * (blk + 1)]
                vb = vn[:, 128 * blk:128 * (blk + 1)]
                acc = None
                for half in range(2):
                    g = 2 * blk + half
                    dmg = jnp.where(low if half == 0 else ~low, dmb, 0.0)
                    dwc_ref[g] += _tril(_dot(dmg, vb, NT))
                    part = _dot(_triu(wct_ref[g]), dmg, NN)
                    acc = part if acc is None else acc + part
                dvn.append(acc)
            dvn = jnp.concatenate(dvn, axis=1)
            dvg_ref[...] += jnp.sum(dvn * vhat, axis=0, keepdims=True)
            dvhat = dvn * vg_v
            dv = r * (dvhat - vhat * (_group_sum(dvhat * vhat) * (1.0 / HEAD_DIM)))
            dge = jnp.concatenate([da * mixed, dv], axis=1)
            own_g = dge * _gelu_grad(zg, th)
            p = zc_ref[:, POOL0:Q0].astype(F32)
            bd_v, scale_v = bd_ref[...], scale_ref[...]
            _, (inv, pooled, mapped) = _pool_fwd(p, zpp_ref[...].astype(F32) * keep, n, bd_v, scale_v)
            db = dc_ref[:, 256:512].astype(F32)
            dscale_ref[...] += jnp.sum(db * mapped, axis=0, keepdims=True)
            dmapped = db * scale_v
            dbd_ref[...] += _dot(pooled, dmapped, TN)
            dpooled = _dot(dmapped, bd_v, NT)
            dps = dpooled * inv
            back = []
            for blk in range(2):
                dpb = dps[:, 128 * blk:128 * (blk + 1)]
                t0 = _exact_dot(_band(POOL_WINDOWS[2 * blk], True), dpb)
                t1 = _exact_dot(_band(POOL_WINDOWS[2 * blk + 1], True), dpb)
                back.append(jnp.where(jnp.concatenate([low, low], axis=0), t0, t1))
            back = jnp.concatenate(back, axis=1)
            halo_p = back[:BLOCK] * keep
            own_p = back[BLOCK:] - dpooled
            cos_c, sin_c, cos_p, sin_p = cc_ref[...], sc_ref[...], cp_ref[...], sp_ref[...]
            qrot = _rope(zc_ref[:, Q0:KV0].astype(F32), cos_c, sin_c)
            kk = jnp.concatenate([_rope(zpk_ref[:, 0:128].astype(F32), cos_p, sin_p), _rope(zc_ref[:, KV0:KV0 + 128].astype(F32), cos_c, sin_c)], axis=0)
            vv = jnp.concatenate([zpk_ref[:, 128:256].astype(F32), zc_ref[:, KV0 + 128:Z_END].astype(F32)], axis=0)
            kt = jnp.transpose(kk).astype(BF16)
            kk, vv = kk.astype(BF16), vv.astype(BF16)
            wide = (2 * BLOCK, ATT_Q_HEADS // 2 * BLOCK)
            ki = lax.broadcasted_iota(jnp.int32, wide, 0)
            qi = lax.broadcasted_iota(jnp.int32, wide, 1) % BLOCK
            valid_t = (ki > qi) & (ki <= qi + BLOCK) & (ki >= jnp.where(n > 0, 0, BLOCK))
            q_slabs = [qrot[:, 128 * s:128 * (s + 1)] for s in range(ATT_Q_HEADS // 2)]
            d_slabs = [dc_ref[:, 512 + 128 * s:512 + 128 * (s + 1)].astype(F32) for s in range(ATT_Q_HEADS // 2)]
            q_rolled = [pltpu.roll(v, HEAD_DIM, 1) for v in q_slabs]
            d_rolled = [pltpu.roll(v, HEAD_DIM, 1) for v in d_slabs]
            dkk = jnp.zeros((2 * BLOCK, 128), F32)
            dvv = jnp.zeros((2 * BLOCK, 128), F32)
            dq_heads = [None] * ATT_Q_HEADS
            for h in range(2):
                lanes = ~low if h else low
                heads = range(4 * h, 4 * h + 4)
                qm = jnp.concatenate([jnp.where(lanes, (q_slabs if j % 2 == h else q_rolled)[j // 2], 0.0) for j in heads], axis=0).astype(BF16)
                dom = jnp.concatenate([jnp.where(lanes, (d_slabs if j % 2 == h else d_rolled)[j // 2], 0.0) for j in heads], axis=0).astype(BF16)
                sink = jnp.concatenate([jnp.full((1, BLOCK), sinks_ref[j], F32) for j in heads], axis=1)
                sc = jnp.where(valid_t, _dot(kk, qm, NT) * (HEAD_DIM ** -0.5), NEG)
                mx = jnp.maximum(jnp.max(sc, axis=0, keepdims=True), sink)
                e = jnp.exp(sc - mx)
                es = jnp.exp(sink - mx)
                inv_den = 1.0 / (jnp.sum(e, axis=0, keepdims=True) + es)
                pt = e * inv_den
                dpt = _dot(vv, dom, NT)
                dvv = dvv + _dot(pt, dom, NN)
                dcol = jnp.sum(pt * dpt, axis=0, keepdims=True)
                dst = (pt * (dpt - dcol) * (HEAD_DIM ** -0.5)).astype(BF16)
                dsink = es * inv_den * dcol
                dkk = dkk + _dot(dst, qm, NN)
                dqt = _dot(kt, dst, NN)
                for i, j in enumerate(heads):
                    part = jnp.sum(dsink[:, BLOCK * i:BLOCK * (i + 1)], axis=1, keepdims=True)
                    dsink_ref[pl.ds(j, 1), :] = dsink_ref[pl.ds(j, 1), :] - jnp.broadcast_to(part, (1, 128))
                    dqh = jnp.transpose(dqt[:, BLOCK * i:BLOCK * (i + 1)])
                    dq_heads[j] = dqh if j % 2 == h else pltpu.roll(dqh, HEAD_DIM, 1)
            dqrot = jnp.concatenate([jnp.where(low, dq_heads[2 * s], dq_heads[2 * s + 1]) for s in range(ATT_Q_HEADS // 2)], axis=1)
            own_q = _rope_bwd(dqrot, cos_c, sin_c)
            own_k = _rope_bwd(dkk[BLOCK:], cos_c, sin_c)
            halo_k = _rope_bwd(dkk[:BLOCK], cos_p, sin_p) * keep
            own_v, halo_v = dvv[BLOCK:], dvv[:BLOCK] * keep

            @pl.when(n > 0)
            def _():
                dz_ref[:, GM0:POOL0] = carry_ref[:, GM0:POOL0].astype(BF16)
                dz_ref[:, POOL0:Q0] = (carry_ref[:, POOL0:Q0] + halo_p).astype(BF16)
                dz_ref[:, Q0:KV0] = carry_ref[:, Q0:KV0].astype(BF16)
                dz_ref[:, KV0:KV0 + 128] = (carry_ref[:, KV0:KV0 + 128] + halo_k).astype(BF16)
                dz_ref[:, KV0 + 128:Z_END] = (carry_ref[:, KV0 + 128:Z_END] + halo_v).astype(BF16)

            carry_ref[:, GM0:POOL0] = own_g
            carry_ref[:, POOL0:Q0] = own_p
            carry_ref[:, Q0:KV0] = own_q
            carry_ref[:, KV0:KV0 + 128] = own_k
            carry_ref[:, KV0 + 128:Z_END] = own_v

        @pl.when(n == nb)
        def _():
            dz_ref[...] = carry_ref[...].astype(BF16)

    cur = lambda n: (jnp.minimum(n, nb - 1), 0)
    prev = lambda n: (jnp.maximum(jnp.minimum(n, nb - 1) - 1, 0), 0)
    in_specs = [
        pl.BlockSpec((BLOCK, Z_END), cur),
        pl.BlockSpec((BLOCK, 256), lambda n: (jnp.maximum(jnp.minimum(n, nb - 1) - 1, 0), POOL0 // 256)),
        pl.BlockSpec((BLOCK, 256), lambda n: (jnp.maximum(jnp.minimum(n, nb - 1) - 1, 0), KV0 // 256)),
        pl.BlockSpec((BLOCK, 1024), cur),
        pl.BlockSpec((BLOCK, 128), cur), pl.BlockSpec((BLOCK, 128), prev),
        pl.BlockSpec((BLOCK, 128), cur), pl.BlockSpec((BLOCK, 128), prev),
        _full((1, 256)), _full((4, BLOCK, BLOCK)), _full((BLOCK, 256)), _full((256, 256)), _full((1, 256)),
        pl.BlockSpec(memory_space=pltpu.SMEM), _full((4, BLOCK, BLOCK)),
    ]
    out_specs = [pl.BlockSpec((BLOCK, Z_END), lambda n: (jnp.maximum(n - 1, 0), 0)),
                 _full((1, 256)), _full((4, BLOCK, BLOCK)), _full((BLOCK, 256)), _full((256, 256)), _full((1, 256)), _full((8, 128))]
    out_shape = [jax.ShapeDtypeStruct((S, Z_END), BF16), jax.ShapeDtypeStruct((1, 256), F32),
                 jax.ShapeDtypeStruct((4, BLOCK, BLOCK), F32), jax.ShapeDtypeStruct((BLOCK, 256), F32),
                 jax.ShapeDtypeStruct((256, 256), F32), jax.ShapeDtypeStruct((1, 256), F32), jax.ShapeDtypeStruct((8, 128), F32)]
    return pl.pallas_call(
        body, name=name, grid=(nb + 1,), in_specs=in_specs, out_specs=out_specs, out_shape=out_shape,
        scratch_shapes=[pltpu.VMEM((BLOCK, Z_END), F32)],
        compiler_params=_params("arbitrary"))(z, z, z, dcat, cos_t, cos_t, sin_t, sin_t, vg, wc, bfull, bd, scale, sinks, jnp.swapaxes(wc, 1, 2))


HBM = pl.BlockSpec(memory_space=pl.ANY)


def _place():
    return lax.axis_index("x"), lax.axis_index("y"), lax.axis_index("c")


def _chip(k, x, y):
    return (1 - x if k & 1 else x, 1 - y if k & 2 else y)


def _block_rows(ref, rows, dev):
    start = pl.multiple_of((4 * dev[0] + 2 * dev[1] + dev[2]) * rows, 8)
    return ref.at[pl.ds(start, rows), :]


def _group_size(arrays, group):
    return sum(arrays[w].shape[0] for w in group)


def _all_gather(shards, groups, carrier, *, name):
    n = len(shards)
    assert all(_group_size(shards, g) == carrier.shape[0] for g in groups) and carrier.dtype == shards[0].dtype

    def body(*refs):
        ins, car, outs = refs[:n], refs[n], refs[n + 1:2 * n + 1]
        send_sems, recv_sems, local_sems = refs[2 * n + 1:]
        x, y, c = _place()
        me, sibling = (x, y, c), (x, y, 1 - c)
        chips = [_chip(k, x, y) for k in (1, 2, 3)]

        def copy(gi, w, k, block, to, src=None):
            dst = _block_rows(outs[w], ins[w].shape[0], block)
            return pltpu.make_async_remote_copy(src_ref=dst if src is None else src, dst_ref=dst, send_sem=send_sems.at[gi, k],
                                                recv_sem=recv_sems.at[gi, k], device_id=to, device_id_type=MESH)

        def drained(gi, k):
            return pltpu.make_async_remote_copy(src_ref=car, dst_ref=car, send_sem=send_sems.at[gi, k],
                                                recv_sem=recv_sems.at[gi, k], device_id=me, device_id_type=MESH)

        for gi, group in enumerate(groups):
            for w in group:
                pltpu.make_async_copy(ins[w], _block_rows(outs[w], ins[w].shape[0], me), local_sems.at[gi]).start()
                copy(gi, w, 0, me, sibling, src=ins[w]).start()
                for j, chip in enumerate(chips):
                    copy(gi, w, 1 + j, me, (*chip, c), src=ins[w]).start()
        for gi, group in enumerate(groups):
            for j, chip in enumerate(chips):
                drained(gi, 1 + j).wait_recv()
                for w in group:
                    copy(gi, w, 4 + j, (*chip, c), sibling).start()
        for gi in range(len(groups)):
            drained(gi, 0).wait_recv()
            for j in range(3):
                drained(gi, 4 + j).wait_recv()
        for gi in range(len(groups)):
            for k in range(7):
                drained(gi, k).wait_send()
            pltpu.make_async_copy(car, car, local_sems.at[gi]).wait()

    g = len(groups)
    return pl.pallas_call(
        body, name=name, in_specs=[HBM] * (n + 1), out_specs=[HBM] * n,
        out_shape=[jax.ShapeDtypeStruct((N_DEV * s.shape[0], s.shape[1]), s.dtype) for s in shards],
        scratch_shapes=[pltpu.SemaphoreType.DMA((g, 7)), pltpu.SemaphoreType.DMA((g, 7)), pltpu.SemaphoreType.DMA((g,))],
    )(*shards, carrier)


_HBM_SPEC = pl.BlockSpec(memory_space=pltpu.HBM)
_SEM_SPEC = pl.BlockSpec(memory_space=pltpu.SEMAPHORE)
_SPLIT_PARAMS = dict(compiler_params=pltpu.CompilerParams(has_side_effects=pltpu.SideEffectType.DATAFLOW_SIDE_EFFECTING))


def _descriptors(copies, refs, send_sems, recv_sems):
    return [pltpu.make_async_remote_copy(src_ref=s, dst_ref=d, send_sem=send_sems.at[i], recv_sem=recv_sems.at[i], device_id=to, device_id_type=MESH)
            for i, (s, d, to) in enumerate(copies(refs))]


def _start(copies, arrays, fresh, n_copies, *, name):
    operands = [pltpu.with_memory_space_constraint(v, pltpu.HBM) for v in (*arrays, *[lax.empty(f.shape, f.dtype) for f in fresh])]
    n = len(operands)

    def body(*refs):
        for dma in _descriptors(copies, refs[:n], refs[n], refs[n + 1]):
            dma.start()
        refs[-1][...] = jnp.zeros_like(refs[-1])

    res = pl.pallas_call(
        body, name=name, in_specs=[_HBM_SPEC] * n,
        out_shape=(pltpu.SemaphoreType.DMA((n_copies,)), pltpu.SemaphoreType.DMA((n_copies,)),
                   *[pltpu.HBM(v.shape, v.dtype) for v in operands], jax.ShapeDtypeStruct((8, 128), F32)),
        out_specs=(_SEM_SPEC, _SEM_SPEC, *[_HBM_SPEC] * n, pl.BlockSpec(memory_space=pltpu.VMEM)),
        input_output_aliases={i: 2 + i for i in range(n)}, **_SPLIT_PARAMS)(*operands)
    return res[0], res[1], list(res[2:2 + n]), res[-1]


def _finish(copies, send_sems, recv_sems, arrays, after, *, name):
    n = len(arrays)

    def body(*refs):
        for dma in _descriptors(copies, refs[:n], refs[n], refs[n + 1]):
            dma.wait_send()
            dma.wait_recv()

    return list(pl.pallas_call(
        body, name=name, in_specs=[_HBM_SPEC] * n + [_SEM_SPEC, _SEM_SPEC] + [HBM] * len(after),
        out_shape=tuple(pltpu.HBM(v.shape, v.dtype) for v in arrays), out_specs=tuple([_HBM_SPEC] * n),
        input_output_aliases={i: i for i in range(n)}, **_SPLIT_PARAMS)(*arrays, send_sems, recv_sems, *after))


def _tie(value, tokens):
    if not tokens:
        return value

    def body(*refs):
        pass

    return pl.pallas_call(body, name="tie", in_specs=[HBM] * (1 + len(tokens)), out_specs=HBM,
                          out_shape=jax.ShapeDtypeStruct(value.shape, value.dtype), input_output_aliases={0: 0})(value, *tokens)


def _gather_copies(stage):
    def copies(refs):
        x, y, c = _place()
        me, sibling = (x, y, c), (x, y, 1 - c)
        out = []
        for ref in refs:
            r = ref.shape[0] // N_DEV
            if stage == "a":
                hops = [(me, sibling)] + [(me, (*_chip(k, x, y), c)) for k in (1, 2, 3)]
            else:
                hops = [((*_chip(k, x, y), c), sibling) for k in (1, 2, 3)]
            for block, to in hops:
                rows = _block_rows(ref, r, block)
                out.append((rows, rows, to))
        return out
    return copies


def _scatter_copies(stage, n):
    def copies(refs):
        x, y, c = _place()
        out = []
        for w in range(n):
            src, dst = refs[w], refs[n + w]
            if stage == "a":
                r = src.shape[0] // N_DEV
                out += [(_block_rows(src, r, (*_chip(k, x, y), 1 - c)), dst.at[k], (x, y, 1 - c)) for k in range(4)]
            else:
                out += [(src.at[k - 1], dst.at[k - 1], (*_chip(k, x, y), c)) for k in (1, 2, 3)]
        return out
    return copies


def _place_own(shards, layer, blocks, *, name):
    _, r, cdim = shards.shape

    def body(blocks_ref, s_ref, o_ref):
        o_ref[...] = s_ref[...].astype(BF16)

    return pl.pallas_call(
        body, name=name,
        grid_spec=pltpu.PrefetchScalarGridSpec(
            num_scalar_prefetch=1, grid=(1,), in_specs=[pl.BlockSpec((None, r, cdim), lambda i, blocks: (layer, 0, 0))],
            out_specs=pl.BlockSpec((r, cdim), lambda i, blocks: (blocks[0], 0))),
        out_shape=jax.ShapeDtypeStruct((N_DEV * r, cdim), BF16), compiler_params=_params("arbitrary"))(blocks, shards)


def _chip_sum(grad, got, blocks, *, name):
    _, r, cdim = got.shape
    tr = r if r <= 256 else r // (2 if r % 32 == 0 and r // 2 <= 256 else 4)
    steps = r // tr

    def body(blocks_ref, a0_ref, a1_ref, a2_ref, a3_ref, b_ref, mine_ref, away_ref):
        mine_ref[...] = a0_ref[...] + b_ref[0]
        for k, a_ref in ((1, a1_ref), (2, a2_ref), (3, a3_ref)):
            away_ref[k - 1] = (a_ref[...] + b_ref[k]).astype(BF16)

    own = [pl.BlockSpec((tr, cdim), lambda i, blocks, k=k: (blocks[k] * steps + i, 0)) for k in range(4)]
    return pl.pallas_call(
        body, name=name,
        grid_spec=pltpu.PrefetchScalarGridSpec(
            num_scalar_prefetch=1, grid=(steps,),
            in_specs=own + [pl.BlockSpec((4, tr, cdim), lambda i, blocks: (0, i, 0))],
            out_specs=[pl.BlockSpec((tr, cdim), lambda i, blocks: (i, 0)), pl.BlockSpec((3, tr, cdim), lambda i, blocks: (0, i, 0))]),
        out_shape=[jax.ShapeDtypeStruct((r, cdim), F32), jax.ShapeDtypeStruct((3, r, cdim), BF16)],
        compiler_params=_params("arbitrary"))(blocks, grad, grad, grad, grad, got)


def _final_sum(mine, got, *, name):
    r, cdim = mine.shape
    tr = r if r <= 256 else r // (2 if r % 32 == 0 and r // 2 <= 256 else 4)

    def body(m_ref, g_ref, o_ref):
        o_ref[...] = ((m_ref[...] + g_ref[0].astype(F32)) + g_ref[1].astype(F32)) + g_ref[2].astype(F32)

    return pl.pallas_call(
        body, name=name, grid=(r // tr,),
        in_specs=[pl.BlockSpec((tr, cdim), lambda i: (i, 0)), pl.BlockSpec((3, tr, cdim), lambda i: (0, i, 0))],
        out_specs=pl.BlockSpec((tr, cdim), lambda i: (i, 0)), out_shape=jax.ShapeDtypeStruct((r, cdim), F32),
        compiler_params=_params("parallel"))(mine, got)


def _adamw_math(w, g, m, v):
    m = ADAM_B1 * m + (1.0 - ADAM_B1) * g
    v = ADAM_B2 * v + (1.0 - ADAM_B2) * (g * g)
    m_hat = m / (1.0 - ADAM_B1 ** ADAM_STEP)
    v_hat = v / (1.0 - ADAM_B2 ** ADAM_STEP)
    return -ADAM_LR * (m_hat / (jnp.sqrt(v_hat) + ADAM_EPS) + ADAM_WD * w), m, v


def _adamw(w, g, m, v, *, name):
    L, r, cdim = w.shape
    tr = r if r <= 512 else r // 2
    assert r % tr == 0 and tr % 8 == 0

    def body(w_ref, g_ref, m_ref, v_ref, d_ref, nm_ref, nv_ref):
        d_ref[...], nm_ref[...], nv_ref[...] = _adamw_math(w_ref[...], g_ref[...], m_ref[...], v_ref[...])

    blk = pl.BlockSpec((None, tr, cdim), lambda l, i: (l, i, 0))
    return pl.pallas_call(body, name=name, grid=(L, r // tr), in_specs=[blk] * 4, out_specs=[blk] * 3,
                          out_shape=[jax.ShapeDtypeStruct(w.shape, F32)] * 3, compiler_params=_params("parallel", "parallel"))(w, g, m, v)


def _small_update(parts, w, m, v, *, name):
    _, p, lanes = parts.shape
    tp = p // 2 if p % 16 == 0 else p

    def body(p_ref, w_ref, m_ref, v_ref, g_ref, d_ref, nm_ref, nv_ref):
        g = p_ref[0]
        for i in range(1, N_DEV):
            g = g + p_ref[i]
        g_ref[...] = g
        d_ref[...], nm_ref[...], nv_ref[...] = _adamw_math(w_ref[...], g, m_ref[...], v_ref[...])

    blk = pl.BlockSpec((tp, lanes), lambda i: (i, 0))
    return pl.pallas_call(body, name=name, grid=(p // tp,), in_specs=[pl.BlockSpec((N_DEV, tp, lanes), lambda i: (0, i, 0)), blk, blk, blk],
                          out_specs=[blk] * 4, out_shape=[jax.ShapeDtypeStruct((p, lanes), F32)] * 4,
                          compiler_params=_params("parallel"))(parts, w, m, v)


BIG = ("w_in", "w_o", "w_xq", "w_xkv", "w_xo", "w_gate_up", "w_down")
TRANSPOSED = ("w_in", "w_xkv", "w_gate_up")
SMALL = ("mem_norm_g", "mix_pre_g", "mix_post_g", "gm_v_g", "gm_w_s", "gm_b_s", "pool_w", "pool_scale", "attn_sinks",
         "x_pre_g", "x_post_g", "ffn_pre_g", "ffn_post_g")
WEIGHTS = ("mem_norm_g", "mix_pre_g", "mix_post_g", "w_in", "gm_v_g", "gm_w_s", "gm_b_s", "pool_w", "pool_scale", "attn_sinks", "w_o",
           "x_pre_g", "x_post_g", "w_xq", "w_xkv", "w_xo", "ffn_pre_g", "ffn_post_g", "w_gate_up", "w_down")
PACK_QUANTUM = 8 * 128


def _pack(arrays):
    flat = []
    for a in arrays:
        a = a.reshape(-1).astype(F32)
        flat.append(jnp.pad(a, (0, -a.size % PACK_QUANTUM)))
    return jnp.concatenate(flat).reshape(-1, 128)


def _unpack(pack, shapes):
    out, row = [], 0
    for s in shapes:
        size = math.prod(s)
        rows = (size + (-size % PACK_QUANTUM)) // 128
        out.append(pack[row:row + rows].reshape(-1)[:size].reshape(s))
        row += rows
    return out


def _row(v):
    return v.reshape(1, -1)


def _local_step(x, mem, positions, target, small, weight, fwd_hook=None, bwd_hook=None):
    fwd_hook = fwd_hook or (lambda l, point, v: v)
    bwd_hook = bwd_hook or (lambda l, point, v, gb: v)
    depth = small["mix_pre_g"].shape[0]
    half = HEAD_DIM // 2
    inv = ROPE_THETA ** (-jnp.arange(half, dtype=F32) / half)
    cos_t, sin_t = _rope_tables(positions.reshape(-1, 1), jnp.tile(inv, 128 // half).reshape(1, 128), name="rope_tables")
    mem_g = _row(small["mem_norm_g"])
    saved = []
    for l in range(depth):
        w = functools.partial(weight, l)
        bfull = jnp.repeat(small["gm_b_s"][l].T, HEAD_DIM, axis=1)
        bd = jax.scipy.linalg.block_diag(*[small["pool_w"][l, g] for g in range(len(POOL_WINDOWS))])
        mix_par = (_row(small["gm_v_g"][l]), small["gm_w_s"][l], bfull, bd, _row(small["pool_scale"][l]), small["attn_sinks"][l])
        x = fwd_hook(l, 0, x)
        z, h1 = _mm_fwd(x, w("w_in"), nt=True, pre="norm", g_pre=_row(small["mix_pre_g"][l]), name="in_proj")
        cat = fwd_hook(l, 1, _mixer_fwd(z, cos_t, sin_t, *mix_par, name="mixer_fwd"))
        mix, x1 = _mm_fwd(cat, w("w_o"), nt=False, post="norm_res", g_post=_row(small["mix_post_g"][l]), xres=x, name="mix_out")
        qx, h2 = _mm_fwd(x1, w("w_xq"), nt=False, pre="norm", g_pre=_row(small["x_pre_g"][l]), name="xq_proj")
        kv, memn = _mm_fwd(mem, w("w_xkv"), nt=True, pre="norm", g_pre=mem_g, name="xkv_proj")
        ox = _xattn_fwd(qx, kv, name="xattn_fwd")
        xo, x2 = _mm_fwd(ox, w("w_xo"), nt=False, post="norm_res", g_post=_row(small["x_post_g"][l]), xres=x1, name="xattn_out")
        x2 = fwd_hook(l, 2, x2)
        gu, h3 = _mm_fwd(x2, w("w_gate_up"), nt=True, pre="norm", g_pre=_row(small["ffn_pre_g"][l]), name="ffn_in")
        f, x3 = _mm_fwd(gu, w("w_down"), nt=False, pre="swiglu", post="norm_res", g_post=_row(small["ffn_post_g"][l]), xres=x2,
                        tile=256, name="ffn_out")
        saved.append((mix_par, x, z, h1, cat, mix, x1, qx, h2, kv, memn, ox, xo, x2, gu, h3, f))
        x = fwd_hook(l, 3, x3)

    loss, dx = _loss_grad(x, target, name="loss")

    gs = {n: [None] * depth for n in SMALL if n != "mem_norm_g"}
    gb = {n: [None] * depth for n in BIG}
    dmemn = [None] * depth
    for l in reversed(range(depth)):
        w = functools.partial(weight, l)
        mix_par, x0, z, h1, cat, mix, x1, qx, h2, kv, memn, ox, xo, x2, gu, h3, f = saved[l]
        dgu, df, gs["ffn_post_g"][l] = _ffn_out_bwd(f, dx, _row(small["ffn_post_g"][l]), w("w_down"), gu, name="ffn_out_bwd")
        gb["w_down"][l] = _wgrad(gu, df, swiglu=True, name="w_down_grad")
        gb["w_gate_up"][l] = _wgrad(dgu, h3, name="w_gate_up_grad")
        dx2, gs["ffn_pre_g"][l] = _mm_bwd_post(dgu, w("w_gate_up"), x2, _row(small["ffn_pre_g"][l]), dx, nt=False, tile=256, name="ffn_in_bwd")
        dx2 = bwd_hook(l, 1, dx2, gb)
        do, dxo, gs["x_post_g"][l] = _mm_bwd_pre(xo, dx2, _row(small["x_post_g"][l]), w("w_xo"), name="xattn_out_bwd")
        gb["w_xo"][l] = _wgrad(ox, dxo, name="w_xo_grad")
        dqx, dkv = _xattn_bwd(qx, kv, do, name="xattn_bwd")
        gb["w_xq"][l] = _wgrad(h2, dqx, name="w_xq_grad")
        dx1, gs["x_pre_g"][l] = _mm_bwd_post(dqx, w("w_xq"), x1, _row(small["x_pre_g"][l]), dx2, nt=True, name="xq_proj_bwd")
        dx1 = bwd_hook(l, 2, dx1, gb)
        gb["w_xkv"][l] = _wgrad(dkv, memn, name="w_xkv_grad")
        dmemn[l] = _mm_fwd(dkv, w("w_xkv"), nt=False, out_dtype=F32, name="xkv_proj_bwd")[0]
        dcat, dmix, gs["mix_post_g"][l] = _mm_bwd_pre(mix, dx1, _row(small["mix_post_g"][l]), w("w_o"), name="mix_out_bwd")
        gb["w_o"][l] = _wgrad(cat, dmix, name="w_o_grad")
        dz, dvg, dwc, dbf, dbd, dscale, dsink = _mixer_bwd(z, dcat, cos_t, sin_t, *mix_par, name="mixer_bwd")
        gs["gm_v_g"][l], gs["gm_w_s"][l], gs["pool_scale"][l], gs["attn_sinks"][l] = dvg, dwc, dscale, dsink[:, 0]
        gs["gm_b_s"][l] = dbf.reshape(BLOCK, -1, HEAD_DIM).sum(-1).T
        gs["pool_w"][l] = jnp.stack([dbd[HEAD_DIM * g:HEAD_DIM * (g + 1), HEAD_DIM * g:HEAD_DIM * (g + 1)] for g in range(len(POOL_WINDOWS))])
        gb["w_in"][l] = _wgrad(dz, h1, name="w_in_grad")
        dx, gs["mix_pre_g"][l] = _mm_bwd_post(dz, w("w_in"), x0, _row(small["mix_pre_g"][l]), dx1, nt=False, name="in_proj_bwd")
        dx = bwd_hook(l, 3, dx, gb)

    small_grads = {n: jnp.stack(v).reshape(small[n].shape) for n, v in gs.items()}
    small_grads["mem_norm_g"] = _norm_dgain(mem, jnp.stack(dmemn), name="mem_norm_grad").reshape(-1)
    return loss, dx, small_grads, gb


EARLY = ("w_in", "w_o", "w_xq", "w_xkv", "w_xo")
LATE = ("w_gate_up", "w_down")


def _step(a):
    depth = a["mix_pre_g"].shape[0]
    x, mem, target = a["x"][0], a["mem"][0], a["loss_target"][0]
    px, py, pc = _place()
    blocks = jnp.stack([4 * cx + 2 * cy + pc for cx, cy in (_chip(k, px, py) for k in range(4))]).astype(jnp.int32)
    flight = {}

    def laid_out(v, n):
        return v.transpose(0, 2, 1) if n in TRANSPOSED else v

    whole = {(l, n): _place_own(laid_out(a[n], n), l, blocks, name="place_" + n) for l in range(depth) for n in BIG}
    gather = {"first_in": [(0, "w_in")], "first_mid": [(0, n) for n in EARLY[1:]], "first_late": [(0, n) for n in LATE],
              "second_early": [(1, n) for n in EARLY], "second_late": [(1, n) for n in LATE]}
    gather.update({f"layer{l}": [(l, n) for n in BIG] for l in range(2, depth)})
    forward_plan = {(0, 0): [("a", "first_in"), ("a", "first_mid"), ("a", "first_late"), ("a", "second_early"), ("a", "second_late"),
                             ("mid", "first_in"), ("end", "first_in")],
                    (0, 1): [("mid", "first_mid"), ("end", "first_mid")], (0, 2): [("mid", "first_late"), ("end", "first_late")],
                    (0, 3): [("mid", "second_early")], (1, 0): [("end", "second_early"), ("a", "layer2")],
                    (1, 1): [("mid", "second_late")], (1, 2): [("end", "second_late"), ("mid", "layer2")], (1, 3): [("end", "layer2")]}
    for l in range(2, depth):
        forward_plan.update({(l, 0): [("a", f"layer{l + 1}")], (l, 2): [("mid", f"layer{l + 1}")], (l, 3): [("end", f"layer{l + 1}")]})
    if depth == 1:
        gather = {tag: keys for tag, keys in gather.items() if tag.startswith("first")}

    issued = []

    def gather_step(kind, tag, v):
        keys = gather[tag]
        if kind == "a":
            arrays = [whole[k] for k in keys]
            arrays[0] = _tie(arrays[0], issued[-1:])
            send, recv, arrays, token = _start(_gather_copies("a"), arrays, [], 4 * len(keys), name=f"gather_a_{tag}")
            flight[tag] = (send, recv, arrays)
            issued.append(token)
            return [token]
        send, recv, arrays = flight.pop(tag)
        if kind == "mid":
            arrays = _finish(_gather_copies("a"), send, recv, arrays, [v], name=f"gather_a_done_{tag}")
            send, recv, arrays, token = _start(_gather_copies("b"), arrays, [], 3 * len(keys), name=f"gather_b_{tag}")
            flight[tag] = (send, recv, arrays)
            return [token]
        for key, arr in zip(keys, _finish(_gather_copies("b"), send, recv, arrays, [v], name=f"gather_b_done_{tag}")):
            whole[key] = arr
        return []

    def fwd_hook(l, point, v):
        tokens = []
        for kind, tag in forward_plan.get((l, point), []):
            if tag in gather:
                tokens += gather_step(kind, tag, v)
        return _tie(v, tokens)

    mine = {}
    scatter = {f"{part}{l}": [(l, n) for n in names] for l in range(1, depth) for part, names in (("early", EARLY), ("late", LATE))}
    scatter.update({"late0": [(0, n) for n in LATE], "attn0": [(0, "w_xq"), (0, "w_xo")], "rest0": [(0, "w_in"), (0, "w_o"), (0, "w_xkv")]})
    backward_plan = {}
    for l in range(depth):
        backward_plan[(l, 1)] = [("a", f"late{l}")] + ([("mid", f"early{l + 1}")] if l + 1 < depth else [])
        backward_plan[(l, 2)] = [("mid", f"late{l}")] + ([("end", f"early{l + 1}")] if l + 1 < depth else [])
        backward_plan[(l, 3)] = [("end", f"late{l}"), ("a", f"early{l}")]
    backward_plan[(0, 2)] = [("a", "attn0")] + backward_plan[(0, 2)]
    backward_plan[(0, 3)] = [("end", "late0"), ("mid", "attn0"), ("a", "rest0")]

    def scatter_step(kind, tag, v, gb):
        keys = scatter[tag]
        n = len(keys)
        if kind == "a":
            grads = [gb[name][l] for l, name in keys]
            zones = [jax.ShapeDtypeStruct((4, g.shape[0] // N_DEV, g.shape[1]), F32) for g in grads]
            send, recv, arrays, token = _start(_scatter_copies("a", n), grads, zones, 4 * n, name=f"scatter_a_{tag}")
            flight[tag] = (send, recv, arrays, None)
            return [token]
        send, recv, arrays, kept = flight.pop(tag)
        if kind == "mid":
            arrays = _finish(_scatter_copies("a", n), send, recv, arrays, [v], name=f"scatter_a_done_{tag}")
            sums = [_chip_sum(arrays[i], arrays[n + i], blocks, name="chip_sum") for i in range(n)]
            zones = [jax.ShapeDtypeStruct(s[1].shape, BF16) for s in sums]
            send, recv, arrays, token = _start(_scatter_copies("b", n), [s[1] for s in sums], zones, 3 * n, name=f"scatter_b_{tag}")
            flight[tag] = (send, recv, arrays, [s[0] for s in sums])
            return [token]
        arrays = _finish(_scatter_copies("b", n), send, recv, arrays, [v], name=f"scatter_b_done_{tag}")
        for i, key in enumerate(keys):
            mine[key] = _final_sum(kept[i], arrays[n + i], name="final_sum")
        return []

    def bwd_hook(l, point, v, gb):
        tokens = []
        for kind, tag in backward_plan.get((l, point), []):
            tokens += scatter_step(kind, tag, v, gb)
        return _tie(v, tokens)

    loss, dx, small_grads, _ = _local_step(x, mem, a["positions"][0], target, {n: a[n] for n in SMALL}, lambda l, n: whole[(l, n)],
                                           fwd_hook, bwd_hook)

    out = {}

    def update(n, tokens):
        g = _tie(jnp.stack([mine[(l, n)] for l in range(depth)]), tokens)
        moved = _adamw(laid_out(a[n], n), g, laid_out(a["m_" + n], n), laid_out(a["v_" + n], n), name="adamw_" + n)
        for p, v in zip(("grad_", "delta_", "new_m_", "new_v_"), (g, *moved)):
            out[p + n] = laid_out(v, n)
        return moved[0]

    part = _tie(_pack([small_grads[n] for n in SMALL]), [dx])
    parts = _all_gather([part], [[0]], part, name="gather_small_grads")[0].reshape(N_DEV, *part.shape)
    packs = _small_update(parts, *[_pack([a[p + n] for n in SMALL]) for p in ("", "m_", "v_")], name="small_update")
    for p, pack in zip(("grad_", "delta_", "new_m_", "new_v_"), packs):
        for n, v in zip(SMALL, _unpack(pack, [a[n].shape for n in SMALL])):
            out[p + n] = v
    tokens = scatter_step("mid", "rest0", packs[0], None)
    behind = [update(n, tokens) for n in LATE]
    scatter_step("end", "attn0", behind[-1], None)
    scatter_step("end", "rest0", behind[-1], None)
    for n in EARLY:
        update(n, [])

    total = lax.psum(loss[0, 0], ("x", "y", "c"))
    return (total, dx[None], *[out[p + n] for p in ("grad_", "delta_", "new_m_", "new_v_") for n in WEIGHTS])


def kernel(x, mem, positions, mem_norm_g, mix_pre_g, mix_post_g, w_in, gm_v_g, gm_w_s, gm_b_s, pool_w, pool_scale, attn_sinks,
           w_o, x_pre_g, x_post_g, w_xq, w_xkv, w_xo, ffn_pre_g, ffn_post_g, w_gate_up, w_down, loss_target, m_mem_norm_g,
           m_mix_pre_g, m_mix_post_g, m_w_in, m_gm_v_g, m_gm_w_s, m_gm_b_s, m_pool_w, m_pool_scale, m_attn_sinks, m_w_o,
           m_x_pre_g, m_x_post_g, m_w_xq, m_w_xkv, m_w_xo, m_ffn_pre_g, m_ffn_post_g, m_w_gate_up, m_w_down, v_mem_norm_g,
           v_mix_pre_g, v_mix_post_g, v_w_in, v_gm_v_g, v_gm_w_s, v_gm_b_s, v_pool_w, v_pool_scale, v_attn_sinks, v_w_o,
           v_x_pre_g, v_x_post_g, v_w_xq, v_w_xkv, v_w_xo, v_ffn_pre_g, v_ffn_post_g, v_w_gate_up, v_w_down):
    return _step(dict(locals()))
```

```python
import functools
import math

import jax
import jax.numpy as jnp
from jax import lax
from jax.experimental import pallas as pl
from jax.experimental.pallas import tpu as pltpu

F32, BF16 = jnp.float32, jnp.bfloat16
EPS = 1e-6
HEAD_DIM = 64
BLOCK = 128
POOL_WINDOWS = (2, 4, 8, 16)
ATT_Q_HEADS = 8
X_HEADS = 4
ROPE_THETA = 10000.0
ADAM_LR, ADAM_B1, ADAM_B2, ADAM_EPS, ADAM_WD, ADAM_STEP = 0.001, 0.9, 0.999, 1e-08, 0.01, 10
N_DEV = 8
TOKEN_TILE = 512
VMEM_LIMIT_BYTES = 50 * 2**20
NEG = -1e30
MESH = pl.DeviceIdType.MESH

NN = ((1,), (0,))
NT = ((1,), (1,))
TN = ((0,), (0,))


def _dot(a, b, dims):
    return lax.dot_general(a.astype(BF16), b.astype(BF16), (dims, ((), ())), preferred_element_type=F32)


def _params(*sem):
    return pltpu.CompilerParams(dimension_semantics=sem, vmem_limit_bytes=VMEM_LIMIT_BYTES)


def _full(shape):
    return pl.BlockSpec(shape, lambda *_: (0,) * len(shape))


def _lane(shape):
    return lax.broadcasted_iota(jnp.int32, shape, len(shape) - 1)


def _rms_fwd(x, g):
    r = lax.rsqrt(jnp.mean(x * x, axis=-1, keepdims=True) + EPS)
    return x * r * g


def _rms_bwd(x, g, dy):
    r = lax.rsqrt(jnp.mean(x * x, axis=-1, keepdims=True) + EPS)
    xh = x * r
    dg = jnp.sum(dy * xh, axis=0, keepdims=True)
    dxh = dy * g
    dx = r * (dxh - xh * jnp.mean(dxh * xh, axis=-1, keepdims=True))
    return dx, dg


def _silu_parts(gate):
    sg = 1.0 / (1.0 + jnp.exp(-gate))
    return gate * sg, sg


def _swiglu_cols(gu_ref, out_ref, width):
    step = 128 * max(1, (width // 128) // 4)
    for c0 in range(0, width, step):
        c1 = min(width, c0 + step)
        gate = gu_ref[:, c0:c1].astype(F32)
        up = gu_ref[:, width + c0:width + c1].astype(F32)
        out_ref[:, c0:c1] = (_silu_parts(gate)[0] * up).astype(out_ref.dtype)


def _col_tile(n, cap=1408):
    best = n
    for t in range(128, cap + 1, 128):
        if n % t == 0:
            best = t
    return best if n > cap else n


def _mm_fwd(a, w, *, nt, name, pre=None, g_pre=None, post=None, g_post=None, xres=None, g_next=None, out_dtype=BF16, tile=TOKEN_TILE):
    S = a.shape[0]
    n_out, k = (w.shape[0], w.shape[1]) if nt else (w.shape[1], w.shape[0])
    T = min(tile, S)
    tn = n_out if post else _col_tile(n_out)
    grid = (S // T, n_out // tn)

    def body(*refs):
        it = iter(refs)
        a_ref, w_ref = next(it), next(it)
        gpre_ref = next(it) if pre == "norm" else None
        gpost_ref, x_ref = (next(it), next(it)) if post else (None, None)
        gnext_ref = next(it) if g_next is not None else None
        outs = [next(it) for _ in range((2 if post else 1) + (1 if pre == "norm" else 0))]
        hnext_ref = next(it) if g_next is not None else None
        as_ref = next(it) if pre else None
        if pre:
            @pl.when(pl.program_id(1) == 0)
            def _():
                if pre == "norm":
                    h = _rms_fwd(a_ref[...], gpre_ref[...]).astype(BF16)
                    outs[-1][...] = h
                    as_ref[...] = h
                else:
                    _swiglu_cols(a_ref, as_ref, k)
            av = as_ref[...]
        else:
            av = a_ref[...]
        acc = _dot(av, w_ref[...], NT if nt else NN)
        if post:
            outs[0][...] = acc.astype(BF16)
            x_new = x_ref[...] + _rms_fwd(acc, gpost_ref[...])
            outs[1][...] = x_new
            if g_next is not None:
                hnext_ref[...] = _rms_fwd(x_new, gnext_ref[...]).astype(BF16)
        else:
            outs[0][...] = acc.astype(out_dtype)

    in_specs = [pl.BlockSpec((T, a.shape[1]), lambda i, j: (i, 0)),
                pl.BlockSpec((tn, k), lambda i, j: (j, 0)) if nt else pl.BlockSpec((k, tn), lambda i, j: (0, j))]
    args = [a, w]
    if pre == "norm":
        in_specs.append(_full((1, k)))
        args.append(g_pre)
    if post:
        in_specs += [_full((1, n_out)), pl.BlockSpec((T, n_out), lambda i, j: (i, 0))]
        args += [g_post, xres]
    if g_next is not None:
        in_specs.append(_full((1, n_out)))
        args.append(g_next)
    out_block = pl.BlockSpec((T, tn), lambda i, j: (i, j))
    if post:
        out_shape = [jax.ShapeDtypeStruct((S, n_out), BF16), jax.ShapeDtypeStruct((S, n_out), F32)]
        out_specs = [out_block, out_block]
    else:
        out_shape = [jax.ShapeDtypeStruct((S, n_out), out_dtype)]
        out_specs = [out_block]
    if pre == "norm":
        out_shape.append(jax.ShapeDtypeStruct((S, k), BF16))
        out_specs.append(pl.BlockSpec((T, k), lambda i, j: (i, 0)))
    if g_next is not None:
        out_shape.append(jax.ShapeDtypeStruct((S, n_out), BF16))
        out_specs.append(out_block)
    scratch = [pltpu.VMEM((T, k), BF16)] if pre else []
    return pl.pallas_call(body, name=name, grid=grid, in_specs=in_specs, out_specs=out_specs, out_shape=out_shape,
                          scratch_shapes=scratch, compiler_params=_params("parallel", "arbitrary"))(*args)


def _mm_bwd_pre(m, dx, g, w, *, name, tile=TOKEN_TILE):
    S, n = m.shape
    ku = w.shape[0]
    T = min(tile, S)
    tn = _col_tile(ku)
    grid = (S // T, ku // tn)

    def body(m_ref, dx_ref, g_ref, w_ref, du_ref, dm_ref, dg_ref, as_ref):
        i, j = pl.program_id(0), pl.program_id(1)

        @pl.when((i == 0) & (j == 0))
        def _():
            dg_ref[...] = jnp.zeros_like(dg_ref)

        @pl.when(j == 0)
        def _():
            dmv, dg = _rms_bwd(m_ref[...].astype(F32), g_ref[...], dx_ref[...])
            dg_ref[...] += dg
            dmb = dmv.astype(BF16)
            dm_ref[...] = dmb
            as_ref[...] = dmb

        du_ref[...] = _dot(as_ref[...], w_ref[...], NT).astype(BF16)

    row = pl.BlockSpec((T, n), lambda i, j: (i, 0))
    return pl.pallas_call(
        body, name=name, grid=grid,
        in_specs=[row, row, _full((1, n)), pl.BlockSpec((tn, n), lambda i, j: (j, 0))],
        out_specs=[pl.BlockSpec((T, tn), lambda i, j: (i, j)), row, _full((1, n))],
        out_shape=[jax.ShapeDtypeStruct((S, ku), BF16), jax.ShapeDtypeStruct((S, n), BF16), jax.ShapeDtypeStruct((1, n), F32)],
        scratch_shapes=[pltpu.VMEM((T, n), BF16)], compiler_params=_params("arbitrary", "arbitrary"))(m, dx, g, w)


def _mm_bwd_post(dy, w, x, g, dx_in, *, nt, name, tile=TOKEN_TILE):
    S, k = dy.shape
    n = x.shape[1]
    T = min(tile, S)

    def body(dy_ref, w_ref, x_ref, g_ref, dxi_ref, dxo_ref, dg_ref):
        @pl.when(pl.program_id(0) == 0)
        def _():
            dg_ref[...] = jnp.zeros_like(dg_ref)

        dh = _dot(dy_ref[...], w_ref[...], NT if nt else NN)
        dxn, dg = _rms_bwd(x_ref[...], g_ref[...], dh)
        dg_ref[...] += dg
        dxo_ref[...] = dxi_ref[...] + dxn

    row = pl.BlockSpec((T, n), lambda i: (i, 0))
    return pl.pallas_call(
        body, name=name, grid=(S // T,),
        in_specs=[pl.BlockSpec((T, k), lambda i: (i, 0)), _full(w.shape), row, _full((1, n)), row],
        out_specs=[row, _full((1, n))],
        out_shape=[jax.ShapeDtypeStruct((S, n), F32), jax.ShapeDtypeStruct((1, n), F32)],
        compiler_params=_params("arbitrary"))(dy, w, x, g, dx_in)


def _wgrad(a, b, *, name, swiglu=False, tile=TOKEN_TILE):
    S = a.shape[0]
    k = a.shape[1] // 2 if swiglu else a.shape[1]
    n = b.shape[1]
    T = min(tile, S)
    tk = 512 if k % 512 == 0 else 256
    nk = k // tk

    def body(*refs):
        if swiglu:
            gate_ref, up_ref, b_ref, o_ref, act_ref = refs
            for r0 in range(0, S, T):
                gate, up = gate_ref[r0:r0 + T, :].astype(F32), up_ref[r0:r0 + T, :].astype(F32)
                act_ref[r0:r0 + T, :] = (_silu_parts(gate)[0] * up).astype(BF16)
            av = act_ref[...]
        else:
            a_ref, b_ref, o_ref = refs
            av = a_ref[...]
        o_ref[...] = _dot(av, b_ref[...], TN)

    in_specs = [pl.BlockSpec((S, tk), lambda kk: (0, kk))]
    args = [a]
    if swiglu:
        in_specs.append(pl.BlockSpec((S, tk), lambda kk: (0, kk + nk)))
        args.append(a)
    in_specs.append(_full((S, n)))
    args.append(b)
    return pl.pallas_call(body, name=name, grid=(nk,), in_specs=in_specs,
                          out_specs=pl.BlockSpec((tk, n), lambda kk: (kk, 0)),
                          out_shape=jax.ShapeDtypeStruct((k, n), F32),
                          scratch_shapes=[pltpu.VMEM((S, tk), BF16)] if swiglu else [],
                          compiler_params=_params("parallel"))(*args)


def _ffn_out_bwd(m, dx, g, w, gu, *, name, tile=256):
    S, n = m.shape
    f = w.shape[0]
    T = min(tile, S)

    def body(m_ref, dx_ref, g_ref, w_ref, gu_ref, dgu_ref, dm_ref, dg_ref, da_ref):
        @pl.when(pl.program_id(0) == 0)
        def _():
            dg_ref[...] = jnp.zeros_like(dg_ref)

        dmv, dg = _rms_bwd(m_ref[...].astype(F32), g_ref[...], dx_ref[...])
        dg_ref[...] += dg
        dmb = dmv.astype(BF16)
        dm_ref[...] = dmb
        da_ref[...] = _dot(dmb, w_ref[...], NT)
        step = 128 * max(1, (f // 128) // 4)
        for c0 in range(0, f, step):
            c1 = min(f, c0 + step)
            gate = gu_ref[:, c0:c1].astype(F32)
            up = gu_ref[:, f + c0:f + c1].astype(F32)
            da = da_ref[:, c0:c1]
            sl, sg = _silu_parts(gate)
            dgu_ref[:, c0:c1] = (da * up * (sg + sl * (1.0 - sg))).astype(BF16)
            dgu_ref[:, f + c0:f + c1] = (da * sl).astype(BF16)

    row = pl.BlockSpec((T, n), lambda i: (i, 0))
    wide = pl.BlockSpec((T, 2 * f), lambda i: (i, 0))
    return pl.pallas_call(
        body, name=name, grid=(S // T,), in_specs=[row, row, _full((1, n)), _full((f, n)), wide],
        out_specs=[wide, row, _full((1, n))],
        out_shape=[jax.ShapeDtypeStruct((S, 2 * f), BF16), jax.ShapeDtypeStruct((S, n), BF16), jax.ShapeDtypeStruct((1, n), F32)],
        scratch_shapes=[pltpu.VMEM((T, f), F32)], compiler_params=_params("arbitrary"))(m, dx, g, w, gu)


def _loss_grad(y, target, *, name, tile=TOKEN_TILE):
    S, d = y.shape
    T = min(tile, S)
    steps = S // T

    def body(y_ref, t_ref, loss_ref, dy_ref, acc_ref):
        i = pl.program_id(0)

        @pl.when(i == 0)
        def _():
            acc_ref[...] = jnp.zeros_like(acc_ref)

        diff = y_ref[...] - t_ref[...]
        dy_ref[...] = diff * (1.0 / d)
        acc_ref[...] += jnp.sum(diff * diff, axis=0, keepdims=True)

        @pl.when(i == steps - 1)
        def _():
            total = jnp.sum(acc_ref[...], axis=1, keepdims=True) * (0.5 / d)
            loss_ref[...] = jnp.broadcast_to(total, loss_ref.shape)

    row = pl.BlockSpec((T, d), lambda i: (i, 0))
    return pl.pallas_call(
        body, name=name, grid=(steps,), in_specs=[row, row], out_specs=[_full((1, 128)), row],
        out_shape=[jax.ShapeDtypeStruct((1, 128), F32), jax.ShapeDtypeStruct((S, d), F32)],
        scratch_shapes=[pltpu.VMEM((1, d), F32)], compiler_params=_params("arbitrary"))(y, target)


def _norm_dgain(x, dys, *, name):
    def body(x_ref, dy_ref, o_ref):
        xv = x_ref[...]
        dy = dy_ref[0]
        for l in range(1, dys.shape[0]):
            dy = dy + dy_ref[l]
        r = lax.rsqrt(jnp.mean(xv * xv, axis=-1, keepdims=True) + EPS)
        o_ref[...] = jnp.sum(dy * xv * r, axis=0, keepdims=True)

    return pl.pallas_call(body, name=name, out_shape=jax.ShapeDtypeStruct((1, x.shape[1]), F32),
                          compiler_params=pltpu.CompilerParams(vmem_limit_bytes=VMEM_LIMIT_BYTES))(x, dys)


def _xattn_probs(q_ref, kv_ref, h, d, hd):
    q = q_ref[:, h * hd:(h + 1) * hd]
    kh = kv_ref[:, h * hd:(h + 1) * hd]
    vh = kv_ref[:, d + h * hd:d + (h + 1) * hd]
    s = _dot(q, kh, NT) * (hd ** -0.5)
    e = jnp.exp(s - jnp.max(s, axis=-1, keepdims=True))
    return q, kh, vh, e / jnp.sum(e, axis=-1, keepdims=True)


def _xattn_fwd(q, kv, *, name, tile=TOKEN_TILE):
    S, d = q.shape
    mlen = kv.shape[0]
    hd = d // X_HEADS
    T = min(tile, S)

    def body(q_ref, kv_ref, o_ref):
        for h in range(X_HEADS):
            _, _, vh, p = _xattn_probs(q_ref, kv_ref, h, d, hd)
            o_ref[:, h * hd:(h + 1) * hd] = _dot(p, vh, NN).astype(BF16)

    row = pl.BlockSpec((T, d), lambda i: (i, 0))
    return pl.pallas_call(body, name=name, grid=(S // T,), in_specs=[row, _full((mlen, 2 * d))], out_specs=row,
                          out_shape=jax.ShapeDtypeStruct((S, d), BF16), compiler_params=_params("parallel"))(q, kv)


def _xattn_bwd(q, kv, do, *, name, tile=TOKEN_TILE):
    S, d = q.shape
    mlen = kv.shape[0]
    hd = d // X_HEADS
    T = min(tile, S)

    def body(q_ref, kv_ref, do_ref, dq_ref, dkv_ref):
        @pl.when(pl.program_id(0) == 0)
        def _():
            dkv_ref[...] = jnp.zeros_like(dkv_ref)

        for h in range(X_HEADS):
            qh, kh, vh, p = _xattn_probs(q_ref, kv_ref, h, d, hd)
            doh = do_ref[:, h * hd:(h + 1) * hd]
            dp = _dot(doh, vh, NT)
            ds = p * (dp - jnp.sum(p * dp, axis=-1, keepdims=True)) * (hd ** -0.5)
            dq_ref[:, h * hd:(h + 1) * hd] = _dot(ds, kh, NN).astype(BF16)
            dkv_ref[:, h * hd:(h + 1) * hd] += _dot(ds, qh, TN)
            dkv_ref[:, d + h * hd:d + (h + 1) * hd] += _dot(p, doh, TN)

    row = pl.BlockSpec((T, d), lambda i: (i, 0))
    return pl.pallas_call(
        body, name=name, grid=(S // T,), in_specs=[row, _full((mlen, 2 * d)), row],
        out_specs=[row, _full((mlen, 2 * d))],
        out_shape=[jax.ShapeDtypeStruct((S, d), BF16), jax.ShapeDtypeStruct((mlen, 2 * d), F32)],
        compiler_params=_params("arbitrary"))(q, kv, do)


GM0, POOL0, Q0, KV0, Z_END = 0, 512, 768, 1280, 1536


def _rope_tables(pos_col, inv_row, *, name):
    S = pos_col.shape[0]

    def body(p_ref, inv_ref, cos_ref, sin_ref):
        ang = p_ref[...].astype(F32) * inv_ref[...]
        sin = jnp.sin(ang)
        cos_ref[...] = jnp.cos(ang)
        sin_ref[...] = jnp.where(_lane(ang.shape) % HEAD_DIM < HEAD_DIM // 2, -sin, sin)

    T = min(TOKEN_TILE, S)
    return pl.pallas_call(
        body, name=name, grid=(S // T,),
        in_specs=[pl.BlockSpec((T, 1), lambda i: (i, 0)), _full((1, 128))],
        out_specs=[pl.BlockSpec((T, 128), lambda i: (i, 0))] * 2,
        out_shape=[jax.ShapeDtypeStruct((S, 128), F32)] * 2, compiler_params=_params("parallel"))(pos_col, inv_row)


def _swap_halves(x):
    n = x.shape[-1]
    return jnp.where(_lane(x.shape) % HEAD_DIM < HEAD_DIM // 2, pltpu.roll(x, n - HEAD_DIM // 2, 1), pltpu.roll(x, HEAD_DIM // 2, 1))


def _rope(x, cos, sin_s):
    reps = x.shape[-1] // 128
    if reps > 1:
        cos, sin_s = jnp.tile(cos, (1, reps)), jnp.tile(sin_s, (1, reps))
    return x * cos + _swap_halves(x) * sin_s


def _rope_bwd(dy, cos, sin_s):
    reps = dy.shape[-1] // 128
    if reps > 1:
        cos, sin_s = jnp.tile(cos, (1, reps)), jnp.tile(sin_s, (1, reps))
    return dy * cos + _swap_halves(dy * sin_s)


def _gelu_parts(x):
    c = math.sqrt(2.0 / math.pi)
    t = jnp.tanh(c * (x + 0.044715 * x * x * x))
    return 0.5 * x * (1.0 + t), t


def _gelu_grad(x, t):
    c = math.sqrt(2.0 / math.pi)
    return 0.5 * (1.0 + t) + 0.5 * x * (1.0 - t * t) * c * (1.0 + 3.0 * 0.044715 * x * x)


def _group_sum(x):
    gid = _lane(x.shape) // HEAD_DIM
    out = jnp.zeros_like(x)
    for g in range(x.shape[-1] // HEAD_DIM):
        sel = gid == g
        out = jnp.where(sel, jnp.sum(jnp.where(sel, x, 0.0), axis=-1, keepdims=True), out)
    return out


def _low_half(shape):
    return _lane(shape) % 128 < HEAD_DIM


def _tril(w):
    r = lax.broadcasted_iota(jnp.int32, w.shape, 0)
    c = lax.broadcasted_iota(jnp.int32, w.shape, 1)
    return jnp.where(r >= c, w, 0.0)


def _triu(w):
    r = lax.broadcasted_iota(jnp.int32, w.shape, 0)
    c = lax.broadcasted_iota(jnp.int32, w.shape, 1)
    return jnp.where(r <= c, w, 0.0)


def _gating_fwd(zg, vg, wc_ref, bfull):
    ge, th = _gelu_parts(zg)
    u, v = ge[:, :256], ge[:, 256:]
    r = lax.rsqrt(_group_sum(v * v) * (1.0 / HEAD_DIM) + EPS)
    vhat = v * r
    vn = vhat * vg
    low = _low_half((BLOCK, 128))
    blocks = []
    for blk in range(2):
        vb = vn[:, 128 * blk:128 * (blk + 1)]
        m0 = _dot(_tril(wc_ref[2 * blk]), vb, NN)
        m1 = _dot(_tril(wc_ref[2 * blk + 1]), vb, NN)
        blocks.append(jnp.where(low, m0, m1))
    mixed = jnp.concatenate(blocks, axis=1) + bfull
    return u * mixed, (th, u, r, vhat, vn, mixed)


def _band(w, transpose):
    if transpose:
        s2 = lax.broadcasted_iota(jnp.int32, (2 * BLOCK, BLOCK), 0)
        t = lax.broadcasted_iota(jnp.int32, (2 * BLOCK, BLOCK), 1)
    else:
        t = lax.broadcasted_iota(jnp.int32, (BLOCK, 2 * BLOCK), 0)
        s2 = lax.broadcasted_iota(jnp.int32, (BLOCK, 2 * BLOCK), 1)
    d = t - s2 + BLOCK
    return jnp.where((d >= 0) & (d < w), 1.0, 0.0).astype(BF16)


def _exact_dot(b01, x):
    hi = x.astype(BF16)
    lo = (x - hi.astype(F32)).astype(BF16)
    return _dot(b01, hi, NN) + _dot(b01, lo, NN)


def _inv_count(n):
    t = lax.broadcasted_iota(jnp.int32, (BLOCK, 256), 0) + n * BLOCK
    gid = _lane((BLOCK, 256)) // HEAD_DIM
    win = jnp.where(gid == 0, POOL_WINDOWS[0], jnp.where(gid == 1, POOL_WINDOWS[1], jnp.where(gid == 2, POOL_WINDOWS[2], POOL_WINDOWS[3])))
    return 1.0 / jnp.minimum(t + 1, win).astype(F32)


def _pool_fwd(p, pp, n, bd, scale):
    low = _low_half((BLOCK, 128))
    blocks = []
    for blk in range(2):
        both = jnp.concatenate([pp[:, 128 * blk:128 * (blk + 1)], p[:, 128 * blk:128 * (blk + 1)]], axis=0)
        s0 = _exact_dot(_band(POOL_WINDOWS[2 * blk], False), both)
        s1 = _exact_dot(_band(POOL_WINDOWS[2 * blk + 1], False), both)
        blocks.append(jnp.where(low, s0, s1))
    inv = _inv_count(n)
    pooled = jnp.concatenate(blocks, axis=1) * inv - p
    mapped = _dot(pooled, bd, NN)
    return mapped * scale, (inv, pooled, mapped)


def _mixer_fwd(z, cos_t, sin_t, vg, wc, bfull, bd, scale, sinks, *, name):
    S = z.shape[0]
    nb = S // BLOCK

    def body(zc_ref, zpp_ref, zpk_ref, cc_ref, cp_ref, sc_ref, sp_ref, vg_ref, wc_ref, bf_ref, bd_ref, scale_ref, sinks_ref, out_ref):
        n = pl.program_id(0)
        keep = (n > 0).astype(F32)
        a, _ = _gating_fwd(zc_ref[:, GM0:POOL0].astype(F32), vg_ref[...], wc_ref, bf_ref[...])
        out_ref[:, 0:256] = a.astype(BF16)
        b, _ = _pool_fwd(zc_ref[:, POOL0:Q0].astype(F32), zpp_ref[...].astype(F32) * keep, n, bd_ref[...], scale_ref[...])
        out_ref[:, 256:512] = b.astype(BF16)
        cos_c, sin_c = cc_ref[...], sc_ref[...]
        qrot = _rope(zc_ref[:, Q0:KV0].astype(F32), cos_c, sin_c)
        kk = jnp.concatenate([_rope(zpk_ref[:, 0:128].astype(F32), cp_ref[...], sp_ref[...]), _rope(zc_ref[:, KV0:KV0 + 128].astype(F32), cos_c, sin_c)], axis=0)
        vv = jnp.concatenate([zpk_ref[:, 128:256].astype(F32), zc_ref[:, KV0 + 128:Z_END].astype(F32)], axis=0)
        kk, vv = kk.astype(BF16), vv.astype(BF16)
        tall = (ATT_Q_HEADS // 2 * BLOCK, 2 * BLOCK)
        qi = lax.broadcasted_iota(jnp.int32, tall, 0) % BLOCK
        ki = lax.broadcasted_iota(jnp.int32, tall, 1)
        valid = (ki > qi) & (ki <= qi + BLOCK) & (ki >= jnp.where(n > 0, 0, BLOCK))
        low = _low_half((BLOCK, 128))
        q_slabs = [qrot[:, 128 * s:128 * (s + 1)] for s in range(ATT_Q_HEADS // 2)]
        q_rolled = [pltpu.roll(v, HEAD_DIM, 1) for v in q_slabs]
        o_heads = [None] * ATT_Q_HEADS
        for h in range(2):
            lanes = ~low if h else low
            heads = range(4 * h, 4 * h + 4)
            qm = jnp.concatenate([jnp.where(lanes, (q_slabs if j % 2 == h else q_rolled)[j // 2], 0.0) for j in heads], axis=0)
            sc = jnp.where(valid, _dot(qm, kk, NT) * (HEAD_DIM ** -0.5), NEG)
            p = []
            for i, j in enumerate(heads):
                sch, sink = sc[BLOCK * i:BLOCK * (i + 1)], sinks_ref[j]
                mx = jnp.maximum(jnp.max(sch, axis=-1, keepdims=True), sink)
                e = jnp.exp(sch - mx)
                p.append((e / (jnp.sum(e, axis=-1, keepdims=True) + jnp.exp(sink - mx))).astype(BF16))
            o = _dot(jnp.concatenate(p, axis=0), vv, NN)
            for i, j in enumerate(heads):
                oh = o[BLOCK * i:BLOCK * (i + 1)]
                o_heads[j] = oh if j % 2 == h else pltpu.roll(oh, HEAD_DIM, 1)
        for s in range(ATT_Q_HEADS // 2):
            out_ref[:, 512 + 128 * s:512 + 128 * (s + 1)] = jnp.where(low, o_heads[2 * s], o_heads[2 * s + 1]).astype(BF16)

    cur = lambda n: (n, 0)
    prev = lambda n: (jnp.maximum(n - 1, 0), 0)
    in_specs = [
        pl.BlockSpec((BLOCK, Z_END), cur),
        pl.BlockSpec((BLOCK, 256), lambda n: (jnp.maximum(n - 1, 0), POOL0 // 256)),
        pl.BlockSpec((BLOCK, 256), lambda n: (jnp.maximum(n - 1, 0), KV0 // 256)),
        pl.BlockSpec((BLOCK, 128), cur), pl.BlockSpec((BLOCK, 128), prev),
        pl.BlockSpec((BLOCK, 128), cur), pl.BlockSpec((BLOCK, 128), prev),
        _full((1, 256)), _full((4, BLOCK, BLOCK)), _full((BLOCK, 256)), _full((256, 256)), _full((1, 256)),
        pl.BlockSpec(memory_space=pltpu.SMEM),
    ]
    return pl.pallas_call(
        body, name=name, grid=(nb,), in_specs=in_specs, out_specs=pl.BlockSpec((BLOCK, 1024), cur),
        out_shape=jax.ShapeDtypeStruct((S, 1024), BF16),
        compiler_params=_params("parallel"))(z, z, z, cos_t, cos_t, sin_t, sin_t, vg, wc, bfull, bd, scale, sinks)


def _mixer_bwd(z, dcat, cos_t, sin_t, vg, wc, bfull, bd, scale, sinks, *, name):
    S = z.shape[0]
    nb = S // BLOCK

    def body(zc_ref, zpp_ref, zpk_ref, dc_ref, cc_ref, cp_ref, sc_ref, sp_ref, vg_ref, wc_ref, bf_ref, bd_ref, scale_ref, sinks_ref, wct_ref,
             dz_ref, dvg_ref, dwc_ref, dbf_ref, dbd_ref, dscale_ref, dsink_ref, carry_ref):
        n = pl.program_id(0)

        @pl.when(n == 0)
        def _():
            for ref in (dvg_ref, dwc_ref, dbf_ref, dbd_ref, dscale_ref, dsink_ref):
                ref[...] = jnp.zeros_like(ref)

        @pl.when(n < nb)
        def _():
            keep = (n > 0).astype(F32)
            low = _low_half((BLOCK, 128))
            zg = zc_ref[:, GM0:POOL0].astype(F32)
            vg_v = vg_ref[...]
            _, (th, u, r, vhat, vn, mixed) = _gating_fwd(zg, vg_v, wc_ref, bf_ref[...])
            da = dc_ref[:, 0:256].astype(F32)
            dmixed = da * u
            dbf_ref[...] += dmixed
            dvn = []
            for blk in range(2):
                dmb = dmixed[:, 128 * blk:128 * (blk + 1)]
                vb = vn[:, 128 * blk:128 * (blk + 1)]
                acc = None
                for half in range(2):
                    g = 2 * blk + half
                    dmg = jnp.where(low if half == 0 else ~low, dmb, 0.0)
                    dwc_ref[g] += _tril(_dot(dmg, vb, NT))
                    part = _dot(_triu(wct_ref[g]), dmg, NN)
                    acc = part if acc is None else acc + part
                dvn.append(acc)
            dvn = jnp.concatenate(dvn, axis=1)
            dvg_ref[...] += jnp.sum(dvn * vhat, axis=0, keepdims=True)
            dvhat = dvn * vg_v
            dv = r * (dvhat - vhat * (_group_sum(dvhat * vhat) * (1.0 / HEAD_DIM)))
            dge = jnp.concatenate([da * mixed, dv], axis=1)
            own_g = dge * _gelu_grad(zg, th)
            p = zc_ref[:, POOL0:Q0].astype(F32)
            bd_v, scale_v = bd_ref[...], scale_ref[...]
            _, (inv, pooled, mapped) = _pool_fwd(p, zpp_ref[...].astype(F32) * keep, n, bd_v, scale_v)
            db = dc_ref[:, 256:512].astype(F32)
            dscale_ref[...] += jnp.sum(db * mapped, axis=0, keepdims=True)
            dmapped = db * scale_v
            dbd_ref[...] += _dot(pooled, dmapped, TN)
            dpooled = _dot(dmapped, bd_v, NT)
            dps = dpooled * inv
            back = []
            for blk in range(2):
                dpb = dps[:, 128 * blk:128 * (blk + 1)]
                t0 = _exact_dot(_band(POOL_WINDOWS[2 * blk], True), dpb)
                t1 = _exact_dot(_band(POOL_WINDOWS[2 * blk + 1], True), dpb)
                back.append(jnp.where(jnp.concatenate([low, low], axis=0), t0, t1))
            back = jnp.concatenate(back, axis=1)
            halo_p = back[:BLOCK] * keep
            own_p = back[BLOCK:] - dpooled
            cos_c, sin_c, cos_p, sin_p = cc_ref[...], sc_ref[...], cp_ref[...], sp_ref[...]
            qrot = _rope(zc_ref[:, Q0:KV0].astype(F32), cos_c, sin_c)
            kk = jnp.concatenate([_rope(zpk_ref[:, 0:128].astype(F32), cos_p, sin_p), _rope(zc_ref[:, KV0:KV0 + 128].astype(F32), cos_c, sin_c)], axis=0)
            vv = jnp.concatenate([zpk_ref[:, 128:256].astype(F32), zc_ref[:, KV0 + 128:Z_END].astype(F32)], axis=0)
            kt = jnp.transpose(kk).astype(BF16)
            kk, vv = kk.astype(BF16), vv.astype(BF16)
            wide = (2 * BLOCK, ATT_Q_HEADS // 2 * BLOCK)
            ki = lax.broadcasted_iota(jnp.int32, wide, 0)
            qi = lax.broadcasted_iota(jnp.int32, wide, 1) % BLOCK
            valid_t = (ki > qi) & (ki <= qi + BLOCK) & (ki >= jnp.where(n > 0, 0, BLOCK))
            q_slabs = [qrot[:, 128 * s:128 * (s + 1)] for s in range(ATT_Q_HEADS // 2)]
            d_slabs = [dc_ref[:, 512 + 128 * s:512 + 128 * (s + 1)].astype(F32) for s in range(ATT_Q_HEADS // 2)]
            q_rolled = [pltpu.roll(v, HEAD_DIM, 1) for v in q_slabs]
            d_rolled = [pltpu.roll(v, HEAD_DIM, 1) for v in d_slabs]
            dkk = jnp.zeros((2 * BLOCK, 128), F32)
            dvv = jnp.zeros((2 * BLOCK, 128), F32)
            dq_heads = [None] * ATT_Q_HEADS
            for h in range(2):
                lanes = ~low if h else low
                heads = range(4 * h, 4 * h + 4)
                qm = jnp.concatenate([jnp.where(lanes, (q_slabs if j % 2 == h else q_rolled)[j // 2], 0.0) for j in heads], axis=0).astype(BF16)
                dom = jnp.concatenate([jnp.where(lanes, (d_slabs if j % 2 == h else d_rolled)[j // 2], 0.0) for j in heads], axis=0).astype(BF16)
                sink = jnp.concatenate([jnp.full((1, BLOCK), sinks_ref[j], F32) for j in heads], axis=1)
                sc = jnp.where(valid_t, _dot(kk, qm, NT) * (HEAD_DIM ** -0.5), NEG)
                mx = jnp.maximum(jnp.max(sc, axis=0, keepdims=True), sink)
                e = jnp.exp(sc - mx)
                es = jnp.exp(sink - mx)
                inv_den = 1.0 / (jnp.sum(e, axis=0, keepdims=True) + es)
                pt = e * inv_den
                dpt = _dot(vv, dom, NT)
                dvv = dvv + _dot(pt, dom, NN)
                dcol = jnp.sum(pt * dpt, axis=0, keepdims=True)
                dst = (pt * (dpt - dcol) * (HEAD_DIM ** -0.5)).astype(BF16)
                dsink = es * inv_den * dcol
                dkk = dkk + _dot(dst, qm, NN)
                dqt = _dot(kt, dst, NN)
                for i, j in enumerate(heads):
                    part = jnp.sum(dsink[:, BLOCK * i:BLOCK * (i + 1)], axis=1, keepdims=True)
                    dsink_ref[pl.ds(j, 1), :] = dsink_ref[pl.ds(j, 1), :] - jnp.broadcast_to(part, (1, 128))
                    dqh = jnp.transpose(dqt[:, BLOCK * i:BLOCK * (i + 1)])
                    dq_heads[j] = dqh if j % 2 == h else pltpu.roll(dqh, HEAD_DIM, 1)
            dqrot = jnp.concatenate([jnp.where(low, dq_heads[2 * s], dq_heads[2 * s + 1]) for s in range(ATT_Q_HEADS // 2)], axis=1)
            own_q = _rope_bwd(dqrot, cos_c, sin_c)
            own_k = _rope_bwd(dkk[BLOCK:], cos_c, sin_c)
            halo_k = _rope_bwd(dkk[:BLOCK], cos_p, sin_p) * keep
            own_v, halo_v = dvv[BLOCK:], dvv[:BLOCK] * keep

            @pl.when(n > 0)
            def _():
                dz_ref[:, GM0:POOL0] = carry_ref[:, GM0:POOL0].astype(BF16)
                dz_ref[:, POOL0:Q0] = (carry_ref[:, POOL0:Q0] + halo_p).astype(BF16)
                dz_ref[:, Q0:KV0] = carry_ref[:, Q0:KV0].astype(BF16)
                dz_ref[:, KV0:KV0 + 128] = (carry_ref[:, KV0:KV0 + 128] + halo_k).astype(BF16)
                dz_ref[:, KV0 + 128:Z_END] = (carry_ref[:, KV0 + 128:Z_END] + halo_v).astype(BF16)

            carry_ref[:, GM0:POOL0] = own_g
            carry_ref[:, POOL0:Q0] = own_p
            carry_ref[:, Q0:KV0] = own_q
            carry_ref[:, KV0:KV0 + 128] = own_k
            carry_ref[:, KV0 + 128:Z_END] = own_v

        @pl.when(n == nb)
        def _():
            dz_ref[...] = carry_ref[...].astype(BF16)

    cur = lambda n: (jnp.minimum(n, nb - 1), 0)
    prev = lambda n: (jnp.maximum(jnp.minimum(n, nb - 1) - 1, 0), 0)
    in_specs = [
        pl.BlockSpec((BLOCK, Z_END), cur),
        pl.BlockSpec((BLOCK, 256), lambda n: (jnp.maximum(jnp.minimum(n, nb - 1) - 1, 0), POOL0 // 256)),
        pl.BlockSpec((BLOCK, 256), lambda n: (jnp.maximum(jnp.minimum(n, nb - 1) - 1, 0), KV0 // 256)),
        pl.BlockSpec((BLOCK, 1024), cur),
        pl.BlockSpec((BLOCK, 128), cur), pl.BlockSpec((BLOCK, 128), prev),
        pl.BlockSpec((BLOCK, 128), cur), pl.BlockSpec((BLOCK, 128), prev),
        _full((1, 256)), _full((4, BLOCK, BLOCK)), _full((BLOCK, 256)), _full((256, 256)), _full((1, 256)),
        pl.BlockSpec(memory_space=pltpu.SMEM), _full((4, BLOCK, BLOCK)),
    ]
    out_specs = [pl.BlockSpec((BLOCK, Z_END), lambda n: (jnp.maximum(n - 1, 0), 0)),
                 _full((1, 256)), _full((4, BLOCK, BLOCK)), _full((BLOCK, 256)), _full((256, 256)), _full((1, 256)), _full((8, 128))]
    out_shape = [jax.ShapeDtypeStruct((S, Z_END), BF16), jax.ShapeDtypeStruct((1, 256), F32),
                 jax.ShapeDtypeStruct((4, BLOCK, BLOCK), F32), jax.ShapeDtypeStruct((BLOCK, 256), F32),
                 jax.ShapeDtypeStruct((256, 256), F32), jax.ShapeDtypeStruct((1, 256), F32), jax.ShapeDtypeStruct((8, 128), F32)]
    return pl.pallas_call(
        body, name=name, grid=(nb + 1,), in_specs=in_specs, out_specs=out_specs, out_shape=out_shape,
        scratch_shapes=[pltpu.VMEM((BLOCK, Z_END), F32)],
        compiler_params=_params("arbitrary"))(z, z, z, dcat, cos_t, cos_t, sin_t, sin_t, vg, wc, bfull, bd, scale, sinks, jnp.swapaxes(wc, 1, 2))


HBM = pl.BlockSpec(memory_space=pl.ANY)


def _place():
    return lax.axis_index("x"), lax.axis_index("y"), lax.axis_index("c")


def _chip(k, x, y):
    return (1 - x if k & 1 else x, 1 - y if k & 2 else y)


def _block_rows(ref, rows, dev):
    start = pl.multiple_of((4 * dev[0] + 2 * dev[1] + dev[2]) * rows, 8)
    return ref.at[pl.ds(start, rows), :]


_HBM_SPEC = pl.BlockSpec(memory_space=pltpu.HBM)
_SEM_SPEC = pl.BlockSpec(memory_space=pltpu.SEMAPHORE)
_SPLIT_PARAMS = dict(compiler_params=pltpu.CompilerParams(has_side_effects=pltpu.SideEffectType.DATAFLOW_SIDE_EFFECTING))


def _descriptors(copies, refs, send_sems, recv_sems):
    return [pltpu.make_async_remote_copy(src_ref=s, dst_ref=d, send_sem=send_sems.at[i], recv_sem=recv_sems.at[i], device_id=to, device_id_type=MESH)
            for i, (s, d, to) in enumerate(copies(refs))]


def _start(copies, arrays, fresh, n_copies, *, name):
    operands = [pltpu.with_memory_space_constraint(v, pltpu.HBM) for v in (*arrays, *[lax.empty(f.shape, f.dtype) for f in fresh])]
    n = len(operands)

    def body(*refs):
        for dma in _descriptors(copies, refs[:n], refs[n], refs[n + 1]):
            dma.start()
        refs[-1][...] = jnp.zeros_like(refs[-1])

    res = pl.pallas_call(
        body, name=name, in_specs=[_HBM_SPEC] * n,
        out_shape=(pltpu.SemaphoreType.DMA((n_copies,)), pltpu.SemaphoreType.DMA((n_copies,)),
                   *[pltpu.HBM(v.shape, v.dtype) for v in operands], jax.ShapeDtypeStruct((8, 128), F32)),
        out_specs=(_SEM_SPEC, _SEM_SPEC, *[_HBM_SPEC] * n, pl.BlockSpec(memory_space=pltpu.VMEM)),
        input_output_aliases={i: 2 + i for i in range(n)}, **_SPLIT_PARAMS)(*operands)
    return res[0], res[1], list(res[2:2 + n]), res[-1]


def _finish(copies, send_sems, recv_sems, arrays, after, *, name):
    n = len(arrays)

    def body(*refs):
        for dma in _descriptors(copies, refs[:n], refs[n], refs[n + 1]):
            dma.wait_send()
            dma.wait_recv()

    return list(pl.pallas_call(
        body, name=name, in_specs=[_HBM_SPEC] * n + [_SEM_SPEC, _SEM_SPEC] + [HBM] * len(after),
        out_shape=tuple(pltpu.HBM(v.shape, v.dtype) for v in arrays), out_specs=tuple([_HBM_SPEC] * n),
        input_output_aliases={i: i for i in range(n)}, **_SPLIT_PARAMS)(*arrays, send_sems, recv_sems, *after))


def _tie(value, tokens):
    if not tokens:
        return value

    def body(*refs):
        pass

    return pl.pallas_call(body, name="tie", in_specs=[HBM] * (1 + len(tokens)), out_specs=HBM,
                          out_shape=jax.ShapeDtypeStruct(value.shape, value.dtype), input_output_aliases={0: 0})(value, *tokens)


def _gather_copies(stage):
    def copies(refs):
        x, y, c = _place()
        me, sibling = (x, y, c), (x, y, 1 - c)
        out = []
        for ref in refs:
            r = ref.shape[0] // N_DEV
            if stage == "a":
                hops = [(me, sibling)] + [(me, (*_chip(k, x, y), c)) for k in (1, 2, 3)]
            else:
                hops = [((*_chip(k, x, y), c), sibling) for k in (1, 2, 3)]
            for block, to in hops:
                rows = _block_rows(ref, r, block)
                out.append((rows, rows, to))
        return out
    return copies


def _scatter_copies(stage, n):
    def copies(refs):
        x, y, c = _place()
        out = []
        for w in range(n):
            src, dst = refs[w], refs[n + w]
            if stage == "a":
                r = src.shape[0] // N_DEV
                out += [(_block_rows(src, r, (*_chip(k, x, y), 1 - c)), dst.at[k], (x, y, 1 - c)) for k in range(4)]
            else:
                out += [(src.at[k - 1], dst.at[k - 1], (*_chip(k, x, y), c)) for k in (1, 2, 3)]
        return out
    return copies


def _place_own(shards, layer, blocks, *, name, dtype=BF16):
    _, r, cdim = shards.shape

    def body(blocks_ref, s_ref, o_ref):
        o_ref[...] = s_ref[...].astype(dtype)

    return pl.pallas_call(
        body, name=name,
        grid_spec=pltpu.PrefetchScalarGridSpec(
            num_scalar_prefetch=1, grid=(1,), in_specs=[pl.BlockSpec((None, r, cdim), lambda i, blocks: (layer, 0, 0))],
            out_specs=pl.BlockSpec((r, cdim), lambda i, blocks: (blocks[0], 0))),
        out_shape=jax.ShapeDtypeStruct((N_DEV * r, cdim), dtype), compiler_params=_params("arbitrary"))(blocks, shards)


def _chip_sum(grad, got, blocks, *, name):
    _, r, cdim = got.shape
    tr = r if r <= 256 else r // (2 if r % 32 == 0 and r // 2 <= 256 else 4)
    steps = r // tr

    def body(blocks_ref, a0_ref, a1_ref, a2_ref, a3_ref, b_ref, mine_ref, away_ref):
        mine_ref[...] = a0_ref[...] + b_ref[0]
        for k, a_ref in ((1, a1_ref), (2, a2_ref), (3, a3_ref)):
            away_ref[k - 1] = (a_ref[...] + b_ref[k]).astype(BF16)

    own = [pl.BlockSpec((tr, cdim), lambda i, blocks, k=k: (blocks[k] * steps + i, 0)) for k in range(4)]
    return pl.pallas_call(
        body, name=name,
        grid_spec=pltpu.PrefetchScalarGridSpec(
            num_scalar_prefetch=1, grid=(steps,),
            in_specs=own + [pl.BlockSpec((4, tr, cdim), lambda i, blocks: (0, i, 0))],
            out_specs=[pl.BlockSpec((tr, cdim), lambda i, blocks: (i, 0)), pl.BlockSpec((3, tr, cdim), lambda i, blocks: (0, i, 0))]),
        out_shape=[jax.ShapeDtypeStruct((r, cdim), F32), jax.ShapeDtypeStruct((3, r, cdim), BF16)],
        compiler_params=_params("arbitrary"))(blocks, grad, grad, grad, grad, got)


def _final_sum(mine, got, *, name):
    r, cdim = mine.shape
    tr = r if r <= 256 else r // (2 if r % 32 == 0 and r // 2 <= 256 else 4)

    def body(m_ref, g_ref, o_ref):
        o_ref[...] = ((m_ref[...] + g_ref[0].astype(F32)) + g_ref[1].astype(F32)) + g_ref[2].astype(F32)

    return pl.pallas_call(
        body, name=name, grid=(r // tr,),
        in_specs=[pl.BlockSpec((tr, cdim), lambda i: (i, 0)), pl.BlockSpec((3, tr, cdim), lambda i: (0, i, 0))],
        out_specs=pl.BlockSpec((tr, cdim), lambda i: (i, 0)), out_shape=jax.ShapeDtypeStruct((r, cdim), F32),
        compiler_params=_params("parallel"))(mine, got)


def _adamw_math(w, g, m, v):
    m = ADAM_B1 * m + (1.0 - ADAM_B1) * g
    v = ADAM_B2 * v + (1.0 - ADAM_B2) * (g * g)
    m_hat = m / (1.0 - ADAM_B1 ** ADAM_STEP)
    v_hat = v / (1.0 - ADAM_B2 ** ADAM_STEP)
    return -ADAM_LR * (m_hat / (jnp.sqrt(v_hat) + ADAM_EPS) + ADAM_WD * w), m, v


def _adamw(w, g, m, v, *, name):
    L, r, cdim = w.shape
    tr = r if r <= 512 else r // 2
    assert r % tr == 0 and tr % 8 == 0

    def body(w_ref, g_ref, m_ref, v_ref, d_ref, nm_ref, nv_ref):
        d_ref[...], nm_ref[...], nv_ref[...] = _adamw_math(w_ref[...], g_ref[...], m_ref[...], v_ref[...])

    blk = pl.BlockSpec((None, tr, cdim), lambda l, i: (l, i, 0))
    return pl.pallas_call(body, name=name, grid=(L, r // tr), in_specs=[blk] * 4, out_specs=[blk] * 3,
                          out_shape=[jax.ShapeDtypeStruct(w.shape, F32)] * 3, compiler_params=_params("parallel", "parallel"))(w, g, m, v)


def _small_update(parts, w, m, v, *, name):
    _, p, lanes = parts.shape
    tp = p // 2 if p % 16 == 0 else p

    def body(p_ref, w_ref, m_ref, v_ref, g_ref, d_ref, nm_ref, nv_ref):
        g = p_ref[0]
        for i in range(1, N_DEV):
            g = g + p_ref[i]
        g_ref[...] = g
        d_ref[...], nm_ref[...], nv_ref[...] = _adamw_math(w_ref[...], g, m_ref[...], v_ref[...])

    blk = pl.BlockSpec((tp, lanes), lambda i: (i, 0))
    return pl.pallas_call(body, name=name, grid=(p // tp,), in_specs=[pl.BlockSpec((N_DEV, tp, lanes), lambda i: (0, i, 0)), blk, blk, blk],
                          out_specs=[blk] * 4, out_shape=[jax.ShapeDtypeStruct((p, lanes), F32)] * 4,
                          compiler_params=_params("parallel"))(parts, w, m, v)


BIG = ("w_in", "w_o", "w_xq", "w_xkv", "w_xo", "w_gate_up", "w_down")
TRANSPOSED = ("w_in", "w_xkv", "w_gate_up")
SMALL = ("mem_norm_g", "mix_pre_g", "mix_post_g", "gm_v_g", "gm_w_s", "gm_b_s", "pool_w", "pool_scale", "attn_sinks",
         "x_pre_g", "x_post_g", "ffn_pre_g", "ffn_post_g")
WEIGHTS = ("mem_norm_g", "mix_pre_g", "mix_post_g", "w_in", "gm_v_g", "gm_w_s", "gm_b_s", "pool_w", "pool_scale", "attn_sinks", "w_o",
           "x_pre_g", "x_post_g", "w_xq", "w_xkv", "w_xo", "ffn_pre_g", "ffn_post_g", "w_gate_up", "w_down")
PACK_QUANTUM = 8 * 128


def _pack(arrays):
    flat = []
    for a in arrays:
        a = a.reshape(-1).astype(F32)
        flat.append(jnp.pad(a, (0, -a.size % PACK_QUANTUM)))
    return jnp.concatenate(flat).reshape(-1, 128)


def _unpack(pack, shapes):
    out, row = [], 0
    for s in shapes:
        size = math.prod(s)
        rows = (size + (-size % PACK_QUANTUM)) // 128
        out.append(pack[row:row + rows].reshape(-1)[:size].reshape(s))
        row += rows
    return out


def _row(v):
    return v.reshape(1, -1)


def _local_step(x, mem, positions, target, small, weight, fwd_hook=None, bwd_hook=None):
    fwd_hook = fwd_hook or (lambda l, point, v: v)
    bwd_hook = bwd_hook or (lambda l, point, v, gb: v)
    depth = small["mix_pre_g"].shape[0]
    half = HEAD_DIM // 2
    inv = ROPE_THETA ** (-jnp.arange(half, dtype=F32) / half)
    cos_t, sin_t = _rope_tables(positions.reshape(-1, 1), jnp.tile(inv, 128 // half).reshape(1, 128), name="rope_tables")
    mem_g = _row(small["mem_norm_g"])
    saved = []
    for l in range(depth):
        w = functools.partial(weight, l)
        bfull = jnp.repeat(small["gm_b_s"][l].T, HEAD_DIM, axis=1)
        bd = jax.scipy.linalg.block_diag(*[small["pool_w"][l, g] for g in range(len(POOL_WINDOWS))])
        mix_par = (_row(small["gm_v_g"][l]), small["gm_w_s"][l], bfull, bd, _row(small["pool_scale"][l]), small["attn_sinks"][l])
        if l == 0:
            z, h1 = _mm_fwd(fwd_hook(l, 0, x), w("w_in"), nt=True, pre="norm", g_pre=_row(small["mix_pre_g"][l]), name="first_in_proj")
        else:
            h1 = fwd_hook(l, 0, h1)
            z, = _mm_fwd(h1, w("w_in"), nt=True, name="in_proj")
        cat = fwd_hook(l, 1, _mixer_fwd(z, cos_t, sin_t, *mix_par, name="mixer_fwd"))
        mix, x1, h2 = _mm_fwd(cat, w("w_o"), nt=False, post="norm_res", g_post=_row(small["mix_post_g"][l]), xres=x,
                              g_next=_row(small["x_pre_g"][l]), name="mix_out")
        qx, = _mm_fwd(h2, w("w_xq"), nt=False, name="xq_proj")
        kv, memn = _mm_fwd(mem, w("w_xkv"), nt=True, pre="norm", g_pre=mem_g, name="xkv_proj")
        ox = _xattn_fwd(qx, kv, name="xattn_fwd")
        xo, x2, h3 = _mm_fwd(ox, w("w_xo"), nt=False, post="norm_res", g_post=_row(small["x_post_g"][l]), xres=x1,
                             g_next=_row(small["ffn_pre_g"][l]), name="xattn_out")
        h3 = fwd_hook(l, 2, h3)
        gu, = _mm_fwd(h3, w("w_gate_up"), nt=True, name="ffn_in")
        last = l + 1 == depth
        f, x3, *h_next = _mm_fwd(gu, w("w_down"), nt=False, pre="swiglu", post="norm_res", g_post=_row(small["ffn_post_g"][l]), xres=x2,
                                 g_next=None if last else _row(small["mix_pre_g"][l + 1]), tile=256, name="last_ffn_out" if last else "ffn_out")
        saved.append((mix_par, x, z, h1, cat, mix, x1, qx, h2, kv, memn, ox, xo, x2, gu, h3, f))
        if last:
            x = fwd_hook(l, 3, x3)
        else:
            x, h1 = x3, fwd_hook(l, 3, h_next[0])

    loss, dx = _loss_grad(x, target, name="loss")

    gs = {n: [None] * depth for n in SMALL if n != "mem_norm_g"}
    gb = {n: [None] * depth for n in BIG}
    dmemn = [None] * depth
    for l in reversed(range(depth)):
        w = functools.partial(weight, l)
        mix_par, x0, z, h1, cat, mix, x1, qx, h2, kv, memn, ox, xo, x2, gu, h3, f = saved[l]
        dgu, df, gs["ffn_post_g"][l] = _ffn_out_bwd(f, dx, _row(small["ffn_post_g"][l]), w("w_down"), gu, name="ffn_out_bwd")
        gb["w_down"][l] = _wgrad(gu, df, swiglu=True, name="w_down_grad")
        gb["w_gate_up"][l] = _wgrad(dgu, h3, name="w_gate_up_grad")
        dx2, gs["ffn_pre_g"][l] = _mm_bwd_post(dgu, w("w_gate_up"), x2, _row(small["ffn_pre_g"][l]), dx, nt=False, tile=256, name="ffn_in_bwd")
        dx2 = bwd_hook(l, 1, dx2, gb)
        do, dxo, gs["x_post_g"][l] = _mm_bwd_pre(xo, dx2, _row(small["x_post_g"][l]), w("w_xo"), name="xattn_out_bwd")
        gb["w_xo"][l] = _wgrad(ox, dxo, name="w_xo_grad")
        dqx, dkv = _xattn_bwd(qx, kv, do, name="xattn_bwd")
        gb["w_xq"][l] = _wgrad(h2, dqx, name="w_xq_grad")
        dx1, gs["x_pre_g"][l] = _mm_bwd_post(dqx, w("w_xq"), x1, _row(small["x_pre_g"][l]), dx2, nt=True, name="xq_proj_bwd")
        dx1 = bwd_hook(l, 2, dx1, gb)
        gb["w_xkv"][l] = _wgrad(dkv, memn, name="w_xkv_grad")
        dmemn[l] = _mm_fwd(dkv, w("w_xkv"), nt=False, out_dtype=F32, name="xkv_proj_bwd")[0]
        dcat, dmix, gs["mix_post_g"][l] = _mm_bwd_pre(mix, dx1, _row(small["mix_post_g"][l]), w("w_o"), name="mix_out_bwd")
        gb["w_o"][l] = _wgrad(cat, dmix, name="w_o_grad")
        dz, dvg, dwc, dbf, dbd, dscale, dsink = _mixer_bwd(z, dcat, cos_t, sin_t, *mix_par, name="mixer_bwd")
        gs["gm_v_g"][l], gs["gm_w_s"][l], gs["pool_scale"][l], gs["attn_sinks"][l] = dvg, dwc, dscale, dsink[:, 0]
        gs["gm_b_s"][l] = dbf.reshape(BLOCK, -1, HEAD_DIM).sum(-1).T
        gs["pool_w"][l] = jnp.stack([dbd[HEAD_DIM * g:HEAD_DIM * (g + 1), HEAD_DIM * g:HEAD_DIM * (g + 1)] for g in range(len(POOL_WINDOWS))])
        gb["w_in"][l] = _wgrad(dz, h1, name="w_in_grad")
        dx, gs["mix_pre_g"][l] = _mm_bwd_post(dz, w("w_in"), x0, _row(small["mix_pre_g"][l]), dx1, nt=False, name="in_proj_bwd")
        dx = bwd_hook(l, 3, dx, gb)

    small_grads = {n: jnp.stack(v).reshape(small[n].shape) for n, v in gs.items()}
    small_grads["mem_norm_g"] = _norm_dgain(mem, jnp.stack(dmemn), name="mem_norm_grad").reshape(-1)
    return loss, dx, small_grads, gb


EARLY = ("w_in", "w_o", "w_xq", "w_xkv", "w_xo")
LATE = ("w_gate_up", "w_down")


def _step(a):
    depth = a["mix_pre_g"].shape[0]
    x, mem, target = a["x"][0], a["mem"][0], a["loss_target"][0]
    px, py, pc = _place()
    blocks = jnp.stack([4 * cx + 2 * cy + pc for cx, cy in (_chip(k, px, py) for k in range(4))]).astype(jnp.int32)
    flight = {}

    def laid_out(v, n):
        return v.transpose(0, 2, 1) if n in TRANSPOSED else v

    whole = {(l, n): _place_own(laid_out(a[n], n), l, blocks, name="place_" + n) for l in range(depth) for n in BIG}
    gather = {"first_in": [(0, "w_in")], "first_mid": [(0, n) for n in EARLY[1:]], "first_late": [(0, n) for n in LATE],
              "second_early": [(1, n) for n in EARLY], "second_late": [(1, n) for n in LATE]}
    gather.update({f"layer{l}": [(l, n) for n in BIG] for l in range(2, depth)})
    forward_plan = {(0, 0): [("a", "first_in"), ("a", "first_mid"), ("a", "first_late"), ("a", "second_early"), ("a", "second_late"),
                             ("mid", "first_in"), ("end", "first_in")],
                    (0, 1): [("mid", "first_mid"), ("end", "first_mid")], (0, 2): [("mid", "first_late"), ("end", "first_late")],
                    (0, 3): [("mid", "second_early")], (1, 0): [("end", "second_early"), ("a", "layer2")],
                    (1, 1): [("mid", "second_late")], (1, 2): [("end", "second_late"), ("mid", "layer2")], (1, 3): [("end", "layer2")]}
    for l in range(2, depth):
        forward_plan.update({(l, 0): [("a", f"layer{l + 1}")], (l, 2): [("mid", f"layer{l + 1}")], (l, 3): [("end", f"layer{l + 1}")]})
    if depth == 1:
        gather = {tag: keys for tag, keys in gather.items() if tag.startswith("first")}

    issued = []

    def gather_step(kind, tag, v):
        keys = gather[tag]
        if kind == "a":
            arrays = [whole[k] for k in keys]
            arrays[0] = _tie(arrays[0], issued[-1:])
            send, recv, arrays, token = _start(_gather_copies("a"), arrays, [], 4 * len(keys), name=f"gather_a_{tag}")
            flight[tag] = (send, recv, arrays)
            issued.append(token)
            return [token]
        send, recv, arrays = flight.pop(tag)
        if kind == "mid":
            arrays = _finish(_gather_copies("a"), send, recv, arrays, [v], name=f"gather_a_done_{tag}")
            send, recv, arrays, token = _start(_gather_copies("b"), arrays, [], 3 * len(keys), name=f"gather_b_{tag}")
            flight[tag] = (send, recv, arrays)
            return [token]
        for key, arr in zip(keys, _finish(_gather_copies("b"), send, recv, arrays, [v], name=f"gather_b_done_{tag}")):
            whole[key] = arr
        return []

    def fwd_hook(l, point, v):
        tokens = []
        for kind, tag in forward_plan.get((l, point), []):
            if tag in gather:
                tokens += gather_step(kind, tag, v)
        return _tie(v, tokens)

    mine = {}
    scatter = {f"{part}{l}": [(l, n) for n in names] for l in range(1, depth) for part, names in (("early", EARLY), ("late", LATE))}
    scatter.update({"late0": [(0, n) for n in LATE], "attn0": [(0, "w_xq"), (0, "w_xo")], "rest0": [(0, "w_in"), (0, "w_o"), (0, "w_xkv")]})
    backward_plan = {}
    for l in range(depth):
        backward_plan[(l, 1)] = [("a", f"late{l}")] + ([("mid", f"early{l + 1}")] if l + 1 < depth else [])
        backward_plan[(l, 2)] = [("mid", f"late{l}")] + ([("end", f"early{l + 1}")] if l + 1 < depth else [])
        backward_plan[(l, 3)] = [("end", f"late{l}"), ("a", f"early{l}")]
    backward_plan[(0, 2)] = [("a", "attn0")] + backward_plan[(0, 2)]
    backward_plan[(0, 3)] = [("end", "late0"), ("mid", "attn0"), ("a", "rest0")]

    def scatter_step(kind, tag, v, gb):
        keys = scatter[tag]
        n = len(keys)
        if kind == "a":
            grads = [gb[name][l] for l, name in keys]
            zones = [jax.ShapeDtypeStruct((4, g.shape[0] // N_DEV, g.shape[1]), F32) for g in grads]
            send, recv, arrays, token = _start(_scatter_copies("a", n), grads, zones, 4 * n, name=f"scatter_a_{tag}")
            flight[tag] = (send, recv, arrays, None)
            return [token]
        send, recv, arrays, kept = flight.pop(tag)
        if kind == "mid":
            arrays = _finish(_scatter_copies("a", n), send, recv, arrays, [v], name=f"scatter_a_done_{tag}")
            sums = [_chip_sum(arrays[i], arrays[n + i], blocks, name="chip_sum") for i in range(n)]
            zones = [jax.ShapeDtypeStruct(s[1].shape, BF16) for s in sums]
            send, recv, arrays, token = _start(_scatter_copies("b", n), [s[1] for s in sums], zones, 3 * n, name=f"scatter_b_{tag}")
            flight[tag] = (send, recv, arrays, [s[0] for s in sums])
            return [token]
        arrays = _finish(_scatter_copies("b", n), send, recv, arrays, [v], name=f"scatter_b_done_{tag}")
        for i, key in enumerate(keys):
            mine[key] = _final_sum(kept[i], arrays[n + i], name="final_sum")
        return []

    def bwd_hook(l, point, v, gb):
        tokens = []
        for kind, tag in backward_plan.get((l, point), []):
            tokens += scatter_step(kind, tag, v, gb)
        return _tie(v, tokens)

    loss, dx, small_grads, _ = _local_step(x, mem, a["positions"][0], target, {n: a[n] for n in SMALL}, lambda l, n: whole[(l, n)],
                                           fwd_hook, bwd_hook)

    out = {}

    def update(n, tokens):
        g = _tie(jnp.stack([mine[(l, n)] for l in range(depth)]), tokens)
        moved = _adamw(laid_out(a[n], n), g, laid_out(a["m_" + n], n), laid_out(a["v_" + n], n), name="adamw_" + n)
        for p, v in zip(("grad_", "delta_", "new_m_", "new_v_"), (g, *moved)):
            out[p + n] = laid_out(v, n)
        return moved[0]

    part = _tie(_pack([small_grads[n] for n in SMALL]), [dx])
    placed = _place_own(part[None], 0, blocks, dtype=F32, name="place_small_grads")
    send, recv, arrays, token = _start(_gather_copies("a"), [placed], [], 4, name="gather_a_small_grads")
    behind = [update(n, [token]) for n in LATE]
    tokens = scatter_step("mid", "rest0", behind[-1], None)
    arrays = _finish(_gather_copies("a"), send, recv, arrays, behind, name="gather_a_done_small_grads")
    send, recv, arrays, token = _start(_gather_copies("b"), arrays, [], 3, name="gather_b_small_grads")
    scatter_step("end", "attn0", behind[-1], None)
    behind = [update(n, tokens + [token]) for n in ("w_xq", "w_xo")]
    parts = _finish(_gather_copies("b"), send, recv, arrays, behind, name="gather_b_done_small_grads")[0].reshape(N_DEV, *part.shape)
    packs = _small_update(parts, *[_pack([a[p + n] for n in SMALL]) for p in ("", "m_", "v_")], name="small_update")
    for p, pack in zip(("grad_", "delta_", "new_m_", "new_v_"), packs):
        for n, v in zip(SMALL, _unpack(pack, [a[n].shape for n in SMALL])):
            out[p + n] = v
    scatter_step("end", "rest0", packs[0], None)
    for n in ("w_in", "w_o", "w_xkv"):
        update(n, [])

    total = lax.psum(loss[0, 0], ("x", "y", "c"))
    return (total, dx[None], *[out[p + n] for p in ("grad_", "delta_", "new_m_", "new_v_") for n in WEIGHTS])


def kernel(x, mem, positions, mem_norm_g, mix_pre_g, mix_post_g, w_in, gm_v_g, gm_w_s, gm_b_s, pool_w, pool_scale, attn_sinks,
           w_o, x_pre_g, x_post_g, w_xq, w_xkv, w_xo, ffn_pre_g, ffn_post_g, w_gate_up, w_down, loss_target, m_mem_norm_g,
           m_mix_pre_g, m_mix_post_g, m_w_in, m_gm_v_g, m_gm_w_s, m_gm_b_s, m_pool_w, m_pool_scale, m_attn_sinks, m_w_o,
           m_x_pre_g, m_x_post_g, m_w_xq, m_w_xkv, m_w_xo, m_ffn_pre_g, m_ffn_post_g, m_w_gate_up, m_w_down, v_mem_norm_g,
           v_mix_pre_g, v_mix_post_g, v_w_in, v_gm_v_g, v_gm_w_s, v_gm_b_s, v_pool_w, v_pool_scale, v_attn_sinks, v_w_o,
           v_x_pre_g, v_x_post_g, v_w_xq, v_w_xkv, v_w_xo, v_ffn_pre_g, v_ffn_post_g, v_w_gate_up, v_w_down):
    return _step(dict(locals()))
```

```python
import functools
import math

import jax
import jax.numpy as jnp
from jax import lax
from jax.experimental import pallas as pl
from jax.experimental.pallas import tpu as pltpu

F32, BF16 = jnp.float32, jnp.bfloat16
EPS = 1e-6
HEAD_DIM = 64
BLOCK = 128
POOL_WINDOWS = (2, 4, 8, 16)
ATT_Q_HEADS = 8
X_HEADS = 4
ROPE_THETA = 10000.0
ADAM_LR, ADAM_B1, ADAM_B2, ADAM_EPS, ADAM_WD, ADAM_STEP = 0.001, 0.9, 0.999, 1e-08, 0.01, 10
N_DEV = 8
TOKEN_TILE = 512
VMEM_LIMIT_BYTES = 50 * 2**20
NEG = -1e30
MESH = pl.DeviceIdType.MESH

NN = ((1,), (0,))
NT = ((1,), (1,))
TN = ((0,), (0,))


def _dot(a, b, dims):
    return lax.dot_general(a.astype(BF16), b.astype(BF16), (dims, ((), ())), preferred_element_type=F32)


def _params(*sem):
    return pltpu.CompilerParams(dimension_semantics=sem, vmem_limit_bytes=VMEM_LIMIT_BYTES)


def _full(shape):
    return pl.BlockSpec(shape, lambda *_: (0,) * len(shape))


def _lane(shape):
    return lax.broadcasted_iota(jnp.int32, shape, len(shape) - 1)


def _rms_fwd(x, g):
    r = lax.rsqrt(jnp.mean(x * x, axis=-1, keepdims=True) + EPS)
    return x * r * g


def _rms_bwd(x, g, dy):
    r = lax.rsqrt(jnp.mean(x * x, axis=-1, keepdims=True) + EPS)
    xh = x * r
    dg = jnp.sum(dy * xh, axis=0, keepdims=True)
    dxh = dy * g
    dx = r * (dxh - xh * jnp.mean(dxh * xh, axis=-1, keepdims=True))
    return dx, dg


def _silu_parts(gate):
    sg = 1.0 / (1.0 + jnp.exp(-gate))
    return gate * sg, sg


def _swiglu_cols(gu_ref, out_ref, width):
    step = 128 * max(1, (width // 128) // 4)
    for c0 in range(0, width, step):
        c1 = min(width, c0 + step)
        gate = gu_ref[:, c0:c1].astype(F32)
        up = gu_ref[:, width + c0:width + c1].astype(F32)
        out_ref[:, c0:c1] = (_silu_parts(gate)[0] * up).astype(out_ref.dtype)


def _col_tile(n, cap=1408):
    best = n
    for t in range(128, cap + 1, 128):
        if n % t == 0:
            best = t
    return best if n > cap else n


def _mm_fwd(a, w, *, nt, name, pre=None, g_pre=None, post=None, g_post=None, xres=None, g_next=None, out_dtype=BF16, tile=TOKEN_TILE):
    S = a.shape[0]
    n_out, k = (w.shape[0], w.shape[1]) if nt else (w.shape[1], w.shape[0])
    T = min(tile, S)
    tn = n_out if post else _col_tile(n_out)
    grid = (S // T, n_out // tn)

    def body(*refs):
        it = iter(refs)
        a_ref, w_ref = next(it), next(it)
        gpre_ref = next(it) if pre == "norm" else None
        gpost_ref, x_ref = (next(it), next(it)) if post else (None, None)
        gnext_ref = next(it) if g_next is not None else None
        outs = [next(it) for _ in range((2 if post else 1) + (1 if pre == "norm" else 0))]
        hnext_ref = next(it) if g_next is not None else None
        as_ref = next(it) if pre else None
        if pre:
            @pl.when(pl.program_id(1) == 0)
            def _():
                if pre == "norm":
                    h = _rms_fwd(a_ref[...], gpre_ref[...]).astype(BF16)
                    outs[-1][...] = h
                    as_ref[...] = h
                else:
                    _swiglu_cols(a_ref, as_ref, k)
            av = as_ref[...]
        else:
            av = a_ref[...]
        acc = _dot(av, w_ref[...], NT if nt else NN)
        if post:
            outs[0][...] = acc.astype(BF16)
            x_new = x_ref[...] + _rms_fwd(acc, gpost_ref[...])
            outs[1][...] = x_new
            if g_next is not None:
                hnext_ref[...] = _rms_fwd(x_new, gnext_ref[...]).astype(BF16)
        else:
            outs[0][...] = acc.astype(out_dtype)

    in_specs = [pl.BlockSpec((T, a.shape[1]), lambda i, j: (i, 0)),
                pl.BlockSpec((tn, k), lambda i, j: (j, 0)) if nt else pl.BlockSpec((k, tn), lambda i, j: (0, j))]
    args = [a, w]
    if pre == "norm":
        in_specs.append(_full((1, k)))
        args.append(g_pre)
    if post:
        in_specs += [_full((1, n_out)), pl.BlockSpec((T, n_out), lambda i, j: (i, 0))]
        args += [g_post, xres]
    if g_next is not None:
        in_specs.append(_full((1, n_out)))
        args.append(g_next)
    out_block = pl.BlockSpec((T, tn), lambda i, j: (i, j))
    if post:
        out_shape = [jax.ShapeDtypeStruct((S, n_out), BF16), jax.ShapeDtypeStruct((S, n_out), F32)]
        out_specs = [out_block, out_block]
    else:
        out_shape = [jax.ShapeDtypeStruct((S, n_out), out_dtype)]
        out_specs = [out_block]
    if pre == "norm":
        out_shape.append(jax.ShapeDtypeStruct((S, k), BF16))
        out_specs.append(pl.BlockSpec((T, k), lambda i, j: (i, 0)))
    if g_next is not None:
        out_shape.append(jax.ShapeDtypeStruct((S, n_out), BF16))
        out_specs.append(out_block)
    scratch = [pltpu.VMEM((T, k), BF16)] if pre else []
    return pl.pallas_call(body, name=name, grid=grid, in_specs=in_specs, out_specs=out_specs, out_shape=out_shape,
                          scratch_shapes=scratch, compiler_params=_params("parallel", "arbitrary"))(*args)


def _mm_bwd_post(dy, w, x, g, dx_in, *, nt, name, below=None, tile=TOKEN_TILE):
    S, k = dy.shape
    n = x.shape[1]
    T = min(tile, S)

    def body(*refs):
        dy_ref, w_ref, x_ref, g_ref, dxi_ref = refs[:5]
        m_ref, gb_ref = refs[5:7] if below else (None, None)
        dxo_ref, dg_ref = refs[-4:-2] if below else refs[-2:]

        @pl.when(pl.program_id(0) == 0)
        def _():
            dg_ref[...] = jnp.zeros_like(dg_ref)
            if below:
                refs[-1][...] = jnp.zeros_like(refs[-1])

        dh = _dot(dy_ref[...], w_ref[...], NT if nt else NN)
        dxn, dg = _rms_bwd(x_ref[...], g_ref[...], dh)
        dg_ref[...] += dg
        dxo = dxi_ref[...] + dxn
        dxo_ref[...] = dxo
        if below:
            dmv, dgb = _rms_bwd(m_ref[...].astype(F32), gb_ref[...], dxo)
            refs[-2][...] = dmv.astype(BF16)
            refs[-1][...] += dgb

    row = pl.BlockSpec((T, n), lambda i: (i, 0))
    in_specs = [pl.BlockSpec((T, k), lambda i: (i, 0)), _full(w.shape), row, _full((1, n)), row]
    out_specs = [row, _full((1, n))]
    out_shape = [jax.ShapeDtypeStruct((S, n), F32), jax.ShapeDtypeStruct((1, n), F32)]
    if below:
        in_specs += [row, _full((1, n))]
        out_specs += [row, _full((1, n))]
        out_shape += [jax.ShapeDtypeStruct((S, n), BF16), jax.ShapeDtypeStruct((1, n), F32)]
    return pl.pallas_call(body, name=name, grid=(S // T,), in_specs=in_specs, out_specs=out_specs, out_shape=out_shape,
                          compiler_params=_params("arbitrary"))(dy, w, x, g, dx_in, *(below or ()))


def _wgrad(a, b, *, name, swiglu=False, tile=TOKEN_TILE):
    S = a.shape[0]
    k = a.shape[1] // 2 if swiglu else a.shape[1]
    n = b.shape[1]
    T = min(tile, S)
    tk = 512 if k % 512 == 0 else 256
    nk = k // tk

    def body(*refs):
        if swiglu:
            gate_ref, up_ref, b_ref, o_ref, act_ref = refs
            for r0 in range(0, S, T):
                gate, up = gate_ref[r0:r0 + T, :].astype(F32), up_ref[r0:r0 + T, :].astype(F32)
                act_ref[r0:r0 + T, :] = (_silu_parts(gate)[0] * up).astype(BF16)
            av = act_ref[...]
        else:
            a_ref, b_ref, o_ref = refs
            av = a_ref[...]
        o_ref[...] = _dot(av, b_ref[...], TN)

    in_specs = [pl.BlockSpec((S, tk), lambda kk: (0, kk))]
    args = [a]
    if swiglu:
        in_specs.append(pl.BlockSpec((S, tk), lambda kk: (0, kk + nk)))
        args.append(a)
    in_specs.append(_full((S, n)))
    args.append(b)
    return pl.pallas_call(body, name=name, grid=(nk,), in_specs=in_specs,
                          out_specs=pl.BlockSpec((tk, n), lambda kk: (kk, 0)),
                          out_shape=jax.ShapeDtypeStruct((k, n), F32),
                          scratch_shapes=[pltpu.VMEM((S, tk), BF16)] if swiglu else [],
                          compiler_params=_params("parallel"))(*args)


def _ffn_out_bwd(dm, w, gu, *, name, tile=256):
    S, n = dm.shape
    f = w.shape[0]
    T = min(tile, S)

    def body(dm_ref, w_ref, gu_ref, dgu_ref, da_ref):
        da_ref[...] = _dot(dm_ref[...], w_ref[...], NT)
        step = 128 * max(1, (f // 128) // 4)
        for c0 in range(0, f, step):
            c1 = min(f, c0 + step)
            gate = gu_ref[:, c0:c1].astype(F32)
            up = gu_ref[:, f + c0:f + c1].astype(F32)
            da = da_ref[:, c0:c1]
            sl, sg = _silu_parts(gate)
            dgu_ref[:, c0:c1] = (da * up * (sg + sl * (1.0 - sg))).astype(BF16)
            dgu_ref[:, f + c0:f + c1] = (da * sl).astype(BF16)

    row = pl.BlockSpec((T, n), lambda i: (i, 0))
    wide = pl.BlockSpec((T, 2 * f), lambda i: (i, 0))
    return pl.pallas_call(
        body, name=name, grid=(S // T,), in_specs=[row, _full((f, n)), wide], out_specs=wide,
        out_shape=jax.ShapeDtypeStruct((S, 2 * f), BF16),
        scratch_shapes=[pltpu.VMEM((T, f), F32)], compiler_params=_params("parallel"))(dm, w, gu)


def _loss_grad(y, target, m, g, *, name, tile=TOKEN_TILE):
    S, d = y.shape
    T = min(tile, S)
    steps = S // T

    def body(y_ref, t_ref, m_ref, g_ref, loss_ref, dy_ref, dm_ref, dg_ref, acc_ref):
        i = pl.program_id(0)

        @pl.when(i == 0)
        def _():
            acc_ref[...] = jnp.zeros_like(acc_ref)
            dg_ref[...] = jnp.zeros_like(dg_ref)

        diff = y_ref[...] - t_ref[...]
        dy = diff * (1.0 / d)
        dy_ref[...] = dy
        dmv, dg = _rms_bwd(m_ref[...].astype(F32), g_ref[...], dy)
        dm_ref[...] = dmv.astype(BF16)
        dg_ref[...] += dg
        acc_ref[...] += jnp.sum(diff * diff, axis=0, keepdims=True)

        @pl.when(i == steps - 1)
        def _():
            total = jnp.sum(acc_ref[...], axis=1, keepdims=True) * (0.5 / d)
            loss_ref[...] = jnp.broadcast_to(total, loss_ref.shape)

    row = pl.BlockSpec((T, d), lambda i: (i, 0))
    return pl.pallas_call(
        body, name=name, grid=(steps,), in_specs=[row, row, row, _full((1, d))], out_specs=[_full((1, 128)), row, row, _full((1, d))],
        out_shape=[jax.ShapeDtypeStruct((1, 128), F32), jax.ShapeDtypeStruct((S, d), F32), jax.ShapeDtypeStruct((S, d), BF16),
                   jax.ShapeDtypeStruct((1, d), F32)],
        scratch_shapes=[pltpu.VMEM((1, d), F32)], compiler_params=_params("arbitrary"))(y, target, m, g)


def _norm_dgain(x, dys, *, name):
    def body(x_ref, dy_ref, o_ref):
        xv = x_ref[...]
        dy = dy_ref[0]
        for l in range(1, dys.shape[0]):
            dy = dy + dy_ref[l]
        r = lax.rsqrt(jnp.mean(xv * xv, axis=-1, keepdims=True) + EPS)
        o_ref[...] = jnp.sum(dy * xv * r, axis=0, keepdims=True)

    return pl.pallas_call(body, name=name, out_shape=jax.ShapeDtypeStruct((1, x.shape[1]), F32),
                          compiler_params=pltpu.CompilerParams(vmem_limit_bytes=VMEM_LIMIT_BYTES))(x, dys)


def _xattn_probs(q_ref, kv_ref, h, d, hd):
    q = q_ref[:, h * hd:(h + 1) * hd]
    kh = kv_ref[:, h * hd:(h + 1) * hd]
    vh = kv_ref[:, d + h * hd:d + (h + 1) * hd]
    s = _dot(q, kh, NT) * (hd ** -0.5)
    e = jnp.exp(s - jnp.max(s, axis=-1, keepdims=True))
    return q, kh, vh, e / jnp.sum(e, axis=-1, keepdims=True)


def _xattn_fwd(q, kv, *, name, tile=TOKEN_TILE):
    S, d = q.shape
    mlen = kv.shape[0]
    hd = d // X_HEADS
    T = min(tile, S)

    def body(q_ref, kv_ref, o_ref):
        for h in range(X_HEADS):
            _, _, vh, p = _xattn_probs(q_ref, kv_ref, h, d, hd)
            o_ref[:, h * hd:(h + 1) * hd] = _dot(p, vh, NN).astype(BF16)

    row = pl.BlockSpec((T, d), lambda i: (i, 0))
    return pl.pallas_call(body, name=name, grid=(S // T,), in_specs=[row, _full((mlen, 2 * d))], out_specs=row,
                          out_shape=jax.ShapeDtypeStruct((S, d), BF16), compiler_params=_params("parallel"))(q, kv)


def _xattn_bwd(q, kv, do, *, name, tile=TOKEN_TILE):
    S, d = q.shape
    mlen = kv.shape[0]
    hd = d // X_HEADS
    T = min(tile, S)

    def body(q_ref, kv_ref, do_ref, dq_ref, dkv_ref):
        @pl.when(pl.program_id(0) == 0)
        def _():
            dkv_ref[...] = jnp.zeros_like(dkv_ref)

        for h in range(X_HEADS):
            qh, kh, vh, p = _xattn_probs(q_ref, kv_ref, h, d, hd)
            doh = do_ref[:, h * hd:(h + 1) * hd]
            dp = _dot(doh, vh, NT)
            ds = p * (dp - jnp.sum(p * dp, axis=-1, keepdims=True)) * (hd ** -0.5)
            dq_ref[:, h * hd:(h + 1) * hd] = _dot(ds, kh, NN).astype(BF16)
            dkv_ref[:, h * hd:(h + 1) * hd] += _dot(ds, qh, TN)
            dkv_ref[:, d + h * hd:d + (h + 1) * hd] += _dot(p, doh, TN)

    row = pl.BlockSpec((T, d), lambda i: (i, 0))
    return pl.pallas_call(
        body, name=name, grid=(S // T,), in_specs=[row, _full((mlen, 2 * d)), row],
        out_specs=[row, _full((mlen, 2 * d))],
        out_shape=[jax.ShapeDtypeStruct((S, d), BF16), jax.ShapeDtypeStruct((mlen, 2 * d), F32)],
        compiler_params=_params("arbitrary"))(q, kv, do)


GM0, POOL0, Q0, KV0, Z_END = 0, 512, 768, 1280, 1536


def _rope_tables(pos_col, inv_row, *, name):
    S = pos_col.shape[0]

    def body(p_ref, inv_ref, cos_ref, sin_ref):
        ang = p_ref[...].astype(F32) * inv_ref[...]
        sin = jnp.sin(ang)
        cos_ref[...] = jnp.cos(ang)
        sin_ref[...] = jnp.where(_lane(ang.shape) % HEAD_DIM < HEAD_DIM // 2, -sin, sin)

    T = min(TOKEN_TILE, S)
    return pl.pallas_call(
        body, name=name, grid=(S // T,),
        in_specs=[pl.BlockSpec((T, 1), lambda i: (i, 0)), _full((1, 128))],
        out_specs=[pl.BlockSpec((T, 128), lambda i: (i, 0))] * 2,
        out_shape=[jax.ShapeDtypeStruct((S, 128), F32)] * 2, compiler_params=_params("parallel"))(pos_col, inv_row)


def _swap_halves(x):
    n = x.shape[-1]
    return jnp.where(_lane(x.shape) % HEAD_DIM < HEAD_DIM // 2, pltpu.roll(x, n - HEAD_DIM // 2, 1), pltpu.roll(x, HEAD_DIM // 2, 1))


def _rope(x, cos, sin_s):
    reps = x.shape[-1] // 128
    if reps > 1:
        cos, sin_s = jnp.tile(cos, (1, reps)), jnp.tile(sin_s, (1, reps))
    return x * cos + _swap_halves(x) * sin_s


def _rope_bwd(dy, cos, sin_s):
    reps = dy.shape[-1] // 128
    if reps > 1:
        cos, sin_s = jnp.tile(cos, (1, reps)), jnp.tile(sin_s, (1, reps))
    return dy * cos + _swap_halves(dy * sin_s)


def _gelu_parts(x):
    c = math.sqrt(2.0 / math.pi)
    t = jnp.tanh(c * (x + 0.044715 * x * x * x))
    return 0.5 * x * (1.0 + t), t


def _gelu_grad(x, t):
    c = math.sqrt(2.0 / math.pi)
    return 0.5 * (1.0 + t) + 0.5 * x * (1.0 - t * t) * c * (1.0 + 3.0 * 0.044715 * x * x)


def _group_sum(x):
    gid = _lane(x.shape) // HEAD_DIM
    out = jnp.zeros_like(x)
    for g in range(x.shape[-1] // HEAD_DIM):
        sel = gid == g
        out = jnp.where(sel, jnp.sum(jnp.where(sel, x, 0.0), axis=-1, keepdims=True), out)
    return out


def _low_half(shape):
    return _lane(shape) % 128 < HEAD_DIM


def _tril(w):
    r = lax.broadcasted_iota(jnp.int32, w.shape, 0)
    c = lax.broadcasted_iota(jnp.int32, w.shape, 1)
    return jnp.where(r >= c, w, 0.0)


def _triu(w):
    r = lax.broadcasted_iota(jnp.int32, w.shape, 0)
    c = lax.broadcasted_iota(jnp.int32, w.shape, 1)
    return jnp.where(r <= c, w, 0.0)


def _gating_fwd(zg, vg, wc_ref, bfull):
    ge, th = _gelu_parts(zg)
    u, v = ge[:, :256], ge[:, 256:]
    r = lax.rsqrt(_group_sum(v * v) * (1.0 / HEAD_DIM) + EPS)
    vhat = v * r
    vn = vhat * vg
    low = _low_half((BLOCK, 128))
    blocks = []
    for blk in range(2):
        vb = vn[:, 128 * blk:128 * (blk + 1)]
        m0 = _dot(_tril(wc_ref[2 * blk]), vb, NN)
        m1 = _dot(_tril(wc_ref[2 * blk + 1]), vb, NN)
        blocks.append(jnp.where(low, m0, m1))
    mixed = jnp.concatenate(blocks, axis=1) + bfull
    return u * mixed, (th, u, r, vhat, vn, mixed)


def _band(w, transpose):
    if transpose:
        s2 = lax.broadcasted_iota(jnp.int32, (2 * BLOCK, BLOCK), 0)
        t = lax.broadcasted_iota(jnp.int32, (2 * BLOCK, BLOCK), 1)
    else:
        t = lax.broadcasted_iota(jnp.int32, (BLOCK, 2 * BLOCK), 0)
        s2 = lax.broadcasted_iota(jnp.int32, (BLOCK, 2 * BLOCK), 1)
    d = t - s2 + BLOCK
    return jnp.where((d >= 0) & (d < w), 1.0, 0.0).astype(BF16)


def _inv_count(n):
    t = lax.broadcasted_iota(jnp.int32, (BLOCK, 256), 0) + n * BLOCK
    gid = _lane((BLOCK, 256)) // HEAD_DIM
    win = jnp.where(gid == 0, POOL_WINDOWS[0], jnp.where(gid == 1, POOL_WINDOWS[1], jnp.where(gid == 2, POOL_WINDOWS[2], POOL_WINDOWS[3])))
    return 1.0 / jnp.minimum(t + 1, win).astype(F32)


def _pool_fwd(p, pp, n, bd, scale):
    low = _low_half((BLOCK, 128))
    blocks = []
    for blk in range(2):
        both = jnp.concatenate([pp[:, 128 * blk:128 * (blk + 1)], p[:, 128 * blk:128 * (blk + 1)]], axis=0)
        s0 = _dot(_band(POOL_WINDOWS[2 * blk], False), both, NN)
        s1 = _dot(_band(POOL_WINDOWS[2 * blk + 1], False), both, NN)
        blocks.append(jnp.where(low, s0, s1))
    inv = _inv_count(n)
    pooled = jnp.concatenate(blocks, axis=1) * inv - p
    mapped = _dot(pooled, bd, NN)
    return mapped * scale, (inv, pooled, mapped)


def _mixer_fwd(z, cos_t, sin_t, vg, wc, bfull, bd, scale, sinks, *, name):
    S = z.shape[0]
    nb = S // BLOCK

    def body(zc_ref, zpp_ref, zpk_ref, cc_ref, cp_ref, sc_ref, sp_ref, vg_ref, wc_ref, bf_ref, bd_ref, scale_ref, sinks_ref, out_ref):
        n = pl.program_id(0)
        keep = (n > 0).astype(F32)
        a, _ = _gating_fwd(zc_ref[:, GM0:POOL0].astype(F32), vg_ref[...], wc_ref, bf_ref[...])
        out_ref[:, 0:256] = a.astype(BF16)
        b, _ = _pool_fwd(zc_ref[:, POOL0:Q0].astype(F32), zpp_ref[...].astype(F32) * keep, n, bd_ref[...], scale_ref[...])
        out_ref[:, 256:512] = b.astype(BF16)
        cos_c, sin_c = cc_ref[...], sc_ref[...]
        qrot = _rope(zc_ref[:, Q0:KV0].astype(F32), cos_c, sin_c)
        kk = jnp.concatenate([_rope(zpk_ref[:, 0:128].astype(F32), cp_ref[...], sp_ref[...]), _rope(zc_ref[:, KV0:KV0 + 128].astype(F32), cos_c, sin_c)], axis=0)
        vv = jnp.concatenate([zpk_ref[:, 128:256].astype(F32), zc_ref[:, KV0 + 128:Z_END].astype(F32)], axis=0)
        kk, vv = kk.astype(BF16), vv.astype(BF16)
        tall = (ATT_Q_HEADS // 2 * BLOCK, 2 * BLOCK)
        qi = lax.broadcasted_iota(jnp.int32, tall, 0) % BLOCK
        ki = lax.broadcasted_iota(jnp.int32, tall, 1)
        valid = (ki > qi) & (ki <= qi + BLOCK) & (ki >= jnp.where(n > 0, 0, BLOCK))
        low = _low_half((BLOCK, 128))
        q_slabs = [qrot[:, 128 * s:128 * (s + 1)] for s in range(ATT_Q_HEADS // 2)]
        q_rolled = [pltpu.roll(v, HEAD_DIM, 1) for v in q_slabs]
        o_heads = [None] * ATT_Q_HEADS
        for h in range(2):
            lanes = ~low if h else low
            heads = range(4 * h, 4 * h + 4)
            qm = jnp.concatenate([jnp.where(lanes, (q_slabs if j % 2 == h else q_rolled)[j // 2], 0.0) for j in heads], axis=0)
            sc = jnp.where(valid, _dot(qm, kk, NT) * (HEAD_DIM ** -0.5), NEG)
            p = []
            for i, j in enumerate(heads):
                sch, sink = sc[BLOCK * i:BLOCK * (i + 1)], sinks_ref[j]
                mx = jnp.maximum(jnp.max(sch, axis=-1, keepdims=True), sink)
                e = jnp.exp(sch - mx)
                p.append((e / (jnp.sum(e, axis=-1, keepdims=True) + jnp.exp(sink - mx))).astype(BF16))
            o = _dot(jnp.concatenate(p, axis=0), vv, NN)
            for i, j in enumerate(heads):
                oh = o[BLOCK * i:BLOCK * (i + 1)]
                o_heads[j] = oh if j % 2 == h else pltpu.roll(oh, HEAD_DIM, 1)
        for s in range(ATT_Q_HEADS // 2):
            out_ref[:, 512 + 128 * s:512 + 128 * (s + 1)] = jnp.where(low, o_heads[2 * s], o_heads[2 * s + 1]).astype(BF16)

    cur = lambda n: (n, 0)
    prev = lambda n: (jnp.maximum(n - 1, 0), 0)
    in_specs = [
        pl.BlockSpec((BLOCK, Z_END), cur),
        pl.BlockSpec((BLOCK, 256), lambda n: (jnp.maximum(n - 1, 0), POOL0 // 256)),
        pl.BlockSpec((BLOCK, 256), lambda n: (jnp.maximum(n - 1, 0), KV0 // 256)),
        pl.BlockSpec((BLOCK, 128), cur), pl.BlockSpec((BLOCK, 128), prev),
        pl.BlockSpec((BLOCK, 128), cur), pl.BlockSpec((BLOCK, 128), prev),
        _full((1, 256)), _full((4, BLOCK, BLOCK)), _full((BLOCK, 256)), _full((256, 256)), _full((1, 256)),
        pl.BlockSpec(memory_space=pltpu.SMEM),
    ]
    return pl.pallas_call(
        body, name=name, grid=(nb,), in_specs=in_specs, out_specs=pl.BlockSpec((BLOCK, 1024), cur),
        out_shape=jax.ShapeDtypeStruct((S, 1024), BF16),
        compiler_params=_params("parallel"))(z, z, z, cos_t, cos_t, sin_t, sin_t, vg, wc, bfull, bd, scale, sinks)


def _mixer_bwd(z, dcat, cos_t, sin_t, vg, wc, bfull, bd, scale, sinks, *, name):
    S = z.shape[0]
    nb = S // BLOCK

    def body(zc_ref, zpp_ref, zpk_ref, dc_ref, cc_ref, cp_ref, sc_ref, sp_ref, vg_ref, wc_ref, bf_ref, bd_ref, scale_ref, sinks_ref, wct_ref,
             dz_ref, dvg_ref, dwc_ref, dbf_ref, dbd_ref, dscale_ref, dsink_ref, carry_ref):
        n = pl.program_id(0)

        @pl.when(n == 0)
        def _():
            for ref in (dvg_ref, dwc_ref, dbf_ref, dbd_ref, dscale_ref, dsink_ref):
                ref[...] = jnp.zeros_like(ref)

        @pl.when(n < nb)
        def _():
            keep = (n > 0).astype(F32)
            low = _low_half((BLOCK, 128))
            zg = zc_ref[:, GM0:POOL0].astype(F32)
            vg_v = vg_ref[...]
            _, (th, u, r, vhat, vn, mixed) = _gating_fwd(zg, vg_v, wc_ref, bf_ref[...])
            da = dc_ref[:, 0:256].astype(F32)
            dmixed = da * u
            dbf_ref[...] += dmixed
            dvn = []
            for blk in range(2):
                dmb = dmixed[:, 128 * blk:128 * (blk + 1)]
                vb = vn[:, 128 * blk:128 * (blk + 1)]
                acc = None
                for half in range(2):
                    g = 2 * blk + half
                    dmg = jnp.where(low if half == 0 else ~low, dmb, 0.0)
                    dwc_ref[g] += _tril(_dot(dmg, vb, NT))
                    part = _dot(_triu(wct_ref[g]), dmg, NN)
                    acc = part if acc is None else acc + part
                dvn.append(acc)
            dvn = jnp.concatenate(dvn, axis=1)
            dvg_ref[...] += jnp.sum(dvn * vhat, axis=0, keepdims=True)
            dvhat = dvn * vg_v
            dv = r * (dvhat - vhat * (_group_sum(dvhat * vhat) * (1.0 / HEAD_DIM)))
            dge = jnp.concatenate([da * mixed, dv], axis=1)
            own_g = dge * _gelu_grad(zg, th)
            p = zc_ref[:, POOL0:Q0].astype(F32)
            bd_v, scale_v = bd_ref[...], scale_ref[...]
            _, (inv, pooled, mapped) = _pool_fwd(p, zpp_ref[...].astype(F32) * keep, n, bd_v, scale_v)
            db = dc_ref[:, 256:512].astype(F32)
            dscale_ref[...] += jnp.sum(db * mapped, axis=0, keepdims=True)
            dmapped = db * scale_v
            dbd_ref[...] += _dot(pooled, dmapped, TN)
            dpooled = _dot(dmapped, bd_v, NT)
            dps = dpooled * inv
            back = []
            for blk in range(2):
                dpb = dps[:, 128 * blk:128 * (blk + 1)]
                t0 = _dot(_band(POOL_WINDOWS[2 * blk], True), dpb, NN)
                t1 = _dot(_band(POOL_WINDOWS[2 * blk + 1], True), dpb, NN)
                back.append(jnp.where(jnp.concatenate([low, low], axis=0), t0, t1))
            back = jnp.concatenate(back, axis=1)
            halo_p = back[:BLOCK] * keep
            own_p = back[BLOCK:] - dpooled
            cos_c, sin_c, cos_p, sin_p = cc_ref[...], sc_ref[...], cp_ref[...], sp_ref[...]
            qrot = _rope(zc_ref[:, Q0:KV0].astype(F32), cos_c, sin_c)
            kk = jnp.concatenate([_rope(zpk_ref[:, 0:128].astype(F32), cos_p, sin_p), _rope(zc_ref[:, KV0:KV0 + 128].astype(F32), cos_c, sin_c)], axis=0)
            vv = jnp.concatenate([zpk_ref[:, 128:256].astype(F32), zc_ref[:, KV0 + 128:Z_END].astype(F32)], axis=0)
            kt = jnp.transpose(kk).astype(BF16)
            kk, vv = kk.astype(BF16), vv.astype(BF16)
            wide = (2 * BLOCK, ATT_Q_HEADS // 2 * BLOCK)
            ki = lax.broadcasted_iota(jnp.int32, wide, 0)
            qi = lax.broadcasted_iota(jnp.int32, wide, 1) % BLOCK
            valid_t = (ki > qi) & (ki <= qi + BLOCK) & (ki >= jnp.where(n > 0, 0, BLOCK))
            q_slabs = [qrot[:, 128 * s:128 * (s + 1)] for s in range(ATT_Q_HEADS // 2)]
            d_slabs = [dc_ref[:, 512 + 128 * s:512 + 128 * (s + 1)].astype(F32) for s in range(ATT_Q_HEADS // 2)]
            q_rolled = [pltpu.roll(v, HEAD_DIM, 1) for v in q_slabs]
            d_rolled = [pltpu.roll(v, HEAD_DIM, 1) for v in d_slabs]
            dkk = jnp.zeros((2 * BLOCK, 128), F32)
            dvv = jnp.zeros((2 * BLOCK, 128), F32)
            dq_heads = [None] * ATT_Q_HEADS
            for h in range(2):
                lanes = ~low if h else low
                heads = range(4 * h, 4 * h + 4)
                qm = jnp.concatenate([jnp.where(lanes, (q_slabs if j % 2 == h else q_rolled)[j // 2], 0.0) for j in heads], axis=0).astype(BF16)
                dom = jnp.concatenate([jnp.where(lanes, (d_slabs if j % 2 == h else d_rolled)[j // 2], 0.0) for j in heads], axis=0).astype(BF16)
                sink = jnp.concatenate([jnp.full((1, BLOCK), sinks_ref[j], F32) for j in heads], axis=1)
                sc = jnp.where(valid_t, _dot(kk, qm, NT) * (HEAD_DIM ** -0.5), NEG)
                mx = jnp.maximum(jnp.max(sc, axis=0, keepdims=True), sink)
                e = jnp.exp(sc - mx)
                es = jnp.exp(sink - mx)
                inv_den = 1.0 / (jnp.sum(e, axis=0, keepdims=True) + es)
                pt = e * inv_den
                dpt = _dot(vv, dom, NT)
                dvv = dvv + _dot(pt, dom, NN)
                dcol = jnp.sum(pt * dpt, axis=0, keepdims=True)
                dst = (pt * (dpt - dcol) * (HEAD_DIM ** -0.5)).astype(BF16)
                dsink = es * inv_den * dcol
                dkk = dkk + _dot(dst, qm, NN)
                dqt = _dot(kt, dst, NN)
                for i, j in enumerate(heads):
                    part = jnp.sum(dsink[:, BLOCK * i:BLOCK * (i + 1)], axis=1, keepdims=True)
                    dsink_ref[pl.ds(j, 1), :] = dsink_ref[pl.ds(j, 1), :] - jnp.broadcast_to(part, (1, 128))
                    dqh = jnp.transpose(dqt[:, BLOCK * i:BLOCK * (i + 1)])
                    dq_heads[j] = dqh if j % 2 == h else pltpu.roll(dqh, HEAD_DIM, 1)
            dqrot = jnp.concatenate([jnp.where(low, dq_heads[2 * s], dq_heads[2 * s + 1]) for s in range(ATT_Q_HEADS // 2)], axis=1)
            own_q = _rope_bwd(dqrot, cos_c, sin_c)
            own_k = _rope_bwd(dkk[BLOCK:], cos_c, sin_c)
            halo_k = _rope_bwd(dkk[:BLOCK], cos_p, sin_p) * keep
            own_v, halo_v = dvv[BLOCK:], dvv[:BLOCK] * keep

            @pl.when(n > 0)
            def _():
                dz_ref[:, GM0:POOL0] = carry_ref[:, GM0:POOL0].astype(BF16)
                dz_ref[:, POOL0:Q0] = (carry_ref[:, POOL0:Q0] + halo_p).astype(BF16)
                dz_ref[:, Q0:KV0] = carry_ref[:, Q0:KV0].astype(BF16)
                dz_ref[:, KV0:KV0 + 128] = (carry_ref[:, KV0:KV0 + 128] + halo_k).astype(BF16)
                dz_ref[:, KV0 + 128:Z_END] = (carry_ref[:, KV0 + 128:Z_END] + halo_v).astype(BF16)

            carry_ref[:, GM0:POOL0] = own_g
            carry_ref[:, POOL0:Q0] = own_p
            carry_ref[:, Q0:KV0] = own_q
            carry_ref[:, KV0:KV0 + 128] = own_k
            carry_ref[:, KV0 + 128:Z_END] = own_v

        @pl.when(n == nb)
        def _():
            dz_ref[...] = carry_ref[...].astype(BF16)

    cur = lambda n: (jnp.minimum(n, nb - 1), 0)
    prev = lambda n: (jnp.maximum(jnp.minimum(n, nb - 1) - 1, 0), 0)
    in_specs = [
        pl.BlockSpec((BLOCK, Z_END), cur),
        pl.BlockSpec((BLOCK, 256), lambda n: (jnp.maximum(jnp.minimum(n, nb - 1) - 1, 0), POOL0 // 256)),
        pl.BlockSpec((BLOCK, 256), lambda n: (jnp.maximum(jnp.minimum(n, nb - 1) - 1, 0), KV0 // 256)),
        pl.BlockSpec((BLOCK, 1024), cur),
        pl.BlockSpec((BLOCK, 128), cur), pl.BlockSpec((BLOCK, 128), prev),
        pl.BlockSpec((BLOCK, 128), cur), pl.BlockSpec((BLOCK, 128), prev),
        _full((1, 256)), _full((4, BLOCK, BLOCK)), _full((BLOCK, 256)), _full((256, 256)), _full((1, 256)),
        pl.BlockSpec(memory_space=pltpu.SMEM), _full((4, BLOCK, BLOCK)),
    ]
    out_specs = [pl.BlockSpec((BLOCK, Z_END), lambda n: (jnp.maximum(n - 1, 0), 0)),
                 _full((1, 256)), _full((4, BLOCK, BLOCK)), _full((BLOCK, 256)), _full((256, 256)), _full((1, 256)), _full((8, 128))]
    out_shape = [jax.ShapeDtypeStruct((S, Z_END), BF16), jax.ShapeDtypeStruct((1, 256), F32),
                 jax.ShapeDtypeStruct((4, BLOCK, BLOCK), F32), jax.ShapeDtypeStruct((BLOCK, 256), F32),
                 jax.ShapeDtypeStruct((256, 256), F32), jax.ShapeDtypeStruct((1, 256), F32), jax.ShapeDtypeStruct((8, 128), F32)]
    return pl.pallas_call(
        body, name=name, grid=(nb + 1,), in_specs=in_specs, out_specs=out_specs, out_shape=out_shape,
        scratch_shapes=[pltpu.VMEM((BLOCK, Z_END), F32)],
        compiler_params=_params("arbitrary"))(z, z, z, dcat, cos_t, cos_t, sin_t, sin_t, vg, wc, bfull, bd, scale, sinks, jnp.swapaxes(wc, 1, 2))


HBM = pl.BlockSpec(memory_space=pl.ANY)


def _place():
    return lax.axis_index("x"), lax.axis_index("y"), lax.axis_index("c")


def _chip(k, x, y):
    return (1 - x if k & 1 else x, 1 - y if k & 2 else y)


def _block_rows(ref, rows, dev):
    start = pl.multiple_of((4 * dev[0] + 2 * dev[1] + dev[2]) * rows, 8)
    return ref.at[pl.ds(start, rows), :]


_HBM_SPEC = pl.BlockSpec(memory_space=pltpu.HBM)
_SEM_SPEC = pl.BlockSpec(memory_space=pltpu.SEMAPHORE)
_SPLIT_PARAMS = dict(compiler_params=pltpu.CompilerParams(has_side_effects=pltpu.SideEffectType.DATAFLOW_SIDE_EFFECTING))


def _descriptors(copies, refs, send_sems, recv_sems):
    return [pltpu.make_async_remote_copy(src_ref=s, dst_ref=d, send_sem=send_sems.at[i], recv_sem=recv_sems.at[i], device_id=to, device_id_type=MESH)
            for i, (s, d, to) in enumerate(copies(refs))]


def _start(copies, arrays, fresh, n_copies, *, name):
    operands = [pltpu.with_memory_space_constraint(v, pltpu.HBM) for v in (*arrays, *[lax.empty(f.shape, f.dtype) for f in fresh])]
    n = len(operands)

    def body(*refs):
        for dma in _descriptors(copies, refs[:n], refs[n], refs[n + 1]):
            dma.start()
        refs[-1][...] = jnp.zeros_like(refs[-1])

    res = pl.pallas_call(
        body, name=name, in_specs=[_HBM_SPEC] * n,
        out_shape=(pltpu.SemaphoreType.DMA((n_copies,)), pltpu.SemaphoreType.DMA((n_copies,)),
                   *[pltpu.HBM(v.shape, v.dtype) for v in operands], jax.ShapeDtypeStruct((8, 128), F32)),
        out_specs=(_SEM_SPEC, _SEM_SPEC, *[_HBM_SPEC] * n, pl.BlockSpec(memory_space=pltpu.VMEM)),
        input_output_aliases={i: 2 + i for i in range(n)}, **_SPLIT_PARAMS)(*operands)
    return res[0], res[1], list(res[2:2 + n]), res[-1]


def _finish(copies, send_sems, recv_sems, arrays, after, *, name):
    n = len(arrays)

    def body(*refs):
        for dma in _descriptors(copies, refs[:n], refs[n], refs[n + 1]):
            dma.wait_send()
            dma.wait_recv()

    return list(pl.pallas_call(
        body, name=name, in_specs=[_HBM_SPEC] * n + [_SEM_SPEC, _SEM_SPEC] + [HBM] * len(after),
        out_shape=tuple(pltpu.HBM(v.shape, v.dtype) for v in arrays), out_specs=tuple([_HBM_SPEC] * n),
        input_output_aliases={i: i for i in range(n)}, **_SPLIT_PARAMS)(*arrays, send_sems, recv_sems, *after))


def _tie(value, tokens):
    if not tokens:
        return value

    def body(*refs):
        pass

    return pl.pallas_call(body, name="tie", in_specs=[HBM] * (1 + len(tokens)), out_specs=HBM,
                          out_shape=jax.ShapeDtypeStruct(value.shape, value.dtype), input_output_aliases={0: 0})(value, *tokens)


def _gather_copies(stage):
    def copies(refs):
        x, y, c = _place()
        me, sibling = (x, y, c), (x, y, 1 - c)
        out = []
        for ref in refs:
            r = ref.shape[0] // N_DEV
            if stage == "a":
                hops = [(me, sibling)] + [(me, (*_chip(k, x, y), c)) for k in (1, 2, 3)]
            else:
                hops = [((*_chip(k, x, y), c), sibling) for k in (1, 2, 3)]
            for block, to in hops:
                rows = _block_rows(ref, r, block)
                out.append((rows, rows, to))
        return out
    return copies


def _scatter_copies(stage, n):
    def copies(refs):
        x, y, c = _place()
        out = []
        for w in range(n):
            src, dst = refs[w], refs[n + w]
            if stage == "a":
                r = src.shape[0] // N_DEV
                out += [(_block_rows(src, r, (*_chip(k, x, y), 1 - c)), dst.at[k], (x, y, 1 - c)) for k in range(4)]
            else:
                out += [(src.at[k - 1], dst.at[k - 1], (*_chip(k, x, y), c)) for k in (1, 2, 3)]
        return out
    return copies


def _place_own(shards, layer, blocks, *, name, dtype=BF16):
    _, r, cdim = shards.shape

    def body(blocks_ref, s_ref, o_ref):
        o_ref[...] = s_ref[...].astype(dtype)

    return pl.pallas_call(
        body, name=name,
        grid_spec=pltpu.PrefetchScalarGridSpec(
            num_scalar_prefetch=1, grid=(1,), in_specs=[pl.BlockSpec((None, r, cdim), lambda i, blocks: (layer, 0, 0))],
            out_specs=pl.BlockSpec((r, cdim), lambda i, blocks: (blocks[0], 0))),
        out_shape=jax.ShapeDtypeStruct((N_DEV * r, cdim), dtype), compiler_params=_params("arbitrary"))(blocks, shards)


def _chip_sum(grad, got, blocks, *, name):
    _, r, cdim = got.shape
    tr = r if r <= 256 else r // (2 if r % 32 == 0 and r // 2 <= 256 else 4)
    steps = r // tr

    def body(blocks_ref, a0_ref, a1_ref, a2_ref, a3_ref, b_ref, mine_ref, away_ref):
        mine_ref[...] = a0_ref[...] + b_ref[0]
        for k, a_ref in ((1, a1_ref), (2, a2_ref), (3, a3_ref)):
            away_ref[k - 1] = (a_ref[...] + b_ref[k]).astype(BF16)

    own = [pl.BlockSpec((tr, cdim), lambda i, blocks, k=k: (blocks[k] * steps + i, 0)) for k in range(4)]
    return pl.pallas_call(
        body, name=name,
        grid_spec=pltpu.PrefetchScalarGridSpec(
            num_scalar_prefetch=1, grid=(steps,),
            in_specs=own + [pl.BlockSpec((4, tr, cdim), lambda i, blocks: (0, i, 0))],
            out_specs=[pl.BlockSpec((tr, cdim), lambda i, blocks: (i, 0)), pl.BlockSpec((3, tr, cdim), lambda i, blocks: (0, i, 0))]),
        out_shape=[jax.ShapeDtypeStruct((r, cdim), F32), jax.ShapeDtypeStruct((3, r, cdim), BF16)],
        compiler_params=_params("arbitrary"))(blocks, grad, grad, grad, grad, got)


def _final_sum(mine, got, *, name):
    r, cdim = mine.shape
    tr = r if r <= 256 else r // (2 if r % 32 == 0 and r // 2 <= 256 else 4)

    def body(m_ref, g_ref, o_ref):
        o_ref[...] = ((m_ref[...] + g_ref[0].astype(F32)) + g_ref[1].astype(F32)) + g_ref[2].astype(F32)

    return pl.pallas_call(
        body, name=name, grid=(r // tr,),
        in_specs=[pl.BlockSpec((tr, cdim), lambda i: (i, 0)), pl.BlockSpec((3, tr, cdim), lambda i: (0, i, 0))],
        out_specs=pl.BlockSpec((tr, cdim), lambda i: (i, 0)), out_shape=jax.ShapeDtypeStruct((r, cdim), F32),
        compiler_params=_params("parallel"))(mine, got)


def _adamw_math(w, g, m, v):
    m = ADAM_B1 * m + (1.0 - ADAM_B1) * g
    v = ADAM_B2 * v + (1.0 - ADAM_B2) * (g * g)
    m_hat = m / (1.0 - ADAM_B1 ** ADAM_STEP)
    v_hat = v / (1.0 - ADAM_B2 ** ADAM_STEP)
    return -ADAM_LR * (m_hat / (jnp.sqrt(v_hat) + ADAM_EPS) + ADAM_WD * w), m, v


def _adamw(w, g, m, v, *, name):
    L, r, cdim = w.shape
    tr = r if r <= 512 else r // 2
    assert r % tr == 0 and tr % 8 == 0

    def body(w_ref, g_ref, m_ref, v_ref, d_ref, nm_ref, nv_ref):
        d_ref[...], nm_ref[...], nv_ref[...] = _adamw_math(w_ref[...], g_ref[...], m_ref[...], v_ref[...])

    blk = pl.BlockSpec((None, tr, cdim), lambda l, i: (l, i, 0))
    return pl.pallas_call(body, name=name, grid=(L, r // tr), in_specs=[blk] * 4, out_specs=[blk] * 3,
                          out_shape=[jax.ShapeDtypeStruct(w.shape, F32)] * 3, compiler_params=_params("parallel", "parallel"))(w, g, m, v)


def _small_update(parts, w, m, v, *, name):
    _, p, lanes = parts.shape
    tp = p // 2 if p % 16 == 0 else p

    def body(p_ref, w_ref, m_ref, v_ref, g_ref, d_ref, nm_ref, nv_ref):
        g = p_ref[0]
        for i in range(1, N_DEV):
            g = g + p_ref[i]
        g_ref[...] = g
        d_ref[...], nm_ref[...], nv_ref[...] = _adamw_math(w_ref[...], g, m_ref[...], v_ref[...])

    blk = pl.BlockSpec((tp, lanes), lambda i: (i, 0))
    return pl.pallas_call(body, name=name, grid=(p // tp,), in_specs=[pl.BlockSpec((N_DEV, tp, lanes), lambda i: (0, i, 0)), blk, blk, blk],
                          out_specs=[blk] * 4, out_shape=[jax.ShapeDtypeStruct((p, lanes), F32)] * 4,
                          compiler_params=_params("parallel"))(parts, w, m, v)


BIG = ("w_in", "w_o", "w_xq", "w_xkv", "w_xo", "w_gate_up", "w_down")
TRANSPOSED = ("w_in", "w_xkv", "w_gate_up")
SMALL = ("mem_norm_g", "mix_pre_g", "mix_post_g", "gm_v_g", "gm_w_s", "gm_b_s", "pool_w", "pool_scale", "attn_sinks",
         "x_pre_g", "x_post_g", "ffn_pre_g", "ffn_post_g")
WEIGHTS = ("mem_norm_g", "mix_pre_g", "mix_post_g", "w_in", "gm_v_g", "gm_w_s", "gm_b_s", "pool_w", "pool_scale", "attn_sinks", "w_o",
           "x_pre_g", "x_post_g", "w_xq", "w_xkv", "w_xo", "ffn_pre_g", "ffn_post_g", "w_gate_up", "w_down")
PACK_QUANTUM = 8 * 128


def _pack(arrays):
    flat = []
    for a in arrays:
        a = a.reshape(-1).astype(F32)
        flat.append(jnp.pad(a, (0, -a.size % PACK_QUANTUM)))
    return jnp.concatenate(flat).reshape(-1, 128)


def _unpack(pack, shapes):
    out, row = [], 0
    for s in shapes:
        size = math.prod(s)
        rows = (size + (-size % PACK_QUANTUM)) // 128
        out.append(pack[row:row + rows].reshape(-1)[:size].reshape(s))
        row += rows
    return out


def _row(v):
    return v.reshape(1, -1)


def _local_step(x, mem, positions, target, small, weight, fwd_hook=None, bwd_hook=None):
    fwd_hook = fwd_hook or (lambda l, point, v: v)
    bwd_hook = bwd_hook or (lambda l, point, v, gb: v)
    depth = small["mix_pre_g"].shape[0]
    half = HEAD_DIM // 2
    inv = ROPE_THETA ** (-jnp.arange(half, dtype=F32) / half)
    cos_t, sin_t = _rope_tables(positions.reshape(-1, 1), jnp.tile(inv, 128 // half).reshape(1, 128), name="rope_tables")
    mem_g = _row(small["mem_norm_g"])
    saved = []
    for l in range(depth):
        w = functools.partial(weight, l)
        bfull = jnp.repeat(small["gm_b_s"][l].T, HEAD_DIM, axis=1)
        bd = jax.scipy.linalg.block_diag(*[small["pool_w"][l, g] for g in range(len(POOL_WINDOWS))])
        mix_par = (_row(small["gm_v_g"][l]), small["gm_w_s"][l], bfull, bd, _row(small["pool_scale"][l]), small["attn_sinks"][l])
        if l == 0:
            z, h1 = _mm_fwd(fwd_hook(l, 0, x), w("w_in"), nt=True, pre="norm", g_pre=_row(small["mix_pre_g"][l]), name="first_in_proj")
        else:
            h1 = fwd_hook(l, 0, h1)
            z, = _mm_fwd(h1, w("w_in"), nt=True, name="in_proj")
        cat = fwd_hook(l, 1, _mixer_fwd(z, cos_t, sin_t, *mix_par, name="mixer_fwd"))
        mix, x1, h2 = _mm_fwd(cat, w("w_o"), nt=False, post="norm_res", g_post=_row(small["mix_post_g"][l]), xres=x,
                              g_next=_row(small["x_pre_g"][l]), name="mix_out")
        qx, = _mm_fwd(h2, w("w_xq"), nt=False, name="xq_proj")
        kv, memn = _mm_fwd(mem, w("w_xkv"), nt=True, pre="norm", g_pre=mem_g, name="xkv_proj")
        ox = _xattn_fwd(qx, kv, name="xattn_fwd")
        xo, x2, h3 = _mm_fwd(ox, w("w_xo"), nt=False, post="norm_res", g_post=_row(small["x_post_g"][l]), xres=x1,
                             g_next=_row(small["ffn_pre_g"][l]), name="xattn_out")
        h3 = fwd_hook(l, 2, h3)
        gu, = _mm_fwd(h3, w("w_gate_up"), nt=True, name="ffn_in")
        last = l + 1 == depth
        f, x3, *h_next = _mm_fwd(gu, w("w_down"), nt=False, pre="swiglu", post="norm_res", g_post=_row(small["ffn_post_g"][l]), xres=x2,
                                 g_next=None if last else _row(small["mix_pre_g"][l + 1]), tile=256, name="last_ffn_out" if last else "ffn_out")
        saved.append((mix_par, x, z, h1, cat, mix, x1, qx, h2, kv, memn, ox, xo, x2, gu, h3, f))
        if last:
            x = fwd_hook(l, 3, x3)
        else:
            x, h1 = x3, fwd_hook(l, 3, h_next[0])

    gs = {n: [None] * depth for n in SMALL if n != "mem_norm_g"}
    gb = {n: [None] * depth for n in BIG}
    dmemn = [None] * depth
    loss, dx, df, gs["ffn_post_g"][depth - 1] = _loss_grad(x, target, saved[-1][-1], _row(small["ffn_post_g"][depth - 1]), name="loss")
    for l in reversed(range(depth)):
        w = functools.partial(weight, l)
        mix_par, x0, z, h1, cat, mix, x1, qx, h2, kv, memn, ox, xo, x2, gu, h3, f = saved[l]
        dgu = _ffn_out_bwd(df, w("w_down"), gu, name="ffn_out_bwd")
        gb["w_down"][l] = _wgrad(gu, df, swiglu=True, name="w_down_grad")
        gb["w_gate_up"][l] = _wgrad(dgu, h3, name="w_gate_up_grad")
        dx2, gs["ffn_pre_g"][l], dxo, gs["x_post_g"][l] = _mm_bwd_post(
            dgu, w("w_gate_up"), x2, _row(small["ffn_pre_g"][l]), dx, nt=False, below=(xo, _row(small["x_post_g"][l])), tile=256, name="ffn_in_bwd")
        dxo = bwd_hook(l, 1, dxo, gb)
        do, = _mm_fwd(dxo, w("w_xo"), nt=True, name="xattn_out_bwd")
        gb["w_xo"][l] = _wgrad(ox, dxo, name="w_xo_grad")
        dqx, dkv = _xattn_bwd(qx, kv, do, name="xattn_bwd")
        gb["w_xq"][l] = _wgrad(h2, dqx, name="w_xq_grad")
        dx1, gs["x_pre_g"][l], dmix, gs["mix_post_g"][l] = _mm_bwd_post(
            dqx, w("w_xq"), x1, _row(small["x_pre_g"][l]), dx2, nt=True, below=(mix, _row(small["mix_post_g"][l])), name="xq_proj_bwd")
        dmix = bwd_hook(l, 2, dmix, gb)
        gb["w_xkv"][l] = _wgrad(dkv, memn, name="w_xkv_grad")
        dmemn[l] = _mm_fwd(dkv, w("w_xkv"), nt=False, out_dtype=F32, name="xkv_proj_bwd")[0]
        dcat, = _mm_fwd(dmix, w("w_o"), nt=True, name="mix_out_bwd")
        gb["w_o"][l] = _wgrad(cat, dmix, name="w_o_grad")
        dz, dvg, dwc, dbf, dbd, dscale, dsink = _mixer_bwd(z, dcat, cos_t, sin_t, *mix_par, name="mixer_bwd")
        gs["gm_v_g"][l], gs["gm_w_s"][l], gs["pool_scale"][l], gs["attn_sinks"][l] = dvg, dwc, dscale, dsink[:, 0]
        gs["gm_b_s"][l] = dbf.reshape(BLOCK, -1, HEAD_DIM).sum(-1).T
        gs["pool_w"][l] = jnp.stack([dbd[HEAD_DIM * g:HEAD_DIM * (g + 1), HEAD_DIM * g:HEAD_DIM * (g + 1)] for g in range(len(POOL_WINDOWS))])
        gb["w_in"][l] = _wgrad(dz, h1, name="w_in_grad")
        if l == 0:
            dx, gs["mix_pre_g"][l] = _mm_bwd_post(dz, w("w_in"), x0, _row(small["mix_pre_g"][l]), dx1, nt=False, name="first_in_proj_bwd")
            dx = bwd_hook(l, 3, dx, gb)
        else:
            dx, gs["mix_pre_g"][l], df, gs["ffn_post_g"][l - 1] = _mm_bwd_post(
                dz, w("w_in"), x0, _row(small["mix_pre_g"][l]), dx1, nt=False, below=(saved[l - 1][-1], _row(small["ffn_post_g"][l - 1])),
                name="in_proj_bwd")
            df = bwd_hook(l, 3, df, gb)

    small_grads = {n: jnp.stack(v).reshape(small[n].shape) for n, v in gs.items()}
    small_grads["mem_norm_g"] = _norm_dgain(mem, jnp.stack(dmemn), name="mem_norm_grad").reshape(-1)
    return loss, dx, small_grads, gb


EARLY = ("w_in", "w_o", "w_xq", "w_xkv", "w_xo")
LATE = ("w_gate_up", "w_down")


def _step(a):
    depth = a["mix_pre_g"].shape[0]
    x, mem, target = a["x"][0], a["mem"][0], a["loss_target"][0]
    px, py, pc = _place()
    blocks = jnp.stack([4 * cx + 2 * cy + pc for cx, cy in (_chip(k, px, py) for k in range(4))]).astype(jnp.int32)
    flight = {}

    def laid_out(v, n):
        return v.transpose(0, 2, 1) if n in TRANSPOSED else v

    whole = {(l, n): _place_own(laid_out(a[n], n), l, blocks, name="place_" + n) for l in range(depth) for n in BIG}
    gather = {"first_in": [(0, "w_in")], "first_mid": [(0, n) for n in EARLY[1:]], "first_late": [(0, n) for n in LATE],
              "second_early": [(1, n) for n in EARLY], "second_late": [(1, n) for n in LATE]}
    gather.update({f"layer{l}": [(l, n) for n in BIG] for l in range(2, depth)})
    forward_plan = {(0, 0): [("a", "first_in"), ("a", "first_mid"), ("a", "first_late"), ("a", "second_early"), ("a", "second_late"),
                             ("mid", "first_in"), ("end", "first_in")],
                    (0, 1): [("mid", "first_mid"), ("end", "first_mid")], (0, 2): [("mid", "first_late"), ("end", "first_late")],
                    (0, 3): [("mid", "second_early")], (1, 0): [("end", "second_early"), ("a", "layer2")],
                    (1, 1): [("mid", "second_late")], (1, 2): [("end", "second_late"), ("mid", "layer2")], (1, 3): [("end", "layer2")]}
    for l in range(2, depth):
        forward_plan.update({(l, 0): [("a", f"layer{l + 1}")], (l, 2): [("mid", f"layer{l + 1}")], (l, 3): [("end", f"layer{l + 1}")]})
    if depth == 1:
        gather = {tag: keys for tag, keys in gather.items() if tag.startswith("first")}

    issued = []

    def gather_step(kind, tag, v):
        keys = gather[tag]
        if kind == "a":
            arrays = [whole[k] for k in keys]
            arrays[0] = _tie(arrays[0], issued[-1:])
            send, recv, arrays, token = _start(_gather_copies("a"), arrays, [], 4 * len(keys), name=f"gather_a_{tag}")
            flight[tag] = (send, recv, arrays)
            issued.append(token)
            return [token]
        send, recv, arrays = flight.pop(tag)
        if kind == "mid":
            arrays = _finish(_gather_copies("a"), send, recv, arrays, [v], name=f"gather_a_done_{tag}")
            send, recv, arrays, token = _start(_gather_copies("b"), arrays, [], 3 * len(keys), name=f"gather_b_{tag}")
            flight[tag] = (send, recv, arrays)
            return [token]
        for key, arr in zip(keys, _finish(_gather_copies("b"), send, recv, arrays, [v], name=f"gather_b_done_{tag}")):
            whole[key] = arr
        return []

    def fwd_hook(l, point, v):
        tokens = []
        for kind, tag in forward_plan.get((l, point), []):
            if tag in gather:
                tokens += gather_step(kind, tag, v)
        return _tie(v, tokens)

    mine = {}
    scatter = {f"{part}{l}": [(l, n) for n in names] for l in range(1, depth) for part, names in (("early", EARLY), ("late", LATE))}
    scatter.update({"late0": [(0, n) for n in LATE], "attn0": [(0, "w_xq"), (0, "w_xo")], "rest0": [(0, "w_in"), (0, "w_o"), (0, "w_xkv")]})
    backward_plan = {}
    for l in range(depth):
        backward_plan[(l, 1)] = [("a", f"late{l}")] + ([("mid", f"early{l + 1}")] if l + 1 < depth else [])
        backward_plan[(l, 2)] = [("mid", f"late{l}")] + ([("end", f"early{l + 1}")] if l + 1 < depth else [])
        backward_plan[(l, 3)] = [("end", f"late{l}"), ("a", f"early{l}")]
    backward_plan[(0, 2)] = [("a", "attn0")] + backward_plan[(0, 2)]
    backward_plan[(0, 3)] = [("end", "late0"), ("mid", "attn0"), ("a", "rest0")]

    def scatter_step(kind, tag, v, gb):
        keys = scatter[tag]
        n = len(keys)
        if kind == "a":
            grads = [gb[name][l] for l, name in keys]
            zones = [jax.ShapeDtypeStruct((4, g.shape[0] // N_DEV, g.shape[1]), F32) for g in grads]
            send, recv, arrays, token = _start(_scatter_copies("a", n), grads, zones, 4 * n, name=f"scatter_a_{tag}")
            flight[tag] = (send, recv, arrays, None)
            return [token]
        send, recv, arrays, kept = flight.pop(tag)
        if kind == "mid":
            arrays = _finish(_scatter_copies("a", n), send, recv, arrays, [v], name=f"scatter_a_done_{tag}")
            sums = [_chip_sum(arrays[i], arrays[n + i], blocks, name="chip_sum") for i in range(n)]
            zones = [jax.ShapeDtypeStruct(s[1].shape, BF16) for s in sums]
            send, recv, arrays, token = _start(_scatter_copies("b", n), [s[1] for s in sums], zones, 3 * n, name=f"scatter_b_{tag}")
            flight[tag] = (send, recv, arrays, [s[0] for s in sums])
            return [token]
        arrays = _finish(_scatter_copies("b", n), send, recv, arrays, [v], name=f"scatter_b_done_{tag}")
        for i, key in enumerate(keys):
            mine[key] = _final_sum(kept[i], arrays[n + i], name="final_sum")
        return []

    def bwd_hook(l, point, v, gb):
        tokens = []
        for kind, tag in backward_plan.get((l, point), []):
            tokens += scatter_step(kind, tag, v, gb)
        return _tie(v, tokens)

    loss, dx, small_grads, _ = _local_step(x, mem, a["positions"][0], target, {n: a[n] for n in SMALL}, lambda l, n: whole[(l, n)],
                                           fwd_hook, bwd_hook)

    out = {}

    def update(n, tokens):
        g = _tie(jnp.stack([mine[(l, n)] for l in range(depth)]), tokens)
        moved = _adamw(laid_out(a[n], n), g, laid_out(a["m_" + n], n), laid_out(a["v_" + n], n), name="adamw_" + n)
        for p, v in zip(("grad_", "delta_", "new_m_", "new_v_"), (g, *moved)):
            out[p + n] = laid_out(v, n)
        return moved[0]

    part = _tie(_pack([small_grads[n] for n in SMALL]), [dx])
    placed = _place_own(part[None], 0, blocks, dtype=F32, name="place_small_grads")
    send, recv, arrays, token = _start(_gather_copies("a"), [placed], [], 4, name="gather_a_small_grads")
    behind = [update(n, [token]) for n in LATE]
    tokens = scatter_step("mid", "rest0", behind[-1], None)
    arrays = _finish(_gather_copies("a"), send, recv, arrays, behind, name="gather_a_done_small_grads")
    send, recv, arrays, token = _start(_gather_copies("b"), arrays, [], 3, name="gather_b_small_grads")
    scatter_step("end", "attn0", behind[-1], None)
    behind = [update(n, tokens + [token]) for n in ("w_xq", "w_xo")]
    parts = _finish(_gather_copies("b"), send, recv, arrays, behind, name="gather_b_done_small_grads")[0].reshape(N_DEV, *part.shape)
    packs = _small_update(parts, *[_pack([a[p + n] for n in SMALL]) for p in ("", "m_", "v_")], name="small_update")
    for p, pack in zip(("grad_", "delta_", "new_m_", "new_v_"), packs):
        for n, v in zip(SMALL, _unpack(pack, [a[n].shape for n in SMALL])):
            out[p + n] = v
    scatter_step("end", "rest0", packs[0], None)
    for n in ("w_in", "w_o", "w_xkv"):
        update(n, [])

    total = lax.psum(loss[0, 0], ("x", "y", "c"))
    return (total, dx[None], *[out[p + n] for p in ("grad_", "delta_", "new_m_", "new_v_") for n in WEIGHTS])


def kernel(x, mem, positions, mem_norm_g, mix_pre_g, mix_post_g, w_in, gm_v_g, gm_w_s, gm_b_s, pool_w, pool_scale, attn_sinks,
           w_o, x_pre_g, x_post_g, w_xq, w_xkv, w_xo, ffn_pre_g, ffn_post_g, w_gate_up, w_down, loss_target, m_mem_norm_g,
           m_mix_pre_g, m_mix_post_g, m_w_in, m_gm_v_g, m_gm_w_s, m_gm_b_s, m_pool_w, m_pool_scale, m_attn_sinks, m_w_o,
           m_x_pre_g, m_x_post_g, m_w_xq, m_w_xkv, m_w_xo, m_ffn_pre_g, m_ffn_post_g, m_w_gate_up, m_w_down, v_mem_norm_g,
           v_mix_pre_g, v_mix_post_g, v_w_in, v_gm_v_g, v_gm_w_s, v_gm_b_s, v_pool_w, v_pool_scale, v_attn_sinks, v_w_o,
           v_x_pre_g, v_x_post_g, v_w_xq, v_w_xkv, v_w_xo, v_ffn_pre_g, v_ffn_post_g, v_w_gate_up, v_w_down):
    return _step(dict(locals()))
```

```python
import functools
import math

import jax
import jax.numpy as jnp
from jax import lax
from jax.experimental import pallas as pl
from jax.experimental.pallas import tpu as pltpu

F32, BF16 = jnp.float32, jnp.bfloat16
EPS = 1e-6
HEAD_DIM = 64
BLOCK = 128
POOL_WINDOWS = (2, 4, 8, 16)
ATT_Q_HEADS = 8
X_HEADS = 4
ROPE_THETA = 10000.0
ADAM_LR, ADAM_B1, ADAM_B2, ADAM_EPS, ADAM_WD, ADAM_STEP = 0.001, 0.9, 0.999, 1e-08, 0.01, 10
N_DEV = 8
TOKEN_TILE = 512
VMEM_LIMIT_BYTES = 50 * 2**20
NEG = -1e30
MESH = pl.DeviceIdType.MESH

NN = ((1,), (0,))
NT = ((1,), (1,))
TN = ((0,), (0,))


def _dot(a, b, dims):
    return lax.dot_general(a.astype(BF16), b.astype(BF16), (dims, ((), ())), preferred_element_type=F32)


def _params(*sem):
    return pltpu.CompilerParams(dimension_semantics=sem, vmem_limit_bytes=VMEM_LIMIT_BYTES)


def _full(shape):
    return pl.BlockSpec(shape, lambda *_: (0,) * len(shape))


def _lane(shape):
    return lax.broadcasted_iota(jnp.int32, shape, len(shape) - 1)


def _rms_fwd(x, g):
    r = lax.rsqrt(jnp.mean(x * x, axis=-1, keepdims=True) + EPS)
    return x * r * g


def _rms_bwd(x, g, dy):
    r = lax.rsqrt(jnp.mean(x * x, axis=-1, keepdims=True) + EPS)
    xh = x * r
    dg = jnp.sum(dy * xh, axis=0, keepdims=True)
    dxh = dy * g
    dx = r * (dxh - xh * jnp.mean(dxh * xh, axis=-1, keepdims=True))
    return dx, dg


def _silu_parts(gate):
    sg = 1.0 / (1.0 + jnp.exp(-gate))
    return gate * sg, sg


def _swiglu_cols(gu_ref, out_ref, width):
    step = 128 * max(1, (width // 128) // 4)
    for c0 in range(0, width, step):
        c1 = min(width, c0 + step)
        gate = gu_ref[:, c0:c1].astype(F32)
        up = gu_ref[:, width + c0:width + c1].astype(F32)
        out_ref[:, c0:c1] = (_silu_parts(gate)[0] * up).astype(out_ref.dtype)


def _col_tile(n, cap=1408):
    best = n
    for t in range(128, cap + 1, 128):
        if n % t == 0:
            best = t
    return best if n > cap else n


def _mm_fwd(a, w, *, nt, name, pre=None, g_pre=None, post=None, g_post=None, xres=None, g_next=None, out_dtype=BF16, tile=TOKEN_TILE):
    S = a.shape[0]
    n_out, k = (w.shape[0], w.shape[1]) if nt else (w.shape[1], w.shape[0])
    T = min(tile, S)
    tn = n_out if post else _col_tile(n_out)
    grid = (S // T, n_out // tn)

    def body(*refs):
        it = iter(refs)
        a_ref, w_ref = next(it), next(it)
        gpre_ref = next(it) if pre == "norm" else None
        gpost_ref, x_ref = (next(it), next(it)) if post else (None, None)
        gnext_ref = next(it) if g_next is not None else None
        outs = [next(it) for _ in range((2 if post else 1) + (1 if pre == "norm" else 0))]
        hnext_ref = next(it) if g_next is not None else None
        as_ref = next(it) if pre else None
        if pre:
            @pl.when(pl.program_id(1) == 0)
            def _():
                if pre == "norm":
                    h = _rms_fwd(a_ref[...], gpre_ref[...]).astype(BF16)
                    outs[-1][...] = h
                    as_ref[...] = h
                else:
                    _swiglu_cols(a_ref, as_ref, k)
            av = as_ref[...]
        else:
            av = a_ref[...]
        acc = _dot(av, w_ref[...], NT if nt else NN)
        if post:
            outs[0][...] = acc.astype(BF16)
            x_new = x_ref[...] + _rms_fwd(acc, gpost_ref[...])
            outs[1][...] = x_new
            if g_next is not None:
                hnext_ref[...] = _rms_fwd(x_new, gnext_ref[...]).astype(BF16)
        else:
            outs[0][...] = acc.astype(out_dtype)

    in_specs = [pl.BlockSpec((T, a.shape[1]), lambda i, j: (i, 0)),
                pl.BlockSpec((tn, k), lambda i, j: (j, 0)) if nt else pl.BlockSpec((k, tn), lambda i, j: (0, j))]
    args = [a, w]
    if pre == "norm":
        in_specs.append(_full((1, k)))
        args.append(g_pre)
    if post:
        in_specs += [_full((1, n_out)), pl.BlockSpec((T, n_out), lambda i, j: (i, 0))]
        args += [g_post, xres]
    if g_next is not None:
        in_specs.append(_full((1, n_out)))
        args.append(g_next)
    out_block = pl.BlockSpec((T, tn), lambda i, j: (i, j))
    if post:
        out_shape = [jax.ShapeDtypeStruct((S, n_out), BF16), jax.ShapeDtypeStruct((S, n_out), F32)]
        out_specs = [out_block, out_block]
    else:
        out_shape = [jax.ShapeDtypeStruct((S, n_out), out_dtype)]
        out_specs = [out_block]
    if pre == "norm":
        out_shape.append(jax.ShapeDtypeStruct((S, k), BF16))
        out_specs.append(pl.BlockSpec((T, k), lambda i, j: (i, 0)))
    if g_next is not None:
        out_shape.append(jax.ShapeDtypeStruct((S, n_out), BF16))
        out_specs.append(out_block)
    scratch = [pltpu.VMEM((T, k), BF16)] if pre else []
    return pl.pallas_call(body, name=name, grid=grid, in_specs=in_specs, out_specs=out_specs, out_shape=out_shape,
                          scratch_shapes=scratch, compiler_params=_params("parallel", "arbitrary"))(*args)


def _mm_bwd_post(dy, w, x, g, dx_in, *, nt, name, below=None, tile=TOKEN_TILE):
    S, k = dy.shape
    n = x.shape[1]
    T = min(tile, S)

    def body(*refs):
        dy_ref, w_ref, x_ref, g_ref, dxi_ref = refs[:5]
        m_ref, gb_ref = refs[5:7] if below else (None, None)
        dxo_ref, dg_ref = refs[-4:-2] if below else refs[-2:]

        @pl.when(pl.program_id(0) == 0)
        def _():
            dg_ref[...] = jnp.zeros_like(dg_ref)
            if below:
                refs[-1][...] = jnp.zeros_like(refs[-1])

        dh = _dot(dy_ref[...], w_ref[...], NT if nt else NN)
        dxn, dg = _rms_bwd(x_ref[...], g_ref[...], dh)
        dg_ref[...] += dg
        dxo = dxi_ref[...] + dxn
        dxo_ref[...] = dxo
        if below:
            dmv, dgb = _rms_bwd(m_ref[...].astype(F32), gb_ref[...], dxo)
            refs[-2][...] = dmv.astype(BF16)
            refs[-1][...] += dgb

    row = pl.BlockSpec((T, n), lambda i: (i, 0))
    in_specs = [pl.BlockSpec((T, k), lambda i: (i, 0)), _full(w.shape), row, _full((1, n)), row]
    out_specs = [row, _full((1, n))]
    out_shape = [jax.ShapeDtypeStruct((S, n), F32), jax.ShapeDtypeStruct((1, n), F32)]
    if below:
        in_specs += [row, _full((1, n))]
        out_specs += [row, _full((1, n))]
        out_shape += [jax.ShapeDtypeStruct((S, n), BF16), jax.ShapeDtypeStruct((1, n), F32)]
    return pl.pallas_call(body, name=name, grid=(S // T,), in_specs=in_specs, out_specs=out_specs, out_shape=out_shape,
                          compiler_params=_params("arbitrary"))(dy, w, x, g, dx_in, *(below or ()))


def _wgrad(a, b, *, name, swiglu=False, tile=TOKEN_TILE):
    S = a.shape[0]
    k = a.shape[1] // 2 if swiglu else a.shape[1]
    n = b.shape[1]
    T = min(tile, S)
    tk = 512 if k % 512 == 0 else 256
    nk = k // tk

    def body(*refs):
        if swiglu:
            gate_ref, up_ref, b_ref, o_ref, act_ref = refs
            for r0 in range(0, S, T):
                gate, up = gate_ref[r0:r0 + T, :].astype(F32), up_ref[r0:r0 + T, :].astype(F32)
                act_ref[r0:r0 + T, :] = (_silu_parts(gate)[0] * up).astype(BF16)
            av = act_ref[...]
        else:
            a_ref, b_ref, o_ref = refs
            av = a_ref[...]
        o_ref[...] = _dot(av, b_ref[...], TN)

    in_specs = [pl.BlockSpec((S, tk), lambda kk: (0, kk))]
    args = [a]
    if swiglu:
        in_specs.append(pl.BlockSpec((S, tk), lambda kk: (0, kk + nk)))
        args.append(a)
    in_specs.append(_full((S, n)))
    args.append(b)
    return pl.pallas_call(body, name=name, grid=(nk,), in_specs=in_specs,
                          out_specs=pl.BlockSpec((tk, n), lambda kk: (kk, 0)),
                          out_shape=jax.ShapeDtypeStruct((k, n), F32),
                          scratch_shapes=[pltpu.VMEM((S, tk), BF16)] if swiglu else [],
                          compiler_params=_params("parallel"))(*args)


def _ffn_out_bwd(dm, w, gu, *, name, tile=256):
    S, n = dm.shape
    f = w.shape[0]
    T = min(tile, S)

    def body(dm_ref, w_ref, gu_ref, dgu_ref, da_ref):
        da_ref[...] = _dot(dm_ref[...], w_ref[...], NT)
        step = 128 * max(1, (f // 128) // 4)
        for c0 in range(0, f, step):
            c1 = min(f, c0 + step)
            gate = gu_ref[:, c0:c1].astype(F32)
            up = gu_ref[:, f + c0:f + c1].astype(F32)
            da = da_ref[:, c0:c1]
            sl, sg = _silu_parts(gate)
            dgu_ref[:, c0:c1] = (da * up * (sg + sl * (1.0 - sg))).astype(BF16)
            dgu_ref[:, f + c0:f + c1] = (da * sl).astype(BF16)

    row = pl.BlockSpec((T, n), lambda i: (i, 0))
    wide = pl.BlockSpec((T, 2 * f), lambda i: (i, 0))
    return pl.pallas_call(
        body, name=name, grid=(S // T,), in_specs=[row, _full((f, n)), wide], out_specs=wide,
        out_shape=jax.ShapeDtypeStruct((S, 2 * f), BF16),
        scratch_shapes=[pltpu.VMEM((T, f), F32)], compiler_params=_params("parallel"))(dm, w, gu)


def _loss_grad(y, target, m, g, *, name, tile=TOKEN_TILE):
    S, d = y.shape
    T = min(tile, S)
    steps = S // T

    def body(y_ref, t_ref, m_ref, g_ref, loss_ref, dy_ref, dm_ref, dg_ref, acc_ref):
        i = pl.program_id(0)

        @pl.when(i == 0)
        def _():
            acc_ref[...] = jnp.zeros_like(acc_ref)
            dg_ref[...] = jnp.zeros_like(dg_ref)

        diff = y_ref[...] - t_ref[...]
        dy = diff * (1.0 / d)
        dy_ref[...] = dy
        dmv, dg = _rms_bwd(m_ref[...].astype(F32), g_ref[...], dy)
        dm_ref[...] = dmv.astype(BF16)
        dg_ref[...] += dg
        acc_ref[...] += jnp.sum(diff * diff, axis=0, keepdims=True)

        @pl.when(i == steps - 1)
        def _():
            total = jnp.sum(acc_ref[...], axis=1, keepdims=True) * (0.5 / d)
            loss_ref[...] = jnp.broadcast_to(total, loss_ref.shape)

    row = pl.BlockSpec((T, d), lambda i: (i, 0))
    return pl.pallas_call(
        body, name=name, grid=(steps,), in_specs=[row, row, row, _full((1, d))], out_specs=[_full((1, 128)), row, row, _full((1, d))],
        out_shape=[jax.ShapeDtypeStruct((1, 128), F32), jax.ShapeDtypeStruct((S, d), F32), jax.ShapeDtypeStruct((S, d), BF16),
                   jax.ShapeDtypeStruct((1, d), F32)],
        scratch_shapes=[pltpu.VMEM((1, d), F32)], compiler_params=_params("arbitrary"))(y, target, m, g)


def _norm_dgain(x, dys, *, name):
    def body(x_ref, dy_ref, o_ref):
        xv = x_ref[...]
        dy = dy_ref[0]
        for l in range(1, dys.shape[0]):
            dy = dy + dy_ref[l]
        r = lax.rsqrt(jnp.mean(xv * xv, axis=-1, keepdims=True) + EPS)
        o_ref[...] = jnp.sum(dy * xv * r, axis=0, keepdims=True)

    return pl.pallas_call(body, name=name, out_shape=jax.ShapeDtypeStruct((1, x.shape[1]), F32),
                          compiler_params=pltpu.CompilerParams(vmem_limit_bytes=VMEM_LIMIT_BYTES))(x, dys)


def _xattn_probs(q_ref, kv_ref, h, d, hd):
    q = q_ref[:, h * hd:(h + 1) * hd]
    kh = kv_ref[:, h * hd:(h + 1) * hd]
    vh = kv_ref[:, d + h * hd:d + (h + 1) * hd]
    s = _dot(q, kh, NT) * (hd ** -0.5)
    e = jnp.exp(s - jnp.max(s, axis=-1, keepdims=True))
    return q, kh, vh, e / jnp.sum(e, axis=-1, keepdims=True)


def _xattn_fwd(q, kv, *, name, tile=TOKEN_TILE):
    S, d = q.shape
    mlen = kv.shape[0]
    hd = d // X_HEADS
    T = min(tile, S)

    def body(q_ref, kv_ref, o_ref):
        for h in range(X_HEADS):
            _, _, vh, p = _xattn_probs(q_ref, kv_ref, h, d, hd)
            o_ref[:, h * hd:(h + 1) * hd] = _dot(p, vh, NN).astype(BF16)

    row = pl.BlockSpec((T, d), lambda i: (i, 0))
    return pl.pallas_call(body, name=name, grid=(S // T,), in_specs=[row, _full((mlen, 2 * d))], out_specs=row,
                          out_shape=jax.ShapeDtypeStruct((S, d), BF16), compiler_params=_params("parallel"))(q, kv)


def _xattn_bwd(q, kv, do, *, name, tile=TOKEN_TILE):
    S, d = q.shape
    mlen = kv.shape[0]
    hd = d // X_HEADS
    T = min(tile, S)

    def body(q_ref, kv_ref, do_ref, dq_ref, dkv_ref):
        @pl.when(pl.program_id(0) == 0)
        def _():
            dkv_ref[...] = jnp.zeros_like(dkv_ref)

        for h in range(X_HEADS):
            qh, kh, vh, p = _xattn_probs(q_ref, kv_ref, h, d, hd)
            doh = do_ref[:, h * hd:(h + 1) * hd]
            dp = _dot(doh, vh, NT)
            ds = p * (dp - jnp.sum(p * dp, axis=-1, keepdims=True)) * (hd ** -0.5)
            dq_ref[:, h * hd:(h + 1) * hd] = _dot(ds, kh, NN).astype(BF16)
            dkv_ref[:, h * hd:(h + 1) * hd] += _dot(ds, qh, TN)
            dkv_ref[:, d + h * hd:d + (h + 1) * hd] += _dot(p, doh, TN)

    row = pl.BlockSpec((T, d), lambda i: (i, 0))
    return pl.pallas_call(
        body, name=name, grid=(S // T,), in_specs=[row, _full((mlen, 2 * d)), row],
        out_specs=[row, _full((mlen, 2 * d))],
        out_shape=[jax.ShapeDtypeStruct((S, d), BF16), jax.ShapeDtypeStruct((mlen, 2 * d), F32)],
        compiler_params=_params("arbitrary"))(q, kv, do)


GM0, POOL0, Q0, KV0, Z_END = 0, 512, 768, 1280, 1536


def _rope_tables(pos_col, inv_row, *, name):
    S = pos_col.shape[0]

    def body(p_ref, inv_ref, cos_ref, sin_ref):
        ang = p_ref[...].astype(F32) * inv_ref[...]
        sin = jnp.sin(ang)
        cos_ref[...] = jnp.cos(ang)
        sin_ref[...] = jnp.where(_lane(ang.shape) % HEAD_DIM < HEAD_DIM // 2, -sin, sin)

    T = min(TOKEN_TILE, S)
    return pl.pallas_call(
        body, name=name, grid=(S // T,),
        in_specs=[pl.BlockSpec((T, 1), lambda i: (i, 0)), _full((1, 128))],
        out_specs=[pl.BlockSpec((T, 128), lambda i: (i, 0))] * 2,
        out_shape=[jax.ShapeDtypeStruct((S, 128), F32)] * 2, compiler_params=_params("parallel"))(pos_col, inv_row)


def _swap_halves(x):
    n = x.shape[-1]
    return jnp.where(_lane(x.shape) % HEAD_DIM < HEAD_DIM // 2, pltpu.roll(x, n - HEAD_DIM // 2, 1), pltpu.roll(x, HEAD_DIM // 2, 1))


def _rope(x, cos, sin_s):
    reps = x.shape[-1] // 128
    if reps > 1:
        cos, sin_s = jnp.tile(cos, (1, reps)), jnp.tile(sin_s, (1, reps))
    return x * cos + _swap_halves(x) * sin_s


def _rope_bwd(dy, cos, sin_s):
    reps = dy.shape[-1] // 128
    if reps > 1:
        cos, sin_s = jnp.tile(cos, (1, reps)), jnp.tile(sin_s, (1, reps))
    return dy * cos + _swap_halves(dy * sin_s)


def _gelu_parts(x):
    c = math.sqrt(2.0 / math.pi)
    t = jnp.tanh(c * (x + 0.044715 * x * x * x))
    return 0.5 * x * (1.0 + t), t


def _gelu_grad(x, t):
    c = math.sqrt(2.0 / math.pi)
    return 0.5 * (1.0 + t) + 0.5 * x * (1.0 - t * t) * c * (1.0 + 3.0 * 0.044715 * x * x)


def _group_sum(x):
    gid = _lane(x.shape) // HEAD_DIM
    out = jnp.zeros_like(x)
    for g in range(x.shape[-1] // HEAD_DIM):
        sel = gid == g
        out = jnp.where(sel, jnp.sum(jnp.where(sel, x, 0.0), axis=-1, keepdims=True), out)
    return out


def _low_half(shape):
    return _lane(shape) % 128 < HEAD_DIM


def _tril(w):
    r = lax.broadcasted_iota(jnp.int32, w.shape, 0)
    c = lax.broadcasted_iota(jnp.int32, w.shape, 1)
    return jnp.where(r >= c, w, 0.0)


def _triu(w):
    r = lax.broadcasted_iota(jnp.int32, w.shape, 0)
    c = lax.broadcasted_iota(jnp.int32, w.shape, 1)
    return jnp.where(r <= c, w, 0.0)


def _gating_fwd(zg, vg, wc_ref, bfull):
    ge, th = _gelu_parts(zg)
    u, v = ge[:, :256], ge[:, 256:]
    r = lax.rsqrt(_group_sum(v * v) * (1.0 / HEAD_DIM) + EPS)
    vhat = v * r
    vn = vhat * vg
    low = _low_half((BLOCK, 128))
    blocks = []
    for blk in range(2):
        vb = vn[:, 128 * blk:128 * (blk + 1)]
        m0 = _dot(_tril(wc_ref[2 * blk]), vb, NN)
        m1 = _dot(_tril(wc_ref[2 * blk + 1]), vb, NN)
        blocks.append(jnp.where(low, m0, m1))
    mixed = jnp.concatenate(blocks, axis=1) + bfull
    return u * mixed, (th, u, r, vhat, vn, mixed)


def _band(w, transpose):
    if transpose:
        s2 = lax.broadcasted_iota(jnp.int32, (2 * BLOCK, BLOCK), 0)
        t = lax.broadcasted_iota(jnp.int32, (2 * BLOCK, BLOCK), 1)
    else:
        t = lax.broadcasted_iota(jnp.int32, (BLOCK, 2 * BLOCK), 0)
        s2 = lax.broadcasted_iota(jnp.int32, (BLOCK, 2 * BLOCK), 1)
    d = t - s2 + BLOCK
    return jnp.where((d >= 0) & (d < w), 1.0, 0.0).astype(BF16)


def _inv_count(n):
    t = lax.broadcasted_iota(jnp.int32, (BLOCK, 256), 0) + n * BLOCK
    gid = _lane((BLOCK, 256)) // HEAD_DIM
    win = jnp.where(gid == 0, POOL_WINDOWS[0], jnp.where(gid == 1, POOL_WINDOWS[1], jnp.where(gid == 2, POOL_WINDOWS[2], POOL_WINDOWS[3])))
    return 1.0 / jnp.minimum(t + 1, win).astype(F32)


def _pool_fwd(p, pp, n, bd, scale):
    low = _low_half((BLOCK, 128))
    blocks = []
    for blk in range(2):
        both = jnp.concatenate([pp[:, 128 * blk:128 * (blk + 1)], p[:, 128 * blk:128 * (blk + 1)]], axis=0)
        s0 = _dot(_band(POOL_WINDOWS[2 * blk], False), both, NN)
        s1 = _dot(_band(POOL_WINDOWS[2 * blk + 1], False), both, NN)
        blocks.append(jnp.where(low, s0, s1))
    inv = _inv_count(n)
    pooled = jnp.concatenate(blocks, axis=1) * inv - p
    mapped = _dot(pooled, bd, NN)
    return mapped * scale, (inv, pooled, mapped)


def _mixer_fwd(z, cos_t, sin_t, vg, wc, bfull, bd, scale, sinks, *, name):
    S = z.shape[0]
    nb = S // BLOCK

    def body(zc_ref, zpp_ref, zpk_ref, cc_ref, cp_ref, sc_ref, sp_ref, vg_ref, wc_ref, bf_ref, bd_ref, scale_ref, sinks_ref, out_ref):
        n = pl.program_id(0)
        keep = (n > 0).astype(F32)
        a, _ = _gating_fwd(zc_ref[:, GM0:POOL0].astype(F32), vg_ref[...], wc_ref, bf_ref[...])
        out_ref[:, 0:256] = a.astype(BF16)
        b, _ = _pool_fwd(zc_ref[:, POOL0:Q0].astype(F32), zpp_ref[...].astype(F32) * keep, n, bd_ref[...], scale_ref[...])
        out_ref[:, 256:512] = b.astype(BF16)
        cos_c, sin_c = cc_ref[...], sc_ref[...]
        qrot = _rope(zc_ref[:, Q0:KV0].astype(F32), cos_c, sin_c)
        kk = jnp.concatenate([_rope(zpk_ref[:, 0:128].astype(F32), cp_ref[...], sp_ref[...]), _rope(zc_ref[:, KV0:KV0 + 128].astype(F32), cos_c, sin_c)], axis=0)
        vv = jnp.concatenate([zpk_ref[:, 128:256].astype(F32), zc_ref[:, KV0 + 128:Z_END].astype(F32)], axis=0)
        kk, vv = kk.astype(BF16), vv.astype(BF16)
        tall = (ATT_Q_HEADS // 2 * BLOCK, 2 * BLOCK)
        qi = lax.broadcasted_iota(jnp.int32, tall, 0) % BLOCK
        ki = lax.broadcasted_iota(jnp.int32, tall, 1)
        valid = (ki > qi) & (ki <= qi + BLOCK) & (ki >= jnp.where(n > 0, 0, BLOCK))
        low = _low_half((BLOCK, 128))
        q_slabs = [qrot[:, 128 * s:128 * (s + 1)] for s in range(ATT_Q_HEADS // 2)]
        q_rolled = [pltpu.roll(v, HEAD_DIM, 1) for v in q_slabs]
        o_heads = [None] * ATT_Q_HEADS
        for h in range(2):
            lanes = ~low if h else low
            heads = range(4 * h, 4 * h + 4)
            qm = jnp.concatenate([jnp.where(lanes, (q_slabs if j % 2 == h else q_rolled)[j // 2], 0.0) for j in heads], axis=0)
            sc = jnp.where(valid, _dot(qm, kk, NT) * (HEAD_DIM ** -0.5), NEG)
            p = []
            for i, j in enumerate(heads):
                sch, sink = sc[BLOCK * i:BLOCK * (i + 1)], sinks_ref[j]
                mx = jnp.maximum(jnp.max(sch, axis=-1, keepdims=True), sink)
                e = jnp.exp(sch - mx)
                p.append((e / (jnp.sum(e, axis=-1, keepdims=True) + jnp.exp(sink - mx))).astype(BF16))
            o = _dot(jnp.concatenate(p, axis=0), vv, NN)
            for i, j in enumerate(heads):
                oh = o[BLOCK * i:BLOCK * (i + 1)]
                o_heads[j] = oh if j % 2 == h else pltpu.roll(oh, HEAD_DIM, 1)
        for s in range(ATT_Q_HEADS // 2):
            out_ref[:, 512 + 128 * s:512 + 128 * (s + 1)] = jnp.where(low, o_heads[2 * s], o_heads[2 * s + 1]).astype(BF16)

    cur = lambda n: (n, 0)
    prev = lambda n: (jnp.maximum(n - 1, 0), 0)
    in_specs = [
        pl.BlockSpec((BLOCK, Z_END), cur),
        pl.BlockSpec((BLOCK, 256), lambda n: (jnp.maximum(n - 1, 0), POOL0 // 256)),
        pl.BlockSpec((BLOCK, 256), lambda n: (jnp.maximum(n - 1, 0), KV0 // 256)),
        pl.BlockSpec((BLOCK, 128), cur), pl.BlockSpec((BLOCK, 128), prev),
        pl.BlockSpec((BLOCK, 128), cur), pl.BlockSpec((BLOCK, 128), prev),
        _full((1, 256)), _full((4, BLOCK, BLOCK)), _full((BLOCK, 256)), _full((256, 256)), _full((1, 256)),
        pl.BlockSpec(memory_space=pltpu.SMEM),
    ]
    return pl.pallas_call(
        body, name=name, grid=(nb,), in_specs=in_specs, out_specs=pl.BlockSpec((BLOCK, 1024), cur),
        out_shape=jax.ShapeDtypeStruct((S, 1024), BF16),
        compiler_params=_params("parallel"))(z, z, z, cos_t, cos_t, sin_t, sin_t, vg, wc, bfull, bd, scale, sinks)


def _mixer_bwd(z, dcat, cos_t, sin_t, vg, wc, bfull, bd, scale, sinks, *, name):
    S = z.shape[0]
    nb = S // BLOCK

    def body(zc_ref, zpp_ref, zpk_ref, dc_ref, cc_ref, cp_ref, sc_ref, sp_ref, vg_ref, wc_ref, bf_ref, bd_ref, scale_ref, sinks_ref, wct_ref,
             dz_ref, dvg_ref, dwc_ref, dbf_ref, dbd_ref, dscale_ref, dsink_ref, carry_ref):
        n = pl.program_id(0)

        @pl.when(n == 0)
        def _():
            for ref in (dvg_ref, dwc_ref, dbf_ref, dbd_ref, dscale_ref, dsink_ref):
                ref[...] = jnp.zeros_like(ref)

        @pl.when(n < nb)
        def _():
            keep = (n > 0).astype(F32)
            low = _low_half((BLOCK, 128))
            zg = zc_ref[:, GM0:POOL0].astype(F32)
            vg_v = vg_ref[...]
            _, (th, u, r, vhat, vn, mixed) = _gating_fwd(zg, vg_v, wc_ref, bf_ref[...])
            da = dc_ref[:, 0:256].astype(F32)
            dmixed = da * u
            dbf_ref[...] += dmixed
            dvn = []
            for blk in range(2):
                dmb = dmixed[:, 128 * blk:128 * (blk + 1)]
                vb = vn[:, 128 * blk:128 * (blk + 1)]
                acc = None
                for half in range(2):
                    g = 2 * blk + half
                    dmg = jnp.where(low if half == 0 else ~low, dmb, 0.0)
                    dwc_ref[g] += _tril(_dot(dmg, vb, NT))
                    part = _dot(_triu(wct_ref[g]), dmg, NN)
                    acc = part if acc is None else acc + part
                dvn.append(acc)
            dvn = jnp.concatenate(dvn, axis=1)
            dvg_ref[...] += jnp.sum(dvn * vhat, axis=0, keepdims=True)
            dvhat = dvn * vg_v
            dv = r * (dvhat - vhat * (_group_sum(dvhat * vhat) * (1.0 / HEAD_DIM)))
            dge = jnp.concatenate([da * mixed, dv], axis=1)
            own_g = dge * _gelu_grad(zg, th)
            p = zc_ref[:, POOL0:Q0].astype(F32)
            bd_v, scale_v = bd_ref[...], scale_ref[...]
            _, (inv, pooled, mapped) = _pool_fwd(p, zpp_ref[...].astype(F32) * keep, n, bd_v, scale_v)
            db = dc_ref[:, 256:512].astype(F32)
            dscale_ref[...] += jnp.sum(db * mapped, axis=0, keepdims=True)
            dmapped = db * scale_v
            dbd_ref[...] += _dot(pooled, dmapped, TN)
            dpooled = _dot(dmapped, bd_v, NT)
            dps = dpooled * inv
            back = []
            for blk in range(2):
                dpb = dps[:, 128 * blk:128 * (blk + 1)]
                t0 = _dot(_band(POOL_WINDOWS[2 * blk], True), dpb, NN)
                t1 = _dot(_band(POOL_WINDOWS[2 * blk + 1], True), dpb, NN)
                back.append(jnp.where(jnp.concatenate([low, low], axis=0), t0, t1))
            back = jnp.concatenate(back, axis=1)
            halo_p = back[:BLOCK] * keep
            own_p = back[BLOCK:] - dpooled
            cos_c, sin_c, cos_p, sin_p = cc_ref[...], sc_ref[...], cp_ref[...], sp_ref[...]
            qrot = _rope(zc_ref[:, Q0:KV0].astype(F32), cos_c, sin_c)
            kk = jnp.concatenate([_rope(zpk_ref[:, 0:128].astype(F32), cos_p, sin_p), _rope(zc_ref[:, KV0:KV0 + 128].astype(F32), cos_c, sin_c)], axis=0)
            vv = jnp.concatenate([zpk_ref[:, 128:256].astype(F32), zc_ref[:, KV0 + 128:Z_END].astype(F32)], axis=0)
            kt = jnp.transpose(kk).astype(BF16)
            kk, vv = kk.astype(BF16), vv.astype(BF16)
            wide = (2 * BLOCK, ATT_Q_HEADS // 2 * BLOCK)
            ki = lax.broadcasted_iota(jnp.int32, wide, 0)
            qi = lax.broadcasted_iota(jnp.int32, wide, 1) % BLOCK
            valid_t = (ki > qi) & (ki <= qi + BLOCK) & (ki >= jnp.where(n > 0, 0, BLOCK))
            q_slabs = [qrot[:, 128 * s:128 * (s + 1)] for s in range(ATT_Q_HEADS // 2)]
            d_slabs = [dc_ref[:, 512 + 128 * s:512 + 128 * (s + 1)].astype(F32) for s in range(ATT_Q_HEADS // 2)]
            q_rolled = [pltpu.roll(v, HEAD_DIM, 1) for v in q_slabs]
            d_rolled = [pltpu.roll(v, HEAD_DIM, 1) for v in d_slabs]
            dkk = jnp.zeros((2 * BLOCK, 128), F32)
            dvv = jnp.zeros((2 * BLOCK, 128), F32)
            dq_heads = [None] * ATT_Q_HEADS
            for h in range(2):
                lanes = ~low if h else low
                heads = range(4 * h, 4 * h + 4)
                qm = jnp.concatenate([jnp.where(lanes, (q_slabs if j % 2 == h else q_rolled)[j // 2], 0.0) for j in heads], axis=0).astype(BF16)
                dom = jnp.concatenate([jnp.where(lanes, (d_slabs if j % 2 == h else d_rolled)[j // 2], 0.0) for j in heads], axis=0).astype(BF16)
                sink = jnp.concatenate([jnp.full((1, BLOCK), sinks_ref[j], F32) for j in heads], axis=1)
                sc = jnp.where(valid_t, _dot(kk, qm, NT) * (HEAD_DIM ** -0.5), NEG)
                mx = jnp.maximum(jnp.max(sc, axis=0, keepdims=True), sink)
                e = jnp.exp(sc - mx)
                es = jnp.exp(sink - mx)
                inv_den = 1.0 / (jnp.sum(e, axis=0, keepdims=True) + es)
                pt = e * inv_den
                dpt = _dot(vv, dom, NT)
                dvv = dvv + _dot(pt, dom, NN)
                dcol = jnp.sum(pt * dpt, axis=0, keepdims=True)
                dst = (pt * (dpt - dcol) * (HEAD_DIM ** -0.5)).astype(BF16)
                dsink = es * inv_den * dcol
                dkk = dkk + _dot(dst, qm, NN)
                dqt = _dot(kt, dst, NN)
                for i, j in enumerate(heads):
                    part = jnp.sum(dsink[:, BLOCK * i:BLOCK * (i + 1)], axis=1, keepdims=True)
                    dsink_ref[pl.ds(j, 1), :] = dsink_ref[pl.ds(j, 1), :] - jnp.broadcast_to(part, (1, 128))
                    dqh = jnp.transpose(dqt[:, BLOCK * i:BLOCK * (i + 1)])
                    dq_heads[j] = dqh if j % 2 == h else pltpu.roll(dqh, HEAD_DIM, 1)
            dqrot = jnp.concatenate([jnp.where(low, dq_heads[2 * s], dq_heads[2 * s + 1]) for s in range(ATT_Q_HEADS // 2)], axis=1)
            own_q = _rope_bwd(dqrot, cos_c, sin_c)
            own_k = _rope_bwd(dkk[BLOCK:], cos_c, sin_c)
            halo_k = _rope_bwd(dkk[:BLOCK], cos_p, sin_p) * keep
            own_v, halo_v = dvv[BLOCK:], dvv[:BLOCK] * keep

            @pl.when(n > 0)
            def _():
                dz_ref[:, GM0:POOL0] = carry_ref[:, GM0:POOL0].astype(BF16)
                dz_ref[:, POOL0:Q0] = (carry_ref[:, POOL0:Q0] + halo_p).astype(BF16)
                dz_ref[:, Q0:KV0] = carry_ref[:, Q0:KV0].astype(BF16)
                dz_ref[:, KV0:KV0 + 128] = (carry_ref[:, KV0:KV0 + 128] + halo_k).astype(BF16)
                dz_ref[:, KV0 + 128:Z_END] = (carry_ref[:, KV0 + 128:Z_END] + halo_v).astype(BF16)

            carry_ref[:, GM0:POOL0] = own_g
            carry_ref[:, POOL0:Q0] = own_p
            carry_ref[:, Q0:KV0] = own_q
            carry_ref[:, KV0:KV0 + 128] = own_k
            carry_ref[:, KV0 + 128:Z_END] = own_v

        @pl.when(n == nb)
        def _():
            dz_ref[...] = carry_ref[...].astype(BF16)

    cur = lambda n: (jnp.minimum(n, nb - 1), 0)
    prev = lambda n: (jnp.maximum(jnp.minimum(n, nb - 1) - 1, 0), 0)
    in_specs = [
        pl.BlockSpec((BLOCK, Z_END), cur),
        pl.BlockSpec((BLOCK, 256), lambda n: (jnp.maximum(jnp.minimum(n, nb - 1) - 1, 0), POOL0 // 256)),
        pl.BlockSpec((BLOCK, 256), lambda n: (jnp.maximum(jnp.minimum(n, nb - 1) - 1, 0), KV0 // 256)),
        pl.BlockSpec((BLOCK, 1024), cur),
        pl.BlockSpec((BLOCK, 128), cur), pl.BlockSpec((BLOCK, 128), prev),
        pl.BlockSpec((BLOCK, 128), cur), pl.BlockSpec((BLOCK, 128), prev),
        _full((1, 256)), _full((4, BLOCK, BLOCK)), _full((BLOCK, 256)), _full((256, 256)), _full((1, 256)),
        pl.BlockSpec(memory_space=pltpu.SMEM), _full((4, BLOCK, BLOCK)),
    ]
    out_specs = [pl.BlockSpec((BLOCK, Z_END), lambda n: (jnp.maximum(n - 1, 0), 0)),
                 _full((1, 256)), _full((4, BLOCK, BLOCK)), _full((BLOCK, 256)), _full((256, 256)), _full((1, 256)), _full((8, 128))]
    out_shape = [jax.ShapeDtypeStruct((S, Z_END), BF16), jax.ShapeDtypeStruct((1, 256), F32),
                 jax.ShapeDtypeStruct((4, BLOCK, BLOCK), F32), jax.ShapeDtypeStruct((BLOCK, 256), F32),
                 jax.ShapeDtypeStruct((256, 256), F32), jax.ShapeDtypeStruct((1, 256), F32), jax.ShapeDtypeStruct((8, 128), F32)]
    return pl.pallas_call(
        body, name=name, grid=(nb + 1,), in_specs=in_specs, out_specs=out_specs, out_shape=out_shape,
        scratch_shapes=[pltpu.VMEM((BLOCK, Z_END), F32)],
        compiler_params=_params("arbitrary"))(z, z, z, dcat, cos_t, cos_t, sin_t, sin_t, vg, wc, bfull, bd, scale, sinks, jnp.swapaxes(wc, 1, 2))


HBM = pl.BlockSpec(memory_space=pl.ANY)


def _place():
    return lax.axis_index("x"), lax.axis_index("y"), lax.axis_index("c")


def _chip(k, x, y):
    return (1 - x if k & 1 else x, 1 - y if k & 2 else y)


def _block_rows(ref, rows, dev):
    start = pl.multiple_of((4 * dev[0] + 2 * dev[1] + dev[2]) * rows, 8)
    return ref.at[pl.ds(start, rows), :]


_HBM_SPEC = pl.BlockSpec(memory_space=pltpu.HBM)
_SEM_SPEC = pl.BlockSpec(memory_space=pltpu.SEMAPHORE)
_SPLIT_PARAMS = dict(compiler_params=pltpu.CompilerParams(has_side_effects=pltpu.SideEffectType.DATAFLOW_SIDE_EFFECTING))


def _descriptors(copies, refs, send_sems, recv_sems):
    return [pltpu.make_async_remote_copy(src_ref=s, dst_ref=d, send_sem=send_sems.at[i], recv_sem=recv_sems.at[i], device_id=to, device_id_type=MESH)
            for i, (s, d, to) in enumerate(copies(refs))]


def _start(copies, arrays, fresh, n_copies, *, name):
    operands = [pltpu.with_memory_space_constraint(v, pltpu.HBM) for v in (*arrays, *[lax.empty(f.shape, f.dtype) for f in fresh])]
    n = len(operands)

    def body(*refs):
        for dma in _descriptors(copies, refs[:n], refs[n], refs[n + 1]):
            dma.start()
        refs[-1][...] = jnp.zeros_like(refs[-1])

    res = pl.pallas_call(
        body, name=name, in_specs=[_HBM_SPEC] * n,
        out_shape=(pltpu.SemaphoreType.DMA((n_copies,)), pltpu.SemaphoreType.DMA((n_copies,)),
                   *[pltpu.HBM(v.shape, v.dtype) for v in operands], jax.ShapeDtypeStruct((8, 128), F32)),
        out_specs=(_SEM_SPEC, _SEM_SPEC, *[_HBM_SPEC] * n, pl.BlockSpec(memory_space=pltpu.VMEM)),
        input_output_aliases={i: 2 + i for i in range(n)}, **_SPLIT_PARAMS)(*operands)
    return res[0], res[1], list(res[2:2 + n]), res[-1]


def _finish(copies, send_sems, recv_sems, arrays, after, *, name):
    n = len(arrays)

    def body(*refs):
        for dma in _descriptors(copies, refs[:n], refs[n], refs[n + 1]):
            dma.wait_send()
            dma.wait_recv()

    return list(pl.pallas_call(
        body, name=name, in_specs=[_HBM_SPEC] * n + [_SEM_SPEC, _SEM_SPEC] + [HBM] * len(after),
        out_shape=tuple(pltpu.HBM(v.shape, v.dtype) for v in arrays), out_specs=tuple([_HBM_SPEC] * n),
        input_output_aliases={i: i for i in range(n)}, **_SPLIT_PARAMS)(*arrays, send_sems, recv_sems, *after))


def _tie(value, tokens):
    if not tokens:
        return value

    def body(*refs):
        pass

    return pl.pallas_call(body, name="tie", in_specs=[HBM] * (1 + len(tokens)), out_specs=HBM,
                          out_shape=jax.ShapeDtypeStruct(value.shape, value.dtype), input_output_aliases={0: 0})(value, *tokens)


def _gather_copies(stage):
    def copies(refs):
        x, y, c = _place()
        me, sibling = (x, y, c), (x, y, 1 - c)
        out = []
        for ref in refs:
            r = ref.shape[0] // N_DEV
            if stage == "a":
                hops = [(me, sibling)] + [(me, (*_chip(k, x, y), c)) for k in (1, 2, 3)]
            else:
                hops = [((*_chip(k, x, y), c), sibling) for k in (1, 2, 3)]
            for block, to in hops:
                rows = _block_rows(ref, r, block)
                out.append((rows, rows, to))
        return out
    return copies


def _scatter_copies(stage, n):
    def copies(refs):
        x, y, c = _place()
        out = []
        for w in range(n):
            src, dst = refs[w], refs[n + w]
            if stage == "a":
                r = src.shape[0] // N_DEV
                out += [(_block_rows(src, r, (*_chip(k, x, y), 1 - c)), dst.at[k], (x, y, 1 - c)) for k in range(4)]
            else:
                out += [(src.at[k - 1], dst.at[k - 1], (*_chip(k, x, y), c)) for k in (1, 2, 3)]
        return out
    return copies


def _in_hbm(*operands):
    return [pltpu.with_memory_space_constraint(v, pltpu.HBM) for v in operands]


def _place_own(shards, layer, blocks, *, name, dtype=BF16):
    _, r, cdim = shards.shape

    def body(blocks_ref, s_ref, o_ref):
        o_ref[...] = s_ref[...].astype(dtype)

    return pl.pallas_call(
        body, name=name,
        grid_spec=pltpu.PrefetchScalarGridSpec(
            num_scalar_prefetch=1, grid=(1,), in_specs=[pl.BlockSpec((None, r, cdim), lambda i, blocks: (layer, 0, 0))],
            out_specs=pl.BlockSpec((r, cdim), lambda i, blocks: (blocks[0], 0))),
        out_shape=jax.ShapeDtypeStruct((N_DEV * r, cdim), dtype), compiler_params=_params("arbitrary"))(blocks, shards)


def _chip_sum(grad, got, blocks, *, name):
    _, r, cdim = got.shape
    tr = r if r <= 256 else r // (2 if r % 32 == 0 and r // 2 <= 256 else 4)
    steps = r // tr

    def body(blocks_ref, a0_ref, a1_ref, a2_ref, a3_ref, b_ref, mine_ref, away_ref):
        mine_ref[...] = a0_ref[...] + b_ref[0]
        for k, a_ref in ((1, a1_ref), (2, a2_ref), (3, a3_ref)):
            away_ref[k - 1] = (a_ref[...] + b_ref[k]).astype(BF16)

    own = [pl.BlockSpec((tr, cdim), lambda i, blocks, k=k: (blocks[k] * steps + i, 0)) for k in range(4)]
    return pl.pallas_call(
        body, name=name,
        grid_spec=pltpu.PrefetchScalarGridSpec(
            num_scalar_prefetch=1, grid=(steps,),
            in_specs=own + [pl.BlockSpec((4, tr, cdim), lambda i, blocks: (0, i, 0))],
            out_specs=[pl.BlockSpec((tr, cdim), lambda i, blocks: (i, 0)), pl.BlockSpec((3, tr, cdim), lambda i, blocks: (0, i, 0))]),
        out_shape=[jax.ShapeDtypeStruct((r, cdim), F32), jax.ShapeDtypeStruct((3, r, cdim), BF16)],
        compiler_params=_params("arbitrary"))(blocks, *_in_hbm(grad, grad, grad, grad, got))


def _final_sum(mine, got, *, name):
    r, cdim = mine.shape
    tr = r if r <= 256 else r // (2 if r % 32 == 0 and r // 2 <= 256 else 4)

    def body(m_ref, g_ref, o_ref):
        o_ref[...] = ((m_ref[...] + g_ref[0].astype(F32)) + g_ref[1].astype(F32)) + g_ref[2].astype(F32)

    return pl.pallas_call(
        body, name=name, grid=(r // tr,),
        in_specs=[pl.BlockSpec((tr, cdim), lambda i: (i, 0)), pl.BlockSpec((3, tr, cdim), lambda i: (0, i, 0))],
        out_specs=pl.BlockSpec((tr, cdim), lambda i: (i, 0)), out_shape=jax.ShapeDtypeStruct((r, cdim), F32),
        compiler_params=_params("parallel"))(*_in_hbm(mine, got))


def _adamw_math(w, g, m, v):
    m = ADAM_B1 * m + (1.0 - ADAM_B1) * g
    v = ADAM_B2 * v + (1.0 - ADAM_B2) * (g * g)
    m_hat = m / (1.0 - ADAM_B1 ** ADAM_STEP)
    v_hat = v / (1.0 - ADAM_B2 ** ADAM_STEP)
    return -ADAM_LR * (m_hat / (jnp.sqrt(v_hat) + ADAM_EPS) + ADAM_WD * w), m, v


def _adamw(w, g, m, v, *, name):
    L, r, cdim = w.shape
    tr = r if r <= 512 else r // 2
    assert r % tr == 0 and tr % 8 == 0

    def body(w_ref, g_ref, m_ref, v_ref, d_ref, nm_ref, nv_ref):
        d_ref[...], nm_ref[...], nv_ref[...] = _adamw_math(w_ref[...], g_ref[...], m_ref[...], v_ref[...])

    blk = pl.BlockSpec((None, tr, cdim), lambda l, i: (l, i, 0))
    return pl.pallas_call(body, name=name, grid=(L, r // tr), in_specs=[blk] * 4, out_specs=[blk] * 3,
                          out_shape=[jax.ShapeDtypeStruct(w.shape, F32)] * 3, compiler_params=_params("parallel", "parallel"))(*_in_hbm(w, g, m, v))


def _small_update(parts, w, m, v, *, name):
    _, p, lanes = parts.shape
    tp = p // 2 if p % 16 == 0 else p

    def body(p_ref, w_ref, m_ref, v_ref, g_ref, d_ref, nm_ref, nv_ref):
        g = p_ref[0]
        for i in range(1, N_DEV):
            g = g + p_ref[i]
        g_ref[...] = g
        d_ref[...], nm_ref[...], nv_ref[...] = _adamw_math(w_ref[...], g, m_ref[...], v_ref[...])

    blk = pl.BlockSpec((tp, lanes), lambda i: (i, 0))
    return pl.pallas_call(body, name=name, grid=(p // tp,), in_specs=[pl.BlockSpec((N_DEV, tp, lanes), lambda i: (0, i, 0)), blk, blk, blk],
                          out_specs=[blk] * 4, out_shape=[jax.ShapeDtypeStruct((p, lanes), F32)] * 4,
                          compiler_params=_params("parallel"))(*_in_hbm(parts, w, m, v))


BIG = ("w_in", "w_o", "w_xq", "w_xkv", "w_xo", "w_gate_up", "w_down")
TRANSPOSED = ("w_in", "w_xkv", "w_gate_up")
SMALL = ("mem_norm_g", "mix_pre_g", "mix_post_g", "gm_v_g", "gm_w_s", "gm_b_s", "pool_w", "pool_scale", "attn_sinks",
         "x_pre_g", "x_post_g", "ffn_pre_g", "ffn_post_g")
WEIGHTS = ("mem_norm_g", "mix_pre_g", "mix_post_g", "w_in", "gm_v_g", "gm_w_s", "gm_b_s", "pool_w", "pool_scale", "attn_sinks", "w_o",
           "x_pre_g", "x_post_g", "w_xq", "w_xkv", "w_xo", "ffn_pre_g", "ffn_post_g", "w_gate_up", "w_down")
PACK_QUANTUM = 8 * 128


def _pack(arrays):
    flat = []
    for a in arrays:
        a = a.reshape(-1).astype(F32)
        flat.append(jnp.pad(a, (0, -a.size % PACK_QUANTUM)))
    return jnp.concatenate(flat).reshape(-1, 128)


def _unpack(pack, shapes):
    out, row = [], 0
    for s in shapes:
        size = math.prod(s)
        rows = (size + (-size % PACK_QUANTUM)) // 128
        out.append(pack[row:row + rows].reshape(-1)[:size].reshape(s))
        row += rows
    return out


def _row(v):
    return v.reshape(1, -1)


def _local_step(x, mem, positions, target, small, weight, fwd_hook=None, bwd_hook=None):
    fwd_hook = fwd_hook or (lambda l, point, v: v)
    bwd_hook = bwd_hook or (lambda l, point, v, gb: v)
    depth = small["mix_pre_g"].shape[0]
    half = HEAD_DIM // 2
    inv = ROPE_THETA ** (-jnp.arange(half, dtype=F32) / half)
    cos_t, sin_t = _rope_tables(positions.reshape(-1, 1), jnp.tile(inv, 128 // half).reshape(1, 128), name="rope_tables")
    mem_g = _row(small["mem_norm_g"])
    saved = []
    for l in range(depth):
        w = functools.partial(weight, l)
        bfull = jnp.repeat(small["gm_b_s"][l].T, HEAD_DIM, axis=1)
        bd = jax.scipy.linalg.block_diag(*[small["pool_w"][l, g] for g in range(len(POOL_WINDOWS))])
        mix_par = (_row(small["gm_v_g"][l]), small["gm_w_s"][l], bfull, bd, _row(small["pool_scale"][l]), small["attn_sinks"][l])
        if l == 0:
            z, h1 = _mm_fwd(fwd_hook(l, 0, x), w("w_in"), nt=True, pre="norm", g_pre=_row(small["mix_pre_g"][l]), name="first_in_proj")
        else:
            h1 = fwd_hook(l, 0, h1)
            z, = _mm_fwd(h1, w("w_in"), nt=True, name="in_proj")
        cat = fwd_hook(l, 1, _mixer_fwd(z, cos_t, sin_t, *mix_par, name="mixer_fwd"))
        mix, x1, h2 = _mm_fwd(cat, w("w_o"), nt=False, post="norm_res", g_post=_row(small["mix_post_g"][l]), xres=x,
                              g_next=_row(small["x_pre_g"][l]), name="mix_out")
        qx, = _mm_fwd(h2, w("w_xq"), nt=False, name="xq_proj")
        kv, memn = _mm_fwd(mem, w("w_xkv"), nt=True, pre="norm", g_pre=mem_g, name="xkv_proj")
        ox = _xattn_fwd(qx, kv, name="xattn_fwd")
        xo, x2, h3 = _mm_fwd(ox, w("w_xo"), nt=False, post="norm_res", g_post=_row(small["x_post_g"][l]), xres=x1,
                             g_next=_row(small["ffn_pre_g"][l]), name="xattn_out")
        h3 = fwd_hook(l, 2, h3)
        gu, = _mm_fwd(h3, w("w_gate_up"), nt=True, name="ffn_in")
        last = l + 1 == depth
        f, x3, *h_next = _mm_fwd(gu, w("w_down"), nt=False, pre="swiglu", post="norm_res", g_post=_row(small["ffn_post_g"][l]), xres=x2,
                                 g_next=None if last else _row(small["mix_pre_g"][l + 1]), tile=256, name="last_ffn_out" if last else "ffn_out")
        saved.append((mix_par, x, z, h1, cat, mix, x1, qx, h2, kv, memn, ox, xo, x2, gu, h3, f))
        if last:
            x = fwd_hook(l, 3, x3)
        else:
            x, h1 = x3, fwd_hook(l, 3, h_next[0])

    gs = {n: [None] * depth for n in SMALL if n != "mem_norm_g"}
    gb = {n: [None] * depth for n in BIG}
    dmemn = [None] * depth
    loss, dx, df, gs["ffn_post_g"][depth - 1] = _loss_grad(x, target, saved[-1][-1], _row(small["ffn_post_g"][depth - 1]), name="loss")
    for l in reversed(range(depth)):
        w = functools.partial(weight, l)
        mix_par, x0, z, h1, cat, mix, x1, qx, h2, kv, memn, ox, xo, x2, gu, h3, f = saved[l]
        dgu = _ffn_out_bwd(df, w("w_down"), gu, name="ffn_out_bwd")
        gb["w_down"][l] = _wgrad(gu, df, swiglu=True, name="w_down_grad")
        gb["w_gate_up"][l] = _wgrad(dgu, h3, name="w_gate_up_grad")
        dx2, gs["ffn_pre_g"][l], dxo, gs["x_post_g"][l] = _mm_bwd_post(
            dgu, w("w_gate_up"), x2, _row(small["ffn_pre_g"][l]), dx, nt=False, below=(xo, _row(small["x_post_g"][l])), tile=256, name="ffn_in_bwd")
        dxo = bwd_hook(l, 1, dxo, gb)
        do, = _mm_fwd(dxo, w("w_xo"), nt=True, name="xattn_out_bwd")
        gb["w_xo"][l] = _wgrad(ox, dxo, name="w_xo_grad")
        dqx, dkv = _xattn_bwd(qx, kv, do, name="xattn_bwd")
        gb["w_xq"][l] = _wgrad(h2, dqx, name="w_xq_grad")
        dx1, gs["x_pre_g"][l], dmix, gs["mix_post_g"][l] = _mm_bwd_post(
            dqx, w("w_xq"), x1, _row(small["x_pre_g"][l]), dx2, nt=True, below=(mix, _row(small["mix_post_g"][l])), name="xq_proj_bwd")
        dmix = bwd_hook(l, 2, dmix, gb)
        gb["w_xkv"][l] = _wgrad(dkv, memn, name="w_xkv_grad")
        dmemn[l] = _mm_fwd(dkv, w("w_xkv"), nt=False, out_dtype=F32, name="xkv_proj_bwd")[0]
        dcat, = _mm_fwd(dmix, w("w_o"), nt=True, name="mix_out_bwd")
        gb["w_o"][l] = _wgrad(cat, dmix, name="w_o_grad")
        dz, dvg, dwc, dbf, dbd, dscale, dsink = _mixer_bwd(z, dcat, cos_t, sin_t, *mix_par, name="mixer_bwd")
        gs["gm_v_g"][l], gs["gm_w_s"][l], gs["pool_scale"][l], gs["attn_sinks"][l] = dvg, dwc, dscale, dsink[:, 0]
        gs["gm_b_s"][l] = dbf.reshape(BLOCK, -1, HEAD_DIM).sum(-1).T
        gs["pool_w"][l] = jnp.stack([dbd[HEAD_DIM * g:HEAD_DIM * (g + 1), HEAD_DIM * g:HEAD_DIM * (g + 1)] for g in range(len(POOL_WINDOWS))])
        gb["w_in"][l] = _wgrad(dz, h1, name="w_in_grad")
        if l == 0:
            dx, gs["mix_pre_g"][l] = _mm_bwd_post(dz, w("w_in"), x0, _row(small["mix_pre_g"][l]), dx1, nt=False, name="first_in_proj_bwd")
            dx = bwd_hook(l, 3, dx, gb)
        else:
            dx, gs["mix_pre_g"][l], df, gs["ffn_post_g"][l - 1] = _mm_bwd_post(
                dz, w("w_in"), x0, _row(small["mix_pre_g"][l]), dx1, nt=False, below=(saved[l - 1][-1], _row(small["ffn_post_g"][l - 1])),
                name="in_proj_bwd")
            df = bwd_hook(l, 3, df, gb)

    small_grads = {n: jnp.stack(v).reshape(small[n].shape) for n, v in gs.items()}
    small_grads["mem_norm_g"] = _norm_dgain(mem, jnp.stack(dmemn), name="mem_norm_grad").reshape(-1)
    return loss, dx, small_grads, gb


EARLY = ("w_in", "w_o", "w_xq", "w_xkv", "w_xo")
LATE = ("w_gate_up", "w_down")


def _step(a):
    depth = a["mix_pre_g"].shape[0]
    x, mem, target = a["x"][0], a["mem"][0], a["loss_target"][0]
    px, py, pc = _place()
    blocks = jnp.stack([4 * cx + 2 * cy + pc for cx, cy in (_chip(k, px, py) for k in range(4))]).astype(jnp.int32)
    flight = {}

    def laid_out(v, n):
        return v.transpose(0, 2, 1) if n in TRANSPOSED else v

    whole = {(l, n): _place_own(laid_out(a[n], n), l, blocks, name="place_" + n) for l in range(depth) for n in BIG}
    gather = {"first_in": [(0, "w_in")], "first_mid": [(0, n) for n in EARLY[1:]], "first_late": [(0, n) for n in LATE],
              "second_early": [(1, n) for n in EARLY], "second_late": [(1, n) for n in LATE]}
    gather.update({f"layer{l}": [(l, n) for n in BIG] for l in range(2, depth)})
    forward_plan = {(0, 0): [("a", "first_in"), ("a", "first_mid"), ("a", "first_late"), ("a", "second_early"), ("a", "second_late"),
                             ("mid", "first_in"), ("end", "first_in")],
                    (0, 1): [("mid", "first_mid"), ("end", "first_mid")], (0, 2): [("mid", "first_late"), ("end", "first_late")],
                    (0, 3): [("mid", "second_early")], (1, 0): [("end", "second_early"), ("a", "layer2")],
                    (1, 1): [("mid", "second_late")], (1, 2): [("end", "second_late"), ("mid", "layer2")], (1, 3): [("end", "layer2")]}
    for l in range(2, depth):
        forward_plan.update({(l, 0): [("a", f"layer{l + 1}")], (l, 2): [("mid", f"layer{l + 1}")], (l, 3): [("end", f"layer{l + 1}")]})
    if depth == 1:
        gather = {tag: keys for tag, keys in gather.items() if tag.startswith("first")}

    issued = []

    def gather_step(kind, tag, v):
        keys = gather[tag]
        if kind == "a":
            arrays = [whole[k] for k in keys]
            arrays[0] = _tie(arrays[0], issued[-1:])
            send, recv, arrays, token = _start(_gather_copies("a"), arrays, [], 4 * len(keys), name=f"gather_a_{tag}")
            flight[tag] = (send, recv, arrays)
            issued.append(token)
            return [token]
        send, recv, arrays = flight.pop(tag)
        if kind == "mid":
            arrays = _finish(_gather_copies("a"), send, recv, arrays, [v], name=f"gather_a_done_{tag}")
            send, recv, arrays, token = _start(_gather_copies("b"), arrays, [], 3 * len(keys), name=f"gather_b_{tag}")
            flight[tag] = (send, recv, arrays)
            return [token]
        for key, arr in zip(keys, _finish(_gather_copies("b"), send, recv, arrays, [v], name=f"gather_b_done_{tag}")):
            whole[key] = arr
        return []

    def fwd_hook(l, point, v):
        tokens = []
        for kind, tag in forward_plan.get((l, point), []):
            if tag in gather:
                tokens += gather_step(kind, tag, v)
        return _tie(v, tokens)

    mine = {}
    scatter = {f"layer{l}": [(l, n) for n in BIG] for l in range(1, depth)}
    scatter.update({"late0": [(0, n) for n in LATE], "attn0": [(0, "w_xq"), (0, "w_xo")], "rest0": [(0, "w_in"), (0, "w_o"), (0, "w_xkv")]})
    backward_plan = {}
    for l in range(1, depth):
        backward_plan[(l, 3)] = [("a", f"layer{l}")]
        backward_plan[(l - 1, 1)] = [("mid", f"layer{l}")]
        backward_plan[(l - 1, 2)] = [("end", f"layer{l}")]
    backward_plan[(0, 1)] = [("a", "late0")] + backward_plan.get((0, 1), [])
    backward_plan[(0, 2)] = [("a", "attn0"), ("mid", "late0")] + backward_plan.get((0, 2), [])
    backward_plan[(0, 3)] = [("end", "late0"), ("mid", "attn0"), ("a", "rest0")]

    def scatter_step(kind, tag, v, gb):
        keys = scatter[tag]
        n = len(keys)
        if kind == "a":
            grads = [gb[name][l] for l, name in keys]
            zones = [jax.ShapeDtypeStruct((4, g.shape[0] // N_DEV, g.shape[1]), F32) for g in grads]
            send, recv, arrays, token = _start(_scatter_copies("a", n), grads, zones, 4 * n, name=f"scatter_a_{tag}")
            flight[tag] = (send, recv, arrays, None)
            return [token]
        send, recv, arrays, kept = flight.pop(tag)
        if kind == "mid":
            arrays = _finish(_scatter_copies("a", n), send, recv, arrays, [v], name=f"scatter_a_done_{tag}")
            sums = [_chip_sum(arrays[i], arrays[n + i], blocks, name="chip_sum") for i in range(n)]
            zones = [jax.ShapeDtypeStruct(s[1].shape, BF16) for s in sums]
            send, recv, arrays, token = _start(_scatter_copies("b", n), [s[1] for s in sums], zones, 3 * n, name=f"scatter_b_{tag}")
            flight[tag] = (send, recv, arrays, [s[0] for s in sums])
            return [token]
        arrays = _finish(_scatter_copies("b", n), send, recv, arrays, [v], name=f"scatter_b_done_{tag}")
        for i, key in enumerate(keys):
            mine[key] = _final_sum(kept[i], arrays[n + i], name="final_sum")
        return []

    def bwd_hook(l, point, v, gb):
        tokens = []
        for kind, tag in backward_plan.get((l, point), []):
            tokens += scatter_step(kind, tag, v, gb)
        return _tie(v, tokens)

    loss, dx, small_grads, _ = _local_step(x, mem, a["positions"][0], target, {n: a[n] for n in SMALL}, lambda l, n: whole[(l, n)],
                                           fwd_hook, bwd_hook)

    out = {}

    def update(n, tokens):
        g = _tie(jnp.stack([mine[(l, n)] for l in range(depth)]), tokens)
        moved = _adamw(laid_out(a[n], n), g, laid_out(a["m_" + n], n), laid_out(a["v_" + n], n), name="adamw_" + n)
        for p, v in zip(("grad_", "delta_", "new_m_", "new_v_"), (g, *moved)):
            out[p + n] = laid_out(v, n)
        return moved[0]

    part = _tie(_pack([small_grads[n] for n in SMALL]), [dx])
    placed = _place_own(part[None], 0, blocks, dtype=F32, name="place_small_grads")
    send, recv, arrays, token = _start(_gather_copies("a"), [placed], [], 4, name="gather_a_small_grads")
    behind = [update(n, [token]) for n in LATE]
    tokens = scatter_step("mid", "rest0", behind[-1], None)
    arrays = _finish(_gather_copies("a"), send, recv, arrays, behind, name="gather_a_done_small_grads")
    send, recv, arrays, token = _start(_gather_copies("b"), arrays, [], 3, name="gather_b_small_grads")
    scatter_step("end", "attn0", behind[-1], None)
    behind = [update(n, tokens + [token]) for n in ("w_xq", "w_xo")]
    parts = _finish(_gather_copies("b"), send, recv, arrays, behind, name="gather_b_done_small_grads")[0].reshape(N_DEV, *part.shape)
    packs = _small_update(parts, *[_pack([a[p + n] for n in SMALL]) for p in ("", "m_", "v_")], name="small_update")
    for p, pack in zip(("grad_", "delta_", "new_m_", "new_v_"), packs):
        for n, v in zip(SMALL, _unpack(pack, [a[n].shape for n in SMALL])):
            out[p + n] = v
    scatter_step("end", "rest0", packs[0], None)
    for n in ("w_in", "w_o", "w_xkv"):
        update(n, [])

    total = lax.psum(loss[0, 0], ("x", "y", "c"))
    return (total, dx[None], *[out[p + n] for p in ("grad_", "delta_", "new_m_", "new_v_") for n in WEIGHTS])


def kernel(x, mem, positions, mem_norm_g, mix_pre_g, mix_post_g, w_in, gm_v_g, gm_w_s, gm_b_s, pool_w, pool_scale, attn_sinks,
           w_o, x_pre_g, x_post_g, w_xq, w_xkv, w_xo, ffn_pre_g, ffn_post_g, w_gate_up, w_down, loss_target, m_mem_norm_g,
           m_mix_pre_g, m_mix_post_g, m_w_in, m_gm_v_g, m_gm_w_s, m_gm_b_s, m_pool_w, m_pool_scale, m_attn_sinks, m_w_o,
           m_x_pre_g, m_x_post_g, m_w_xq, m_w_xkv, m_w_xo, m_ffn_pre_g, m_ffn_post_g, m_w_gate_up, m_w_down, v_mem_norm_g,
           v_mix_pre_g, v_mix_post_g, v_w_in, v_gm_v_g, v_gm_w_s, v_gm_b_s, v_pool_w, v_pool_scale, v_attn_sinks, v_w_o,
           v_x_pre_g, v_x_post_g, v_w_xq, v_w_xkv, v_w_xo, v_ffn_pre_g, v_ffn_post_g, v_w_gate_up, v_w_down):
    return _step(dict(locals()))
```

```python
import functools
import math

import jax
import jax.numpy as jnp
from jax import lax
from jax.experimental import pallas as pl
from jax.experimental.pallas import tpu as pltpu

F32, BF16 = jnp.float32, jnp.bfloat16
EPS = 1e-6
HEAD_DIM = 64
BLOCK = 128
POOL_WINDOWS = (2, 4, 8, 16)
ATT_Q_HEADS = 8
X_HEADS = 4
ROPE_THETA = 10000.0
ADAM_LR, ADAM_B1, ADAM_B2, ADAM_EPS, ADAM_WD, ADAM_STEP = 0.001, 0.9, 0.999, 1e-08, 0.01, 10
N_DEV = 8
TOKEN_TILE = 512
VMEM_LIMIT_BYTES = 50 * 2**20
NEG = -1e30
MESH = pl.DeviceIdType.MESH

NN = ((1,), (0,))
NT = ((1,), (1,))
TN = ((0,), (0,))


def _dot(a, b, dims):
    return lax.dot_general(a.astype(BF16), b.astype(BF16), (dims, ((), ())), preferred_element_type=F32)


def _params(*sem):
    return pltpu.CompilerParams(dimension_semantics=sem, vmem_limit_bytes=VMEM_LIMIT_BYTES)


def _full(shape):
    return pl.BlockSpec(shape, lambda *_: (0,) * len(shape))


def _lane(shape):
    return lax.broadcasted_iota(jnp.int32, shape, len(shape) - 1)


def _rms_fwd(x, g):
    r = lax.rsqrt(jnp.mean(x * x, axis=-1, keepdims=True) + EPS)
    return x * r * g


def _rms_bwd(x, g, dy):
    r = lax.rsqrt(jnp.mean(x * x, axis=-1, keepdims=True) + EPS)
    xh = x * r
    dg = jnp.sum(dy * xh, axis=0, keepdims=True)
    dxh = dy * g
    dx = r * (dxh - xh * jnp.mean(dxh * xh, axis=-1, keepdims=True))
    return dx, dg


def _silu_parts(gate):
    sg = 1.0 / (1.0 + jnp.exp(-gate))
    return gate * sg, sg


def _swiglu_cols(gu_ref, out_ref, width):
    step = 128 * max(1, (width // 128) // 4)
    for c0 in range(0, width, step):
        c1 = min(width, c0 + step)
        gate = gu_ref[:, c0:c1].astype(F32)
        up = gu_ref[:, width + c0:width + c1].astype(F32)
        out_ref[:, c0:c1] = (_silu_parts(gate)[0] * up).astype(out_ref.dtype)


def _col_tile(n, cap=1408):
    best = n
    for t in range(128, cap + 1, 128):
        if n % t == 0:
            best = t
    return best if n > cap else n


def _mm_fwd(a, w, *, nt, name, pre=None, g_pre=None, post=None, g_post=None, xres=None, g_next=None, out_dtype=BF16, tile=TOKEN_TILE):
    S = a.shape[0]
    n_out, k = (w.shape[0], w.shape[1]) if nt else (w.shape[1], w.shape[0])
    T = min(tile, S)
    tn = n_out if post else _col_tile(n_out)
    grid = (S // T, n_out // tn)

    def body(*refs):
        it = iter(refs)
        a_ref, w_ref = next(it), next(it)
        gpre_ref = next(it) if pre == "norm" else None
        gpost_ref, x_ref = (next(it), next(it)) if post else (None, None)
        gnext_ref = next(it) if g_next is not None else None
        outs = [next(it) for _ in range((2 if post else 1) + (1 if pre == "norm" else 0))]
        hnext_ref = next(it) if g_next is not None else None
        as_ref = next(it) if pre else None
        if pre:
            @pl.when(pl.program_id(1) == 0)
            def _():
                if pre == "norm":
                    h = _rms_fwd(a_ref[...], gpre_ref[...]).astype(BF16)
                    outs[-1][...] = h
                    as_ref[...] = h
                else:
                    _swiglu_cols(a_ref, as_ref, k)
            av = as_ref[...]
        else:
            av = a_ref[...]
        acc = _dot(av, w_ref[...], NT if nt else NN)
        if post:
            outs[0][...] = acc.astype(BF16)
            x_new = x_ref[...] + _rms_fwd(acc, gpost_ref[...])
            outs[1][...] = x_new
            if g_next is not None:
                hnext_ref[...] = _rms_fwd(x_new, gnext_ref[...]).astype(BF16)
        else:
            outs[0][...] = acc.astype(out_dtype)

    in_specs = [pl.BlockSpec((T, a.shape[1]), lambda i, j: (i, 0)),
                pl.BlockSpec((tn, k), lambda i, j: (j, 0)) if nt else pl.BlockSpec((k, tn), lambda i, j: (0, j))]
    args = [a, w]
    if pre == "norm":
        in_specs.append(_full((1, k)))
        args.append(g_pre)
    if post:
        in_specs += [_full((1, n_out)), pl.BlockSpec((T, n_out), lambda i, j: (i, 0))]
        args += [g_post, xres]
    if g_next is not None:
        in_specs.append(_full((1, n_out)))
        args.append(g_next)
    out_block = pl.BlockSpec((T, tn), lambda i, j: (i, j))
    if post:
        out_shape = [jax.ShapeDtypeStruct((S, n_out), BF16), jax.ShapeDtypeStruct((S, n_out), F32)]
        out_specs = [out_block, out_block]
    else:
        out_shape = [jax.ShapeDtypeStruct((S, n_out), out_dtype)]
        out_specs = [out_block]
    if pre == "norm":
        out_shape.append(jax.ShapeDtypeStruct((S, k), BF16))
        out_specs.append(pl.BlockSpec((T, k), lambda i, j: (i, 0)))
    if g_next is not None:
        out_shape.append(jax.ShapeDtypeStruct((S, n_out), BF16))
        out_specs.append(out_block)
    scratch = [pltpu.VMEM((T, k), BF16)] if pre else []
    return pl.pallas_call(body, name=name, grid=grid, in_specs=in_specs, out_specs=out_specs, out_shape=out_shape,
                          scratch_shapes=scratch, compiler_params=_params("parallel", "arbitrary"))(*args)


def _mm_bwd_post(dy, w, x, g, dx_in, *, nt, name, below=None, tile=TOKEN_TILE):
    S, k = dy.shape
    n = x.shape[1]
    T = min(tile, S)

    def body(*refs):
        dy_ref, w_ref, x_ref, g_ref, dxi_ref = refs[:5]
        m_ref, gb_ref = refs[5:7] if below else (None, None)
        dxo_ref, dg_ref = refs[-4:-2] if below else refs[-2:]

        @pl.when(pl.program_id(0) == 0)
        def _():
            dg_ref[...] = jnp.zeros_like(dg_ref)
            if below:
                refs[-1][...] = jnp.zeros_like(refs[-1])

        dh = _dot(dy_ref[...], w_ref[...], NT if nt else NN)
        dxn, dg = _rms_bwd(x_ref[...], g_ref[...], dh)
        dg_ref[...] += dg
        dxo = dxi_ref[...] + dxn
        dxo_ref[...] = dxo
        if below:
            dmv, dgb = _rms_bwd(m_ref[...].astype(F32), gb_ref[...], dxo)
            refs[-2][...] = dmv.astype(BF16)
            refs[-1][...] += dgb

    row = pl.BlockSpec((T, n), lambda i: (i, 0))
    in_specs = [pl.BlockSpec((T, k), lambda i: (i, 0)), _full(w.shape), row, _full((1, n)), row]
    out_specs = [row, _full((1, n))]
    out_shape = [jax.ShapeDtypeStruct((S, n), F32), jax.ShapeDtypeStruct((1, n), F32)]
    if below:
        in_specs += [row, _full((1, n))]
        out_specs += [row, _full((1, n))]
        out_shape += [jax.ShapeDtypeStruct((S, n), BF16), jax.ShapeDtypeStruct((1, n), F32)]
    return pl.pallas_call(body, name=name, grid=(S // T,), in_specs=in_specs, out_specs=out_specs, out_shape=out_shape,
                          compiler_params=_params("arbitrary"))(dy, w, x, g, dx_in, *(below or ()))


def _wgrad(a, b, *, name, swiglu=False, tile=TOKEN_TILE):
    S = a.shape[0]
    k = a.shape[1] // 2 if swiglu else a.shape[1]
    n = b.shape[1]
    T = min(tile, S)
    tk = 512 if k % 512 == 0 and k > 1536 else 256
    nk = k // tk

    def body(*refs):
        if swiglu:
            gate_ref, up_ref, b_ref, o_ref, act_ref = refs
            for r0 in range(0, S, T):
                gate, up = gate_ref[r0:r0 + T, :].astype(F32), up_ref[r0:r0 + T, :].astype(F32)
                act_ref[r0:r0 + T, :] = (_silu_parts(gate)[0] * up).astype(BF16)
            av = act_ref[...]
        else:
            a_ref, b_ref, o_ref = refs
            av = a_ref[...]
        o_ref[...] = _dot(av, b_ref[...], TN)

    in_specs = [pl.BlockSpec((S, tk), lambda kk: (0, kk))]
    args = [a]
    if swiglu:
        in_specs.append(pl.BlockSpec((S, tk), lambda kk: (0, kk + nk)))
        args.append(a)
    in_specs.append(_full((S, n)))
    args.append(b)
    return pl.pallas_call(body, name=name, grid=(nk,), in_specs=in_specs,
                          out_specs=pl.BlockSpec((tk, n), lambda kk: (kk, 0)),
                          out_shape=jax.ShapeDtypeStruct((k, n), F32),
                          scratch_shapes=[pltpu.VMEM((S, tk), BF16)] if swiglu else [],
                          compiler_params=_params("parallel"))(*args)


def _ffn_out_bwd(dm, w, gu, *, name, tile=256):
    S, n = dm.shape
    f = w.shape[0]
    T = min(tile, S)

    def body(dm_ref, w_ref, gu_ref, dgu_ref, da_ref):
        da_ref[...] = _dot(dm_ref[...], w_ref[...], NT)
        step = 128 * max(1, (f // 128) // 4)
        for c0 in range(0, f, step):
            c1 = min(f, c0 + step)
            gate = gu_ref[:, c0:c1].astype(F32)
            up = gu_ref[:, f + c0:f + c1].astype(F32)
            da = da_ref[:, c0:c1]
            sl, sg = _silu_parts(gate)
            dgu_ref[:, c0:c1] = (da * up * (sg + sl * (1.0 - sg))).astype(BF16)
            dgu_ref[:, f + c0:f + c1] = (da * sl).astype(BF16)

    row = pl.BlockSpec((T, n), lambda i: (i, 0))
    wide = pl.BlockSpec((T, 2 * f), lambda i: (i, 0))
    return pl.pallas_call(
        body, name=name, grid=(S // T,), in_specs=[row, _full((f, n)), wide], out_specs=wide,
        out_shape=jax.ShapeDtypeStruct((S, 2 * f), BF16),
        scratch_shapes=[pltpu.VMEM((T, f), F32)], compiler_params=_params("parallel"))(dm, w, gu)


def _loss_grad(y, target, m, g, *, name, tile=TOKEN_TILE):
    S, d = y.shape
    T = min(tile, S)
    steps = S // T

    def body(y_ref, t_ref, m_ref, g_ref, loss_ref, dy_ref, dm_ref, dg_ref, acc_ref):
        i = pl.program_id(0)

        @pl.when(i == 0)
        def _():
            acc_ref[...] = jnp.zeros_like(acc_ref)
            dg_ref[...] = jnp.zeros_like(dg_ref)

        diff = y_ref[...] - t_ref[...]
        dy = diff * (1.0 / d)
        dy_ref[...] = dy
        dmv, dg = _rms_bwd(m_ref[...].astype(F32), g_ref[...], dy)
        dm_ref[...] = dmv.astype(BF16)
        dg_ref[...] += dg
        acc_ref[...] += jnp.sum(diff * diff, axis=0, keepdims=True)

        @pl.when(i == steps - 1)
        def _():
            total = jnp.sum(acc_ref[...], axis=1, keepdims=True) * (0.5 / d)
            loss_ref[...] = jnp.broadcast_to(total, loss_ref.shape)

    row = pl.BlockSpec((T, d), lambda i: (i, 0))
    return pl.pallas_call(
        body, name=name, grid=(steps,), in_specs=[row, row, row, _full((1, d))], out_specs=[_full((1, 128)), row, row, _full((1, d))],
        out_shape=[jax.ShapeDtypeStruct((1, 128), F32), jax.ShapeDtypeStruct((S, d), F32), jax.ShapeDtypeStruct((S, d), BF16),
                   jax.ShapeDtypeStruct((1, d), F32)],
        scratch_shapes=[pltpu.VMEM((1, d), F32)], compiler_params=_params("arbitrary"))(y, target, m, g)


def _norm_dgain(x, dys, *, name):
    def body(x_ref, dy_ref, o_ref):
        xv = x_ref[...]
        dy = dy_ref[0]
        for l in range(1, dys.shape[0]):
            dy = dy + dy_ref[l]
        r = lax.rsqrt(jnp.mean(xv * xv, axis=-1, keepdims=True) + EPS)
        o_ref[...] = jnp.sum(dy * xv * r, axis=0, keepdims=True)

    return pl.pallas_call(body, name=name, out_shape=jax.ShapeDtypeStruct((1, x.shape[1]), F32),
                          compiler_params=pltpu.CompilerParams(vmem_limit_bytes=VMEM_LIMIT_BYTES))(x, dys)


def _xattn_probs(q_ref, kv_ref, h, d, hd):
    q = q_ref[:, h * hd:(h + 1) * hd]
    kh = kv_ref[:, h * hd:(h + 1) * hd]
    vh = kv_ref[:, d + h * hd:d + (h + 1) * hd]
    s = _dot(q, kh, NT) * (hd ** -0.5)
    e = jnp.exp(s - jnp.max(s, axis=-1, keepdims=True))
    return q, kh, vh, e / jnp.sum(e, axis=-1, keepdims=True)


def _xattn_fwd(q, kv, *, name, tile=TOKEN_TILE):
    S, d = q.shape
    mlen = kv.shape[0]
    hd = d // X_HEADS
    T = min(tile, S)

    def body(q_ref, kv_ref, o_ref):
        for h in range(X_HEADS):
            _, _, vh, p = _xattn_probs(q_ref, kv_ref, h, d, hd)
            o_ref[:, h * hd:(h + 1) * hd] = _dot(p, vh, NN).astype(BF16)

    row = pl.BlockSpec((T, d), lambda i: (i, 0))
    return pl.pallas_call(body, name=name, grid=(S // T,), in_specs=[row, _full((mlen, 2 * d))], out_specs=row,
                          out_shape=jax.ShapeDtypeStruct((S, d), BF16), compiler_params=_params("parallel"))(q, kv)


def _xattn_bwd(q, kv, do, *, name, tile=TOKEN_TILE):
    S, d = q.shape
    mlen = kv.shape[0]
    hd = d // X_HEADS
    T = min(tile, S)

    def body(q_ref, kv_ref, do_ref, dq_ref, dkv_ref):
        @pl.when(pl.program_id(0) == 0)
        def _():
            dkv_ref[...] = jnp.zeros_like(dkv_ref)

        for h in range(X_HEADS):
            cols = slice(h * hd, (h + 1) * hd)
            qh, kh, vh, doh = q_ref[:, cols], kv_ref[:, cols], kv_ref[:, d + h * hd:d + (h + 1) * hd], do_ref[:, cols]
            st = _dot(kh, qh, NT) * (hd ** -0.5)
            e = jnp.exp(st - jnp.max(st, axis=0, keepdims=True))
            pt = e / jnp.sum(e, axis=0, keepdims=True)
            dpt = _dot(vh, doh, NT)
            dst = (pt * (dpt - jnp.sum(pt * dpt, axis=0, keepdims=True)) * (hd ** -0.5)).astype(BF16)
            dkv_ref[:, cols] += _dot(dst, qh, NN)
            dkv_ref[:, d + h * hd:d + (h + 1) * hd] += _dot(pt, doh, NN)
            dq_ref[:, cols] = _dot(dst, kh, TN).astype(BF16)

    row = pl.BlockSpec((T, d), lambda i: (i, 0))
    return pl.pallas_call(
        body, name=name, grid=(S // T,), in_specs=[row, _full((mlen, 2 * d)), row],
        out_specs=[row, _full((mlen, 2 * d))],
        out_shape=[jax.ShapeDtypeStruct((S, d), BF16), jax.ShapeDtypeStruct((mlen, 2 * d), F32)],
        compiler_params=_params("arbitrary"))(q, kv, do)


GM0, POOL0, Q0, KV0, Z_END = 0, 512, 768, 1280, 1536


def _rope_tables(pos_col, inv_row, *, name):
    S = pos_col.shape[0]

    def body(p_ref, inv_ref, cos_ref, sin_ref):
        ang = p_ref[...].astype(F32) * inv_ref[...]
        sin = jnp.sin(ang)
        cos_ref[...] = jnp.cos(ang)
        sin_ref[...] = jnp.where(_lane(ang.shape) % HEAD_DIM < HEAD_DIM // 2, -sin, sin)

    T = min(TOKEN_TILE, S)
    return pl.pallas_call(
        body, name=name, grid=(S // T,),
        in_specs=[pl.BlockSpec((T, 1), lambda i: (i, 0)), _full((1, 128))],
        out_specs=[pl.BlockSpec((T, 128), lambda i: (i, 0))] * 2,
        out_shape=[jax.ShapeDtypeStruct((S, 128), F32)] * 2, compiler_params=_params("parallel"))(pos_col, inv_row)


def _swap_halves(x):
    n = x.shape[-1]
    return jnp.where(_lane(x.shape) % HEAD_DIM < HEAD_DIM // 2, pltpu.roll(x, n - HEAD_DIM // 2, 1), pltpu.roll(x, HEAD_DIM // 2, 1))


def _rope(x, cos, sin_s):
    reps = x.shape[-1] // 128
    if reps > 1:
        cos, sin_s = jnp.tile(cos, (1, reps)), jnp.tile(sin_s, (1, reps))
    return x * cos + _swap_halves(x) * sin_s


def _rope_bwd(dy, cos, sin_s):
    reps = dy.shape[-1] // 128
    if reps > 1:
        cos, sin_s = jnp.tile(cos, (1, reps)), jnp.tile(sin_s, (1, reps))
    return dy * cos + _swap_halves(dy * sin_s)


def _gelu_parts(x):
    c = math.sqrt(2.0 / math.pi)
    t = jnp.tanh(c * (x + 0.044715 * x * x * x))
    return 0.5 * x * (1.0 + t), t


def _gelu_grad(x, t):
    c = math.sqrt(2.0 / math.pi)
    return 0.5 * (1.0 + t) + 0.5 * x * (1.0 - t * t) * c * (1.0 + 3.0 * 0.044715 * x * x)


def _group_sum(x):
    gid = _lane(x.shape) // HEAD_DIM
    out = jnp.zeros_like(x)
    for g in range(x.shape[-1] // HEAD_DIM):
        sel = gid == g
        out = jnp.where(sel, jnp.sum(jnp.where(sel, x, 0.0), axis=-1, keepdims=True), out)
    return out


def _low_half(shape):
    return _lane(shape) % 128 < HEAD_DIM


def _tril(w):
    r = lax.broadcasted_iota(jnp.int32, w.shape, 0)
    c = lax.broadcasted_iota(jnp.int32, w.shape, 1)
    return jnp.where(r >= c, w, 0.0)


def _triu(w):
    r = lax.broadcasted_iota(jnp.int32, w.shape, 0)
    c = lax.broadcasted_iota(jnp.int32, w.shape, 1)
    return jnp.where(r <= c, w, 0.0)


def _gating_fwd(zg, vg, wc_ref, bfull):
    ge, th = _gelu_parts(zg)
    u, v = ge[:, :256], ge[:, 256:]
    r = lax.rsqrt(_group_sum(v * v) * (1.0 / HEAD_DIM) + EPS)
    vhat = v * r
    vn = vhat * vg
    low = _low_half((BLOCK, 128))
    blocks = []
    for blk in range(2):
        vb = vn[:, 128 * blk:128 * (blk + 1)]
        m0 = _dot(_tril(wc_ref[2 * blk]), vb, NN)
        m1 = _dot(_tril(wc_ref[2 * blk + 1]), vb, NN)
        blocks.append(jnp.where(low, m0, m1))
    mixed = jnp.concatenate(blocks, axis=1) + bfull
    return u * mixed, (th, u, r, vhat, vn, mixed)


def _band(w, transpose):
    if transpose:
        s2 = lax.broadcasted_iota(jnp.int32, (2 * BLOCK, BLOCK), 0)
        t = lax.broadcasted_iota(jnp.int32, (2 * BLOCK, BLOCK), 1)
    else:
        t = lax.broadcasted_iota(jnp.int32, (BLOCK, 2 * BLOCK), 0)
        s2 = lax.broadcasted_iota(jnp.int32, (BLOCK, 2 * BLOCK), 1)
    d = t - s2 + BLOCK
    return jnp.where((d >= 0) & (d < w), 1.0, 0.0).astype(BF16)


def _inv_count(n):
    t = lax.broadcasted_iota(jnp.int32, (BLOCK, 256), 0) + n * BLOCK
    gid = _lane((BLOCK, 256)) // HEAD_DIM
    win = jnp.where(gid == 0, POOL_WINDOWS[0], jnp.where(gid == 1, POOL_WINDOWS[1], jnp.where(gid == 2, POOL_WINDOWS[2], POOL_WINDOWS[3])))
    return 1.0 / jnp.minimum(t + 1, win).astype(F32)


def _pool_fwd(p, pp, n, bd, scale):
    low = _low_half((BLOCK, 128))
    blocks = []
    for blk in range(2):
        both = jnp.concatenate([pp[:, 128 * blk:128 * (blk + 1)], p[:, 128 * blk:128 * (blk + 1)]], axis=0)
        s0 = _dot(_band(POOL_WINDOWS[2 * blk], False), both, NN)
        s1 = _dot(_band(POOL_WINDOWS[2 * blk + 1], False), both, NN)
        blocks.append(jnp.where(low, s0, s1))
    inv = _inv_count(n)
    pooled = jnp.concatenate(blocks, axis=1) * inv - p
    mapped = _dot(pooled, bd, NN)
    return mapped * scale, (inv, pooled, mapped)


def _mixer_fwd(z, cos_t, sin_t, vg, wc, bfull, bd, scale, sinks, *, name):
    S = z.shape[0]
    nb = S // BLOCK

    def body(zc_ref, zpp_ref, zpk_ref, cc_ref, cp_ref, sc_ref, sp_ref, vg_ref, wc_ref, bf_ref, bd_ref, scale_ref, sinks_ref, out_ref):
        n = pl.program_id(0)
        keep = (n > 0).astype(F32)
        a, _ = _gating_fwd(zc_ref[:, GM0:POOL0].astype(F32), vg_ref[...], wc_ref, bf_ref[...])
        out_ref[:, 0:256] = a.astype(BF16)
        b, _ = _pool_fwd(zc_ref[:, POOL0:Q0].astype(F32), zpp_ref[...].astype(F32) * keep, n, bd_ref[...], scale_ref[...])
        out_ref[:, 256:512] = b.astype(BF16)
        cos_c, sin_c = cc_ref[...], sc_ref[...]
        qrot = _rope(zc_ref[:, Q0:KV0].astype(F32), cos_c, sin_c)
        kk = jnp.concatenate([_rope(zpk_ref[:, 0:128].astype(F32), cp_ref[...], sp_ref[...]), _rope(zc_ref[:, KV0:KV0 + 128].astype(F32), cos_c, sin_c)], axis=0)
        vv = jnp.concatenate([zpk_ref[:, 128:256].astype(F32), zc_ref[:, KV0 + 128:Z_END].astype(F32)], axis=0)
        kk, vv = kk.astype(BF16), vv.astype(BF16)
        tall = (ATT_Q_HEADS // 2 * BLOCK, 2 * BLOCK)
        qi = lax.broadcasted_iota(jnp.int32, tall, 0) % BLOCK
        ki = lax.broadcasted_iota(jnp.int32, tall, 1)
        valid = (ki > qi) & (ki <= qi + BLOCK) & (ki >= jnp.where(n > 0, 0, BLOCK))
        low = _low_half((BLOCK, 128))
        q_slabs = [qrot[:, 128 * s:128 * (s + 1)] for s in range(ATT_Q_HEADS // 2)]
        q_rolled = [pltpu.roll(v, HEAD_DIM, 1) for v in q_slabs]
        o_heads = [None] * ATT_Q_HEADS
        for h in range(2):
            lanes = ~low if h else low
            heads = range(4 * h, 4 * h + 4)
            qm = jnp.concatenate([jnp.where(lanes, (q_slabs if j % 2 == h else q_rolled)[j // 2], 0.0) for j in heads], axis=0)
            sc = jnp.where(valid, _dot(qm, kk, NT) * (HEAD_DIM ** -0.5), NEG)
            p = []
            for i, j in enumerate(heads):
                sch, sink = sc[BLOCK * i:BLOCK * (i + 1)], sinks_ref[j]
                mx = jnp.maximum(jnp.max(sch, axis=-1, keepdims=True), sink)
                e = jnp.exp(sch - mx)
                p.append((e / (jnp.sum(e, axis=-1, keepdims=True) + jnp.exp(sink - mx))).astype(BF16))
            o = _dot(jnp.concatenate(p, axis=0), vv, NN)
            for i, j in enumerate(heads):
                oh = o[BLOCK * i:BLOCK * (i + 1)]
                o_heads[j] = oh if j % 2 == h else pltpu.roll(oh, HEAD_DIM, 1)
        for s in range(ATT_Q_HEADS // 2):
            out_ref[:, 512 + 128 * s:512 + 128 * (s + 1)] = jnp.where(low, o_heads[2 * s], o_heads[2 * s + 1]).astype(BF16)

    cur = lambda n: (n, 0)
    prev = lambda n: (jnp.maximum(n - 1, 0), 0)
    in_specs = [
        pl.BlockSpec((BLOCK, Z_END), cur),
        pl.BlockSpec((BLOCK, 256), lambda n: (jnp.maximum(n - 1, 0), POOL0 // 256)),
        pl.BlockSpec((BLOCK, 256), lambda n: (jnp.maximum(n - 1, 0), KV0 // 256)),
        pl.BlockSpec((BLOCK, 128), cur), pl.BlockSpec((BLOCK, 128), prev),
        pl.BlockSpec((BLOCK, 128), cur), pl.BlockSpec((BLOCK, 128), prev),
        _full((1, 256)), _full((4, BLOCK, BLOCK)), _full((BLOCK, 256)), _full((256, 256)), _full((1, 256)),
        pl.BlockSpec(memory_space=pltpu.SMEM),
    ]
    return pl.pallas_call(
        body, name=name, grid=(nb,), in_specs=in_specs, out_specs=pl.BlockSpec((BLOCK, 1024), cur),
        out_shape=jax.ShapeDtypeStruct((S, 1024), BF16),
        compiler_params=_params("parallel"))(z, z, z, cos_t, cos_t, sin_t, sin_t, vg, wc, bfull, bd, scale, sinks)


def _mixer_bwd(z, dcat, cos_t, sin_t, vg, wc, bfull, bd, scale, sinks, *, name):
    S = z.shape[0]
    nb = S // BLOCK

    def body(zc_ref, zpp_ref, zpk_ref, dc_ref, cc_ref, cp_ref, sc_ref, sp_ref, vg_ref, wc_ref, bf_ref, bd_ref, scale_ref, sinks_ref, wct_ref,
             dz_ref, dvg_ref, dwc_ref, dbf_ref, dbd_ref, dscale_ref, dsink_ref, carry_ref):
        n = pl.program_id(0)

        @pl.when(n == 0)
        def _():
            for ref in (dvg_ref, dwc_ref, dbf_ref, dbd_ref, dscale_ref, dsink_ref):
                ref[...] = jnp.zeros_like(ref)

        @pl.when(n < nb)
        def _():
            keep = (n > 0).astype(F32)
            low = _low_half((BLOCK, 128))
            zg = zc_ref[:, GM0:POOL0].astype(F32)
            vg_v = vg_ref[...]
            _, (th, u, r, vhat, vn, mixed) = _gating_fwd(zg, vg_v, wc_ref, bf_ref[...])
            da = dc_ref[:, 0:256].astype(F32)
            dmixed = da * u
            dbf_ref[...] += dmixed
            dvn = []
            for blk in range(2):
                dmb = dmixed[:, 128 * blk:128 * (blk + 1)]
                vb = vn[:, 128 * blk:128 * (blk + 1)]
                acc = None
                for half in range(2):
                    g = 2 * blk + half
                    dmg = jnp.where(low if half == 0 else ~low, dmb, 0.0)
                    dwc_ref[g] += _tril(_dot(dmg, vb, NT))
                    part = _dot(_triu(wct_ref[g]), dmg, NN)
                    acc = part if acc is None else acc + part
                dvn.append(acc)
            dvn = jnp.concatenate(dvn, axis=1)
            dvg_ref[...] += jnp.sum(dvn * vhat, axis=0, keepdims=True)
            dvhat = dvn * vg_v
            dv = r * (dvhat - vhat * (_group_sum(dvhat * vhat) * (1.0 / HEAD_DIM)))
            dge = jnp.concatenate([da * mixed, dv], axis=1)
            own_g = dge * _gelu_grad(zg, th)
            p = zc_ref[:, POOL0:Q0].astype(F32)
            bd_v, scale_v = bd_ref[...], scale_ref[...]
            _, (inv, pooled, mapped) = _pool_fwd(p, zpp_ref[...].astype(F32) * keep, n, bd_v, scale_v)
            db = dc_ref[:, 256:512].astype(F32)
            dscale_ref[...] += jnp.sum(db * mapped, axis=0, keepdims=True)
            dmapped = db * scale_v
            dbd_ref[...] += _dot(pooled, dmapped, TN)
            dpooled = _dot(dmapped, bd_v, NT)
            dps = dpooled * inv
            back = []
            for blk in range(2):
                dpb = dps[:, 128 * blk:128 * (blk + 1)]
                t0 = _dot(_band(POOL_WINDOWS[2 * blk], True), dpb, NN)
                t1 = _dot(_band(POOL_WINDOWS[2 * blk + 1], True), dpb, NN)
                back.append(jnp.where(jnp.concatenate([low, low], axis=0), t0, t1))
            back = jnp.concatenate(back, axis=1)
            halo_p = back[:BLOCK] * keep
            own_p = back[BLOCK:] - dpooled
            cos_c, sin_c, cos_p, sin_p = cc_ref[...], sc_ref[...], cp_ref[...], sp_ref[...]
            qrot = _rope(zc_ref[:, Q0:KV0].astype(F32), cos_c, sin_c)
            kk = jnp.concatenate([_rope(zpk_ref[:, 0:128].astype(F32), cos_p, sin_p), _rope(zc_ref[:, KV0:KV0 + 128].astype(F32), cos_c, sin_c)], axis=0)
            vv = jnp.concatenate([zpk_ref[:, 128:256].astype(F32), zc_ref[:, KV0 + 128:Z_END].astype(F32)], axis=0)
            kt = jnp.transpose(kk).astype(BF16)
            kk, vv = kk.astype(BF16), vv.astype(BF16)
            wide = (2 * BLOCK, ATT_Q_HEADS // 2 * BLOCK)
            ki = lax.broadcasted_iota(jnp.int32, wide, 0)
            qi = lax.broadcasted_iota(jnp.int32, wide, 1) % BLOCK
            valid_t = (ki > qi) & (ki <= qi + BLOCK) & (ki >= jnp.where(n > 0, 0, BLOCK))
            q_slabs = [qrot[:, 128 * s:128 * (s + 1)] for s in range(ATT_Q_HEADS // 2)]
            d_slabs = [dc_ref[:, 512 + 128 * s:512 + 128 * (s + 1)].astype(F32) for s in range(ATT_Q_HEADS // 2)]
            q_rolled = [pltpu.roll(v, HEAD_DIM, 1) for v in q_slabs]
            d_rolled = [pltpu.roll(v, HEAD_DIM, 1) for v in d_slabs]
            dkk = jnp.zeros((2 * BLOCK, 128), F32)
            dvv = jnp.zeros((2 * BLOCK, 128), F32)
            dq_heads = [None] * ATT_Q_HEADS
            for h in range(2):
                lanes = ~low if h else low
                heads = range(4 * h, 4 * h + 4)
                qm = jnp.concatenate([jnp.where(lanes, (q_slabs if j % 2 == h else q_rolled)[j // 2], 0.0) for j in heads], axis=0).astype(BF16)
                dom = jnp.concatenate([jnp.where(lanes, (d_slabs if j % 2 == h else d_rolled)[j // 2], 0.0) for j in heads], axis=0).astype(BF16)
                sink = jnp.concatenate([jnp.full((1, BLOCK), sinks_ref[j], F32) for j in heads], axis=1)
                sc = jnp.where(valid_t, _dot(kk, qm, NT) * (HEAD_DIM ** -0.5), NEG)
                mx = jnp.maximum(jnp.max(sc, axis=0, keepdims=True), sink)
                e = jnp.exp(sc - mx)
                es = jnp.exp(sink - mx)
                inv_den = 1.0 / (jnp.sum(e, axis=0, keepdims=True) + es)
                pt = e * inv_den
                dpt = _dot(vv, dom, NT)
                dvv = dvv + _dot(pt, dom, NN)
                dcol = jnp.sum(pt * dpt, axis=0, keepdims=True)
                dst = (pt * (dpt - dcol) * (HEAD_DIM ** -0.5)).astype(BF16)
                dsink = es * inv_den * dcol
                dkk = dkk + _dot(dst, qm, NN)
                dqt = _dot(kt, dst, NN)
                for i, j in enumerate(heads):
                    part = jnp.sum(dsink[:, BLOCK * i:BLOCK * (i + 1)], axis=1, keepdims=True)
                    dsink_ref[pl.ds(j, 1), :] = dsink_ref[pl.ds(j, 1), :] - jnp.broadcast_to(part, (1, 128))
                    dqh = jnp.transpose(dqt[:, BLOCK * i:BLOCK * (i + 1)])
                    dq_heads[j] = dqh if j % 2 == h else pltpu.roll(dqh, HEAD_DIM, 1)
            dqrot = jnp.concatenate([jnp.where(low, dq_heads[2 * s], dq_heads[2 * s + 1]) for s in range(ATT_Q_HEADS // 2)], axis=1)
            own_q = _rope_bwd(dqrot, cos_c, sin_c)
            own_k = _rope_bwd(dkk[BLOCK:], cos_c, sin_c)
            halo_k = _rope_bwd(dkk[:BLOCK], cos_p, sin_p) * keep
            own_v, halo_v = dvv[BLOCK:], dvv[:BLOCK] * keep

            @pl.when(n > 0)
            def _():
                dz_ref[:, GM0:POOL0] = carry_ref[:, GM0:POOL0].astype(BF16)
                dz_ref[:, POOL0:Q0] = (carry_ref[:, POOL0:Q0] + halo_p).astype(BF16)
                dz_ref[:, Q0:KV0] = carry_ref[:, Q0:KV0].astype(BF16)
                dz_ref[:, KV0:KV0 + 128] = (carry_ref[:, KV0:KV0 + 128] + halo_k).astype(BF16)
                dz_ref[:, KV0 + 128:Z_END] = (carry_ref[:, KV0 + 128:Z_END] + halo_v).astype(BF16)

            carry_ref[:, GM0:POOL0] = own_g
            carry_ref[:, POOL0:Q0] = own_p
            carry_ref[:, Q0:KV0] = own_q
            carry_ref[:, KV0:KV0 + 128] = own_k
            carry_ref[:, KV0 + 128:Z_END] = own_v

        @pl.when(n == nb)
        def _():
            dz_ref[...] = carry_ref[...].astype(BF16)

    cur = lambda n: (jnp.minimum(n, nb - 1), 0)
    prev = lambda n: (jnp.maximum(jnp.minimum(n, nb - 1) - 1, 0), 0)
    in_specs = [
        pl.BlockSpec((BLOCK, Z_END), cur),
        pl.BlockSpec((BLOCK, 256), lambda n: (jnp.maximum(jnp.minimum(n, nb - 1) - 1, 0), POOL0 // 256)),
        pl.BlockSpec((BLOCK, 256), lambda n: (jnp.maximum(jnp.minimum(n, nb - 1) - 1, 0), KV0 // 256)),
        pl.BlockSpec((BLOCK, 1024), cur),
        pl.BlockSpec((BLOCK, 128), cur), pl.BlockSpec((BLOCK, 128), prev),
        pl.BlockSpec((BLOCK, 128), cur), pl.BlockSpec((BLOCK, 128), prev),
        _full((1, 256)), _full((4, BLOCK, BLOCK)), _full((BLOCK, 256)), _full((256, 256)), _full((1, 256)),
        pl.BlockSpec(memory_space=pltpu.SMEM), _full((4, BLOCK, BLOCK)),
    ]
    out_specs = [pl.BlockSpec((BLOCK, Z_END), lambda n: (jnp.maximum(n - 1, 0), 0)),
                 _full((1, 256)), _full((4, BLOCK, BLOCK)), _full((BLOCK, 256)), _full((256, 256)), _full((1, 256)), _full((8, 128))]
    out_shape = [jax.ShapeDtypeStruct((S, Z_END), BF16), jax.ShapeDtypeStruct((1, 256), F32),
                 jax.ShapeDtypeStruct((4, BLOCK, BLOCK), F32), jax.ShapeDtypeStruct((BLOCK, 256), F32),
                 jax.ShapeDtypeStruct((256, 256), F32), jax.ShapeDtypeStruct((1, 256), F32), jax.ShapeDtypeStruct((8, 128), F32)]
    return pl.pallas_call(
        body, name=name, grid=(nb + 1,), in_specs=in_specs, out_specs=out_specs, out_shape=out_shape,
        scratch_shapes=[pltpu.VMEM((BLOCK, Z_END), F32)],
        compiler_params=_params("arbitrary"))(z, z, z, dcat, cos_t, cos_t, sin_t, sin_t, vg, wc, bfull, bd, scale, sinks, jnp.swapaxes(wc, 1, 2))


HBM = pl.BlockSpec(memory_space=pl.ANY)


def _place():
    return lax.axis_index("x"), lax.axis_index("y"), lax.axis_index("c")


def _chip(k, x, y):
    return (1 - x if k & 1 else x, 1 - y if k & 2 else y)


def _block_rows(ref, rows, dev):
    start = pl.multiple_of((4 * dev[0] + 2 * dev[1] + dev[2]) * rows, 8)
    return ref.at[pl.ds(start, rows), :]


_HBM_SPEC = pl.BlockSpec(memory_space=pltpu.HBM)
_SEM_SPEC = pl.BlockSpec(memory_space=pltpu.SEMAPHORE)
_SPLIT_PARAMS = dict(compiler_params=pltpu.CompilerParams(has_side_effects=pltpu.SideEffectType.DATAFLOW_SIDE_EFFECTING))


def _descriptors(copies, refs, send_sems, recv_sems):
    return [pltpu.make_async_remote_copy(src_ref=s, dst_ref=d, send_sem=send_sems.at[i], recv_sem=recv_sems.at[i], device_id=to, device_id_type=MESH)
            for i, (s, d, to) in enumerate(copies(refs))]


def _start(copies, arrays, fresh, n_copies, *, name):
    operands = [pltpu.with_memory_space_constraint(v, pltpu.HBM) for v in (*arrays, *[lax.empty(f.shape, f.dtype) for f in fresh])]
    n = len(operands)

    def body(*refs):
        for dma in _descriptors(copies, refs[:n], refs[n], refs[n + 1]):
            dma.start()
        refs[-1][...] = jnp.zeros_like(refs[-1])

    res = pl.pallas_call(
        body, name=name, in_specs=[_HBM_SPEC] * n,
        out_shape=(pltpu.SemaphoreType.DMA((n_copies,)), pltpu.SemaphoreType.DMA((n_copies,)),
                   *[pltpu.HBM(v.shape, v.dtype) for v in operands], jax.ShapeDtypeStruct((8, 128), F32)),
        out_specs=(_SEM_SPEC, _SEM_SPEC, *[_HBM_SPEC] * n, pl.BlockSpec(memory_space=pltpu.VMEM)),
        input_output_aliases={i: 2 + i for i in range(n)}, **_SPLIT_PARAMS)(*operands)
    return res[0], res[1], list(res[2:2 + n]), res[-1]


def _finish(copies, send_sems, recv_sems, arrays, after, *, name):
    n = len(arrays)

    def body(*refs):
        for dma in _descriptors(copies, refs[:n], refs[n], refs[n + 1]):
            dma.wait_send()
            dma.wait_recv()

    return list(pl.pallas_call(
        body, name=name, in_specs=[_HBM_SPEC] * n + [_SEM_SPEC, _SEM_SPEC] + [HBM] * len(after),
        out_shape=tuple(pltpu.HBM(v.shape, v.dtype) for v in arrays), out_specs=tuple([_HBM_SPEC] * n),
        input_output_aliases={i: i for i in range(n)}, **_SPLIT_PARAMS)(*arrays, send_sems, recv_sems, *after))


def _tie(value, tokens):
    if not tokens:
        return value

    def body(*refs):
        pass

    return pl.pallas_call(body, name="tie", in_specs=[HBM] * (1 + len(tokens)), out_specs=HBM,
                          out_shape=jax.ShapeDtypeStruct(value.shape, value.dtype), input_output_aliases={0: 0})(value, *tokens)


def _gather_copies(stage):
    def copies(refs):
        x, y, c = _place()
        me, sibling = (x, y, c), (x, y, 1 - c)
        out = []
        for ref in refs:
            r = ref.shape[0] // N_DEV
            if stage == "a":
                hops = [(me, sibling)] + [(me, (*_chip(k, x, y), c)) for k in (1, 2, 3)]
            else:
                hops = [((*_chip(k, x, y), c), sibling) for k in (1, 2, 3)]
            for block, to in hops:
                rows = _block_rows(ref, r, block)
                out.append((rows, rows, to))
        return out
    return copies


def _scatter_copies(stage, n):
    def copies(refs):
        x, y, c = _place()
        out = []
        for w in range(n):
            src, dst = refs[w], refs[n + w]
            if stage == "a":
                r = src.shape[0] // N_DEV
                out += [(_block_rows(src, r, (*_chip(k, x, y), 1 - c)), dst.at[k], (x, y, 1 - c)) for k in range(4)]
            else:
                out += [(src.at[k - 1], dst.at[k - 1], (*_chip(k, x, y), c)) for k in (1, 2, 3)]
        return out
    return copies


def _place_own(shards, layer, blocks, *, name, dtype=BF16):
    _, r, cdim = shards.shape

    def body(blocks_ref, s_ref, o_ref):
        o_ref[...] = s_ref[...].astype(dtype)

    return pl.pallas_call(
        body, name=name,
        grid_spec=pltpu.PrefetchScalarGridSpec(
            num_scalar_prefetch=1, grid=(1,), in_specs=[pl.BlockSpec((None, r, cdim), lambda i, blocks: (layer, 0, 0))],
            out_specs=pl.BlockSpec((r, cdim), lambda i, blocks: (blocks[0], 0))),
        out_shape=jax.ShapeDtypeStruct((N_DEV * r, cdim), dtype), compiler_params=_params("arbitrary"))(blocks, shards)


def _chip_sum(grad, got, blocks, *, name):
    _, r, cdim = got.shape
    tr = r if r <= 256 else r // (2 if r % 32 == 0 and r // 2 <= 256 else 4)
    steps = r // tr

    def body(blocks_ref, a0_ref, a1_ref, a2_ref, a3_ref, b_ref, mine_ref, away_ref):
        mine_ref[...] = a0_ref[...] + b_ref[0]
        for k, a_ref in ((1, a1_ref), (2, a2_ref), (3, a3_ref)):
            away_ref[k - 1] = (a_ref[...] + b_ref[k]).astype(BF16)

    own = [pl.BlockSpec((tr, cdim), lambda i, blocks, k=k: (blocks[k] * steps + i, 0)) for k in range(4)]
    return pl.pallas_call(
        body, name=name,
        grid_spec=pltpu.PrefetchScalarGridSpec(
            num_scalar_prefetch=1, grid=(steps,),
            in_specs=own + [pl.BlockSpec((4, tr, cdim), lambda i, blocks: (0, i, 0))],
            out_specs=[pl.BlockSpec((tr, cdim), lambda i, blocks: (i, 0)), pl.BlockSpec((3, tr, cdim), lambda i, blocks: (0, i, 0))]),
        out_shape=[jax.ShapeDtypeStruct((r, cdim), F32), jax.ShapeDtypeStruct((3, r, cdim), BF16)],
        compiler_params=_params("arbitrary"))(blocks, grad, grad, grad, grad, got)


def _final_sum(mine, got, *, name):
    r, cdim = mine.shape
    tr = r if r <= 256 else r // (2 if r % 32 == 0 and r // 2 <= 256 else 4)

    def body(m_ref, g_ref, o_ref):
        o_ref[...] = ((m_ref[...] + g_ref[0].astype(F32)) + g_ref[1].astype(F32)) + g_ref[2].astype(F32)

    return pl.pallas_call(
        body, name=name, grid=(r // tr,),
        in_specs=[pl.BlockSpec((tr, cdim), lambda i: (i, 0)), pl.BlockSpec((3, tr, cdim), lambda i: (0, i, 0))],
        out_specs=pl.BlockSpec((tr, cdim), lambda i: (i, 0)), out_shape=jax.ShapeDtypeStruct((r, cdim), F32),
        compiler_params=_params("parallel"))(mine, got)


def _adamw_math(w, g, m, v):
    m = ADAM_B1 * m + (1.0 - ADAM_B1) * g
    v = ADAM_B2 * v + (1.0 - ADAM_B2) * (g * g)
    m_hat = m / (1.0 - ADAM_B1 ** ADAM_STEP)
    v_hat = v / (1.0 - ADAM_B2 ** ADAM_STEP)
    return -ADAM_LR * (m_hat / (jnp.sqrt(v_hat) + ADAM_EPS) + ADAM_WD * w), m, v


def _adamw(w, g, m, v, *, name):
    L, r, cdim = w.shape
    tr = r if r <= 512 else r // 2
    assert r % tr == 0 and tr % 8 == 0

    def body(w_ref, g_ref, m_ref, v_ref, d_ref, nm_ref, nv_ref):
        d_ref[...], nm_ref[...], nv_ref[...] = _adamw_math(w_ref[...], g_ref[...], m_ref[...], v_ref[...])

    blk = pl.BlockSpec((None, tr, cdim), lambda l, i: (l, i, 0))
    return pl.pallas_call(body, name=name, grid=(L, r // tr), in_specs=[blk] * 4, out_specs=[blk] * 3,
                          out_shape=[jax.ShapeDtypeStruct(w.shape, F32)] * 3, compiler_params=_params("parallel", "parallel"))(w, g, m, v)


def _small_update(parts, w, m, v, *, name):
    _, p, lanes = parts.shape
    tp = p // 2 if p % 16 == 0 else p

    def body(p_ref, w_ref, m_ref, v_ref, g_ref, d_ref, nm_ref, nv_ref):
        g = p_ref[0]
        for i in range(1, N_DEV):
            g = g + p_ref[i]
        g_ref[...] = g
        d_ref[...], nm_ref[...], nv_ref[...] = _adamw_math(w_ref[...], g, m_ref[...], v_ref[...])

    blk = pl.BlockSpec((tp, lanes), lambda i: (i, 0))
    return pl.pallas_call(body, name=name, grid=(p // tp,), in_specs=[pl.BlockSpec((N_DEV, tp, lanes), lambda i: (0, i, 0)), blk, blk, blk],
                          out_specs=[blk] * 4, out_shape=[jax.ShapeDtypeStruct((p, lanes), F32)] * 4,
                          compiler_params=_params("parallel"))(parts, w, m, v)


BIG = ("w_in", "w_o", "w_xq", "w_xkv", "w_xo", "w_gate_up", "w_down")
TRANSPOSED = ("w_in", "w_xkv", "w_gate_up")
SMALL = ("mem_norm_g", "mix_pre_g", "mix_post_g", "gm_v_g", "gm_w_s", "gm_b_s", "pool_w", "pool_scale", "attn_sinks",
         "x_pre_g", "x_post_g", "ffn_pre_g", "ffn_post_g")
WEIGHTS = ("mem_norm_g", "mix_pre_g", "mix_post_g", "w_in", "gm_v_g", "gm_w_s", "gm_b_s", "pool_w", "pool_scale", "attn_sinks", "w_o",
           "x_pre_g", "x_post_g", "w_xq", "w_xkv", "w_xo", "ffn_pre_g", "ffn_post_g", "w_gate_up", "w_down")
PACK_QUANTUM = 8 * 128


def _pack(arrays):
    flat = []
    for a in arrays:
        a = a.reshape(-1).astype(F32)
        flat.append(jnp.pad(a, (0, -a.size % PACK_QUANTUM)))
    return jnp.concatenate(flat).reshape(-1, 128)


def _unpack(pack, shapes):
    out, row = [], 0
    for s in shapes:
        size = math.prod(s)
        rows = (size + (-size % PACK_QUANTUM)) // 128
        out.append(pack[row:row + rows].reshape(-1)[:size].reshape(s))
        row += rows
    return out


def _row(v):
    return v.reshape(1, -1)


def _local_step(x, mem, positions, target, small, weight, fwd_hook=None, bwd_hook=None):
    fwd_hook = fwd_hook or (lambda l, point, v: v)
    bwd_hook = bwd_hook or (lambda l, point, v, gb: v)
    depth = small["mix_pre_g"].shape[0]
    half = HEAD_DIM // 2
    inv = ROPE_THETA ** (-jnp.arange(half, dtype=F32) / half)
    cos_t, sin_t = _rope_tables(positions.reshape(-1, 1), jnp.tile(inv, 128 // half).reshape(1, 128), name="rope_tables")
    mem_g = _row(small["mem_norm_g"])
    saved = []
    for l in range(depth):
        w = functools.partial(weight, l)
        bfull = jnp.repeat(small["gm_b_s"][l].T, HEAD_DIM, axis=1)
        bd = jax.scipy.linalg.block_diag(*[small["pool_w"][l, g] for g in range(len(POOL_WINDOWS))])
        mix_par = (_row(small["gm_v_g"][l]), small["gm_w_s"][l], bfull, bd, _row(small["pool_scale"][l]), small["attn_sinks"][l])
        if l == 0:
            z, h1 = _mm_fwd(fwd_hook(l, 0, x), w("w_in"), nt=True, pre="norm", g_pre=_row(small["mix_pre_g"][l]), name="first_in_proj")
        else:
            h1 = fwd_hook(l, 0, h1)
            z, = _mm_fwd(h1, w("w_in"), nt=True, name="in_proj")
        cat = fwd_hook(l, 1, _mixer_fwd(z, cos_t, sin_t, *mix_par, name="mixer_fwd"))
        mix, x1, h2 = _mm_fwd(cat, w("w_o"), nt=False, post="norm_res", g_post=_row(small["mix_post_g"][l]), xres=x,
                              g_next=_row(small["x_pre_g"][l]), name="mix_out")
        qx, = _mm_fwd(h2, w("w_xq"), nt=False, name="xq_proj")
        kv, memn = _mm_fwd(mem, w("w_xkv"), nt=True, pre="norm", g_pre=mem_g, name="xkv_proj")
        ox = _xattn_fwd(qx, kv, name="xattn_fwd")
        xo, x2, h3 = _mm_fwd(ox, w("w_xo"), nt=False, post="norm_res", g_post=_row(small["x_post_g"][l]), xres=x1,
                             g_next=_row(small["ffn_pre_g"][l]), name="xattn_out")
        h3 = fwd_hook(l, 2, h3)
        gu, = _mm_fwd(h3, w("w_gate_up"), nt=True, name="ffn_in")
        last = l + 1 == depth
        f, x3, *h_next = _mm_fwd(gu, w("w_down"), nt=False, pre="swiglu", post="norm_res", g_post=_row(small["ffn_post_g"][l]), xres=x2,
                                 g_next=None if last else _row(small["mix_pre_g"][l + 1]), tile=256, name="last_ffn_out" if last else "ffn_out")
        saved.append((mix_par, x, z, h1, cat, mix, x1, qx, h2, kv, memn, ox, xo, x2, gu, h3, f))
        if last:
            x = fwd_hook(l, 3, x3)
        else:
            x, h1 = x3, fwd_hook(l, 3, h_next[0])

    gs = {n: [None] * depth for n in SMALL if n != "mem_norm_g"}
    gb = {n: [None] * depth for n in BIG}
    dmemn = [None] * depth
    loss, dx, df, gs["ffn_post_g"][depth - 1] = _loss_grad(x, target, saved[-1][-1], _row(small["ffn_post_g"][depth - 1]), name="loss")
    for l in reversed(range(depth)):
        w = functools.partial(weight, l)
        mix_par, x0, z, h1, cat, mix, x1, qx, h2, kv, memn, ox, xo, x2, gu, h3, f = saved[l]
        dgu = _ffn_out_bwd(df, w("w_down"), gu, name="ffn_out_bwd")
        gb["w_down"][l] = _wgrad(gu, df, swiglu=True, name="w_down_grad")
        gb["w_gate_up"][l] = _wgrad(dgu, h3, name="w_gate_up_grad")
        dx2, gs["ffn_pre_g"][l], dxo, gs["x_post_g"][l] = _mm_bwd_post(
            dgu, w("w_gate_up"), x2, _row(small["ffn_pre_g"][l]), dx, nt=False, below=(xo, _row(small["x_post_g"][l])), tile=256, name="ffn_in_bwd")
        dxo = bwd_hook(l, 1, dxo, gb)
        do, = _mm_fwd(dxo, w("w_xo"), nt=True, name="xattn_out_bwd")
        gb["w_xo"][l] = _wgrad(ox, dxo, name="w_xo_grad")
        dqx, dkv = _xattn_bwd(qx, kv, do, name="xattn_bwd")
        gb["w_xq"][l] = _wgrad(h2, dqx, name="w_xq_grad")
        dx1, gs["x_pre_g"][l], dmix, gs["mix_post_g"][l] = _mm_bwd_post(
            dqx, w("w_xq"), x1, _row(small["x_pre_g"][l]), dx2, nt=True, below=(mix, _row(small["mix_post_g"][l])), name="xq_proj_bwd")
        dmix = bwd_hook(l, 2, dmix, gb)
        gb["w_xkv"][l] = _wgrad(dkv, memn, name="w_xkv_grad")
        dmemn[l] = _mm_fwd(dkv, w("w_xkv"), nt=False, out_dtype=F32, name="xkv_proj_bwd")[0]
        dcat, = _mm_fwd(dmix, w("w_o"), nt=True, name="mix_out_bwd")
        gb["w_o"][l] = _wgrad(cat, dmix, name="w_o_grad")
        dz, dvg, dwc, dbf, dbd, dscale, dsink = _mixer_bwd(z, dcat, cos_t, sin_t, *mix_par, name="mixer_bwd")
        gs["gm_v_g"][l], gs["gm_w_s"][l], gs["pool_scale"][l], gs["attn_sinks"][l] = dvg, dwc, dscale, dsink[:, 0]
        gs["gm_b_s"][l] = dbf.reshape(BLOCK, -1, HEAD_DIM).sum(-1).T
        gs["pool_w"][l] = jnp.stack([dbd[HEAD_DIM * g:HEAD_DIM * (g + 1), HEAD_DIM * g:HEAD_DIM * (g + 1)] for g in range(len(POOL_WINDOWS))])
        gb["w_in"][l] = _wgrad(dz, h1, name="w_in_grad")
        if l == 0:
            dx, gs["mix_pre_g"][l] = _mm_bwd_post(dz, w("w_in"), x0, _row(small["mix_pre_g"][l]), dx1, nt=False, name="first_in_proj_bwd")
            dx = bwd_hook(l, 3, dx, gb)
        else:
            dx, gs["mix_pre_g"][l], df, gs["ffn_post_g"][l - 1] = _mm_bwd_post(
                dz, w("w_in"), x0, _row(small["mix_pre_g"][l]), dx1, nt=False, below=(saved[l - 1][-1], _row(small["ffn_post_g"][l - 1])),
                name="in_proj_bwd")
            df = bwd_hook(l, 3, df, gb)

    small_grads = {n: jnp.stack(v).reshape(small[n].shape) for n, v in gs.items()}
    small_grads["mem_norm_g"] = _norm_dgain(mem, jnp.stack(dmemn), name="mem_norm_grad").reshape(-1)
    return loss, dx, small_grads, gb


EARLY = ("w_in", "w_o", "w_xq", "w_xkv", "w_xo")
LATE = ("w_gate_up", "w_down")


def _step(a):
    depth = a["mix_pre_g"].shape[0]
    x, mem, target = a["x"][0], a["mem"][0], a["loss_target"][0]
    px, py, pc = _place()
    blocks = jnp.stack([4 * cx + 2 * cy + pc for cx, cy in (_chip(k, px, py) for k in range(4))]).astype(jnp.int32)
    flight = {}

    def laid_out(v, n):
        return v.transpose(0, 2, 1) if n in TRANSPOSED else v

    whole = {(l, n): _place_own(laid_out(a[n], n), l, blocks, name="place_" + n) for l in range(depth) for n in BIG}
    gather = {"first_in": [(0, "w_in")], "first_mid": [(0, n) for n in EARLY[1:]], "first_late": [(0, n) for n in LATE],
              "second_early": [(1, n) for n in EARLY], "second_late": [(1, n) for n in LATE]}
    gather.update({f"layer{l}": [(l, n) for n in BIG] for l in range(2, depth)})
    forward_plan = {(0, 0): [("a", "first_in"), ("a", "first_mid"), ("a", "first_late"), ("a", "second_early"), ("a", "second_late"),
                             ("mid", "first_in"), ("end", "first_in")],
                    (0, 1): [("mid", "first_mid"), ("end", "first_mid")], (0, 2): [("mid", "first_late"), ("end", "first_late")],
                    (0, 3): [("mid", "second_early")], (1, 0): [("end", "second_early"), ("a", "layer2")],
                    (1, 1): [("mid", "second_late")], (1, 2): [("end", "second_late"), ("mid", "layer2")], (1, 3): [("end", "layer2")]}
    for l in range(2, depth):
        forward_plan.update({(l, 0): [("a", f"layer{l + 1}")], (l, 2): [("mid", f"layer{l + 1}")], (l, 3): [("end", f"layer{l + 1}")]})
    if depth == 1:
        gather = {tag: keys for tag, keys in gather.items() if tag.startswith("first")}

    issued = []

    def gather_step(kind, tag, v):
        keys = gather[tag]
        if kind == "a":
            arrays = [whole[k] for k in keys]
            arrays[0] = _tie(arrays[0], issued[-1:])
            send, recv, arrays, token = _start(_gather_copies("a"), arrays, [], 4 * len(keys), name=f"gather_a_{tag}")
            flight[tag] = (send, recv, arrays)
            issued.append(token)
            return [token]
        send, recv, arrays = flight.pop(tag)
        if kind == "mid":
            arrays = _finish(_gather_copies("a"), send, recv, arrays, [v], name=f"gather_a_done_{tag}")
            send, recv, arrays, token = _start(_gather_copies("b"), arrays, [], 3 * len(keys), name=f"gather_b_{tag}")
            flight[tag] = (send, recv, arrays)
            return [token]
        for key, arr in zip(keys, _finish(_gather_copies("b"), send, recv, arrays, [v], name=f"gather_b_done_{tag}")):
            whole[key] = arr
        return []

    def fwd_hook(l, point, v):
        tokens = []
        for kind, tag in forward_plan.get((l, point), []):
            if tag in gather:
                tokens += gather_step(kind, tag, v)
        return _tie(v, tokens)

    mine = {}
    scatter = {f"layer{l}": [(l, n) for n in BIG] for l in range(1, depth)}
    scatter.update({"late0": [(0, n) for n in LATE], "attn0": [(0, "w_xq"), (0, "w_xo")], "rest0": [(0, "w_in"), (0, "w_o"), (0, "w_xkv")]})
    backward_plan = {}
    for l in range(1, depth):
        backward_plan[(l, 3)] = [("a", f"layer{l}")]
        backward_plan[(l - 1, 1)] = [("mid", f"layer{l}")]
        backward_plan[(l - 1, 2)] = [("end", f"layer{l}")]
    backward_plan[(0, 1)] = [("a", "late0")] + backward_plan.get((0, 1), [])
    backward_plan[(0, 2)] = [("a", "attn0"), ("mid", "late0")] + backward_plan.get((0, 2), [])
    backward_plan[(0, 3)] = [("end", "late0"), ("mid", "attn0"), ("a", "rest0")]

    def scatter_step(kind, tag, v, gb):
        keys = scatter[tag]
        n = len(keys)
        if kind == "a":
            grads = [gb[name][l] for l, name in keys]
            zones = [jax.ShapeDtypeStruct((4, g.shape[0] // N_DEV, g.shape[1]), F32) for g in grads]
            send, recv, arrays, token = _start(_scatter_copies("a", n), grads, zones, 4 * n, name=f"scatter_a_{tag}")
            flight[tag] = (send, recv, arrays, None)
            return [token]
        send, recv, arrays, kept = flight.pop(tag)
        if kind == "mid":
            arrays = _finish(_scatter_copies("a", n), send, recv, arrays, [v], name=f"scatter_a_done_{tag}")
            sums = [_chip_sum(arrays[i], arrays[n + i], blocks, name="chip_sum") for i in range(n)]
            zones = [jax.ShapeDtypeStruct(s[1].shape, BF16) for s in sums]
            send, recv, arrays, token = _start(_scatter_copies("b", n), [s[1] for s in sums], zones, 3 * n, name=f"scatter_b_{tag}")
            flight[tag] = (send, recv, arrays, [s[0] for s in sums])
            return [token]
        arrays = _finish(_scatter_copies("b", n), send, recv, arrays, [v], name=f"scatter_b_done_{tag}")
        for i, key in enumerate(keys):
            mine[key] = _final_sum(kept[i], arrays[n + i], name="final_sum")
        return []

    def bwd_hook(l, point, v, gb):
        tokens = []
        for kind, tag in backward_plan.get((l, point), []):
            tokens += scatter_step(kind, tag, v, gb)
        return _tie(v, tokens)

    loss, dx, small_grads, _ = _local_step(x, mem, a["positions"][0], target, {n: a[n] for n in SMALL}, lambda l, n: whole[(l, n)],
                                           fwd_hook, bwd_hook)

    out = {}

    def update(n, tokens):
        g = _tie(jnp.stack([mine[(l, n)] for l in range(depth)]), tokens)
        moved = _adamw(laid_out(a[n], n), g, laid_out(a["m_" + n], n), laid_out(a["v_" + n], n), name="adamw_" + n)
        for p, v in zip(("grad_", "delta_", "new_m_", "new_v_"), (g, *moved)):
            out[p + n] = laid_out(v, n)
        return moved[0]

    part = _tie(_pack([small_grads[n] for n in SMALL]), [dx])
    placed = _place_own(part[None], 0, blocks, dtype=F32, name="place_small_grads")
    send, recv, arrays, token = _start(_gather_copies("a"), [placed], [], 4, name="gather_a_small_grads")
    behind = [update(n, [token]) for n in LATE]
    tokens = scatter_step("mid", "rest0", behind[-1], None)
    arrays = _finish(_gather_copies("a"), send, recv, arrays, behind, name="gather_a_done_small_grads")
    send, recv, arrays, token = _start(_gather_copies("b"), arrays, [], 3, name="gather_b_small_grads")
    scatter_step("end", "attn0", behind[-1], None)
    behind = [update(n, tokens + [token]) for n in ("w_xq", "w_xo")]
    parts = _finish(_gather_copies("b"), send, recv, arrays, behind, name="gather_b_done_small_grads")[0].reshape(N_DEV, *part.shape)
    packs = _small_update(parts, *[_pack([a[p + n] for n in SMALL]) for p in ("", "m_", "v_")], name="small_update")
    for p, pack in zip(("grad_", "delta_", "new_m_", "new_v_"), packs):
        for n, v in zip(SMALL, _unpack(pack, [a[n].shape for n in SMALL])):
            out[p + n] = v
    scatter_step("end", "rest0", packs[0], None)
    for n in ("w_in", "w_o", "w_xkv"):
        update(n, [])

    total = lax.psum(loss[0, 0], ("x", "y", "c"))
    return (total, dx[None], *[out[p + n] for p in ("grad_", "delta_", "new_m_", "new_v_") for n in WEIGHTS])


def kernel(x, mem, positions, mem_norm_g, mix_pre_g, mix_post_g, w_in, gm_v_g, gm_w_s, gm_b_s, pool_w, pool_scale, attn_sinks,
           w_o, x_pre_g, x_post_g, w_xq, w_xkv, w_xo, ffn_pre_g, ffn_post_g, w_gate_up, w_down, loss_target, m_mem_norm_g,
           m_mix_pre_g, m_mix_post_g, m_w_in, m_gm_v_g, m_gm_w_s, m_gm_b_s, m_pool_w, m_pool_scale, m_attn_sinks, m_w_o,
           m_x_pre_g, m_x_post_g, m_w_xq, m_w_xkv, m_w_xo, m_ffn_pre_g, m_ffn_post_g, m_w_gate_up, m_w_down, v_mem_norm_g,
           v_mix_pre_g, v_mix_post_g, v_w_in, v_gm_v_g, v_gm_w_s, v_gm_b_s, v_pool_w, v_pool_scale, v_attn_sinks, v_w_o,
           v_x_pre_g, v_x_post_g, v_w_xq, v_w_xkv, v_w_xo, v_ffn_pre_g, v_ffn_post_g, v_w_gate_up, v_w_down):
    return _step(dict(locals()))
```

```python
import functools
import math

import jax
import jax.numpy as jnp
from jax import lax
from jax.experimental import pallas as pl
from jax.experimental.pallas import tpu as pltpu

F32, BF16 = jnp.float32, jnp.bfloat16
EPS = 1e-6
HEAD_DIM = 64
BLOCK = 128
POOL_WINDOWS = (2, 4, 8, 16)
ATT_Q_HEADS = 8
X_HEADS = 4
ROPE_THETA = 10000.0
ADAM_LR, ADAM_B1, ADAM_B2, ADAM_EPS, ADAM_WD, ADAM_STEP = 0.001, 0.9, 0.999, 1e-08, 0.01, 10
N_DEV = 8
TOKEN_TILE = 512
VMEM_LIMIT_BYTES = 50 * 2**20
NEG = -1e30
MESH = pl.DeviceIdType.MESH

NN = ((1,), (0,))
NT = ((1,), (1,))
TN = ((0,), (0,))


def _dot(a, b, dims):
    return lax.dot_general(a.astype(BF16), b.astype(BF16), (dims, ((), ())), preferred_element_type=F32)


def _params(*sem):
    return pltpu.CompilerParams(dimension_semantics=sem, vmem_limit_bytes=VMEM_LIMIT_BYTES)


def _full(shape):
    return pl.BlockSpec(shape, lambda *_: (0,) * len(shape))


def _lane(shape):
    return lax.broadcasted_iota(jnp.int32, shape, len(shape) - 1)


def _rms_fwd(x, g):
    r = lax.rsqrt(jnp.mean(x * x, axis=-1, keepdims=True) + EPS)
    return x * r * g


def _rms_bwd(x, g, dy):
    r = lax.rsqrt(jnp.mean(x * x, axis=-1, keepdims=True) + EPS)
    xh = x * r
    dg = jnp.sum(dy * xh, axis=0, keepdims=True)
    dxh = dy * g
    dx = r * (dxh - xh * jnp.mean(dxh * xh, axis=-1, keepdims=True))
    return dx, dg


def _silu_parts(gate):
    sg = 1.0 / (1.0 + jnp.exp(-gate))
    return gate * sg, sg


def _swiglu_cols(gu_ref, out_ref, width):
    step = 128 * max(1, (width // 128) // 4)
    for c0 in range(0, width, step):
        c1 = min(width, c0 + step)
        gate = gu_ref[:, c0:c1].astype(F32)
        up = gu_ref[:, width + c0:width + c1].astype(F32)
        out_ref[:, c0:c1] = (_silu_parts(gate)[0] * up).astype(out_ref.dtype)


def _col_tile(n, cap=1408):
    best = n
    for t in range(128, cap + 1, 128):
        if n % t == 0:
            best = t
    return best if n > cap else n


def _mm_fwd(a, w, *, nt, name, pre=None, g_pre=None, post=None, g_post=None, xres=None, g_next=None, out_dtype=BF16, tile=TOKEN_TILE):
    S = a.shape[0]
    n_out, k = (w.shape[0], w.shape[1]) if nt else (w.shape[1], w.shape[0])
    T = min(tile, S)
    tn = n_out if post else _col_tile(n_out)
    grid = (S // T, n_out // tn)

    def body(*refs):
        it = iter(refs)
        a_ref, w_ref = next(it), next(it)
        gpre_ref = next(it) if pre == "norm" else None
        gpost_ref, x_ref = (next(it), next(it)) if post else (None, None)
        gnext_ref = next(it) if g_next is not None else None
        outs = [next(it) for _ in range((2 if post else 1) + (1 if pre == "norm" else 0))]
        hnext_ref = next(it) if g_next is not None else None
        as_ref = next(it) if pre else None
        if pre:
            @pl.when(pl.program_id(1) == 0)
            def _():
                if pre == "norm":
                    h = _rms_fwd(a_ref[...], gpre_ref[...]).astype(BF16)
                    outs[-1][...] = h
                    as_ref[...] = h
                else:
                    _swiglu_cols(a_ref, as_ref, k)
            av = as_ref[...]
        else:
            av = a_ref[...]
        acc = _dot(av, w_ref[...], NT if nt else NN)
        if post:
            outs[0][...] = acc.astype(BF16)
            x_new = x_ref[...] + _rms_fwd(acc, gpost_ref[...])
            outs[1][...] = x_new
            if g_next is not None:
                hnext_ref[...] = _rms_fwd(x_new, gnext_ref[...]).astype(BF16)
        else:
            outs[0][...] = acc.astype(out_dtype)

    in_specs = [pl.BlockSpec((T, a.shape[1]), lambda i, j: (i, 0)),
                pl.BlockSpec((tn, k), lambda i, j: (j, 0)) if nt else pl.BlockSpec((k, tn), lambda i, j: (0, j))]
    args = [a, w]
    if pre == "norm":
        in_specs.append(_full((1, k)))
        args.append(g_pre)
    if post:
        in_specs += [_full((1, n_out)), pl.BlockSpec((T, n_out), lambda i, j: (i, 0))]
        args += [g_post, xres]
    if g_next is not None:
        in_specs.append(_full((1, n_out)))
        args.append(g_next)
    out_block = pl.BlockSpec((T, tn), lambda i, j: (i, j))
    if post:
        out_shape = [jax.ShapeDtypeStruct((S, n_out), BF16), jax.ShapeDtypeStruct((S, n_out), F32)]
        out_specs = [out_block, out_block]
    else:
        out_shape = [jax.ShapeDtypeStruct((S, n_out), out_dtype)]
        out_specs = [out_block]
    if pre == "norm":
        out_shape.append(jax.ShapeDtypeStruct((S, k), BF16))
        out_specs.append(pl.BlockSpec((T, k), lambda i, j: (i, 0)))
    if g_next is not None:
        out_shape.append(jax.ShapeDtypeStruct((S, n_out), BF16))
        out_specs.append(out_block)
    scratch = [pltpu.VMEM((T, k), BF16)] if pre else []
    return pl.pallas_call(body, name=name, grid=grid, in_specs=in_specs, out_specs=out_specs, out_shape=out_shape,
                          scratch_shapes=scratch, compiler_params=_params("parallel", "arbitrary"))(*args)


def _mm_bwd_post(dy, w, x, g, dx_in, *, nt, name, below=None, tile=TOKEN_TILE):
    S, k = dy.shape
    n = x.shape[1]
    T = min(tile, S)

    def body(*refs):
        dy_ref, w_ref, x_ref, g_ref, dxi_ref = refs[:5]
        m_ref, gb_ref = refs[5:7] if below else (None, None)
        dxo_ref, dg_ref = refs[-4:-2] if below else refs[-2:]

        @pl.when(pl.program_id(0) == 0)
        def _():
            dg_ref[...] = jnp.zeros_like(dg_ref)
            if below:
                refs[-1][...] = jnp.zeros_like(refs[-1])

        dh = _dot(dy_ref[...], w_ref[...], NT if nt else NN)
        dxn, dg = _rms_bwd(x_ref[...], g_ref[...], dh)
        dg_ref[...] += dg
        dxo = dxi_ref[...] + dxn
        dxo_ref[...] = dxo
        if below:
            dmv, dgb = _rms_bwd(m_ref[...].astype(F32), gb_ref[...], dxo)
            refs[-2][...] = dmv.astype(BF16)
            refs[-1][...] += dgb

    row = pl.BlockSpec((T, n), lambda i: (i, 0))
    in_specs = [pl.BlockSpec((T, k), lambda i: (i, 0)), _full(w.shape), row, _full((1, n)), row]
    out_specs = [row, _full((1, n))]
    out_shape = [jax.ShapeDtypeStruct((S, n), F32), jax.ShapeDtypeStruct((1, n), F32)]
    if below:
        in_specs += [row, _full((1, n))]
        out_specs += [row, _full((1, n))]
        out_shape += [jax.ShapeDtypeStruct((S, n), BF16), jax.ShapeDtypeStruct((1, n), F32)]
    return pl.pallas_call(body, name=name, grid=(S // T,), in_specs=in_specs, out_specs=out_specs, out_shape=out_shape,
                          compiler_params=_params("arbitrary"))(dy, w, x, g, dx_in, *(below or ()))


def _wgrad(a, b, *, name, swiglu=False, tile=TOKEN_TILE):
    S = a.shape[0]
    k = a.shape[1] // 2 if swiglu else a.shape[1]
    n = b.shape[1]
    T = min(tile, S)
    tk = 512 if k % 512 == 0 and k > 1536 else 256
    nk = k // tk

    def body(*refs):
        if swiglu:
            gate_ref, up_ref, b_ref, o_ref, act_ref = refs
            for r0 in range(0, S, T):
                gate, up = gate_ref[r0:r0 + T, :].astype(F32), up_ref[r0:r0 + T, :].astype(F32)
                act_ref[r0:r0 + T, :] = (_silu_parts(gate)[0] * up).astype(BF16)
            av = act_ref[...]
        else:
            a_ref, b_ref, o_ref = refs
            av = a_ref[...]
        o_ref[...] = _dot(av, b_ref[...], TN)

    in_specs = [pl.BlockSpec((S, tk), lambda kk: (0, kk))]
    args = [a]
    if swiglu:
        in_specs.append(pl.BlockSpec((S, tk), lambda kk: (0, kk + nk)))
        args.append(a)
    in_specs.append(_full((S, n)))
    args.append(b)
    return pl.pallas_call(body, name=name, grid=(nk,), in_specs=in_specs,
                          out_specs=pl.BlockSpec((tk, n), lambda kk: (kk, 0)),
                          out_shape=jax.ShapeDtypeStruct((k, n), F32),
                          scratch_shapes=[pltpu.VMEM((S, tk), BF16)] if swiglu else [],
                          compiler_params=_params("parallel"))(*args)


def _ffn_out_bwd(dm, w, gu, *, name, tile=256):
    S, n = dm.shape
    f = w.shape[0]
    T = min(tile, S)

    def body(dm_ref, w_ref, gu_ref, dgu_ref, da_ref):
        da_ref[...] = _dot(dm_ref[...], w_ref[...], NT)
        step = 128 * max(1, (f // 128) // 4)
        for c0 in range(0, f, step):
            c1 = min(f, c0 + step)
            gate = gu_ref[:, c0:c1].astype(F32)
            up = gu_ref[:, f + c0:f + c1].astype(F32)
            da = da_ref[:, c0:c1]
            sl, sg = _silu_parts(gate)
            dgu_ref[:, c0:c1] = (da * up * (sg + sl * (1.0 - sg))).astype(BF16)
            dgu_ref[:, f + c0:f + c1] = (da * sl).astype(BF16)

    row = pl.BlockSpec((T, n), lambda i: (i, 0))
    wide = pl.BlockSpec((T, 2 * f), lambda i: (i, 0))
    return pl.pallas_call(
        body, name=name, grid=(S // T,), in_specs=[row, _full((f, n)), wide], out_specs=wide,
        out_shape=jax.ShapeDtypeStruct((S, 2 * f), BF16),
        scratch_shapes=[pltpu.VMEM((T, f), F32)], compiler_params=_params("parallel"))(dm, w, gu)


def _loss_grad(y, target, m, g, *, name, tile=TOKEN_TILE):
    S, d = y.shape
    T = min(tile, S)
    steps = S // T

    def body(y_ref, t_ref, m_ref, g_ref, loss_ref, dy_ref, dm_ref, dg_ref, acc_ref):
        i = pl.program_id(0)

        @pl.when(i == 0)
        def _():
            acc_ref[...] = jnp.zeros_like(acc_ref)
            dg_ref[...] = jnp.zeros_like(dg_ref)

        diff = y_ref[...] - t_ref[...]
        dy = diff * (1.0 / d)
        dy_ref[...] = dy
        dmv, dg = _rms_bwd(m_ref[...].astype(F32), g_ref[...], dy)
        dm_ref[...] = dmv.astype(BF16)
        dg_ref[...] += dg
        acc_ref[...] += jnp.sum(diff * diff, axis=0, keepdims=True)

        @pl.when(i == steps - 1)
        def _():
            total = jnp.sum(acc_ref[...], axis=1, keepdims=True) * (0.5 / d)
            loss_ref[...] = jnp.broadcast_to(total, loss_ref.shape)

    row = pl.BlockSpec((T, d), lambda i: (i, 0))
    return pl.pallas_call(
        body, name=name, grid=(steps,), in_specs=[row, row, row, _full((1, d))], out_specs=[_full((1, 128)), row, row, _full((1, d))],
        out_shape=[jax.ShapeDtypeStruct((1, 128), F32), jax.ShapeDtypeStruct((S, d), F32), jax.ShapeDtypeStruct((S, d), BF16),
                   jax.ShapeDtypeStruct((1, d), F32)],
        scratch_shapes=[pltpu.VMEM((1, d), F32)], compiler_params=_params("arbitrary"))(y, target, m, g)


def _norm_dgain(x, dys, *, name):
    def body(x_ref, dy_ref, o_ref):
        xv = x_ref[...]
        dy = dy_ref[0]
        for l in range(1, dys.shape[0]):
            dy = dy + dy_ref[l]
        r = lax.rsqrt(jnp.mean(xv * xv, axis=-1, keepdims=True) + EPS)
        o_ref[...] = jnp.sum(dy * xv * r, axis=0, keepdims=True)

    return pl.pallas_call(body, name=name, out_shape=jax.ShapeDtypeStruct((1, x.shape[1]), F32),
                          compiler_params=pltpu.CompilerParams(vmem_limit_bytes=VMEM_LIMIT_BYTES))(x, dys)


def _xattn_probs(q_ref, kv_ref, h, d, hd):
    q = q_ref[:, h * hd:(h + 1) * hd]
    kh = kv_ref[:, h * hd:(h + 1) * hd]
    vh = kv_ref[:, d + h * hd:d + (h + 1) * hd]
    s = _dot(q, kh, NT) * (hd ** -0.5)
    e = jnp.exp(s - jnp.max(s, axis=-1, keepdims=True))
    return q, kh, vh, e / jnp.sum(e, axis=-1, keepdims=True)


def _xattn_fwd(q, kv, *, name, tile=TOKEN_TILE):
    S, d = q.shape
    mlen = kv.shape[0]
    hd = d // X_HEADS
    T = min(tile, S)

    def body(q_ref, kv_ref, o_ref):
        for h in range(X_HEADS):
            _, _, vh, p = _xattn_probs(q_ref, kv_ref, h, d, hd)
            o_ref[:, h * hd:(h + 1) * hd] = _dot(p, vh, NN).astype(BF16)

    row = pl.BlockSpec((T, d), lambda i: (i, 0))
    return pl.pallas_call(body, name=name, grid=(S // T,), in_specs=[row, _full((mlen, 2 * d))], out_specs=row,
                          out_shape=jax.ShapeDtypeStruct((S, d), BF16), compiler_params=_params("parallel"))(q, kv)


def _xattn_bwd(q, kv, do, *, name, tile=TOKEN_TILE):
    S, d = q.shape
    mlen = kv.shape[0]
    hd = d // X_HEADS
    T = min(tile, S)

    def body(q_ref, kv_ref, do_ref, dq_ref, dkv_ref):
        @pl.when(pl.program_id(0) == 0)
        def _():
            dkv_ref[...] = jnp.zeros_like(dkv_ref)

        for h in range(X_HEADS):
            cols = slice(h * hd, (h + 1) * hd)
            qh, kh, vh, doh = q_ref[:, cols], kv_ref[:, cols], kv_ref[:, d + h * hd:d + (h + 1) * hd], do_ref[:, cols]
            st = _dot(kh, qh, NT) * (hd ** -0.5)
            e = jnp.exp(st - jnp.max(st, axis=0, keepdims=True))
            pt = e / jnp.sum(e, axis=0, keepdims=True)
            dpt = _dot(vh, doh, NT)
            dst = (pt * (dpt - jnp.sum(pt * dpt, axis=0, keepdims=True)) * (hd ** -0.5)).astype(BF16)
            dkv_ref[:, cols] += _dot(dst, qh, NN)
            dkv_ref[:, d + h * hd:d + (h + 1) * hd] += _dot(pt, doh, NN)
            dq_ref[:, cols] = _dot(dst, kh, TN).astype(BF16)

    row = pl.BlockSpec((T, d), lambda i: (i, 0))
    return pl.pallas_call(
        body, name=name, grid=(S // T,), in_specs=[row, _full((mlen, 2 * d)), row],
        out_specs=[row, _full((mlen, 2 * d))],
        out_shape=[jax.ShapeDtypeStruct((S, d), BF16), jax.ShapeDtypeStruct((mlen, 2 * d), F32)],
        compiler_params=_params("arbitrary"))(q, kv, do)


GM0, POOL0, Q0, KV0, Z_END = 0, 512, 768, 1280, 1536


def _rope_tables(pos_col, inv_row, *, name):
    S = pos_col.shape[0]

    def body(p_ref, inv_ref, cos_ref, sin_ref):
        ang = p_ref[...].astype(F32) * inv_ref[...]
        sin = jnp.sin(ang)
        cos_ref[...] = jnp.cos(ang)
        sin_ref[...] = jnp.where(_lane(ang.shape) % HEAD_DIM < HEAD_DIM // 2, -sin, sin)

    T = min(TOKEN_TILE, S)
    return pl.pallas_call(
        body, name=name, grid=(S // T,),
        in_specs=[pl.BlockSpec((T, 1), lambda i: (i, 0)), _full((1, 128))],
        out_specs=[pl.BlockSpec((T, 128), lambda i: (i, 0))] * 2,
        out_shape=[jax.ShapeDtypeStruct((S, 128), F32)] * 2, compiler_params=_params("parallel"))(pos_col, inv_row)


def _swap_halves(x):
    n = x.shape[-1]
    return jnp.where(_lane(x.shape) % HEAD_DIM < HEAD_DIM // 2, pltpu.roll(x, n - HEAD_DIM // 2, 1), pltpu.roll(x, HEAD_DIM // 2, 1))


def _rope(x, cos, sin_s):
    reps = x.shape[-1] // 128
    if reps > 1:
        cos, sin_s = jnp.tile(cos, (1, reps)), jnp.tile(sin_s, (1, reps))
    return x * cos + _swap_halves(x) * sin_s


def _rope_bwd(dy, cos, sin_s):
    reps = dy.shape[-1] // 128
    if reps > 1:
        cos, sin_s = jnp.tile(cos, (1, reps)), jnp.tile(sin_s, (1, reps))
    return dy * cos + _swap_halves(dy * sin_s)


def _gelu_parts(x):
    c = math.sqrt(2.0 / math.pi)
    t = jnp.tanh(c * (x + 0.044715 * x * x * x))
    return 0.5 * x * (1.0 + t), t


def _gelu_grad(x, t):
    c = math.sqrt(2.0 / math.pi)
    return 0.5 * (1.0 + t) + 0.5 * x * (1.0 - t * t) * c * (1.0 + 3.0 * 0.044715 * x * x)


def _group_sum(x):
    gid = _lane(x.shape) // HEAD_DIM
    out = jnp.zeros_like(x)
    for g in range(x.shape[-1] // HEAD_DIM):
        sel = gid == g
        out = jnp.where(sel, jnp.sum(jnp.where(sel, x, 0.0), axis=-1, keepdims=True), out)
    return out


def _low_half(shape):
    return _lane(shape) % 128 < HEAD_DIM


def _tril(w):
    r = lax.broadcasted_iota(jnp.int32, w.shape, 0)
    c = lax.broadcasted_iota(jnp.int32, w.shape, 1)
    return jnp.where(r >= c, w, 0.0)


def _triu(w):
    r = lax.broadcasted_iota(jnp.int32, w.shape, 0)
    c = lax.broadcasted_iota(jnp.int32, w.shape, 1)
    return jnp.where(r <= c, w, 0.0)


def _gating_fwd(zg, vg, wc_ref, bfull):
    ge, th = _gelu_parts(zg)
    u, v = ge[:, :256], ge[:, 256:]
    r = lax.rsqrt(_group_sum(v * v) * (1.0 / HEAD_DIM) + EPS)
    vhat = v * r
    vn = vhat * vg
    low = _low_half((BLOCK, 128))
    blocks = []
    for blk in range(2):
        vb = vn[:, 128 * blk:128 * (blk + 1)]
        m0 = _dot(_tril(wc_ref[2 * blk]), vb, NN)
        m1 = _dot(_tril(wc_ref[2 * blk + 1]), vb, NN)
        blocks.append(jnp.where(low, m0, m1))
    mixed = jnp.concatenate(blocks, axis=1) + bfull
    return u * mixed, (th, u, r, vhat, vn, mixed)


def _band(w, transpose):
    if transpose:
        s2 = lax.broadcasted_iota(jnp.int32, (2 * BLOCK, BLOCK), 0)
        t = lax.broadcasted_iota(jnp.int32, (2 * BLOCK, BLOCK), 1)
    else:
        t = lax.broadcasted_iota(jnp.int32, (BLOCK, 2 * BLOCK), 0)
        s2 = lax.broadcasted_iota(jnp.int32, (BLOCK, 2 * BLOCK), 1)
    d = t - s2 + BLOCK
    return jnp.where((d >= 0) & (d < w), 1.0, 0.0).astype(BF16)


def _inv_count(n):
    t = lax.broadcasted_iota(jnp.int32, (BLOCK, 256), 0) + n * BLOCK
    gid = _lane((BLOCK, 256)) // HEAD_DIM
    win = jnp.where(gid == 0, POOL_WINDOWS[0], jnp.where(gid == 1, POOL_WINDOWS[1], jnp.where(gid == 2, POOL_WINDOWS[2], POOL_WINDOWS[3])))
    return 1.0 / jnp.minimum(t + 1, win).astype(F32)


def _pool_fwd(p, pp, n, bd, scale):
    low = _low_half((BLOCK, 128))
    blocks = []
    for blk in range(2):
        both = jnp.concatenate([pp[:, 128 * blk:128 * (blk + 1)], p[:, 128 * blk:128 * (blk + 1)]], axis=0)
        s0 = _dot(_band(POOL_WINDOWS[2 * blk], False), both, NN)
        s1 = _dot(_band(POOL_WINDOWS[2 * blk + 1], False), both, NN)
        blocks.append(jnp.where(low, s0, s1))
    inv = _inv_count(n)
    pooled = jnp.concatenate(blocks, axis=1) * inv - p
    mapped = _dot(pooled, bd, NN)
    return mapped * scale, (inv, pooled, mapped)


def _mixer_fwd(z, cos_t, sin_t, vg, wc, bfull, bd, scale, sinks, *, name):
    S = z.shape[0]
    nb = S // BLOCK

    def body(zc_ref, zpp_ref, zpk_ref, cc_ref, cp_ref, sc_ref, sp_ref, vg_ref, wc_ref, bf_ref, bd_ref, scale_ref, sinks_ref, out_ref, pen_ref):
        n = pl.program_id(0)

        @pl.when(n == 0)
        def _():
            tall = (ATT_Q_HEADS // 2 * BLOCK, 2 * BLOCK)
            qi = lax.broadcasted_iota(jnp.int32, tall, 0) % BLOCK
            ki = lax.broadcasted_iota(jnp.int32, tall, 1)
            inside = (ki > qi) & (ki <= qi + BLOCK)
            pen_ref[0] = jnp.where(inside, 0.0, NEG)
            pen_ref[1] = jnp.where(inside & (ki >= BLOCK), 0.0, NEG)

        keep = (n > 0).astype(F32)
        a, _ = _gating_fwd(zc_ref[:, GM0:POOL0].astype(F32), vg_ref[...], wc_ref, bf_ref[...])
        out_ref[:, 0:256] = a.astype(BF16)
        b, _ = _pool_fwd(zc_ref[:, POOL0:Q0].astype(F32), zpp_ref[...].astype(F32) * keep, n, bd_ref[...], scale_ref[...])
        out_ref[:, 256:512] = b.astype(BF16)
        cos_c, sin_c = cc_ref[...], sc_ref[...]
        qrot = _rope(zc_ref[:, Q0:KV0].astype(F32), cos_c, sin_c)
        kk = jnp.concatenate([_rope(zpk_ref[:, 0:128].astype(F32), cp_ref[...], sp_ref[...]), _rope(zc_ref[:, KV0:KV0 + 128].astype(F32), cos_c, sin_c)], axis=0)
        vv = jnp.concatenate([zpk_ref[:, 128:256].astype(F32), zc_ref[:, KV0 + 128:Z_END].astype(F32)], axis=0)
        kk, vv = kk.astype(BF16), vv.astype(BF16)
        pen = pen_ref[jnp.where(n > 0, 0, 1)]
        low = _low_half((BLOCK, 128))
        q_slabs = [qrot[:, 128 * s:128 * (s + 1)] for s in range(ATT_Q_HEADS // 2)]
        q_rolled = [pltpu.roll(v, HEAD_DIM, 1) for v in q_slabs]
        o_heads = [None] * ATT_Q_HEADS
        for h in range(2):
            lanes = ~low if h else low
            heads = range(4 * h, 4 * h + 4)
            qm = jnp.concatenate([jnp.where(lanes, (q_slabs if j % 2 == h else q_rolled)[j // 2], 0.0) for j in heads], axis=0)
            sc = _dot(qm, kk, NT) * (HEAD_DIM ** -0.5) + pen
            p = []
            for i, j in enumerate(heads):
                sch, sink = sc[BLOCK * i:BLOCK * (i + 1)], sinks_ref[j]
                mx = jnp.maximum(jnp.max(sch, axis=-1, keepdims=True), sink)
                e = jnp.exp(sch - mx)
                p.append((e * (1.0 / (jnp.sum(e, axis=-1, keepdims=True) + jnp.exp(sink - mx)))).astype(BF16))
            o = _dot(jnp.concatenate(p, axis=0), vv, NN)
            for i, j in enumerate(heads):
                oh = o[BLOCK * i:BLOCK * (i + 1)]
                o_heads[j] = oh if j % 2 == h else pltpu.roll(oh, HEAD_DIM, 1)
        for s in range(ATT_Q_HEADS // 2):
            out_ref[:, 512 + 128 * s:512 + 128 * (s + 1)] = jnp.where(low, o_heads[2 * s], o_heads[2 * s + 1]).astype(BF16)

    cur = lambda n: (n, 0)
    prev = lambda n: (jnp.maximum(n - 1, 0), 0)
    in_specs = [
        pl.BlockSpec((BLOCK, Z_END), cur),
        pl.BlockSpec((BLOCK, 256), lambda n: (jnp.maximum(n - 1, 0), POOL0 // 256)),
        pl.BlockSpec((BLOCK, 256), lambda n: (jnp.maximum(n - 1, 0), KV0 // 256)),
        pl.BlockSpec((BLOCK, 128), cur), pl.BlockSpec((BLOCK, 128), prev),
        pl.BlockSpec((BLOCK, 128), cur), pl.BlockSpec((BLOCK, 128), prev),
        _full((1, 256)), _full((4, BLOCK, BLOCK)), _full((BLOCK, 256)), _full((256, 256)), _full((1, 256)),
        pl.BlockSpec(memory_space=pltpu.SMEM),
    ]
    return pl.pallas_call(
        body, name=name, grid=(nb,), in_specs=in_specs, out_specs=pl.BlockSpec((BLOCK, 1024), cur),
        out_shape=jax.ShapeDtypeStruct((S, 1024), BF16), scratch_shapes=[pltpu.VMEM((2, ATT_Q_HEADS // 2 * BLOCK, 2 * BLOCK), F32)],
        compiler_params=_params("arbitrary"))(z, z, z, cos_t, cos_t, sin_t, sin_t, vg, wc, bfull, bd, scale, sinks)


def _mixer_bwd(z, dcat, cos_t, sin_t, vg, wc, bfull, bd, scale, sinks, *, name):
    S = z.shape[0]
    nb = S // BLOCK

    def body(zc_ref, zpp_ref, zpk_ref, dc_ref, cc_ref, cp_ref, sc_ref, sp_ref, vg_ref, wc_ref, bf_ref, bd_ref, scale_ref, sinks_ref, wct_ref,
             dz_ref, dvg_ref, dwc_ref, dbf_ref, dbd_ref, dscale_ref, dsink_ref, carry_ref, pen_ref):
        n = pl.program_id(0)

        @pl.when(n == 0)
        def _():
            for ref in (dvg_ref, dwc_ref, dbf_ref, dbd_ref, dscale_ref, dsink_ref):
                ref[...] = jnp.zeros_like(ref)
            wide = (2 * BLOCK, ATT_Q_HEADS // 2 * BLOCK)
            ki = lax.broadcasted_iota(jnp.int32, wide, 0)
            qi = lax.broadcasted_iota(jnp.int32, wide, 1) % BLOCK
            inside = (ki > qi) & (ki <= qi + BLOCK)
            pen_ref[0] = jnp.where(inside, 0.0, NEG)
            pen_ref[1] = jnp.where(inside & (ki >= BLOCK), 0.0, NEG)

        @pl.when(n < nb)
        def _():
            keep = (n > 0).astype(F32)
            low = _low_half((BLOCK, 128))
            zg = zc_ref[:, GM0:POOL0].astype(F32)
            vg_v = vg_ref[...]
            _, (th, u, r, vhat, vn, mixed) = _gating_fwd(zg, vg_v, wc_ref, bf_ref[...])
            da = dc_ref[:, 0:256].astype(F32)
            dmixed = da * u
            dbf_ref[...] += dmixed
            dvn = []
            for blk in range(2):
                dmb = dmixed[:, 128 * blk:128 * (blk + 1)]
                vb = vn[:, 128 * blk:128 * (blk + 1)]
                acc = None
                for half in range(2):
                    g = 2 * blk + half
                    dmg = jnp.where(low if half == 0 else ~low, dmb, 0.0)
                    dwc_ref[g] += _tril(_dot(dmg, vb, NT))
                    part = _dot(_triu(wct_ref[g]), dmg, NN)
                    acc = part if acc is None else acc + part
                dvn.append(acc)
            dvn = jnp.concatenate(dvn, axis=1)
            dvg_ref[...] += jnp.sum(dvn * vhat, axis=0, keepdims=True)
            dvhat = dvn * vg_v
            dv = r * (dvhat - vhat * (_group_sum(dvhat * vhat) * (1.0 / HEAD_DIM)))
            dge = jnp.concatenate([da * mixed, dv], axis=1)
            own_g = dge * _gelu_grad(zg, th)
            p = zc_ref[:, POOL0:Q0].astype(F32)
            bd_v, scale_v = bd_ref[...], scale_ref[...]
            _, (inv, pooled, mapped) = _pool_fwd(p, zpp_ref[...].astype(F32) * keep, n, bd_v, scale_v)
            db = dc_ref[:, 256:512].astype(F32)
            dscale_ref[...] += jnp.sum(db * mapped, axis=0, keepdims=True)
            dmapped = db * scale_v
            dbd_ref[...] += _dot(pooled, dmapped, TN)
            dpooled = _dot(dmapped, bd_v, NT)
            dps = dpooled * inv
            back = []
            for blk in range(2):
                dpb = dps[:, 128 * blk:128 * (blk + 1)]
                t0 = _dot(_band(POOL_WINDOWS[2 * blk], True), dpb, NN)
                t1 = _dot(_band(POOL_WINDOWS[2 * blk + 1], True), dpb, NN)
                back.append(jnp.where(jnp.concatenate([low, low], axis=0), t0, t1))
            back = jnp.concatenate(back, axis=1)
            halo_p = back[:BLOCK] * keep
            own_p = back[BLOCK:] - dpooled
            cos_c, sin_c, cos_p, sin_p = cc_ref[...], sc_ref[...], cp_ref[...], sp_ref[...]
            qrot = _rope(zc_ref[:, Q0:KV0].astype(F32), cos_c, sin_c)
            kk = jnp.concatenate([_rope(zpk_ref[:, 0:128].astype(F32), cos_p, sin_p), _rope(zc_ref[:, KV0:KV0 + 128].astype(F32), cos_c, sin_c)], axis=0)
            vv = jnp.concatenate([zpk_ref[:, 128:256].astype(F32), zc_ref[:, KV0 + 128:Z_END].astype(F32)], axis=0)
            kt = jnp.transpose(kk).astype(BF16)
            kk, vv = kk.astype(BF16), vv.astype(BF16)
            pen = pen_ref[jnp.where(n > 0, 0, 1)]
            q_slabs = [qrot[:, 128 * s:128 * (s + 1)] for s in range(ATT_Q_HEADS // 2)]
            d_slabs = [dc_ref[:, 512 + 128 * s:512 + 128 * (s + 1)].astype(F32) for s in range(ATT_Q_HEADS // 2)]
            q_rolled = [pltpu.roll(v, HEAD_DIM, 1) for v in q_slabs]
            d_rolled = [pltpu.roll(v, HEAD_DIM, 1) for v in d_slabs]
            dkk = jnp.zeros((2 * BLOCK, 128), F32)
            dvv = jnp.zeros((2 * BLOCK, 128), F32)
            dq_heads = [None] * ATT_Q_HEADS
            for h in range(2):
                lanes = ~low if h else low
                heads = range(4 * h, 4 * h + 4)
                qm = jnp.concatenate([jnp.where(lanes, (q_slabs if j % 2 == h else q_rolled)[j // 2], 0.0) for j in heads], axis=0).astype(BF16)
                dom = jnp.concatenate([jnp.where(lanes, (d_slabs if j % 2 == h else d_rolled)[j // 2], 0.0) for j in heads], axis=0).astype(BF16)
                sink = jnp.concatenate([jnp.full((1, BLOCK), sinks_ref[j], F32) for j in heads], axis=1)
                sc = _dot(kk, qm, NT) * (HEAD_DIM ** -0.5) + pen
                mx = jnp.maximum(jnp.max(sc, axis=0, keepdims=True), sink)
                e = jnp.exp(sc - mx)
                es = jnp.exp(sink - mx)
                inv_den = 1.0 / (jnp.sum(e, axis=0, keepdims=True) + es)
                pt = e * inv_den
                dpt = _dot(vv, dom, NT)
                dvv = dvv + _dot(pt, dom, NN)
                dcol = jnp.sum(pt * dpt, axis=0, keepdims=True)
                dst = (pt * (dpt - dcol) * (HEAD_DIM ** -0.5)).astype(BF16)
                dsink = es * inv_den * dcol
                dkk = dkk + _dot(dst, qm, NN)
                dqt = _dot(kt, dst, NN)
                for i, j in enumerate(heads):
                    part = jnp.sum(dsink[:, BLOCK * i:BLOCK * (i + 1)], axis=1, keepdims=True)
                    dsink_ref[pl.ds(j, 1), :] = dsink_ref[pl.ds(j, 1), :] - jnp.broadcast_to(part, (1, 128))
                    dqh = jnp.transpose(dqt[:, BLOCK * i:BLOCK * (i + 1)])
                    dq_heads[j] = dqh if j % 2 == h else pltpu.roll(dqh, HEAD_DIM, 1)
            dqrot = jnp.concatenate([jnp.where(low, dq_heads[2 * s], dq_heads[2 * s + 1]) for s in range(ATT_Q_HEADS // 2)], axis=1)
            own_q = _rope_bwd(dqrot, cos_c, sin_c)
            own_k = _rope_bwd(dkk[BLOCK:], cos_c, sin_c)
            halo_k = _rope_bwd(dkk[:BLOCK], cos_p, sin_p) * keep
            own_v, halo_v = dvv[BLOCK:], dvv[:BLOCK] * keep

            @pl.when(n > 0)
            def _():
                dz_ref[:, GM0:POOL0] = carry_ref[:, GM0:POOL0].astype(BF16)
                dz_ref[:, POOL0:Q0] = (carry_ref[:, POOL0:Q0] + halo_p).astype(BF16)
                dz_ref[:, Q0:KV0] = carry_ref[:, Q0:KV0].astype(BF16)
                dz_ref[:, KV0:KV0 + 128] = (carry_ref[:, KV0:KV0 + 128] + halo_k).astype(BF16)
                dz_ref[:, KV0 + 128:Z_END] = (carry_ref[:, KV0 + 128:Z_END] + halo_v).astype(BF16)

            carry_ref[:, GM0:POOL0] = own_g
            carry_ref[:, POOL0:Q0] = own_p
            carry_ref[:, Q0:KV0] = own_q
            carry_ref[:, KV0:KV0 + 128] = own_k
            carry_ref[:, KV0 + 128:Z_END] = own_v

        @pl.when(n == nb)
        def _():
            dz_ref[...] = carry_ref[...].astype(BF16)

    cur = lambda n: (jnp.minimum(n, nb - 1), 0)
    prev = lambda n: (jnp.maximum(jnp.minimum(n, nb - 1) - 1, 0), 0)
    in_specs = [
        pl.BlockSpec((BLOCK, Z_END), cur),
        pl.BlockSpec((BLOCK, 256), lambda n: (jnp.maximum(jnp.minimum(n, nb - 1) - 1, 0), POOL0 // 256)),
        pl.BlockSpec((BLOCK, 256), lambda n: (jnp.maximum(jnp.minimum(n, nb - 1) - 1, 0), KV0 // 256)),
        pl.BlockSpec((BLOCK, 1024), cur),
        pl.BlockSpec((BLOCK, 128), cur), pl.BlockSpec((BLOCK, 128), prev),
        pl.BlockSpec((BLOCK, 128), cur), pl.BlockSpec((BLOCK, 128), prev),
        _full((1, 256)), _full((4, BLOCK, BLOCK)), _full((BLOCK, 256)), _full((256, 256)), _full((1, 256)),
        pl.BlockSpec(memory_space=pltpu.SMEM), _full((4, BLOCK, BLOCK)),
    ]
    out_specs = [pl.BlockSpec((BLOCK, Z_END), lambda n: (jnp.maximum(n - 1, 0), 0)),
                 _full((1, 256)), _full((4, BLOCK, BLOCK)), _full((BLOCK, 256)), _full((256, 256)), _full((1, 256)), _full((8, 128))]
    out_shape = [jax.ShapeDtypeStruct((S, Z_END), BF16), jax.ShapeDtypeStruct((1, 256), F32),
                 jax.ShapeDtypeStruct((4, BLOCK, BLOCK), F32), jax.ShapeDtypeStruct((BLOCK, 256), F32),
                 jax.ShapeDtypeStruct((256, 256), F32), jax.ShapeDtypeStruct((1, 256), F32), jax.ShapeDtypeStruct((8, 128), F32)]
    return pl.pallas_call(
        body, name=name, grid=(nb + 1,), in_specs=in_specs, out_specs=out_specs, out_shape=out_shape,
        scratch_shapes=[pltpu.VMEM((BLOCK, Z_END), F32), pltpu.VMEM((2, 2 * BLOCK, ATT_Q_HEADS // 2 * BLOCK), F32)],
        compiler_params=_params("arbitrary"))(z, z, z, dcat, cos_t, cos_t, sin_t, sin_t, vg, wc, bfull, bd, scale, sinks, jnp.swapaxes(wc, 1, 2))


HBM = pl.BlockSpec(memory_space=pl.ANY)


def _place():
    return lax.axis_index("x"), lax.axis_index("y"), lax.axis_index("c")


def _chip(k, x, y):
    return (1 - x if k & 1 else x, 1 - y if k & 2 else y)


def _block_rows(ref, rows, dev):
    start = pl.multiple_of((4 * dev[0] + 2 * dev[1] + dev[2]) * rows, 8)
    return ref.at[pl.ds(start, rows), :]


_HBM_SPEC = pl.BlockSpec(memory_space=pltpu.HBM)
_SEM_SPEC = pl.BlockSpec(memory_space=pltpu.SEMAPHORE)
_SPLIT_PARAMS = dict(compiler_params=pltpu.CompilerParams(has_side_effects=pltpu.SideEffectType.DATAFLOW_SIDE_EFFECTING))


def _descriptors(copies, refs, send_sems, recv_sems):
    return [pltpu.make_async_remote_copy(src_ref=s, dst_ref=d, send_sem=send_sems.at[i], recv_sem=recv_sems.at[i], device_id=to, device_id_type=MESH)
            for i, (s, d, to) in enumerate(copies(refs))]


def _start(copies, arrays, fresh, n_copies, *, name):
    operands = [pltpu.with_memory_space_constraint(v, pltpu.HBM) for v in (*arrays, *[lax.empty(f.shape, f.dtype) for f in fresh])]
    n = len(operands)

    def body(*refs):
        for dma in _descriptors(copies, refs[:n], refs[n], refs[n + 1]):
            dma.start()
        refs[-1][...] = jnp.zeros_like(refs[-1])

    res = pl.pallas_call(
        body, name=name, in_specs=[_HBM_SPEC] * n,
        out_shape=(pltpu.SemaphoreType.DMA((n_copies,)), pltpu.SemaphoreType.DMA((n_copies,)),
                   *[pltpu.HBM(v.shape, v.dtype) for v in operands], jax.ShapeDtypeStruct((8, 128), F32)),
        out_specs=(_SEM_SPEC, _SEM_SPEC, *[_HBM_SPEC] * n, pl.BlockSpec(memory_space=pltpu.VMEM)),
        input_output_aliases={i: 2 + i for i in range(n)}, **_SPLIT_PARAMS)(*operands)
    return res[0], res[1], list(res[2:2 + n]), res[-1]


def _finish(copies, send_sems, recv_sems, arrays, after, *, name):
    n = len(arrays)

    def body(*refs):
        for dma in _descriptors(copies, refs[:n], refs[n], refs[n + 1]):
            dma.wait_send()
            dma.wait_recv()

    return list(pl.pallas_call(
        body, name=name, in_specs=[_HBM_SPEC] * n + [_SEM_SPEC, _SEM_SPEC] + [HBM] * len(after),
        out_shape=tuple(pltpu.HBM(v.shape, v.dtype) for v in arrays), out_specs=tuple([_HBM_SPEC] * n),
        input_output_aliases={i: i for i in range(n)}, **_SPLIT_PARAMS)(*arrays, send_sems, recv_sems, *after))


def _tie(value, tokens):
    if not tokens:
        return value

    def body(*refs):
        pass

    return pl.pallas_call(body, name="tie", in_specs=[HBM] * (1 + len(tokens)), out_specs=HBM,
                          out_shape=jax.ShapeDtypeStruct(value.shape, value.dtype), input_output_aliases={0: 0})(value, *tokens)


def _gather_copies(stage):
    def copies(refs):
        x, y, c = _place()
        me, sibling = (x, y, c), (x, y, 1 - c)
        out = []
        for ref in refs:
            r = ref.shape[0] // N_DEV
            if stage == "a":
                hops = [(me, sibling)] + [(me, (*_chip(k, x, y), c)) for k in (1, 2, 3)]
            else:
                hops = [((*_chip(k, x, y), c), sibling) for k in (1, 2, 3)]
            for block, to in hops:
                rows = _block_rows(ref, r, block)
                out.append((rows, rows, to))
        return out
    return copies


def _scatter_copies(stage, n):
    def copies(refs):
        x, y, c = _place()
        out = []
        for w in range(n):
            src, dst = refs[w], refs[n + w]
            if stage == "a":
                r = src.shape[0] // N_DEV
                out += [(_block_rows(src, r, (*_chip(k, x, y), 1 - c)), dst.at[k], (x, y, 1 - c)) for k in range(4)]
            else:
                out += [(src.at[k - 1], dst.at[k - 1], (*_chip(k, x, y), c)) for k in (1, 2, 3)]
        return out
    return copies


def _place_own(shards, layer, blocks, *, name, dtype=BF16):
    _, r, cdim = shards.shape

    def body(blocks_ref, s_ref, o_ref):
        o_ref[...] = s_ref[...].astype(dtype)

    return pl.pallas_call(
        body, name=name,
        grid_spec=pltpu.PrefetchScalarGridSpec(
            num_scalar_prefetch=1, grid=(1,), in_specs=[pl.BlockSpec((None, r, cdim), lambda i, blocks: (layer, 0, 0))],
            out_specs=pl.BlockSpec((r, cdim), lambda i, blocks: (blocks[0], 0))),
        out_shape=jax.ShapeDtypeStruct((N_DEV * r, cdim), dtype), compiler_params=_params("arbitrary"))(blocks, shards)


def _chip_sum(grad, got, blocks, *, name):
    _, r, cdim = got.shape
    tr = r if r <= 256 else r // (2 if r % 32 == 0 and r // 2 <= 256 else 4)
    steps = r // tr

    def body(blocks_ref, a0_ref, a1_ref, a2_ref, a3_ref, b_ref, mine_ref, away_ref):
        mine_ref[...] = a0_ref[...] + b_ref[0]
        for k, a_ref in ((1, a1_ref), (2, a2_ref), (3, a3_ref)):
            away_ref[k - 1] = (a_ref[...] + b_ref[k]).astype(BF16)

    own = [pl.BlockSpec((tr, cdim), lambda i, blocks, k=k: (blocks[k] * steps + i, 0)) for k in range(4)]
    return pl.pallas_call(
        body, name=name,
        grid_spec=pltpu.PrefetchScalarGridSpec(
            num_scalar_prefetch=1, grid=(steps,),
            in_specs=own + [pl.BlockSpec((4, tr, cdim), lambda i, blocks: (0, i, 0))],
            out_specs=[pl.BlockSpec((tr, cdim), lambda i, blocks: (i, 0)), pl.BlockSpec((3, tr, cdim), lambda i, blocks: (0, i, 0))]),
        out_shape=[jax.ShapeDtypeStruct((r, cdim), F32), jax.ShapeDtypeStruct((3, r, cdim), BF16)],
        compiler_params=_params("arbitrary"))(blocks, grad, grad, grad, grad, got)


def _final_sum(mine, got, *, name):
    r, cdim = mine.shape
    tr = r if r <= 256 else r // (2 if r % 32 == 0 and r // 2 <= 256 else 4)

    def body(m_ref, g_ref, o_ref):
        o_ref[...] = ((m_ref[...] + g_ref[0].astype(F32)) + g_ref[1].astype(F32)) + g_ref[2].astype(F32)

    return pl.pallas_call(
        body, name=name, grid=(r // tr,),
        in_specs=[pl.BlockSpec((tr, cdim), lambda i: (i, 0)), pl.BlockSpec((3, tr, cdim), lambda i: (0, i, 0))],
        out_specs=pl.BlockSpec((tr, cdim), lambda i: (i, 0)), out_shape=jax.ShapeDtypeStruct((r, cdim), F32),
        compiler_params=_params("parallel"))(mine, got)


def _adamw_math(w, g, m, v):
    m = ADAM_B1 * m + (1.0 - ADAM_B1) * g
    v = ADAM_B2 * v + (1.0 - ADAM_B2) * (g * g)
    m_hat = m / (1.0 - ADAM_B1 ** ADAM_STEP)
    v_hat = v / (1.0 - ADAM_B2 ** ADAM_STEP)
    return -ADAM_LR * (m_hat / (jnp.sqrt(v_hat) + ADAM_EPS) + ADAM_WD * w), m, v


def _adamw(w, g, m, v, *, name):
    L, r, cdim = w.shape
    tr = r if r <= 512 else r // 2
    assert r % tr == 0 and tr % 8 == 0

    def body(w_ref, g_ref, m_ref, v_ref, d_ref, nm_ref, nv_ref):
        d_ref[...], nm_ref[...], nv_ref[...] = _adamw_math(w_ref[...], g_ref[...], m_ref[...], v_ref[...])

    blk = pl.BlockSpec((None, tr, cdim), lambda l, i: (l, i, 0))
    return pl.pallas_call(body, name=name, grid=(L, r // tr), in_specs=[blk] * 4, out_specs=[blk] * 3,
                          out_shape=[jax.ShapeDtypeStruct(w.shape, F32)] * 3, compiler_params=_params("parallel", "parallel"))(w, g, m, v)


def _small_update(parts, w, m, v, *, name):
    _, p, lanes = parts.shape
    tp = p // 2 if p % 16 == 0 else p

    def body(p_ref, w_ref, m_ref, v_ref, g_ref, d_ref, nm_ref, nv_ref):
        g = p_ref[0]
        for i in range(1, N_DEV):
            g = g + p_ref[i]
        g_ref[...] = g
        d_ref[...], nm_ref[...], nv_ref[...] = _adamw_math(w_ref[...], g, m_ref[...], v_ref[...])

    blk = pl.BlockSpec((tp, lanes), lambda i: (i, 0))
    return pl.pallas_call(body, name=name, grid=(p // tp,), in_specs=[pl.BlockSpec((N_DEV, tp, lanes), lambda i: (0, i, 0)), blk, blk, blk],
                          out_specs=[blk] * 4, out_shape=[jax.ShapeDtypeStruct((p, lanes), F32)] * 4,
                          compiler_params=_params("parallel"))(parts, w, m, v)


BIG = ("w_in", "w_o", "w_xq", "w_xkv", "w_xo", "w_gate_up", "w_down")
TRANSPOSED = ("w_in", "w_xkv", "w_gate_up")
SMALL = ("mem_norm_g", "mix_pre_g", "mix_post_g", "gm_v_g", "gm_w_s", "gm_b_s", "pool_w", "pool_scale", "attn_sinks",
         "x_pre_g", "x_post_g", "ffn_pre_g", "ffn_post_g")
WEIGHTS = ("mem_norm_g", "mix_pre_g", "mix_post_g", "w_in", "gm_v_g", "gm_w_s", "gm_b_s", "pool_w", "pool_scale", "attn_sinks", "w_o",
           "x_pre_g", "x_post_g", "w_xq", "w_xkv", "w_xo", "ffn_pre_g", "ffn_post_g", "w_gate_up", "w_down")
PACK_QUANTUM = 8 * 128


def _pack(arrays):
    flat = []
    for a in arrays:
        a = a.reshape(-1).astype(F32)
        flat.append(jnp.pad(a, (0, -a.size % PACK_QUANTUM)))
    return jnp.concatenate(flat).reshape(-1, 128)


def _unpack(pack, shapes):
    out, row = [], 0
    for s in shapes:
        size = math.prod(s)
        rows = (size + (-size % PACK_QUANTUM)) // 128
        out.append(pack[row:row + rows].reshape(-1)[:size].reshape(s))
        row += rows
    return out


def _row(v):
    return v.reshape(1, -1)


def _local_step(x, mem, positions, target, small, weight, fwd_hook=None, bwd_hook=None):
    fwd_hook = fwd_hook or (lambda l, point, v: v)
    bwd_hook = bwd_hook or (lambda l, point, v, gb: v)
    depth = small["mix_pre_g"].shape[0]
    half = HEAD_DIM // 2
    inv = ROPE_THETA ** (-jnp.arange(half, dtype=F32) / half)
    cos_t, sin_t = _rope_tables(positions.reshape(-1, 1), jnp.tile(inv, 128 // half).reshape(1, 128), name="rope_tables")
    mem_g = _row(small["mem_norm_g"])
    saved = []
    for l in range(depth):
        w = functools.partial(weight, l)
        bfull = jnp.repeat(small["gm_b_s"][l].T, HEAD_DIM, axis=1)
        bd = jax.scipy.linalg.block_diag(*[small["pool_w"][l, g] for g in range(len(POOL_WINDOWS))])
        mix_par = (_row(small["gm_v_g"][l]), small["gm_w_s"][l], bfull, bd, _row(small["pool_scale"][l]), small["attn_sinks"][l])
        if l == 0:
            z, h1 = _mm_fwd(fwd_hook(l, 0, x), w("w_in"), nt=True, pre="norm", g_pre=_row(small["mix_pre_g"][l]), name="first_in_proj")
        else:
            h1 = fwd_hook(l, 0, h1)
            z, = _mm_fwd(h1, w("w_in"), nt=True, name="in_proj")
        cat = fwd_hook(l, 1, _mixer_fwd(z, cos_t, sin_t, *mix_par, name="mixer_fwd"))
        mix, x1, h2 = _mm_fwd(cat, w("w_o"), nt=False, post="norm_res", g_post=_row(small["mix_post_g"][l]), xres=x,
                              g_next=_row(small["x_pre_g"][l]), name="mix_out")
        qx, = _mm_fwd(h2, w("w_xq"), nt=False, name="xq_proj")
        kv, memn = _mm_fwd(mem, w("w_xkv"), nt=True, pre="norm", g_pre=mem_g, name="xkv_proj")
        ox = _xattn_fwd(qx, kv, name="xattn_fwd")
        xo, x2, h3 = _mm_fwd(ox, w("w_xo"), nt=False, post="norm_res", g_post=_row(small["x_post_g"][l]), xres=x1,
                             g_next=_row(small["ffn_pre_g"][l]), name="xattn_out")
        h3 = fwd_hook(l, 2, h3)
        gu, = _mm_fwd(h3, w("w_gate_up"), nt=True, name="ffn_in")
        last = l + 1 == depth
        f, x3, *h_next = _mm_fwd(gu, w("w_down"), nt=False, pre="swiglu", post="norm_res", g_post=_row(small["ffn_post_g"][l]), xres=x2,
                                 g_next=None if last else _row(small["mix_pre_g"][l + 1]), tile=256, name="last_ffn_out" if last else "ffn_out")
        saved.append((mix_par, x, z, h1, cat, mix, x1, qx, h2, kv, memn, ox, xo, x2, gu, h3, f))
        if last:
            x = fwd_hook(l, 3, x3)
        else:
            x, h1 = x3, fwd_hook(l, 3, h_next[0])

    gs = {n: [None] * depth for n in SMALL if n != "mem_norm_g"}
    gb = {n: [None] * depth for n in BIG}
    dmemn = [None] * depth
    loss, dx, df, gs["ffn_post_g"][depth - 1] = _loss_grad(x, target, saved[-1][-1], _row(small["ffn_post_g"][depth - 1]), name="loss")
    for l in reversed(range(depth)):
        w = functools.partial(weight, l)
        mix_par, x0, z, h1, cat, mix, x1, qx, h2, kv, memn, ox, xo, x2, gu, h3, f = saved[l]
        dgu = _ffn_out_bwd(df, w("w_down"), gu, name="ffn_out_bwd")
        gb["w_down"][l] = _wgrad(gu, df, swiglu=True, name="w_down_grad")
        gb["w_gate_up"][l] = _wgrad(dgu, h3, name="w_gate_up_grad")
        dx2, gs["ffn_pre_g"][l], dxo, gs["x_post_g"][l] = _mm_bwd_post(
            dgu, w("w_gate_up"), x2, _row(small["ffn_pre_g"][l]), dx, nt=False, below=(xo, _row(small["x_post_g"][l])), tile=256, name="ffn_in_bwd")
        dxo = bwd_hook(l, 1, dxo, gb)
        do, = _mm_fwd(dxo, w("w_xo"), nt=True, name="xattn_out_bwd")
        gb["w_xo"][l] = _wgrad(ox, dxo, name="w_xo_grad")
        dqx, dkv = _xattn_bwd(qx, kv, do, name="xattn_bwd")
        gb["w_xq"][l] = _wgrad(h2, dqx, name="w_xq_grad")
        dx1, gs["x_pre_g"][l], dmix, gs["mix_post_g"][l] = _mm_bwd_post(
            dqx, w("w_xq"), x1, _row(small["x_pre_g"][l]), dx2, nt=True, below=(mix, _row(small["mix_post_g"][l])), name="xq_proj_bwd")
        dmix = bwd_hook(l, 2, dmix, gb)
        gb["w_xkv"][l] = _wgrad(dkv, memn, name="w_xkv_grad")
        dmemn[l] = _mm_fwd(dkv, w("w_xkv"), nt=False, out_dtype=F32, name="xkv_proj_bwd")[0]
        dcat, = _mm_fwd(dmix, w("w_o"), nt=True, name="mix_out_bwd")
        gb["w_o"][l] = _wgrad(cat, dmix, name="w_o_grad")
        dz, dvg, dwc, dbf, dbd, dscale, dsink = _mixer_bwd(z, dcat, cos_t, sin_t, *mix_par, name="mixer_bwd")
        gs["gm_v_g"][l], gs["gm_w_s"][l], gs["pool_scale"][l], gs["attn_sinks"][l] = dvg, dwc, dscale, dsink[:, 0]
        gs["gm_b_s"][l] = dbf.reshape(BLOCK, -1, HEAD_DIM).sum(-1).T
        gs["pool_w"][l] = jnp.stack([dbd[HEAD_DIM * g:HEAD_DIM * (g + 1), HEAD_DIM * g:HEAD_DIM * (g + 1)] for g in range(len(POOL_WINDOWS))])
        gb["w_in"][l] = _wgrad(dz, h1, name="w_in_grad")
        if l == 0:
            dx, gs["mix_pre_g"][l] = _mm_bwd_post(dz, w("w_in"), x0, _row(small["mix_pre_g"][l]), dx1, nt=False, name="first_in_proj_bwd")
            dx = bwd_hook(l, 3, dx, gb)
        else:
            dx, gs["mix_pre_g"][l], df, gs["ffn_post_g"][l - 1] = _mm_bwd_post(
                dz, w("w_in"), x0, _row(small["mix_pre_g"][l]), dx1, nt=False, below=(saved[l - 1][-1], _row(small["ffn_post_g"][l - 1])),
                name="in_proj_bwd")
            df = bwd_hook(l, 3, df, gb)

    small_grads = {n: jnp.stack(v).reshape(small[n].shape) for n, v in gs.items()}
    small_grads["mem_norm_g"] = _norm_dgain(mem, jnp.stack(dmemn), name="mem_norm_grad").reshape(-1)
    return loss, dx, small_grads, gb


EARLY = ("w_in", "w_o", "w_xq", "w_xkv", "w_xo")
LATE = ("w_gate_up", "w_down")


def _step(a):
    depth = a["mix_pre_g"].shape[0]
    x, mem, target = a["x"][0], a["mem"][0], a["loss_target"][0]
    px, py, pc = _place()
    blocks = jnp.stack([4 * cx + 2 * cy + pc for cx, cy in (_chip(k, px, py) for k in range(4))]).astype(jnp.int32)
    flight = {}

    def laid_out(v, n):
        return v.transpose(0, 2, 1) if n in TRANSPOSED else v

    whole = {(l, n): _place_own(laid_out(a[n], n), l, blocks, name="place_" + n) for l in range(depth) for n in BIG}
    gather = {"first_in": [(0, "w_in")], "first_mid": [(0, n) for n in EARLY[1:]], "first_late": [(0, n) for n in LATE],
              "second_early": [(1, n) for n in EARLY], "second_late": [(1, n) for n in LATE]}
    gather.update({f"layer{l}": [(l, n) for n in BIG] for l in range(2, depth)})
    forward_plan = {(0, 0): [("a", "first_in"), ("a", "first_mid"), ("a", "first_late"), ("a", "second_early"), ("a", "second_late"),
                             ("mid", "first_in"), ("end", "first_in")],
                    (0, 1): [("mid", "first_mid"), ("end", "first_mid")], (0, 2): [("mid", "first_late"), ("end", "first_late")],
                    (0, 3): [("mid", "second_early")], (1, 0): [("end", "second_early"), ("a", "layer2")],
                    (1, 1): [("mid", "second_late")], (1, 2): [("end", "second_late"), ("mid", "layer2")], (1, 3): [("end", "layer2")]}
    for l in range(2, depth):
        forward_plan.update({(l, 0): [("a", f"layer{l + 1}")], (l, 2): [("mid", f"layer{l + 1}")], (l, 3): [("end", f"layer{l + 1}")]})
    if depth == 1:
        gather = {tag: keys for tag, keys in gather.items() if tag.startswith("first")}

    issued = []

    def gather_step(kind, tag, v):
        keys = gather[tag]
        if kind == "a":
            arrays = [whole[k] for k in keys]
            arrays[0] = _tie(arrays[0], issued[-1:])
            send, recv, arrays, token = _start(_gather_copies("a"), arrays, [], 4 * len(keys), name=f"gather_a_{tag}")
            flight[tag] = (send, recv, arrays)
            issued.append(token)
            return [token]
        send, recv, arrays = flight.pop(tag)
        if kind == "mid":
            arrays = _finish(_gather_copies("a"), send, recv, arrays, [v], name=f"gather_a_done_{tag}")
            send, recv, arrays, token = _start(_gather_copies("b"), arrays, [], 3 * len(keys), name=f"gather_b_{tag}")
            flight[tag] = (send, recv, arrays)
            return [token]
        for key, arr in zip(keys, _finish(_gather_copies("b"), send, recv, arrays, [v], name=f"gather_b_done_{tag}")):
            whole[key] = arr
        return []

    def fwd_hook(l, point, v):
        tokens = []
        for kind, tag in forward_plan.get((l, point), []):
            if tag in gather:
                tokens += gather_step(kind, tag, v)
        return _tie(v, tokens)

    mine = {}
    scatter = {f"layer{l}": [(l, n) for n in BIG] for l in range(1, depth)}
    scatter.update({"late0": [(0, n) for n in LATE], "attn0": [(0, "w_xq"), (0, "w_xo")], "rest0": [(0, "w_in"), (0, "w_o"), (0, "w_xkv")]})
    backward_plan = {}
    for l in range(1, depth):
        backward_plan[(l, 3)] = [("a", f"layer{l}")]
        backward_plan[(l - 1, 1)] = [("mid", f"layer{l}")]
        backward_plan[(l - 1, 2)] = [("end", f"layer{l}")]
    backward_plan[(0, 1)] = [("a", "late0")] + backward_plan.get((0, 1), [])
    backward_plan[(0, 2)] = [("a", "attn0"), ("mid", "late0")] + backward_plan.get((0, 2), [])
    backward_plan[(0, 3)] = [("end", "late0"), ("mid", "attn0"), ("a", "rest0")]

    def scatter_step(kind, tag, v, gb):
        keys = scatter[tag]
        n = len(keys)
        if kind == "a":
            grads = [gb[name][l] for l, name in keys]
            zones = [jax.ShapeDtypeStruct((4, g.shape[0] // N_DEV, g.shape[1]), F32) for g in grads]
            send, recv, arrays, token = _start(_scatter_copies("a", n), grads, zones, 4 * n, name=f"scatter_a_{tag}")
            flight[tag] = (send, recv, arrays, None)
            return [token]
        send, recv, arrays, kept = flight.pop(tag)
        if kind == "mid":
            arrays = _finish(_scatter_copies("a", n), send, recv, arrays, [v], name=f"scatter_a_done_{tag}")
            sums = [_chip_sum(arrays[i], arrays[n + i], blocks, name="chip_sum") for i in range(n)]
            zones = [jax.ShapeDtypeStruct(s[1].shape, BF16) for s in sums]
            send, recv, arrays, token = _start(_scatter_copies("b", n), [s[1] for s in sums], zones, 3 * n, name=f"scatter_b_{tag}")
            flight[tag] = (send, recv, arrays, [s[0] for s in sums])
            return [token]
        arrays = _finish(_scatter_copies("b", n), send, recv, arrays, [v], name=f"scatter_b_done_{tag}")
        for i, key in enumerate(keys):
            mine[key] = _final_sum(kept[i], arrays[n + i], name="final_sum")
        return []

    def bwd_hook(l, point, v, gb):
        tokens = []
        for kind, tag in backward_plan.get((l, point), []):
            tokens += scatter_step(kind, tag, v, gb)
        return _tie(v, tokens)

    loss, dx, small_grads, _ = _local_step(x, mem, a["positions"][0], target, {n: a[n] for n in SMALL}, lambda l, n: whole[(l, n)],
                                           fwd_hook, bwd_hook)

    out = {}

    def update(n, tokens):
        g = _tie(jnp.stack([mine[(l, n)] for l in range(depth)]), tokens)
        moved = _adamw(laid_out(a[n], n), g, laid_out(a["m_" + n], n), laid_out(a["v_" + n], n), name="adamw_" + n)
        for p, v in zip(("grad_", "delta_", "new_m_", "new_v_"), (g, *moved)):
            out[p + n] = laid_out(v, n)
        return moved[0]

    part = _tie(_pack([small_grads[n] for n in SMALL]), [dx])
    placed = _place_own(part[None], 0, blocks, dtype=F32, name="place_small_grads")
    send, recv, arrays, token = _start(_gather_copies("a"), [placed], [], 4, name="gather_a_small_grads")
    behind = [update(n, [token]) for n in LATE]
    tokens = scatter_step("mid", "rest0", behind[-1], None)
    arrays = _finish(_gather_copies("a"), send, recv, arrays, behind, name="gather_a_done_small_grads")
    send, recv, arrays, token = _start(_gather_copies("b"), arrays, [], 3, name="gather_b_small_grads")
    scatter_step("end", "attn0", behind[-1], None)
    behind = [update(n, tokens + [token]) for n in ("w_xq", "w_xo")]
    parts = _finish(_gather_copies("b"), send, recv, arrays, behind, name="gather_b_done_small_grads")[0].reshape(N_DEV, *part.shape)
    packs = _small_update(parts, *[_pack([a[p + n] for n in SMALL]) for p in ("", "m_", "v_")], name="small_update")
    for p, pack in zip(("grad_", "delta_", "new_m_", "new_v_"), packs):
        for n, v in zip(SMALL, _unpack(pack, [a[n].shape for n in SMALL])):
            out[p + n] = v
    scatter_step("end", "rest0", packs[0], None)
    for n in ("w_in", "w_o", "w_xkv"):
        update(n, [])

    total = lax.psum(loss[0, 0], ("x", "y", "c"))
    return (total, dx[None], *[out[p + n] for p in ("grad_", "delta_", "new_m_", "new_v_") for n in WEIGHTS])


def kernel(x, mem, positions, mem_norm_g, mix_pre_g, mix_post_g, w_in, gm_v_g, gm_w_s, gm_b_s, pool_w, pool_scale, attn_sinks,
           w_o, x_pre_g, x_post_g, w_xq, w_xkv, w_xo, ffn_pre_g, ffn_post_g, w_gate_up, w_down, loss_target, m_mem_norm_g,
           m_mix_pre_g, m_mix_post_g, m_w_in, m_gm_v_g, m_gm_w_s, m_gm_b_s, m_pool_w, m_pool_scale, m_attn_sinks, m_w_o,
           m_x_pre_g, m_x_post_g, m_w_xq, m_w_xkv, m_w_xo, m_ffn_pre_g, m_ffn_post_g, m_w_gate_up, m_w_down, v_mem_norm_g,
           v_mix_pre_g, v_mix_post_g, v_w_in, v_gm_v_g, v_gm_w_s, v_gm_b_s, v_pool_w, v_pool_scale, v_attn_sinks, v_w_o,
           v_x_pre_g, v_x_post_g, v_w_xq, v_w_xkv, v_w_xo, v_ffn_pre_g, v_ffn_post_g, v_w_gate_up, v_w_down):
    return _step(dict(locals()))
```

```python
import functools
import math

import jax
import jax.numpy as jnp
from jax import lax
from jax.experimental import pallas as pl
from jax.experimental.pallas import tpu as pltpu

F32, BF16 = jnp.float32, jnp.bfloat16
EPS = 1e-6
HEAD_DIM = 64
BLOCK = 128
POOL_WINDOWS = (2, 4, 8, 16)
ATT_Q_HEADS = 8
X_HEADS = 4
ROPE_THETA = 10000.0
ADAM_LR, ADAM_B1, ADAM_B2, ADAM_EPS, ADAM_WD, ADAM_STEP = 0.001, 0.9, 0.999, 1e-08, 0.01, 10
N_DEV = 8
TOKEN_TILE = 512
VMEM_LIMIT_BYTES = 50 * 2**20
NEG = -1e30
MESH = pl.DeviceIdType.MESH

NN = ((1,), (0,))
NT = ((1,), (1,))
TN = ((0,), (0,))


def _dot(a, b, dims):
    return lax.dot_general(a.astype(BF16), b.astype(BF16), (dims, ((), ())), preferred_element_type=F32)


def _params(*sem):
    return pltpu.CompilerParams(dimension_semantics=sem, vmem_limit_bytes=VMEM_LIMIT_BYTES)


def _full(shape):
    return pl.BlockSpec(shape, lambda *_: (0,) * len(shape))


def _lane(shape):
    return lax.broadcasted_iota(jnp.int32, shape, len(shape) - 1)


def _rms_fwd(x, g):
    r = lax.rsqrt(jnp.mean(x * x, axis=-1, keepdims=True) + EPS)
    return x * r * g


def _rms_bwd(x, g, dy):
    r = lax.rsqrt(jnp.mean(x * x, axis=-1, keepdims=True) + EPS)
    xh = x * r
    dg = jnp.sum(dy * xh, axis=0, keepdims=True)
    dxh = dy * g
    dx = r * (dxh - xh * jnp.mean(dxh * xh, axis=-1, keepdims=True))
    return dx, dg


def _silu_parts(gate):
    sg = 1.0 / (1.0 + jnp.exp(-gate))
    return gate * sg, sg


def _swiglu_cols(gu_ref, out_ref, width):
    step = 128 * max(1, (width // 128) // 4)
    for c0 in range(0, width, step):
        c1 = min(width, c0 + step)
        gate = gu_ref[:, c0:c1].astype(F32)
        up = gu_ref[:, width + c0:width + c1].astype(F32)
        out_ref[:, c0:c1] = (_silu_parts(gate)[0] * up).astype(out_ref.dtype)


def _col_tile(n, cap=1408):
    best = n
    for t in range(128, cap + 1, 128):
        if n % t == 0:
            best = t
    return best if n > cap else n


def _mm_fwd(a, w, *, nt, name, pre=None, g_pre=None, post=None, g_post=None, xres=None, g_next=None, out_dtype=BF16, tile=TOKEN_TILE):
    S = a.shape[0]
    n_out, k = (w.shape[0], w.shape[1]) if nt else (w.shape[1], w.shape[0])
    T = min(tile, S)
    tn = n_out if post else _col_tile(n_out)
    grid = (S // T, n_out // tn)

    def body(*refs):
        it = iter(refs)
        a_ref, w_ref = next(it), next(it)
        gpre_ref = next(it) if pre == "norm" else None
        gpost_ref, x_ref = (next(it), next(it)) if post else (None, None)
        gnext_ref = next(it) if g_next is not None else None
        outs = [next(it) for _ in range((2 if post else 1) + (1 if pre else 0))]
        hnext_ref = next(it) if g_next is not None else None
        as_ref = next(it) if pre else None
        if pre:
            @pl.when(pl.program_id(1) == 0)
            def _():
                if pre == "norm":
                    h = _rms_fwd(a_ref[...], gpre_ref[...]).astype(BF16)
                    outs[-1][...] = h
                    as_ref[...] = h
                else:
                    _swiglu_cols(a_ref, as_ref, k)
                    outs[-1][...] = as_ref[...]
            av = as_ref[...]
        else:
            av = a_ref[...]
        acc = _dot(av, w_ref[...], NT if nt else NN)
        if post:
            outs[0][...] = acc.astype(BF16)
            x_new = x_ref[...] + _rms_fwd(acc, gpost_ref[...])
            outs[1][...] = x_new
            if g_next is not None:
                hnext_ref[...] = _rms_fwd(x_new, gnext_ref[...]).astype(BF16)
        else:
            outs[0][...] = acc.astype(out_dtype)

    in_specs = [pl.BlockSpec((T, a.shape[1]), lambda i, j: (i, 0)),
                pl.BlockSpec((tn, k), lambda i, j: (j, 0)) if nt else pl.BlockSpec((k, tn), lambda i, j: (0, j))]
    args = [a, w]
    if pre == "norm":
        in_specs.append(_full((1, k)))
        args.append(g_pre)
    if post:
        in_specs += [_full((1, n_out)), pl.BlockSpec((T, n_out), lambda i, j: (i, 0))]
        args += [g_post, xres]
    if g_next is not None:
        in_specs.append(_full((1, n_out)))
        args.append(g_next)
    out_block = pl.BlockSpec((T, tn), lambda i, j: (i, j))
    if post:
        out_shape = [jax.ShapeDtypeStruct((S, n_out), BF16), jax.ShapeDtypeStruct((S, n_out), F32)]
        out_specs = [out_block, out_block]
    else:
        out_shape = [jax.ShapeDtypeStruct((S, n_out), out_dtype)]
        out_specs = [out_block]
    if pre:
        out_shape.append(jax.ShapeDtypeStruct((S, k), BF16))
        out_specs.append(pl.BlockSpec((T, k), lambda i, j: (i, 0)))
    if g_next is not None:
        out_shape.append(jax.ShapeDtypeStruct((S, n_out), BF16))
        out_specs.append(out_block)
    scratch = [pltpu.VMEM((T, k), BF16)] if pre else []
    return pl.pallas_call(body, name=name, grid=grid, in_specs=in_specs, out_specs=out_specs, out_shape=out_shape,
                          scratch_shapes=scratch, compiler_params=_params("parallel", "arbitrary"))(*args)


def _mm_bwd_post(dy, w, x, g, dx_in, *, nt, name, below=None, tile=TOKEN_TILE):
    S, k = dy.shape
    n = x.shape[1]
    T = min(tile, S)

    def body(*refs):
        dy_ref, w_ref, x_ref, g_ref, dxi_ref = refs[:5]
        m_ref, gb_ref = refs[5:7] if below else (None, None)
        dxo_ref, dg_ref = refs[-4:-2] if below else refs[-2:]

        @pl.when(pl.program_id(0) == 0)
        def _():
            dg_ref[...] = jnp.zeros_like(dg_ref)
            if below:
                refs[-1][...] = jnp.zeros_like(refs[-1])

        dh = _dot(dy_ref[...], w_ref[...], NT if nt else NN)
        dxn, dg = _rms_bwd(x_ref[...], g_ref[...], dh)
        dg_ref[...] += dg
        dxo = dxi_ref[...] + dxn
        dxo_ref[...] = dxo
        if below:
            dmv, dgb = _rms_bwd(m_ref[...].astype(F32), gb_ref[...], dxo)
            refs[-2][...] = dmv.astype(BF16)
            refs[-1][...] += dgb

    row = pl.BlockSpec((T, n), lambda i: (i, 0))
    in_specs = [pl.BlockSpec((T, k), lambda i: (i, 0)), _full(w.shape), row, _full((1, n)), row]
    out_specs = [row, _full((1, n))]
    out_shape = [jax.ShapeDtypeStruct((S, n), F32), jax.ShapeDtypeStruct((1, n), F32)]
    if below:
        in_specs += [row, _full((1, n))]
        out_specs += [row, _full((1, n))]
        out_shape += [jax.ShapeDtypeStruct((S, n), BF16), jax.ShapeDtypeStruct((1, n), F32)]
    return pl.pallas_call(body, name=name, grid=(S // T,), in_specs=in_specs, out_specs=out_specs, out_shape=out_shape,
                          compiler_params=_params("arbitrary"))(dy, w, x, g, dx_in, *(below or ()))


def _wgrad(a, b, *, name):
    S, k = a.shape
    n = b.shape[1]
    tk = 512 if k % 512 == 0 and k > 1536 else 256

    def body(a_ref, b_ref, o_ref):
        o_ref[...] = _dot(a_ref[...], b_ref[...], TN)

    return pl.pallas_call(body, name=name, grid=(k // tk,), in_specs=[pl.BlockSpec((S, tk), lambda kk: (0, kk)), _full((S, n))],
                          out_specs=pl.BlockSpec((tk, n), lambda kk: (kk, 0)),
                          out_shape=jax.ShapeDtypeStruct((k, n), F32), compiler_params=_params("parallel"))(a, b)


def _ffn_out_bwd(dm, w, gu, *, name, tile=256):
    S, n = dm.shape
    f = w.shape[0]
    T = min(tile, S)

    def body(dm_ref, w_ref, gu_ref, dgu_ref, da_ref):
        da_ref[...] = _dot(dm_ref[...], w_ref[...], NT)
        step = 128 * max(1, (f // 128) // 4)
        for c0 in range(0, f, step):
            c1 = min(f, c0 + step)
            gate = gu_ref[:, c0:c1].astype(F32)
            up = gu_ref[:, f + c0:f + c1].astype(F32)
            da = da_ref[:, c0:c1]
            sl, sg = _silu_parts(gate)
            dgu_ref[:, c0:c1] = (da * up * (sg + sl * (1.0 - sg))).astype(BF16)
            dgu_ref[:, f + c0:f + c1] = (da * sl).astype(BF16)

    row = pl.BlockSpec((T, n), lambda i: (i, 0))
    wide = pl.BlockSpec((T, 2 * f), lambda i: (i, 0))
    return pl.pallas_call(
        body, name=name, grid=(S // T,), in_specs=[row, _full((f, n)), wide], out_specs=wide,
        out_shape=jax.ShapeDtypeStruct((S, 2 * f), BF16),
        scratch_shapes=[pltpu.VMEM((T, f), F32)], compiler_params=_params("parallel"))(dm, w, gu)


def _loss_grad(y, target, m, g, *, name, tile=TOKEN_TILE):
    S, d = y.shape
    T = min(tile, S)
    steps = S // T

    def body(y_ref, t_ref, m_ref, g_ref, loss_ref, dy_ref, dm_ref, dg_ref, acc_ref):
        i = pl.program_id(0)

        @pl.when(i == 0)
        def _():
            acc_ref[...] = jnp.zeros_like(acc_ref)
            dg_ref[...] = jnp.zeros_like(dg_ref)

        diff = y_ref[...] - t_ref[...]
        dy = diff * (1.0 / d)
        dy_ref[...] = dy
        dmv, dg = _rms_bwd(m_ref[...].astype(F32), g_ref[...], dy)
        dm_ref[...] = dmv.astype(BF16)
        dg_ref[...] += dg
        acc_ref[...] += jnp.sum(diff * diff, axis=0, keepdims=True)

        @pl.when(i == steps - 1)
        def _():
            total = jnp.sum(acc_ref[...], axis=1, keepdims=True) * (0.5 / d)
            loss_ref[...] = jnp.broadcast_to(total, loss_ref.shape)

    row = pl.BlockSpec((T, d), lambda i: (i, 0))
    return pl.pallas_call(
        body, name=name, grid=(steps,), in_specs=[row, row, row, _full((1, d))], out_specs=[_full((1, 128)), row, row, _full((1, d))],
        out_shape=[jax.ShapeDtypeStruct((1, 128), F32), jax.ShapeDtypeStruct((S, d), F32), jax.ShapeDtypeStruct((S, d), BF16),
                   jax.ShapeDtypeStruct((1, d), F32)],
        scratch_shapes=[pltpu.VMEM((1, d), F32)], compiler_params=_params("arbitrary"))(y, target, m, g)


def _norm_dgain(x, dys, *, name):
    def body(x_ref, dy_ref, o_ref):
        xv = x_ref[...]
        dy = dy_ref[0]
        for l in range(1, dys.shape[0]):
            dy = dy + dy_ref[l]
        r = lax.rsqrt(jnp.mean(xv * xv, axis=-1, keepdims=True) + EPS)
        o_ref[...] = jnp.sum(dy * xv * r, axis=0, keepdims=True)

    return pl.pallas_call(body, name=name, out_shape=jax.ShapeDtypeStruct((1, x.shape[1]), F32),
                          compiler_params=pltpu.CompilerParams(vmem_limit_bytes=VMEM_LIMIT_BYTES))(x, dys)


def _xattn_probs(q_ref, kv_ref, h, d, hd):
    q = q_ref[:, h * hd:(h + 1) * hd]
    kh = kv_ref[:, h * hd:(h + 1) * hd]
    vh = kv_ref[:, d + h * hd:d + (h + 1) * hd]
    s = _dot(q, kh, NT) * (hd ** -0.5)
    e = jnp.exp(s - jnp.max(s, axis=-1, keepdims=True))
    return q, kh, vh, e / jnp.sum(e, axis=-1, keepdims=True)


def _xattn_fwd(q, kv, *, name, tile=TOKEN_TILE):
    S, d = q.shape
    mlen = kv.shape[0]
    hd = d // X_HEADS
    T = min(tile, S)

    def body(q_ref, kv_ref, o_ref):
        for h in range(X_HEADS):
            _, _, vh, p = _xattn_probs(q_ref, kv_ref, h, d, hd)
            o_ref[:, h * hd:(h + 1) * hd] = _dot(p, vh, NN).astype(BF16)

    row = pl.BlockSpec((T, d), lambda i: (i, 0))
    return pl.pallas_call(body, name=name, grid=(S // T,), in_specs=[row, _full((mlen, 2 * d))], out_specs=row,
                          out_shape=jax.ShapeDtypeStruct((S, d), BF16), compiler_params=_params("parallel"))(q, kv)


def _xattn_bwd(q, kv, do, *, name, tile=TOKEN_TILE):
    S, d = q.shape
    mlen = kv.shape[0]
    hd = d // X_HEADS
    T = min(tile, S)

    def body(q_ref, kv_ref, do_ref, dq_ref, dkv_ref):
        @pl.when(pl.program_id(0) == 0)
        def _():
            dkv_ref[...] = jnp.zeros_like(dkv_ref)

        for h in range(X_HEADS):
            cols = slice(h * hd, (h + 1) * hd)
            qh, kh, vh, doh = q_ref[:, cols], kv_ref[:, cols], kv_ref[:, d + h * hd:d + (h + 1) * hd], do_ref[:, cols]
            st = _dot(kh, qh, NT) * (hd ** -0.5)
            e = jnp.exp(st - jnp.max(st, axis=0, keepdims=True))
            pt = e / jnp.sum(e, axis=0, keepdims=True)
            dpt = _dot(vh, doh, NT)
            dst = (pt * (dpt - jnp.sum(pt * dpt, axis=0, keepdims=True)) * (hd ** -0.5)).astype(BF16)
            dkv_ref[:, cols] += _dot(dst, qh, NN)
            dkv_ref[:, d + h * hd:d + (h + 1) * hd] += _dot(pt, doh, NN)
            dq_ref[:, cols] = _dot(dst, kh, TN).astype(BF16)

    row = pl.BlockSpec((T, d), lambda i: (i, 0))
    return pl.pallas_call(
        body, name=name, grid=(S // T,), in_specs=[row, _full((mlen, 2 * d)), row],
        out_specs=[row, _full((mlen, 2 * d))],
        out_shape=[jax.ShapeDtypeStruct((S, d), BF16), jax.ShapeDtypeStruct((mlen, 2 * d), F32)],
        compiler_params=_params("arbitrary"))(q, kv, do)


GM0, POOL0, Q0, KV0, Z_END = 0, 512, 768, 1280, 1536


def _rope_tables(pos_col, inv_row, *, name):
    S = pos_col.shape[0]

    def body(p_ref, inv_ref, cos_ref, sin_ref):
        ang = p_ref[...].astype(F32) * inv_ref[...]
        sin = jnp.sin(ang)
        cos_ref[...] = jnp.cos(ang)
        sin_ref[...] = jnp.where(_lane(ang.shape) % HEAD_DIM < HEAD_DIM // 2, -sin, sin)

    T = min(TOKEN_TILE, S)
    return pl.pallas_call(
        body, name=name, grid=(S // T,),
        in_specs=[pl.BlockSpec((T, 1), lambda i: (i, 0)), _full((1, 128))],
        out_specs=[pl.BlockSpec((T, 128), lambda i: (i, 0))] * 2,
        out_shape=[jax.ShapeDtypeStruct((S, 128), F32)] * 2, compiler_params=_params("parallel"))(pos_col, inv_row)


def _swap_halves(x):
    n = x.shape[-1]
    return jnp.where(_lane(x.shape) % HEAD_DIM < HEAD_DIM // 2, pltpu.roll(x, n - HEAD_DIM // 2, 1), pltpu.roll(x, HEAD_DIM // 2, 1))


def _rope(x, cos, sin_s):
    reps = x.shape[-1] // 128
    if reps > 1:
        cos, sin_s = jnp.tile(cos, (1, reps)), jnp.tile(sin_s, (1, reps))
    return x * cos + _swap_halves(x) * sin_s


def _rope_bwd(dy, cos, sin_s):
    reps = dy.shape[-1] // 128
    if reps > 1:
        cos, sin_s = jnp.tile(cos, (1, reps)), jnp.tile(sin_s, (1, reps))
    return dy * cos + _swap_halves(dy * sin_s)


def _gelu_parts(x):
    c = math.sqrt(2.0 / math.pi)
    t = jnp.tanh(c * (x + 0.044715 * x * x * x))
    return 0.5 * x * (1.0 + t), t


def _gelu_grad(x, t):
    c = math.sqrt(2.0 / math.pi)
    return 0.5 * (1.0 + t) + 0.5 * x * (1.0 - t * t) * c * (1.0 + 3.0 * 0.044715 * x * x)


def _group_sum(x):
    gid = _lane(x.shape) // HEAD_DIM
    out = jnp.zeros_like(x)
    for g in range(x.shape[-1] // HEAD_DIM):
        sel = gid == g
        out = jnp.where(sel, jnp.sum(jnp.where(sel, x, 0.0), axis=-1, keepdims=True), out)
    return out


def _low_half(shape):
    return _lane(shape) % 128 < HEAD_DIM


def _tril(w):
    r = lax.broadcasted_iota(jnp.int32, w.shape, 0)
    c = lax.broadcasted_iota(jnp.int32, w.shape, 1)
    return jnp.where(r >= c, w, 0.0)


def _triu(w):
    r = lax.broadcasted_iota(jnp.int32, w.shape, 0)
    c = lax.broadcasted_iota(jnp.int32, w.shape, 1)
    return jnp.where(r <= c, w, 0.0)


def _gating_fwd(zg, vg, wc_ref, bfull):
    ge, th = _gelu_parts(zg)
    u, v = ge[:, :256], ge[:, 256:]
    r = lax.rsqrt(_group_sum(v * v) * (1.0 / HEAD_DIM) + EPS)
    vhat = v * r
    vn = vhat * vg
    low = _low_half((BLOCK, 128))
    blocks = []
    for blk in range(2):
        vb = vn[:, 128 * blk:128 * (blk + 1)]
        m0 = _dot(_tril(wc_ref[2 * blk]), vb, NN)
        m1 = _dot(_tril(wc_ref[2 * blk + 1]), vb, NN)
        blocks.append(jnp.where(low, m0, m1))
    mixed = jnp.concatenate(blocks, axis=1) + bfull
    return u * mixed, (th, u, r, vhat, vn, mixed)


def _band(w, transpose):
    if transpose:
        s2 = lax.broadcasted_iota(jnp.int32, (2 * BLOCK, BLOCK), 0)
        t = lax.broadcasted_iota(jnp.int32, (2 * BLOCK, BLOCK), 1)
    else:
        t = lax.broadcasted_iota(jnp.int32, (BLOCK, 2 * BLOCK), 0)
        s2 = lax.broadcasted_iota(jnp.int32, (BLOCK, 2 * BLOCK), 1)
    d = t - s2 + BLOCK
    return jnp.where((d >= 0) & (d < w), 1.0, 0.0).astype(BF16)


def _inv_count(n):
    t = lax.broadcasted_iota(jnp.int32, (BLOCK, 256), 0) + n * BLOCK
    gid = _lane((BLOCK, 256)) // HEAD_DIM
    win = jnp.where(gid == 0, POOL_WINDOWS[0], jnp.where(gid == 1, POOL_WINDOWS[1], jnp.where(gid == 2, POOL_WINDOWS[2], POOL_WINDOWS[3])))
    return 1.0 / jnp.minimum(t + 1, win).astype(F32)


def _pool_fwd(p, pp, n, bd, scale):
    low = _low_half((BLOCK, 128))
    blocks = []
    for blk in range(2):
        both = jnp.concatenate([pp[:, 128 * blk:128 * (blk + 1)], p[:, 128 * blk:128 * (blk + 1)]], axis=0)
        s0 = _dot(_band(POOL_WINDOWS[2 * blk], False), both, NN)
        s1 = _dot(_band(POOL_WINDOWS[2 * blk + 1], False), both, NN)
        blocks.append(jnp.where(low, s0, s1))
    inv = _inv_count(n)
    pooled = jnp.concatenate(blocks, axis=1) * inv - p
    mapped = _dot(pooled, bd, NN)
    return mapped * scale, (inv, pooled, mapped)


def _mixer_fwd(z, cos_t, sin_t, vg, wc, bfull, bd, scale, sinks, *, name):
    S = z.shape[0]
    nb = S // BLOCK

    def body(zc_ref, zpp_ref, zpk_ref, cc_ref, cp_ref, sc_ref, sp_ref, vg_ref, wc_ref, bf_ref, bd_ref, scale_ref, sinks_ref, out_ref, pen_ref):
        n = pl.program_id(0)

        @pl.when(n == 0)
        def _():
            tall = (ATT_Q_HEADS // 2 * BLOCK, 2 * BLOCK)
            qi = lax.broadcasted_iota(jnp.int32, tall, 0) % BLOCK
            ki = lax.broadcasted_iota(jnp.int32, tall, 1)
            inside = (ki > qi) & (ki <= qi + BLOCK)
            pen_ref[0] = jnp.where(inside, 0.0, NEG)
            pen_ref[1] = jnp.where(inside & (ki >= BLOCK), 0.0, NEG)

        keep = (n > 0).astype(F32)
        a, _ = _gating_fwd(zc_ref[:, GM0:POOL0].astype(F32), vg_ref[...], wc_ref, bf_ref[...])
        out_ref[:, 0:256] = a.astype(BF16)
        b, _ = _pool_fwd(zc_ref[:, POOL0:Q0].astype(F32), zpp_ref[...].astype(F32) * keep, n, bd_ref[...], scale_ref[...])
        out_ref[:, 256:512] = b.astype(BF16)
        cos_c, sin_c = cc_ref[...], sc_ref[...]
        qrot = _rope(zc_ref[:, Q0:KV0].astype(F32), cos_c, sin_c)
        kk = jnp.concatenate([_rope(zpk_ref[:, 0:128].astype(F32), cp_ref[...], sp_ref[...]), _rope(zc_ref[:, KV0:KV0 + 128].astype(F32), cos_c, sin_c)], axis=0)
        vv = jnp.concatenate([zpk_ref[:, 128:256].astype(F32), zc_ref[:, KV0 + 128:Z_END].astype(F32)], axis=0)
        kk, vv = kk.astype(BF16), vv.astype(BF16)
        pen = pen_ref[jnp.where(n > 0, 0, 1)]
        low = _low_half((BLOCK, 128))
        q_slabs = [qrot[:, 128 * s:128 * (s + 1)] for s in range(ATT_Q_HEADS // 2)]
        q_rolled = [pltpu.roll(v, HEAD_DIM, 1) for v in q_slabs]
        o_heads = [None] * ATT_Q_HEADS
        for h in range(2):
            lanes = ~low if h else low
            heads = range(4 * h, 4 * h + 4)
            qm = jnp.concatenate([jnp.where(lanes, (q_slabs if j % 2 == h else q_rolled)[j // 2], 0.0) for j in heads], axis=0)
            sc = _dot(qm, kk, NT) * (HEAD_DIM ** -0.5) + pen
            p = []
            for i, j in enumerate(heads):
                sch, sink = sc[BLOCK * i:BLOCK * (i + 1)], sinks_ref[j]
                mx = jnp.maximum(jnp.max(sch, axis=-1, keepdims=True), sink)
                e = jnp.exp(sch - mx)
                p.append((e * (1.0 / (jnp.sum(e, axis=-1, keepdims=True) + jnp.exp(sink - mx)))).astype(BF16))
            o = _dot(jnp.concatenate(p, axis=0), vv, NN)
            for i, j in enumerate(heads):
                oh = o[BLOCK * i:BLOCK * (i + 1)]
                o_heads[j] = oh if j % 2 == h else pltpu.roll(oh, HEAD_DIM, 1)
        for s in range(ATT_Q_HEADS // 2):
            out_ref[:, 512 + 128 * s:512 + 128 * (s + 1)] = jnp.where(low, o_heads[2 * s], o_heads[2 * s + 1]).astype(BF16)

    cur = lambda n: (n, 0)
    prev = lambda n: (jnp.maximum(n - 1, 0), 0)
    in_specs = [
        pl.BlockSpec((BLOCK, Z_END), cur),
        pl.BlockSpec((BLOCK, 256), lambda n: (jnp.maximum(n - 1, 0), POOL0 // 256)),
        pl.BlockSpec((BLOCK, 256), lambda n: (jnp.maximum(n - 1, 0), KV0 // 256)),
        pl.BlockSpec((BLOCK, 128), cur), pl.BlockSpec((BLOCK, 128), prev),
        pl.BlockSpec((BLOCK, 128), cur), pl.BlockSpec((BLOCK, 128), prev),
        _full((1, 256)), _full((4, BLOCK, BLOCK)), _full((BLOCK, 256)), _full((256, 256)), _full((1, 256)),
        pl.BlockSpec(memory_space=pltpu.SMEM),
    ]
    return pl.pallas_call(
        body, name=name, grid=(nb,), in_specs=in_specs, out_specs=pl.BlockSpec((BLOCK, 1024), cur),
        out_shape=jax.ShapeDtypeStruct((S, 1024), BF16), scratch_shapes=[pltpu.VMEM((2, ATT_Q_HEADS // 2 * BLOCK, 2 * BLOCK), F32)],
        compiler_params=_params("arbitrary"))(z, z, z, cos_t, cos_t, sin_t, sin_t, vg, wc, bfull, bd, scale, sinks)


def _mixer_bwd(z, dcat, cos_t, sin_t, vg, wc, bfull, bd, scale, sinks, *, name):
    S = z.shape[0]
    nb = S // BLOCK

    def body(zc_ref, zpp_ref, zpk_ref, dc_ref, cc_ref, cp_ref, sc_ref, sp_ref, vg_ref, wc_ref, bf_ref, bd_ref, scale_ref, sinks_ref, wct_ref,
             dz_ref, dvg_ref, dwc_ref, dbf_ref, dbd_ref, dscale_ref, dsink_ref, carry_ref, pen_ref):
        n = pl.program_id(0)

        @pl.when(n == 0)
        def _():
            for ref in (dvg_ref, dwc_ref, dbf_ref, dbd_ref, dscale_ref, dsink_ref):
                ref[...] = jnp.zeros_like(ref)
            wide = (2 * BLOCK, ATT_Q_HEADS // 2 * BLOCK)
            ki = lax.broadcasted_iota(jnp.int32, wide, 0)
            qi = lax.broadcasted_iota(jnp.int32, wide, 1) % BLOCK
            inside = (ki > qi) & (ki <= qi + BLOCK)
            pen_ref[0] = jnp.where(inside, 0.0, NEG)
            pen_ref[1] = jnp.where(inside & (ki >= BLOCK), 0.0, NEG)

        @pl.when(n < nb)
        def _():
            keep = (n > 0).astype(F32)
            low = _low_half((BLOCK, 128))
            zg = zc_ref[:, GM0:POOL0].astype(F32)
            vg_v = vg_ref[...]
            _, (th, u, r, vhat, vn, mixed) = _gating_fwd(zg, vg_v, wc_ref, bf_ref[...])
            da = dc_ref[:, 0:256].astype(F32)
            dmixed = da * u
            dbf_ref[...] += dmixed
            dvn = []
            for blk in range(2):
                dmb = dmixed[:, 128 * blk:128 * (blk + 1)]
                vb = vn[:, 128 * blk:128 * (blk + 1)]
                acc = None
                for half in range(2):
                    g = 2 * blk + half
                    dmg = jnp.where(low if half == 0 else ~low, dmb, 0.0)
                    dwc_ref[g] += _tril(_dot(dmg, vb, NT))
                    part = _dot(_triu(wct_ref[g]), dmg, NN)
                    acc = part if acc is None else acc + part
                dvn.append(acc)
            dvn = jnp.concatenate(dvn, axis=1)
            dvg_ref[...] += jnp.sum(dvn * vhat, axis=0, keepdims=True)
            dvhat = dvn * vg_v
            dv = r * (dvhat - vhat * (_group_sum(dvhat * vhat) * (1.0 / HEAD_DIM)))
            dge = jnp.concatenate([da * mixed, dv], axis=1)
            own_g = dge * _gelu_grad(zg, th)
            p = zc_ref[:, POOL0:Q0].astype(F32)
            bd_v, scale_v = bd_ref[...], scale_ref[...]
            _, (inv, pooled, mapped) = _pool_fwd(p, zpp_ref[...].astype(F32) * keep, n, bd_v, scale_v)
            db = dc_ref[:, 256:512].astype(F32)
            dscale_ref[...] += jnp.sum(db * mapped, axis=0, keepdims=True)
            dmapped = db * scale_v
            dbd_ref[...] += _dot(pooled, dmapped, TN)
            dpooled = _dot(dmapped, bd_v, NT)
            dps = dpooled * inv
            back = []
            for blk in range(2):
                dpb = dps[:, 128 * blk:128 * (blk + 1)]
                t0 = _dot(_band(POOL_WINDOWS[2 * blk], True), dpb, NN)
                t1 = _dot(_band(POOL_WINDOWS[2 * blk + 1], True), dpb, NN)
                back.append(jnp.where(jnp.concatenate([low, low], axis=0), t0, t1))
            back = jnp.concatenate(back, axis=1)
            halo_p = back[:BLOCK] * keep
            own_p = back[BLOCK:] - dpooled
            cos_c, sin_c, cos_p, sin_p = cc_ref[...], sc_ref[...], cp_ref[...], sp_ref[...]
            qrot = _rope(zc_ref[:, Q0:KV0].astype(F32), cos_c, sin_c)
            kk = jnp.concatenate([_rope(zpk_ref[:, 0:128].astype(F32), cos_p, sin_p), _rope(zc_ref[:, KV0:KV0 + 128].astype(F32), cos_c, sin_c)], axis=0)
            vv = jnp.concatenate([zpk_ref[:, 128:256].astype(F32), zc_ref[:, KV0 + 128:Z_END].astype(F32)], axis=0)
            kt = jnp.transpose(kk).astype(BF16)
            kk, vv = kk.astype(BF16), vv.astype(BF16)
            pen = pen_ref[jnp.where(n > 0, 0, 1)]
            q_slabs = [qrot[:, 128 * s:128 * (s + 1)] for s in range(ATT_Q_HEADS // 2)]
            d_slabs = [dc_ref[:, 512 + 128 * s:512 + 128 * (s + 1)].astype(F32) for s in range(ATT_Q_HEADS // 2)]
            q_rolled = [pltpu.roll(v, HEAD_DIM, 1) for v in q_slabs]
            d_rolled = [pltpu.roll(v, HEAD_DIM, 1) for v in d_slabs]
            dkk = jnp.zeros((2 * BLOCK, 128), F32)
            dvv = jnp.zeros((2 * BLOCK, 128), F32)
            dq_heads = [None] * ATT_Q_HEADS
            for h in range(2):
                lanes = ~low if h else low
                heads = range(4 * h, 4 * h + 4)
                qm = jnp.concatenate([jnp.where(lanes, (q_slabs if j % 2 == h else q_rolled)[j // 2], 0.0) for j in heads], axis=0).astype(BF16)
                dom = jnp.concatenate([jnp.where(lanes, (d_slabs if j % 2 == h else d_rolled)[j // 2], 0.0) for j in heads], axis=0).astype(BF16)
                sink = jnp.concatenate([jnp.full((1, BLOCK), sinks_ref[j], F32) for j in heads], axis=1)
                sc = _dot(kk, qm, NT) * (HEAD_DIM ** -0.5) + pen
                mx = jnp.maximum(jnp.max(sc, axis=0, keepdims=True), sink)
                e = jnp.exp(sc - mx)
                es = jnp.exp(sink - mx)
                inv_den = 1.0 / (jnp.sum(e, axis=0, keepdims=True) + es)
                pt = e * inv_den
                dpt = _dot(vv, dom, NT)
                dvv = dvv + _dot(pt, dom, NN)
                dcol = jnp.sum(pt * dpt, axis=0, keepdims=True)
                dst = (pt * (dpt - dcol) * (HEAD_DIM ** -0.5)).astype(BF16)
                dsink = es * inv_den * dcol
                dkk = dkk + _dot(dst, qm, NN)
                dqt = _dot(kt, dst, NN)
                for i, j in enumerate(heads):
                    part = jnp.sum(dsink[:, BLOCK * i:BLOCK * (i + 1)], axis=1, keepdims=True)
                    dsink_ref[pl.ds(j, 1), :] = dsink_ref[pl.ds(j, 1), :] - jnp.broadcast_to(part, (1, 128))
                    dqh = jnp.transpose(dqt[:, BLOCK * i:BLOCK * (i + 1)])
                    dq_heads[j] = dqh if j % 2 == h else pltpu.roll(dqh, HEAD_DIM, 1)
            dqrot = jnp.concatenate([jnp.where(low, dq_heads[2 * s], dq_heads[2 * s + 1]) for s in range(ATT_Q_HEADS // 2)], axis=1)
            own_q = _rope_bwd(dqrot, cos_c, sin_c)
            own_k = _rope_bwd(dkk[BLOCK:], cos_c, sin_c)
            halo_k = _rope_bwd(dkk[:BLOCK], cos_p, sin_p) * keep
            own_v, halo_v = dvv[BLOCK:], dvv[:BLOCK] * keep

            @pl.when(n > 0)
            def _():
                dz_ref[:, GM0:POOL0] = carry_ref[:, GM0:POOL0].astype(BF16)
                dz_ref[:, POOL0:Q0] = (carry_ref[:, POOL0:Q0] + halo_p).astype(BF16)
                dz_ref[:, Q0:KV0] = carry_ref[:, Q0:KV0].astype(BF16)
                dz_ref[:, KV0:KV0 + 128] = (carry_ref[:, KV0:KV0 + 128] + halo_k).astype(BF16)
                dz_ref[:, KV0 + 128:Z_END] = (carry_ref[:, KV0 + 128:Z_END] + halo_v).astype(BF16)

            carry_ref[:, GM0:POOL0] = own_g
            carry_ref[:, POOL0:Q0] = own_p
            carry_ref[:, Q0:KV0] = own_q
            carry_ref[:, KV0:KV0 + 128] = own_k
            carry_ref[:, KV0 + 128:Z_END] = own_v

        @pl.when(n == nb)
        def _():
            dz_ref[...] = carry_ref[...].astype(BF16)

    cur = lambda n: (jnp.minimum(n, nb - 1), 0)
    prev = lambda n: (jnp.maximum(jnp.minimum(n, nb - 1) - 1, 0), 0)
    in_specs = [
        pl.BlockSpec((BLOCK, Z_END), cur),
        pl.BlockSpec((BLOCK, 256), lambda n: (jnp.maximum(jnp.minimum(n, nb - 1) - 1, 0), POOL0 // 256)),
        pl.BlockSpec((BLOCK, 256), lambda n: (jnp.maximum(jnp.minimum(n, nb - 1) - 1, 0), KV0 // 256)),
        pl.BlockSpec((BLOCK, 1024), cur),
        pl.BlockSpec((BLOCK, 128), cur), pl.BlockSpec((BLOCK, 128), prev),
        pl.BlockSpec((BLOCK, 128), cur), pl.BlockSpec((BLOCK, 128), prev),
        _full((1, 256)), _full((4, BLOCK, BLOCK)), _full((BLOCK, 256)), _full((256, 256)), _full((1, 256)),
        pl.BlockSpec(memory_space=pltpu.SMEM), _full((4, BLOCK, BLOCK)),
    ]
    out_specs = [pl.BlockSpec((BLOCK, Z_END), lambda n: (jnp.maximum(n - 1, 0), 0)),
                 _full((1, 256)), _full((4, BLOCK, BLOCK)), _full((BLOCK, 256)), _full((256, 256)), _full((1, 256)), _full((8, 128))]
    out_shape = [jax.ShapeDtypeStruct((S, Z_END), BF16), jax.ShapeDtypeStruct((1, 256), F32),
                 jax.ShapeDtypeStruct((4, BLOCK, BLOCK), F32), jax.ShapeDtypeStruct((BLOCK, 256), F32),
                 jax.ShapeDtypeStruct((256, 256), F32), jax.ShapeDtypeStruct((1, 256), F32), jax.ShapeDtypeStruct((8, 128), F32)]
    return pl.pallas_call(
        body, name=name, grid=(nb + 1,), in_specs=in_specs, out_specs=out_specs, out_shape=out_shape,
        scratch_shapes=[pltpu.VMEM((BLOCK, Z_END), F32), pltpu.VMEM((2, 2 * BLOCK, ATT_Q_HEADS // 2 * BLOCK), F32)],
        compiler_params=_params("arbitrary"))(z, z, z, dcat, cos_t, cos_t, sin_t, sin_t, vg, wc, bfull, bd, scale, sinks, jnp.swapaxes(wc, 1, 2))


HBM = pl.BlockSpec(memory_space=pl.ANY)


def _place():
    return lax.axis_index("x"), lax.axis_index("y"), lax.axis_index("c")


def _chip(k, x, y):
    return (1 - x if k & 1 else x, 1 - y if k & 2 else y)


def _block_rows(ref, rows, dev):
    start = pl.multiple_of((4 * dev[0] + 2 * dev[1] + dev[2]) * rows, 8)
    return ref.at[pl.ds(start, rows), :]


_HBM_SPEC = pl.BlockSpec(memory_space=pltpu.HBM)
_SEM_SPEC = pl.BlockSpec(memory_space=pltpu.SEMAPHORE)
_SPLIT_PARAMS = dict(compiler_params=pltpu.CompilerParams(has_side_effects=pltpu.SideEffectType.DATAFLOW_SIDE_EFFECTING))


def _descriptors(copies, refs, send_sems, recv_sems):
    return [pltpu.make_async_remote_copy(src_ref=s, dst_ref=d, send_sem=send_sems.at[i], recv_sem=recv_sems.at[i], device_id=to, device_id_type=MESH)
            for i, (s, d, to) in enumerate(copies(refs))]


def _start(copies, arrays, fresh, n_copies, *, name):
    operands = [pltpu.with_memory_space_constraint(v, pltpu.HBM) for v in (*arrays, *[lax.empty(f.shape, f.dtype) for f in fresh])]
    n = len(operands)

    def body(*refs):
        for dma in _descriptors(copies, refs[:n], refs[n], refs[n + 1]):
            dma.start()
        refs[-1][...] = jnp.zeros_like(refs[-1])

    res = pl.pallas_call(
        body, name=name, in_specs=[_HBM_SPEC] * n,
        out_shape=(pltpu.SemaphoreType.DMA((n_copies,)), pltpu.SemaphoreType.DMA((n_copies,)),
                   *[pltpu.HBM(v.shape, v.dtype) for v in operands], jax.ShapeDtypeStruct((8, 128), F32)),
        out_specs=(_SEM_SPEC, _SEM_SPEC, *[_HBM_SPEC] * n, pl.BlockSpec(memory_space=pltpu.VMEM)),
        input_output_aliases={i: 2 + i for i in range(n)}, **_SPLIT_PARAMS)(*operands)
    return res[0], res[1], list(res[2:2 + n]), res[-1]


def _finish(copies, send_sems, recv_sems, arrays, after, *, name):
    n = len(arrays)

    def body(*refs):
        for dma in _descriptors(copies, refs[:n], refs[n], refs[n + 1]):
            dma.wait_send()
            dma.wait_recv()

    return list(pl.pallas_call(
        body, name=name, in_specs=[_HBM_SPEC] * n + [_SEM_SPEC, _SEM_SPEC] + [HBM] * len(after),
        out_shape=tuple(pltpu.HBM(v.shape, v.dtype) for v in arrays), out_specs=tuple([_HBM_SPEC] * n),
        input_output_aliases={i: i for i in range(n)}, **_SPLIT_PARAMS)(*arrays, send_sems, recv_sems, *after))


def _tie(value, tokens):
    if not tokens:
        return value

    def body(*refs):
        pass

    return pl.pallas_call(body, name="tie", in_specs=[HBM] * (1 + len(tokens)), out_specs=HBM,
                          out_shape=jax.ShapeDtypeStruct(value.shape, value.dtype), input_output_aliases={0: 0})(value, *tokens)


def _gather_copies(stage):
    def copies(refs):
        x, y, c = _place()
        me, sibling = (x, y, c), (x, y, 1 - c)
        out = []
        for ref in refs:
            r = ref.shape[0] // N_DEV
            if stage == "a":
                hops = [(me, sibling)] + [(me, (*_chip(k, x, y), c)) for k in (1, 2, 3)]
            else:
                hops = [((*_chip(k, x, y), c), sibling) for k in (1, 2, 3)]
            for block, to in hops:
                rows = _block_rows(ref, r, block)
                out.append((rows, rows, to))
        return out
    return copies


def _scatter_copies(stage, n):
    def copies(refs):
        x, y, c = _place()
        out = []
        for w in range(n):
            src, dst = refs[w], refs[n + w]
            if stage == "a":
                r = src.shape[0] // N_DEV
                out += [(_block_rows(src, r, (*_chip(k, x, y), 1 - c)), dst.at[k], (x, y, 1 - c)) for k in range(4)]
            else:
                out += [(src.at[k - 1], dst.at[k - 1], (*_chip(k, x, y), c)) for k in (1, 2, 3)]
        return out
    return copies


def _place_own(shards, layer, blocks, *, name, dtype=BF16):
    _, r, cdim = shards.shape

    def body(blocks_ref, s_ref, o_ref):
        o_ref[...] = s_ref[...].astype(dtype)

    return pl.pallas_call(
        body, name=name,
        grid_spec=pltpu.PrefetchScalarGridSpec(
            num_scalar_prefetch=1, grid=(1,), in_specs=[pl.BlockSpec((None, r, cdim), lambda i, blocks: (layer, 0, 0))],
            out_specs=pl.BlockSpec((r, cdim), lambda i, blocks: (blocks[0], 0))),
        out_shape=jax.ShapeDtypeStruct((N_DEV * r, cdim), dtype), compiler_params=_params("arbitrary"))(blocks, shards)


def _chip_sum(grad, got, blocks, *, name):
    _, r, cdim = got.shape
    tr = r if r <= 256 else r // (2 if r % 32 == 0 and r // 2 <= 256 else 4)
    steps = r // tr

    def body(blocks_ref, a0_ref, a1_ref, a2_ref, a3_ref, b_ref, mine_ref, away_ref):
        mine_ref[...] = a0_ref[...] + b_ref[0]
        for k, a_ref in ((1, a1_ref), (2, a2_ref), (3, a3_ref)):
            away_ref[k - 1] = (a_ref[...] + b_ref[k]).astype(BF16)

    own = [pl.BlockSpec((tr, cdim), lambda i, blocks, k=k: (blocks[k] * steps + i, 0)) for k in range(4)]
    return pl.pallas_call(
        body, name=name,
        grid_spec=pltpu.PrefetchScalarGridSpec(
            num_scalar_prefetch=1, grid=(steps,),
            in_specs=own + [pl.BlockSpec((4, tr, cdim), lambda i, blocks: (0, i, 0))],
            out_specs=[pl.BlockSpec((tr, cdim), lambda i, blocks: (i, 0)), pl.BlockSpec((3, tr, cdim), lambda i, blocks: (0, i, 0))]),
        out_shape=[jax.ShapeDtypeStruct((r, cdim), F32), jax.ShapeDtypeStruct((3, r, cdim), BF16)],
        compiler_params=_params("arbitrary"))(blocks, grad, grad, grad, grad, got)


def _final_sum(mine, got, *, name):
    r, cdim = mine.shape
    tr = r if r <= 256 else r // (2 if r % 32 == 0 and r // 2 <= 256 else 4)

    def body(m_ref, g_ref, o_ref):
        o_ref[...] = ((m_ref[...] + g_ref[0].astype(F32)) + g_ref[1].astype(F32)) + g_ref[2].astype(F32)

    return pl.pallas_call(
        body, name=name, grid=(r // tr,),
        in_specs=[pl.BlockSpec((tr, cdim), lambda i: (i, 0)), pl.BlockSpec((3, tr, cdim), lambda i: (0, i, 0))],
        out_specs=pl.BlockSpec((tr, cdim), lambda i: (i, 0)), out_shape=jax.ShapeDtypeStruct((r, cdim), F32),
        compiler_params=_params("parallel"))(mine, got)


def _adamw_math(w, g, m, v):
    m = ADAM_B1 * m + (1.0 - ADAM_B1) * g
    v = ADAM_B2 * v + (1.0 - ADAM_B2) * (g * g)
    m_hat = m / (1.0 - ADAM_B1 ** ADAM_STEP)
    v_hat = v / (1.0 - ADAM_B2 ** ADAM_STEP)
    return -ADAM_LR * (m_hat / (jnp.sqrt(v_hat) + ADAM_EPS) + ADAM_WD * w), m, v


def _adamw(w, g, m, v, *, name):
    L, r, cdim = w.shape
    tr = r if r <= 512 else r // 2
    assert r % tr == 0 and tr % 8 == 0

    def body(w_ref, g_ref, m_ref, v_ref, d_ref, nm_ref, nv_ref):
        d_ref[...], nm_ref[...], nv_ref[...] = _adamw_math(w_ref[...], g_ref[...], m_ref[...], v_ref[...])

    blk = pl.BlockSpec((None, tr, cdim), lambda l, i: (l, i, 0))
    return pl.pallas_call(body, name=name, grid=(L, r // tr), in_specs=[blk] * 4, out_specs=[blk] * 3,
                          out_shape=[jax.ShapeDtypeStruct(w.shape, F32)] * 3, compiler_params=_params("parallel", "parallel"))(w, g, m, v)


def _small_update(parts, w, m, v, *, name):
    _, p, lanes = parts.shape
    tp = p // 2 if p % 16 == 0 else p

    def body(p_ref, w_ref, m_ref, v_ref, g_ref, d_ref, nm_ref, nv_ref):
        g = p_ref[0]
        for i in range(1, N_DEV):
            g = g + p_ref[i]
        g_ref[...] = g
        d_ref[...], nm_ref[...], nv_ref[...] = _adamw_math(w_ref[...], g, m_ref[...], v_ref[...])

    blk = pl.BlockSpec((tp, lanes), lambda i: (i, 0))
    return pl.pallas_call(body, name=name, grid=(p // tp,), in_specs=[pl.BlockSpec((N_DEV, tp, lanes), lambda i: (0, i, 0)), blk, blk, blk],
                          out_specs=[blk] * 4, out_shape=[jax.ShapeDtypeStruct((p, lanes), F32)] * 4,
                          compiler_params=_params("parallel"))(parts, w, m, v)


BIG = ("w_in", "w_o", "w_xq", "w_xkv", "w_xo", "w_gate_up", "w_down")
TRANSPOSED = ("w_in", "w_xkv", "w_gate_up")
SMALL = ("mem_norm_g", "mix_pre_g", "mix_post_g", "gm_v_g", "gm_w_s", "gm_b_s", "pool_w", "pool_scale", "attn_sinks",
         "x_pre_g", "x_post_g", "ffn_pre_g", "ffn_post_g")
WEIGHTS = ("mem_norm_g", "mix_pre_g", "mix_post_g", "w_in", "gm_v_g", "gm_w_s", "gm_b_s", "pool_w", "pool_scale", "attn_sinks", "w_o",
           "x_pre_g", "x_post_g", "w_xq", "w_xkv", "w_xo", "ffn_pre_g", "ffn_post_g", "w_gate_up", "w_down")
PACK_QUANTUM = 8 * 128


def _pack(arrays):
    flat = []
    for a in arrays:
        a = a.reshape(-1).astype(F32)
        flat.append(jnp.pad(a, (0, -a.size % PACK_QUANTUM)))
    return jnp.concatenate(flat).reshape(-1, 128)


def _unpack(pack, shapes):
    out, row = [], 0
    for s in shapes:
        size = math.prod(s)
        rows = (size + (-size % PACK_QUANTUM)) // 128
        out.append(pack[row:row + rows].reshape(-1)[:size].reshape(s))
        row += rows
    return out


def _row(v):
    return v.reshape(1, -1)


def _local_step(x, mem, positions, target, small, weight, fwd_hook=None, bwd_hook=None):
    fwd_hook = fwd_hook or (lambda l, point, v: v)
    bwd_hook = bwd_hook or (lambda l, point, v, gb: v)
    depth = small["mix_pre_g"].shape[0]
    half = HEAD_DIM // 2
    inv = ROPE_THETA ** (-jnp.arange(half, dtype=F32) / half)
    cos_t, sin_t = _rope_tables(positions.reshape(-1, 1), jnp.tile(inv, 128 // half).reshape(1, 128), name="rope_tables")
    mem_g = _row(small["mem_norm_g"])
    saved = []
    for l in range(depth):
        w = functools.partial(weight, l)
        bfull = jnp.repeat(small["gm_b_s"][l].T, HEAD_DIM, axis=1)
        bd = jax.scipy.linalg.block_diag(*[small["pool_w"][l, g] for g in range(len(POOL_WINDOWS))])
        mix_par = (_row(small["gm_v_g"][l]), small["gm_w_s"][l], bfull, bd, _row(small["pool_scale"][l]), small["attn_sinks"][l])
        if l == 0:
            z, h1 = _mm_fwd(fwd_hook(l, 0, x), w("w_in"), nt=True, pre="norm", g_pre=_row(small["mix_pre_g"][l]), name="first_in_proj")
        else:
            h1 = fwd_hook(l, 0, h1)
            z, = _mm_fwd(h1, w("w_in"), nt=True, name="in_proj")
        cat = fwd_hook(l, 1, _mixer_fwd(z, cos_t, sin_t, *mix_par, name="mixer_fwd"))
        mix, x1, h2 = _mm_fwd(cat, w("w_o"), nt=False, post="norm_res", g_post=_row(small["mix_post_g"][l]), xres=x,
                              g_next=_row(small["x_pre_g"][l]), name="mix_out")
        qx, = _mm_fwd(h2, w("w_xq"), nt=False, name="xq_proj")
        kv, memn = _mm_fwd(mem, w("w_xkv"), nt=True, pre="norm", g_pre=mem_g, name="xkv_proj")
        ox = _xattn_fwd(qx, kv, name="xattn_fwd")
        xo, x2, h3 = _mm_fwd(ox, w("w_xo"), nt=False, post="norm_res", g_post=_row(small["x_post_g"][l]), xres=x1,
                             g_next=_row(small["ffn_pre_g"][l]), name="xattn_out")
        h3 = fwd_hook(l, 2, h3)
        gu, = _mm_fwd(h3, w("w_gate_up"), nt=True, name="ffn_in")
        last = l + 1 == depth
        f, x3, act, *h_next = _mm_fwd(gu, w("w_down"), nt=False, pre="swiglu", post="norm_res", g_post=_row(small["ffn_post_g"][l]), xres=x2,
                                      g_next=None if last else _row(small["mix_pre_g"][l + 1]), tile=256, name="last_ffn_out" if last else "ffn_out")
        saved.append((mix_par, x, z, h1, cat, mix, x1, qx, h2, kv, memn, ox, xo, x2, gu, h3, act, f))
        if last:
            x = fwd_hook(l, 3, x3)
        else:
            x, h1 = x3, fwd_hook(l, 3, h_next[0])

    gs = {n: [None] * depth for n in SMALL if n != "mem_norm_g"}
    gb = {n: [None] * depth for n in BIG}
    dmemn = [None] * depth
    loss, dx, df, gs["ffn_post_g"][depth - 1] = _loss_grad(x, target, saved[-1][-1], _row(small["ffn_post_g"][depth - 1]), name="loss")
    for l in reversed(range(depth)):
        w = functools.partial(weight, l)
        mix_par, x0, z, h1, cat, mix, x1, qx, h2, kv, memn, ox, xo, x2, gu, h3, act, f = saved[l]
        dgu = _ffn_out_bwd(df, w("w_down"), gu, name="ffn_out_bwd")
        gb["w_down"][l] = _wgrad(act, df, name="w_down_grad")
        gb["w_gate_up"][l] = _wgrad(dgu, h3, name="w_gate_up_grad")
        dx2, gs["ffn_pre_g"][l], dxo, gs["x_post_g"][l] = _mm_bwd_post(
            dgu, w("w_gate_up"), x2, _row(small["ffn_pre_g"][l]), dx, nt=False, below=(xo, _row(small["x_post_g"][l])), tile=256, name="ffn_in_bwd")
        dxo = bwd_hook(l, 1, dxo, gb)
        do, = _mm_fwd(dxo, w("w_xo"), nt=True, name="xattn_out_bwd")
        gb["w_xo"][l] = _wgrad(ox, dxo, name="w_xo_grad")
        dqx, dkv = _xattn_bwd(qx, kv, do, name="xattn_bwd")
        gb["w_xq"][l] = _wgrad(h2, dqx, name="w_xq_grad")
        dx1, gs["x_pre_g"][l], dmix, gs["mix_post_g"][l] = _mm_bwd_post(
            dqx, w("w_xq"), x1, _row(small["x_pre_g"][l]), dx2, nt=True, below=(mix, _row(small["mix_post_g"][l])), name="xq_proj_bwd")
        dmix = bwd_hook(l, 2, dmix, gb)
        gb["w_xkv"][l] = _wgrad(dkv, memn, name="w_xkv_grad")
        dmemn[l] = _mm_fwd(dkv, w("w_xkv"), nt=False, out_dtype=F32, name="xkv_proj_bwd")[0]
        dcat, = _mm_fwd(dmix, w("w_o"), nt=True, name="mix_out_bwd")
        gb["w_o"][l] = _wgrad(cat, dmix, name="w_o_grad")
        dz, dvg, dwc, dbf, dbd, dscale, dsink = _mixer_bwd(z, dcat, cos_t, sin_t, *mix_par, name="mixer_bwd")
        gs["gm_v_g"][l], gs["gm_w_s"][l], gs["pool_scale"][l], gs["attn_sinks"][l] = dvg, dwc, dscale, dsink[:, 0]
        gs["gm_b_s"][l] = dbf.reshape(BLOCK, -1, HEAD_DIM).sum(-1).T
        gs["pool_w"][l] = jnp.stack([dbd[HEAD_DIM * g:HEAD_DIM * (g + 1), HEAD_DIM * g:HEAD_DIM * (g + 1)] for g in range(len(POOL_WINDOWS))])
        gb["w_in"][l] = _wgrad(dz, h1, name="w_in_grad")
        if l == 0:
            dx, gs["mix_pre_g"][l] = _mm_bwd_post(dz, w("w_in"), x0, _row(small["mix_pre_g"][l]), dx1, nt=False, name="first_in_proj_bwd")
            dx = bwd_hook(l, 3, dx, gb)
        else:
            dx, gs["mix_pre_g"][l], df, gs["ffn_post_g"][l - 1] = _mm_bwd_post(
                dz, w("w_in"), x0, _row(small["mix_pre_g"][l]), dx1, nt=False, below=(saved[l - 1][-1], _row(small["ffn_post_g"][l - 1])),
                name="in_proj_bwd")
            df = bwd_hook(l, 3, df, gb)

    small_grads = {n: jnp.stack(v).reshape(small[n].shape) for n, v in gs.items()}
    small_grads["mem_norm_g"] = _norm_dgain(mem, jnp.stack(dmemn), name="mem_norm_grad").reshape(-1)
    return loss, dx, small_grads, gb


EARLY = ("w_in", "w_o", "w_xq", "w_xkv", "w_xo")
LATE = ("w_gate_up", "w_down")


def _step(a):
    depth = a["mix_pre_g"].shape[0]
    x, mem, target = a["x"][0], a["mem"][0], a["loss_target"][0]
    px, py, pc = _place()
    blocks = jnp.stack([4 * cx + 2 * cy + pc for cx, cy in (_chip(k, px, py) for k in range(4))]).astype(jnp.int32)
    flight = {}

    def laid_out(v, n):
        return v.transpose(0, 2, 1) if n in TRANSPOSED else v

    whole = {(l, n): _place_own(laid_out(a[n], n), l, blocks, name="place_" + n) for l in range(depth) for n in BIG}
    gather = {"first_in": [(0, "w_in")], "first_mid": [(0, n) for n in EARLY[1:]], "first_late": [(0, n) for n in LATE],
              "second_early": [(1, n) for n in EARLY], "second_late": [(1, n) for n in LATE]}
    gather.update({f"layer{l}": [(l, n) for n in BIG] for l in range(2, depth)})
    forward_plan = {(0, 0): [("a", "first_in"), ("a", "first_mid"), ("a", "first_late"), ("a", "second_early"), ("a", "second_late"),
                             ("mid", "first_in"), ("end", "first_in")],
                    (0, 1): [("mid", "first_mid"), ("end", "first_mid")], (0, 2): [("mid", "first_late"), ("end", "first_late")],
                    (0, 3): [("mid", "second_early")], (1, 0): [("end", "second_early"), ("a", "layer2")],
                    (1, 1): [("mid", "second_late")], (1, 2): [("end", "second_late"), ("mid", "layer2")], (1, 3): [("end", "layer2")]}
    for l in range(2, depth):
        forward_plan.update({(l, 0): [("a", f"layer{l + 1}")], (l, 2): [("mid", f"layer{l + 1}")], (l, 3): [("end", f"layer{l + 1}")]})
    if depth == 1:
        gather = {tag: keys for tag, keys in gather.items() if tag.startswith("first")}

    issued = []

    def gather_step(kind, tag, v):
        keys = gather[tag]
        if kind == "a":
            arrays = [whole[k] for k in keys]
            arrays[0] = _tie(arrays[0], issued[-1:])
            send, recv, arrays, token = _start(_gather_copies("a"), arrays, [], 4 * len(keys), name=f"gather_a_{tag}")
            flight[tag] = (send, recv, arrays)
            issued.append(token)
            return [token]
        send, recv, arrays = flight.pop(tag)
        if kind == "mid":
            arrays = _finish(_gather_copies("a"), send, recv, arrays, [v], name=f"gather_a_done_{tag}")
            send, recv, arrays, token = _start(_gather_copies("b"), arrays, [], 3 * len(keys), name=f"gather_b_{tag}")
            flight[tag] = (send, recv, arrays)
            return [token]
        for key, arr in zip(keys, _finish(_gather_copies("b"), send, recv, arrays, [v], name=f"gather_b_done_{tag}")):
            whole[key] = arr
        return []

    def fwd_hook(l, point, v):
        tokens = []
        for kind, tag in forward_plan.get((l, point), []):
            if tag in gather:
                tokens += gather_step(kind, tag, v)
        return _tie(v, tokens)

    mine = {}
    scatter = {f"layer{l}": [(l, n) for n in BIG] for l in range(1, depth)}
    scatter.update({"late0": [(0, n) for n in LATE], "attn0": [(0, "w_xq"), (0, "w_xo")], "rest0": [(0, "w_in"), (0, "w_o"), (0, "w_xkv")]})
    backward_plan = {}
    for l in range(1, depth):
        backward_plan[(l, 3)] = [("a", f"layer{l}")]
        backward_plan[(l - 1, 1)] = [("mid", f"layer{l}")]
        backward_plan[(l - 1, 2)] = [("end", f"layer{l}")]
    backward_plan[(0, 1)] = [("a", "late0")] + backward_plan.get((0, 1), [])
    backward_plan[(0, 2)] = [("a", "attn0"), ("mid", "late0")] + backward_plan.get((0, 2), [])
    backward_plan[(0, 3)] = [("end", "late0"), ("mid", "attn0"), ("a", "rest0")]

    def scatter_step(kind, tag, v, gb):
        keys = scatter[tag]
        n = len(keys)
        if kind == "a":
            grads = [gb[name][l] for l, name in keys]
            zones = [jax.ShapeDtypeStruct((4, g.shape[0] // N_DEV, g.shape[1]), F32) for g in grads]
            send, recv, arrays, token = _start(_scatter_copies("a", n), grads, zones, 4 * n, name=f"scatter_a_{tag}")
            flight[tag] = (send, recv, arrays, None)
            return [token]
        send, recv, arrays, kept = flight.pop(tag)
        if kind == "mid":
            arrays = _finish(_scatter_copies("a", n), send, recv, arrays, [v], name=f"scatter_a_done_{tag}")
            sums = [_chip_sum(arrays[i], arrays[n + i], blocks, name="chip_sum") for i in range(n)]
            zones = [jax.ShapeDtypeStruct(s[1].shape, BF16) for s in sums]
            send, recv, arrays, token = _start(_scatter_copies("b", n), [s[1] for s in sums], zones, 3 * n, name=f"scatter_b_{tag}")
            flight[tag] = (send, recv, arrays, [s[0] for s in sums])
            return [token]
        arrays = _finish(_scatter_copies("b", n), send, recv, arrays, [v], name=f"scatter_b_done_{tag}")
        for i, key in enumerate(keys):
            mine[key] = _final_sum(kept[i], arrays[n + i], name="final_sum")
        return []

    def bwd_hook(l, point, v, gb):
        tokens = []
        for kind, tag in backward_plan.get((l, point), []):
            tokens += scatter_step(kind, tag, v, gb)
        return _tie(v, tokens)

    loss, dx, small_grads, _ = _local_step(x, mem, a["positions"][0], target, {n: a[n] for n in SMALL}, lambda l, n: whole[(l, n)],
                                           fwd_hook, bwd_hook)

    out = {}

    def update(n, tokens):
        g = _tie(jnp.stack([mine[(l, n)] for l in range(depth)]), tokens)
        moved = _adamw(laid_out(a[n], n), g, laid_out(a["m_" + n], n), laid_out(a["v_" + n], n), name="adamw_" + n)
        for p, v in zip(("grad_", "delta_", "new_m_", "new_v_"), (g, *moved)):
            out[p + n] = laid_out(v, n)
        return moved[0]

    part = _tie(_pack([small_grads[n] for n in SMALL]), [dx])
    placed = _place_own(part[None], 0, blocks, dtype=F32, name="place_small_grads")
    send, recv, arrays, token = _start(_gather_copies("a"), [placed], [], 4, name="gather_a_small_grads")
    behind = [update(n, [token]) for n in LATE]
    tokens = scatter_step("mid", "rest0", behind[-1], None)
    arrays = _finish(_gather_copies("a"), send, recv, arrays, behind, name="gather_a_done_small_grads")
    send, recv, arrays, token = _start(_gather_copies("b"), arrays, [], 3, name="gather_b_small_grads")
    scatter_step("end", "attn0", behind[-1], None)
    behind = [update(n, tokens + [token]) for n in ("w_xq", "w_xo")]
    parts = _finish(_gather_copies("b"), send, recv, arrays, behind, name="gather_b_done_small_grads")[0].reshape(N_DEV, *part.shape)
    packs = _small_update(parts, *[_pack([a[p + n] for n in SMALL]) for p in ("", "m_", "v_")], name="small_update")
    for p, pack in zip(("grad_", "delta_", "new_m_", "new_v_"), packs):
        for n, v in zip(SMALL, _unpack(pack, [a[n].shape for n in SMALL])):
            out[p + n] = v
    scatter_step("end", "rest0", packs[0], None)
    for n in ("w_in", "w_o", "w_xkv"):
        update(n, [])

    total = lax.psum(loss[0, 0], ("x", "y", "c"))
    return (total, dx[None], *[out[p + n] for p in ("grad_", "delta_", "new_m_", "new_v_") for n in WEIGHTS])


def kernel(x, mem, positions, mem_norm_g, mix_pre_g, mix_post_g, w_in, gm_v_g, gm_w_s, gm_b_s, pool_w, pool_scale, attn_sinks,
           w_o, x_pre_g, x_post_g, w_xq, w_xkv, w_xo, ffn_pre_g, ffn_post_g, w_gate_up, w_down, loss_target, m_mem_norm_g,
           m_mix_pre_g, m_mix_post_g, m_w_in, m_gm_v_g, m_gm_w_s, m_gm_b_s, m_pool_w, m_pool_scale, m_attn_sinks, m_w_o,
           m_x_pre_g, m_x_post_g, m_w_xq, m_w_xkv, m_w_xo, m_ffn_pre_g, m_ffn_post_g, m_w_gate_up, m_w_down, v_mem_norm_g,
           v_mix_pre_g, v_mix_post_g, v_w_in, v_gm_v_g, v_gm_w_s, v_gm_b_s, v_pool_w, v_pool_scale, v_attn_sinks, v_w_o,
           v_x_pre_g, v_x_post_g, v_w_xq, v_w_xkv, v_w_xo, v_ffn_pre_g, v_ffn_post_g, v_w_gate_up, v_w_down):
    return _step(dict(locals()))
```

```python
import functools
import math

import jax
import jax.numpy as jnp
from jax import lax
from jax.experimental import pallas as pl
from jax.experimental.pallas import tpu as pltpu

F32, BF16 = jnp.float32, jnp.bfloat16
EPS = 1e-6
HEAD_DIM = 64
BLOCK = 128
POOL_WINDOWS = (2, 4, 8, 16)
ATT_Q_HEADS = 8
X_HEADS = 4
ROPE_THETA = 10000.0
ADAM_LR, ADAM_B1, ADAM_B2, ADAM_EPS, ADAM_WD, ADAM_STEP = 0.001, 0.9, 0.999, 1e-08, 0.01, 10
N_DEV = 8
TOKEN_TILE = 512
VMEM_LIMIT_BYTES = 50 * 2**20
NEG = -1e30
MESH = pl.DeviceIdType.MESH

NN = ((1,), (0,))
NT = ((1,), (1,))
TN = ((0,), (0,))


def _dot(a, b, dims):
    return lax.dot_general(a.astype(BF16), b.astype(BF16), (dims, ((), ())), preferred_element_type=F32)


def _params(*sem):
    return pltpu.CompilerParams(dimension_semantics=sem, vmem_limit_bytes=VMEM_LIMIT_BYTES)


def _full(shape):
    return pl.BlockSpec(shape, lambda *_: (0,) * len(shape))


def _lane(shape):
    return lax.broadcasted_iota(jnp.int32, shape, len(shape) - 1)


def _rms_fwd(x, g):
    r = lax.rsqrt(jnp.mean(x * x, axis=-1, keepdims=True) + EPS)
    return x * r * g


def _rms_bwd(x, g, dy):
    r = lax.rsqrt(jnp.mean(x * x, axis=-1, keepdims=True) + EPS)
    xh = x * r
    dg = jnp.sum(dy * xh, axis=0, keepdims=True)
    dxh = dy * g
    dx = r * (dxh - xh * jnp.mean(dxh * xh, axis=-1, keepdims=True))
    return dx, dg


def _silu_parts(gate):
    sg = 1.0 / (1.0 + jnp.exp(-gate))
    return gate * sg, sg


def _swiglu_cols(gu_ref, out_ref, width):
    step = 128 * max(1, (width // 128) // 4)
    for c0 in range(0, width, step):
        c1 = min(width, c0 + step)
        gate = gu_ref[:, c0:c1].astype(F32)
        up = gu_ref[:, width + c0:width + c1].astype(F32)
        out_ref[:, c0:c1] = (_silu_parts(gate)[0] * up).astype(out_ref.dtype)


def _col_tile(n, cap=1408):
    best = n
    for t in range(128, cap + 1, 128):
        if n % t == 0:
            best = t
    return best if n > cap else n


def _mm_fwd(a, w, *, nt, name, pre=None, g_pre=None, post=None, g_post=None, xres=None, g_next=None, out_dtype=BF16, tile=TOKEN_TILE):
    S = a.shape[0]
    n_out, k = (w.shape[0], w.shape[1]) if nt else (w.shape[1], w.shape[0])
    T = min(tile, S)
    tn = n_out if post else _col_tile(n_out)
    swap = pre is None and post is None and n_out > tn
    grid = (n_out // tn, S // T) if swap else (S // T, n_out // tn)

    def at(index_map):
        return (lambda p, q: index_map(q, p)) if swap else index_map

    def body(*refs):
        it = iter(refs)
        a_ref, w_ref = next(it), next(it)
        gpre_ref = next(it) if pre == "norm" else None
        gpost_ref, x_ref = (next(it), next(it)) if post else (None, None)
        gnext_ref = next(it) if g_next is not None else None
        outs = [next(it) for _ in range((2 if post else 1) + (1 if pre else 0))]
        hnext_ref = next(it) if g_next is not None else None
        as_ref = next(it) if pre else None
        if pre:
            @pl.when(pl.program_id(1) == 0)
            def _():
                if pre == "norm":
                    h = _rms_fwd(a_ref[...], gpre_ref[...]).astype(BF16)
                    outs[-1][...] = h
                    as_ref[...] = h
                else:
                    _swiglu_cols(a_ref, as_ref, k)
                    outs[-1][...] = as_ref[...]
            av = as_ref[...]
        else:
            av = a_ref[...]
        acc = _dot(av, w_ref[...], NT if nt else NN)
        if post:
            outs[0][...] = acc.astype(BF16)
            x_new = x_ref[...] + _rms_fwd(acc, gpost_ref[...])
            outs[1][...] = x_new
            if g_next is not None:
                hnext_ref[...] = _rms_fwd(x_new, gnext_ref[...]).astype(BF16)
        else:
            outs[0][...] = acc.astype(out_dtype)

    in_specs = [pl.BlockSpec((T, a.shape[1]), at(lambda i, j: (i, 0))),
                pl.BlockSpec((tn, k), at(lambda i, j: (j, 0))) if nt else pl.BlockSpec((k, tn), at(lambda i, j: (0, j)))]
    args = [a, w]
    if pre == "norm":
        in_specs.append(_full((1, k)))
        args.append(g_pre)
    if post:
        in_specs += [_full((1, n_out)), pl.BlockSpec((T, n_out), lambda i, j: (i, 0))]
        args += [g_post, xres]
    if g_next is not None:
        in_specs.append(_full((1, n_out)))
        args.append(g_next)
    out_block = pl.BlockSpec((T, tn), at(lambda i, j: (i, j)))
    if post:
        out_shape = [jax.ShapeDtypeStruct((S, n_out), BF16), jax.ShapeDtypeStruct((S, n_out), F32)]
        out_specs = [out_block, out_block]
    else:
        out_shape = [jax.ShapeDtypeStruct((S, n_out), out_dtype)]
        out_specs = [out_block]
    if pre:
        out_shape.append(jax.ShapeDtypeStruct((S, k), BF16))
        out_specs.append(pl.BlockSpec((T, k), lambda i, j: (i, 0)))
    if g_next is not None:
        out_shape.append(jax.ShapeDtypeStruct((S, n_out), BF16))
        out_specs.append(out_block)
    scratch = [pltpu.VMEM((T, k), BF16)] if pre else []
    return pl.pallas_call(body, name=name, grid=grid, in_specs=in_specs, out_specs=out_specs, out_shape=out_shape,
                          scratch_shapes=scratch, compiler_params=_params("parallel", "arbitrary"))(*args)


def _mm_bwd_post(dy, w, x, g, dx_in, *, nt, name, below=None, tile=TOKEN_TILE):
    S, k = dy.shape
    n = x.shape[1]
    T = min(tile, S)

    def body(*refs):
        dy_ref, w_ref, x_ref, g_ref, dxi_ref = refs[:5]
        m_ref, gb_ref = refs[5:7] if below else (None, None)
        dxo_ref, dg_ref = refs[-4:-2] if below else refs[-2:]

        @pl.when(pl.program_id(0) == 0)
        def _():
            dg_ref[...] = jnp.zeros_like(dg_ref)
            if below:
                refs[-1][...] = jnp.zeros_like(refs[-1])

        dh = _dot(dy_ref[...], w_ref[...], NT if nt else NN)
        dxn, dg = _rms_bwd(x_ref[...], g_ref[...], dh)
        dg_ref[...] += dg
        dxo = dxi_ref[...] + dxn
        dxo_ref[...] = dxo
        if below:
            dmv, dgb = _rms_bwd(m_ref[...].astype(F32), gb_ref[...], dxo)
            refs[-2][...] = dmv.astype(BF16)
            refs[-1][...] += dgb

    row = pl.BlockSpec((T, n), lambda i: (i, 0))
    in_specs = [pl.BlockSpec((T, k), lambda i: (i, 0)), _full(w.shape), row, _full((1, n)), row]
    out_specs = [row, _full((1, n))]
    out_shape = [jax.ShapeDtypeStruct((S, n), F32), jax.ShapeDtypeStruct((1, n), F32)]
    if below:
        in_specs += [row, _full((1, n))]
        out_specs += [row, _full((1, n))]
        out_shape += [jax.ShapeDtypeStruct((S, n), BF16), jax.ShapeDtypeStruct((1, n), F32)]
    return pl.pallas_call(body, name=name, grid=(S // T,), in_specs=in_specs, out_specs=out_specs, out_shape=out_shape,
                          compiler_params=_params("arbitrary"))(dy, w, x, g, dx_in, *(below or ()))


def _wgrad(a, b, *, name):
    S, k = a.shape
    n = b.shape[1]
    tk = 512 if k % 512 == 0 and k > 1536 else 256

    def body(a_ref, b_ref, o_ref):
        o_ref[...] = _dot(a_ref[...], b_ref[...], TN)

    return pl.pallas_call(body, name=name, grid=(k // tk,), in_specs=[pl.BlockSpec((S, tk), lambda kk: (0, kk)), _full((S, n))],
                          out_specs=pl.BlockSpec((tk, n), lambda kk: (kk, 0)),
                          out_shape=jax.ShapeDtypeStruct((k, n), F32), compiler_params=_params("parallel"))(a, b)


def _ffn_out_bwd(dm, w, gu, *, name, tile=256):
    S, n = dm.shape
    f = w.shape[0]
    T = min(tile, S)

    def body(dm_ref, w_ref, gu_ref, dgu_ref, da_ref):
        da_ref[...] = _dot(dm_ref[...], w_ref[...], NT)
        step = 128 * max(1, (f // 128) // 4)
        for c0 in range(0, f, step):
            c1 = min(f, c0 + step)
            gate = gu_ref[:, c0:c1].astype(F32)
            up = gu_ref[:, f + c0:f + c1].astype(F32)
            da = da_ref[:, c0:c1]
            sl, sg = _silu_parts(gate)
            dgu_ref[:, c0:c1] = (da * up * (sg + sl * (1.0 - sg))).astype(BF16)
            dgu_ref[:, f + c0:f + c1] = (da * sl).astype(BF16)

    row = pl.BlockSpec((T, n), lambda i: (i, 0))
    wide = pl.BlockSpec((T, 2 * f), lambda i: (i, 0))
    return pl.pallas_call(
        body, name=name, grid=(S // T,), in_specs=[row, _full((f, n)), wide], out_specs=wide,
        out_shape=jax.ShapeDtypeStruct((S, 2 * f), BF16),
        scratch_shapes=[pltpu.VMEM((T, f), F32)], compiler_params=_params("parallel"))(dm, w, gu)


def _loss_grad(y, target, m, g, *, name, tile=TOKEN_TILE):
    S, d = y.shape
    T = min(tile, S)
    steps = S // T

    def body(y_ref, t_ref, m_ref, g_ref, loss_ref, dy_ref, dm_ref, dg_ref, acc_ref):
        i = pl.program_id(0)

        @pl.when(i == 0)
        def _():
            acc_ref[...] = jnp.zeros_like(acc_ref)
            dg_ref[...] = jnp.zeros_like(dg_ref)

        diff = y_ref[...] - t_ref[...]
        dy = diff * (1.0 / d)
        dy_ref[...] = dy
        dmv, dg = _rms_bwd(m_ref[...].astype(F32), g_ref[...], dy)
        dm_ref[...] = dmv.astype(BF16)
        dg_ref[...] += dg
        acc_ref[...] += jnp.sum(diff * diff, axis=0, keepdims=True)

        @pl.when(i == steps - 1)
        def _():
            total = jnp.sum(acc_ref[...], axis=1, keepdims=True) * (0.5 / d)
            loss_ref[...] = jnp.broadcast_to(total, loss_ref.shape)

    row = pl.BlockSpec((T, d), lambda i: (i, 0))
    return pl.pallas_call(
        body, name=name, grid=(steps,), in_specs=[row, row, row, _full((1, d))], out_specs=[_full((1, 128)), row, row, _full((1, d))],
        out_shape=[jax.ShapeDtypeStruct((1, 128), F32), jax.ShapeDtypeStruct((S, d), F32), jax.ShapeDtypeStruct((S, d), BF16),
                   jax.ShapeDtypeStruct((1, d), F32)],
        scratch_shapes=[pltpu.VMEM((1, d), F32)], compiler_params=_params("arbitrary"))(y, target, m, g)


def _norm_dgain(x, dys, *, name):
    def body(x_ref, dy_ref, o_ref):
        xv = x_ref[...]
        dy = dy_ref[0]
        for l in range(1, dys.shape[0]):
            dy = dy + dy_ref[l]
        r = lax.rsqrt(jnp.mean(xv * xv, axis=-1, keepdims=True) + EPS)
        o_ref[...] = jnp.sum(dy * xv * r, axis=0, keepdims=True)

    return pl.pallas_call(body, name=name, out_shape=jax.ShapeDtypeStruct((1, x.shape[1]), F32),
                          compiler_params=pltpu.CompilerParams(vmem_limit_bytes=VMEM_LIMIT_BYTES))(x, dys)


def _xattn_probs(q_ref, kv_ref, h, d, hd):
    q = q_ref[:, h * hd:(h + 1) * hd]
    kh = kv_ref[:, h * hd:(h + 1) * hd]
    vh = kv_ref[:, d + h * hd:d + (h + 1) * hd]
    s = _dot(q, kh, NT) * (hd ** -0.5)
    e = jnp.exp(s - jnp.max(s, axis=-1, keepdims=True))
    return q, kh, vh, e / jnp.sum(e, axis=-1, keepdims=True)


def _xattn_fwd(q, kv, *, name, tile=TOKEN_TILE):
    S, d = q.shape
    mlen = kv.shape[0]
    hd = d // X_HEADS
    T = min(tile, S)

    def body(q_ref, kv_ref, o_ref):
        for h in range(X_HEADS):
            _, _, vh, p = _xattn_probs(q_ref, kv_ref, h, d, hd)
            o_ref[:, h * hd:(h + 1) * hd] = _dot(p, vh, NN).astype(BF16)

    row = pl.BlockSpec((T, d), lambda i: (i, 0))
    return pl.pallas_call(body, name=name, grid=(S // T,), in_specs=[row, _full((mlen, 2 * d))], out_specs=row,
                          out_shape=jax.ShapeDtypeStruct((S, d), BF16), compiler_params=_params("parallel"))(q, kv)


def _xattn_bwd(q, kv, do, *, name, tile=TOKEN_TILE):
    S, d = q.shape
    mlen = kv.shape[0]
    hd = d // X_HEADS
    T = min(tile, S)

    def body(q_ref, kv_ref, do_ref, dq_ref, dkv_ref):
        @pl.when(pl.program_id(0) == 0)
        def _():
            dkv_ref[...] = jnp.zeros_like(dkv_ref)

        for h in range(X_HEADS):
            cols = slice(h * hd, (h + 1) * hd)
            qh, kh, vh, doh = q_ref[:, cols], kv_ref[:, cols], kv_ref[:, d + h * hd:d + (h + 1) * hd], do_ref[:, cols]
            st = _dot(kh, qh, NT) * (hd ** -0.5)
            e = jnp.exp(st - jnp.max(st, axis=0, keepdims=True))
            pt = e / jnp.sum(e, axis=0, keepdims=True)
            dpt = _dot(vh, doh, NT)
            dst = (pt * (dpt - jnp.sum(pt * dpt, axis=0, keepdims=True)) * (hd ** -0.5)).astype(BF16)
            dkv_ref[:, cols] += _dot(dst, qh, NN)
            dkv_ref[:, d + h * hd:d + (h + 1) * hd] += _dot(pt, doh, NN)
            dq_ref[:, cols] = _dot(dst, kh, TN).astype(BF16)

    row = pl.BlockSpec((T, d), lambda i: (i, 0))
    return pl.pallas_call(
        body, name=name, grid=(S // T,), in_specs=[row, _full((mlen, 2 * d)), row],
        out_specs=[row, _full((mlen, 2 * d))],
        out_shape=[jax.ShapeDtypeStruct((S, d), BF16), jax.ShapeDtypeStruct((mlen, 2 * d), F32)],
        compiler_params=_params("arbitrary"))(q, kv, do)


GM0, POOL0, Q0, KV0, Z_END = 0, 512, 768, 1280, 1536


def _rope_tables(pos_col, inv_row, *, name):
    S = pos_col.shape[0]

    def body(p_ref, inv_ref, cos_ref, sin_ref):
        ang = p_ref[...].astype(F32) * inv_ref[...]
        sin = jnp.sin(ang)
        cos_ref[...] = jnp.cos(ang)
        sin_ref[...] = jnp.where(_lane(ang.shape) % HEAD_DIM < HEAD_DIM // 2, -sin, sin)

    T = min(TOKEN_TILE, S)
    return pl.pallas_call(
        body, name=name, grid=(S // T,),
        in_specs=[pl.BlockSpec((T, 1), lambda i: (i, 0)), _full((1, 128))],
        out_specs=[pl.BlockSpec((T, 128), lambda i: (i, 0))] * 2,
        out_shape=[jax.ShapeDtypeStruct((S, 128), F32)] * 2, compiler_params=_params("parallel"))(pos_col, inv_row)


def _swap_halves(x):
    n = x.shape[-1]
    return jnp.where(_lane(x.shape) % HEAD_DIM < HEAD_DIM // 2, pltpu.roll(x, n - HEAD_DIM // 2, 1), pltpu.roll(x, HEAD_DIM // 2, 1))


def _rope(x, cos, sin_s):
    reps = x.shape[-1] // 128
    if reps > 1:
        cos, sin_s = jnp.tile(cos, (1, reps)), jnp.tile(sin_s, (1, reps))
    return x * cos + _swap_halves(x) * sin_s


def _rope_bwd(dy, cos, sin_s):
    reps = dy.shape[-1] // 128
    if reps > 1:
        cos, sin_s = jnp.tile(cos, (1, reps)), jnp.tile(sin_s, (1, reps))
    return dy * cos + _swap_halves(dy * sin_s)


def _gelu_parts(x):
    c = math.sqrt(2.0 / math.pi)
    t = jnp.tanh(c * (x + 0.044715 * x * x * x))
    return 0.5 * x * (1.0 + t), t


def _gelu_grad(x, t):
    c = math.sqrt(2.0 / math.pi)
    return 0.5 * (1.0 + t) + 0.5 * x * (1.0 - t * t) * c * (1.0 + 3.0 * 0.044715 * x * x)


def _group_sum(x):
    gid = _lane(x.shape) // HEAD_DIM
    out = jnp.zeros_like(x)
    for g in range(x.shape[-1] // HEAD_DIM):
        sel = gid == g
        out = jnp.where(sel, jnp.sum(jnp.where(sel, x, 0.0), axis=-1, keepdims=True), out)
    return out


def _low_half(shape):
    return _lane(shape) % 128 < HEAD_DIM


def _tril(w):
    r = lax.broadcasted_iota(jnp.int32, w.shape, 0)
    c = lax.broadcasted_iota(jnp.int32, w.shape, 1)
    return jnp.where(r >= c, w, 0.0)


def _triu(w):
    r = lax.broadcasted_iota(jnp.int32, w.shape, 0)
    c = lax.broadcasted_iota(jnp.int32, w.shape, 1)
    return jnp.where(r <= c, w, 0.0)


def _gating_fwd(zg, vg, wc_ref, bfull):
    ge, th = _gelu_parts(zg)
    u, v = ge[:, :256], ge[:, 256:]
    r = lax.rsqrt(_group_sum(v * v) * (1.0 / HEAD_DIM) + EPS)
    vhat = v * r
    vn = vhat * vg
    low = _low_half((BLOCK, 128))
    blocks = []
    for blk in range(2):
        vb = vn[:, 128 * blk:128 * (blk + 1)]
        m0 = _dot(_tril(wc_ref[2 * blk]), vb, NN)
        m1 = _dot(_tril(wc_ref[2 * blk + 1]), vb, NN)
        blocks.append(jnp.where(low, m0, m1))
    mixed = jnp.concatenate(blocks, axis=1) + bfull
    return u * mixed, (th, u, r, vhat, vn, mixed)


def _band(w, transpose):
    if transpose:
        s2 = lax.broadcasted_iota(jnp.int32, (2 * BLOCK, BLOCK), 0)
        t = lax.broadcasted_iota(jnp.int32, (2 * BLOCK, BLOCK), 1)
    else:
        t = lax.broadcasted_iota(jnp.int32, (BLOCK, 2 * BLOCK), 0)
        s2 = lax.broadcasted_iota(jnp.int32, (BLOCK, 2 * BLOCK), 1)
    d = t - s2 + BLOCK
    return jnp.where((d >= 0) & (d < w), 1.0, 0.0).astype(BF16)


def _inv_count(n):
    t = lax.broadcasted_iota(jnp.int32, (BLOCK, 256), 0) + n * BLOCK
    gid = _lane((BLOCK, 256)) // HEAD_DIM
    win = jnp.where(gid == 0, POOL_WINDOWS[0], jnp.where(gid == 1, POOL_WINDOWS[1], jnp.where(gid == 2, POOL_WINDOWS[2], POOL_WINDOWS[3])))
    return 1.0 / jnp.minimum(t + 1, win).astype(F32)


def _pool_fwd(p, pp, n, bd, scale):
    low = _low_half((BLOCK, 128))
    blocks = []
    for blk in range(2):
        both = jnp.concatenate([pp[:, 128 * blk:128 * (blk + 1)], p[:, 128 * blk:128 * (blk + 1)]], axis=0)
        s0 = _dot(_band(POOL_WINDOWS[2 * blk], False), both, NN)
        s1 = _dot(_band(POOL_WINDOWS[2 * blk + 1], False), both, NN)
        blocks.append(jnp.where(low, s0, s1))
    inv = _inv_count(n)
    pooled = jnp.concatenate(blocks, axis=1) * inv - p
    mapped = _dot(pooled, bd, NN)
    return mapped * scale, (inv, pooled, mapped)


def _mixer_fwd(z, cos_t, sin_t, vg, wc, bfull, bd, scale, sinks, *, name):
    S = z.shape[0]
    nb = S // BLOCK

    def body(zc_ref, zpp_ref, zpk_ref, cc_ref, cp_ref, sc_ref, sp_ref, vg_ref, wc_ref, bf_ref, bd_ref, scale_ref, sinks_ref, out_ref, pen_ref):
        n = pl.program_id(0)

        @pl.when(n == 0)
        def _():
            tall = (ATT_Q_HEADS // 2 * BLOCK, 2 * BLOCK)
            qi = lax.broadcasted_iota(jnp.int32, tall, 0) % BLOCK
            ki = lax.broadcasted_iota(jnp.int32, tall, 1)
            inside = (ki > qi) & (ki <= qi + BLOCK)
            pen_ref[0] = jnp.where(inside, 0.0, NEG)
            pen_ref[1] = jnp.where(inside & (ki >= BLOCK), 0.0, NEG)

        keep = (n > 0).astype(F32)
        a, _ = _gating_fwd(zc_ref[:, GM0:POOL0].astype(F32), vg_ref[...], wc_ref, bf_ref[...])
        out_ref[:, 0:256] = a.astype(BF16)
        b, _ = _pool_fwd(zc_ref[:, POOL0:Q0].astype(F32), zpp_ref[...].astype(F32) * keep, n, bd_ref[...], scale_ref[...])
        out_ref[:, 256:512] = b.astype(BF16)
        cos_c, sin_c = cc_ref[...], sc_ref[...]
        qrot = _rope(zc_ref[:, Q0:KV0].astype(F32), cos_c, sin_c)
        kk = jnp.concatenate([_rope(zpk_ref[:, 0:128].astype(F32), cp_ref[...], sp_ref[...]), _rope(zc_ref[:, KV0:KV0 + 128].astype(F32), cos_c, sin_c)], axis=0)
        vv = jnp.concatenate([zpk_ref[:, 128:256].astype(F32), zc_ref[:, KV0 + 128:Z_END].astype(F32)], axis=0)
        kk, vv = kk.astype(BF16), vv.astype(BF16)
        pen = pen_ref[jnp.where(n > 0, 0, 1)]
        low = _low_half((BLOCK, 128))
        q_slabs = [qrot[:, 128 * s:128 * (s + 1)] for s in range(ATT_Q_HEADS // 2)]
        q_rolled = [pltpu.roll(v, HEAD_DIM, 1) for v in q_slabs]
        o_heads = [None] * ATT_Q_HEADS
        for h in range(2):
            lanes = ~low if h else low
            heads = range(4 * h, 4 * h + 4)
            qm = jnp.concatenate([jnp.where(lanes, (q_slabs if j % 2 == h else q_rolled)[j // 2], 0.0) for j in heads], axis=0)
            sc = _dot(qm, kk, NT) * (HEAD_DIM ** -0.5) + pen
            p = []
            for i, j in enumerate(heads):
                sch, sink = sc[BLOCK * i:BLOCK * (i + 1)], sinks_ref[j]
                mx = jnp.maximum(jnp.max(sch, axis=-1, keepdims=True), sink)
                e = jnp.exp(sch - mx)
                p.append((e * (1.0 / (jnp.sum(e, axis=-1, keepdims=True) + jnp.exp(sink - mx)))).astype(BF16))
            o = _dot(jnp.concatenate(p, axis=0), vv, NN)
            for i, j in enumerate(heads):
                oh = o[BLOCK * i:BLOCK * (i + 1)]
                o_heads[j] = oh if j % 2 == h else pltpu.roll(oh, HEAD_DIM, 1)
        for s in range(ATT_Q_HEADS // 2):
            out_ref[:, 512 + 128 * s:512 + 128 * (s + 1)] = jnp.where(low, o_heads[2 * s], o_heads[2 * s + 1]).astype(BF16)

    cur = lambda n: (n, 0)
    prev = lambda n: (jnp.maximum(n - 1, 0), 0)
    in_specs = [
        pl.BlockSpec((BLOCK, Z_END), cur),
        pl.BlockSpec((BLOCK, 256), lambda n: (jnp.maximum(n - 1, 0), POOL0 // 256)),
        pl.BlockSpec((BLOCK, 256), lambda n: (jnp.maximum(n - 1, 0), KV0 // 256)),
        pl.BlockSpec((BLOCK, 128), cur), pl.BlockSpec((BLOCK, 128), prev),
        pl.BlockSpec((BLOCK, 128), cur), pl.BlockSpec((BLOCK, 128), prev),
        _full((1, 256)), _full((4, BLOCK, BLOCK)), _full((BLOCK, 256)), _full((256, 256)), _full((1, 256)),
        pl.BlockSpec(memory_space=pltpu.SMEM),
    ]
    return pl.pallas_call(
        body, name=name, grid=(nb,), in_specs=in_specs, out_specs=pl.BlockSpec((BLOCK, 1024), cur),
        out_shape=jax.ShapeDtypeStruct((S, 1024), BF16), scratch_shapes=[pltpu.VMEM((2, ATT_Q_HEADS // 2 * BLOCK, 2 * BLOCK), F32)],
        compiler_params=_params("arbitrary"))(z, z, z, cos_t, cos_t, sin_t, sin_t, vg, wc, bfull, bd, scale, sinks)


def _mixer_bwd(z, dcat, cos_t, sin_t, vg, wc, bfull, bd, scale, sinks, *, name):
    S = z.shape[0]
    nb = S // BLOCK

    def body(zc_ref, zpp_ref, zpk_ref, dc_ref, cc_ref, cp_ref, sc_ref, sp_ref, vg_ref, wc_ref, bf_ref, bd_ref, scale_ref, sinks_ref, wct_ref,
             dz_ref, dvg_ref, dwc_ref, dbf_ref, dbd_ref, dscale_ref, dsink_ref, carry_ref, pen_ref):
        n = pl.program_id(0)

        @pl.when(n == 0)
        def _():
            for ref in (dvg_ref, dwc_ref, dbf_ref, dbd_ref, dscale_ref, dsink_ref):
                ref[...] = jnp.zeros_like(ref)
            wide = (2 * BLOCK, ATT_Q_HEADS // 2 * BLOCK)
            ki = lax.broadcasted_iota(jnp.int32, wide, 0)
            qi = lax.broadcasted_iota(jnp.int32, wide, 1) % BLOCK
            inside = (ki > qi) & (ki <= qi + BLOCK)
            pen_ref[0] = jnp.where(inside, 0.0, NEG)
            pen_ref[1] = jnp.where(inside & (ki >= BLOCK), 0.0, NEG)

        @pl.when(n < nb)
        def _():
            keep = (n > 0).astype(F32)
            low = _low_half((BLOCK, 128))
            zg = zc_ref[:, GM0:POOL0].astype(F32)
            vg_v = vg_ref[...]
            _, (th, u, r, vhat, vn, mixed) = _gating_fwd(zg, vg_v, wc_ref, bf_ref[...])
            da = dc_ref[:, 0:256].astype(F32)
            dmixed = da * u
            dbf_ref[...] += dmixed
            dvn = []
            for blk in range(2):
                dmb = dmixed[:, 128 * blk:128 * (blk + 1)]
                vb = vn[:, 128 * blk:128 * (blk + 1)]
                acc = None
                for half in range(2):
                    g = 2 * blk + half
                    dmg = jnp.where(low if half == 0 else ~low, dmb, 0.0)
                    dwc_ref[g] += _tril(_dot(dmg, vb, NT))
                    part = _dot(_triu(wct_ref[g]), dmg, NN)
                    acc = part if acc is None else acc + part
                dvn.append(acc)
            dvn = jnp.concatenate(dvn, axis=1)
            dvg_ref[...] += jnp.sum(dvn * vhat, axis=0, keepdims=True)
            dvhat = dvn * vg_v
            dv = r * (dvhat - vhat * (_group_sum(dvhat * vhat) * (1.0 / HEAD_DIM)))
            dge = jnp.concatenate([da * mixed, dv], axis=1)
            own_g = dge * _gelu_grad(zg, th)
            p = zc_ref[:, POOL0:Q0].astype(F32)
            bd_v, scale_v = bd_ref[...], scale_ref[...]
            _, (inv, pooled, mapped) = _pool_fwd(p, zpp_ref[...].astype(F32) * keep, n, bd_v, scale_v)
            db = dc_ref[:, 256:512].astype(F32)
            dscale_ref[...] += jnp.sum(db * mapped, axis=0, keepdims=True)
            dmapped = db * scale_v
            dbd_ref[...] += _dot(pooled, dmapped, TN)
            dpooled = _dot(dmapped, bd_v, NT)
            dps = dpooled * inv
            back = []
            for blk in range(2):
                dpb = dps[:, 128 * blk:128 * (blk + 1)]
                t0 = _dot(_band(POOL_WINDOWS[2 * blk], True), dpb, NN)
                t1 = _dot(_band(POOL_WINDOWS[2 * blk + 1], True), dpb, NN)
                back.append(jnp.where(jnp.concatenate([low, low], axis=0), t0, t1))
            back = jnp.concatenate(back, axis=1)
            halo_p = back[:BLOCK] * keep
            own_p = back[BLOCK:] - dpooled
            cos_c, sin_c, cos_p, sin_p = cc_ref[...], sc_ref[...], cp_ref[...], sp_ref[...]
            qrot = _rope(zc_ref[:, Q0:KV0].astype(F32), cos_c, sin_c)
            kk = jnp.concatenate([_rope(zpk_ref[:, 0:128].astype(F32), cos_p, sin_p), _rope(zc_ref[:, KV0:KV0 + 128].astype(F32), cos_c, sin_c)], axis=0)
            vv = jnp.concatenate([zpk_ref[:, 128:256].astype(F32), zc_ref[:, KV0 + 128:Z_END].astype(F32)], axis=0)
            kt = jnp.transpose(kk).astype(BF16)
            kk, vv = kk.astype(BF16), vv.astype(BF16)
            pen = pen_ref[jnp.where(n > 0, 0, 1)]
            q_slabs = [qrot[:, 128 * s:128 * (s + 1)] for s in range(ATT_Q_HEADS // 2)]
            d_slabs = [dc_ref[:, 512 + 128 * s:512 + 128 * (s + 1)].astype(F32) for s in range(ATT_Q_HEADS // 2)]
            q_rolled = [pltpu.roll(v, HEAD_DIM, 1) for v in q_slabs]
            d_rolled = [pltpu.roll(v, HEAD_DIM, 1) for v in d_slabs]
            dkk = jnp.zeros((2 * BLOCK, 128), F32)
            dvv = jnp.zeros((2 * BLOCK, 128), F32)
            dq_heads = [None] * ATT_Q_HEADS
            for h in range(2):
                lanes = ~low if h else low
                heads = range(4 * h, 4 * h + 4)
                qm = jnp.concatenate([jnp.where(lanes, (q_slabs if j % 2 == h else q_rolled)[j // 2], 0.0) for j in heads], axis=0).astype(BF16)
                dom = jnp.concatenate([jnp.where(lanes, (d_slabs if j % 2 == h else d_rolled)[j // 2], 0.0) for j in heads], axis=0).astype(BF16)
                sink = jnp.concatenate([jnp.full((1, BLOCK), sinks_ref[j], F32) for j in heads], axis=1)
                sc = _dot(kk, qm, NT) * (HEAD_DIM ** -0.5) + pen
                mx = jnp.maximum(jnp.max(sc, axis=0, keepdims=True), sink)
                e = jnp.exp(sc - mx)
                es = jnp.exp(sink - mx)
                inv_den = 1.0 / (jnp.sum(e, axis=0, keepdims=True) + es)
                pt = e * inv_den
                dpt = _dot(vv, dom, NT)
                dvv = dvv + _dot(pt, dom, NN)
                dcol = jnp.sum(pt * dpt, axis=0, keepdims=True)
                dst = (pt * (dpt - dcol) * (HEAD_DIM ** -0.5)).astype(BF16)
                dsink = es * inv_den * dcol
                dkk = dkk + _dot(dst, qm, NN)
                dqt = _dot(kt, dst, NN)
                for i, j in enumerate(heads):
                    part = jnp.sum(dsink[:, BLOCK * i:BLOCK * (i + 1)], axis=1, keepdims=True)
                    dsink_ref[pl.ds(j, 1), :] = dsink_ref[pl.ds(j, 1), :] - jnp.broadcast_to(part, (1, 128))
                    dqh = jnp.transpose(dqt[:, BLOCK * i:BLOCK * (i + 1)])
                    dq_heads[j] = dqh if j % 2 == h else pltpu.roll(dqh, HEAD_DIM, 1)
            dqrot = jnp.concatenate([jnp.where(low, dq_heads[2 * s], dq_heads[2 * s + 1]) for s in range(ATT_Q_HEADS // 2)], axis=1)
            own_q = _rope_bwd(dqrot, cos_c, sin_c)
            own_k = _rope_bwd(dkk[BLOCK:], cos_c, sin_c)
            halo_k = _rope_bwd(dkk[:BLOCK], cos_p, sin_p) * keep
            own_v, halo_v = dvv[BLOCK:], dvv[:BLOCK] * keep

            @pl.when(n > 0)
            def _():
                dz_ref[:, GM0:POOL0] = carry_ref[:, GM0:POOL0].astype(BF16)
                dz_ref[:, POOL0:Q0] = (carry_ref[:, POOL0:Q0] + halo_p).astype(BF16)
                dz_ref[:, Q0:KV0] = carry_ref[:, Q0:KV0].astype(BF16)
                dz_ref[:, KV0:KV0 + 128] = (carry_ref[:, KV0:KV0 + 128] + halo_k).astype(BF16)
                dz_ref[:, KV0 + 128:Z_END] = (carry_ref[:, KV0 + 128:Z_END] + halo_v).astype(BF16)

            carry_ref[:, GM0:POOL0] = own_g
            carry_ref[:, POOL0:Q0] = own_p
            carry_ref[:, Q0:KV0] = own_q
            carry_ref[:, KV0:KV0 + 128] = own_k
            carry_ref[:, KV0 + 128:Z_END] = own_v

        @pl.when(n == nb)
        def _():
            dz_ref[...] = carry_ref[...].astype(BF16)

    cur = lambda n: (jnp.minimum(n, nb - 1), 0)
    prev = lambda n: (jnp.maximum(jnp.minimum(n, nb - 1) - 1, 0), 0)
    in_specs = [
        pl.BlockSpec((BLOCK, Z_END), cur),
        pl.BlockSpec((BLOCK, 256), lambda n: (jnp.maximum(jnp.minimum(n, nb - 1) - 1, 0), POOL0 // 256)),
        pl.BlockSpec((BLOCK, 256), lambda n: (jnp.maximum(jnp.minimum(n, nb - 1) - 1, 0), KV0 // 256)),
        pl.BlockSpec((BLOCK, 1024), cur),
        pl.BlockSpec((BLOCK, 128), cur), pl.BlockSpec((BLOCK, 128), prev),
        pl.BlockSpec((BLOCK, 128), cur), pl.BlockSpec((BLOCK, 128), prev),
        _full((1, 256)), _full((4, BLOCK, BLOCK)), _full((BLOCK, 256)), _full((256, 256)), _full((1, 256)),
        pl.BlockSpec(memory_space=pltpu.SMEM), _full((4, BLOCK, BLOCK)),
    ]
    out_specs = [pl.BlockSpec((BLOCK, Z_END), lambda n: (jnp.maximum(n - 1, 0), 0)),
                 _full((1, 256)), _full((4, BLOCK, BLOCK)), _full((BLOCK, 256)), _full((256, 256)), _full((1, 256)), _full((8, 128))]
    out_shape = [jax.ShapeDtypeStruct((S, Z_END), BF16), jax.ShapeDtypeStruct((1, 256), F32),
                 jax.ShapeDtypeStruct((4, BLOCK, BLOCK), F32), jax.ShapeDtypeStruct((BLOCK, 256), F32),
                 jax.ShapeDtypeStruct((256, 256), F32), jax.ShapeDtypeStruct((1, 256), F32), jax.ShapeDtypeStruct((8, 128), F32)]
    return pl.pallas_call(
        body, name=name, grid=(nb + 1,), in_specs=in_specs, out_specs=out_specs, out_shape=out_shape,
        scratch_shapes=[pltpu.VMEM((BLOCK, Z_END), F32), pltpu.VMEM((2, 2 * BLOCK, ATT_Q_HEADS // 2 * BLOCK), F32)],
        compiler_params=_params("arbitrary"))(z, z, z, dcat, cos_t, cos_t, sin_t, sin_t, vg, wc, bfull, bd, scale, sinks, jnp.swapaxes(wc, 1, 2))


HBM = pl.BlockSpec(memory_space=pl.ANY)


def _place():
    return lax.axis_index("x"), lax.axis_index("y"), lax.axis_index("c")


def _chip(k, x, y):
    return (1 - x if k & 1 else x, 1 - y if k & 2 else y)


def _block_rows(ref, rows, dev):
    start = pl.multiple_of((4 * dev[0] + 2 * dev[1] + dev[2]) * rows, 8)
    return ref.at[pl.ds(start, rows), :]


_HBM_SPEC = pl.BlockSpec(memory_space=pltpu.HBM)
_SEM_SPEC = pl.BlockSpec(memory_space=pltpu.SEMAPHORE)
_SPLIT_PARAMS = dict(compiler_params=pltpu.CompilerParams(has_side_effects=pltpu.SideEffectType.DATAFLOW_SIDE_EFFECTING))


def _descriptors(copies, refs, send_sems, recv_sems):
    return [pltpu.make_async_remote_copy(src_ref=s, dst_ref=d, send_sem=send_sems.at[i], recv_sem=recv_sems.at[i], device_id=to, device_id_type=MESH)
            for i, (s, d, to) in enumerate(copies(refs))]


def _start(copies, arrays, fresh, n_copies, *, name):
    operands = [pltpu.with_memory_space_constraint(v, pltpu.HBM) for v in (*arrays, *[lax.empty(f.shape, f.dtype) for f in fresh])]
    n = len(operands)

    def body(*refs):
        for dma in _descriptors(copies, refs[:n], refs[n], refs[n + 1]):
            dma.start()
        refs[-1][...] = jnp.zeros_like(refs[-1])

    res = pl.pallas_call(
        body, name=name, in_specs=[_HBM_SPEC] * n,
        out_shape=(pltpu.SemaphoreType.DMA((n_copies,)), pltpu.SemaphoreType.DMA((n_copies,)),
                   *[pltpu.HBM(v.shape, v.dtype) for v in operands], jax.ShapeDtypeStruct((8, 128), F32)),
        out_specs=(_SEM_SPEC, _SEM_SPEC, *[_HBM_SPEC] * n, pl.BlockSpec(memory_space=pltpu.VMEM)),
        input_output_aliases={i: 2 + i for i in range(n)}, **_SPLIT_PARAMS)(*operands)
    return res[0], res[1], list(res[2:2 + n]), res[-1]


def _finish(copies, send_sems, recv_sems, arrays, after, *, name):
    n = len(arrays)

    def body(*refs):
        for dma in _descriptors(copies, refs[:n], refs[n], refs[n + 1]):
            dma.wait_send()
            dma.wait_recv()

    return list(pl.pallas_call(
        body, name=name, in_specs=[_HBM_SPEC] * n + [_SEM_SPEC, _SEM_SPEC] + [HBM] * len(after),
        out_shape=tuple(pltpu.HBM(v.shape, v.dtype) for v in arrays), out_specs=tuple([_HBM_SPEC] * n),
        input_output_aliases={i: i for i in range(n)}, **_SPLIT_PARAMS)(*arrays, send_sems, recv_sems, *after))


def _tie(value, tokens):
    if not tokens:
        return value

    def body(*refs):
        pass

    return pl.pallas_call(body, name="tie", in_specs=[HBM] * (1 + len(tokens)), out_specs=HBM,
                          out_shape=jax.ShapeDtypeStruct(value.shape, value.dtype), input_output_aliases={0: 0})(value, *tokens)


def _gather_copies(stage):
    def copies(refs):
        x, y, c = _place()
        me, sibling = (x, y, c), (x, y, 1 - c)
        out = []
        for ref in refs:
            r = ref.shape[0] // N_DEV
            if stage == "a":
                hops = [(me, sibling)] + [(me, (*_chip(k, x, y), c)) for k in (1, 2, 3)]
            else:
                hops = [((*_chip(k, x, y), c), sibling) for k in (1, 2, 3)]
            for block, to in hops:
                rows = _block_rows(ref, r, block)
                out.append((rows, rows, to))
        return out
    return copies


def _scatter_copies(stage, n):
    def copies(refs):
        x, y, c = _place()
        out = []
        for w in range(n):
            src, dst = refs[w], refs[n + w]
            if stage == "a":
                r = src.shape[0] // N_DEV
                out += [(_block_rows(src, r, (*_chip(k, x, y), 1 - c)), dst.at[k], (x, y, 1 - c)) for k in range(4)]
            else:
                out += [(src.at[k - 1], dst.at[k - 1], (*_chip(k, x, y), c)) for k in (1, 2, 3)]
        return out
    return copies


def _place_own(shards, layer, blocks, *, name, dtype=BF16):
    _, r, cdim = shards.shape

    def body(blocks_ref, s_ref, o_ref):
        o_ref[...] = s_ref[...].astype(dtype)

    return pl.pallas_call(
        body, name=name,
        grid_spec=pltpu.PrefetchScalarGridSpec(
            num_scalar_prefetch=1, grid=(1,), in_specs=[pl.BlockSpec((None, r, cdim), lambda i, blocks: (layer, 0, 0))],
            out_specs=pl.BlockSpec((r, cdim), lambda i, blocks: (blocks[0], 0))),
        out_shape=jax.ShapeDtypeStruct((N_DEV * r, cdim), dtype), compiler_params=_params("arbitrary"))(blocks, shards)


def _chip_sum(grad, got, blocks, *, name):
    _, r, cdim = got.shape
    tr = r if r <= 256 else r // (2 if r % 32 == 0 and r // 2 <= 256 else 4)
    steps = r // tr

    def body(blocks_ref, a0_ref, a1_ref, a2_ref, a3_ref, b_ref, mine_ref, away_ref):
        mine_ref[...] = a0_ref[...] + b_ref[0]
        for k, a_ref in ((1, a1_ref), (2, a2_ref), (3, a3_ref)):
            away_ref[k - 1] = (a_ref[...] + b_ref[k]).astype(BF16)

    own = [pl.BlockSpec((tr, cdim), lambda i, blocks, k=k: (blocks[k] * steps + i, 0)) for k in range(4)]
    return pl.pallas_call(
        body, name=name,
        grid_spec=pltpu.PrefetchScalarGridSpec(
            num_scalar_prefetch=1, grid=(steps,),
            in_specs=own + [pl.BlockSpec((4, tr, cdim), lambda i, blocks: (0, i, 0))],
            out_specs=[pl.BlockSpec((tr, cdim), lambda i, blocks: (i, 0)), pl.BlockSpec((3, tr, cdim), lambda i, blocks: (0, i, 0))]),
        out_shape=[jax.ShapeDtypeStruct((r, cdim), F32), jax.ShapeDtypeStruct((3, r, cdim), BF16)],
        compiler_params=_params("arbitrary"))(blocks, grad, grad, grad, grad, got)


def _final_sum(mine, got, *, name):
    r, cdim = mine.shape
    tr = r if r <= 256 else r // (2 if r % 32 == 0 and r // 2 <= 256 else 4)

    def body(m_ref, g_ref, o_ref):
        o_ref[...] = ((m_ref[...] + g_ref[0].astype(F32)) + g_ref[1].astype(F32)) + g_ref[2].astype(F32)

    return pl.pallas_call(
        body, name=name, grid=(r // tr,),
        in_specs=[pl.BlockSpec((tr, cdim), lambda i: (i, 0)), pl.BlockSpec((3, tr, cdim), lambda i: (0, i, 0))],
        out_specs=pl.BlockSpec((tr, cdim), lambda i: (i, 0)), out_shape=jax.ShapeDtypeStruct((r, cdim), F32),
        compiler_params=_params("parallel"))(mine, got)


def _adamw_math(w, g, m, v):
    m = ADAM_B1 * m + (1.0 - ADAM_B1) * g
    v = ADAM_B2 * v + (1.0 - ADAM_B2) * (g * g)
    m_hat = m / (1.0 - ADAM_B1 ** ADAM_STEP)
    v_hat = v / (1.0 - ADAM_B2 ** ADAM_STEP)
    return -ADAM_LR * (m_hat / (jnp.sqrt(v_hat) + ADAM_EPS) + ADAM_WD * w), m, v


def _adamw(w, g, m, v, *, name):
    L, r, cdim = w.shape
    tr = r if r <= 512 else r // 2
    assert r % tr == 0 and tr % 8 == 0

    def body(w_ref, g_ref, m_ref, v_ref, d_ref, nm_ref, nv_ref):
        d_ref[...], nm_ref[...], nv_ref[...] = _adamw_math(w_ref[...], g_ref[...], m_ref[...], v_ref[...])

    blk = pl.BlockSpec((None, tr, cdim), lambda l, i: (l, i, 0))
    return pl.pallas_call(body, name=name, grid=(L, r // tr), in_specs=[blk] * 4, out_specs=[blk] * 3,
                          out_shape=[jax.ShapeDtypeStruct(w.shape, F32)] * 3, compiler_params=_params("parallel", "parallel"))(w, g, m, v)


def _small_update(parts, w, m, v, *, name):
    _, p, lanes = parts.shape
    tp = p // 2 if p % 16 == 0 else p

    def body(p_ref, w_ref, m_ref, v_ref, g_ref, d_ref, nm_ref, nv_ref):
        g = p_ref[0]
        for i in range(1, N_DEV):
            g = g + p_ref[i]
        g_ref[...] = g
        d_ref[...], nm_ref[...], nv_ref[...] = _adamw_math(w_ref[...], g, m_ref[...], v_ref[...])

    blk = pl.BlockSpec((tp, lanes), lambda i: (i, 0))
    return pl.pallas_call(body, name=name, grid=(p // tp,), in_specs=[pl.BlockSpec((N_DEV, tp, lanes), lambda i: (0, i, 0)), blk, blk, blk],
                          out_specs=[blk] * 4, out_shape=[jax.ShapeDtypeStruct((p, lanes), F32)] * 4,
                          compiler_params=_params("parallel"))(parts, w, m, v)


BIG = ("w_in", "w_o", "w_xq", "w_xkv", "w_xo", "w_gate_up", "w_down")
TRANSPOSED = ("w_in", "w_xkv", "w_gate_up")
SMALL = ("mem_norm_g", "mix_pre_g", "mix_post_g", "gm_v_g", "gm_w_s", "gm_b_s", "pool_w", "pool_scale", "attn_sinks",
         "x_pre_g", "x_post_g", "ffn_pre_g", "ffn_post_g")
WEIGHTS = ("mem_norm_g", "mix_pre_g", "mix_post_g", "w_in", "gm_v_g", "gm_w_s", "gm_b_s", "pool_w", "pool_scale", "attn_sinks", "w_o",
           "x_pre_g", "x_post_g", "w_xq", "w_xkv", "w_xo", "ffn_pre_g", "ffn_post_g", "w_gate_up", "w_down")
PACK_QUANTUM = 8 * 128


def _pack(arrays):
    flat = []
    for a in arrays:
        a = a.reshape(-1).astype(F32)
        flat.append(jnp.pad(a, (0, -a.size % PACK_QUANTUM)))
    return jnp.concatenate(flat).reshape(-1, 128)


def _unpack(pack, shapes):
    out, row = [], 0
    for s in shapes:
        size = math.prod(s)
        rows = (size + (-size % PACK_QUANTUM)) // 128
        out.append(pack[row:row + rows].reshape(-1)[:size].reshape(s))
        row += rows
    return out


def _row(v):
    return v.reshape(1, -1)


def _local_step(x, mem, positions, target, small, weight, fwd_hook=None, bwd_hook=None):
    fwd_hook = fwd_hook or (lambda l, point, v: v)
    bwd_hook = bwd_hook or (lambda l, point, v, gb: v)
    depth = small["mix_pre_g"].shape[0]
    half = HEAD_DIM // 2
    inv = ROPE_THETA ** (-jnp.arange(half, dtype=F32) / half)
    cos_t, sin_t = _rope_tables(positions.reshape(-1, 1), jnp.tile(inv, 128 // half).reshape(1, 128), name="rope_tables")
    mem_g = _row(small["mem_norm_g"])
    saved = []
    for l in range(depth):
        w = functools.partial(weight, l)
        bfull = jnp.repeat(small["gm_b_s"][l].T, HEAD_DIM, axis=1)
        bd = jax.scipy.linalg.block_diag(*[small["pool_w"][l, g] for g in range(len(POOL_WINDOWS))])
        mix_par = (_row(small["gm_v_g"][l]), small["gm_w_s"][l], bfull, bd, _row(small["pool_scale"][l]), small["attn_sinks"][l])
        if l == 0:
            z, h1 = _mm_fwd(fwd_hook(l, 0, x), w("w_in"), nt=True, pre="norm", g_pre=_row(small["mix_pre_g"][l]), name="first_in_proj")
        else:
            h1 = fwd_hook(l, 0, h1)
            z, = _mm_fwd(h1, w("w_in"), nt=True, name="in_proj")
        cat = fwd_hook(l, 1, _mixer_fwd(z, cos_t, sin_t, *mix_par, name="mixer_fwd"))
        mix, x1, h2 = _mm_fwd(cat, w("w_o"), nt=False, post="norm_res", g_post=_row(small["mix_post_g"][l]), xres=x,
                              g_next=_row(small["x_pre_g"][l]), name="mix_out")
        qx, = _mm_fwd(h2, w("w_xq"), nt=False, name="xq_proj")
        kv, memn = _mm_fwd(mem, w("w_xkv"), nt=True, pre="norm", g_pre=mem_g, name="xkv_proj")
        ox = _xattn_fwd(qx, kv, name="xattn_fwd")
        xo, x2, h3 = _mm_fwd(ox, w("w_xo"), nt=False, post="norm_res", g_post=_row(small["x_post_g"][l]), xres=x1,
                             g_next=_row(small["ffn_pre_g"][l]), name="xattn_out")
        h3 = fwd_hook(l, 2, h3)
        gu, = _mm_fwd(h3, w("w_gate_up"), nt=True, name="ffn_in")
        last = l + 1 == depth
        f, x3, act, *h_next = _mm_fwd(gu, w("w_down"), nt=False, pre="swiglu", post="norm_res", g_post=_row(small["ffn_post_g"][l]), xres=x2,
                                      g_next=None if last else _row(small["mix_pre_g"][l + 1]), tile=256, name="last_ffn_out" if last else "ffn_out")
        saved.append((mix_par, x, z, h1, cat, mix, x1, qx, h2, kv, memn, ox, xo, x2, gu, h3, act, f))
        if last:
            x = fwd_hook(l, 3, x3)
        else:
            x, h1 = x3, fwd_hook(l, 3, h_next[0])

    gs = {n: [None] * depth for n in SMALL if n != "mem_norm_g"}
    gb = {n: [None] * depth for n in BIG}
    dmemn = [None] * depth
    loss, dx, df, gs["ffn_post_g"][depth - 1] = _loss_grad(x, target, saved[-1][-1], _row(small["ffn_post_g"][depth - 1]), name="loss")
    for l in reversed(range(depth)):
        w = functools.partial(weight, l)
        mix_par, x0, z, h1, cat, mix, x1, qx, h2, kv, memn, ox, xo, x2, gu, h3, act, f = saved[l]
        dgu = _ffn_out_bwd(df, w("w_down"), gu, name="ffn_out_bwd")
        gb["w_down"][l] = _wgrad(act, df, name="w_down_grad")
        gb["w_gate_up"][l] = _wgrad(dgu, h3, name="w_gate_up_grad")
        dx2, gs["ffn_pre_g"][l], dxo, gs["x_post_g"][l] = _mm_bwd_post(
            dgu, w("w_gate_up"), x2, _row(small["ffn_pre_g"][l]), dx, nt=False, below=(xo, _row(small["x_post_g"][l])), tile=256, name="ffn_in_bwd")
        dxo = bwd_hook(l, 1, dxo, gb)
        do, = _mm_fwd(dxo, w("w_xo"), nt=True, name="xattn_out_bwd")
        gb["w_xo"][l] = _wgrad(ox, dxo, name="w_xo_grad")
        dqx, dkv = _xattn_bwd(qx, kv, do, name="xattn_bwd")
        gb["w_xq"][l] = _wgrad(h2, dqx, name="w_xq_grad")
        dx1, gs["x_pre_g"][l], dmix, gs["mix_post_g"][l] = _mm_bwd_post(
            dqx, w("w_xq"), x1, _row(small["x_pre_g"][l]), dx2, nt=True, below=(mix, _row(small["mix_post_g"][l])), name="xq_proj_bwd")
        dmix = bwd_hook(l, 2, dmix, gb)
        gb["w_xkv"][l] = _wgrad(dkv, memn, name="w_xkv_grad")
        dmemn[l] = _mm_fwd(dkv, w("w_xkv"), nt=False, out_dtype=F32, name="xkv_proj_bwd")[0]
        dcat, = _mm_fwd(dmix, w("w_o"), nt=True, name="mix_out_bwd")
        gb["w_o"][l] = _wgrad(cat, dmix, name="w_o_grad")
        dz, dvg, dwc, dbf, dbd, dscale, dsink = _mixer_bwd(z, dcat, cos_t, sin_t, *mix_par, name="mixer_bwd")
        gs["gm_v_g"][l], gs["gm_w_s"][l], gs["pool_scale"][l], gs["attn_sinks"][l] = dvg, dwc, dscale, dsink[:, 0]
        gs["gm_b_s"][l] = dbf.reshape(BLOCK, -1, HEAD_DIM).sum(-1).T
        gs["pool_w"][l] = jnp.stack([dbd[HEAD_DIM * g:HEAD_DIM * (g + 1), HEAD_DIM * g:HEAD_DIM * (g + 1)] for g in range(len(POOL_WINDOWS))])
        gb["w_in"][l] = _wgrad(dz, h1, name="w_in_grad")
        if l == 0:
            dx, gs["mix_pre_g"][l] = _mm_bwd_post(dz, w("w_in"), x0, _row(small["mix_pre_g"][l]), dx1, nt=False, name="first_in_proj_bwd")
            dx = bwd_hook(l, 3, dx, gb)
        else:
            dx, gs["mix_pre_g"][l], df, gs["ffn_post_g"][l - 1] = _mm_bwd_post(
                dz, w("w_in"), x0, _row(small["mix_pre_g"][l]), dx1, nt=False, below=(saved[l - 1][-1], _row(small["ffn_post_g"][l - 1])),
                name="in_proj_bwd")
            df = bwd_hook(l, 3, df, gb)

    small_grads = {n: jnp.stack(v).reshape(small[n].shape) for n, v in gs.items()}
    small_grads["mem_norm_g"] = _norm_dgain(mem, jnp.stack(dmemn), name="mem_norm_grad").reshape(-1)
    return loss, dx, small_grads, gb


EARLY = ("w_in", "w_o", "w_xq", "w_xkv", "w_xo")
LATE = ("w_gate_up", "w_down")


def _step(a):
    depth = a["mix_pre_g"].shape[0]
    x, mem, target = a["x"][0], a["mem"][0], a["loss_target"][0]
    px, py, pc = _place()
    blocks = jnp.stack([4 * cx + 2 * cy + pc for cx, cy in (_chip(k, px, py) for k in range(4))]).astype(jnp.int32)
    flight = {}

    def laid_out(v, n):
        return v.transpose(0, 2, 1) if n in TRANSPOSED else v

    whole = {(l, n): _place_own(laid_out(a[n], n), l, blocks, name="place_" + n) for l in range(depth) for n in BIG}
    gather = {"first_in": [(0, "w_in")], "first_mid": [(0, n) for n in EARLY[1:]], "first_late": [(0, n) for n in LATE],
              "second_early": [(1, n) for n in EARLY], "second_late": [(1, n) for n in LATE]}
    gather.update({f"layer{l}": [(l, n) for n in BIG] for l in range(2, depth)})
    forward_plan = {(0, 0): [("a", "first_in"), ("a", "first_mid"), ("a", "first_late"), ("a", "second_early"), ("a", "second_late"),
                             ("mid", "first_in"), ("end", "first_in")],
                    (0, 1): [("mid", "first_mid"), ("end", "first_mid")], (0, 2): [("mid", "first_late"), ("end", "first_late")],
                    (0, 3): [("mid", "second_early")], (1, 0): [("end", "second_early"), ("a", "layer2")],
                    (1, 1): [("mid", "second_late")], (1, 2): [("end", "second_late"), ("mid", "layer2")], (1, 3): [("end", "layer2")]}
    for l in range(2, depth):
        forward_plan.update({(l, 0): [("a", f"layer{l + 1}")], (l, 2): [("mid", f"layer{l + 1}")], (l, 3): [("end", f"layer{l + 1}")]})
    if depth == 1:
        gather = {tag: keys for tag, keys in gather.items() if tag.startswith("first")}

    issued = []

    def gather_step(kind, tag, v):
        keys = gather[tag]
        if kind == "a":
            arrays = [whole[k] for k in keys]
            arrays[0] = _tie(arrays[0], issued[-1:])
            send, recv, arrays, token = _start(_gather_copies("a"), arrays, [], 4 * len(keys), name=f"gather_a_{tag}")
            flight[tag] = (send, recv, arrays)
            issued.append(token)
            return [token]
        send, recv, arrays = flight.pop(tag)
        if kind == "mid":
            arrays = _finish(_gather_copies("a"), send, recv, arrays, [v], name=f"gather_a_done_{tag}")
            send, recv, arrays, token = _start(_gather_copies("b"), arrays, [], 3 * len(keys), name=f"gather_b_{tag}")
            flight[tag] = (send, recv, arrays)
            return [token]
        for key, arr in zip(keys, _finish(_gather_copies("b"), send, recv, arrays, [v], name=f"gather_b_done_{tag}")):
            whole[key] = arr
        return []

    def fwd_hook(l, point, v):
        tokens = []
        for kind, tag in forward_plan.get((l, point), []):
            if tag in gather:
                tokens += gather_step(kind, tag, v)
        return _tie(v, tokens)

    mine = {}
    scatter = {f"layer{l}": [(l, n) for n in BIG] for l in range(1, depth)}
    scatter.update({"late0": [(0, n) for n in LATE], "attn0": [(0, "w_xq"), (0, "w_xo")], "rest0": [(0, "w_in"), (0, "w_o"), (0, "w_xkv")]})
    backward_plan = {}
    for l in range(1, depth):
        backward_plan[(l, 3)] = [("a", f"layer{l}")]
        backward_plan[(l - 1, 1)] = [("mid", f"layer{l}")]
        backward_plan[(l - 1, 2)] = [("end", f"layer{l}")]
    backward_plan[(0, 1)] = [("a", "late0")] + backward_plan.get((0, 1), [])
    backward_plan[(0, 2)] = [("a", "attn0"), ("mid", "late0")] + backward_plan.get((0, 2), [])
    backward_plan[(0, 3)] = [("end", "late0"), ("mid", "attn0"), ("a", "rest0")]

    def scatter_step(kind, tag, v, gb):
        keys = scatter[tag]
        n = len(keys)
        if kind == "a":
            grads = [gb[name][l] for l, name in keys]
            zones = [jax.ShapeDtypeStruct((4, g.shape[0] // N_DEV, g.shape[1]), F32) for g in grads]
            send, recv, arrays, token = _start(_scatter_copies("a", n), grads, zones, 4 * n, name=f"scatter_a_{tag}")
            flight[tag] = (send, recv, arrays, None)
            return [token]
        send, recv, arrays, kept = flight.pop(tag)
        if kind == "mid":
            arrays = _finish(_scatter_copies("a", n), send, recv, arrays, [v], name=f"scatter_a_done_{tag}")
            sums = [_chip_sum(arrays[i], arrays[n + i], blocks, name="chip_sum") for i in range(n)]
            zones = [jax.ShapeDtypeStruct(s[1].shape, BF16) for s in sums]
            send, recv, arrays, token = _start(_scatter_copies("b", n), [s[1] for s in sums], zones, 3 * n, name=f"scatter_b_{tag}")
            flight[tag] = (send, recv, arrays, [s[0] for s in sums])
            return [token]
        arrays = _finish(_scatter_copies("b", n), send, recv, arrays, [v], name=f"scatter_b_done_{tag}")
        for i, key in enumerate(keys):
            mine[key] = _final_sum(kept[i], arrays[n + i], name="final_sum")
        return []

    def bwd_hook(l, point, v, gb):
        tokens = []
        for kind, tag in backward_plan.get((l, point), []):
            tokens += scatter_step(kind, tag, v, gb)
        return _tie(v, tokens)

    loss, dx, small_grads, _ = _local_step(x, mem, a["positions"][0], target, {n: a[n] for n in SMALL}, lambda l, n: whole[(l, n)],
                                           fwd_hook, bwd_hook)

    out = {}

    def update(n, tokens):
        g = _tie(jnp.stack([mine[(l, n)] for l in range(depth)]), tokens)
        moved = _adamw(laid_out(a[n], n), g, laid_out(a["m_" + n], n), laid_out(a["v_" + n], n), name="adamw_" + n)
        for p, v in zip(("grad_", "delta_", "new_m_", "new_v_"), (g, *moved)):
            out[p + n] = laid_out(v, n)
        return moved[0]

    part = _tie(_pack([small_grads[n] for n in SMALL]), [dx])
    placed = _place_own(part[None], 0, blocks, dtype=F32, name="place_small_grads")
    send, recv, arrays, token = _start(_gather_copies("a"), [placed], [], 4, name="gather_a_small_grads")
    behind = [update(n, [token]) for n in LATE]
    tokens = scatter_step("mid", "rest0", behind[-1], None)
    arrays = _finish(_gather_copies("a"), send, recv, arrays, behind, name="gather_a_done_small_grads")
    send, recv, arrays, token = _start(_gather_copies("b"), arrays, [], 3, name="gather_b_small_grads")
    scatter_step("end", "attn0", behind[-1], None)
    behind = [update(n, tokens + [token]) for n in ("w_xq", "w_xo")]
    parts = _finish(_gather_copies("b"), send, recv, arrays, behind, name="gather_b_done_small_grads")[0].reshape(N_DEV, *part.shape)
    packs = _small_update(parts, *[_pack([a[p + n] for n in SMALL]) for p in ("", "m_", "v_")], name="small_update")
    for p, pack in zip(("grad_", "delta_", "new_m_", "new_v_"), packs):
        for n, v in zip(SMALL, _unpack(pack, [a[n].shape for n in SMALL])):
            out[p + n] = v
    scatter_step("end", "rest0", packs[0], None)
    for n in ("w_in", "w_o", "w_xkv"):
        update(n, [])

    total = lax.psum(loss[0, 0], ("x", "y", "c"))
    return (total, dx[None], *[out[p + n] for p in ("grad_", "delta_", "new_m_", "new_v_") for n in WEIGHTS])


def kernel(x, mem, positions, mem_norm_g, mix_pre_g, mix_post_g, w_in, gm_v_g, gm_w_s, gm_b_s, pool_w, pool_scale, attn_sinks,
           w_o, x_pre_g, x_post_g, w_xq, w_xkv, w_xo, ffn_pre_g, ffn_post_g, w_gate_up, w_down, loss_target, m_mem_norm_g,
           m_mix_pre_g, m_mix_post_g, m_w_in, m_gm_v_g, m_gm_w_s, m_gm_b_s, m_pool_w, m_pool_scale, m_attn_sinks, m_w_o,
           m_x_pre_g, m_x_post_g, m_w_xq, m_w_xkv, m_w_xo, m_ffn_pre_g, m_ffn_post_g, m_w_gate_up, m_w_down, v_mem_norm_g,
           v_mix_pre_g, v_mix_post_g, v_w_in, v_gm_v_g, v_gm_w_s, v_gm_b_s, v_pool_w, v_pool_scale, v_attn_sinks, v_w_o,
           v_x_pre_g, v_x_post_g, v_w_xq, v_w_xkv, v_w_xo, v_ffn_pre_g, v_ffn_post_g, v_w_gate_up, v_w_down):
    return _step(dict(locals()))
```

```python
import functools
import math

import jax
import jax.numpy as jnp
from jax import lax
from jax.experimental import pallas as pl
from jax.experimental.pallas import tpu as pltpu

F32, BF16 = jnp.float32, jnp.bfloat16
EPS = 1e-6
HEAD_DIM = 64
BLOCK = 128
POOL_WINDOWS = (2, 4, 8, 16)
ATT_Q_HEADS = 8
X_HEADS = 4
ROPE_THETA = 10000.0
ADAM_LR, ADAM_B1, ADAM_B2, ADAM_EPS, ADAM_WD, ADAM_STEP = 0.001, 0.9, 0.999, 1e-08, 0.01, 10
N_DEV = 8
TOKEN_TILE = 512
VMEM_LIMIT_BYTES = 50 * 2**20
NEG = -1e30
MESH = pl.DeviceIdType.MESH

NN = ((1,), (0,))
NT = ((1,), (1,))
TN = ((0,), (0,))


def _dot(a, b, dims):
    return lax.dot_general(a.astype(BF16), b.astype(BF16), (dims, ((), ())), preferred_element_type=F32)


def _params(*sem):
    return pltpu.CompilerParams(dimension_semantics=sem, vmem_limit_bytes=VMEM_LIMIT_BYTES)


def _full(shape):
    return pl.BlockSpec(shape, lambda *_: (0,) * len(shape))


def _lane(shape):
    return lax.broadcasted_iota(jnp.int32, shape, len(shape) - 1)


def _rms_fwd(x, g):
    r = lax.rsqrt(jnp.mean(x * x, axis=-1, keepdims=True) + EPS)
    return x * r * g


def _rms_bwd(x, g, dy):
    r = lax.rsqrt(jnp.mean(x * x, axis=-1, keepdims=True) + EPS)
    xh = x * r
    dg = jnp.sum(dy * xh, axis=0, keepdims=True)
    dxh = dy * g
    dx = r * (dxh - xh * jnp.mean(dxh * xh, axis=-1, keepdims=True))
    return dx, dg


def _silu_parts(gate):
    sg = 1.0 / (1.0 + jnp.exp(-gate))
    return gate * sg, sg


def _swiglu_cols(gu_ref, out_ref, width):
    step = 128 * max(1, (width // 128) // 4)
    for c0 in range(0, width, step):
        c1 = min(width, c0 + step)
        gate = gu_ref[:, c0:c1].astype(F32)
        up = gu_ref[:, width + c0:width + c1].astype(F32)
        out_ref[:, c0:c1] = (_silu_parts(gate)[0] * up).astype(out_ref.dtype)


def _col_tile(n, cap=1408):
    best = n
    for t in range(128, cap + 1, 128):
        if n % t == 0:
            best = t
    return best if n > cap else n


def _mm_fwd(a, w, *, nt, name, pre=None, g_pre=None, post=None, g_post=None, xres=None, g_next=None, out_dtype=BF16, tile=TOKEN_TILE):
    S = a.shape[0]
    n_out, k = (w.shape[0], w.shape[1]) if nt else (w.shape[1], w.shape[0])
    T = min(tile, S)
    tn = n_out if post else _col_tile(n_out)
    swap = pre is None and post is None and n_out > tn
    grid = (n_out // tn, S // T) if swap else (S // T, n_out // tn)

    def at(index_map):
        return (lambda p, q: index_map(q, p)) if swap else index_map

    def body(*refs):
        it = iter(refs)
        a_ref, w_ref = next(it), next(it)
        gpre_ref = next(it) if pre == "norm" else None
        gpost_ref, x_ref = (next(it), next(it)) if post else (None, None)
        gnext_ref = next(it) if g_next is not None else None
        outs = [next(it) for _ in range((2 if post else 1) + (1 if pre else 0))]
        hnext_ref = next(it) if g_next is not None else None
        as_ref = next(it) if pre else None
        if pre:
            @pl.when(pl.program_id(1) == 0)
            def _():
                if pre == "norm":
                    h = _rms_fwd(a_ref[...], gpre_ref[...]).astype(BF16)
                    outs[-1][...] = h
                    as_ref[...] = h
                else:
                    _swiglu_cols(a_ref, as_ref, k)
                    outs[-1][...] = as_ref[...]
            av = as_ref[...]
        else:
            av = a_ref[...]
        acc = _dot(av, w_ref[...], NT if nt else NN)
        if post:
            outs[0][...] = acc.astype(BF16)
            x_new = x_ref[...] + _rms_fwd(acc, gpost_ref[...])
            outs[1][...] = x_new
            if g_next is not None:
                hnext_ref[...] = _rms_fwd(x_new, gnext_ref[...]).astype(BF16)
        else:
            outs[0][...] = acc.astype(out_dtype)

    in_specs = [pl.BlockSpec((T, a.shape[1]), at(lambda i, j: (i, 0))),
                pl.BlockSpec((tn, k), at(lambda i, j: (j, 0))) if nt else pl.BlockSpec((k, tn), at(lambda i, j: (0, j)))]
    args = [a, w]
    if pre == "norm":
        in_specs.append(_full((1, k)))
        args.append(g_pre)
    if post:
        in_specs += [_full((1, n_out)), pl.BlockSpec((T, n_out), lambda i, j: (i, 0))]
        args += [g_post, xres]
    if g_next is not None:
        in_specs.append(_full((1, n_out)))
        args.append(g_next)
    out_block = pl.BlockSpec((T, tn), at(lambda i, j: (i, j)))
    if post:
        out_shape = [jax.ShapeDtypeStruct((S, n_out), BF16), jax.ShapeDtypeStruct((S, n_out), F32)]
        out_specs = [out_block, out_block]
    else:
        out_shape = [jax.ShapeDtypeStruct((S, n_out), out_dtype)]
        out_specs = [out_block]
    if pre:
        out_shape.append(jax.ShapeDtypeStruct((S, k), BF16))
        out_specs.append(pl.BlockSpec((T, k), lambda i, j: (i, 0)))
    if g_next is not None:
        out_shape.append(jax.ShapeDtypeStruct((S, n_out), BF16))
        out_specs.append(out_block)
    scratch = [pltpu.VMEM((T, k), BF16)] if pre else []
    return pl.pallas_call(body, name=name, grid=grid, in_specs=in_specs, out_specs=out_specs, out_shape=out_shape,
                          scratch_shapes=scratch, compiler_params=_params("parallel", "arbitrary"))(*args)


def _mm_bwd_post(dy, w, x, g, dx_in, *, nt, name, below=None, tile=TOKEN_TILE):
    S, k = dy.shape
    n = x.shape[1]
    T = min(tile, S)

    def body(*refs):
        dy_ref, w_ref, x_ref, g_ref, dxi_ref = refs[:5]
        m_ref, gb_ref = refs[5:7] if below else (None, None)
        dxo_ref, dg_ref = refs[-4:-2] if below else refs[-2:]

        @pl.when(pl.program_id(0) == 0)
        def _():
            dg_ref[...] = jnp.zeros_like(dg_ref)
            if below:
                refs[-1][...] = jnp.zeros_like(refs[-1])

        dh = _dot(dy_ref[...], w_ref[...], NT if nt else NN)
        dxn, dg = _rms_bwd(x_ref[...], g_ref[...], dh)
        dg_ref[...] += dg
        dxo = dxi_ref[...] + dxn
        dxo_ref[...] = dxo
        if below:
            dmv, dgb = _rms_bwd(m_ref[...].astype(F32), gb_ref[...], dxo)
            refs[-2][...] = dmv.astype(BF16)
            refs[-1][...] += dgb

    row = pl.BlockSpec((T, n), lambda i: (i, 0))
    in_specs = [pl.BlockSpec((T, k), lambda i: (i, 0)), _full(w.shape), row, _full((1, n)), row]
    out_specs = [row, _full((1, n))]
    out_shape = [jax.ShapeDtypeStruct((S, n), F32), jax.ShapeDtypeStruct((1, n), F32)]
    if below:
        in_specs += [row, _full((1, n))]
        out_specs += [row, _full((1, n))]
        out_shape += [jax.ShapeDtypeStruct((S, n), BF16), jax.ShapeDtypeStruct((1, n), F32)]
    return pl.pallas_call(body, name=name, grid=(S // T,), in_specs=in_specs, out_specs=out_specs, out_shape=out_shape,
                          compiler_params=_params("arbitrary"))(dy, w, x, g, dx_in, *(below or ()))


def _wgrad(a, b, *, name):
    S, k = a.shape
    n = b.shape[1]
    tk = 512 if k % 512 == 0 and k > 1536 else 256

    def body(a_ref, b_ref, o_ref):
        o_ref[...] = _dot(a_ref[...], b_ref[...], TN)

    return pl.pallas_call(body, name=name, grid=(k // tk,), in_specs=[pl.BlockSpec((S, tk), lambda kk: (0, kk)), _full((S, n))],
                          out_specs=pl.BlockSpec((tk, n), lambda kk: (kk, 0)),
                          out_shape=jax.ShapeDtypeStruct((k, n), F32), compiler_params=_params("parallel"))(a, b)


def _ffn_out_bwd(dm, w, gu, *, name, tile=256):
    S, n = dm.shape
    f = w.shape[0]
    T = min(tile, S)

    def body(dm_ref, w_ref, gu_ref, dgu_ref, da_ref):
        da_ref[...] = _dot(dm_ref[...], w_ref[...], NT)
        step = 128 * max(1, (f // 128) // 4)
        for c0 in range(0, f, step):
            c1 = min(f, c0 + step)
            gate = gu_ref[:, c0:c1].astype(F32)
            up = gu_ref[:, f + c0:f + c1].astype(F32)
            da = da_ref[:, c0:c1]
            sl, sg = _silu_parts(gate)
            dgu_ref[:, c0:c1] = (da * up * (sg + sl * (1.0 - sg))).astype(BF16)
            dgu_ref[:, f + c0:f + c1] = (da * sl).astype(BF16)

    row = pl.BlockSpec((T, n), lambda i: (i, 0))
    wide = pl.BlockSpec((T, 2 * f), lambda i: (i, 0))
    return pl.pallas_call(
        body, name=name, grid=(S // T,), in_specs=[row, _full((f, n)), wide], out_specs=wide,
        out_shape=jax.ShapeDtypeStruct((S, 2 * f), BF16),
        scratch_shapes=[pltpu.VMEM((T, f), F32)], compiler_params=_params("parallel"))(dm, w, gu)


def _loss_grad(y, target, m, g, *, name, tile=TOKEN_TILE):
    S, d = y.shape
    T = min(tile, S)
    steps = S // T

    def body(y_ref, t_ref, m_ref, g_ref, loss_ref, dy_ref, dm_ref, dg_ref, acc_ref):
        i = pl.program_id(0)

        @pl.when(i == 0)
        def _():
            acc_ref[...] = jnp.zeros_like(acc_ref)
            dg_ref[...] = jnp.zeros_like(dg_ref)

        diff = y_ref[...] - t_ref[...]
        dy = diff * (1.0 / d)
        dy_ref[...] = dy
        dmv, dg = _rms_bwd(m_ref[...].astype(F32), g_ref[...], dy)
        dm_ref[...] = dmv.astype(BF16)
        dg_ref[...] += dg
        acc_ref[...] += jnp.sum(diff * diff, axis=0, keepdims=True)

        @pl.when(i == steps - 1)
        def _():
            total = jnp.sum(acc_ref[...], axis=1, keepdims=True) * (0.5 / d)
            loss_ref[...] = jnp.broadcast_to(total, loss_ref.shape)

    row = pl.BlockSpec((T, d), lambda i: (i, 0))
    return pl.pallas_call(
        body, name=name, grid=(steps,), in_specs=[row, row, row, _full((1, d))], out_specs=[_full((1, 128)), row, row, _full((1, d))],
        out_shape=[jax.ShapeDtypeStruct((1, 128), F32), jax.ShapeDtypeStruct((S, d), F32), jax.ShapeDtypeStruct((S, d), BF16),
                   jax.ShapeDtypeStruct((1, d), F32)],
        scratch_shapes=[pltpu.VMEM((1, d), F32)], compiler_params=_params("arbitrary"))(y, target, m, g)


def _norm_dgain(x, dys, *, name):
    def body(x_ref, dy_ref, o_ref):
        xv = x_ref[...]
        dy = dy_ref[0]
        for l in range(1, dys.shape[0]):
            dy = dy + dy_ref[l]
        r = lax.rsqrt(jnp.mean(xv * xv, axis=-1, keepdims=True) + EPS)
        o_ref[...] = jnp.sum(dy * xv * r, axis=0, keepdims=True)

    return pl.pallas_call(body, name=name, out_shape=jax.ShapeDtypeStruct((1, x.shape[1]), F32),
                          compiler_params=pltpu.CompilerParams(vmem_limit_bytes=VMEM_LIMIT_BYTES))(x, dys)


def _xattn_probs(q_ref, kv_ref, h, d, hd):
    q = q_ref[:, h * hd:(h + 1) * hd]
    kh = kv_ref[:, h * hd:(h + 1) * hd]
    vh = kv_ref[:, d + h * hd:d + (h + 1) * hd]
    s = _dot(q, kh, NT) * (hd ** -0.5)
    e = jnp.exp(s - jnp.max(s, axis=-1, keepdims=True))
    return q, kh, vh, e / jnp.sum(e, axis=-1, keepdims=True)


def _xattn_fwd(q, kv, *, name, tile=TOKEN_TILE):
    S, d = q.shape
    mlen = kv.shape[0]
    hd = d // X_HEADS
    T = min(tile, S)

    def body(q_ref, kv_ref, o_ref):
        for h in range(X_HEADS):
            _, _, vh, p = _xattn_probs(q_ref, kv_ref, h, d, hd)
            o_ref[:, h * hd:(h + 1) * hd] = _dot(p, vh, NN).astype(BF16)

    row = pl.BlockSpec((T, d), lambda i: (i, 0))
    return pl.pallas_call(body, name=name, grid=(S // T,), in_specs=[row, _full((mlen, 2 * d))], out_specs=row,
                          out_shape=jax.ShapeDtypeStruct((S, d), BF16), compiler_params=_params("parallel"))(q, kv)


def _xattn_bwd(q, kv, do, *, name, tile=TOKEN_TILE):
    S, d = q.shape
    mlen = kv.shape[0]
    hd = d // X_HEADS
    T = min(tile, S)

    def body(q_ref, kv_ref, do_ref, dq_ref, dkv_ref):
        @pl.when(pl.program_id(0) == 0)
        def _():
            dkv_ref[...] = jnp.zeros_like(dkv_ref)

        for h in range(X_HEADS):
            cols = slice(h * hd, (h + 1) * hd)
            qh, kh, vh, doh = q_ref[:, cols], kv_ref[:, cols], kv_ref[:, d + h * hd:d + (h + 1) * hd], do_ref[:, cols]
            st = _dot(kh, qh, NT) * (hd ** -0.5)
            e = jnp.exp(st - jnp.max(st, axis=0, keepdims=True))
            pt = e / jnp.sum(e, axis=0, keepdims=True)
            dpt = _dot(vh, doh, NT)
            dst = (pt * (dpt - jnp.sum(pt * dpt, axis=0, keepdims=True)) * (hd ** -0.5)).astype(BF16)
            dkv_ref[:, cols] += _dot(dst, qh, NN)
            dkv_ref[:, d + h * hd:d + (h + 1) * hd] += _dot(pt, doh, NN)
            dq_ref[:, cols] = _dot(dst, kh, TN).astype(BF16)

    row = pl.BlockSpec((T, d), lambda i: (i, 0))
    return pl.pallas_call(
        body, name=name, grid=(S // T,), in_specs=[row, _full((mlen, 2 * d)), row],
        out_specs=[row, _full((mlen, 2 * d))],
        out_shape=[jax.ShapeDtypeStruct((S, d), BF16), jax.ShapeDtypeStruct((mlen, 2 * d), F32)],
        compiler_params=_params("arbitrary"))(q, kv, do)


GM0, POOL0, Q0, KV0, Z_END = 0, 512, 768, 1280, 1536


def _rope_tables(pos_col, inv_row, *, name):
    S = pos_col.shape[0]

    def body(p_ref, inv_ref, cos_ref, sin_ref):
        ang = p_ref[...].astype(F32) * inv_ref[...]
        sin = jnp.sin(ang)
        cos_ref[...] = jnp.cos(ang)
        sin_ref[...] = jnp.where(_lane(ang.shape) % HEAD_DIM < HEAD_DIM // 2, -sin, sin)

    T = min(TOKEN_TILE, S)
    return pl.pallas_call(
        body, name=name, grid=(S // T,),
        in_specs=[pl.BlockSpec((T, 1), lambda i: (i, 0)), _full((1, 128))],
        out_specs=[pl.BlockSpec((T, 128), lambda i: (i, 0))] * 2,
        out_shape=[jax.ShapeDtypeStruct((S, 128), F32)] * 2, compiler_params=_params("parallel"))(pos_col, inv_row)


def _swap_halves(x):
    n = x.shape[-1]
    return jnp.where(_lane(x.shape) % HEAD_DIM < HEAD_DIM // 2, pltpu.roll(x, n - HEAD_DIM // 2, 1), pltpu.roll(x, HEAD_DIM // 2, 1))


def _rope(x, cos, sin_s):
    reps = x.shape[-1] // 128
    if reps > 1:
        cos, sin_s = jnp.tile(cos, (1, reps)), jnp.tile(sin_s, (1, reps))
    return x * cos + _swap_halves(x) * sin_s


def _rope_bwd(dy, cos, sin_s):
    reps = dy.shape[-1] // 128
    if reps > 1:
        cos, sin_s = jnp.tile(cos, (1, reps)), jnp.tile(sin_s, (1, reps))
    return dy * cos + _swap_halves(dy * sin_s)


def _gelu_parts(x):
    c = math.sqrt(2.0 / math.pi)
    t = jnp.tanh(c * (x + 0.044715 * x * x * x))
    return 0.5 * x * (1.0 + t), t


def _gelu_grad(x, t):
    c = math.sqrt(2.0 / math.pi)
    return 0.5 * (1.0 + t) + 0.5 * x * (1.0 - t * t) * c * (1.0 + 3.0 * 0.044715 * x * x)


def _group_sum(x):
    gid = _lane(x.shape) // HEAD_DIM
    out = jnp.zeros_like(x)
    for g in range(x.shape[-1] // HEAD_DIM):
        sel = gid == g
        out = jnp.where(sel, jnp.sum(jnp.where(sel, x, 0.0), axis=-1, keepdims=True), out)
    return out


def _low_half(shape):
    return _lane(shape) % 128 < HEAD_DIM


def _tril(w):
    r = lax.broadcasted_iota(jnp.int32, w.shape, 0)
    c = lax.broadcasted_iota(jnp.int32, w.shape, 1)
    return jnp.where(r >= c, w, 0.0)


def _triu(w):
    r = lax.broadcasted_iota(jnp.int32, w.shape, 0)
    c = lax.broadcasted_iota(jnp.int32, w.shape, 1)
    return jnp.where(r <= c, w, 0.0)


def _gating_fwd(zg, vg, wc_ref, bfull):
    ge, th = _gelu_parts(zg)
    u, v = ge[:, :256], ge[:, 256:]
    r = lax.rsqrt(_group_sum(v * v) * (1.0 / HEAD_DIM) + EPS)
    vhat = v * r
    vn = vhat * vg
    low = _low_half((BLOCK, 128))
    blocks = []
    for blk in range(2):
        vb = vn[:, 128 * blk:128 * (blk + 1)]
        m0 = _dot(_tril(wc_ref[2 * blk]), vb, NN)
        m1 = _dot(_tril(wc_ref[2 * blk + 1]), vb, NN)
        blocks.append(jnp.where(low, m0, m1))
    mixed = jnp.concatenate(blocks, axis=1) + bfull
    return u * mixed, (th, u, r, vhat, vn, mixed)


def _band(w, transpose):
    if transpose:
        s2 = lax.broadcasted_iota(jnp.int32, (2 * BLOCK, BLOCK), 0)
        t = lax.broadcasted_iota(jnp.int32, (2 * BLOCK, BLOCK), 1)
    else:
        t = lax.broadcasted_iota(jnp.int32, (BLOCK, 2 * BLOCK), 0)
        s2 = lax.broadcasted_iota(jnp.int32, (BLOCK, 2 * BLOCK), 1)
    d = t - s2 + BLOCK
    return jnp.where((d >= 0) & (d < w), 1.0, 0.0).astype(BF16)


def _inv_count(n):
    t = lax.broadcasted_iota(jnp.int32, (BLOCK, 256), 0) + n * BLOCK
    gid = _lane((BLOCK, 256)) // HEAD_DIM
    win = jnp.where(gid == 0, POOL_WINDOWS[0], jnp.where(gid == 1, POOL_WINDOWS[1], jnp.where(gid == 2, POOL_WINDOWS[2], POOL_WINDOWS[3])))
    return 1.0 / jnp.minimum(t + 1, win).astype(F32)


def _pool_fwd(p, pp, n, bd, scale):
    low = _low_half((BLOCK, 128))
    blocks = []
    for blk in range(2):
        both = jnp.concatenate([pp[:, 128 * blk:128 * (blk + 1)], p[:, 128 * blk:128 * (blk + 1)]], axis=0)
        s0 = _dot(_band(POOL_WINDOWS[2 * blk], False), both, NN)
        s1 = _dot(_band(POOL_WINDOWS[2 * blk + 1], False), both, NN)
        blocks.append(jnp.where(low, s0, s1))
    inv = _inv_count(n)
    pooled = jnp.concatenate(blocks, axis=1) * inv - p
    mapped = _dot(pooled, bd, NN)
    return mapped * scale, (inv, pooled, mapped)


def _mixer_fwd(z, cos_t, sin_t, vg, wc, bfull, bd, scale, sinks, *, name):
    S = z.shape[0]
    nb = S // BLOCK

    def body(zc_ref, zpp_ref, zpk_ref, cc_ref, cp_ref, sc_ref, sp_ref, vg_ref, wc_ref, bf_ref, bd_ref, scale_ref, sinks_ref, out_ref, pen_ref):
        n = pl.program_id(0)

        @pl.when(n == 0)
        def _():
            tall = (ATT_Q_HEADS // 2 * BLOCK, 2 * BLOCK)
            qi = lax.broadcasted_iota(jnp.int32, tall, 0) % BLOCK
            ki = lax.broadcasted_iota(jnp.int32, tall, 1)
            inside = (ki > qi) & (ki <= qi + BLOCK)
            pen_ref[0] = jnp.where(inside, 0.0, NEG)
            pen_ref[1] = jnp.where(inside & (ki >= BLOCK), 0.0, NEG)

        keep = (n > 0).astype(F32)
        a, _ = _gating_fwd(zc_ref[:, GM0:POOL0].astype(F32), vg_ref[...], wc_ref, bf_ref[...])
        out_ref[:, 0:256] = a.astype(BF16)
        b, _ = _pool_fwd(zc_ref[:, POOL0:Q0].astype(F32), zpp_ref[...].astype(F32) * keep, n, bd_ref[...], scale_ref[...])
        out_ref[:, 256:512] = b.astype(BF16)
        cos_c, sin_c = cc_ref[...], sc_ref[...]
        qrot = _rope(zc_ref[:, Q0:KV0].astype(F32), cos_c, sin_c)
        kk = jnp.concatenate([_rope(zpk_ref[:, 0:128].astype(F32), cp_ref[...], sp_ref[...]), _rope(zc_ref[:, KV0:KV0 + 128].astype(F32), cos_c, sin_c)], axis=0)
        vv = jnp.concatenate([zpk_ref[:, 128:256].astype(F32), zc_ref[:, KV0 + 128:Z_END].astype(F32)], axis=0)
        kk, vv = kk.astype(BF16), vv.astype(BF16)
        pen = pen_ref[jnp.where(n > 0, 0, 1)]
        low = _low_half((BLOCK, 128))
        q_slabs = [qrot[:, 128 * s:128 * (s + 1)] for s in range(ATT_Q_HEADS // 2)]
        q_rolled = [pltpu.roll(v, HEAD_DIM, 1) for v in q_slabs]
        o_heads = [None] * ATT_Q_HEADS
        for h in range(2):
            lanes = ~low if h else low
            heads = range(4 * h, 4 * h + 4)
            qm = jnp.concatenate([jnp.where(lanes, (q_slabs if j % 2 == h else q_rolled)[j // 2], 0.0) for j in heads], axis=0)
            sc = _dot(qm, kk, NT) * (HEAD_DIM ** -0.5) + pen
            p = []
            for i, j in enumerate(heads):
                for r0 in (BLOCK * i, BLOCK * i + BLOCK // 2):
                    sch, sink = sc[r0:r0 + BLOCK // 2], sinks_ref[j]
                    mx = jnp.maximum(jnp.max(sch, axis=-1, keepdims=True), sink)
                    e = jnp.exp(sch - mx)
                    p.append((e * (1.0 / (jnp.sum(e, axis=-1, keepdims=True) + jnp.exp(sink - mx)))).astype(BF16))
            o = _dot(jnp.concatenate(p, axis=0), vv, NN)
            for i, j in enumerate(heads):
                oh = o[BLOCK * i:BLOCK * (i + 1)]
                o_heads[j] = oh if j % 2 == h else pltpu.roll(oh, HEAD_DIM, 1)
        for s in range(ATT_Q_HEADS // 2):
            out_ref[:, 512 + 128 * s:512 + 128 * (s + 1)] = jnp.where(low, o_heads[2 * s], o_heads[2 * s + 1]).astype(BF16)

    cur = lambda n: (n, 0)
    prev = lambda n: (jnp.maximum(n - 1, 0), 0)
    in_specs = [
        pl.BlockSpec((BLOCK, Z_END), cur),
        pl.BlockSpec((BLOCK, 256), lambda n: (jnp.maximum(n - 1, 0), POOL0 // 256)),
        pl.BlockSpec((BLOCK, 256), lambda n: (jnp.maximum(n - 1, 0), KV0 // 256)),
        pl.BlockSpec((BLOCK, 128), cur), pl.BlockSpec((BLOCK, 128), prev),
        pl.BlockSpec((BLOCK, 128), cur), pl.BlockSpec((BLOCK, 128), prev),
        _full((1, 256)), _full((4, BLOCK, BLOCK)), _full((BLOCK, 256)), _full((256, 256)), _full((1, 256)),
        pl.BlockSpec(memory_space=pltpu.SMEM),
    ]
    return pl.pallas_call(
        body, name=name, grid=(nb,), in_specs=in_specs, out_specs=pl.BlockSpec((BLOCK, 1024), cur),
        out_shape=jax.ShapeDtypeStruct((S, 1024), BF16), scratch_shapes=[pltpu.VMEM((2, ATT_Q_HEADS // 2 * BLOCK, 2 * BLOCK), F32)],
        compiler_params=_params("arbitrary"))(z, z, z, cos_t, cos_t, sin_t, sin_t, vg, wc, bfull, bd, scale, sinks)


def _mixer_bwd(z, dcat, cos_t, sin_t, vg, wc, bfull, bd, scale, sinks, *, name):
    S = z.shape[0]
    nb = S // BLOCK

    def body(zc_ref, zpp_ref, zpk_ref, dc_ref, cc_ref, cp_ref, sc_ref, sp_ref, vg_ref, wc_ref, bf_ref, bd_ref, scale_ref, sinks_ref, wct_ref,
             dz_ref, dvg_ref, dwc_ref, dbf_ref, dbd_ref, dscale_ref, dsink_ref, carry_ref, pen_ref):
        n = pl.program_id(0)

        @pl.when(n == 0)
        def _():
            for ref in (dvg_ref, dwc_ref, dbf_ref, dbd_ref, dscale_ref, dsink_ref):
                ref[...] = jnp.zeros_like(ref)
            wide = (2 * BLOCK, ATT_Q_HEADS // 2 * BLOCK)
            ki = lax.broadcasted_iota(jnp.int32, wide, 0)
            qi = lax.broadcasted_iota(jnp.int32, wide, 1) % BLOCK
            inside = (ki > qi) & (ki <= qi + BLOCK)
            pen_ref[0] = jnp.where(inside, 0.0, NEG)
            pen_ref[1] = jnp.where(inside & (ki >= BLOCK), 0.0, NEG)

        @pl.when(n < nb)
        def _():
            keep = (n > 0).astype(F32)
            low = _low_half((BLOCK, 128))
            zg = zc_ref[:, GM0:POOL0].astype(F32)
            vg_v = vg_ref[...]
            _, (th, u, r, vhat, vn, mixed) = _gating_fwd(zg, vg_v, wc_ref, bf_ref[...])
            da = dc_ref[:, 0:256].astype(F32)
            dmixed = da * u
            dbf_ref[...] += dmixed
            dvn = []
            for blk in range(2):
                dmb = dmixed[:, 128 * blk:128 * (blk + 1)]
                vb = vn[:, 128 * blk:128 * (blk + 1)]
                acc = None
                for half in range(2):
                    g = 2 * blk + half
                    dmg = jnp.where(low if half == 0 else ~low, dmb, 0.0)
                    dwc_ref[g] += _tril(_dot(dmg, vb, NT))
                    part = _dot(_triu(wct_ref[g]), dmg, NN)
                    acc = part if acc is None else acc + part
                dvn.append(acc)
            dvn = jnp.concatenate(dvn, axis=1)
            dvg_ref[...] += jnp.sum(dvn * vhat, axis=0, keepdims=True)
            dvhat = dvn * vg_v
            dv = r * (dvhat - vhat * (_group_sum(dvhat * vhat) * (1.0 / HEAD_DIM)))
            dge = jnp.concatenate([da * mixed, dv], axis=1)
            own_g = dge * _gelu_grad(zg, th)
            p = zc_ref[:, POOL0:Q0].astype(F32)
            bd_v, scale_v = bd_ref[...], scale_ref[...]
            _, (inv, pooled, mapped) = _pool_fwd(p, zpp_ref[...].astype(F32) * keep, n, bd_v, scale_v)
            db = dc_ref[:, 256:512].astype(F32)
            dscale_ref[...] += jnp.sum(db * mapped, axis=0, keepdims=True)
            dmapped = db * scale_v
            dbd_ref[...] += _dot(pooled, dmapped, TN)
            dpooled = _dot(dmapped, bd_v, NT)
            dps = dpooled * inv
            back = []
            for blk in range(2):
                dpb = dps[:, 128 * blk:128 * (blk + 1)]
                t0 = _dot(_band(POOL_WINDOWS[2 * blk], True), dpb, NN)
                t1 = _dot(_band(POOL_WINDOWS[2 * blk + 1], True), dpb, NN)
                back.append(jnp.where(jnp.concatenate([low, low], axis=0), t0, t1))
            back = jnp.concatenate(back, axis=1)
            halo_p = back[:BLOCK] * keep
            own_p = back[BLOCK:] - dpooled
            cos_c, sin_c, cos_p, sin_p = cc_ref[...], sc_ref[...], cp_ref[...], sp_ref[...]
            qrot = _rope(zc_ref[:, Q0:KV0].astype(F32), cos_c, sin_c)
            kk = jnp.concatenate([_rope(zpk_ref[:, 0:128].astype(F32), cos_p, sin_p), _rope(zc_ref[:, KV0:KV0 + 128].astype(F32), cos_c, sin_c)], axis=0)
            vv = jnp.concatenate([zpk_ref[:, 128:256].astype(F32), zc_ref[:, KV0 + 128:Z_END].astype(F32)], axis=0)
            kt = jnp.transpose(kk).astype(BF16)
            kk, vv = kk.astype(BF16), vv.astype(BF16)
            pen = pen_ref[jnp.where(n > 0, 0, 1)]
            q_slabs = [qrot[:, 128 * s:128 * (s + 1)] for s in range(ATT_Q_HEADS // 2)]
            d_slabs = [dc_ref[:, 512 + 128 * s:512 + 128 * (s + 1)].astype(F32) for s in range(ATT_Q_HEADS // 2)]
            q_rolled = [pltpu.roll(v, HEAD_DIM, 1) for v in q_slabs]
            d_rolled = [pltpu.roll(v, HEAD_DIM, 1) for v in d_slabs]
            dkk = jnp.zeros((2 * BLOCK, 128), F32)
            dvv = jnp.zeros((2 * BLOCK, 128), F32)
            dq_heads = [None] * ATT_Q_HEADS
            for h in range(2):
                lanes = ~low if h else low
                heads = range(4 * h, 4 * h + 4)
                qm = jnp.concatenate([jnp.where(lanes, (q_slabs if j % 2 == h else q_rolled)[j // 2], 0.0) for j in heads], axis=0).astype(BF16)
                dom = jnp.concatenate([jnp.where(lanes, (d_slabs if j % 2 == h else d_rolled)[j // 2], 0.0) for j in heads], axis=0).astype(BF16)
                sink = jnp.concatenate([jnp.full((1, BLOCK), sinks_ref[j], F32) for j in heads], axis=1)
                sc = _dot(kk, qm, NT) * (HEAD_DIM ** -0.5) + pen
                mx = jnp.maximum(jnp.max(sc, axis=0, keepdims=True), sink)
                e = jnp.exp(sc - mx)
                es = jnp.exp(sink - mx)
                inv_den = 1.0 / (jnp.sum(e, axis=0, keepdims=True) + es)
                pt = e * inv_den
                dpt = _dot(vv, dom, NT)
                dvv = dvv + _dot(pt, dom, NN)
                dcol = jnp.sum(pt * dpt, axis=0, keepdims=True)
                dst = (pt * (dpt - dcol) * (HEAD_DIM ** -0.5)).astype(BF16)
                dsink = es * inv_den * dcol
                dkk = dkk + _dot(dst, qm, NN)
                dqt = _dot(kt, dst, NN)
                for i, j in enumerate(heads):
                    part = jnp.sum(dsink[:, BLOCK * i:BLOCK * (i + 1)], axis=1, keepdims=True)
                    dsink_ref[pl.ds(j, 1), :] = dsink_ref[pl.ds(j, 1), :] - jnp.broadcast_to(part, (1, 128))
                    dqh = jnp.transpose(dqt[:, BLOCK * i:BLOCK * (i + 1)])
                    dq_heads[j] = dqh if j % 2 == h else pltpu.roll(dqh, HEAD_DIM, 1)
            dqrot = jnp.concatenate([jnp.where(low, dq_heads[2 * s], dq_heads[2 * s + 1]) for s in range(ATT_Q_HEADS // 2)], axis=1)
            own_q = _rope_bwd(dqrot, cos_c, sin_c)
            own_k = _rope_bwd(dkk[BLOCK:], cos_c, sin_c)
            halo_k = _rope_bwd(dkk[:BLOCK], cos_p, sin_p) * keep
            own_v, halo_v = dvv[BLOCK:], dvv[:BLOCK] * keep

            @pl.when(n > 0)
            def _():
                dz_ref[:, GM0:POOL0] = carry_ref[:, GM0:POOL0].astype(BF16)
                dz_ref[:, POOL0:Q0] = (carry_ref[:, POOL0:Q0] + halo_p).astype(BF16)
                dz_ref[:, Q0:KV0] = carry_ref[:, Q0:KV0].astype(BF16)
                dz_ref[:, KV0:KV0 + 128] = (carry_ref[:, KV0:KV0 + 128] + halo_k).astype(BF16)
                dz_ref[:, KV0 + 128:Z_END] = (carry_ref[:, KV0 + 128:Z_END] + halo_v).astype(BF16)

            carry_ref[:, GM0:POOL0] = own_g
            carry_ref[:, POOL0:Q0] = own_p
            carry_ref[:, Q0:KV0] = own_q
            carry_ref[:, KV0:KV0 + 128] = own_k
            carry_ref[:, KV0 + 128:Z_END] = own_v

        @pl.when(n == nb)
        def _():
            dz_ref[...] = carry_ref[...].astype(BF16)

    cur = lambda n: (jnp.minimum(n, nb - 1), 0)
    prev = lambda n: (jnp.maximum(jnp.minimum(n, nb - 1) - 1, 0), 0)
    in_specs = [
        pl.BlockSpec((BLOCK, Z_END), cur),
        pl.BlockSpec((BLOCK, 256), lambda n: (jnp.maximum(jnp.minimum(n, nb - 1) - 1, 0), POOL0 // 256)),
        pl.BlockSpec((BLOCK, 256), lambda n: (jnp.maximum(jnp.minimum(n, nb - 1) - 1, 0), KV0 // 256)),
        pl.BlockSpec((BLOCK, 1024), cur),
        pl.BlockSpec((BLOCK, 128), cur), pl.BlockSpec((BLOCK, 128), prev),
        pl.BlockSpec((BLOCK, 128), cur), pl.BlockSpec((BLOCK, 128), prev),
        _full((1, 256)), _full((4, BLOCK, BLOCK)), _full((BLOCK, 256)), _full((256, 256)), _full((1, 256)),
        pl.BlockSpec(memory_space=pltpu.SMEM), _full((4, BLOCK, BLOCK)),
    ]
    out_specs = [pl.BlockSpec((BLOCK, Z_END), lambda n: (jnp.maximum(n - 1, 0), 0)),
                 _full((1, 256)), _full((4, BLOCK, BLOCK)), _full((BLOCK, 256)), _full((256, 256)), _full((1, 256)), _full((8, 128))]
    out_shape = [jax.ShapeDtypeStruct((S, Z_END), BF16), jax.ShapeDtypeStruct((1, 256), F32),
                 jax.ShapeDtypeStruct((4, BLOCK, BLOCK), F32), jax.ShapeDtypeStruct((BLOCK, 256), F32),
                 jax.ShapeDtypeStruct((256, 256), F32), jax.ShapeDtypeStruct((1, 256), F32), jax.ShapeDtypeStruct((8, 128), F32)]
    return pl.pallas_call(
        body, name=name, grid=(nb + 1,), in_specs=in_specs, out_specs=out_specs, out_shape=out_shape,
        scratch_shapes=[pltpu.VMEM((BLOCK, Z_END), F32), pltpu.VMEM((2, 2 * BLOCK, ATT_Q_HEADS // 2 * BLOCK), F32)],
        compiler_params=_params("arbitrary"))(z, z, z, dcat, cos_t, cos_t, sin_t, sin_t, vg, wc, bfull, bd, scale, sinks, jnp.swapaxes(wc, 1, 2))


HBM = pl.BlockSpec(memory_space=pl.ANY)


def _place():
    return lax.axis_index("x"), lax.axis_index("y"), lax.axis_index("c")


def _chip(k, x, y):
    return (1 - x if k & 1 else x, 1 - y if k & 2 else y)


def _block_rows(ref, rows, dev):
    start = pl.multiple_of((4 * dev[0] + 2 * dev[1] + dev[2]) * rows, 8)
    return ref.at[pl.ds(start, rows), :]


_HBM_SPEC = pl.BlockSpec(memory_space=pltpu.HBM)
_SEM_SPEC = pl.BlockSpec(memory_space=pltpu.SEMAPHORE)
_SPLIT_PARAMS = dict(compiler_params=pltpu.CompilerParams(has_side_effects=pltpu.SideEffectType.DATAFLOW_SIDE_EFFECTING))


def _descriptors(copies, refs, send_sems, recv_sems):
    return [pltpu.make_async_remote_copy(src_ref=s, dst_ref=d, send_sem=send_sems.at[i], recv_sem=recv_sems.at[i], device_id=to, device_id_type=MESH)
            for i, (s, d, to) in enumerate(copies(refs))]


def _start(copies, arrays, fresh, n_copies, *, name):
    operands = [pltpu.with_memory_space_constraint(v, pltpu.HBM) for v in (*arrays, *[lax.empty(f.shape, f.dtype) for f in fresh])]
    n = len(operands)

    def body(*refs):
        for dma in _descriptors(copies, refs[:n], refs[n], refs[n + 1]):
            dma.start()
        refs[-1][...] = jnp.zeros_like(refs[-1])

    res = pl.pallas_call(
        body, name=name, in_specs=[_HBM_SPEC] * n,
        out_shape=(pltpu.SemaphoreType.DMA((n_copies,)), pltpu.SemaphoreType.DMA((n_copies,)),
                   *[pltpu.HBM(v.shape, v.dtype) for v in operands], jax.ShapeDtypeStruct((8, 128), F32)),
        out_specs=(_SEM_SPEC, _SEM_SPEC, *[_HBM_SPEC] * n, pl.BlockSpec(memory_space=pltpu.VMEM)),
        input_output_aliases={i: 2 + i for i in range(n)}, **_SPLIT_PARAMS)(*operands)
    return res[0], res[1], list(res[2:2 + n]), res[-1]


def _finish(copies, send_sems, recv_sems, arrays, after, *, name):
    n = len(arrays)

    def body(*refs):
        for dma in _descriptors(copies, refs[:n], refs[n], refs[n + 1]):
            dma.wait_send()
            dma.wait_recv()

    return list(pl.pallas_call(
        body, name=name, in_specs=[_HBM_SPEC] * n + [_SEM_SPEC, _SEM_SPEC] + [HBM] * len(after),
        out_shape=tuple(pltpu.HBM(v.shape, v.dtype) for v in arrays), out_specs=tuple([_HBM_SPEC] * n),
        input_output_aliases={i: i for i in range(n)}, **_SPLIT_PARAMS)(*arrays, send_sems, recv_sems, *after))


def _tie(value, tokens):
    if not tokens:
        return value

    def body(*refs):
        pass

    return pl.pallas_call(body, name="tie", in_specs=[HBM] * (1 + len(tokens)), out_specs=HBM,
                          out_shape=jax.ShapeDtypeStruct(value.shape, value.dtype), input_output_aliases={0: 0})(value, *tokens)


def _gather_copies(stage):
    def copies(refs):
        x, y, c = _place()
        me, sibling = (x, y, c), (x, y, 1 - c)
        out = []
        for ref in refs:
            r = ref.shape[0] // N_DEV
            if stage == "a":
                hops = [(me, sibling)] + [(me, (*_chip(k, x, y), c)) for k in (1, 2, 3)]
            else:
                hops = [((*_chip(k, x, y), c), sibling) for k in (1, 2, 3)]
            for block, to in hops:
                rows = _block_rows(ref, r, block)
                out.append((rows, rows, to))
        return out
    return copies


def _scatter_copies(stage, n):
    def copies(refs):
        x, y, c = _place()
        out = []
        for w in range(n):
            src, dst = refs[w], refs[n + w]
            if stage == "a":
                r = src.shape[0] // N_DEV
                out += [(_block_rows(src, r, (*_chip(k, x, y), 1 - c)), dst.at[k], (x, y, 1 - c)) for k in range(4)]
            else:
                out += [(src.at[k - 1], dst.at[k - 1], (*_chip(k, x, y), c)) for k in (1, 2, 3)]
        return out
    return copies


def _place_own(shards, layer, blocks, *, name, dtype=BF16):
    _, r, cdim = shards.shape

    def body(blocks_ref, s_ref, o_ref):
        o_ref[...] = s_ref[...].astype(dtype)

    return pl.pallas_call(
        body, name=name,
        grid_spec=pltpu.PrefetchScalarGridSpec(
            num_scalar_prefetch=1, grid=(1,), in_specs=[pl.BlockSpec((None, r, cdim), lambda i, blocks: (layer, 0, 0))],
            out_specs=pl.BlockSpec((r, cdim), lambda i, blocks: (blocks[0], 0))),
        out_shape=jax.ShapeDtypeStruct((N_DEV * r, cdim), dtype), compiler_params=_params("arbitrary"))(blocks, shards)


def _chip_sum(grad, got, blocks, *, name):
    _, r, cdim = got.shape
    tr = r if r <= 256 else r // (2 if r % 32 == 0 and r // 2 <= 256 else 4)
    steps = r // tr

    def body(blocks_ref, a0_ref, a1_ref, a2_ref, a3_ref, b_ref, mine_ref, away_ref):
        mine_ref[...] = a0_ref[...] + b_ref[0]
        for k, a_ref in ((1, a1_ref), (2, a2_ref), (3, a3_ref)):
            away_ref[k - 1] = (a_ref[...] + b_ref[k]).astype(BF16)

    own = [pl.BlockSpec((tr, cdim), lambda i, blocks, k=k: (blocks[k] * steps + i, 0)) for k in range(4)]
    return pl.pallas_call(
        body, name=name,
        grid_spec=pltpu.PrefetchScalarGridSpec(
            num_scalar_prefetch=1, grid=(steps,),
            in_specs=own + [pl.BlockSpec((4, tr, cdim), lambda i, blocks: (0, i, 0))],
            out_specs=[pl.BlockSpec((tr, cdim), lambda i, blocks: (i, 0)), pl.BlockSpec((3, tr, cdim), lambda i, blocks: (0, i, 0))]),
        out_shape=[jax.ShapeDtypeStruct((r, cdim), F32), jax.ShapeDtypeStruct((3, r, cdim), BF16)],
        compiler_params=_params("arbitrary"))(blocks, grad, grad, grad, grad, got)


def _final_sum(mine, got, *, name):
    r, cdim = mine.shape
    tr = r if r <= 256 else r // (2 if r % 32 == 0 and r // 2 <= 256 else 4)

    def body(m_ref, g_ref, o_ref):
        o_ref[...] = ((m_ref[...] + g_ref[0].astype(F32)) + g_ref[1].astype(F32)) + g_ref[2].astype(F32)

    return pl.pallas_call(
        body, name=name, grid=(r // tr,),
        in_specs=[pl.BlockSpec((tr, cdim), lambda i: (i, 0)), pl.BlockSpec((3, tr, cdim), lambda i: (0, i, 0))],
        out_specs=pl.BlockSpec((tr, cdim), lambda i: (i, 0)), out_shape=jax.ShapeDtypeStruct((r, cdim), F32),
        compiler_params=_params("parallel"))(mine, got)


def _adamw_math(w, g, m, v):
    m = ADAM_B1 * m + (1.0 - ADAM_B1) * g
    v = ADAM_B2 * v + (1.0 - ADAM_B2) * (g * g)
    m_hat = m / (1.0 - ADAM_B1 ** ADAM_STEP)
    v_hat = v / (1.0 - ADAM_B2 ** ADAM_STEP)
    return -ADAM_LR * (m_hat / (jnp.sqrt(v_hat) + ADAM_EPS) + ADAM_WD * w), m, v


def _adamw(w, g, m, v, *, name):
    L, r, cdim = w.shape
    tr = r if r <= 512 else r // 2
    assert r % tr == 0 and tr % 8 == 0

    def body(w_ref, g_ref, m_ref, v_ref, d_ref, nm_ref, nv_ref):
        d_ref[...], nm_ref[...], nv_ref[...] = _adamw_math(w_ref[...], g_ref[...], m_ref[...], v_ref[...])

    blk = pl.BlockSpec((None, tr, cdim), lambda l, i: (l, i, 0))
    return pl.pallas_call(body, name=name, grid=(L, r // tr), in_specs=[blk] * 4, out_specs=[blk] * 3,
                          out_shape=[jax.ShapeDtypeStruct(w.shape, F32)] * 3, compiler_params=_params("parallel", "parallel"))(w, g, m, v)


def _small_update(parts, w, m, v, *, name):
    _, p, lanes = parts.shape
    tp = p // 2 if p % 16 == 0 else p

    def body(p_ref, w_ref, m_ref, v_ref, g_ref, d_ref, nm_ref, nv_ref):
        g = p_ref[0]
        for i in range(1, N_DEV):
            g = g + p_ref[i]
        g_ref[...] = g
        d_ref[...], nm_ref[...], nv_ref[...] = _adamw_math(w_ref[...], g, m_ref[...], v_ref[...])

    blk = pl.BlockSpec((tp, lanes), lambda i: (i, 0))
    return pl.pallas_call(body, name=name, grid=(p // tp,), in_specs=[pl.BlockSpec((N_DEV, tp, lanes), lambda i: (0, i, 0)), blk, blk, blk],
                          out_specs=[blk] * 4, out_shape=[jax.ShapeDtypeStruct((p, lanes), F32)] * 4,
                          compiler_params=_params("parallel"))(parts, w, m, v)


BIG = ("w_in", "w_o", "w_xq", "w_xkv", "w_xo", "w_gate_up", "w_down")
TRANSPOSED = ("w_in", "w_xkv", "w_gate_up")
SMALL = ("mem_norm_g", "mix_pre_g", "mix_post_g", "gm_v_g", "gm_w_s", "gm_b_s", "pool_w", "pool_scale", "attn_sinks",
         "x_pre_g", "x_post_g", "ffn_pre_g", "ffn_post_g")
WEIGHTS = ("mem_norm_g", "mix_pre_g", "mix_post_g", "w_in", "gm_v_g", "gm_w_s", "gm_b_s", "pool_w", "pool_scale", "attn_sinks", "w_o",
           "x_pre_g", "x_post_g", "w_xq", "w_xkv", "w_xo", "ffn_pre_g", "ffn_post_g", "w_gate_up", "w_down")
PACK_QUANTUM = 8 * 128


def _pack(arrays):
    flat = []
    for a in arrays:
        a = a.reshape(-1).astype(F32)
        flat.append(jnp.pad(a, (0, -a.size % PACK_QUANTUM)))
    return jnp.concatenate(flat).reshape(-1, 128)


def _unpack(pack, shapes):
    out, row = [], 0
    for s in shapes:
        size = math.prod(s)
        rows = (size + (-size % PACK_QUANTUM)) // 128
        out.append(pack[row:row + rows].reshape(-1)[:size].reshape(s))
        row += rows
    return out


def _row(v):
    return v.reshape(1, -1)


def _local_step(x, mem, positions, target, small, weight, fwd_hook=None, bwd_hook=None):
    fwd_hook = fwd_hook or (lambda l, point, v: v)
    bwd_hook = bwd_hook or (lambda l, point, v, gb: v)
    depth = small["mix_pre_g"].shape[0]
    half = HEAD_DIM // 2
    inv = ROPE_THETA ** (-jnp.arange(half, dtype=F32) / half)
    cos_t, sin_t = _rope_tables(positions.reshape(-1, 1), jnp.tile(inv, 128 // half).reshape(1, 128), name="rope_tables")
    mem_g = _row(small["mem_norm_g"])
    saved = []
    for l in range(depth):
        w = functools.partial(weight, l)
        bfull = jnp.repeat(small["gm_b_s"][l].T, HEAD_DIM, axis=1)
        bd = jax.scipy.linalg.block_diag(*[small["pool_w"][l, g] for g in range(len(POOL_WINDOWS))])
        mix_par = (_row(small["gm_v_g"][l]), small["gm_w_s"][l], bfull, bd, _row(small["pool_scale"][l]), small["attn_sinks"][l])
        if l == 0:
            z, h1 = _mm_fwd(fwd_hook(l, 0, x), w("w_in"), nt=True, pre="norm", g_pre=_row(small["mix_pre_g"][l]), name="first_in_proj")
        else:
            h1 = fwd_hook(l, 0, h1)
            z, = _mm_fwd(h1, w("w_in"), nt=True, name="in_proj")
        cat = fwd_hook(l, 1, _mixer_fwd(z, cos_t, sin_t, *mix_par, name="mixer_fwd"))
        mix, x1, h2 = _mm_fwd(cat, w("w_o"), nt=False, post="norm_res", g_post=_row(small["mix_post_g"][l]), xres=x,
                              g_next=_row(small["x_pre_g"][l]), name="mix_out")
        qx, = _mm_fwd(h2, w("w_xq"), nt=False, name="xq_proj")
        kv, memn = _mm_fwd(mem, w("w_xkv"), nt=True, pre="norm", g_pre=mem_g, name="xkv_proj")
        ox = _xattn_fwd(qx, kv, name="xattn_fwd")
        xo, x2, h3 = _mm_fwd(ox, w("w_xo"), nt=False, post="norm_res", g_post=_row(small["x_post_g"][l]), xres=x1,
                             g_next=_row(small["ffn_pre_g"][l]), name="xattn_out")
        h3 = fwd_hook(l, 2, h3)
        gu, = _mm_fwd(h3, w("w_gate_up"), nt=True, name="ffn_in")
        last = l + 1 == depth
        f, x3, act, *h_next = _mm_fwd(gu, w("w_down"), nt=False, pre="swiglu", post="norm_res", g_post=_row(small["ffn_post_g"][l]), xres=x2,
                                      g_next=None if last else _row(small["mix_pre_g"][l + 1]), tile=256, name="last_ffn_out" if last else "ffn_out")
        saved.append((mix_par, x, z, h1, cat, mix, x1, qx, h2, kv, memn, ox, xo, x2, gu, h3, act, f))
        if last:
            x = fwd_hook(l, 3, x3)
        else:
            x, h1 = x3, fwd_hook(l, 3, h_next[0])

    gs = {n: [None] * depth for n in SMALL if n != "mem_norm_g"}
    gb = {n: [None] * depth for n in BIG}
    dmemn = [None] * depth
    loss, dx, df, gs["ffn_post_g"][depth - 1] = _loss_grad(x, target, saved[-1][-1], _row(small["ffn_post_g"][depth - 1]), name="loss")
    for l in reversed(range(depth)):
        w = functools.partial(weight, l)
        mix_par, x0, z, h1, cat, mix, x1, qx, h2, kv, memn, ox, xo, x2, gu, h3, act, f = saved[l]
        dgu = _ffn_out_bwd(df, w("w_down"), gu, name="ffn_out_bwd")
        gb["w_down"][l] = _wgrad(act, df, name="w_down_grad")
        gb["w_gate_up"][l] = _wgrad(dgu, h3, name="w_gate_up_grad")
        dx2, gs["ffn_pre_g"][l], dxo, gs["x_post_g"][l] = _mm_bwd_post(
            dgu, w("w_gate_up"), x2, _row(small["ffn_pre_g"][l]), dx, nt=False, below=(xo, _row(small["x_post_g"][l])), tile=256, name="ffn_in_bwd")
        dxo = bwd_hook(l, 1, dxo, gb)
        do, = _mm_fwd(dxo, w("w_xo"), nt=True, name="xattn_out_bwd")
        gb["w_xo"][l] = _wgrad(ox, dxo, name="w_xo_grad")
        dqx, dkv = _xattn_bwd(qx, kv, do, name="xattn_bwd")
        gb["w_xq"][l] = _wgrad(h2, dqx, name="w_xq_grad")
        dx1, gs["x_pre_g"][l], dmix, gs["mix_post_g"][l] = _mm_bwd_post(
            dqx, w("w_xq"), x1, _row(small["x_pre_g"][l]), dx2, nt=True, below=(mix, _row(small["mix_post_g"][l])), name="xq_proj_bwd")
        dmix = bwd_hook(l, 2, dmix, gb)
        gb["w_xkv"][l] = _wgrad(dkv, memn, name="w_xkv_grad")
        dmemn[l] = _mm_fwd(dkv, w("w_xkv"), nt=False, out_dtype=F32, name="xkv_proj_bwd")[0]
        dcat, = _mm_fwd(dmix, w("w_o"), nt=True, name="mix_out_bwd")
        gb["w_o"][l] = _wgrad(cat, dmix, name="w_o_grad")
        dz, dvg, dwc, dbf, dbd, dscale, dsink = _mixer_bwd(z, dcat, cos_t, sin_t, *mix_par, name="mixer_bwd")
        gs["gm_v_g"][l], gs["gm_w_s"][l], gs["pool_scale"][l], gs["attn_sinks"][l] = dvg, dwc, dscale, dsink[:, 0]
        gs["gm_b_s"][l] = dbf.reshape(BLOCK, -1, HEAD_DIM).sum(-1).T
        gs["pool_w"][l] = jnp.stack([dbd[HEAD_DIM * g:HEAD_DIM * (g + 1), HEAD_DIM * g:HEAD_DIM * (g + 1)] for g in range(len(POOL_WINDOWS))])
        gb["w_in"][l] = _wgrad(dz, h1, name="w_in_grad")
        if l == 0:
            dx, gs["mix_pre_g"][l] = _mm_bwd_post(dz, w("w_in"), x0, _row(small["mix_pre_g"][l]), dx1, nt=False, name="first_in_proj_bwd")
            dx = bwd_hook(l, 3, dx, gb)
        else:
            dx, gs["mix_pre_g"][l], df, gs["ffn_post_g"][l - 1] = _mm_bwd_post(
                dz, w("w_in"), x0, _row(small["mix_pre_g"][l]), dx1, nt=False, below=(saved[l - 1][-1], _row(small["ffn_post_g"][l - 1])),
                name="in_proj_bwd")
            df = bwd_hook(l, 3, df, gb)

    small_grads = {n: jnp.stack(v).reshape(small[n].shape) for n, v in gs.items()}
    small_grads["mem_norm_g"] = _norm_dgain(mem, jnp.stack(dmemn), name="mem_norm_grad").reshape(-1)
    return loss, dx, small_grads, gb


EARLY = ("w_in", "w_o", "w_xq", "w_xkv", "w_xo")
LATE = ("w_gate_up", "w_down")


def _step(a):
    depth = a["mix_pre_g"].shape[0]
    x, mem, target = a["x"][0], a["mem"][0], a["loss_target"][0]
    px, py, pc = _place()
    blocks = jnp.stack([4 * cx + 2 * cy + pc for cx, cy in (_chip(k, px, py) for k in range(4))]).astype(jnp.int32)
    flight = {}

    def laid_out(v, n):
        return v.transpose(0, 2, 1) if n in TRANSPOSED else v

    whole = {(l, n): _place_own(laid_out(a[n], n), l, blocks, name="place_" + n) for l in range(depth) for n in BIG}
    gather = {"first_in": [(0, "w_in")], "first_mid": [(0, n) for n in EARLY[1:]], "first_late": [(0, n) for n in LATE],
              "second_early": [(1, n) for n in EARLY], "second_late": [(1, n) for n in LATE]}
    gather.update({f"layer{l}": [(l, n) for n in BIG] for l in range(2, depth)})
    forward_plan = {(0, 0): [("a", "first_in"), ("a", "first_mid"), ("a", "first_late"), ("a", "second_early"), ("a", "second_late"),
                             ("mid", "first_in"), ("end", "first_in")],
                    (0, 1): [("mid", "first_mid"), ("end", "first_mid")], (0, 2): [("mid", "first_late"), ("end", "first_late")],
                    (0, 3): [("mid", "second_early")], (1, 0): [("end", "second_early"), ("a", "layer2")],
                    (1, 1): [("mid", "second_late")], (1, 2): [("end", "second_late"), ("mid", "layer2")], (1, 3): [("end", "layer2")]}
    for l in range(2, depth):
        forward_plan.update({(l, 0): [("a", f"layer{l + 1}")], (l, 2): [("mid", f"layer{l + 1}")], (l, 3): [("end", f"layer{l + 1}")]})
    if depth == 1:
        gather = {tag: keys for tag, keys in gather.items() if tag.startswith("first")}

    issued = []

    def gather_step(kind, tag, v):
        keys = gather[tag]
        if kind == "a":
            arrays = [whole[k] for k in keys]
            arrays[0] = _tie(arrays[0], issued[-1:])
            send, recv, arrays, token = _start(_gather_copies("a"), arrays, [], 4 * len(keys), name=f"gather_a_{tag}")
            flight[tag] = (send, recv, arrays)
            issued.append(token)
            return [token]
        send, recv, arrays = flight.pop(tag)
        if kind == "mid":
            arrays = _finish(_gather_copies("a"), send, recv, arrays, [v], name=f"gather_a_done_{tag}")
            send, recv, arrays, token = _start(_gather_copies("b"), arrays, [], 3 * len(keys), name=f"gather_b_{tag}")
            flight[tag] = (send, recv, arrays)
            return [token]
        for key, arr in zip(keys, _finish(_gather_copies("b"), send, recv, arrays, [v], name=f"gather_b_done_{tag}")):
            whole[key] = arr
        return []

    def fwd_hook(l, point, v):
        tokens = []
        for kind, tag in forward_plan.get((l, point), []):
            if tag in gather:
                tokens += gather_step(kind, tag, v)
        return _tie(v, tokens)

    mine = {}
    scatter = {f"layer{l}": [(l, n) for n in BIG] for l in range(1, depth)}
    scatter.update({"late0": [(0, n) for n in LATE], "attn0": [(0, "w_xq"), (0, "w_xo")], "rest0": [(0, "w_in"), (0, "w_o"), (0, "w_xkv")]})
    backward_plan = {}
    for l in range(1, depth):
        backward_plan[(l, 3)] = [("a", f"layer{l}")]
        backward_plan[(l - 1, 1)] = [("mid", f"layer{l}")]
        backward_plan[(l - 1, 2)] = [("end", f"layer{l}")]
    backward_plan[(0, 1)] = [("a", "late0")] + backward_plan.get((0, 1), [])
    backward_plan[(0, 2)] = [("a", "attn0"), ("mid", "late0")] + backward_plan.get((0, 2), [])
    backward_plan[(0, 3)] = [("end", "late0"), ("mid", "attn0"), ("a", "rest0")]

    def scatter_step(kind, tag, v, gb):
        keys = scatter[tag]
        n = len(keys)
        if kind == "a":
            grads = [gb[name][l] for l, name in keys]
            zones = [jax.ShapeDtypeStruct((4, g.shape[0] // N_DEV, g.shape[1]), F32) for g in grads]
            send, recv, arrays, token = _start(_scatter_copies("a", n), grads, zones, 4 * n, name=f"scatter_a_{tag}")
            flight[tag] = (send, recv, arrays, None)
            return [token]
        send, recv, arrays, kept = flight.pop(tag)
        if kind == "mid":
            arrays = _finish(_scatter_copies("a", n), send, recv, arrays, [v], name=f"scatter_a_done_{tag}")
            sums = [_chip_sum(arrays[i], arrays[n + i], blocks, name="chip_sum") for i in range(n)]
            zones = [jax.ShapeDtypeStruct(s[1].shape, BF16) for s in sums]
            send, recv, arrays, token = _start(_scatter_copies("b", n), [s[1] for s in sums], zones, 3 * n, name=f"scatter_b_{tag}")
            flight[tag] = (send, recv, arrays, [s[0] for s in sums])
            return [token]
        arrays = _finish(_scatter_copies("b", n), send, recv, arrays, [v], name=f"scatter_b_done_{tag}")
        for i, key in enumerate(keys):
            mine[key] = _final_sum(kept[i], arrays[n + i], name="final_sum")
        return []

    def bwd_hook(l, point, v, gb):
        tokens = []
        for kind, tag in backward_plan.get((l, point), []):
            tokens += scatter_step(kind, tag, v, gb)
        return _tie(v, tokens)

    loss, dx, small_grads, _ = _local_step(x, mem, a["positions"][0], target, {n: a[n] for n in SMALL}, lambda l, n: whole[(l, n)],
                                           fwd_hook, bwd_hook)

    out = {}

    def update(n, tokens):
        g = _tie(jnp.stack([mine[(l, n)] for l in range(depth)]), tokens)
        moved = _adamw(laid_out(a[n], n), g, laid_out(a["m_" + n], n), laid_out(a["v_" + n], n), name="adamw_" + n)
        for p, v in zip(("grad_", "delta_", "new_m_", "new_v_"), (g, *moved)):
            out[p + n] = laid_out(v, n)
        return moved[0]

    part = _tie(_pack([small_grads[n] for n in SMALL]), [dx])
    placed = _place_own(part[None], 0, blocks, dtype=F32, name="place_small_grads")
    send, recv, arrays, token = _start(_gather_copies("a"), [placed], [], 4, name="gather_a_small_grads")
    behind = [update(n, [token]) for n in LATE]
    tokens = scatter_step("mid", "rest0", behind[-1], None)
    arrays = _finish(_gather_copies("a"), send, recv, arrays, behind, name="gather_a_done_small_grads")
    send, recv, arrays, token = _start(_gather_copies("b"), arrays, [], 3, name="gather_b_small_grads")
    scatter_step("end", "attn0", behind[-1], None)
    behind = [update(n, tokens + [token]) for n in ("w_xq", "w_xo")]
    parts = _finish(_gather_copies("b"), send, recv, arrays, behind, name="gather_b_done_small_grads")[0].reshape(N_DEV, *part.shape)
    packs = _small_update(parts, *[_pack([a[p + n] for n in SMALL]) for p in ("", "m_", "v_")], name="small_update")
    for p, pack in zip(("grad_", "delta_", "new_m_", "new_v_"), packs):
        for n, v in zip(SMALL, _unpack(pack, [a[n].shape for n in SMALL])):
            out[p + n] = v
    scatter_step("end", "rest0", packs[0], None)
    for n in ("w_in", "w_o", "w_xkv"):
        update(n, [])

    total = lax.psum(loss[0, 0], ("x", "y", "c"))
    return (total, dx[None], *[out[p + n] for p in ("grad_", "delta_", "new_m_", "new_v_") for n in WEIGHTS])


def kernel(x, mem, positions, mem_norm_g, mix_pre_g, mix_post_g, w_in, gm_v_g, gm_w_s, gm_b_s, pool_w, pool_scale, attn_sinks,
           w_o, x_pre_g, x_post_g, w_xq, w_xkv, w_xo, ffn_pre_g, ffn_post_g, w_gate_up, w_down, loss_target, m_mem_norm_g,
           m_mix_pre_g, m_mix_post_g, m_w_in, m_gm_v_g, m_gm_w_s, m_gm_b_s, m_pool_w, m_pool_scale, m_attn_sinks, m_w_o,
           m_x_pre_g, m_x_post_g, m_w_xq, m_w_xkv, m_w_xo, m_ffn_pre_g, m_ffn_post_g, m_w_gate_up, m_w_down, v_mem_norm_g,
           v_mix_pre_g, v_mix_post_g, v_w_in, v_gm_v_g, v_gm_w_s, v_gm_b_s, v_pool_w, v_pool_scale, v_attn_sinks, v_w_o,
           v_x_pre_g, v_x_post_g, v_w_xq, v_w_xkv, v_w_xo, v_ffn_pre_g, v_ffn_post_g, v_w_gate_up, v_w_down):
    return _step(dict(locals()))
```
